```python
import math, functools
import jax, jax.numpy as jnp
from jax import lax
import numpy as np

D_MODEL = 1024
BATCH = 4
SEQ = 8192
DEPTH = 2
DEC_BATCH = 32
DEC_SEQ = 1
PAST_LEN = 16384
PAGE_SIZE = 128

D_CONV = D_MODEL // 2
SHORT_CONV = 3
N_HEADS = 8
HEAD_DIM = 64
N_KV_HEADS = 2
N_IDX_HEADS = 4
IDX_DIM = 64
TOPK_MAX = 256
Q_BLOCK = 128
ROPE_THETA = 10000.0
SSM_INNER = D_MODEL // 2
SSM_HEAD_DIM = 64
SSM_HEADS = SSM_INNER // SSM_HEAD_DIM
SSM_GROUPS = 2
SSM_STATE = 128
SSM_CONV = 4
SSM_CONV_DIM = SSM_INNER + 2 * SSM_GROUPS * SSM_STATE
SSM_CHUNK = 128
N_BRANCH = 3
D_FF = 2816
N_EXPERTS = 8
TOP_K = 2
D_FF_EXPERT = 3584
EPS = 1e-6

SPLIT_SIZES = (D_CONV, D_CONV, D_CONV,
               N_HEADS * HEAD_DIM, N_KV_HEADS * HEAD_DIM, N_KV_HEADS * HEAD_DIM,
               N_IDX_HEADS * IDX_DIM, IDX_DIM, N_IDX_HEADS,
               SSM_INNER, SSM_CONV_DIM, SSM_HEADS,
               N_BRANCH * D_MODEL)
D_IN = sum(SPLIT_SIZES)
SPLIT_POINTS = tuple(int(s) for s in np.cumsum(SPLIT_SIZES)[:-1])

kernel_name = 'hybrid_conv_dsa_ssd_gated_decoder_step'


def rms_norm(x, g):
    xf = x.astype(jnp.float32)
    y = xf * lax.rsqrt(jnp.mean(xf * xf, axis=-1, keepdims=True) + EPS)
    return (y * g.astype(jnp.float32)).astype(x.dtype)


def rope(x, pos):
    half = x.shape[-1] // 2
    inv = ROPE_THETA ** (-jnp.arange(half, dtype=jnp.float32) / half)
    ang = pos.astype(jnp.float32)[:, None] * inv[None, :]
    cos = jnp.cos(ang)[None, :, None, :]
    sin = jnp.sin(ang)[None, :, None, :]
    xf = x.astype(jnp.float32)
    x1, x2 = xf[..., :half], xf[..., half:]
    return jnp.concatenate([x1 * cos - x2 * sin, x1 * sin + x2 * cos], axis=-1).astype(x.dtype)


def causal_depthwise(ext, w, T):
    return sum(ext[:, j:j + T] * w[j] for j in range(w.shape[0]))


def swiglu(h, w_gate, w_up, w_down):
    return (jax.nn.silu(h @ w_gate) * (h @ w_up)) @ w_down


def moe_swiglu(h, w_router, w_gate, w_up, w_down):
    logits = jnp.einsum('btd,de->bte', h, w_router).astype(jnp.float32)
    top_val, top_idx = lax.top_k(logits, TOP_K)
    probs = jax.nn.softmax(top_val, axis=-1)
    combine = jnp.einsum('btk,btke->bte', probs,
                         jax.nn.one_hot(top_idx, N_EXPERTS, dtype=jnp.float32)).astype(h.dtype)
    out = jnp.zeros_like(h)
    for e in range(N_EXPERTS):
        out = out + combine[..., e:e + 1] * swiglu(h, w_gate[e], w_up[e], w_down[e])
    return out


def segsum(a):
    T = a.shape[-1]
    x = jnp.broadcast_to(a[..., None], a.shape + (T,))
    x = jnp.where(jnp.tril(jnp.ones((T, T), bool), -1), x, 0.0)
    cs = jnp.cumsum(x, axis=-2)
    return jnp.where(jnp.tril(jnp.ones((T, T), bool), 0), cs, -jnp.inf)


def ssd_chunked(x, a, B, C, chunk, h0):
    b, L, H, P = x.shape
    N = B.shape[-1]
    nc = L // chunk
    x = x.reshape(b, nc, chunk, H, P)
    B = B.reshape(b, nc, chunk, H, N)
    C = C.reshape(b, nc, chunk, H, N)
    a = a.reshape(b, nc, chunk, H).transpose(0, 3, 1, 2)
    a_cs = jnp.cumsum(a, axis=-1)
    decay_in = jnp.exp(segsum(a))
    cb = jnp.einsum('bclhn,bcshn->bhcls', C, B)
    y_diag = jnp.einsum('bhcls,bcshp->bclhp', cb * decay_in, x)
    decay_states = jnp.exp(a_cs[..., -1:] - a_cs)
    st = jnp.einsum('bclhn,bhcl,bclhp->bchpn', B, decay_states, x)
    chunk_decay = jnp.exp(a_cs[..., -1])

    def step(h, inp):
        s_c, d_c = inp
        return h * d_c[..., None, None] + s_c, h

    h_last, h_in = lax.scan(step, h0, (st.transpose(1, 0, 2, 3, 4), chunk_decay.transpose(2, 0, 1)))
    h_in = h_in.transpose(1, 0, 2, 3, 4)
    y_off = jnp.einsum('bclhn,bchpn,bhcl->bclhp', C, h_in, jnp.exp(a_cs))
    return (y_diag + y_off).reshape(b, L, H, P), h_last


def indexer_scores(qi, wi, ki):
    s = jnp.einsum('bthd,bld->bthl', qi, ki).astype(jnp.float32) * (IDX_DIM ** -0.5)
    return jnp.einsum('bthl,bth->btl', jax.nn.relu(s), wi.astype(jnp.float32))


def sparse_attend(q, ks, vs, valid):
    b, T = q.shape[:2]
    qg = q.reshape(b, T, N_KV_HEADS, N_HEADS // N_KV_HEADS, HEAD_DIM)
    s = jnp.einsum('btkgd,btskd->btkgs', qg, ks).astype(jnp.float32) * (HEAD_DIM ** -0.5)
    s = jnp.where(valid[:, :, None, None, :], s, -jnp.inf)
    p = jax.nn.softmax(s, axis=-1).astype(vs.dtype)
    o = jnp.einsum('btkgs,btskd->btkgd', p, vs)
    return o.reshape(b, T, N_HEADS * HEAD_DIM)


def dsa_prompt(q, k, v, qi, ki, wi):
    b, S = q.shape[:2]
    k_sel = min(TOPK_MAX, S // 4)
    kpos = jnp.arange(S)

    def block(j):
        t0 = j * Q_BLOCK
        qb = lax.dynamic_slice_in_dim(q, t0, Q_BLOCK, axis=1)
        qib = lax.dynamic_slice_in_dim(qi, t0, Q_BLOCK, axis=1)
        wib = lax.dynamic_slice_in_dim(wi, t0, Q_BLOCK, axis=1)
        qpos = t0 + jnp.arange(Q_BLOCK)
        sc = indexer_scores(qib, wib, ki)
        sc = jnp.where(kpos[None, None, :] <= qpos[None, :, None], sc, -jnp.inf)
        _, idx = lax.top_k(sc, k_sel)
        ks = jax.vmap(lambda kk, ii: kk[ii])(k, idx)
        vs = jax.vmap(lambda vv, ii: vv[ii])(v, idx)
        valid = idx <= qpos[None, :, None]
        return sparse_attend(qb, ks, vs, valid)

    o = lax.map(block, jnp.arange(S // Q_BLOCK))
    return o.transpose(1, 0, 2, 3).reshape(b, S, N_HEADS * HEAD_DIM)


def dsa_decode(q, k, v, qi, ki, wi, pool_k, pool_v, pool_kidx, page_table):
    b, T = q.shape[:2]
    n_pages = page_table.shape[1]
    past = n_pages * PAGE_SIZE
    L = past + T
    k_sel = min(TOPK_MAX, L // 4)
    ki_past = pool_kidx[page_table].reshape(b, past, IDX_DIM)
    ki_all = jnp.concatenate([ki_past, ki.astype(ki_past.dtype)], axis=1)
    qpos = past + jnp.arange(T)
    kpos = jnp.arange(L)
    sc = indexer_scores(qi, wi, ki_all)
    sc = jnp.where(kpos[None, None, :] <= qpos[None, :, None], sc, -jnp.inf)
    _, idx = lax.top_k(sc, k_sel)
    page = jnp.minimum(idx // PAGE_SIZE, n_pages - 1)
    phys = jax.vmap(lambda pt, pg: pt[pg])(page_table, page)
    flat = phys * PAGE_SIZE + idx % PAGE_SIZE
    kp = pool_k.reshape(-1, N_KV_HEADS, HEAD_DIM)[flat]
    vp = pool_v.reshape(-1, N_KV_HEADS, HEAD_DIM)[flat]
    loc = jnp.clip(idx - past, 0, T - 1)
    kn = jax.vmap(lambda kk, ii: kk[ii])(k, loc)
    vn = jax.vmap(lambda vv, ii: vv[ii])(v, loc)
    is_new = (idx >= past)[..., None, None]
    ks = jnp.where(is_new, kn.astype(kp.dtype), kp)
    vs = jnp.where(is_new, vn.astype(vp.dtype), vp)
    valid = idx <= qpos[None, :, None]
    return sparse_attend(q, ks.astype(q.dtype), vs.astype(q.dtype), valid)


def hybrid_layer(x, p, pos, conv_buf, ssm_buf, ssm_h0, attend, ffn):
    b, T, _ = x.shape
    h = rms_norm(x, p['norm1'])
    (cb, cc, cx, q, k, v, qi, ki, wi, z, xbc, dt, g) = jnp.split(h @ p['w_in'], SPLIT_POINTS, axis=-1)

    u = cc * cx
    u_ext = jnp.concatenate([conv_buf.astype(u.dtype), u], axis=1)
    y_conv = (cb * causal_depthwise(u_ext, p['conv_w'], T)) @ p['w_pc']

    q = rope(rms_norm(q.reshape(b, T, N_HEADS, HEAD_DIM), p['q_norm']), pos)
    k = rope(rms_norm(k.reshape(b, T, N_KV_HEADS, HEAD_DIM), p['k_norm']), pos)
    v = v.reshape(b, T, N_KV_HEADS, HEAD_DIM)
    qi = rope(qi.reshape(b, T, N_IDX_HEADS, IDX_DIM), pos)
    ki = rope(ki[:, :, None, :], pos)[:, :, 0]
    wi = wi * (N_IDX_HEADS ** -0.5)
    y_attn = attend(q, k, v, qi, ki, wi) @ p['w_pa']

    xbc_ext = jnp.concatenate([ssm_buf.astype(xbc.dtype), xbc], axis=1)
    xbc_c = jax.nn.silu(causal_depthwise(xbc_ext, p['ssm_conv_w'], T) + p['ssm_conv_b'])
    xs, bs, cs = jnp.split(xbc_c, [SSM_INNER, SSM_INNER + SSM_GROUPS * SSM_STATE], axis=-1)
    rep = SSM_HEADS // SSM_GROUPS
    xs = xs.reshape(b, T, SSM_HEADS, SSM_HEAD_DIM).astype(jnp.float32)
    bs = jnp.repeat(bs.reshape(b, T, SSM_GROUPS, SSM_STATE), rep, axis=2).astype(jnp.float32)
    cs = jnp.repeat(cs.reshape(b, T, SSM_GROUPS, SSM_STATE), rep, axis=2).astype(jnp.float32)
    dt = jax.nn.softplus(dt.astype(jnp.float32) + p['dt_bias'].astype(jnp.float32))
    a = -jnp.exp(p['a_log'].astype(jnp.float32))
    chunk = SSM_CHUNK if T % SSM_CHUNK == 0 else T
    y, h_last = ssd_chunked(xs * dt[..., None], dt * a, bs, cs, chunk, ssm_h0.astype(jnp.float32))
    y = (y + p['d_skip'].astype(jnp.float32)[:, None] * xs).reshape(b, T, SSM_INNER).astype(x.dtype)
    y_ssd = rms_norm(y * jax.nn.silu(z), p['ssm_norm']) @ p['w_ps']

    gates = jax.nn.sigmoid(g.reshape(b, T, N_BRANCH, D_MODEL) + p['b_gate'])
    merged = gates[:, :, 0] * y_conv + gates[:, :, 1] * y_attn + gates[:, :, 2] * y_ssd
    x = x + merged @ p['w_o']
    x = x + ffn(rms_norm(x, p['norm2']))
    state = (k, v, ki, u_ext[:, -(SHORT_CONV - 1):], xbc_ext[:, -(SSM_CONV - 1):], h_last.astype(x.dtype))
    return x, state


def setup_inputs(seed: int = 0) -> dict:
    key = jax.random.key(seed)
    ks = iter(jax.random.split(key, 48))
    f32 = jnp.float32

    def nrm(shape, scale):
        return jax.random.normal(next(ks), shape, f32) * scale

    def gain(shape):
        return 1.0 + 0.05 * jax.random.normal(next(ks), shape, f32)

    n_pages = PAST_LEN // PAGE_SIZE
    n_used = DEC_BATCH * n_pages
    n_pool = n_used + max(n_used // 4, 1)
    n_dense = (DEPTH + 1) // 2
    n_moe = DEPTH // 2
    page_table = jax.random.permutation(next(ks), n_pool)[:n_used].reshape(DEC_BATCH, n_pages).astype(jnp.int32)
    dt0 = jnp.exp(jax.random.uniform(next(ks), (DEPTH, SSM_HEADS), f32, math.log(1e-3), math.log(1e-1)))
    dt_bias = dt0 + jnp.log(-jnp.expm1(-dt0))
    a_log = jnp.log(jax.random.uniform(next(ks), (DEPTH, SSM_HEADS), f32, 1.0, 16.0))
    return {
        'x_prompt': nrm((BATCH, SEQ, D_MODEL), 1.0),
        'x_sample': nrm((DEC_BATCH, DEC_SEQ, D_MODEL), 1.0),
        'cache_k': nrm((DEPTH, n_pool, PAGE_SIZE, N_KV_HEADS, HEAD_DIM), 1.0),
        'cache_v': nrm((DEPTH, n_pool, PAGE_SIZE, N_KV_HEADS, HEAD_DIM), 1.0),
        'cache_kidx': nrm((DEPTH, n_pool, PAGE_SIZE, IDX_DIM), 1.0),
        'state_conv': nrm((DEPTH, DEC_BATCH, SHORT_CONV - 1, D_CONV), 1.0),
        'state_ssm_conv': nrm((DEPTH, DEC_BATCH, SSM_CONV - 1, SSM_CONV_DIM), 1.0),
        'state_ssm': nrm((DEPTH, DEC_BATCH, SSM_HEADS, SSM_HEAD_DIM, SSM_STATE), 0.1),
        'page_table': page_table,
        'norm1': gain((DEPTH, D_MODEL)),
        'w_in': nrm((DEPTH, D_MODEL, D_IN), D_MODEL ** -0.5),
        'b_gate': nrm((DEPTH, N_BRANCH, D_MODEL), 0.01),
        'conv_w': nrm((DEPTH, SHORT_CONV, D_CONV), SHORT_CONV ** -0.5),
        'q_norm': gain((DEPTH, HEAD_DIM)),
        'k_norm': gain((DEPTH, HEAD_DIM)),
        'w_pc': nrm((DEPTH, D_CONV, D_MODEL), D_CONV ** -0.5),
        'w_pa': nrm((DEPTH, N_HEADS * HEAD_DIM, D_MODEL), (N_HEADS * HEAD_DIM) ** -0.5),
        'ssm_conv_w': nrm((DEPTH, SSM_CONV, SSM_CONV_DIM), SSM_CONV ** -0.5),
        'ssm_conv_b': nrm((DEPTH, SSM_CONV_DIM), 0.01),
        'dt_bias': dt_bias,
        'a_log': a_log,
        'd_skip': gain((DEPTH, SSM_HEADS)),
        'ssm_norm': gain((DEPTH, SSM_INNER)),
        'w_ps': nrm((DEPTH, SSM_INNER, D_MODEL), SSM_INNER ** -0.5),
        'w_o': nrm((DEPTH, D_MODEL, D_MODEL), 0.5 * D_MODEL ** -0.5),
        'norm2': gain((DEPTH, D_MODEL)),
        'w_gate_dense': nrm((n_dense, D_MODEL, D_FF), D_MODEL ** -0.5),
        'w_up_dense': nrm((n_dense, D_MODEL, D_FF), D_MODEL ** -0.5),
        'w_down_dense': nrm((n_dense, D_FF, D_MODEL), 0.5 * D_FF ** -0.5),
        'w_router': nrm((n_moe, D_MODEL, N_EXPERTS), D_MODEL ** -0.5),
        'w_gate_moe': nrm((n_moe, N_EXPERTS, D_MODEL, D_FF_EXPERT), D_MODEL ** -0.5),
        'w_up_moe': nrm((n_moe, N_EXPERTS, D_MODEL, D_FF_EXPERT), D_MODEL ** -0.5),
        'w_down_moe': nrm((n_moe, N_EXPERTS, D_FF_EXPERT, D_MODEL), 0.5 * D_FF_EXPERT ** -0.5),
    }


def reference(x_prompt, x_sample, cache_k, cache_v, cache_kidx, state_conv, state_ssm_conv, state_ssm,
              page_table, norm1, w_in, b_gate, conv_w, q_norm, k_norm, w_pc, w_pa, ssm_conv_w, ssm_conv_b,
              dt_bias, a_log, d_skip, ssm_norm, w_ps, w_o, norm2, w_gate_dense, w_up_dense, w_down_dense,
              w_router, w_gate_moe, w_up_moe, w_down_moe):
    bp, sp = x_prompt.shape[:2]
    past = page_table.shape[1] * PAGE_SIZE
    pos_p = jnp.arange(sp)
    pos_s = past + jnp.arange(x_sample.shape[1])
    zero_conv = jnp.zeros((bp, SHORT_CONV - 1, D_CONV), x_prompt.dtype)
    zero_ssm_conv = jnp.zeros((bp, SSM_CONV - 1, SSM_CONV_DIM), x_prompt.dtype)
    zero_ssm = jnp.zeros((bp, SSM_HEADS, SSM_HEAD_DIM, SSM_STATE), jnp.float32)

    hp, hs = x_prompt, x_sample
    outs_p = [[] for _ in range(6)]
    outs_s = [[] for _ in range(6)]
    for l in range(DEPTH):
        p = dict(norm1=norm1[l], w_in=w_in[l], b_gate=b_gate[l], conv_w=conv_w[l], q_norm=q_norm[l],
                 k_norm=k_norm[l], w_pc=w_pc[l], w_pa=w_pa[l], ssm_conv_w=ssm_conv_w[l],
                 ssm_conv_b=ssm_conv_b[l], dt_bias=dt_bias[l], a_log=a_log[l], d_skip=d_skip[l],
                 ssm_norm=ssm_norm[l], w_ps=w_ps[l], w_o=w_o[l], norm2=norm2[l])
        i = l // 2
        if l % 2 == 0:
            ffn = functools.partial(swiglu, w_gate=w_gate_dense[i], w_up=w_up_dense[i], w_down=w_down_dense[i])
        else:
            ffn = functools.partial(moe_swiglu, w_router=w_router[i], w_gate=w_gate_moe[i],
                                    w_up=w_up_moe[i], w_down=w_down_moe[i])
        hp, st_p = hybrid_layer(hp, p, pos_p, zero_conv, zero_ssm_conv, zero_ssm, dsa_prompt, ffn)
        attend_s = functools.partial(dsa_decode, pool_k=cache_k[l], pool_v=cache_v[l],
                                     pool_kidx=cache_kidx[l], page_table=page_table)
        hs, st_s = hybrid_layer(hs, p, pos_s, state_conv[l], state_ssm_conv[l], state_ssm[l], attend_s, ffn)
        for j in range(6):
            outs_p[j].append(st_p[j])
            outs_s[j].append(st_s[j])

    return (hp, hs,
            jnp.stack(outs_p[0]), jnp.stack(outs_p[1]), jnp.stack(outs_p[2]),
            jnp.stack(outs_p[3]), jnp.stack(outs_p[4]), jnp.stack(outs_p[5]),
            jnp.stack(outs_s[0]), jnp.stack(outs_s[1]), jnp.stack(outs_s[2]),
            jnp.stack(outs_s[3]), jnp.stack(outs_s[4]), jnp.stack(outs_s[5]))
```

```python
import functools
import math

import jax
import jax.numpy as jnp
from jax import lax
from jax.experimental import pallas as pl
from jax.experimental.pallas import tpu as pltpu

F32 = jnp.float32
BF16 = jnp.bfloat16
I32 = jnp.int32

D_MODEL = 1024
D_CONV = 512
N_HEADS = 8
HEAD_DIM = 64
N_KV_HEADS = 2
N_IDX_HEADS = 4
IDX_DIM = 64
TOPK_MAX = 256
Q_BLOCK = 128
ROPE_THETA = 10000.0
SSM_INNER = 512
SSM_HEADS = 8
SSM_HEAD_DIM = 64
SSM_GROUPS = 2
SSM_STATE = 128
SSM_CHUNK = 128
SSM_CONV_DIM = 1024
PAGE_SIZE = 128
N_EXPERTS = 8
EPS = 1e-6

LANES = 128
SUBLANES = 8
VMEM_LIMIT = 52 * 1024 * 1024

G_OFF, CB_OFF, CC_OFF, CX_OFF, Q_OFF, Z_OFF, XS_OFF, BC_OFF = 0, 3072, 3584, 4096, 4608, 5120, 5632, 6144
K_OFF, V_OFF, QI_OFF, KW_OFF, DT_OFF = 6656, 6784, 6912, 7168, 7296
N_PROJ = 7680

INT_MIN = -2 ** 31
INT_MAX = 2 ** 31 - 1
KEY_NEG_INF = (-8388608) ^ 0x7FFFFFFF
NEG_BIG = -1e30

NT_DIMS = (((1,), (1,)), ((), ()))


def _cparams(sem):
    return pltpu.CompilerParams(dimension_semantics=sem, vmem_limit_bytes=VMEM_LIMIT)


def _sigmoid(x):
    return 1.0 / (1.0 + jnp.exp(-x))


def _silu(x):
    return x * _sigmoid(x)


def _softplus(x):
    return jnp.maximum(x, 0.0) + jnp.log1p(jnp.exp(-jnp.abs(x)))


def _sortable(x):
    bits = pltpu.bitcast(x, I32)
    return bits ^ (lax.shift_right_arithmetic(bits, 31) & 0x7FFFFFFF)


def _rms(x, g):
    return x * lax.rsqrt(jnp.mean(x * x, axis=-1, keepdims=True) + EPS) * g


def _inproj_kernel(x_ref, g_ref, w_ref, o_ref, h_ref):
    @pl.when(pl.program_id(1) == 0)
    def _():
        h_ref[...] = _rms(x_ref[...], g_ref[...]).astype(BF16)

    o_ref[...] = jnp.dot(h_ref[...], w_ref[...], preferred_element_type=F32)


def _inproj(x, g, w):
    m = x.shape[0]
    tm = min(m, 1024)
    tn = 1536
    return pl.pallas_call(
        _inproj_kernel,
        grid=(m // tm, N_PROJ // tn),
        in_specs=[pl.BlockSpec((tm, D_MODEL), lambda i, j: (i, 0)),
                  pl.BlockSpec((1, D_MODEL), lambda i, j: (0, 0)),
                  pl.BlockSpec((D_MODEL, tn), lambda i, j: (0, j))],
        out_specs=pl.BlockSpec((tm, tn), lambda i, j: (i, j)),
        out_shape=jax.ShapeDtypeStruct((m, N_PROJ), F32),
        scratch_shapes=[pltpu.VMEM((tm, D_MODEL), BF16)],
        compiler_params=_cparams(("parallel", "arbitrary")),
        name="inproj",
    )(x, g, w)


def _prep_kernel(q_ref, k_ref, v_ref, qi_ref, kw_ref, cos_ref, sin_ref, qg_ref, kg_ref,
                 qp_ref, kf_ref, kb_ref, vb_ref, qip_ref, kif_ref, kib_ref, wo_ref):
    tm = cos_ref.shape[0]
    cos = cos_ref[...]
    sin = sin_ref[...]
    lane = lax.broadcasted_iota(I32, (tm, LANES), 1)
    low_half = lane < HEAD_DIM
    first_rot = (lane % HEAD_DIM) < (HEAD_DIM // 2)
    r = lax.broadcasted_iota(I32, (LANES, LANES), 0) // HEAD_DIM
    c = lax.broadcasted_iota(I32, (LANES, LANES), 1) // HEAD_DIM
    seg = jnp.where(r == c, 1.0 / HEAD_DIM, 0.0).astype(BF16)

    def rope(x):
        fwd = pltpu.roll(x, LANES - HEAD_DIM // 2, 1)
        bwd = pltpu.roll(x, HEAD_DIM // 2, 1)
        return x * cos + jnp.where(first_rot, fwd, bwd) * sin

    def headnorm(x, g):
        s = x * x
        hi = s.astype(BF16)
        lo = (s - hi.astype(F32)).astype(BF16)
        ms = (jnp.dot(hi, seg, preferred_element_type=F32)
              + jnp.dot(lo, seg, preferred_element_type=F32))
        return x * lax.rsqrt(ms + EPS) * g

    qg = qg_ref[...]
    for s in range(N_HEADS // 2):
        slab = rope(headnorm(q_ref[:, s * LANES:(s + 1) * LANES], qg)) * (HEAD_DIM ** -0.5)
        swapped = pltpu.roll(slab, HEAD_DIM, 1)
        grp = (2 * s) // (N_HEADS // N_KV_HEADS)
        for hh in range(2):
            h = 2 * s + hh
            src = slab if hh == grp else swapped
            keep = low_half if grp == 0 else jnp.logical_not(low_half)
            qp_ref[:, h * LANES:(h + 1) * LANES] = jnp.where(keep, src, 0.0).astype(BF16)

    k = rope(headnorm(k_ref[...], kg_ref[...]))
    kf_ref[...] = k
    kb_ref[...] = k.astype(BF16)
    vb_ref[...] = v_ref[...].astype(BF16)

    for s in range(N_IDX_HEADS // 2):
        slab = rope(qi_ref[:, s * LANES:(s + 1) * LANES])
        swapped = pltpu.roll(slab, HEAD_DIM, 1)
        for hh in range(2):
            h = 2 * s + hh
            src = slab if hh == 0 else swapped
            qip_ref[:, h * LANES:(h + 1) * LANES] = jnp.where(low_half, src, 0.0).astype(BF16)

    kw = kw_ref[...]
    ki = rope(kw)
    kif_ref[...] = ki
    kib_ref[...] = jnp.where(low_half, ki, 0.0).astype(BF16)
    wo_ref[...] = pltpu.roll(kw, HEAD_DIM, 1) * (N_IDX_HEADS ** -0.5)


def _prep(proj, cos, sin, qg, kg):
    m = proj.shape[0]
    tm = min(m, 512)
    row = lambda w, off: pl.BlockSpec((tm, w), lambda i: (i, off // w))
    full = lambda w: pl.BlockSpec((tm, w), lambda i: (i, 0))
    const = pl.BlockSpec((1, LANES), lambda i: (0, 0))
    return pl.pallas_call(
        _prep_kernel,
        grid=(m // tm,),
        in_specs=[row(512, Q_OFF), row(LANES, K_OFF), row(LANES, V_OFF), row(256, QI_OFF), row(LANES, KW_OFF),
                  full(LANES), full(LANES), const, const],
        out_specs=[full(N_HEADS * LANES), full(LANES), full(LANES), full(LANES),
                   full(N_IDX_HEADS * LANES), full(LANES), full(LANES), full(LANES)],
        out_shape=[jax.ShapeDtypeStruct((m, N_HEADS * LANES), BF16),
                   jax.ShapeDtypeStruct((m, LANES), F32),
                   jax.ShapeDtypeStruct((m, LANES), BF16),
                   jax.ShapeDtypeStruct((m, LANES), BF16),
                   jax.ShapeDtypeStruct((m, N_IDX_HEADS * LANES), BF16),
                   jax.ShapeDtypeStruct((m, LANES), F32),
                   jax.ShapeDtypeStruct((m, LANES), BF16),
                   jax.ShapeDtypeStruct((m, LANES), F32)],
        compiler_params=_cparams(("parallel",)),
        name="prep",
    )(proj, proj, proj, proj, proj, cos, sin, qg, kg)


def _assemble_heads(acc_of, inv_of):
    slabs = []
    for s in range(N_HEADS // 2):
        grp = (2 * s) // (N_HEADS // N_KV_HEADS)
        a = acc_of(2 * s) * inv_of(2 * s)
        b = acc_of(2 * s + 1) * inv_of(2 * s + 1)
        lane = lax.broadcasted_iota(I32, a.shape, a.ndim - 1)
        if grp == 0:
            b = pltpu.roll(b, HEAD_DIM, a.ndim - 1)
        else:
            a = pltpu.roll(a, HEAD_DIM, a.ndim - 1)
        slabs.append(jnp.where(lane < HEAD_DIM, a, b))
    return slabs


def _dsa_kernel(qp_ref, qi_ref, w_ref, k_ref, v_ref, ki_ref, o_ref,
                keys_ref, x_ref, m_ref, l_ref, acc_ref, *, k_sel, kc, nbits):
    qb = Q_BLOCK
    j = pl.program_id(1)
    nck = (j * qb + qb + kc - 1) // kc
    qpos = j * qb + lax.broadcasted_iota(I32, (qb, 1), 0)
    lane_k = lax.broadcasted_iota(I32, (qb, kc), 1)
    w = w_ref[...]

    def score_body(c, carry):
        off = pl.multiple_of(c * kc, kc)
        kic = ki_ref[pl.ds(off, kc), :]
        acc = jnp.zeros((qb, kc), F32)
        for h in range(N_IDX_HEADS):
            s = lax.dot_general(qi_ref[:, h * LANES:(h + 1) * LANES], kic, NT_DIMS,
                                preferred_element_type=F32)
            acc = acc + jnp.maximum(s * (IDX_DIM ** -0.5), 0.0) * w[:, h:h + 1]
        acc = jnp.where(off + lane_k <= qpos, acc, -jnp.inf)
        keys_ref[:, pl.ds(off, kc)] = _sortable(acc)
        return carry

    lax.fori_loop(0, nck, score_body, 0)

    def count_where(pred):
        def body(c, acc):
            off = pl.multiple_of(c * kc, kc)
            f = jnp.where(pred(keys_ref[:, pl.ds(off, kc)], off), 1.0, 0.0)
            part = f[:, 0:LANES]
            for t in range(1, kc // LANES):
                part = part + f[:, t * LANES:(t + 1) * LANES]
            return acc + part

        acc = lax.fori_loop(0, nck, body, jnp.zeros((qb, LANES), F32))
        return jnp.sum(acc, axis=1, keepdims=True)

    def bit_body(i, t):
        cand = t + lax.shift_left(jnp.int32(1), 31 - i)
        cnt = count_where(lambda kk, off: kk >= cand)
        return jnp.where(cnt >= k_sel, cand, t)

    thr = lax.fori_loop(0, 32, bit_body, jnp.full((qb, 1), INT_MIN, I32))
    thr = jnp.maximum(thr, KEY_NEG_INF)
    finite_thr = thr > KEY_NEG_INF
    cnt_ge = count_where(lambda kk, off: kk >= thr)
    cnt_gt = count_where(lambda kk, off: kk > thr)
    need = k_sel - cnt_gt
    tie = jnp.logical_and(cnt_ge > k_sel, finite_thr)
    x_default = jnp.where(finite_thr, INT_MAX, -1)
    x_ref[...] = jnp.broadcast_to(x_default, (qb, LANES))

    @pl.when(jnp.max(jnp.where(tie, 1.0, 0.0)) > 0.0)
    def _():
        def xbody(i, x):
            cand = x + lax.shift_left(jnp.int32(1), nbits - 1 - i)
            cnt = count_where(lambda kk, off: jnp.logical_and(kk == thr, off + lane_k < cand))
            return jnp.where(cnt < need, cand, x)

        x = lax.fori_loop(0, nbits, xbody, jnp.zeros((qb, 1), I32))
        x_ref[...] = jnp.broadcast_to(jnp.where(tie, x, x_default), (qb, LANES))

    xcut = x_ref[:, 0:1]
    m_ref[...] = jnp.full(m_ref.shape, NEG_BIG, F32)
    l_ref[...] = jnp.zeros(l_ref.shape, F32)
    acc_ref[...] = jnp.zeros(acc_ref.shape, F32)

    def att_body(c, carry):
        off = pl.multiple_of(c * kc, kc)
        kk = keys_ref[:, pl.ds(off, kc)]
        sel = jnp.logical_or(kk > thr, jnp.logical_and(kk == thr, off + lane_k <= xcut))
        kch = k_ref[pl.ds(off, kc), :]
        vch = v_ref[pl.ds(off, kc), :]
        for h in range(N_HEADS):
            s = lax.dot_general(qp_ref[:, h * LANES:(h + 1) * LANES], kch, NT_DIMS,
                                preferred_element_type=F32)
            s = jnp.where(sel, s, -jnp.inf)
            m_old = m_ref[h][:, 0:1]
            m_new = jnp.maximum(m_old, jnp.max(s, axis=1, keepdims=True))
            alpha = jnp.exp(m_old - m_new)
            p = jnp.exp(s - m_new)
            l_new = alpha * l_ref[h][:, 0:1] + jnp.sum(p, axis=1, keepdims=True)
            acc_ref[h] = alpha * acc_ref[h] + jnp.dot(p.astype(BF16), vch, preferred_element_type=F32)
            m_ref[h] = jnp.broadcast_to(m_new, (qb, LANES))
            l_ref[h] = jnp.broadcast_to(l_new, (qb, LANES))
        return carry

    lax.fori_loop(0, nck, att_body, 0)
    slabs = _assemble_heads(lambda h: acc_ref[h], lambda h: 1.0 / l_ref[h])
    for s, slab in enumerate(slabs):
        o_ref[:, s * LANES:(s + 1) * LANES] = slab


def _dsa_prompt(qp, qip, wi, kb, vb, kib, nb, seq):
    k_sel = min(TOPK_MAX, seq // 4)
    kc = 512 if seq % 512 == 0 else Q_BLOCK
    nbits = max(1, (seq - 1).bit_length())
    r3 = lambda a: a.reshape(nb, seq, a.shape[-1])
    qblk = lambda w: pl.BlockSpec((None, Q_BLOCK, w), lambda b, j: (b, j, 0))
    seqblk = pl.BlockSpec((None, seq, LANES), lambda b, j: (b, 0, 0))
    out = pl.pallas_call(
        functools.partial(_dsa_kernel, k_sel=k_sel, kc=kc, nbits=nbits),
        grid=(nb, seq // Q_BLOCK),
        in_specs=[qblk(N_HEADS * LANES), qblk(N_IDX_HEADS * LANES), qblk(LANES), seqblk, seqblk, seqblk],
        out_specs=qblk(N_HEADS * HEAD_DIM),
        out_shape=jax.ShapeDtypeStruct((nb, seq, N_HEADS * HEAD_DIM), F32),
        scratch_shapes=[pltpu.VMEM((Q_BLOCK, seq), I32),
                        pltpu.VMEM((Q_BLOCK, LANES), I32),
                        pltpu.VMEM((N_HEADS, Q_BLOCK, LANES), F32),
                        pltpu.VMEM((N_HEADS, Q_BLOCK, LANES), F32),
                        pltpu.VMEM((N_HEADS, Q_BLOCK, LANES), F32)],
        compiler_params=_cparams(("parallel", "arbitrary")),
        name="dsa_prompt",
    )(r3(qp), r3(qip), r3(wi), r3(kb), r3(vb), r3(kib))
    return out.reshape(nb * seq, N_HEADS * HEAD_DIM)


def _conva_kernel(cb_ref, cc_ref, cx_ref, cch_ref, cxh_ref, w_ref, o_ref, st_ref, *, tiles_per_seq):
    tm = cb_ref.shape[0]
    first = (pl.program_id(0) % tiles_per_seq) == 0
    u = cc_ref[...] * cx_ref[...]
    uh = jnp.where(first, 0.0, cch_ref[...] * cxh_ref[...])
    ext = jnp.concatenate([uh, u], axis=0)
    conv = (w_ref[2:3, :] * u + w_ref[1:2, :] * ext[SUBLANES - 1:SUBLANES - 1 + tm]
            + w_ref[0:1, :] * ext[SUBLANES - 2:SUBLANES - 2 + tm])
    o_ref[...] = cb_ref[...] * conv
    st_ref[...] = u[tm - 2:tm, :]


def _conva_prompt(proj, conv_w, nb, seq):
    m = proj.shape[0]
    tm = min(seq, 512)
    tps = seq // tm
    col = lambda off: pl.BlockSpec((tm, D_CONV), lambda i: (i, off // D_CONV))
    halo = lambda off: pl.BlockSpec(
        (SUBLANES, D_CONV), lambda i: (jnp.maximum(i * (tm // SUBLANES) - 1, 0), off // D_CONV))
    return pl.pallas_call(
        functools.partial(_conva_kernel, tiles_per_seq=tps),
        grid=(m // tm,),
        in_specs=[col(CB_OFF), col(CC_OFF), col(CX_OFF), halo(CC_OFF), halo(CX_OFF),
                  pl.BlockSpec((3, D_CONV), lambda i: (0, 0))],
        out_specs=[pl.BlockSpec((tm, D_CONV), lambda i: (i, 0)),
                   pl.BlockSpec((None, 2, D_CONV), lambda i: (i // tps, 0, 0))],
        out_shape=[jax.ShapeDtypeStruct((m, D_CONV), F32),
                   jax.ShapeDtypeStruct((nb, 2, D_CONV), F32)],
        compiler_params=_cparams(("arbitrary",)),
        name="conva_prompt",
    )(proj, proj, proj, proj, proj, conv_w)


def _ssd_kernel(xs_ref, bc_ref, dt_ref, xh_ref, bh_ref, cwx_ref, cwb_ref, cbx_ref, cbb_ref,
                dtb_ref, alog_ref, dsk_ref, y_ref, hout_ref, h_ref, *, nchunk):
    cl = SSM_CHUNK
    c = pl.program_id(1)
    first = c == 0

    @pl.when(first)
    def _():
        h_ref[...] = jnp.zeros(h_ref.shape, F32)

    def conv(cur, halo, w_ref, b_ref):
        ext = jnp.concatenate([jnp.where(first, 0.0, halo), cur], axis=0)
        out = (w_ref[3:4, :] * cur + w_ref[2:3, :] * ext[SUBLANES - 1:SUBLANES - 1 + cl]
               + w_ref[1:2, :] * ext[SUBLANES - 2:SUBLANES - 2 + cl]
               + w_ref[0:1, :] * ext[SUBLANES - 3:SUBLANES - 3 + cl] + b_ref[...])
        return _silu(out)

    xs = conv(xs_ref[...], xh_ref[...], cwx_ref, cbx_ref)
    bc = conv(bc_ref[...], bh_ref[...], cwb_ref, cbb_ref)
    dt = _softplus(dt_ref[...] + dtb_ref[...])
    a = dt * (-jnp.exp(alog_ref[...]))
    ri = lax.broadcasted_iota(I32, (cl, cl), 0)
    ci = lax.broadcasted_iota(I32, (cl, cl), 1)
    causal = ri >= ci
    cs = jnp.dot(jnp.where(causal, 1.0, 0.0), a, preferred_element_type=F32,
                 precision=lax.Precision.HIGHEST)
    cs_t = cs.T
    lane = lax.broadcasted_iota(I32, (cl, LANES), 1)
    lo = lane < SSM_HEAD_DIM
    rows_lo = lax.broadcasted_iota(I32, (LANES, 1), 0) < SSM_HEAD_DIM
    heads_per_group = SSM_HEADS // SSM_GROUPS

    cb = []
    for g in range(SSM_GROUPS):
        bg = bc[:, g * SSM_STATE:(g + 1) * SSM_STATE].astype(BF16)
        cg = bc[:, (SSM_GROUPS + g) * SSM_STATE:(SSM_GROUPS + g + 1) * SSM_STATE].astype(BF16)
        cb.append((bg, cg, lax.dot_general(cg, bg, NT_DIMS, preferred_element_type=F32)))

    for s in range(SSM_HEADS // 2):
        h0, h1 = 2 * s, 2 * s + 1
        bg, cg, cbg = cb[h0 // heads_per_group]
        sl = slice(s * LANES, (s + 1) * LANES)
        xs_s = xs[:, sl]
        col0, col1 = cs[:, h0:h0 + 1], cs[:, h1:h1 + 1]
        last0, last1 = cs[cl - 1:cl, h0:h0 + 1], cs[cl - 1:cl, h1:h1 + 1]
        xdt = xs_s * jnp.where(lo, dt[:, h0:h0 + 1], dt[:, h1:h1 + 1])
        xdt_b = xdt.astype(BF16)
        m0 = (cbg * jnp.where(causal, jnp.exp(col0 - cs_t[h0:h0 + 1, :]), 0.0)).astype(BF16)
        m1 = (cbg * jnp.where(causal, jnp.exp(col1 - cs_t[h1:h1 + 1, :]), 0.0)).astype(BF16)
        y_diag = jnp.where(lo, jnp.dot(m0, xdt_b, preferred_element_type=F32),
                           jnp.dot(m1, xdt_b, preferred_element_type=F32))
        hs = h_ref[sl, :]
        y_off = lax.dot_general(cg, hs.astype(BF16), NT_DIMS, preferred_element_type=F32)
        y_off = y_off * jnp.where(lo, jnp.exp(col0), jnp.exp(col1))
        y_ref[:, sl] = y_diag + y_off + dsk_ref[:, sl] * xs_s
        xw = xdt * jnp.where(lo, jnp.exp(last0 - col0), jnp.exp(last1 - col1))
        st = jnp.dot(xw.T.astype(BF16), bg, preferred_element_type=F32)
        h_ref[sl, :] = hs * jnp.where(rows_lo, jnp.exp(last0), jnp.exp(last1)) + st

    @pl.when(c == nchunk - 1)
    def _():
        hout_ref[...] = h_ref[...]


def _ssd_prompt(proj, cw, cbias, dtb, alog, dsk, nb, seq):
    m = proj.shape[0]
    cl = SSM_CHUNK
    nchunk = seq // cl
    blk = lambda w, off: pl.BlockSpec((cl, w), lambda b, c: (b * nchunk + c, off // w))
    halo = lambda off: pl.BlockSpec(
        (SUBLANES, 512), lambda b, c: (jnp.maximum((b * nchunk + c) * (cl // SUBLANES) - 1, 0), off // 512))
    const = lambda r, w, j: pl.BlockSpec((r, w), lambda b, c: (0, j))
    y, hout = pl.pallas_call(
        functools.partial(_ssd_kernel, nchunk=nchunk),
        grid=(nb, nchunk),
        in_specs=[blk(512, XS_OFF), blk(512, BC_OFF), blk(LANES, DT_OFF), halo(XS_OFF), halo(BC_OFF),
                  const(4, 512, 0), const(4, 512, 1), const(1, 512, 0), const(1, 512, 1),
                  const(1, LANES, 0), const(1, LANES, 0), const(1, 512, 0)],
        out_specs=[pl.BlockSpec((cl, SSM_INNER), lambda b, c: (b * nchunk + c, 0)),
                   pl.BlockSpec((None, SSM_INNER, SSM_STATE), lambda b, c: (b, 0, 0))],
        out_shape=[jax.ShapeDtypeStruct((m, SSM_INNER), F32),
                   jax.ShapeDtypeStruct((nb, SSM_INNER, SSM_STATE), F32)],
        scratch_shapes=[pltpu.VMEM((SSM_INNER, SSM_STATE), F32)],
        compiler_params=_cparams(("parallel", "arbitrary")),
        name="ssd_prompt",
    )(proj, proj, proj, proj, proj, cw, cw, cbias, cbias, dtb, alog, dsk)
    return y, hout


def _merge_kernel(x_ref, pa_ref, at_ref, ys_ref, z_ref, g0_ref, g1_ref, g2_ref, sn_ref, bg_ref,
                  wpc_ref, wpa_ref, wps_ref, wo_ref, o_ref):
    ssd = _rms(ys_ref[...] * _silu(z_ref[...]), sn_ref[...])
    ya = jnp.dot(pa_ref[...].astype(BF16), wpc_ref[...], preferred_element_type=F32)
    yb = jnp.dot(at_ref[...].astype(BF16), wpa_ref[...], preferred_element_type=F32)
    yc = jnp.dot(ssd.astype(BF16), wps_ref[...], preferred_element_type=F32)
    merged = (_sigmoid(g0_ref[...] + bg_ref[0:1, :]) * ya + _sigmoid(g1_ref[...] + bg_ref[1:2, :]) * yb
              + _sigmoid(g2_ref[...] + bg_ref[2:3, :]) * yc)
    o_ref[...] = x_ref[...] + jnp.dot(merged.astype(BF16), wo_ref[...], preferred_element_type=F32)


def _merge(x, pre_a, attn, y_ssd, proj, ssm_norm, b_gate, w_pc, w_pa, w_ps, w_o):
    m = x.shape[0]
    tm = min(m, 256)
    row = lambda w: pl.BlockSpec((tm, w), lambda i: (i, 0))
    pcol = lambda w, off: pl.BlockSpec((tm, w), lambda i: (i, off // w))
    const = lambda r, w: pl.BlockSpec((r, w), lambda i: (0, 0))
    return pl.pallas_call(
        _merge_kernel,
        grid=(m // tm,),
        in_specs=[row(D_MODEL), row(512), row(512), row(512), pcol(512, Z_OFF),
                  pcol(D_MODEL, G_OFF), pcol(D_MODEL, G_OFF + D_MODEL), pcol(D_MODEL, G_OFF + 2 * D_MODEL),
                  const(1, 512), const(3, D_MODEL),
                  const(512, D_MODEL), const(512, D_MODEL), const(512, D_MODEL), const(D_MODEL, D_MODEL)],
        out_specs=row(D_MODEL),
        out_shape=jax.ShapeDtypeStruct((m, D_MODEL), F32),
        compiler_params=_cparams(("parallel",)),
        name="merge",
    )(x, pre_a, attn, y_ssd, proj, proj, proj, proj, ssm_norm, b_gate, w_pc, w_pa, w_ps, w_o)


def _ffn_kernel(x_ref, g_ref, wr_ref, wg_ref, wu_ref, wd_ref, o_ref, h_ref, acc_ref, comb_ref,
                *, routed, n_e, n_f):
    e = pl.program_id(1)
    j = pl.program_id(2)
    tm = x_ref.shape[0]

    @pl.when(jnp.logical_and(e == 0, j == 0))
    def _():
        hf = _rms(x_ref[...], g_ref[...])
        h_ref[...] = hf.astype(BF16)
        acc_ref[...] = jnp.zeros(acc_ref.shape, F32)
        if routed:
            lane = lax.broadcasted_iota(I32, (tm, LANES), 1).astype(F32)
            logits = jnp.dot(hf, wr_ref[...], preferred_element_type=F32, precision=lax.Precision.HIGHEST)
            logits = jnp.where(lane < n_e, logits, -jnp.inf)
            m1 = jnp.max(logits, axis=1, keepdims=True)
            i1 = jnp.min(jnp.where(logits == m1, lane, float(LANES)), axis=1, keepdims=True)
            rest = jnp.where(lane == i1, -jnp.inf, logits)
            m2 = jnp.max(rest, axis=1, keepdims=True)
            i2 = jnp.min(jnp.where(rest == m2, lane, float(LANES)), axis=1, keepdims=True)
            e2 = jnp.exp(m2 - m1)
            den = 1.0 + e2
            comb_ref[...] = jnp.where(lane == i1, 1.0 / den, 0.0) + jnp.where(lane == i2, e2 / den, 0.0)

    h = h_ref[...]
    gate = jnp.dot(h, wg_ref[...], preferred_element_type=F32)
    up = jnp.dot(h, wu_ref[...], preferred_element_type=F32)
    act = _silu(gate) * up
    if routed:
        lane = lax.broadcasted_iota(I32, (tm, LANES), 1)
        act = act * jnp.sum(jnp.where(lane == e, comb_ref[...], 0.0), axis=1, keepdims=True)
    acc_ref[...] += jnp.dot(act.astype(BF16), wd_ref[...], preferred_element_type=F32)

    @pl.when(jnp.logical_and(e == n_e - 1, j == n_f - 1))
    def _():
        o_ref[...] = x_ref[...] + acc_ref[...]


def _ffn(x, g, w_router, w_gate, w_up, w_down, routed):
    m = x.shape[0]
    n_e, _, d_ff = w_gate.shape
    tm = min(m, 1024)
    tf = 256
    n_f = d_ff // tf
    return pl.pallas_call(
        functools.partial(_ffn_kernel, routed=routed, n_e=n_e, n_f=n_f),
        grid=(m // tm, n_e, n_f),
        in_specs=[pl.BlockSpec((tm, D_MODEL), lambda i, e, j: (i, 0)),
                  pl.BlockSpec((1, D_MODEL), lambda i, e, j: (0, 0)),
                  pl.BlockSpec((D_MODEL, LANES), lambda i, e, j: (0, 0)),
                  pl.BlockSpec((None, D_MODEL, tf), lambda i, e, j: (e, 0, j)),
                  pl.BlockSpec((None, D_MODEL, tf), lambda i, e, j: (e, 0, j)),
                  pl.BlockSpec((None, tf, D_MODEL), lambda i, e, j: (e, j, 0))],
        out_specs=pl.BlockSpec((tm, D_MODEL), lambda i, e, j: (i, 0)),
        out_shape=jax.ShapeDtypeStruct((m, D_MODEL), F32),
        scratch_shapes=[pltpu.VMEM((tm, D_MODEL), BF16), pltpu.VMEM((tm, D_MODEL), F32),
                        pltpu.VMEM((tm, LANES), F32)],
        compiler_params=_cparams(("parallel", "arbitrary", "arbitrary")),
        name="moe" if routed else "ffn",
    )(x, g, w_router, w_gate, w_up, w_down)


def _decmix_kernel(cb_ref, cc_ref, cx_ref, xs_ref, bc_ref, dt_ref, sc0_ref, sc1_ref,
                   sx0_ref, sx1_ref, sx2_ref, sb0_ref, sb1_ref, sb2_ref,
                   cw_ref, cwx_ref, cwb_ref, cbx_ref, cbb_ref, dtb_ref, alog_ref,
                   pa_ref, u_ref, xso_ref, bco_ref, dto_ref, ea_ref):
    u = cc_ref[...] * cx_ref[...]
    u_ref[...] = u
    pa_ref[...] = cb_ref[...] * (cw_ref[0:1, :] * sc0_ref[...] + cw_ref[1:2, :] * sc1_ref[...] + cw_ref[2:3, :] * u)
    xso_ref[...] = _silu(cwx_ref[0:1, :] * sx0_ref[...] + cwx_ref[1:2, :] * sx1_ref[...]
                         + cwx_ref[2:3, :] * sx2_ref[...] + cwx_ref[3:4, :] * xs_ref[...] + cbx_ref[...])
    bco_ref[...] = _silu(cwb_ref[0:1, :] * sb0_ref[...] + cwb_ref[1:2, :] * sb1_ref[...]
                         + cwb_ref[2:3, :] * sb2_ref[...] + cwb_ref[3:4, :] * bc_ref[...] + cbb_ref[...])
    dt = _softplus(dt_ref[...] + dtb_ref[...])
    dto_ref[...] = dt
    ea_ref[...] = jnp.exp(dt * (-jnp.exp(alog_ref[...])))


def _decmix(proj, st_conv, st_ssm_conv, conv_w, cw, cbias, dtb, alog):
    nb = proj.shape[0]
    pcol = lambda w, off: pl.BlockSpec((nb, w), lambda i: (0, off // w))
    full = lambda w: pl.BlockSpec((nb, w), lambda i: (0, 0))
    const = lambda r, w, j: pl.BlockSpec((r, w), lambda i: (0, j))
    sx = [st_ssm_conv[:, t, :512] for t in range(3)]
    sb = [st_ssm_conv[:, t, 512:] for t in range(3)]
    return pl.pallas_call(
        _decmix_kernel,
        grid=(1,),
        in_specs=[pcol(512, CB_OFF), pcol(512, CC_OFF), pcol(512, CX_OFF), pcol(512, XS_OFF), pcol(512, BC_OFF),
                  pcol(LANES, DT_OFF)] + [full(512)] * 8
                 + [const(3, 512, 0), const(4, 512, 0), const(4, 512, 1), const(1, 512, 0), const(1, 512, 1),
                    const(1, LANES, 0), const(1, LANES, 0)],
        out_specs=[full(512), full(512), full(512), full(512), full(LANES), full(LANES)],
        out_shape=[jax.ShapeDtypeStruct((nb, 512), F32)] * 4 + [jax.ShapeDtypeStruct((nb, LANES), F32)] * 2,
        compiler_params=_cparams(("arbitrary",)),
        name="decode_mix",
    )(proj, proj, proj, proj, proj, proj, st_conv[:, 0], st_conv[:, 1], *sx, *sb,
      conv_w, cw, cw, cbias, cbias, dtb, alog)


def _decssd_kernel(h_ref, dt_ref, xs_ref, b_ref, c_ref, ea_ref, dsk_ref, y_ref, ho_ref):
    xs = xs_ref[...]
    h_new = h_ref[...] * ea_ref[...] + (xs * dt_ref[...]) * b_ref[...]
    ho_ref[...] = h_new
    y_ref[...] = jnp.sum(h_new * c_ref[...], axis=-1, keepdims=True) + dsk_ref[...] * xs


def _decssd(h0, dt, xs, bh, ch, ea, dsk):
    nb = h0.shape[0]
    blk = lambda a, b: pl.BlockSpec((None, SSM_HEADS, a, b), lambda i: (i, 0, 0, 0))
    return pl.pallas_call(
        _decssd_kernel,
        grid=(nb,),
        in_specs=[blk(SSM_HEAD_DIM, SSM_STATE), blk(1, 1), blk(SSM_HEAD_DIM, 1), blk(1, SSM_STATE),
                  blk(1, SSM_STATE), blk(1, 1), pl.BlockSpec((SSM_HEADS, 1, 1), lambda i: (0, 0, 0))],
        out_specs=[blk(SSM_HEAD_DIM, 1), blk(SSM_HEAD_DIM, SSM_STATE)],
        out_shape=[jax.ShapeDtypeStruct((nb, SSM_HEADS, SSM_HEAD_DIM, 1), F32),
                   jax.ShapeDtypeStruct((nb, SSM_HEADS, SSM_HEAD_DIM, SSM_STATE), F32)],
        compiler_params=_cparams(("parallel",)),
        name="decode_ssd",
    )(h0, dt, xs, bh, ch, ea, dsk)


def _decscore_kernel(pt_ref, qi_ref, w_ref, kid_ref, o_ref):
    del pt_ref
    kid = kid_ref[...].astype(BF16)
    s = lax.dot_general(qi_ref[:, 0:IDX_DIM].astype(BF16), kid, NT_DIMS, preferred_element_type=F32)
    sc = jnp.maximum(s * (IDX_DIM ** -0.5), 0.0) * w_ref[...]
    o_ref[...] = jnp.sum(sc, axis=0, keepdims=True)


def _decscore(page_table, qip, wrows, pool_kidx):
    nb, n_pages = page_table.shape
    grid_spec = pltpu.PrefetchScalarGridSpec(
        num_scalar_prefetch=1,
        grid=(nb, n_pages),
        in_specs=[pl.BlockSpec((None, N_IDX_HEADS, LANES), lambda b, p, pt: (b, 0, 0)),
                  pl.BlockSpec((None, N_IDX_HEADS, PAGE_SIZE), lambda b, p, pt: (b, 0, 0)),
                  pl.BlockSpec((None, PAGE_SIZE, IDX_DIM), lambda b, p, pt: (pt[b, p], 0, 0))],
        out_specs=pl.BlockSpec((None, None, 1, PAGE_SIZE), lambda b, p, pt: (b, p, 0, 0)),
    )
    out = pl.pallas_call(
        _decscore_kernel,
        grid_spec=grid_spec,
        out_shape=jax.ShapeDtypeStruct((nb, n_pages, 1, PAGE_SIZE), F32),
        compiler_params=_cparams(("parallel", "arbitrary")),
        name="decode_scores",
    )(page_table, qip, wrows, pool_kidx)
    return out.reshape(nb, n_pages * PAGE_SIZE)


def _decselect_kernel(sc_ref, qi_ref, ki_ref, w_ref, sel_ref, self_ref, keys_ref, *, k_sel, kc, nbits):
    nb, past = sc_ref.shape
    nck = past // kc
    lane_k = lax.broadcasted_iota(I32, (nb, kc), 1)
    ki = ki_ref[...].astype(F32)
    own = jnp.zeros((nb, 1), F32)
    for h in range(N_IDX_HEADS):
        s = jnp.sum(qi_ref[:, h * LANES:(h + 1) * LANES].astype(F32) * ki, axis=1, keepdims=True)
        own = own + jnp.maximum(s * (IDX_DIM ** -0.5), 0.0) * w_ref[:, h:h + 1]
    own_key = _sortable(own)

    def key_body(c, carry):
        off = pl.multiple_of(c * kc, kc)
        keys_ref[:, pl.ds(off, kc)] = _sortable(sc_ref[:, pl.ds(off, kc)])
        return carry

    lax.fori_loop(0, nck, key_body, 0)

    def count_where(pred):
        def body(c, acc):
            off = pl.multiple_of(c * kc, kc)
            f = jnp.where(pred(keys_ref[:, pl.ds(off, kc)], off), 1.0, 0.0)
            part = f[:, 0:LANES]
            for t in range(1, kc // LANES):
                part = part + f[:, t * LANES:(t + 1) * LANES]
            return acc + part

        acc = lax.fori_loop(0, nck, body, jnp.zeros((nb, LANES), F32))
        return jnp.sum(acc, axis=1, keepdims=True)

    def bit_body(i, t):
        cand = t + lax.shift_left(jnp.int32(1), 31 - i)
        cnt = count_where(lambda kk, off: kk >= cand) + jnp.where(own_key >= cand, 1.0, 0.0)
        return jnp.where(cnt >= k_sel, cand, t)

    thr = lax.fori_loop(0, 32, bit_body, jnp.full((nb, 1), INT_MIN, I32))
    thr = jnp.maximum(thr, KEY_NEG_INF)
    finite_thr = thr > KEY_NEG_INF
    cnt_gt = count_where(lambda kk, off: kk > thr) + jnp.where(own_key > thr, 1.0, 0.0)
    need = k_sel - cnt_gt

    def xbody(i, x):
        cand = x + lax.shift_left(jnp.int32(1), nbits - 1 - i)
        cnt = count_where(lambda kk, off: jnp.logical_and(kk == thr, off + lane_k < cand))
        return jnp.where(cnt < need, cand, x)

    xcut = lax.fori_loop(0, nbits, xbody, jnp.zeros((nb, 1), I32))
    xcut = jnp.where(finite_thr, xcut, -1)
    ties_past = count_where(lambda kk, off: kk == thr)
    own_tie = jnp.logical_and(jnp.logical_and(own_key == thr, finite_thr), ties_past < need)
    self_ref[...] = jnp.broadcast_to(
        jnp.where(jnp.logical_or(own_key > thr, own_tie), 1.0, 0.0), (nb, LANES))

    def sel_body(c, carry):
        off = pl.multiple_of(c * kc, kc)
        kk = keys_ref[:, pl.ds(off, kc)]
        sel = jnp.logical_or(kk > thr, jnp.logical_and(kk == thr, off + lane_k <= xcut))
        sel_ref[:, pl.ds(off, kc)] = jnp.where(sel, 1.0, 0.0)
        return carry

    lax.fori_loop(0, nck, sel_body, 0)


def _decselect(scores, qip, kib, wi):
    nb, past = scores.shape
    k_sel = min(TOPK_MAX, (past + 1) // 4)
    kc = 512 if past % 512 == 0 else PAGE_SIZE
    nbits = max(1, past.bit_length())
    full = lambda w: pl.BlockSpec((nb, w), lambda i: (0, 0))
    return pl.pallas_call(
        functools.partial(_decselect_kernel, k_sel=k_sel, kc=kc, nbits=nbits),
        grid=(1,),
        in_specs=[full(past), full(N_IDX_HEADS * LANES), full(LANES), full(LANES)],
        out_specs=[full(past), full(LANES)],
        out_shape=[jax.ShapeDtypeStruct((nb, past), F32), jax.ShapeDtypeStruct((nb, LANES), F32)],
        scratch_shapes=[pltpu.VMEM((nb, past), I32)],
        compiler_params=_cparams(("arbitrary",)),
        name="decode_select",
    )(scores, qip, kib, wi)


def _decattn_kernel(pt_ref, qp_ref, sel_ref, kp_ref, vp_ref, kn_ref, vn_ref, self_ref, o_ref,
                    m_ref, l_ref, acc_ref, *, n_pages):
    del pt_ref
    p = pl.program_id(1)

    @pl.when(p == 0)
    def _():
        m_ref[...] = jnp.full(m_ref.shape, NEG_BIG, F32)
        l_ref[...] = jnp.zeros(l_ref.shape, F32)
        acc_ref[...] = jnp.zeros(acc_ref.shape, F32)

    q = qp_ref[...]
    s = lax.dot_general(q.astype(BF16), kp_ref[...].astype(BF16), NT_DIMS, preferred_element_type=F32)
    s = jnp.where(sel_ref[...] > 0.0, s, -jnp.inf)
    m_old = m_ref[:, 0:1]
    m_new = jnp.maximum(m_old, jnp.max(s, axis=1, keepdims=True))
    alpha = jnp.exp(m_old - m_new)
    pr = jnp.exp(s - m_new)
    l_new = alpha * l_ref[:, 0:1] + jnp.sum(pr, axis=1, keepdims=True)
    acc_ref[...] = alpha * acc_ref[...] + jnp.dot(pr.astype(BF16), vp_ref[...].astype(BF16),
                                                  preferred_element_type=F32)
    m_ref[...] = jnp.broadcast_to(m_new, m_ref.shape)
    l_ref[...] = jnp.broadcast_to(l_new, l_ref.shape)

    @pl.when(p == n_pages - 1)
    def _():
        s_own = jnp.sum(q * kn_ref[...], axis=1, keepdims=True)
        s_own = jnp.where(self_ref[:, 0:1] > 0.0, s_own, -jnp.inf)
        m_o = m_ref[:, 0:1]
        m_n = jnp.maximum(m_o, s_own)
        al = jnp.exp(m_o - m_n)
        p_own = jnp.exp(s_own - m_n)
        l_n = al * l_ref[:, 0:1] + p_own
        acc = al * acc_ref[...] + p_own.astype(BF16).astype(F32) * vn_ref[...]
        o_ref[...] = acc / l_n


def _decattn(page_table, qp, sel, pool_k, pool_v, kb, vb, self_sel):
    nb, n_pages = page_table.shape
    row = lambda r: pl.BlockSpec((None, r, LANES), lambda b, p, pt: (b, 0, 0))
    page = pl.BlockSpec((None, PAGE_SIZE, LANES), lambda b, p, pt: (pt[b, p], 0, 0))
    grid_spec = pltpu.PrefetchScalarGridSpec(
        num_scalar_prefetch=1,
        grid=(nb, n_pages),
        in_specs=[row(N_HEADS),
                  pl.BlockSpec((None, None, 1, PAGE_SIZE), lambda b, p, pt: (b, p, 0, 0)),
                  page, page, row(1), row(1), row(1)],
        out_specs=row(N_HEADS),
        scratch_shapes=[pltpu.VMEM((N_HEADS, LANES), F32)] * 3,
    )
    return pl.pallas_call(
        functools.partial(_decattn_kernel, n_pages=n_pages),
        grid_spec=grid_spec,
        out_shape=jax.ShapeDtypeStruct((nb, N_HEADS, LANES), F32),
        compiler_params=_cparams(("parallel", "arbitrary")),
        name="decode_attend",
    )(page_table, qp, sel, pool_k, pool_v, kb, vb, self_sel)


def _pack_w_in(w):
    d = w.shape[0]
    pad = lambda n: jnp.zeros((d, n), w.dtype)
    cols = [w[:, 4172:7244],
            w[:, 0:2048],
            w[:, 2628:3140],
            w[:, 3140:4164],
            w[:, 2048:2304],
            w[:, 2304:2560],
            w[:, 2560:2628], pad(60),
            w[:, 4164:4172], pad(120),
            pad(N_PROJ - 7424)]
    return jnp.concatenate(cols, axis=1).astype(BF16)


def _pad_lanes(v):
    return jnp.zeros((1, LANES), F32).at[0, :v.shape[0]].set(v)


def _rope_tables(pos):
    half = HEAD_DIM // 2
    inv = ROPE_THETA ** (-jnp.arange(half, dtype=F32) / half)
    ang = pos.astype(F32)[:, None] * inv[None, :]
    cos, sin = jnp.cos(ang), jnp.sin(ang)
    cos2 = jnp.concatenate([cos, cos], axis=1)
    sin2 = jnp.concatenate([-sin, sin], axis=1)
    return jnp.tile(cos2, (1, 2)), jnp.tile(sin2, (1, 2))


def _head_halves(o):
    hpg = N_HEADS // N_KV_HEADS
    parts = [o[:, h, (h // hpg) * HEAD_DIM:(h // hpg + 1) * HEAD_DIM] for h in range(N_HEADS)]
    return jnp.concatenate(parts, axis=-1)


def kernel(x_prompt, x_sample, cache_k, cache_v, cache_kidx, state_conv, state_ssm_conv, state_ssm, page_table,
           norm1, w_in, b_gate, conv_w, q_norm, k_norm, w_pc, w_pa, ssm_conv_w, ssm_conv_b, dt_bias, a_log,
           d_skip, ssm_norm, w_ps, w_o, norm2, w_gate_dense, w_up_dense, w_down_dense, w_router, w_gate_moe,
           w_up_moe, w_down_moe):
    nb, seq, _ = x_prompt.shape
    db = x_sample.shape[0]
    depth = w_in.shape[0]
    n_pages = page_table.shape[1]
    past = n_pages * PAGE_SIZE
    n_pool = cache_k.shape[1]

    hp = x_prompt.reshape(nb * seq, D_MODEL)
    hs = x_sample.reshape(db, D_MODEL)
    cos_p, sin_p = _rope_tables(jnp.tile(jnp.arange(seq), nb))
    cos_s, sin_s = _rope_tables(jnp.full((db,), past))

    outs_p = [[] for _ in range(6)]
    outs_s = [[] for _ in range(6)]
    for l in range(depth):
        wp = _pack_w_in(w_in[l])
        g1 = norm1[l].reshape(1, D_MODEL)
        g2 = norm2[l].reshape(1, D_MODEL)
        qg = jnp.tile(q_norm[l], 2).reshape(1, LANES)
        kg = jnp.tile(k_norm[l], 2).reshape(1, LANES)
        cw = ssm_conv_w[l]
        cbias = ssm_conv_b[l].reshape(1, SSM_CONV_DIM)
        dtb = _pad_lanes(dt_bias[l])
        alog = _pad_lanes(a_log[l])
        dsk = jnp.repeat(d_skip[l], SSM_HEAD_DIM).reshape(1, SSM_INNER)
        sn = ssm_norm[l].reshape(1, SSM_INNER)
        wpc, wpa, wps, wo = (w.astype(BF16) for w in (w_pc[l], w_pa[l], w_ps[l], w_o[l]))
        i = l // 2
        if l % 2 == 0:
            routed = False
            wr = jnp.zeros((D_MODEL, LANES), F32)
            wg, wu, wd = (w[i:i + 1].astype(BF16) for w in (w_gate_dense, w_up_dense, w_down_dense))
        else:
            routed = True
            wr = jnp.zeros((D_MODEL, LANES), F32).at[:, :N_EXPERTS].set(w_router[i])
            wg, wu, wd = (w[i].astype(BF16) for w in (w_gate_moe, w_up_moe, w_down_moe))

        proj = _inproj(hp, g1, wp)
        qp, kf, kb, vb, qip, kif, kib, wi = _prep(proj, cos_p, sin_p, qg, kg)
        attn = _dsa_prompt(qp, qip, wi, kb, vb, kib, nb, seq)
        pre_a, conv_st = _conva_prompt(proj, conv_w[l], nb, seq)
        y_ssd, h_last = _ssd_prompt(proj, cw, cbias, dtb, alog, dsk, nb, seq)
        hp = _merge(hp, pre_a, attn, y_ssd, proj, sn, b_gate[l], wpc, wpa, wps, wo)
        hp = _ffn(hp, g2, wr, wg, wu, wd, routed)
        proj3 = proj.reshape(nb, seq, N_PROJ)
        outs_p[0].append(kf.reshape(nb, seq, N_KV_HEADS, HEAD_DIM))
        outs_p[1].append(proj3[:, :, V_OFF:V_OFF + LANES].reshape(nb, seq, N_KV_HEADS, HEAD_DIM))
        outs_p[2].append(kif[:, :IDX_DIM].reshape(nb, seq, IDX_DIM))
        outs_p[3].append(conv_st)
        outs_p[4].append(proj3[:, seq - 3:, XS_OFF:XS_OFF + SSM_CONV_DIM])
        outs_p[5].append(h_last.reshape(nb, SSM_HEADS, SSM_HEAD_DIM, SSM_STATE))

        proj = _inproj(hs, g1, wp)
        qp, kf, kb, vb, qip, kif, kib, wi = _prep(proj, cos_s, sin_s, qg, kg)
        wrows = jnp.broadcast_to(wi[:, :N_IDX_HEADS, None], (db, N_IDX_HEADS, PAGE_SIZE))
        scores = _decscore(page_table, qip.astype(F32).reshape(db, N_IDX_HEADS, LANES), wrows, cache_kidx[l])
        sel, self_sel = _decselect(scores, qip, kib, wi)
        o = _decattn(page_table, qp.astype(F32).reshape(db, N_HEADS, LANES),
                     sel.reshape(db, n_pages, 1, PAGE_SIZE),
                     cache_k[l].reshape(n_pool, PAGE_SIZE, LANES), cache_v[l].reshape(n_pool, PAGE_SIZE, LANES),
                     kb.astype(F32).reshape(db, 1, LANES), vb.astype(F32).reshape(db, 1, LANES),
                     self_sel.reshape(db, 1, LANES))
        attn = _head_halves(o)
        pre_a, u, xs, bcv, dt, ea = _decmix(proj, state_conv[l], state_ssm_conv[l], conv_w[l], cw, cbias, dtb, alog)
        hpg = SSM_HEADS // SSM_GROUPS
        xs4 = xs.reshape(db, SSM_HEADS, SSM_HEAD_DIM, 1)
        bh = jnp.repeat(bcv[:, :SSM_GROUPS * SSM_STATE].reshape(db, SSM_GROUPS, 1, SSM_STATE), hpg, axis=1)
        ch = jnp.repeat(bcv[:, SSM_GROUPS * SSM_STATE:].reshape(db, SSM_GROUPS, 1, SSM_STATE), hpg, axis=1)
        y4, h_new = _decssd(state_ssm[l], dt[:, :SSM_HEADS, None, None], xs4, bh, ch, ea[:, :SSM_HEADS, None, None],
                            d_skip[l].reshape(SSM_HEADS, 1, 1))
        hs = _merge(hs, pre_a, attn, y4.reshape(db, SSM_INNER), proj, sn, b_gate[l], wpc, wpa, wps, wo)
        hs = _ffn(hs, g2, wr, wg, wu, wd, routed)
        outs_s[0].append(kf.reshape(db, 1, N_KV_HEADS, HEAD_DIM))
        outs_s[1].append(proj[:, V_OFF:V_OFF + LANES].reshape(db, 1, N_KV_HEADS, HEAD_DIM))
        outs_s[2].append(kif[:, :IDX_DIM].reshape(db, 1, IDX_DIM))
        outs_s[3].append(jnp.stack([state_conv[l][:, 1], u], axis=1))
        outs_s[4].append(jnp.concatenate(
            [state_ssm_conv[l][:, 1:], proj[:, None, XS_OFF:XS_OFF + SSM_CONV_DIM]], axis=1))
        outs_s[5].append(h_new)

    return (hp.reshape(nb, seq, D_MODEL), hs.reshape(db, 1, D_MODEL),
            *(jnp.stack(o) for o in outs_p), *(jnp.stack(o) for o in outs_s))
```

```python
import functools
import math

import jax
import jax.numpy as jnp
from jax import lax
from jax.experimental import pallas as pl
from jax.experimental.pallas import tpu as pltpu

F32 = jnp.float32
BF16 = jnp.bfloat16
I32 = jnp.int32

D_MODEL = 1024
D_CONV = 512
N_HEADS = 8
HEAD_DIM = 64
N_KV_HEADS = 2
N_IDX_HEADS = 4
IDX_DIM = 64
TOPK_MAX = 256
Q_BLOCK = 128
ROPE_THETA = 10000.0
SSM_INNER = 512
SSM_HEADS = 8
SSM_HEAD_DIM = 64
SSM_GROUPS = 2
SSM_STATE = 128
SSM_CHUNK = 128
SSM_CONV_DIM = 1024
PAGE_SIZE = 128
N_EXPERTS = 8
EPS = 1e-6

LANES = 128
SUBLANES = 8
VMEM_LIMIT = 52 * 1024 * 1024

G_OFF, CB_OFF, CC_OFF, CX_OFF, Q_OFF, Z_OFF, XS_OFF, BC_OFF = 0, 3072, 3584, 4096, 4608, 5120, 5632, 6144
K_OFF, V_OFF, QI_OFF, KW_OFF, DT_OFF = 6656, 6784, 6912, 7168, 7296
N_PROJ = 7680

INT_MIN = -2 ** 31
INT_MAX = 2 ** 31 - 1
KEY_NEG_INF = (-8388608) ^ 0x7FFFFFFF
NEG_BIG = -1e30

NT_DIMS = (((1,), (1,)), ((), ()))


def _cparams(sem):
    return pltpu.CompilerParams(dimension_semantics=sem, vmem_limit_bytes=VMEM_LIMIT)


def _sigmoid(x):
    return 1.0 / (1.0 + jnp.exp(-x))


def _silu(x):
    return x * _sigmoid(x)


def _softplus(x):
    return jnp.maximum(x, 0.0) + jnp.log1p(jnp.exp(-jnp.abs(x)))


def _sortable(x):
    bits = pltpu.bitcast(x, I32)
    return bits ^ (lax.shift_right_arithmetic(bits, 31) & 0x7FFFFFFF)


def _rms(x, g):
    return x * lax.rsqrt(jnp.mean(x * x, axis=-1, keepdims=True) + EPS) * g


def _inproj_kernel(x_ref, g_ref, w_ref, o_ref, h_ref):
    @pl.when(pl.program_id(1) == 0)
    def _():
        h_ref[...] = _rms(x_ref[...], g_ref[...]).astype(BF16)

    o_ref[...] = jnp.dot(h_ref[...], w_ref[...], preferred_element_type=F32)


def _inproj(x, g, w):
    m = x.shape[0]
    tm = min(m, 1024)
    tn = 1536
    return pl.pallas_call(
        _inproj_kernel,
        grid=(m // tm, N_PROJ // tn),
        in_specs=[pl.BlockSpec((tm, D_MODEL), lambda i, j: (i, 0)),
                  pl.BlockSpec((1, D_MODEL), lambda i, j: (0, 0)),
                  pl.BlockSpec((D_MODEL, tn), lambda i, j: (0, j))],
        out_specs=pl.BlockSpec((tm, tn), lambda i, j: (i, j)),
        out_shape=jax.ShapeDtypeStruct((m, N_PROJ), F32),
        scratch_shapes=[pltpu.VMEM((tm, D_MODEL), BF16)],
        compiler_params=_cparams(("parallel", "arbitrary")),
        name="inproj",
    )(x, g, w)


def _prep_kernel(q_ref, k_ref, v_ref, qi_ref, kw_ref, cos_ref, sin_ref, qg_ref, kg_ref,
                 qp_ref, kf_ref, kb_ref, vb_ref, qip_ref, kif_ref, kib_ref, wo_ref):
    tm = cos_ref.shape[0]
    cos = cos_ref[...]
    sin = sin_ref[...]
    lane = lax.broadcasted_iota(I32, (tm, LANES), 1)
    low_half = lane < HEAD_DIM
    first_rot = (lane % HEAD_DIM) < (HEAD_DIM // 2)
    r = lax.broadcasted_iota(I32, (LANES, LANES), 0) // HEAD_DIM
    c = lax.broadcasted_iota(I32, (LANES, LANES), 1) // HEAD_DIM
    seg = jnp.where(r == c, 1.0 / HEAD_DIM, 0.0).astype(BF16)

    def rope(x):
        fwd = pltpu.roll(x, LANES - HEAD_DIM // 2, 1)
        bwd = pltpu.roll(x, HEAD_DIM // 2, 1)
        return x * cos + jnp.where(first_rot, fwd, bwd) * sin

    def headnorm(x, g):
        s = x * x
        hi = s.astype(BF16)
        lo = (s - hi.astype(F32)).astype(BF16)
        ms = (jnp.dot(hi, seg, preferred_element_type=F32)
              + jnp.dot(lo, seg, preferred_element_type=F32))
        return x * lax.rsqrt(ms + EPS) * g

    qg = qg_ref[...]
    for s in range(N_HEADS // 2):
        slab = rope(headnorm(q_ref[:, s * LANES:(s + 1) * LANES], qg)) * (HEAD_DIM ** -0.5)
        swapped = pltpu.roll(slab, HEAD_DIM, 1)
        grp = (2 * s) // (N_HEADS // N_KV_HEADS)
        for hh in range(2):
            h = 2 * s + hh
            src = slab if hh == grp else swapped
            keep = low_half if grp == 0 else jnp.logical_not(low_half)
            qp_ref[:, h * LANES:(h + 1) * LANES] = jnp.where(keep, src, 0.0).astype(BF16)

    k = rope(headnorm(k_ref[...], kg_ref[...]))
    kf_ref[...] = k
    kb_ref[...] = k.astype(BF16)
    vb_ref[...] = v_ref[...].astype(BF16)

    for s in range(N_IDX_HEADS // 2):
        slab = rope(qi_ref[:, s * LANES:(s + 1) * LANES])
        swapped = pltpu.roll(slab, HEAD_DIM, 1)
        for hh in range(2):
            h = 2 * s + hh
            src = slab if hh == 0 else swapped
            qip_ref[:, h * LANES:(h + 1) * LANES] = jnp.where(low_half, src, 0.0).astype(BF16)

    kw = kw_ref[...]
    ki = rope(kw)
    kif_ref[...] = ki
    kib_ref[...] = jnp.where(low_half, ki, 0.0).astype(BF16)
    wo_ref[...] = pltpu.roll(kw, HEAD_DIM, 1) * (N_IDX_HEADS ** -0.5)


def _prep(proj, cos, sin, qg, kg):
    m = proj.shape[0]
    tm = min(m, 512)
    row = lambda w, off: pl.BlockSpec((tm, w), lambda i: (i, off // w))
    full = lambda w: pl.BlockSpec((tm, w), lambda i: (i, 0))
    const = pl.BlockSpec((1, LANES), lambda i: (0, 0))
    return pl.pallas_call(
        _prep_kernel,
        grid=(m // tm,),
        in_specs=[row(512, Q_OFF), row(LANES, K_OFF), row(LANES, V_OFF), row(256, QI_OFF), row(LANES, KW_OFF),
                  full(LANES), full(LANES), const, const],
        out_specs=[full(N_HEADS * LANES), full(LANES), full(LANES), full(LANES),
                   full(N_IDX_HEADS * LANES), full(LANES), full(LANES), full(LANES)],
        out_shape=[jax.ShapeDtypeStruct((m, N_HEADS * LANES), BF16),
                   jax.ShapeDtypeStruct((m, LANES), F32),
                   jax.ShapeDtypeStruct((m, LANES), BF16),
                   jax.ShapeDtypeStruct((m, LANES), BF16),
                   jax.ShapeDtypeStruct((m, N_IDX_HEADS * LANES), BF16),
                   jax.ShapeDtypeStruct((m, LANES), F32),
                   jax.ShapeDtypeStruct((m, LANES), BF16),
                   jax.ShapeDtypeStruct((m, LANES), F32)],
        compiler_params=_cparams(("parallel",)),
        name="prep",
    )(proj, proj, proj, proj, proj, cos, sin, qg, kg)


def _assemble_heads(acc_of, inv_of):
    slabs = []
    for s in range(N_HEADS // 2):
        grp = (2 * s) // (N_HEADS // N_KV_HEADS)
        a = acc_of(2 * s) * inv_of(2 * s)
        b = acc_of(2 * s + 1) * inv_of(2 * s + 1)
        lane = lax.broadcasted_iota(I32, a.shape, a.ndim - 1)
        if grp == 0:
            b = pltpu.roll(b, HEAD_DIM, a.ndim - 1)
        else:
            a = pltpu.roll(a, HEAD_DIM, a.ndim - 1)
        slabs.append(jnp.where(lane < HEAD_DIM, a, b))
    return slabs


def _dsa_kernel(qp_ref, qi_ref, w_ref, k_ref, v_ref, ki_ref, o_ref,
                keys_ref, x_ref, m_ref, l_ref, acc_ref, *, k_sel, kc, nbits):
    qb = Q_BLOCK
    j = pl.program_id(1)
    nck = (j * qb + qb + kc - 1) // kc
    qpos = j * qb + lax.broadcasted_iota(I32, (qb, 1), 0)
    lane_k = lax.broadcasted_iota(I32, (qb, kc), 1)
    w = w_ref[...]

    def score_body(c, carry):
        off = pl.multiple_of(c * kc, kc)
        kic = ki_ref[pl.ds(off, kc), :]
        acc = jnp.zeros((qb, kc), F32)
        for h in range(N_IDX_HEADS):
            s = lax.dot_general(qi_ref[:, h * LANES:(h + 1) * LANES], kic, NT_DIMS,
                                preferred_element_type=F32)
            acc = acc + jnp.maximum(s * (IDX_DIM ** -0.5), 0.0) * w[:, h:h + 1]
        acc = jnp.where(off + lane_k <= qpos, acc, -jnp.inf)
        keys_ref[:, pl.ds(off, kc)] = _sortable(acc)
        return carry

    lax.fori_loop(0, nck, score_body, 0)

    def count_where(pred):
        def body(c, acc):
            off = pl.multiple_of(c * kc, kc)
            f = jnp.where(pred(keys_ref[:, pl.ds(off, kc)], off), 1.0, 0.0)
            part = f[:, 0:LANES]
            for t in range(1, kc // LANES):
                part = part + f[:, t * LANES:(t + 1) * LANES]
            return acc + part

        acc = lax.fori_loop(0, nck, body, jnp.zeros((qb, LANES), F32))
        return jnp.sum(acc, axis=1, keepdims=True)

    def bit_body(i, t):
        cand = t + lax.shift_left(jnp.int32(1), 31 - i)
        cnt = count_where(lambda kk, off: kk >= cand)
        return jnp.where(cnt >= k_sel, cand, t)

    thr = lax.fori_loop(0, 32, bit_body, jnp.full((qb, 1), INT_MIN, I32))
    thr = jnp.maximum(thr, KEY_NEG_INF)
    finite_thr = thr > KEY_NEG_INF
    cnt_ge = count_where(lambda kk, off: kk >= thr)
    cnt_gt = count_where(lambda kk, off: kk > thr)
    need = k_sel - cnt_gt
    tie = jnp.logical_and(cnt_ge > k_sel, finite_thr)
    x_default = jnp.where(finite_thr, INT_MAX, -1)
    x_ref[...] = jnp.broadcast_to(x_default, (qb, LANES))

    @pl.when(jnp.max(jnp.where(tie, 1.0, 0.0)) > 0.0)
    def _():
        def xbody(i, x):
            cand = x + lax.shift_left(jnp.int32(1), nbits - 1 - i)
            cnt = count_where(lambda kk, off: jnp.logical_and(kk == thr, off + lane_k < cand))
            return jnp.where(cnt < need, cand, x)

        x = lax.fori_loop(0, nbits, xbody, jnp.zeros((qb, 1), I32))
        x_ref[...] = jnp.broadcast_to(jnp.where(tie, x, x_default), (qb, LANES))

    xcut = x_ref[:, 0:1]
    m_ref[...] = jnp.full(m_ref.shape, NEG_BIG, F32)
    l_ref[...] = jnp.zeros(l_ref.shape, F32)
    acc_ref[...] = jnp.zeros(acc_ref.shape, F32)

    def att_body(c, carry):
        off = pl.multiple_of(c * kc, kc)
        kk = keys_ref[:, pl.ds(off, kc)]
        sel = jnp.logical_or(kk > thr, jnp.logical_and(kk == thr, off + lane_k <= xcut))
        kch = k_ref[pl.ds(off, kc), :]
        vch = v_ref[pl.ds(off, kc), :]
        for h in range(N_HEADS):
            s = lax.dot_general(qp_ref[:, h * LANES:(h + 1) * LANES], kch, NT_DIMS,
                                preferred_element_type=F32)
            s = jnp.where(sel, s, -jnp.inf)
            m_old = m_ref[h][:, 0:1]
            m_new = jnp.maximum(m_old, jnp.max(s, axis=1, keepdims=True))
            alpha = jnp.exp(m_old - m_new)
            p = jnp.exp(s - m_new)
            l_new = alpha * l_ref[h][:, 0:1] + jnp.sum(p, axis=1, keepdims=True)
            acc_ref[h] = alpha * acc_ref[h] + jnp.dot(p.astype(BF16), vch, preferred_element_type=F32)
            m_ref[h] = jnp.broadcast_to(m_new, (qb, LANES))
            l_ref[h] = jnp.broadcast_to(l_new, (qb, LANES))
        return carry

    lax.fori_loop(0, nck, att_body, 0)
    slabs = _assemble_heads(lambda h: acc_ref[h], lambda h: 1.0 / l_ref[h])
    for s, slab in enumerate(slabs):
        o_ref[:, s * LANES:(s + 1) * LANES] = slab


def _dsa_prompt(qp, qip, wi, kb, vb, kib, nb, seq):
    k_sel = min(TOPK_MAX, seq // 4)
    kc = 512 if seq % 512 == 0 else Q_BLOCK
    nbits = max(1, (seq - 1).bit_length())
    r3 = lambda a: a.reshape(nb, seq, a.shape[-1])
    qblk = lambda w: pl.BlockSpec((None, Q_BLOCK, w), lambda b, j: (b, j, 0))
    seqblk = pl.BlockSpec((None, seq, LANES), lambda b, j: (b, 0, 0))
    out = pl.pallas_call(
        functools.partial(_dsa_kernel, k_sel=k_sel, kc=kc, nbits=nbits),
        grid=(nb, seq // Q_BLOCK),
        in_specs=[qblk(N_HEADS * LANES), qblk(N_IDX_HEADS * LANES), qblk(LANES), seqblk, seqblk, seqblk],
        out_specs=qblk(N_HEADS * HEAD_DIM),
        out_shape=jax.ShapeDtypeStruct((nb, seq, N_HEADS * HEAD_DIM), F32),
        scratch_shapes=[pltpu.VMEM((Q_BLOCK, seq), I32),
                        pltpu.VMEM((Q_BLOCK, LANES), I32),
                        pltpu.VMEM((N_HEADS, Q_BLOCK, LANES), F32),
                        pltpu.VMEM((N_HEADS, Q_BLOCK, LANES), F32),
                        pltpu.VMEM((N_HEADS, Q_BLOCK, LANES), F32)],
        compiler_params=_cparams(("parallel", "arbitrary")),
        name="dsa_prompt",
    )(r3(qp), r3(qip), r3(wi), r3(kb), r3(vb), r3(kib))
    return out.reshape(nb * seq, N_HEADS * HEAD_DIM)


def _conva_kernel(cb_ref, cc_ref, cx_ref, cch_ref, cxh_ref, w_ref, o_ref, st_ref, *, tiles_per_seq):
    tm = cb_ref.shape[0]
    first = (pl.program_id(0) % tiles_per_seq) == 0
    u = cc_ref[...] * cx_ref[...]
    uh = jnp.where(first, 0.0, cch_ref[...] * cxh_ref[...])
    ext = jnp.concatenate([uh, u], axis=0)
    conv = (w_ref[2:3, :] * u + w_ref[1:2, :] * ext[SUBLANES - 1:SUBLANES - 1 + tm]
            + w_ref[0:1, :] * ext[SUBLANES - 2:SUBLANES - 2 + tm])
    o_ref[...] = cb_ref[...] * conv
    st_ref[...] = u[tm - 2:tm, :]


def _conva_prompt(proj, conv_w, nb, seq):
    m = proj.shape[0]
    tm = min(seq, 512)
    tps = seq // tm
    col = lambda off: pl.BlockSpec((tm, D_CONV), lambda i: (i, off // D_CONV))
    halo = lambda off: pl.BlockSpec(
        (SUBLANES, D_CONV), lambda i: (jnp.maximum(i * (tm // SUBLANES) - 1, 0), off // D_CONV))
    return pl.pallas_call(
        functools.partial(_conva_kernel, tiles_per_seq=tps),
        grid=(m // tm,),
        in_specs=[col(CB_OFF), col(CC_OFF), col(CX_OFF), halo(CC_OFF), halo(CX_OFF),
                  pl.BlockSpec((3, D_CONV), lambda i: (0, 0))],
        out_specs=[pl.BlockSpec((tm, D_CONV), lambda i: (i, 0)),
                   pl.BlockSpec((None, 2, D_CONV), lambda i: (i // tps, 0, 0))],
        out_shape=[jax.ShapeDtypeStruct((m, D_CONV), F32),
                   jax.ShapeDtypeStruct((nb, 2, D_CONV), F32)],
        compiler_params=_cparams(("arbitrary",)),
        name="conva_prompt",
    )(proj, proj, proj, proj, proj, conv_w)


def _ssd_kernel(xs_ref, bc_ref, dt_ref, xh_ref, bh_ref, cwx_ref, cwb_ref, cbx_ref, cbb_ref,
                dtb_ref, alog_ref, dsk_ref, y_ref, hout_ref, h_ref, *, nchunk):
    cl = SSM_CHUNK
    c = pl.program_id(1)
    first = c == 0

    @pl.when(first)
    def _():
        h_ref[...] = jnp.zeros(h_ref.shape, F32)

    def conv(cur, halo, w_ref, b_ref):
        ext = jnp.concatenate([jnp.where(first, 0.0, halo), cur], axis=0)
        out = (w_ref[3:4, :] * cur + w_ref[2:3, :] * ext[SUBLANES - 1:SUBLANES - 1 + cl]
               + w_ref[1:2, :] * ext[SUBLANES - 2:SUBLANES - 2 + cl]
               + w_ref[0:1, :] * ext[SUBLANES - 3:SUBLANES - 3 + cl] + b_ref[...])
        return _silu(out)

    xs = conv(xs_ref[...], xh_ref[...], cwx_ref, cbx_ref)
    bc = conv(bc_ref[...], bh_ref[...], cwb_ref, cbb_ref)
    dt = _softplus(dt_ref[...] + dtb_ref[...])
    a = dt * (-jnp.exp(alog_ref[...]))
    ri = lax.broadcasted_iota(I32, (cl, cl), 0)
    ci = lax.broadcasted_iota(I32, (cl, cl), 1)
    causal = ri >= ci
    cs = jnp.dot(jnp.where(causal, 1.0, 0.0), a, preferred_element_type=F32,
                 precision=lax.Precision.HIGHEST)
    cs_t = cs.T
    lane = lax.broadcasted_iota(I32, (cl, LANES), 1)
    lo = lane < SSM_HEAD_DIM
    rows_lo = lax.broadcasted_iota(I32, (LANES, 1), 0) < SSM_HEAD_DIM
    heads_per_group = SSM_HEADS // SSM_GROUPS

    cb = []
    for g in range(SSM_GROUPS):
        bg = bc[:, g * SSM_STATE:(g + 1) * SSM_STATE].astype(BF16)
        cg = bc[:, (SSM_GROUPS + g) * SSM_STATE:(SSM_GROUPS + g + 1) * SSM_STATE].astype(BF16)
        cb.append((bg, cg, lax.dot_general(cg, bg, NT_DIMS, preferred_element_type=F32)))

    for s in range(SSM_HEADS // 2):
        h0, h1 = 2 * s, 2 * s + 1
        bg, cg, cbg = cb[h0 // heads_per_group]
        sl = slice(s * LANES, (s + 1) * LANES)
        xs_s = xs[:, sl]
        col0, col1 = cs[:, h0:h0 + 1], cs[:, h1:h1 + 1]
        last0, last1 = cs[cl - 1:cl, h0:h0 + 1], cs[cl - 1:cl, h1:h1 + 1]
        xdt = xs_s * jnp.where(lo, dt[:, h0:h0 + 1], dt[:, h1:h1 + 1])
        xdt_b = xdt.astype(BF16)
        m0 = (cbg * jnp.where(causal, jnp.exp(col0 - cs_t[h0:h0 + 1, :]), 0.0)).astype(BF16)
        m1 = (cbg * jnp.where(causal, jnp.exp(col1 - cs_t[h1:h1 + 1, :]), 0.0)).astype(BF16)
        y_diag = jnp.where(lo, jnp.dot(m0, xdt_b, preferred_element_type=F32),
                           jnp.dot(m1, xdt_b, preferred_element_type=F32))
        hs = h_ref[sl, :]
        y_off = lax.dot_general(cg, hs.astype(BF16), NT_DIMS, preferred_element_type=F32)
        y_off = y_off * jnp.where(lo, jnp.exp(col0), jnp.exp(col1))
        y_ref[:, sl] = y_diag + y_off + dsk_ref[:, sl] * xs_s
        xw = xdt * jnp.where(lo, jnp.exp(last0 - col0), jnp.exp(last1 - col1))
        st = jnp.dot(xw.T.astype(BF16), bg, preferred_element_type=F32)
        h_ref[sl, :] = hs * jnp.where(rows_lo, jnp.exp(last0), jnp.exp(last1)) + st

    @pl.when(c == nchunk - 1)
    def _():
        hout_ref[...] = h_ref[...]


def _ssd_prompt(proj, cw, cbias, dtb, alog, dsk, nb, seq):
    m = proj.shape[0]
    cl = SSM_CHUNK
    nchunk = seq // cl
    blk = lambda w, off: pl.BlockSpec((cl, w), lambda b, c: (b * nchunk + c, off // w))
    halo = lambda off: pl.BlockSpec(
        (SUBLANES, 512), lambda b, c: (jnp.maximum((b * nchunk + c) * (cl // SUBLANES) - 1, 0), off // 512))
    const = lambda r, w, j: pl.BlockSpec((r, w), lambda b, c: (0, j))
    y, hout = pl.pallas_call(
        functools.partial(_ssd_kernel, nchunk=nchunk),
        grid=(nb, nchunk),
        in_specs=[blk(512, XS_OFF), blk(512, BC_OFF), blk(LANES, DT_OFF), halo(XS_OFF), halo(BC_OFF),
                  const(4, 512, 0), const(4, 512, 1), const(1, 512, 0), const(1, 512, 1),
                  const(1, LANES, 0), const(1, LANES, 0), const(1, 512, 0)],
        out_specs=[pl.BlockSpec((cl, SSM_INNER), lambda b, c: (b * nchunk + c, 0)),
                   pl.BlockSpec((None, SSM_INNER, SSM_STATE), lambda b, c: (b, 0, 0))],
        out_shape=[jax.ShapeDtypeStruct((m, SSM_INNER), F32),
                   jax.ShapeDtypeStruct((nb, SSM_INNER, SSM_STATE), F32)],
        scratch_shapes=[pltpu.VMEM((SSM_INNER, SSM_STATE), F32)],
        compiler_params=_cparams(("parallel", "arbitrary")),
        name="ssd_prompt",
    )(proj, proj, proj, proj, proj, cw, cw, cbias, cbias, dtb, alog, dsk)
    return y, hout


def _merge_kernel(x_ref, pa_ref, at_ref, ys_ref, z_ref, g0_ref, g1_ref, g2_ref, sn_ref, bg_ref,
                  wpc_ref, wpa_ref, wps_ref, wo_ref, o_ref):
    ssd = _rms(ys_ref[...] * _silu(z_ref[...]), sn_ref[...])
    ya = jnp.dot(pa_ref[...].astype(BF16), wpc_ref[...], preferred_element_type=F32)
    yb = jnp.dot(at_ref[...].astype(BF16), wpa_ref[...], preferred_element_type=F32)
    yc = jnp.dot(ssd.astype(BF16), wps_ref[...], preferred_element_type=F32)
    merged = (_sigmoid(g0_ref[...] + bg_ref[0:1, :]) * ya + _sigmoid(g1_ref[...] + bg_ref[1:2, :]) * yb
              + _sigmoid(g2_ref[...] + bg_ref[2:3, :]) * yc)
    o_ref[...] = x_ref[...] + jnp.dot(merged.astype(BF16), wo_ref[...], preferred_element_type=F32)


def _merge(x, pre_a, attn, y_ssd, proj, ssm_norm, b_gate, w_pc, w_pa, w_ps, w_o):
    m = x.shape[0]
    tm = min(m, 256)
    row = lambda w: pl.BlockSpec((tm, w), lambda i: (i, 0))
    pcol = lambda w, off: pl.BlockSpec((tm, w), lambda i: (i, off // w))
    const = lambda r, w: pl.BlockSpec((r, w), lambda i: (0, 0))
    return pl.pallas_call(
        _merge_kernel,
        grid=(m // tm,),
        in_specs=[row(D_MODEL), row(512), row(512), row(512), pcol(512, Z_OFF),
                  pcol(D_MODEL, G_OFF), pcol(D_MODEL, G_OFF + D_MODEL), pcol(D_MODEL, G_OFF + 2 * D_MODEL),
                  const(1, 512), const(3, D_MODEL),
                  const(512, D_MODEL), const(512, D_MODEL), const(512, D_MODEL), const(D_MODEL, D_MODEL)],
        out_specs=row(D_MODEL),
        out_shape=jax.ShapeDtypeStruct((m, D_MODEL), F32),
        compiler_params=_cparams(("parallel",)),
        name="merge",
    )(x, pre_a, attn, y_ssd, proj, proj, proj, proj, ssm_norm, b_gate, w_pc, w_pa, w_ps, w_o)


def _ffn_kernel(x_ref, g_ref, wr_ref, wg_ref, wu_ref, wd_ref, o_ref, h_ref, acc_ref, comb_ref,
                *, routed, n_e, n_f):
    e = pl.program_id(1)
    j = pl.program_id(2)
    tm = x_ref.shape[0]

    @pl.when(jnp.logical_and(e == 0, j == 0))
    def _():
        hf = _rms(x_ref[...], g_ref[...])
        h_ref[...] = hf.astype(BF16)
        acc_ref[...] = jnp.zeros(acc_ref.shape, F32)
        if routed:
            lane = lax.broadcasted_iota(I32, (tm, LANES), 1).astype(F32)
            logits = jnp.dot(hf, wr_ref[...], preferred_element_type=F32, precision=lax.Precision.HIGHEST)
            logits = jnp.where(lane < n_e, logits, -jnp.inf)
            m1 = jnp.max(logits, axis=1, keepdims=True)
            i1 = jnp.min(jnp.where(logits == m1, lane, float(LANES)), axis=1, keepdims=True)
            rest = jnp.where(lane == i1, -jnp.inf, logits)
            m2 = jnp.max(rest, axis=1, keepdims=True)
            i2 = jnp.min(jnp.where(rest == m2, lane, float(LANES)), axis=1, keepdims=True)
            e2 = jnp.exp(m2 - m1)
            den = 1.0 + e2
            comb_ref[...] = jnp.where(lane == i1, 1.0 / den, 0.0) + jnp.where(lane == i2, e2 / den, 0.0)

    h = h_ref[...]
    gate = jnp.dot(h, wg_ref[...], preferred_element_type=F32)
    up = jnp.dot(h, wu_ref[...], preferred_element_type=F32)
    act = _silu(gate) * up
    if routed:
        lane = lax.broadcasted_iota(I32, (tm, LANES), 1)
        act = act * jnp.sum(jnp.where(lane == e, comb_ref[...], 0.0), axis=1, keepdims=True)
    acc_ref[...] += jnp.dot(act.astype(BF16), wd_ref[...], preferred_element_type=F32)

    @pl.when(jnp.logical_and(e == n_e - 1, j == n_f - 1))
    def _():
        o_ref[...] = x_ref[...] + acc_ref[...]


def _ffn(x, g, w_router, w_gate, w_up, w_down, routed):
    m = x.shape[0]
    n_e, _, d_ff = w_gate.shape
    tm = min(m, 1024)
    tf = 256
    n_f = d_ff // tf
    return pl.pallas_call(
        functools.partial(_ffn_kernel, routed=routed, n_e=n_e, n_f=n_f),
        grid=(m // tm, n_e, n_f),
        in_specs=[pl.BlockSpec((tm, D_MODEL), lambda i, e, j: (i, 0)),
                  pl.BlockSpec((1, D_MODEL), lambda i, e, j: (0, 0)),
                  pl.BlockSpec((D_MODEL, LANES), lambda i, e, j: (0, 0)),
                  pl.BlockSpec((None, D_MODEL, tf), lambda i, e, j: (e, 0, j)),
                  pl.BlockSpec((None, D_MODEL, tf), lambda i, e, j: (e, 0, j)),
                  pl.BlockSpec((None, tf, D_MODEL), lambda i, e, j: (e, j, 0))],
        out_specs=pl.BlockSpec((tm, D_MODEL), lambda i, e, j: (i, 0)),
        out_shape=jax.ShapeDtypeStruct((m, D_MODEL), F32),
        scratch_shapes=[pltpu.VMEM((tm, D_MODEL), BF16), pltpu.VMEM((tm, D_MODEL), F32),
                        pltpu.VMEM((tm, LANES), F32)],
        compiler_params=_cparams(("parallel", "arbitrary", "arbitrary")),
        name="moe" if routed else "ffn",
    )(x, g, w_router, w_gate, w_up, w_down)


def _decmix_kernel(cb_ref, cc_ref, cx_ref, xs_ref, bc_ref, dt_ref, sc0_ref, sc1_ref,
                   sx0_ref, sx1_ref, sx2_ref, sb0_ref, sb1_ref, sb2_ref,
                   cw_ref, cwx_ref, cwb_ref, cbx_ref, cbb_ref, dtb_ref, alog_ref,
                   pa_ref, u_ref, xso_ref, bco_ref, dto_ref, ea_ref):
    u = cc_ref[...] * cx_ref[...]
    u_ref[...] = u
    pa_ref[...] = cb_ref[...] * (cw_ref[0:1, :] * sc0_ref[...] + cw_ref[1:2, :] * sc1_ref[...] + cw_ref[2:3, :] * u)
    xso_ref[...] = _silu(cwx_ref[0:1, :] * sx0_ref[...] + cwx_ref[1:2, :] * sx1_ref[...]
                         + cwx_ref[2:3, :] * sx2_ref[...] + cwx_ref[3:4, :] * xs_ref[...] + cbx_ref[...])
    bco_ref[...] = _silu(cwb_ref[0:1, :] * sb0_ref[...] + cwb_ref[1:2, :] * sb1_ref[...]
                         + cwb_ref[2:3, :] * sb2_ref[...] + cwb_ref[3:4, :] * bc_ref[...] + cbb_ref[...])
    dt = _softplus(dt_ref[...] + dtb_ref[...])
    dto_ref[...] = dt
    ea_ref[...] = jnp.exp(dt * (-jnp.exp(alog_ref[...])))


def _decmix(proj, st_conv, st_ssm_conv, conv_w, cw, cbias, dtb, alog):
    nb = proj.shape[0]
    pcol = lambda w, off: pl.BlockSpec((nb, w), lambda i: (0, off // w))
    full = lambda w: pl.BlockSpec((nb, w), lambda i: (0, 0))
    const = lambda r, w, j: pl.BlockSpec((r, w), lambda i: (0, j))
    sx = [st_ssm_conv[:, t, :512] for t in range(3)]
    sb = [st_ssm_conv[:, t, 512:] for t in range(3)]
    return pl.pallas_call(
        _decmix_kernel,
        grid=(1,),
        in_specs=[pcol(512, CB_OFF), pcol(512, CC_OFF), pcol(512, CX_OFF), pcol(512, XS_OFF), pcol(512, BC_OFF),
                  pcol(LANES, DT_OFF)] + [full(512)] * 8
                 + [const(3, 512, 0), const(4, 512, 0), const(4, 512, 1), const(1, 512, 0), const(1, 512, 1),
                    const(1, LANES, 0), const(1, LANES, 0)],
        out_specs=[full(512), full(512), full(512), full(512), full(LANES), full(LANES)],
        out_shape=[jax.ShapeDtypeStruct((nb, 512), F32)] * 4 + [jax.ShapeDtypeStruct((nb, LANES), F32)] * 2,
        compiler_params=_cparams(("arbitrary",)),
        name="decode_mix",
    )(proj, proj, proj, proj, proj, proj, st_conv[:, 0], st_conv[:, 1], *sx, *sb,
      conv_w, cw, cw, cbias, cbias, dtb, alog)


def _decssd_kernel(h_ref, dt_ref, xs_ref, b_ref, c_ref, ea_ref, dsk_ref, y_ref, ho_ref):
    xs = xs_ref[...]
    h_new = h_ref[...] * ea_ref[...] + (xs * dt_ref[...]) * b_ref[...]
    ho_ref[...] = h_new
    y_ref[...] = jnp.sum(h_new * c_ref[...], axis=-1, keepdims=True) + dsk_ref[...] * xs


def _decssd(h0, dt, xs, bh, ch, ea, dsk):
    nb = h0.shape[0]
    blk = lambda a, b: pl.BlockSpec((None, SSM_HEADS, a, b), lambda i: (i, 0, 0, 0))
    return pl.pallas_call(
        _decssd_kernel,
        grid=(nb,),
        in_specs=[blk(SSM_HEAD_DIM, SSM_STATE), blk(1, 1), blk(SSM_HEAD_DIM, 1), blk(1, SSM_STATE),
                  blk(1, SSM_STATE), blk(1, 1), pl.BlockSpec((SSM_HEADS, 1, 1), lambda i: (0, 0, 0))],
        out_specs=[blk(SSM_HEAD_DIM, 1), blk(SSM_HEAD_DIM, SSM_STATE)],
        out_shape=[jax.ShapeDtypeStruct((nb, SSM_HEADS, SSM_HEAD_DIM, 1), F32),
                   jax.ShapeDtypeStruct((nb, SSM_HEADS, SSM_HEAD_DIM, SSM_STATE), F32)],
        compiler_params=_cparams(("parallel",)),
        name="decode_ssd",
    )(h0, dt, xs, bh, ch, ea, dsk)


DECODE_PAGE_GROUP = 16


def _decscore_kernel(pt_ref, qi_ref, w_ref, *refs):
    del pt_ref
    kid_refs, o_ref = refs[:-1], refs[-1]
    kid = jnp.concatenate([r[...].astype(BF16) for r in kid_refs], axis=0)
    s = lax.dot_general(qi_ref[:, 0:IDX_DIM].astype(BF16), kid, NT_DIMS, preferred_element_type=F32)
    sc = jnp.maximum(s * (IDX_DIM ** -0.5), 0.0) * w_ref[:, 0:1]
    o_ref[...] = jnp.sum(sc, axis=0, keepdims=True)


def _page_specs(block, pg):
    return [pl.BlockSpec(block, lambda b, p, pt, t=t: (pt[b, p * pg + t], 0, 0)) for t in range(pg)]


def _decscore(page_table, qip, wrows, pool_kidx):
    nb, n_pages = page_table.shape
    pg = math.gcd(DECODE_PAGE_GROUP, n_pages)
    grid_spec = pltpu.PrefetchScalarGridSpec(
        num_scalar_prefetch=1,
        grid=(nb, n_pages // pg),
        in_specs=[pl.BlockSpec((None, N_IDX_HEADS, LANES), lambda b, p, pt: (b, 0, 0)),
                  pl.BlockSpec((None, N_IDX_HEADS, PAGE_SIZE), lambda b, p, pt: (b, 0, 0))]
                 + _page_specs((None, PAGE_SIZE, IDX_DIM), pg),
        out_specs=pl.BlockSpec((None, None, 1, pg * PAGE_SIZE), lambda b, p, pt: (b, p, 0, 0)),
    )
    out = pl.pallas_call(
        _decscore_kernel,
        grid_spec=grid_spec,
        out_shape=jax.ShapeDtypeStruct((nb, n_pages // pg, 1, pg * PAGE_SIZE), F32),
        compiler_params=_cparams(("parallel", "arbitrary")),
        name="decode_scores",
    )(page_table, qip, wrows, *([pool_kidx] * pg))
    return out.reshape(nb, n_pages * PAGE_SIZE)


def _decselect_kernel(sc_ref, qi_ref, ki_ref, w_ref, sel_ref, self_ref, keys_ref, *, k_sel, kc, nbits):
    nb, past = sc_ref.shape
    nck = past // kc
    lane_k = lax.broadcasted_iota(I32, (nb, kc), 1)
    ki = ki_ref[...].astype(F32)
    own = jnp.zeros((nb, 1), F32)
    for h in range(N_IDX_HEADS):
        s = jnp.sum(qi_ref[:, h * LANES:(h + 1) * LANES].astype(F32) * ki, axis=1, keepdims=True)
        own = own + jnp.maximum(s * (IDX_DIM ** -0.5), 0.0) * w_ref[:, h:h + 1]
    own_key = _sortable(own)

    def key_body(c, carry):
        off = pl.multiple_of(c * kc, kc)
        keys_ref[:, pl.ds(off, kc)] = _sortable(sc_ref[:, pl.ds(off, kc)])
        return carry

    lax.fori_loop(0, nck, key_body, 0)

    def count_where(pred):
        def body(c, acc):
            off = pl.multiple_of(c * kc, kc)
            f = jnp.where(pred(keys_ref[:, pl.ds(off, kc)], off), 1.0, 0.0)
            part = f[:, 0:LANES]
            for t in range(1, kc // LANES):
                part = part + f[:, t * LANES:(t + 1) * LANES]
            return acc + part

        acc = lax.fori_loop(0, nck, body, jnp.zeros((nb, LANES), F32))
        return jnp.sum(acc, axis=1, keepdims=True)

    def bit_body(i, t):
        cand = t + lax.shift_left(jnp.int32(1), 31 - i)
        cnt = count_where(lambda kk, off: kk >= cand) + jnp.where(own_key >= cand, 1.0, 0.0)
        return jnp.where(cnt >= k_sel, cand, t)

    thr = lax.fori_loop(0, 32, bit_body, jnp.full((nb, 1), INT_MIN, I32))
    thr = jnp.maximum(thr, KEY_NEG_INF)
    finite_thr = thr > KEY_NEG_INF
    cnt_gt = count_where(lambda kk, off: kk > thr) + jnp.where(own_key > thr, 1.0, 0.0)
    need = k_sel - cnt_gt

    def xbody(i, x):
        cand = x + lax.shift_left(jnp.int32(1), nbits - 1 - i)
        cnt = count_where(lambda kk, off: jnp.logical_and(kk == thr, off + lane_k < cand))
        return jnp.where(cnt < need, cand, x)

    xcut = lax.fori_loop(0, nbits, xbody, jnp.zeros((nb, 1), I32))
    xcut = jnp.where(finite_thr, xcut, -1)
    ties_past = count_where(lambda kk, off: kk == thr)
    own_tie = jnp.logical_and(jnp.logical_and(own_key == thr, finite_thr), ties_past < need)
    self_ref[...] = jnp.broadcast_to(
        jnp.where(jnp.logical_or(own_key > thr, own_tie), 1.0, 0.0), (nb, LANES))

    def sel_body(c, carry):
        off = pl.multiple_of(c * kc, kc)
        kk = keys_ref[:, pl.ds(off, kc)]
        sel = jnp.logical_or(kk > thr, jnp.logical_and(kk == thr, off + lane_k <= xcut))
        sel_ref[:, pl.ds(off, kc)] = jnp.where(sel, 1.0, 0.0)
        return carry

    lax.fori_loop(0, nck, sel_body, 0)


def _decselect(scores, qip, kib, wi):
    nb, past = scores.shape
    k_sel = min(TOPK_MAX, (past + 1) // 4)
    kc = 512 if past % 512 == 0 else PAGE_SIZE
    nbits = max(1, past.bit_length())
    full = lambda w: pl.BlockSpec((nb, w), lambda i: (0, 0))
    return pl.pallas_call(
        functools.partial(_decselect_kernel, k_sel=k_sel, kc=kc, nbits=nbits),
        grid=(1,),
        in_specs=[full(past), full(N_IDX_HEADS * LANES), full(LANES), full(LANES)],
        out_specs=[full(past), full(LANES)],
        out_shape=[jax.ShapeDtypeStruct((nb, past), F32), jax.ShapeDtypeStruct((nb, LANES), F32)],
        scratch_shapes=[pltpu.VMEM((nb, past), I32)],
        compiler_params=_cparams(("arbitrary",)),
        name="decode_select",
    )(scores, qip, kib, wi)


def _decattn_kernel(pt_ref, qp_ref, sel_ref, kn_ref, vn_ref, self_ref, *refs, n_steps):
    del pt_ref
    pg = (len(refs) - 4) // 2
    kp_refs, vp_refs = refs[:pg], refs[pg:2 * pg]
    o_ref, m_ref, l_ref, acc_ref = refs[2 * pg:]
    p = pl.program_id(1)

    @pl.when(p == 0)
    def _():
        m_ref[...] = jnp.full(m_ref.shape, NEG_BIG, F32)
        l_ref[...] = jnp.zeros(l_ref.shape, F32)
        acc_ref[...] = jnp.zeros(acc_ref.shape, F32)

    q = qp_ref[...]
    kcat = jnp.concatenate([r[...].astype(BF16) for r in kp_refs], axis=0)
    vcat = jnp.concatenate([r[...].astype(BF16) for r in vp_refs], axis=0)
    s = lax.dot_general(q.astype(BF16), kcat, NT_DIMS, preferred_element_type=F32)
    s = jnp.where(sel_ref[...] > 0.0, s, -jnp.inf)
    m_old = m_ref[:, 0:1]
    m_new = jnp.maximum(m_old, jnp.max(s, axis=1, keepdims=True))
    alpha = jnp.exp(m_old - m_new)
    pr = jnp.exp(s - m_new)
    l_new = alpha * l_ref[:, 0:1] + jnp.sum(pr, axis=1, keepdims=True)
    acc_ref[...] = alpha * acc_ref[...] + jnp.dot(pr.astype(BF16), vcat, preferred_element_type=F32)
    m_ref[...] = jnp.broadcast_to(m_new, m_ref.shape)
    l_ref[...] = jnp.broadcast_to(l_new, l_ref.shape)

    @pl.when(p == n_steps - 1)
    def _():
        s_own = jnp.sum(q * kn_ref[...], axis=1, keepdims=True)
        s_own = jnp.where(self_ref[:, 0:1] > 0.0, s_own, -jnp.inf)
        m_o = m_ref[:, 0:1]
        m_n = jnp.maximum(m_o, s_own)
        al = jnp.exp(m_o - m_n)
        p_own = jnp.exp(s_own - m_n)
        l_n = al * l_ref[:, 0:1] + p_own
        acc = al * acc_ref[...] + p_own.astype(BF16).astype(F32) * vn_ref[...]
        o_ref[...] = acc / l_n


def _decattn(page_table, qp, sel, pool_k, pool_v, kb, vb, self_sel):
    nb, n_pages = page_table.shape
    pg = math.gcd(DECODE_PAGE_GROUP, n_pages)
    n_steps = n_pages // pg
    row = lambda r: pl.BlockSpec((None, r, LANES), lambda b, p, pt: (b, 0, 0))
    pages = _page_specs((None, PAGE_SIZE, LANES), pg)
    grid_spec = pltpu.PrefetchScalarGridSpec(
        num_scalar_prefetch=1,
        grid=(nb, n_steps),
        in_specs=[row(N_HEADS),
                  pl.BlockSpec((None, None, 1, pg * PAGE_SIZE), lambda b, p, pt: (b, p, 0, 0)),
                  row(1), row(1), row(1)] + pages + pages,
        out_specs=row(N_HEADS),
        scratch_shapes=[pltpu.VMEM((N_HEADS, LANES), F32)] * 3,
    )
    return pl.pallas_call(
        functools.partial(_decattn_kernel, n_steps=n_steps),
        grid_spec=grid_spec,
        out_shape=jax.ShapeDtypeStruct((nb, N_HEADS, LANES), F32),
        compiler_params=_cparams(("parallel", "arbitrary")),
        name="decode_attend",
    )(page_table, qp, sel.reshape(nb, n_steps, 1, pg * PAGE_SIZE), kb, vb, self_sel,
      *([pool_k] * pg), *([pool_v] * pg))


def _pack_w_in(w):
    d = w.shape[0]
    pad = lambda n: jnp.zeros((d, n), w.dtype)
    cols = [w[:, 4172:7244],
            w[:, 0:2048],
            w[:, 2628:3140],
            w[:, 3140:4164],
            w[:, 2048:2304],
            w[:, 2304:2560],
            w[:, 2560:2628], pad(60),
            w[:, 4164:4172], pad(120),
            pad(N_PROJ - 7424)]
    return jnp.concatenate(cols, axis=1).astype(BF16)


def _pad_lanes(v):
    return jnp.zeros((1, LANES), F32).at[0, :v.shape[0]].set(v)


def _rope_tables(pos):
    half = HEAD_DIM // 2
    inv = ROPE_THETA ** (-jnp.arange(half, dtype=F32) / half)
    ang = pos.astype(F32)[:, None] * inv[None, :]
    cos, sin = jnp.cos(ang), jnp.sin(ang)
    cos2 = jnp.concatenate([cos, cos], axis=1)
    sin2 = jnp.concatenate([-sin, sin], axis=1)
    return jnp.tile(cos2, (1, 2)), jnp.tile(sin2, (1, 2))


def _head_halves(o):
    hpg = N_HEADS // N_KV_HEADS
    parts = [o[:, h, (h // hpg) * HEAD_DIM:(h // hpg + 1) * HEAD_DIM] for h in range(N_HEADS)]
    return jnp.concatenate(parts, axis=-1)


def kernel(x_prompt, x_sample, cache_k, cache_v, cache_kidx, state_conv, state_ssm_conv, state_ssm, page_table,
           norm1, w_in, b_gate, conv_w, q_norm, k_norm, w_pc, w_pa, ssm_conv_w, ssm_conv_b, dt_bias, a_log,
           d_skip, ssm_norm, w_ps, w_o, norm2, w_gate_dense, w_up_dense, w_down_dense, w_router, w_gate_moe,
           w_up_moe, w_down_moe):
    nb, seq, _ = x_prompt.shape
    db = x_sample.shape[0]
    depth = w_in.shape[0]
    n_pages = page_table.shape[1]
    past = n_pages * PAGE_SIZE
    n_pool = cache_k.shape[1]

    hp = x_prompt.reshape(nb * seq, D_MODEL)
    hs = x_sample.reshape(db, D_MODEL)
    cos_p, sin_p = _rope_tables(jnp.tile(jnp.arange(seq), nb))
    cos_s, sin_s = _rope_tables(jnp.full((db,), past))

    outs_p = [[] for _ in range(6)]
    outs_s = [[] for _ in range(6)]
    for l in range(depth):
        wp = _pack_w_in(w_in[l])
        g1 = norm1[l].reshape(1, D_MODEL)
        g2 = norm2[l].reshape(1, D_MODEL)
        qg = jnp.tile(q_norm[l], 2).reshape(1, LANES)
        kg = jnp.tile(k_norm[l], 2).reshape(1, LANES)
        cw = ssm_conv_w[l]
        cbias = ssm_conv_b[l].reshape(1, SSM_CONV_DIM)
        dtb = _pad_lanes(dt_bias[l])
        alog = _pad_lanes(a_log[l])
        dsk = jnp.repeat(d_skip[l], SSM_HEAD_DIM).reshape(1, SSM_INNER)
        sn = ssm_norm[l].reshape(1, SSM_INNER)
        wpc, wpa, wps, wo = (w.astype(BF16) for w in (w_pc[l], w_pa[l], w_ps[l], w_o[l]))
        i = l // 2
        if l % 2 == 0:
            routed = False
            wr = jnp.zeros((D_MODEL, LANES), F32)
            wg, wu, wd = (w[i:i + 1].astype(BF16) for w in (w_gate_dense, w_up_dense, w_down_dense))
        else:
            routed = True
            wr = jnp.zeros((D_MODEL, LANES), F32).at[:, :N_EXPERTS].set(w_router[i])
            wg, wu, wd = (w[i].astype(BF16) for w in (w_gate_moe, w_up_moe, w_down_moe))

        proj = _inproj(hp, g1, wp)
        qp, kf, kb, vb, qip, kif, kib, wi = _prep(proj, cos_p, sin_p, qg, kg)
        attn = _dsa_prompt(qp, qip, wi, kb, vb, kib, nb, seq)
        pre_a, conv_st = _conva_prompt(proj, conv_w[l], nb, seq)
        y_ssd, h_last = _ssd_prompt(proj, cw, cbias, dtb, alog, dsk, nb, seq)
        hp = _merge(hp, pre_a, attn, y_ssd, proj, sn, b_gate[l], wpc, wpa, wps, wo)
        hp = _ffn(hp, g2, wr, wg, wu, wd, routed)
        proj3 = proj.reshape(nb, seq, N_PROJ)
        outs_p[0].append(kf.reshape(nb, seq, N_KV_HEADS, HEAD_DIM))
        outs_p[1].append(proj3[:, :, V_OFF:V_OFF + LANES].reshape(nb, seq, N_KV_HEADS, HEAD_DIM))
        outs_p[2].append(kif[:, :IDX_DIM].reshape(nb, seq, IDX_DIM))
        outs_p[3].append(conv_st)
        outs_p[4].append(proj3[:, seq - 3:, XS_OFF:XS_OFF + SSM_CONV_DIM])
        outs_p[5].append(h_last.reshape(nb, SSM_HEADS, SSM_HEAD_DIM, SSM_STATE))

        proj = _inproj(hs, g1, wp)
        qp, kf, kb, vb, qip, kif, kib, wi = _prep(proj, cos_s, sin_s, qg, kg)
        wrows = jnp.broadcast_to(wi[:, :N_IDX_HEADS, None], (db, N_IDX_HEADS, PAGE_SIZE))
        scores = _decscore(page_table, qip.astype(F32).reshape(db, N_IDX_HEADS, LANES), wrows, cache_kidx[l])
        sel, self_sel = _decselect(scores, qip, kib, wi)
        o = _decattn(page_table, qp.astype(F32).reshape(db, N_HEADS, LANES), sel,
                     cache_k[l].reshape(n_pool, PAGE_SIZE, LANES), cache_v[l].reshape(n_pool, PAGE_SIZE, LANES),
                     kb.astype(F32).reshape(db, 1, LANES), vb.astype(F32).reshape(db, 1, LANES),
                     self_sel.reshape(db, 1, LANES))
        attn = _head_halves(o)
        pre_a, u, xs, bcv, dt, ea = _decmix(proj, state_conv[l], state_ssm_conv[l], conv_w[l], cw, cbias, dtb, alog)
        hpg = SSM_HEADS // SSM_GROUPS
        xs4 = xs.reshape(db, SSM_HEADS, SSM_HEAD_DIM, 1)
        bh = jnp.repeat(bcv[:, :SSM_GROUPS * SSM_STATE].reshape(db, SSM_GROUPS, 1, SSM_STATE), hpg, axis=1)
        ch = jnp.repeat(bcv[:, SSM_GROUPS * SSM_STATE:].reshape(db, SSM_GROUPS, 1, SSM_STATE), hpg, axis=1)
        y4, h_new = _decssd(state_ssm[l], dt[:, :SSM_HEADS, None, None], xs4, bh, ch, ea[:, :SSM_HEADS, None, None],
                            d_skip[l].reshape(SSM_HEADS, 1, 1))
        hs = _merge(hs, pre_a, attn, y4.reshape(db, SSM_INNER), proj, sn, b_gate[l], wpc, wpa, wps, wo)
        hs = _ffn(hs, g2, wr, wg, wu, wd, routed)
        outs_s[0].append(kf.reshape(db, 1, N_KV_HEADS, HEAD_DIM))
        outs_s[1].append(proj[:, V_OFF:V_OFF + LANES].reshape(db, 1, N_KV_HEADS, HEAD_DIM))
        outs_s[2].append(kif[:, :IDX_DIM].reshape(db, 1, IDX_DIM))
        outs_s[3].append(jnp.stack([state_conv[l][:, 1], u], axis=1))
        outs_s[4].append(jnp.concatenate(
            [state_ssm_conv[l][:, 1:], proj[:, None, XS_OFF:XS_OFF + SSM_CONV_DIM]], axis=1))
        outs_s[5].append(h_new)

    return (hp.reshape(nb, seq, D_MODEL), hs.reshape(db, 1, D_MODEL),
            *(jnp.stack(o) for o in outs_p), *(jnp.stack(o) for o in outs_s))
```

```python
import functools
import math

import jax
import jax.numpy as jnp
from jax import lax
from jax.experimental import pallas as pl
from jax.experimental.pallas import tpu as pltpu

F32 = jnp.float32
BF16 = jnp.bfloat16
I32 = jnp.int32

D_MODEL = 1024
D_CONV = 512
N_HEADS = 8
HEAD_DIM = 64
N_KV_HEADS = 2
N_IDX_HEADS = 4
IDX_DIM = 64
TOPK_MAX = 256
Q_BLOCK = 128
ROPE_THETA = 10000.0
SSM_INNER = 512
SSM_HEADS = 8
SSM_HEAD_DIM = 64
SSM_GROUPS = 2
SSM_STATE = 128
SSM_CHUNK = 128
SSM_CONV_DIM = 1024
PAGE_SIZE = 128
N_EXPERTS = 8
EPS = 1e-6

LANES = 128
SUBLANES = 8
VMEM_LIMIT = 52 * 1024 * 1024

G_OFF, CB_OFF, CC_OFF, CX_OFF, Q_OFF, Z_OFF, XS_OFF, BC_OFF = 0, 3072, 3584, 4096, 4608, 5120, 5632, 6144
K_OFF, V_OFF, QI_OFF, KW_OFF, DT_OFF = 6656, 6784, 6912, 7168, 7296
N_PROJ = 7680

INT_MIN = -2 ** 31
INT_MAX = 2 ** 31 - 1
KEY_NEG_INF = (-8388608) ^ 0x7FFFFFFF
NEG_BIG = -1e30
LOG2_E = 1.4426950408889634

NT_DIMS = (((1,), (1,)), ((), ()))


def _cparams(sem):
    return pltpu.CompilerParams(dimension_semantics=sem, vmem_limit_bytes=VMEM_LIMIT)


def _sigmoid(x):
    return 1.0 / (1.0 + jnp.exp(-x))


def _silu(x):
    return x * _sigmoid(x)


def _softplus(x):
    return jnp.maximum(x, 0.0) + jnp.log1p(jnp.exp(-jnp.abs(x)))


def _sortable(x):
    bits = pltpu.bitcast(x, I32)
    return bits ^ (lax.shift_right_arithmetic(bits, 31) & 0x7FFFFFFF)


def _rms(x, g):
    return x * lax.rsqrt(jnp.mean(x * x, axis=-1, keepdims=True) + EPS) * g


def _inproj_kernel(x_ref, g_ref, w_ref, o_ref, h_ref):
    @pl.when(pl.program_id(1) == 0)
    def _():
        h_ref[...] = _rms(x_ref[...], g_ref[...]).astype(BF16)

    o_ref[...] = jnp.dot(h_ref[...], w_ref[...], preferred_element_type=F32)


def _inproj(x, g, w):
    m = x.shape[0]
    tm = min(m, 1024)
    tn = 1536
    return pl.pallas_call(
        _inproj_kernel,
        grid=(m // tm, N_PROJ // tn),
        in_specs=[pl.BlockSpec((tm, D_MODEL), lambda i, j: (i, 0)),
                  pl.BlockSpec((1, D_MODEL), lambda i, j: (0, 0)),
                  pl.BlockSpec((D_MODEL, tn), lambda i, j: (0, j))],
        out_specs=pl.BlockSpec((tm, tn), lambda i, j: (i, j)),
        out_shape=jax.ShapeDtypeStruct((m, N_PROJ), F32),
        scratch_shapes=[pltpu.VMEM((tm, D_MODEL), BF16)],
        compiler_params=_cparams(("parallel", "arbitrary")),
        name="inproj",
    )(x, g, w)


V_T_ROWS = LANES + 16


def _prep_kernel(q_ref, k_ref, v_ref, qi_ref, kw_ref, cos_ref, sin_ref, qg_ref, kg_ref,
                 qp_ref, kf_ref, kb_ref, vb_ref, qip_ref, kif_ref, kib_ref, wo_ref, *, transposed):
    tm = cos_ref.shape[0]

    def put(ref, h, val):
        if transposed:
            ref[h * LANES:(h + 1) * LANES, :] = val.T.astype(BF16)
        else:
            ref[:, h * LANES:(h + 1) * LANES] = val.astype(BF16)

    cos = cos_ref[...]
    sin = sin_ref[...]
    lane = lax.broadcasted_iota(I32, (tm, LANES), 1)
    low_half = lane < HEAD_DIM
    first_rot = (lane % HEAD_DIM) < (HEAD_DIM // 2)
    r = lax.broadcasted_iota(I32, (LANES, LANES), 0) // HEAD_DIM
    c = lax.broadcasted_iota(I32, (LANES, LANES), 1) // HEAD_DIM
    seg = jnp.where(r == c, 1.0 / HEAD_DIM, 0.0).astype(BF16)

    def rope(x):
        fwd = pltpu.roll(x, LANES - HEAD_DIM // 2, 1)
        bwd = pltpu.roll(x, HEAD_DIM // 2, 1)
        return x * cos + jnp.where(first_rot, fwd, bwd) * sin

    def headnorm(x, g):
        s = x * x
        hi = s.astype(BF16)
        lo = (s - hi.astype(F32)).astype(BF16)
        ms = (jnp.dot(hi, seg, preferred_element_type=F32)
              + jnp.dot(lo, seg, preferred_element_type=F32))
        return x * lax.rsqrt(ms + EPS) * g

    q_scale = HEAD_DIM ** -0.5 * (LOG2_E if transposed else 1.0)
    qg = qg_ref[...]
    for s in range(N_HEADS // 2):
        slab = rope(headnorm(q_ref[:, s * LANES:(s + 1) * LANES], qg)) * q_scale
        swapped = pltpu.roll(slab, HEAD_DIM, 1)
        grp = (2 * s) // (N_HEADS // N_KV_HEADS)
        for hh in range(2):
            h = 2 * s + hh
            src = slab if hh == grp else swapped
            keep = low_half if grp == 0 else jnp.logical_not(low_half)
            put(qp_ref, h, jnp.where(keep, src, 0.0))

    k = rope(headnorm(k_ref[...], kg_ref[...]))
    kf_ref[...] = k
    kb_ref[...] = k.astype(BF16)
    if transposed:
        vb_ref[0:LANES, :] = v_ref[...].T.astype(BF16)
        vb_ref[LANES:V_T_ROWS, :] = jnp.ones((V_T_ROWS - LANES, tm), BF16)
    else:
        vb_ref[...] = v_ref[...].astype(BF16)

    for s in range(N_IDX_HEADS // 2):
        slab = rope(qi_ref[:, s * LANES:(s + 1) * LANES]) * (IDX_DIM ** -0.5)
        swapped = pltpu.roll(slab, HEAD_DIM, 1)
        for hh in range(2):
            h = 2 * s + hh
            src = slab if hh == 0 else swapped
            put(qip_ref, h, jnp.where(low_half, src, 0.0))

    kw = kw_ref[...]
    ki = rope(kw)
    kif_ref[...] = ki
    kib_ref[...] = jnp.where(low_half, ki, 0.0).astype(BF16)
    wi = pltpu.roll(kw, HEAD_DIM, 1) * (N_IDX_HEADS ** -0.5)
    if transposed:
        wo_ref[...] = wi.T[0:SUBLANES, :]
    else:
        wo_ref[...] = wi


def _prep(proj, cos, sin, qg, kg, transposed):
    m = proj.shape[0]
    tm = min(m, 512)
    row = lambda w, off: pl.BlockSpec((tm, w), lambda i: (i, off // w))
    full = lambda w: pl.BlockSpec((tm, w), lambda i: (i, 0))
    const = pl.BlockSpec((1, LANES), lambda i: (0, 0))
    if transposed:
        feat = lambda r: pl.BlockSpec((r, tm), lambda i: (0, i))
        fshape = lambda r, dt: jax.ShapeDtypeStruct((r, m), dt)
    else:
        feat = full
        fshape = lambda r, dt: jax.ShapeDtypeStruct((m, r), dt)
    return pl.pallas_call(
        functools.partial(_prep_kernel, transposed=transposed),
        grid=(m // tm,),
        in_specs=[row(512, Q_OFF), row(LANES, K_OFF), row(LANES, V_OFF), row(256, QI_OFF), row(LANES, KW_OFF),
                  full(LANES), full(LANES), const, const],
        out_specs=[feat(N_HEADS * LANES), full(LANES), full(LANES), feat(V_T_ROWS if transposed else LANES),
                   feat(N_IDX_HEADS * LANES), full(LANES), full(LANES), feat(SUBLANES if transposed else LANES)],
        out_shape=[fshape(N_HEADS * LANES, BF16),
                   jax.ShapeDtypeStruct((m, LANES), F32),
                   jax.ShapeDtypeStruct((m, LANES), BF16),
                   fshape(V_T_ROWS if transposed else LANES, BF16),
                   fshape(N_IDX_HEADS * LANES, BF16),
                   jax.ShapeDtypeStruct((m, LANES), F32),
                   jax.ShapeDtypeStruct((m, LANES), BF16),
                   fshape(SUBLANES if transposed else LANES, F32)],
        compiler_params=_cparams(("parallel",)),
        name="prep",
    )(proj, proj, proj, proj, proj, cos, sin, qg, kg)


def _fold_rows(f):
    while f.shape[0] > SUBLANES:
        half = f.shape[0] // 2
        f = f[:half] + f[half:]
    return f


def _dsa_kernel(qt_ref, qit_ref, wt_ref, k_ref, vt_ref, ki_ref, o_ref,
                keys_ref, x_ref, m_ref, kmax_ref, acc_ref, *, k_sel, kc, nbits):
    qb = Q_BLOCK
    j = pl.program_id(1)
    nck = (j * qb + qb + kc - 1) // kc
    qpos = j * qb + lax.broadcasted_iota(I32, (1, qb), 1)
    sub_k = lax.broadcasted_iota(I32, (kc, qb), 0)
    pair = lambda ref, s: jnp.concatenate(
        [ref[(2 * s) * LANES:(2 * s + 1) * LANES, :], ref[(2 * s + 1) * LANES:(2 * s + 2) * LANES, :]], axis=1)
    qi2 = [pair(qit_ref, s) for s in range(N_IDX_HEADS // 2)]
    q2 = [pair(qt_ref, s) for s in range(N_HEADS // 2)]

    def score_body(c, carry):
        off = pl.multiple_of(c * kc, kc)
        kic = ki_ref[pl.ds(off, kc), :]
        acc = jnp.zeros((kc, qb), F32)
        for s in range(N_IDX_HEADS // 2):
            s2 = jnp.dot(kic, qi2[s], preferred_element_type=F32)
            for hh in range(2):
                h = 2 * s + hh
                acc = acc + jnp.maximum(s2[:, hh * qb:(hh + 1) * qb], 0.0) * wt_ref[h:h + 1, :]
        acc = jnp.where(off + sub_k <= qpos, acc, -jnp.inf)
        keys_ref[pl.ds(off, kc), :] = _sortable(acc)
        return carry

    lax.fori_loop(0, nck, score_body, 0)

    def count_where(pred):
        def body(c, acc):
            off = pl.multiple_of(c * kc, kc)
            return acc + _fold_rows(jnp.where(pred(keys_ref[pl.ds(off, kc), :], off), 1.0, 0.0))

        acc = lax.fori_loop(0, nck, body, jnp.zeros((SUBLANES, qb), F32))
        return jnp.sum(acc, axis=0, keepdims=True)

    def bit_body(i, t):
        cand = t + lax.shift_left(jnp.int32(1), 31 - i)
        cnt = count_where(lambda kk, off: kk >= cand)
        return jnp.where(cnt >= k_sel, cand, t)

    thr = lax.fori_loop(0, 32, bit_body, jnp.full((1, qb), INT_MIN, I32))
    thr = jnp.maximum(thr, KEY_NEG_INF)
    finite_thr = thr > KEY_NEG_INF
    cnt_ge = count_where(lambda kk, off: kk >= thr)
    cnt_gt = count_where(lambda kk, off: kk > thr)
    need = k_sel - cnt_gt
    tie = jnp.logical_and(cnt_ge > k_sel, finite_thr)
    x_default = jnp.where(finite_thr, INT_MAX, -1)
    x_ref[...] = jnp.broadcast_to(x_default, x_ref.shape)

    @pl.when(jnp.max(jnp.where(tie, 1.0, 0.0)) > 0.0)
    def _():
        def xbody(i, x):
            cand = x + lax.shift_left(jnp.int32(1), nbits - 1 - i)
            cnt = count_where(lambda kk, off: jnp.logical_and(kk == thr, off + sub_k < cand))
            return jnp.where(cnt < need, cand, x)

        x = lax.fori_loop(0, nbits, xbody, jnp.zeros((1, qb), I32))
        x_ref[...] = jnp.broadcast_to(jnp.where(tie, x, x_default), x_ref.shape)

    xcut = x_ref[0:1, :]
    hpg = N_HEADS // N_KV_HEADS

    @pl.when(j == 0)
    def _():
        r = lax.broadcasted_iota(I32, (LANES, LANES), 0) // HEAD_DIM
        cc = lax.broadcasted_iota(I32, (LANES, LANES), 1) // HEAD_DIM
        seg = jnp.where(r == cc, 1.0, 0.0).astype(BF16)

        def body(c, mx):
            kf = k_ref[pl.ds(pl.multiple_of(c * kc, kc), kc), :].astype(F32)
            gs = jnp.dot((kf * kf).astype(BF16), seg, preferred_element_type=F32)
            while gs.shape[0] > SUBLANES:
                half = gs.shape[0] // 2
                gs = jnp.maximum(gs[:half], gs[half:])
            return jnp.maximum(mx, gs)

        mx = lax.fori_loop(0, k_ref.shape[0] // kc, body, jnp.zeros((SUBLANES, LANES), F32))
        mx = jnp.max(mx, axis=0, keepdims=True)
        lane = lax.broadcasted_iota(I32, (1, LANES), 1)
        other = pltpu.roll(mx, HEAD_DIM, 1)
        kmax_ref[0:1, :] = jnp.where(lane < HEAD_DIM, mx, other)
        kmax_ref[1:2, :] = jnp.where(lane < HEAD_DIM, other, mx)

    def chunk(c):
        off = pl.multiple_of(c * kc, kc)
        kk = keys_ref[pl.ds(off, kc), :]
        sel = jnp.logical_or(kk > thr, jnp.logical_and(kk == thr, off + sub_k <= xcut))
        return sel, k_ref[pl.ds(off, kc), :], vt_ref[:, pl.ds(off, kc)]

    bound = []
    for h in range(N_HEADS):
        qh = qt_ref[h * LANES:(h + 1) * LANES, :].astype(F32)
        qsq = jnp.sum(qh * qh, axis=0, keepdims=True)
        bound.append(jnp.sqrt(qsq * kmax_ref[h // hpg:h // hpg + 1, :]))
    acc_ref[...] = jnp.zeros(acc_ref.shape, F32)

    def fast_body(c, carry):
        sel, kch, vch = chunk(c)
        n_pair = N_HEADS // 2
        s2 = jnp.dot(kch, q2[0], preferred_element_type=F32)
        for s in range(n_pair):
            s2_next = jnp.dot(kch, q2[s + 1], preferred_element_type=F32) if s + 1 < n_pair else None
            ps = [jnp.exp2(jnp.where(sel, s2[:, hh * qb:(hh + 1) * qb] - bound[2 * s + hh], -jnp.inf)).astype(BF16)
                  for hh in range(2)]
            acc_ref[s] += jnp.dot(vch, jnp.concatenate(ps, axis=1), preferred_element_type=F32)
            s2 = s2_next
        return carry

    lax.fori_loop(0, nck, fast_body, 0)

    dens = jnp.concatenate([acc_ref[s, LANES:LANES + 1, :] for s in range(N_HEADS // 2)], axis=0)

    @pl.when(jnp.logical_not(jnp.min(dens) >= 1e-20))
    def _():
        m_ref[...] = jnp.full(m_ref.shape, NEG_BIG, F32)
        acc_ref[...] = jnp.zeros(acc_ref.shape, F32)

        def exact_body(c, carry):
            sel, kch, vch = chunk(c)
            for s in range(N_HEADS // 2):
                s2 = jnp.dot(kch, q2[s], preferred_element_type=F32)
                ps, alphas = [], []
                for hh in range(2):
                    h = 2 * s + hh
                    sh = jnp.where(sel, s2[:, hh * qb:(hh + 1) * qb], -jnp.inf)
                    m_old = m_ref[h:h + 1, :]
                    m_new = jnp.maximum(m_old, jnp.max(sh, axis=0, keepdims=True))
                    m_ref[h:h + 1, :] = m_new
                    alphas.append(jnp.exp2(m_old - m_new))
                    ps.append(jnp.exp2(sh - m_new).astype(BF16))
                o2 = jnp.dot(vch, jnp.concatenate(ps, axis=1), preferred_element_type=F32)
                acc_ref[s] = jnp.concatenate(alphas, axis=1) * acc_ref[s] + o2
            return carry

        lax.fori_loop(0, nck, exact_body, 0)

    for s in range(N_HEADS // 2):
        g = (2 * s) // hpg
        num = acc_ref[s, g * HEAD_DIM:(g + 1) * HEAD_DIM, :]
        out2 = num / acc_ref[s, LANES:LANES + 1, :]
        o_ref[:, s * LANES:(s + 1) * LANES] = jnp.concatenate([out2[:, :qb], out2[:, qb:]], axis=0).T


def _dsa_prompt(qt, qit, wt, kb, vt, kib, nb, seq):
    k_sel = min(TOPK_MAX, seq // 4)
    kc = 512 if seq % 512 == 0 else Q_BLOCK
    nbits = max(1, (seq - 1).bit_length())
    nq = seq // Q_BLOCK
    r3 = lambda a: a.reshape(nb, seq, a.shape[-1])
    qcol = lambda r: pl.BlockSpec((r, Q_BLOCK), lambda b, j: (0, b * nq + j))
    seqblk = pl.BlockSpec((None, seq, LANES), lambda b, j: (b, 0, 0))
    out = pl.pallas_call(
        functools.partial(_dsa_kernel, k_sel=k_sel, kc=kc, nbits=nbits),
        grid=(nb, nq),
        in_specs=[qcol(N_HEADS * LANES), qcol(N_IDX_HEADS * LANES), qcol(SUBLANES), seqblk,
                  pl.BlockSpec((V_T_ROWS, seq), lambda b, j: (0, b)), seqblk],
        out_specs=pl.BlockSpec((None, Q_BLOCK, N_HEADS * HEAD_DIM), lambda b, j: (b, j, 0)),
        out_shape=jax.ShapeDtypeStruct((nb, seq, N_HEADS * HEAD_DIM), F32),
        scratch_shapes=[pltpu.VMEM((seq, Q_BLOCK), I32),
                        pltpu.VMEM((SUBLANES, Q_BLOCK), I32),
                        pltpu.VMEM((N_HEADS, Q_BLOCK), F32),
                        pltpu.VMEM((SUBLANES, LANES), F32),
                        pltpu.VMEM((N_HEADS // 2, V_T_ROWS, 2 * Q_BLOCK), F32)],
        compiler_params=_cparams(("arbitrary", "arbitrary")),
        name="dsa_prompt",
    )(qt, qit, wt, r3(kb), vt, r3(kib))
    return out.reshape(nb * seq, N_HEADS * HEAD_DIM)


def _conva_kernel(cb_ref, cc_ref, cx_ref, cch_ref, cxh_ref, w_ref, o_ref, st_ref, *, tiles_per_seq):
    tm = cb_ref.shape[0]
    first = (pl.program_id(0) % tiles_per_seq) == 0
    u = cc_ref[...] * cx_ref[...]
    uh = jnp.where(first, 0.0, cch_ref[...] * cxh_ref[...])
    ext = jnp.concatenate([uh, u], axis=0)
    conv = (w_ref[2:3, :] * u + w_ref[1:2, :] * ext[SUBLANES - 1:SUBLANES - 1 + tm]
            + w_ref[0:1, :] * ext[SUBLANES - 2:SUBLANES - 2 + tm])
    o_ref[...] = cb_ref[...] * conv
    st_ref[...] = u[tm - 2:tm, :]


def _conva_prompt(proj, conv_w, nb, seq):
    m = proj.shape[0]
    tm = min(seq, 512)
    tps = seq // tm
    col = lambda off: pl.BlockSpec((tm, D_CONV), lambda i: (i, off // D_CONV))
    halo = lambda off: pl.BlockSpec(
        (SUBLANES, D_CONV), lambda i: (jnp.maximum(i * (tm // SUBLANES) - 1, 0), off // D_CONV))
    return pl.pallas_call(
        functools.partial(_conva_kernel, tiles_per_seq=tps),
        grid=(m // tm,),
        in_specs=[col(CB_OFF), col(CC_OFF), col(CX_OFF), halo(CC_OFF), halo(CX_OFF),
                  pl.BlockSpec((3, D_CONV), lambda i: (0, 0))],
        out_specs=[pl.BlockSpec((tm, D_CONV), lambda i: (i, 0)),
                   pl.BlockSpec((None, 2, D_CONV), lambda i: (i // tps, 0, 0))],
        out_shape=[jax.ShapeDtypeStruct((m, D_CONV), F32),
                   jax.ShapeDtypeStruct((nb, 2, D_CONV), F32)],
        compiler_params=_cparams(("arbitrary",)),
        name="conva_prompt",
    )(proj, proj, proj, proj, proj, conv_w)


def _ssd_kernel(xs_ref, bc_ref, dt_ref, xh_ref, bh_ref, cwx_ref, cwb_ref, cbx_ref, cbb_ref,
                dtb_ref, alog_ref, dsk_ref, y_ref, hout_ref, h_ref, *, nchunk):
    cl = SSM_CHUNK
    c = pl.program_id(1)
    first = c == 0

    @pl.when(first)
    def _():
        h_ref[...] = jnp.zeros(h_ref.shape, F32)

    def conv(cur, halo, w_ref, b_ref):
        ext = jnp.concatenate([jnp.where(first, 0.0, halo), cur], axis=0)
        out = (w_ref[3:4, :] * cur + w_ref[2:3, :] * ext[SUBLANES - 1:SUBLANES - 1 + cl]
               + w_ref[1:2, :] * ext[SUBLANES - 2:SUBLANES - 2 + cl]
               + w_ref[0:1, :] * ext[SUBLANES - 3:SUBLANES - 3 + cl] + b_ref[...])
        return _silu(out)

    xs = conv(xs_ref[...], xh_ref[...], cwx_ref, cbx_ref)
    bc = conv(bc_ref[...], bh_ref[...], cwb_ref, cbb_ref)
    dt = _softplus(dt_ref[...] + dtb_ref[...])
    a = dt * (-jnp.exp(alog_ref[...]))
    ri = lax.broadcasted_iota(I32, (cl, cl), 0)
    ci = lax.broadcasted_iota(I32, (cl, cl), 1)
    causal = ri >= ci
    cs = jnp.dot(jnp.where(causal, 1.0, 0.0), a, preferred_element_type=F32,
                 precision=lax.Precision.HIGHEST)
    cs_t = cs.T
    lane = lax.broadcasted_iota(I32, (cl, LANES), 1)
    lo = lane < SSM_HEAD_DIM
    rows_lo = lax.broadcasted_iota(I32, (LANES, 1), 0) < SSM_HEAD_DIM
    heads_per_group = SSM_HEADS // SSM_GROUPS

    cb = []
    for g in range(SSM_GROUPS):
        bg = bc[:, g * SSM_STATE:(g + 1) * SSM_STATE].astype(BF16)
        cg = bc[:, (SSM_GROUPS + g) * SSM_STATE:(SSM_GROUPS + g + 1) * SSM_STATE].astype(BF16)
        cb.append((bg, cg, lax.dot_general(cg, bg, NT_DIMS, preferred_element_type=F32)))

    for s in range(SSM_HEADS // 2):
        h0, h1 = 2 * s, 2 * s + 1
        bg, cg, cbg = cb[h0 // heads_per_group]
        sl = slice(s * LANES, (s + 1) * LANES)
        xs_s = xs[:, sl]
        col0, col1 = cs[:, h0:h0 + 1], cs[:, h1:h1 + 1]
        last0, last1 = cs[cl - 1:cl, h0:h0 + 1], cs[cl - 1:cl, h1:h1 + 1]
        xdt = xs_s * jnp.where(lo, dt[:, h0:h0 + 1], dt[:, h1:h1 + 1])
        xdt_b = xdt.astype(BF16)
        m0 = (cbg * jnp.where(causal, jnp.exp(col0 - cs_t[h0:h0 + 1, :]), 0.0)).astype(BF16)
        m1 = (cbg * jnp.where(causal, jnp.exp(col1 - cs_t[h1:h1 + 1, :]), 0.0)).astype(BF16)
        y_diag = jnp.where(lo, jnp.dot(m0, xdt_b, preferred_element_type=F32),
                           jnp.dot(m1, xdt_b, preferred_element_type=F32))
        hs = h_ref[sl, :]
        y_off = lax.dot_general(cg, hs.astype(BF16), NT_DIMS, preferred_element_type=F32)
        y_off = y_off * jnp.where(lo, jnp.exp(col0), jnp.exp(col1))
        y_ref[:, sl] = y_diag + y_off + dsk_ref[:, sl] * xs_s
        xw = xdt * jnp.where(lo, jnp.exp(last0 - col0), jnp.exp(last1 - col1))
        st = jnp.dot(xw.T.astype(BF16), bg, preferred_element_type=F32)
        h_ref[sl, :] = hs * jnp.where(rows_lo, jnp.exp(last0), jnp.exp(last1)) + st

    @pl.when(c == nchunk - 1)
    def _():
        hout_ref[...] = h_ref[...]


def _ssd_prompt(proj, cw, cbias, dtb, alog, dsk, nb, seq):
    m = proj.shape[0]
    cl = SSM_CHUNK
    nchunk = seq // cl
    blk = lambda w, off: pl.BlockSpec((cl, w), lambda b, c: (b * nchunk + c, off // w))
    halo = lambda off: pl.BlockSpec(
        (SUBLANES, 512), lambda b, c: (jnp.maximum((b * nchunk + c) * (cl // SUBLANES) - 1, 0), off // 512))
    const = lambda r, w, j: pl.BlockSpec((r, w), lambda b, c: (0, j))
    y, hout = pl.pallas_call(
        functools.partial(_ssd_kernel, nchunk=nchunk),
        grid=(nb, nchunk),
        in_specs=[blk(512, XS_OFF), blk(512, BC_OFF), blk(LANES, DT_OFF), halo(XS_OFF), halo(BC_OFF),
                  const(4, 512, 0), const(4, 512, 1), const(1, 512, 0), const(1, 512, 1),
                  const(1, LANES, 0), const(1, LANES, 0), const(1, 512, 0)],
        out_specs=[pl.BlockSpec((cl, SSM_INNER), lambda b, c: (b * nchunk + c, 0)),
                   pl.BlockSpec((None, SSM_INNER, SSM_STATE), lambda b, c: (b, 0, 0))],
        out_shape=[jax.ShapeDtypeStruct((m, SSM_INNER), F32),
                   jax.ShapeDtypeStruct((nb, SSM_INNER, SSM_STATE), F32)],
        scratch_shapes=[pltpu.VMEM((SSM_INNER, SSM_STATE), F32)],
        compiler_params=_cparams(("parallel", "arbitrary")),
        name="ssd_prompt",
    )(proj, proj, proj, proj, proj, cw, cw, cbias, cbias, dtb, alog, dsk)
    return y, hout


def _merge_kernel(x_ref, pa_ref, at_ref, ys_ref, z_ref, g0_ref, g1_ref, g2_ref, sn_ref, bg_ref,
                  wpc_ref, wpa_ref, wps_ref, wo_ref, o_ref):
    ssd = _rms(ys_ref[...] * _silu(z_ref[...]), sn_ref[...])
    ya = jnp.dot(pa_ref[...].astype(BF16), wpc_ref[...], preferred_element_type=F32)
    yb = jnp.dot(at_ref[...].astype(BF16), wpa_ref[...], preferred_element_type=F32)
    yc = jnp.dot(ssd.astype(BF16), wps_ref[...], preferred_element_type=F32)
    merged = (_sigmoid(g0_ref[...] + bg_ref[0:1, :]) * ya + _sigmoid(g1_ref[...] + bg_ref[1:2, :]) * yb
              + _sigmoid(g2_ref[...] + bg_ref[2:3, :]) * yc)
    o_ref[...] = x_ref[...] + jnp.dot(merged.astype(BF16), wo_ref[...], preferred_element_type=F32)


def _merge(x, pre_a, attn, y_ssd, proj, ssm_norm, b_gate, w_pc, w_pa, w_ps, w_o):
    m = x.shape[0]
    tm = min(m, 256)
    row = lambda w: pl.BlockSpec((tm, w), lambda i: (i, 0))
    pcol = lambda w, off: pl.BlockSpec((tm, w), lambda i: (i, off // w))
    const = lambda r, w: pl.BlockSpec((r, w), lambda i: (0, 0))
    return pl.pallas_call(
        _merge_kernel,
        grid=(m // tm,),
        in_specs=[row(D_MODEL), row(512), row(512), row(512), pcol(512, Z_OFF),
                  pcol(D_MODEL, G_OFF), pcol(D_MODEL, G_OFF + D_MODEL), pcol(D_MODEL, G_OFF + 2 * D_MODEL),
                  const(1, 512), const(3, D_MODEL),
                  const(512, D_MODEL), const(512, D_MODEL), const(512, D_MODEL), const(D_MODEL, D_MODEL)],
        out_specs=row(D_MODEL),
        out_shape=jax.ShapeDtypeStruct((m, D_MODEL), F32),
        compiler_params=_cparams(("parallel",)),
        name="merge",
    )(x, pre_a, attn, y_ssd, proj, proj, proj, proj, ssm_norm, b_gate, w_pc, w_pa, w_ps, w_o)


def _ffn_kernel(x_ref, g_ref, wr_ref, wg_ref, wu_ref, wd_ref, o_ref, h_ref, acc_ref, comb_ref,
                *, routed, n_e, n_f):
    e = pl.program_id(1)
    j = pl.program_id(2)
    tm = x_ref.shape[0]

    @pl.when(jnp.logical_and(e == 0, j == 0))
    def _():
        hf = _rms(x_ref[...], g_ref[...])
        h_ref[...] = hf.astype(BF16)
        acc_ref[...] = jnp.zeros(acc_ref.shape, F32)
        if routed:
            lane = lax.broadcasted_iota(I32, (tm, LANES), 1).astype(F32)
            logits = jnp.dot(hf, wr_ref[...], preferred_element_type=F32, precision=lax.Precision.HIGHEST)
            logits = jnp.where(lane < n_e, logits, -jnp.inf)
            m1 = jnp.max(logits, axis=1, keepdims=True)
            i1 = jnp.min(jnp.where(logits == m1, lane, float(LANES)), axis=1, keepdims=True)
            rest = jnp.where(lane == i1, -jnp.inf, logits)
            m2 = jnp.max(rest, axis=1, keepdims=True)
            i2 = jnp.min(jnp.where(rest == m2, lane, float(LANES)), axis=1, keepdims=True)
            e2 = jnp.exp(m2 - m1)
            den = 1.0 + e2
            comb_ref[...] = jnp.where(lane == i1, 1.0 / den, 0.0) + jnp.where(lane == i2, e2 / den, 0.0)

    h = h_ref[...]
    gate = jnp.dot(h, wg_ref[...], preferred_element_type=F32)
    up = jnp.dot(h, wu_ref[...], preferred_element_type=F32)
    act = _silu(gate) * up
    if routed:
        lane = lax.broadcasted_iota(I32, (tm, LANES), 1)
        act = act * jnp.sum(jnp.where(lane == e, comb_ref[...], 0.0), axis=1, keepdims=True)
    acc_ref[...] += jnp.dot(act.astype(BF16), wd_ref[...], preferred_element_type=F32)

    @pl.when(jnp.logical_and(e == n_e - 1, j == n_f - 1))
    def _():
        o_ref[...] = x_ref[...] + acc_ref[...]


def _ffn(x, g, w_router, w_gate, w_up, w_down, routed):
    m = x.shape[0]
    n_e, _, d_ff = w_gate.shape
    tm = min(m, 1024)
    tf = 256
    n_f = d_ff // tf
    return pl.pallas_call(
        functools.partial(_ffn_kernel, routed=routed, n_e=n_e, n_f=n_f),
        grid=(m // tm, n_e, n_f),
        in_specs=[pl.BlockSpec((tm, D_MODEL), lambda i, e, j: (i, 0)),
                  pl.BlockSpec((1, D_MODEL), lambda i, e, j: (0, 0)),
                  pl.BlockSpec((D_MODEL, LANES), lambda i, e, j: (0, 0)),
                  pl.BlockSpec((None, D_MODEL, tf), lambda i, e, j: (e, 0, j)),
                  pl.BlockSpec((None, D_MODEL, tf), lambda i, e, j: (e, 0, j)),
                  pl.BlockSpec((None, tf, D_MODEL), lambda i, e, j: (e, j, 0))],
        out_specs=pl.BlockSpec((tm, D_MODEL), lambda i, e, j: (i, 0)),
        out_shape=jax.ShapeDtypeStruct((m, D_MODEL), F32),
        scratch_shapes=[pltpu.VMEM((tm, D_MODEL), BF16), pltpu.VMEM((tm, D_MODEL), F32),
                        pltpu.VMEM((tm, LANES), F32)],
        compiler_params=_cparams(("parallel", "arbitrary", "arbitrary")),
        name="moe" if routed else "ffn",
    )(x, g, w_router, w_gate, w_up, w_down)


def _decmix_kernel(cb_ref, cc_ref, cx_ref, xs_ref, bc_ref, dt_ref, sc0_ref, sc1_ref,
                   sx0_ref, sx1_ref, sx2_ref, sb0_ref, sb1_ref, sb2_ref,
                   cw_ref, cwx_ref, cwb_ref, cbx_ref, cbb_ref, dtb_ref, alog_ref,
                   pa_ref, u_ref, xso_ref, bco_ref, dto_ref, ea_ref):
    u = cc_ref[...] * cx_ref[...]
    u_ref[...] = u
    pa_ref[...] = cb_ref[...] * (cw_ref[0:1, :] * sc0_ref[...] + cw_ref[1:2, :] * sc1_ref[...] + cw_ref[2:3, :] * u)
    xso_ref[...] = _silu(cwx_ref[0:1, :] * sx0_ref[...] + cwx_ref[1:2, :] * sx1_ref[...]
                         + cwx_ref[2:3, :] * sx2_ref[...] + cwx_ref[3:4, :] * xs_ref[...] + cbx_ref[...])
    bco_ref[...] = _silu(cwb_ref[0:1, :] * sb0_ref[...] + cwb_ref[1:2, :] * sb1_ref[...]
                         + cwb_ref[2:3, :] * sb2_ref[...] + cwb_ref[3:4, :] * bc_ref[...] + cbb_ref[...])
    dt = _softplus(dt_ref[...] + dtb_ref[...])
    dto_ref[...] = dt
    ea_ref[...] = jnp.exp(dt * (-jnp.exp(alog_ref[...])))


def _decmix(proj, st_conv, st_ssm_conv, conv_w, cw, cbias, dtb, alog):
    nb = proj.shape[0]
    pcol = lambda w, off: pl.BlockSpec((nb, w), lambda i: (0, off // w))
    full = lambda w: pl.BlockSpec((nb, w), lambda i: (0, 0))
    const = lambda r, w, j: pl.BlockSpec((r, w), lambda i: (0, j))
    sx = [st_ssm_conv[:, t, :512] for t in range(3)]
    sb = [st_ssm_conv[:, t, 512:] for t in range(3)]
    return pl.pallas_call(
        _decmix_kernel,
        grid=(1,),
        in_specs=[pcol(512, CB_OFF), pcol(512, CC_OFF), pcol(512, CX_OFF), pcol(512, XS_OFF), pcol(512, BC_OFF),
                  pcol(LANES, DT_OFF)] + [full(512)] * 8
                 + [const(3, 512, 0), const(4, 512, 0), const(4, 512, 1), const(1, 512, 0), const(1, 512, 1),
                    const(1, LANES, 0), const(1, LANES, 0)],
        out_specs=[full(512), full(512), full(512), full(512), full(LANES), full(LANES)],
        out_shape=[jax.ShapeDtypeStruct((nb, 512), F32)] * 4 + [jax.ShapeDtypeStruct((nb, LANES), F32)] * 2,
        compiler_params=_cparams(("arbitrary",)),
        name="decode_mix",
    )(proj, proj, proj, proj, proj, proj, st_conv[:, 0], st_conv[:, 1], *sx, *sb,
      conv_w, cw, cw, cbias, cbias, dtb, alog)


def _decssd_kernel(h_ref, dt_ref, xs_ref, b_ref, c_ref, ea_ref, dsk_ref, y_ref, ho_ref):
    xs = xs_ref[...]
    h_new = h_ref[...] * ea_ref[...] + (xs * dt_ref[...]) * b_ref[...]
    ho_ref[...] = h_new
    y_ref[...] = jnp.sum(h_new * c_ref[...], axis=-1, keepdims=True) + dsk_ref[...] * xs


def _decssd(h0, dt, xs, bh, ch, ea, dsk):
    nb = h0.shape[0]
    blk = lambda a, b: pl.BlockSpec((None, SSM_HEADS, a, b), lambda i: (i, 0, 0, 0))
    return pl.pallas_call(
        _decssd_kernel,
        grid=(nb,),
        in_specs=[blk(SSM_HEAD_DIM, SSM_STATE), blk(1, 1), blk(SSM_HEAD_DIM, 1), blk(1, SSM_STATE),
                  blk(1, SSM_STATE), blk(1, 1), pl.BlockSpec((SSM_HEADS, 1, 1), lambda i: (0, 0, 0))],
        out_specs=[blk(SSM_HEAD_DIM, 1), blk(SSM_HEAD_DIM, SSM_STATE)],
        out_shape=[jax.ShapeDtypeStruct((nb, SSM_HEADS, SSM_HEAD_DIM, 1), F32),
                   jax.ShapeDtypeStruct((nb, SSM_HEADS, SSM_HEAD_DIM, SSM_STATE), F32)],
        compiler_params=_cparams(("parallel",)),
        name="decode_ssd",
    )(h0, dt, xs, bh, ch, ea, dsk)


DECODE_PAGE_GROUP = 16


def _decscore_kernel(pt_ref, qi_ref, w_ref, *refs):
    del pt_ref
    kid_refs, o_ref = refs[:-1], refs[-1]
    kid = jnp.concatenate([r[...].astype(BF16) for r in kid_refs], axis=0)
    s = lax.dot_general(qi_ref[:, 0:IDX_DIM].astype(BF16), kid, NT_DIMS, preferred_element_type=F32)
    sc = jnp.maximum(s, 0.0) * w_ref[:, 0:1]
    o_ref[...] = jnp.sum(sc, axis=0, keepdims=True)


def _page_specs(block, pg):
    return [pl.BlockSpec(block, lambda b, p, pt, t=t: (pt[b, p * pg + t], 0, 0)) for t in range(pg)]


def _decscore(page_table, qip, wrows, pool_kidx):
    nb, n_pages = page_table.shape
    pg = math.gcd(DECODE_PAGE_GROUP, n_pages)
    grid_spec = pltpu.PrefetchScalarGridSpec(
        num_scalar_prefetch=1,
        grid=(nb, n_pages // pg),
        in_specs=[pl.BlockSpec((None, N_IDX_HEADS, LANES), lambda b, p, pt: (b, 0, 0)),
                  pl.BlockSpec((None, N_IDX_HEADS, PAGE_SIZE), lambda b, p, pt: (b, 0, 0))]
                 + _page_specs((None, PAGE_SIZE, IDX_DIM), pg),
        out_specs=pl.BlockSpec((None, None, 1, pg * PAGE_SIZE), lambda b, p, pt: (b, p, 0, 0)),
    )
    out = pl.pallas_call(
        _decscore_kernel,
        grid_spec=grid_spec,
        out_shape=jax.ShapeDtypeStruct((nb, n_pages // pg, 1, pg * PAGE_SIZE), F32),
        compiler_params=_cparams(("parallel", "arbitrary")),
        name="decode_scores",
    )(page_table, qip, wrows, *([pool_kidx] * pg))
    return out.reshape(nb, n_pages * PAGE_SIZE)


def _decselect_kernel(sc_ref, qi_ref, ki_ref, w_ref, sel_ref, self_ref, keys_ref, *, k_sel, kc, nbits):
    nb, past = sc_ref.shape
    nck = past // kc
    lane_k = lax.broadcasted_iota(I32, (nb, kc), 1)
    ki = ki_ref[...].astype(F32)
    own = jnp.zeros((nb, 1), F32)
    for h in range(N_IDX_HEADS):
        s = jnp.sum(qi_ref[:, h * LANES:(h + 1) * LANES].astype(F32) * ki, axis=1, keepdims=True)
        own = own + jnp.maximum(s, 0.0) * w_ref[:, h:h + 1]
    own_key = _sortable(own)

    def key_body(c, carry):
        off = pl.multiple_of(c * kc, kc)
        keys_ref[:, pl.ds(off, kc)] = _sortable(sc_ref[:, pl.ds(off, kc)])
        return carry

    lax.fori_loop(0, nck, key_body, 0)

    def count_where(pred):
        def body(c, acc):
            off = pl.multiple_of(c * kc, kc)
            f = jnp.where(pred(keys_ref[:, pl.ds(off, kc)], off), 1.0, 0.0)
            part = f[:, 0:LANES]
            for t in range(1, kc // LANES):
                part = part + f[:, t * LANES:(t + 1) * LANES]
            return acc + part

        acc = lax.fori_loop(0, nck, body, jnp.zeros((nb, LANES), F32))
        return jnp.sum(acc, axis=1, keepdims=True)

    def bit_body(i, t):
        cand = t + lax.shift_left(jnp.int32(1), 31 - i)
        cnt = count_where(lambda kk, off: kk >= cand) + jnp.where(own_key >= cand, 1.0, 0.0)
        return jnp.where(cnt >= k_sel, cand, t)

    thr = lax.fori_loop(0, 32, bit_body, jnp.full((nb, 1), INT_MIN, I32))
    thr = jnp.maximum(thr, KEY_NEG_INF)
    finite_thr = thr > KEY_NEG_INF
    cnt_gt = count_where(lambda kk, off: kk > thr) + jnp.where(own_key > thr, 1.0, 0.0)
    need = k_sel - cnt_gt

    def xbody(i, x):
        cand = x + lax.shift_left(jnp.int32(1), nbits - 1 - i)
        cnt = count_where(lambda kk, off: jnp.logical_and(kk == thr, off + lane_k < cand))
        return jnp.where(cnt < need, cand, x)

    xcut = lax.fori_loop(0, nbits, xbody, jnp.zeros((nb, 1), I32))
    xcut = jnp.where(finite_thr, xcut, -1)
    ties_past = count_where(lambda kk, off: kk == thr)
    own_tie = jnp.logical_and(jnp.logical_and(own_key == thr, finite_thr), ties_past < need)
    self_ref[...] = jnp.broadcast_to(
        jnp.where(jnp.logical_or(own_key > thr, own_tie), 1.0, 0.0), (nb, LANES))

    def sel_body(c, carry):
        off = pl.multiple_of(c * kc, kc)
        kk = keys_ref[:, pl.ds(off, kc)]
        sel = jnp.logical_or(kk > thr, jnp.logical_and(kk == thr, off + lane_k <= xcut))
        sel_ref[:, pl.ds(off, kc)] = jnp.where(sel, 1.0, 0.0)
        return carry

    lax.fori_loop(0, nck, sel_body, 0)


def _decselect(scores, qip, kib, wi):
    nb, past = scores.shape
    k_sel = min(TOPK_MAX, (past + 1) // 4)
    kc = 512 if past % 512 == 0 else PAGE_SIZE
    nbits = max(1, past.bit_length())
    full = lambda w: pl.BlockSpec((nb, w), lambda i: (0, 0))
    return pl.pallas_call(
        functools.partial(_decselect_kernel, k_sel=k_sel, kc=kc, nbits=nbits),
        grid=(1,),
        in_specs=[full(past), full(N_IDX_HEADS * LANES), full(LANES), full(LANES)],
        out_specs=[full(past), full(LANES)],
        out_shape=[jax.ShapeDtypeStruct((nb, past), F32), jax.ShapeDtypeStruct((nb, LANES), F32)],
        scratch_shapes=[pltpu.VMEM((nb, past), I32)],
        compiler_params=_cparams(("arbitrary",)),
        name="decode_select",
    )(scores, qip, kib, wi)


def _decattn_kernel(pt_ref, qp_ref, sel_ref, kn_ref, vn_ref, self_ref, *refs, n_steps):
    del pt_ref
    pg = (len(refs) - 4) // 2
    kp_refs, vp_refs = refs[:pg], refs[pg:2 * pg]
    o_ref, m_ref, l_ref, acc_ref = refs[2 * pg:]
    p = pl.program_id(1)

    @pl.when(p == 0)
    def _():
        m_ref[...] = jnp.full(m_ref.shape, NEG_BIG, F32)
        l_ref[...] = jnp.zeros(l_ref.shape, F32)
        acc_ref[...] = jnp.zeros(acc_ref.shape, F32)

    q = qp_ref[...]
    kcat = jnp.concatenate([r[...].astype(BF16) for r in kp_refs], axis=0)
    vcat = jnp.concatenate([r[...].astype(BF16) for r in vp_refs], axis=0)
    s = lax.dot_general(q.astype(BF16), kcat, NT_DIMS, preferred_element_type=F32)
    s = jnp.where(sel_ref[...] > 0.0, s, -jnp.inf)
    m_old = m_ref[:, 0:1]
    m_new = jnp.maximum(m_old, jnp.max(s, axis=1, keepdims=True))
    alpha = jnp.exp(m_old - m_new)
    pr = jnp.exp(s - m_new)
    l_new = alpha * l_ref[:, 0:1] + jnp.sum(pr, axis=1, keepdims=True)
    acc_ref[...] = alpha * acc_ref[...] + jnp.dot(pr.astype(BF16), vcat, preferred_element_type=F32)
    m_ref[...] = jnp.broadcast_to(m_new, m_ref.shape)
    l_ref[...] = jnp.broadcast_to(l_new, l_ref.shape)

    @pl.when(p == n_steps - 1)
    def _():
        s_own = jnp.sum(q * kn_ref[...], axis=1, keepdims=True)
        s_own = jnp.where(self_ref[:, 0:1] > 0.0, s_own, -jnp.inf)
        m_o = m_ref[:, 0:1]
        m_n = jnp.maximum(m_o, s_own)
        al = jnp.exp(m_o - m_n)
        p_own = jnp.exp(s_own - m_n)
        l_n = al * l_ref[:, 0:1] + p_own
        acc = al * acc_ref[...] + p_own.astype(BF16).astype(F32) * vn_ref[...]
        o_ref[...] = acc / l_n


def _decattn(page_table, qp, sel, pool_k, pool_v, kb, vb, self_sel):
    nb, n_pages = page_table.shape
    pg = math.gcd(DECODE_PAGE_GROUP, n_pages)
    n_steps = n_pages // pg
    row = lambda r: pl.BlockSpec((None, r, LANES), lambda b, p, pt: (b, 0, 0))
    pages = _page_specs((None, PAGE_SIZE, LANES), pg)
    grid_spec = pltpu.PrefetchScalarGridSpec(
        num_scalar_prefetch=1,
        grid=(nb, n_steps),
        in_specs=[row(N_HEADS),
                  pl.BlockSpec((None, None, 1, pg * PAGE_SIZE), lambda b, p, pt: (b, p, 0, 0)),
                  row(1), row(1), row(1)] + pages + pages,
        out_specs=row(N_HEADS),
        scratch_shapes=[pltpu.VMEM((N_HEADS, LANES), F32)] * 3,
    )
    return pl.pallas_call(
        functools.partial(_decattn_kernel, n_steps=n_steps),
        grid_spec=grid_spec,
        out_shape=jax.ShapeDtypeStruct((nb, N_HEADS, LANES), F32),
        compiler_params=_cparams(("parallel", "arbitrary")),
        name="decode_attend",
    )(page_table, qp, sel.reshape(nb, n_steps, 1, pg * PAGE_SIZE), kb, vb, self_sel,
      *([pool_k] * pg), *([pool_v] * pg))


def _pack_w_in(w):
    d = w.shape[0]
    pad = lambda n: jnp.zeros((d, n), w.dtype)
    cols = [w[:, 4172:7244],
            w[:, 0:2048],
            w[:, 2628:3140],
            w[:, 3140:4164],
            w[:, 2048:2304],
            w[:, 2304:2560],
            w[:, 2560:2628], pad(60),
            w[:, 4164:4172], pad(120),
            pad(N_PROJ - 7424)]
    return jnp.concatenate(cols, axis=1).astype(BF16)


def _pad_lanes(v):
    return jnp.zeros((1, LANES), F32).at[0, :v.shape[0]].set(v)


def _rope_tables(pos):
    half = HEAD_DIM // 2
    inv = ROPE_THETA ** (-jnp.arange(half, dtype=F32) / half)
    ang = pos.astype(F32)[:, None] * inv[None, :]
    cos, sin = jnp.cos(ang), jnp.sin(ang)
    cos2 = jnp.concatenate([cos, cos], axis=1)
    sin2 = jnp.concatenate([-sin, sin], axis=1)
    return jnp.tile(cos2, (1, 2)), jnp.tile(sin2, (1, 2))


def _head_halves(o):
    hpg = N_HEADS // N_KV_HEADS
    parts = [o[:, h, (h // hpg) * HEAD_DIM:(h // hpg + 1) * HEAD_DIM] for h in range(N_HEADS)]
    return jnp.concatenate(parts, axis=-1)


def kernel(x_prompt, x_sample, cache_k, cache_v, cache_kidx, state_conv, state_ssm_conv, state_ssm, page_table,
           norm1, w_in, b_gate, conv_w, q_norm, k_norm, w_pc, w_pa, ssm_conv_w, ssm_conv_b, dt_bias, a_log,
           d_skip, ssm_norm, w_ps, w_o, norm2, w_gate_dense, w_up_dense, w_down_dense, w_router, w_gate_moe,
           w_up_moe, w_down_moe):
    nb, seq, _ = x_prompt.shape
    db = x_sample.shape[0]
    depth = w_in.shape[0]
    n_pages = page_table.shape[1]
    past = n_pages * PAGE_SIZE
    n_pool = cache_k.shape[1]

    hp = x_prompt.reshape(nb * seq, D_MODEL)
    hs = x_sample.reshape(db, D_MODEL)
    cos_p, sin_p = _rope_tables(jnp.tile(jnp.arange(seq), nb))
    cos_s, sin_s = _rope_tables(jnp.full((db,), past))

    outs_p = [[] for _ in range(6)]
    outs_s = [[] for _ in range(6)]
    for l in range(depth):
        wp = _pack_w_in(w_in[l])
        g1 = norm1[l].reshape(1, D_MODEL)
        g2 = norm2[l].reshape(1, D_MODEL)
        qg = jnp.tile(q_norm[l], 2).reshape(1, LANES)
        kg = jnp.tile(k_norm[l], 2).reshape(1, LANES)
        cw = ssm_conv_w[l]
        cbias = ssm_conv_b[l].reshape(1, SSM_CONV_DIM)
        dtb = _pad_lanes(dt_bias[l])
        alog = _pad_lanes(a_log[l])
        dsk = jnp.repeat(d_skip[l], SSM_HEAD_DIM).reshape(1, SSM_INNER)
        sn = ssm_norm[l].reshape(1, SSM_INNER)
        wpc, wpa, wps, wo = (w.astype(BF16) for w in (w_pc[l], w_pa[l], w_ps[l], w_o[l]))
        i = l // 2
        if l % 2 == 0:
            routed = False
            wr = jnp.zeros((D_MODEL, LANES), F32)
            wg, wu, wd = (w[i:i + 1].astype(BF16) for w in (w_gate_dense, w_up_dense, w_down_dense))
        else:
            routed = True
            wr = jnp.zeros((D_MODEL, LANES), F32).at[:, :N_EXPERTS].set(w_router[i])
            wg, wu, wd = (w[i].astype(BF16) for w in (w_gate_moe, w_up_moe, w_down_moe))

        proj = _inproj(hp, g1, wp)
        qt, kf, kb, vt, qit, kif, kib, wt = _prep(proj, cos_p, sin_p, qg, kg, True)
        attn = _dsa_prompt(qt, qit, wt, kb, vt, kib, nb, seq)
        pre_a, conv_st = _conva_prompt(proj, conv_w[l], nb, seq)
        y_ssd, h_last = _ssd_prompt(proj, cw, cbias, dtb, alog, dsk, nb, seq)
        hp = _merge(hp, pre_a, attn, y_ssd, proj, sn, b_gate[l], wpc, wpa, wps, wo)
        hp = _ffn(hp, g2, wr, wg, wu, wd, routed)
        proj3 = proj.reshape(nb, seq, N_PROJ)
        outs_p[0].append(kf.reshape(nb, seq, N_KV_HEADS, HEAD_DIM))
        outs_p[1].append(proj3[:, :, V_OFF:V_OFF + LANES].reshape(nb, seq, N_KV_HEADS, HEAD_DIM))
        outs_p[2].append(kif[:, :IDX_DIM].reshape(nb, seq, IDX_DIM))
        outs_p[3].append(conv_st)
        outs_p[4].append(proj3[:, seq - 3:, XS_OFF:XS_OFF + SSM_CONV_DIM])
        outs_p[5].append(h_last.reshape(nb, SSM_HEADS, SSM_HEAD_DIM, SSM_STATE))

        proj = _inproj(hs, g1, wp)
        qp, kf, kb, vb, qip, kif, kib, wi = _prep(proj, cos_s, sin_s, qg, kg, False)
        wrows = jnp.broadcast_to(wi[:, :N_IDX_HEADS, None], (db, N_IDX_HEADS, PAGE_SIZE))
        scores = _decscore(page_table, qip.astype(F32).reshape(db, N_IDX_HEADS, LANES), wrows, cache_kidx[l])
        sel, self_sel = _decselect(scores, qip, kib, wi)
        o = _decattn(page_table, qp.astype(F32).reshape(db, N_HEADS, LANES), sel,
                     cache_k[l].reshape(n_pool, PAGE_SIZE, LANES), cache_v[l].reshape(n_pool, PAGE_SIZE, LANES),
                     kb.astype(F32).reshape(db, 1, LANES), vb.astype(F32).reshape(db, 1, LANES),
                     self_sel.reshape(db, 1, LANES))
        attn = _head_halves(o)
        pre_a, u, xs, bcv, dt, ea = _decmix(proj, state_conv[l], state_ssm_conv[l], conv_w[l], cw, cbias, dtb, alog)
        hpg = SSM_HEADS // SSM_GROUPS
        xs4 = xs.reshape(db, SSM_HEADS, SSM_HEAD_DIM, 1)
        bh = jnp.repeat(bcv[:, :SSM_GROUPS * SSM_STATE].reshape(db, SSM_GROUPS, 1, SSM_STATE), hpg, axis=1)
        ch = jnp.repeat(bcv[:, SSM_GROUPS * SSM_STATE:].reshape(db, SSM_GROUPS, 1, SSM_STATE), hpg, axis=1)
        y4, h_new = _decssd(state_ssm[l], dt[:, :SSM_HEADS, None, None], xs4, bh, ch, ea[:, :SSM_HEADS, None, None],
                            d_skip[l].reshape(SSM_HEADS, 1, 1))
        hs = _merge(hs, pre_a, attn, y4.reshape(db, SSM_INNER), proj, sn, b_gate[l], wpc, wpa, wps, wo)
        hs = _ffn(hs, g2, wr, wg, wu, wd, routed)
        outs_s[0].append(kf.reshape(db, 1, N_KV_HEADS, HEAD_DIM))
        outs_s[1].append(proj[:, V_OFF:V_OFF + LANES].reshape(db, 1, N_KV_HEADS, HEAD_DIM))
        outs_s[2].append(kif[:, :IDX_DIM].reshape(db, 1, IDX_DIM))
        outs_s[3].append(jnp.stack([state_conv[l][:, 1], u], axis=1))
        outs_s[4].append(jnp.concatenate(
            [state_ssm_conv[l][:, 1:], proj[:, None, XS_OFF:XS_OFF + SSM_CONV_DIM]], axis=1))
        outs_s[5].append(h_new)

    return (hp.reshape(nb, seq, D_MODEL), hs.reshape(db, 1, D_MODEL),
            *(jnp.stack(o) for o in outs_p), *(jnp.stack(o) for o in outs_s))
```

```python
import functools
import math

import jax
import jax.numpy as jnp
from jax import lax
from jax.experimental import pallas as pl
from jax.experimental.pallas import tpu as pltpu

F32 = jnp.float32
BF16 = jnp.bfloat16
I32 = jnp.int32

D_MODEL = 1024
D_CONV = 512
N_HEADS = 8
HEAD_DIM = 64
N_KV_HEADS = 2
N_IDX_HEADS = 4
IDX_DIM = 64
TOPK_MAX = 256
Q_BLOCK = 128
ROPE_THETA = 10000.0
SSM_INNER = 512
SSM_HEADS = 8
SSM_HEAD_DIM = 64
SSM_GROUPS = 2
SSM_STATE = 128
SSM_CHUNK = 128
SSM_CONV_DIM = 1024
PAGE_SIZE = 128
N_EXPERTS = 8
EPS = 1e-6

LANES = 128
SUBLANES = 8
VMEM_LIMIT = 52 * 1024 * 1024

G_OFF, CB_OFF, CC_OFF, CX_OFF, Q_OFF, Z_OFF, XS_OFF, BC_OFF = 0, 3072, 3584, 4096, 4608, 5120, 5632, 6144
K_OFF, V_OFF, QI_OFF, KW_OFF, DT_OFF = 6656, 6784, 6912, 7168, 7296
N_PROJ = 7680

INT_MIN = -2 ** 31
INT_MAX = 2 ** 31 - 1
KEY_NEG_INF = (-8388608) ^ 0x7FFFFFFF
NEG_BIG = -1e30
LOG2_E = 1.4426950408889634

NT_DIMS = (((1,), (1,)), ((), ()))


def _cparams(sem):
    return pltpu.CompilerParams(dimension_semantics=sem, vmem_limit_bytes=VMEM_LIMIT)


def _sigmoid(x):
    return 1.0 / (1.0 + jnp.exp(-x))


def _silu(x):
    return x * _sigmoid(x)


def _softplus(x):
    return jnp.maximum(x, 0.0) + jnp.log1p(jnp.exp(-jnp.abs(x)))


def _sortable(x):
    bits = pltpu.bitcast(x, I32)
    return bits ^ (lax.shift_right_arithmetic(bits, 31) & 0x7FFFFFFF)


def _rms(x, g):
    return x * lax.rsqrt(jnp.mean(x * x, axis=-1, keepdims=True) + EPS) * g


def _inproj_kernel(x_ref, g_ref, w_ref, o_ref, h_ref):
    @pl.when(pl.program_id(1) == 0)
    def _():
        h_ref[...] = _rms(x_ref[...], g_ref[...]).astype(BF16)

    o_ref[...] = jnp.dot(h_ref[...], w_ref[...], preferred_element_type=F32)


def _inproj(x, g, w):
    m = x.shape[0]
    tm = min(m, 1024)
    tn = 1536
    return pl.pallas_call(
        _inproj_kernel,
        grid=(m // tm, N_PROJ // tn),
        in_specs=[pl.BlockSpec((tm, D_MODEL), lambda i, j: (i, 0)),
                  pl.BlockSpec((1, D_MODEL), lambda i, j: (0, 0)),
                  pl.BlockSpec((D_MODEL, tn), lambda i, j: (0, j))],
        out_specs=pl.BlockSpec((tm, tn), lambda i, j: (i, j)),
        out_shape=jax.ShapeDtypeStruct((m, N_PROJ), F32),
        scratch_shapes=[pltpu.VMEM((tm, D_MODEL), BF16)],
        compiler_params=_cparams(("parallel", "arbitrary")),
        name="inproj",
    )(x, g, w)


V_T_ROWS = LANES + 16


def _prep_kernel(q_ref, k_ref, v_ref, qi_ref, kw_ref, cos_ref, sin_ref, qg_ref, kg_ref,
                 qp_ref, kf_ref, kb_ref, vb_ref, qip_ref, kif_ref, kib_ref, wo_ref, *, transposed):
    tm = cos_ref.shape[0]

    def put(ref, h, val):
        if transposed:
            ref[h * LANES:(h + 1) * LANES, :] = val.T.astype(BF16)
        else:
            ref[:, h * LANES:(h + 1) * LANES] = val.astype(BF16)

    cos = cos_ref[...]
    sin = sin_ref[...]
    lane = lax.broadcasted_iota(I32, (tm, LANES), 1)
    low_half = lane < HEAD_DIM
    first_rot = (lane % HEAD_DIM) < (HEAD_DIM // 2)
    r = lax.broadcasted_iota(I32, (LANES, LANES), 0) // HEAD_DIM
    c = lax.broadcasted_iota(I32, (LANES, LANES), 1) // HEAD_DIM
    seg = jnp.where(r == c, 1.0 / HEAD_DIM, 0.0).astype(BF16)

    def rope(x):
        fwd = pltpu.roll(x, LANES - HEAD_DIM // 2, 1)
        bwd = pltpu.roll(x, HEAD_DIM // 2, 1)
        return x * cos + jnp.where(first_rot, fwd, bwd) * sin

    def headnorm(x, g):
        s = x * x
        hi = s.astype(BF16)
        lo = (s - hi.astype(F32)).astype(BF16)
        ms = (jnp.dot(hi, seg, preferred_element_type=F32)
              + jnp.dot(lo, seg, preferred_element_type=F32))
        return x * lax.rsqrt(ms + EPS) * g

    q_scale = HEAD_DIM ** -0.5 * (LOG2_E if transposed else 1.0)
    qg = qg_ref[...]
    for s in range(N_HEADS // 2):
        slab = rope(headnorm(q_ref[:, s * LANES:(s + 1) * LANES], qg)) * q_scale
        swapped = pltpu.roll(slab, HEAD_DIM, 1)
        grp = (2 * s) // (N_HEADS // N_KV_HEADS)
        for hh in range(2):
            h = 2 * s + hh
            src = slab if hh == grp else swapped
            keep = low_half if grp == 0 else jnp.logical_not(low_half)
            put(qp_ref, h, jnp.where(keep, src, 0.0))

    k = rope(headnorm(k_ref[...], kg_ref[...]))
    kf_ref[...] = k
    kb_ref[...] = k.astype(BF16)
    if transposed:
        vb_ref[0:LANES, :] = v_ref[...].T.astype(BF16)
        vb_ref[LANES:V_T_ROWS, :] = jnp.ones((V_T_ROWS - LANES, tm), BF16)
    else:
        vb_ref[...] = v_ref[...].astype(BF16)

    for s in range(N_IDX_HEADS // 2):
        slab = rope(qi_ref[:, s * LANES:(s + 1) * LANES]) * (IDX_DIM ** -0.5)
        swapped = pltpu.roll(slab, HEAD_DIM, 1)
        for hh in range(2):
            h = 2 * s + hh
            src = slab if hh == 0 else swapped
            put(qip_ref, h, jnp.where(low_half, src, 0.0))

    kw = kw_ref[...]
    ki = rope(kw)
    kif_ref[...] = ki
    kib_ref[...] = jnp.where(low_half, ki, 0.0).astype(BF16)
    wi = pltpu.roll(kw, HEAD_DIM, 1) * (N_IDX_HEADS ** -0.5)
    if transposed:
        wo_ref[...] = wi.T[0:SUBLANES, :]
    else:
        wo_ref[...] = wi


def _prep(proj, cos, sin, qg, kg, transposed):
    m = proj.shape[0]
    tm = min(m, 512)
    row = lambda w, off: pl.BlockSpec((tm, w), lambda i: (i, off // w))
    full = lambda w: pl.BlockSpec((tm, w), lambda i: (i, 0))
    const = pl.BlockSpec((1, LANES), lambda i: (0, 0))
    if transposed:
        feat = lambda r: pl.BlockSpec((r, tm), lambda i: (0, i))
        fshape = lambda r, dt: jax.ShapeDtypeStruct((r, m), dt)
    else:
        feat = full
        fshape = lambda r, dt: jax.ShapeDtypeStruct((m, r), dt)
    return pl.pallas_call(
        functools.partial(_prep_kernel, transposed=transposed),
        grid=(m // tm,),
        in_specs=[row(512, Q_OFF), row(LANES, K_OFF), row(LANES, V_OFF), row(256, QI_OFF), row(LANES, KW_OFF),
                  full(LANES), full(LANES), const, const],
        out_specs=[feat(N_HEADS * LANES), full(LANES), full(LANES), feat(V_T_ROWS if transposed else LANES),
                   feat(N_IDX_HEADS * LANES), full(LANES), full(LANES), feat(SUBLANES if transposed else LANES)],
        out_shape=[fshape(N_HEADS * LANES, BF16),
                   jax.ShapeDtypeStruct((m, LANES), F32),
                   jax.ShapeDtypeStruct((m, LANES), BF16),
                   fshape(V_T_ROWS if transposed else LANES, BF16),
                   fshape(N_IDX_HEADS * LANES, BF16),
                   jax.ShapeDtypeStruct((m, LANES), F32),
                   jax.ShapeDtypeStruct((m, LANES), BF16),
                   fshape(SUBLANES if transposed else LANES, F32)],
        compiler_params=_cparams(("parallel",)),
        name="prep",
    )(proj, proj, proj, proj, proj, cos, sin, qg, kg)


def _fold_rows(f):
    while f.shape[0] > SUBLANES:
        half = f.shape[0] // 2
        f = f[:half] + f[half:]
    return f


def _dsa_kernel(qt_ref, qit_ref, wt_ref, k_ref, vt_ref, ki_ref, o_ref,
                keys_ref, x_ref, m_ref, kmax_ref, acc_ref, *, k_sel, kc, nbits):
    qb = Q_BLOCK
    j = pl.program_id(1)
    nck = (j * qb + qb + kc - 1) // kc
    qpos = j * qb + lax.broadcasted_iota(I32, (1, qb), 1)
    sub_k = lax.broadcasted_iota(I32, (kc, qb), 0)
    pair = lambda ref, s: jnp.concatenate(
        [ref[(2 * s) * LANES:(2 * s + 1) * LANES, :], ref[(2 * s + 1) * LANES:(2 * s + 2) * LANES, :]], axis=1)
    qi2 = [pair(qit_ref, s) for s in range(N_IDX_HEADS // 2)]
    q2 = [pair(qt_ref, s) for s in range(N_HEADS // 2)]

    def score_body(c, carry):
        off = pl.multiple_of(c * kc, kc)
        kic = ki_ref[pl.ds(off, kc), :]
        acc = jnp.zeros((kc, qb), F32)
        for s in range(N_IDX_HEADS // 2):
            s2 = jnp.dot(kic, qi2[s], preferred_element_type=F32)
            for hh in range(2):
                h = 2 * s + hh
                acc = acc + jnp.maximum(s2[:, hh * qb:(hh + 1) * qb], 0.0) * wt_ref[h:h + 1, :]
        acc = jnp.where(off + sub_k <= qpos, acc, -jnp.inf)
        keys_ref[pl.ds(off, kc), :] = _sortable(acc)
        return carry

    lax.fori_loop(0, nck, score_body, 0)

    def count_where(pred):
        def body(c, acc):
            off = pl.multiple_of(c * kc, kc)
            return acc + _fold_rows(jnp.where(pred(keys_ref[pl.ds(off, kc), :], off), 1.0, 0.0))

        acc = lax.fori_loop(0, nck, body, jnp.zeros((SUBLANES, qb), F32))
        return jnp.sum(acc, axis=0, keepdims=True)

    def bit_body(i, t):
        cand = t + lax.shift_left(jnp.int32(1), 31 - i)
        cnt = count_where(lambda kk, off: kk >= cand)
        return jnp.where(cnt >= k_sel, cand, t)

    thr = lax.fori_loop(0, 32, bit_body, jnp.full((1, qb), INT_MIN, I32))
    thr = jnp.maximum(thr, KEY_NEG_INF)
    finite_thr = thr > KEY_NEG_INF
    cnt_ge = count_where(lambda kk, off: kk >= thr)
    cnt_gt = count_where(lambda kk, off: kk > thr)
    need = k_sel - cnt_gt
    tie = jnp.logical_and(cnt_ge > k_sel, finite_thr)
    x_default = jnp.where(finite_thr, INT_MAX, -1)
    x_ref[...] = jnp.broadcast_to(x_default, x_ref.shape)

    @pl.when(jnp.max(jnp.where(tie, 1.0, 0.0)) > 0.0)
    def _():
        def xbody(i, x):
            cand = x + lax.shift_left(jnp.int32(1), nbits - 1 - i)
            cnt = count_where(lambda kk, off: jnp.logical_and(kk == thr, off + sub_k < cand))
            return jnp.where(cnt < need, cand, x)

        x = lax.fori_loop(0, nbits, xbody, jnp.zeros((1, qb), I32))
        x_ref[...] = jnp.broadcast_to(jnp.where(tie, x, x_default), x_ref.shape)

    xcut = x_ref[0:1, :]
    hpg = N_HEADS // N_KV_HEADS

    @pl.when(j == 0)
    def _():
        r = lax.broadcasted_iota(I32, (LANES, LANES), 0) // HEAD_DIM
        cc = lax.broadcasted_iota(I32, (LANES, LANES), 1) // HEAD_DIM
        seg = jnp.where(r == cc, 1.0, 0.0).astype(BF16)

        def body(c, mx):
            kf = k_ref[pl.ds(pl.multiple_of(c * kc, kc), kc), :].astype(F32)
            gs = jnp.dot((kf * kf).astype(BF16), seg, preferred_element_type=F32)
            while gs.shape[0] > SUBLANES:
                half = gs.shape[0] // 2
                gs = jnp.maximum(gs[:half], gs[half:])
            return jnp.maximum(mx, gs)

        mx = lax.fori_loop(0, k_ref.shape[0] // kc, body, jnp.zeros((SUBLANES, LANES), F32))
        mx = jnp.max(mx, axis=0, keepdims=True)
        lane = lax.broadcasted_iota(I32, (1, LANES), 1)
        other = pltpu.roll(mx, HEAD_DIM, 1)
        kmax_ref[0:1, :] = jnp.where(lane < HEAD_DIM, mx, other)
        kmax_ref[1:2, :] = jnp.where(lane < HEAD_DIM, other, mx)

    def chunk(c):
        off = pl.multiple_of(c * kc, kc)
        kk = keys_ref[pl.ds(off, kc), :]
        sel = jnp.logical_or(kk > thr, jnp.logical_and(kk == thr, off + sub_k <= xcut))
        return sel, k_ref[pl.ds(off, kc), :], vt_ref[:, pl.ds(off, kc)]

    bound = []
    for h in range(N_HEADS):
        qh = qt_ref[h * LANES:(h + 1) * LANES, :].astype(F32)
        qsq = jnp.sum(qh * qh, axis=0, keepdims=True)
        bound.append(jnp.sqrt(qsq * kmax_ref[h // hpg:h // hpg + 1, :]))
    acc_ref[...] = jnp.zeros(acc_ref.shape, F32)

    def fast_body(c, carry):
        sel, kch, vch = chunk(c)
        n_pair = N_HEADS // 2
        s2 = jnp.dot(kch, q2[0], preferred_element_type=F32)
        for s in range(n_pair):
            s2_next = jnp.dot(kch, q2[s + 1], preferred_element_type=F32) if s + 1 < n_pair else None
            ps = [jnp.exp2(jnp.where(sel, s2[:, hh * qb:(hh + 1) * qb] - bound[2 * s + hh], -jnp.inf)).astype(BF16)
                  for hh in range(2)]
            acc_ref[s] += jnp.dot(vch, jnp.concatenate(ps, axis=1), preferred_element_type=F32)
            s2 = s2_next
        return carry

    lax.fori_loop(0, nck, fast_body, 0)

    dens = jnp.concatenate([acc_ref[s, LANES:LANES + 1, :] for s in range(N_HEADS // 2)], axis=0)

    @pl.when(jnp.logical_not(jnp.min(dens) >= 1e-20))
    def _():
        m_ref[...] = jnp.full(m_ref.shape, NEG_BIG, F32)
        acc_ref[...] = jnp.zeros(acc_ref.shape, F32)

        def exact_body(c, carry):
            sel, kch, vch = chunk(c)
            for s in range(N_HEADS // 2):
                s2 = jnp.dot(kch, q2[s], preferred_element_type=F32)
                ps, alphas = [], []
                for hh in range(2):
                    h = 2 * s + hh
                    sh = jnp.where(sel, s2[:, hh * qb:(hh + 1) * qb], -jnp.inf)
                    m_old = m_ref[h:h + 1, :]
                    m_new = jnp.maximum(m_old, jnp.max(sh, axis=0, keepdims=True))
                    m_ref[h:h + 1, :] = m_new
                    alphas.append(jnp.exp2(m_old - m_new))
                    ps.append(jnp.exp2(sh - m_new).astype(BF16))
                o2 = jnp.dot(vch, jnp.concatenate(ps, axis=1), preferred_element_type=F32)
                acc_ref[s] = jnp.concatenate(alphas, axis=1) * acc_ref[s] + o2
            return carry

        lax.fori_loop(0, nck, exact_body, 0)

    for s in range(N_HEADS // 2):
        g = (2 * s) // hpg
        num = acc_ref[s, g * HEAD_DIM:(g + 1) * HEAD_DIM, :]
        out2 = num / acc_ref[s, LANES:LANES + 1, :]
        o_ref[:, s * LANES:(s + 1) * LANES] = jnp.concatenate([out2[:, :qb], out2[:, qb:]], axis=0).T


def _dsa_prompt(qt, qit, wt, kb, vt, kib, nb, seq):
    k_sel = min(TOPK_MAX, seq // 4)
    kc = 512 if seq % 512 == 0 else Q_BLOCK
    nbits = max(1, (seq - 1).bit_length())
    nq = seq // Q_BLOCK
    r3 = lambda a: a.reshape(nb, seq, a.shape[-1])
    qcol = lambda r: pl.BlockSpec((r, Q_BLOCK), lambda b, j: (0, b * nq + j))
    seqblk = pl.BlockSpec((None, seq, LANES), lambda b, j: (b, 0, 0))
    out = pl.pallas_call(
        functools.partial(_dsa_kernel, k_sel=k_sel, kc=kc, nbits=nbits),
        grid=(nb, nq),
        in_specs=[qcol(N_HEADS * LANES), qcol(N_IDX_HEADS * LANES), qcol(SUBLANES), seqblk,
                  pl.BlockSpec((V_T_ROWS, seq), lambda b, j: (0, b)), seqblk],
        out_specs=pl.BlockSpec((None, Q_BLOCK, N_HEADS * HEAD_DIM), lambda b, j: (b, j, 0)),
        out_shape=jax.ShapeDtypeStruct((nb, seq, N_HEADS * HEAD_DIM), F32),
        scratch_shapes=[pltpu.VMEM((seq, Q_BLOCK), I32),
                        pltpu.VMEM((SUBLANES, Q_BLOCK), I32),
                        pltpu.VMEM((N_HEADS, Q_BLOCK), F32),
                        pltpu.VMEM((SUBLANES, LANES), F32),
                        pltpu.VMEM((N_HEADS // 2, V_T_ROWS, 2 * Q_BLOCK), F32)],
        compiler_params=_cparams(("arbitrary", "arbitrary")),
        name="dsa_prompt",
    )(qt, qit, wt, r3(kb), vt, r3(kib))
    return out.reshape(nb * seq, N_HEADS * HEAD_DIM)


def _conva_kernel(cb_ref, cc_ref, cx_ref, cch_ref, cxh_ref, w_ref, o_ref, st_ref, *, tiles_per_seq):
    tm = cb_ref.shape[0]
    first = (pl.program_id(0) % tiles_per_seq) == 0
    u = cc_ref[...] * cx_ref[...]
    uh = jnp.where(first, 0.0, cch_ref[...] * cxh_ref[...])
    ext = jnp.concatenate([uh, u], axis=0)
    conv = (w_ref[2:3, :] * u + w_ref[1:2, :] * ext[SUBLANES - 1:SUBLANES - 1 + tm]
            + w_ref[0:1, :] * ext[SUBLANES - 2:SUBLANES - 2 + tm])
    o_ref[...] = cb_ref[...] * conv
    st_ref[...] = u[tm - 2:tm, :]


def _conva_prompt(proj, conv_w, nb, seq):
    m = proj.shape[0]
    tm = min(seq, 512)
    tps = seq // tm
    col = lambda off: pl.BlockSpec((tm, D_CONV), lambda i: (i, off // D_CONV))
    halo = lambda off: pl.BlockSpec(
        (SUBLANES, D_CONV), lambda i: (jnp.maximum(i * (tm // SUBLANES) - 1, 0), off // D_CONV))
    return pl.pallas_call(
        functools.partial(_conva_kernel, tiles_per_seq=tps),
        grid=(m // tm,),
        in_specs=[col(CB_OFF), col(CC_OFF), col(CX_OFF), halo(CC_OFF), halo(CX_OFF),
                  pl.BlockSpec((3, D_CONV), lambda i: (0, 0))],
        out_specs=[pl.BlockSpec((tm, D_CONV), lambda i: (i, 0)),
                   pl.BlockSpec((None, 2, D_CONV), lambda i: (i // tps, 0, 0))],
        out_shape=[jax.ShapeDtypeStruct((m, D_CONV), F32),
                   jax.ShapeDtypeStruct((nb, 2, D_CONV), F32)],
        compiler_params=_cparams(("arbitrary",)),
        name="conva_prompt",
    )(proj, proj, proj, proj, proj, conv_w)


def _ssd_kernel(xs_ref, bc_ref, dt_ref, xh_ref, bh_ref, cwx_ref, cwb_ref, cbx_ref, cbb_ref,
                dtb_ref, alog_ref, dsk_ref, y_ref, hout_ref, h_ref, *, nchunk):
    cl = SSM_CHUNK
    c = pl.program_id(1)
    first = c == 0

    @pl.when(first)
    def _():
        h_ref[...] = jnp.zeros(h_ref.shape, F32)

    def conv(cur, halo, w_ref, b_ref):
        ext = jnp.concatenate([jnp.where(first, 0.0, halo), cur], axis=0)
        out = (w_ref[3:4, :] * cur + w_ref[2:3, :] * ext[SUBLANES - 1:SUBLANES - 1 + cl]
               + w_ref[1:2, :] * ext[SUBLANES - 2:SUBLANES - 2 + cl]
               + w_ref[0:1, :] * ext[SUBLANES - 3:SUBLANES - 3 + cl] + b_ref[...])
        return _silu(out)

    xs = conv(xs_ref[...], xh_ref[...], cwx_ref, cbx_ref)
    bc = conv(bc_ref[...], bh_ref[...], cwb_ref, cbb_ref)
    dt = _softplus(dt_ref[...] + dtb_ref[...])
    a = dt * (-jnp.exp(alog_ref[...]))
    ri = lax.broadcasted_iota(I32, (cl, cl), 0)
    ci = lax.broadcasted_iota(I32, (cl, cl), 1)
    causal = ri >= ci
    cs = jnp.dot(jnp.where(causal, 1.0, 0.0), a, preferred_element_type=F32,
                 precision=lax.Precision.HIGHEST)
    cs_t = cs.T
    lane = lax.broadcasted_iota(I32, (cl, LANES), 1)
    lo = lane < SSM_HEAD_DIM
    rows_lo = lax.broadcasted_iota(I32, (LANES, 1), 0) < SSM_HEAD_DIM
    heads_per_group = SSM_HEADS // SSM_GROUPS

    cb = []
    for g in range(SSM_GROUPS):
        bg = bc[:, g * SSM_STATE:(g + 1) * SSM_STATE].astype(BF16)
        cg = bc[:, (SSM_GROUPS + g) * SSM_STATE:(SSM_GROUPS + g + 1) * SSM_STATE].astype(BF16)
        cb.append((bg, cg, lax.dot_general(cg, bg, NT_DIMS, preferred_element_type=F32)))

    for s in range(SSM_HEADS // 2):
        h0, h1 = 2 * s, 2 * s + 1
        bg, cg, cbg = cb[h0 // heads_per_group]
        sl = slice(s * LANES, (s + 1) * LANES)
        xs_s = xs[:, sl]
        col0, col1 = cs[:, h0:h0 + 1], cs[:, h1:h1 + 1]
        last0, last1 = cs[cl - 1:cl, h0:h0 + 1], cs[cl - 1:cl, h1:h1 + 1]
        xdt = xs_s * jnp.where(lo, dt[:, h0:h0 + 1], dt[:, h1:h1 + 1])
        xdt_b = xdt.astype(BF16)
        m0 = (cbg * jnp.where(causal, jnp.exp(col0 - cs_t[h0:h0 + 1, :]), 0.0)).astype(BF16)
        m1 = (cbg * jnp.where(causal, jnp.exp(col1 - cs_t[h1:h1 + 1, :]), 0.0)).astype(BF16)
        y_diag = jnp.where(lo, jnp.dot(m0, xdt_b, preferred_element_type=F32),
                           jnp.dot(m1, xdt_b, preferred_element_type=F32))
        hs = h_ref[sl, :]
        y_off = lax.dot_general(cg, hs.astype(BF16), NT_DIMS, preferred_element_type=F32)
        y_off = y_off * jnp.where(lo, jnp.exp(col0), jnp.exp(col1))
        y_ref[:, sl] = y_diag + y_off + dsk_ref[:, sl] * xs_s
        xw = xdt * jnp.where(lo, jnp.exp(last0 - col0), jnp.exp(last1 - col1))
        st = jnp.dot(xw.T.astype(BF16), bg, preferred_element_type=F32)
        h_ref[sl, :] = hs * jnp.where(rows_lo, jnp.exp(last0), jnp.exp(last1)) + st

    @pl.when(c == nchunk - 1)
    def _():
        hout_ref[...] = h_ref[...]


def _ssd_prompt(proj, cw, cbias, dtb, alog, dsk, nb, seq):
    m = proj.shape[0]
    cl = SSM_CHUNK
    nchunk = seq // cl
    blk = lambda w, off: pl.BlockSpec((cl, w), lambda b, c: (b * nchunk + c, off // w))
    halo = lambda off: pl.BlockSpec(
        (SUBLANES, 512), lambda b, c: (jnp.maximum((b * nchunk + c) * (cl // SUBLANES) - 1, 0), off // 512))
    const = lambda r, w, j: pl.BlockSpec((r, w), lambda b, c: (0, j))
    y, hout = pl.pallas_call(
        functools.partial(_ssd_kernel, nchunk=nchunk),
        grid=(nb, nchunk),
        in_specs=[blk(512, XS_OFF), blk(512, BC_OFF), blk(LANES, DT_OFF), halo(XS_OFF), halo(BC_OFF),
                  const(4, 512, 0), const(4, 512, 1), const(1, 512, 0), const(1, 512, 1),
                  const(1, LANES, 0), const(1, LANES, 0), const(1, 512, 0)],
        out_specs=[pl.BlockSpec((cl, SSM_INNER), lambda b, c: (b * nchunk + c, 0)),
                   pl.BlockSpec((None, SSM_INNER, SSM_STATE), lambda b, c: (b, 0, 0))],
        out_shape=[jax.ShapeDtypeStruct((m, SSM_INNER), F32),
                   jax.ShapeDtypeStruct((nb, SSM_INNER, SSM_STATE), F32)],
        scratch_shapes=[pltpu.VMEM((SSM_INNER, SSM_STATE), F32)],
        compiler_params=_cparams(("parallel", "arbitrary")),
        name="ssd_prompt",
    )(proj, proj, proj, proj, proj, cw, cw, cbias, cbias, dtb, alog, dsk)
    return y, hout


def _merge_kernel(x_ref, pa_ref, at_ref, ys_ref, z_ref, g0_ref, g1_ref, g2_ref, sn_ref, bg_ref,
                  wpc_ref, wpa_ref, wps_ref, wo_ref, o_ref):
    ssd = _rms(ys_ref[...] * _silu(z_ref[...]), sn_ref[...])
    ya = jnp.dot(pa_ref[...].astype(BF16), wpc_ref[...], preferred_element_type=F32)
    yb = jnp.dot(at_ref[...].astype(BF16), wpa_ref[...], preferred_element_type=F32)
    yc = jnp.dot(ssd.astype(BF16), wps_ref[...], preferred_element_type=F32)
    merged = (_sigmoid(g0_ref[...] + bg_ref[0:1, :]) * ya + _sigmoid(g1_ref[...] + bg_ref[1:2, :]) * yb
              + _sigmoid(g2_ref[...] + bg_ref[2:3, :]) * yc)
    o_ref[...] = x_ref[...] + jnp.dot(merged.astype(BF16), wo_ref[...], preferred_element_type=F32)


def _merge(x, pre_a, attn, y_ssd, proj, ssm_norm, b_gate, w_pc, w_pa, w_ps, w_o):
    m = x.shape[0]
    tm = min(m, 256)
    row = lambda w: pl.BlockSpec((tm, w), lambda i: (i, 0))
    pcol = lambda w, off: pl.BlockSpec((tm, w), lambda i: (i, off // w))
    const = lambda r, w: pl.BlockSpec((r, w), lambda i: (0, 0))
    return pl.pallas_call(
        _merge_kernel,
        grid=(m // tm,),
        in_specs=[row(D_MODEL), row(512), row(512), row(512), pcol(512, Z_OFF),
                  pcol(D_MODEL, G_OFF), pcol(D_MODEL, G_OFF + D_MODEL), pcol(D_MODEL, G_OFF + 2 * D_MODEL),
                  const(1, 512), const(3, D_MODEL),
                  const(512, D_MODEL), const(512, D_MODEL), const(512, D_MODEL), const(D_MODEL, D_MODEL)],
        out_specs=row(D_MODEL),
        out_shape=jax.ShapeDtypeStruct((m, D_MODEL), F32),
        compiler_params=_cparams(("parallel",)),
        name="merge",
    )(x, pre_a, attn, y_ssd, proj, proj, proj, proj, ssm_norm, b_gate, w_pc, w_pa, w_ps, w_o)


def _ffn_kernel(x_ref, g_ref, wr_ref, wg_ref, wu_ref, wd_ref, o_ref, h_ref, acc_ref, comb_ref,
                *, routed, n_e, n_f):
    e = pl.program_id(1)
    j = pl.program_id(2)
    tm = x_ref.shape[0]

    @pl.when(jnp.logical_and(e == 0, j == 0))
    def _():
        hf = _rms(x_ref[...], g_ref[...])
        h_ref[...] = hf.astype(BF16)
        acc_ref[...] = jnp.zeros(acc_ref.shape, F32)
        if routed:
            lane = lax.broadcasted_iota(I32, (tm, LANES), 1).astype(F32)
            logits = jnp.dot(hf, wr_ref[...], preferred_element_type=F32, precision=lax.Precision.HIGHEST)
            logits = jnp.where(lane < n_e, logits, -jnp.inf)
            m1 = jnp.max(logits, axis=1, keepdims=True)
            i1 = jnp.min(jnp.where(logits == m1, lane, float(LANES)), axis=1, keepdims=True)
            rest = jnp.where(lane == i1, -jnp.inf, logits)
            m2 = jnp.max(rest, axis=1, keepdims=True)
            i2 = jnp.min(jnp.where(rest == m2, lane, float(LANES)), axis=1, keepdims=True)
            e2 = jnp.exp(m2 - m1)
            den = 1.0 + e2
            comb_ref[...] = jnp.where(lane == i1, 1.0 / den, 0.0) + jnp.where(lane == i2, e2 / den, 0.0)

    h = h_ref[...]
    gate = jnp.dot(h, wg_ref[...], preferred_element_type=F32)
    up = jnp.dot(h, wu_ref[...], preferred_element_type=F32)
    act = _silu(gate) * up
    if routed:
        lane = lax.broadcasted_iota(I32, (tm, LANES), 1)
        act = act * jnp.sum(jnp.where(lane == e, comb_ref[...], 0.0), axis=1, keepdims=True)
    acc_ref[...] += jnp.dot(act.astype(BF16), wd_ref[...], preferred_element_type=F32)

    @pl.when(jnp.logical_and(e == n_e - 1, j == n_f - 1))
    def _():
        o_ref[...] = x_ref[...] + acc_ref[...]


def _ffn(x, g, w_router, w_gate, w_up, w_down, routed):
    m = x.shape[0]
    n_e, _, d_ff = w_gate.shape
    tm = min(m, 1024)
    tf = 256
    n_f = d_ff // tf
    return pl.pallas_call(
        functools.partial(_ffn_kernel, routed=routed, n_e=n_e, n_f=n_f),
        grid=(m // tm, n_e, n_f),
        in_specs=[pl.BlockSpec((tm, D_MODEL), lambda i, e, j: (i, 0)),
                  pl.BlockSpec((1, D_MODEL), lambda i, e, j: (0, 0)),
                  pl.BlockSpec((D_MODEL, LANES), lambda i, e, j: (0, 0)),
                  pl.BlockSpec((None, D_MODEL, tf), lambda i, e, j: (e, 0, j)),
                  pl.BlockSpec((None, D_MODEL, tf), lambda i, e, j: (e, 0, j)),
                  pl.BlockSpec((None, tf, D_MODEL), lambda i, e, j: (e, j, 0))],
        out_specs=pl.BlockSpec((tm, D_MODEL), lambda i, e, j: (i, 0)),
        out_shape=jax.ShapeDtypeStruct((m, D_MODEL), F32),
        scratch_shapes=[pltpu.VMEM((tm, D_MODEL), BF16), pltpu.VMEM((tm, D_MODEL), F32),
                        pltpu.VMEM((tm, LANES), F32)],
        compiler_params=_cparams(("parallel", "arbitrary", "arbitrary")),
        name="moe" if routed else "ffn",
    )(x, g, w_router, w_gate, w_up, w_down)


def _decmix_kernel(cb_ref, cc_ref, cx_ref, xs_ref, bc_ref, dt_ref, sc0_ref, sc1_ref,
                   sx0_ref, sx1_ref, sx2_ref, sb0_ref, sb1_ref, sb2_ref,
                   cw_ref, cwx_ref, cwb_ref, cbx_ref, cbb_ref, dtb_ref, alog_ref,
                   pa_ref, u_ref, xso_ref, bco_ref, dto_ref, ea_ref):
    u = cc_ref[...] * cx_ref[...]
    u_ref[...] = u
    pa_ref[...] = cb_ref[...] * (cw_ref[0:1, :] * sc0_ref[...] + cw_ref[1:2, :] * sc1_ref[...] + cw_ref[2:3, :] * u)
    xso_ref[...] = _silu(cwx_ref[0:1, :] * sx0_ref[...] + cwx_ref[1:2, :] * sx1_ref[...]
                         + cwx_ref[2:3, :] * sx2_ref[...] + cwx_ref[3:4, :] * xs_ref[...] + cbx_ref[...])
    bco_ref[...] = _silu(cwb_ref[0:1, :] * sb0_ref[...] + cwb_ref[1:2, :] * sb1_ref[...]
                         + cwb_ref[2:3, :] * sb2_ref[...] + cwb_ref[3:4, :] * bc_ref[...] + cbb_ref[...])
    dt = _softplus(dt_ref[...] + dtb_ref[...])
    dto_ref[...] = dt
    ea_ref[...] = jnp.exp(dt * (-jnp.exp(alog_ref[...])))


def _decmix(proj, st_conv, st_ssm_conv, conv_w, cw, cbias, dtb, alog):
    nb = proj.shape[0]
    pcol = lambda w, off: pl.BlockSpec((nb, w), lambda i: (0, off // w))
    full = lambda w: pl.BlockSpec((nb, w), lambda i: (0, 0))
    const = lambda r, w, j: pl.BlockSpec((r, w), lambda i: (0, j))
    sx = [st_ssm_conv[:, t, :512] for t in range(3)]
    sb = [st_ssm_conv[:, t, 512:] for t in range(3)]
    return pl.pallas_call(
        _decmix_kernel,
        grid=(1,),
        in_specs=[pcol(512, CB_OFF), pcol(512, CC_OFF), pcol(512, CX_OFF), pcol(512, XS_OFF), pcol(512, BC_OFF),
                  pcol(LANES, DT_OFF)] + [full(512)] * 8
                 + [const(3, 512, 0), const(4, 512, 0), const(4, 512, 1), const(1, 512, 0), const(1, 512, 1),
                    const(1, LANES, 0), const(1, LANES, 0)],
        out_specs=[full(512), full(512), full(512), full(512), full(LANES), full(LANES)],
        out_shape=[jax.ShapeDtypeStruct((nb, 512), F32)] * 4 + [jax.ShapeDtypeStruct((nb, LANES), F32)] * 2,
        compiler_params=_cparams(("arbitrary",)),
        name="decode_mix",
    )(proj, proj, proj, proj, proj, proj, st_conv[:, 0], st_conv[:, 1], *sx, *sb,
      conv_w, cw, cw, cbias, cbias, dtb, alog)


def _decssd_kernel(h_ref, dt_ref, xs_ref, b_ref, c_ref, ea_ref, dsk_ref, y_ref, ho_ref):
    xs = xs_ref[...]
    h_new = h_ref[...] * ea_ref[...] + (xs * dt_ref[...]) * b_ref[...]
    ho_ref[...] = h_new
    y_ref[...] = jnp.sum(h_new * c_ref[...], axis=-1, keepdims=True) + dsk_ref[...] * xs


def _decssd(h0, dt, xs, bh, ch, ea, dsk):
    nb = h0.shape[0]
    blk = lambda a, b: pl.BlockSpec((None, SSM_HEADS, a, b), lambda i: (i, 0, 0, 0))
    return pl.pallas_call(
        _decssd_kernel,
        grid=(nb,),
        in_specs=[blk(SSM_HEAD_DIM, SSM_STATE), blk(1, 1), blk(SSM_HEAD_DIM, 1), blk(1, SSM_STATE),
                  blk(1, SSM_STATE), blk(1, 1), pl.BlockSpec((SSM_HEADS, 1, 1), lambda i: (0, 0, 0))],
        out_specs=[blk(SSM_HEAD_DIM, 1), blk(SSM_HEAD_DIM, SSM_STATE)],
        out_shape=[jax.ShapeDtypeStruct((nb, SSM_HEADS, SSM_HEAD_DIM, 1), F32),
                   jax.ShapeDtypeStruct((nb, SSM_HEADS, SSM_HEAD_DIM, SSM_STATE), F32)],
        compiler_params=_cparams(("parallel",)),
        name="decode_ssd",
    )(h0, dt, xs, bh, ch, ea, dsk)


DECODE_PAGE_GROUP = 16


def _decscore_kernel(pt_ref, qi_ref, w_ref, *refs):
    del pt_ref
    kid_refs, o_ref = refs[:-1], refs[-1]
    kid = jnp.concatenate([r[...].astype(BF16) for r in kid_refs], axis=0)
    s = lax.dot_general(qi_ref[:, 0:IDX_DIM].astype(BF16), kid, NT_DIMS, preferred_element_type=F32)
    sc = jnp.maximum(s, 0.0) * w_ref[:, 0:1]
    o_ref[...] = jnp.sum(sc, axis=0, keepdims=True)


def _page_specs(block, layer, pg):
    zeros = (0,) * (len(block) - 2)
    return [pl.BlockSpec(block, lambda b, p, pt, t=t: (layer, pt[b, p * pg + t]) + zeros) for t in range(pg)]


def _decscore(page_table, qip, wrows, pool_kidx, layer):
    nb, n_pages = page_table.shape
    pg = math.gcd(DECODE_PAGE_GROUP, n_pages)
    grid_spec = pltpu.PrefetchScalarGridSpec(
        num_scalar_prefetch=1,
        grid=(nb, n_pages // pg),
        in_specs=[pl.BlockSpec((None, N_IDX_HEADS, LANES), lambda b, p, pt: (b, 0, 0)),
                  pl.BlockSpec((None, N_IDX_HEADS, PAGE_SIZE), lambda b, p, pt: (b, 0, 0))]
                 + _page_specs((None, None, PAGE_SIZE, IDX_DIM), layer, pg),
        out_specs=pl.BlockSpec((None, None, 1, pg * PAGE_SIZE), lambda b, p, pt: (b, p, 0, 0)),
    )
    out = pl.pallas_call(
        _decscore_kernel,
        grid_spec=grid_spec,
        out_shape=jax.ShapeDtypeStruct((nb, n_pages // pg, 1, pg * PAGE_SIZE), F32),
        compiler_params=_cparams(("parallel", "arbitrary")),
        name="decode_scores",
    )(page_table, qip, wrows, *([pool_kidx] * pg))
    return out.reshape(nb, n_pages * PAGE_SIZE)


def _decselect_kernel(sc_ref, qi_ref, ki_ref, w_ref, sel_ref, self_ref, keys_ref, *, k_sel, kc, nbits):
    nb, past = sc_ref.shape
    nck = past // kc
    lane_k = lax.broadcasted_iota(I32, (nb, kc), 1)
    ki = ki_ref[...].astype(F32)
    own = jnp.zeros((nb, 1), F32)
    for h in range(N_IDX_HEADS):
        s = jnp.sum(qi_ref[:, h * LANES:(h + 1) * LANES].astype(F32) * ki, axis=1, keepdims=True)
        own = own + jnp.maximum(s, 0.0) * w_ref[:, h:h + 1]
    own_key = _sortable(own)

    def key_body(c, carry):
        off = pl.multiple_of(c * kc, kc)
        keys_ref[:, pl.ds(off, kc)] = _sortable(sc_ref[:, pl.ds(off, kc)])
        return carry

    lax.fori_loop(0, nck, key_body, 0)

    def count_where(pred):
        def body(c, acc):
            off = pl.multiple_of(c * kc, kc)
            f = jnp.where(pred(keys_ref[:, pl.ds(off, kc)], off), 1.0, 0.0)
            part = f[:, 0:LANES]
            for t in range(1, kc // LANES):
                part = part + f[:, t * LANES:(t + 1) * LANES]
            return acc + part

        acc = lax.fori_loop(0, nck, body, jnp.zeros((nb, LANES), F32))
        return jnp.sum(acc, axis=1, keepdims=True)

    def bit_body(i, t):
        cand = t + lax.shift_left(jnp.int32(1), 31 - i)
        cnt = count_where(lambda kk, off: kk >= cand) + jnp.where(own_key >= cand, 1.0, 0.0)
        return jnp.where(cnt >= k_sel, cand, t)

    thr = lax.fori_loop(0, 32, bit_body, jnp.full((nb, 1), INT_MIN, I32))
    thr = jnp.maximum(thr, KEY_NEG_INF)
    finite_thr = thr > KEY_NEG_INF
    cnt_gt = count_where(lambda kk, off: kk > thr) + jnp.where(own_key > thr, 1.0, 0.0)
    need = k_sel - cnt_gt

    def xbody(i, x):
        cand = x + lax.shift_left(jnp.int32(1), nbits - 1 - i)
        cnt = count_where(lambda kk, off: jnp.logical_and(kk == thr, off + lane_k < cand))
        return jnp.where(cnt < need, cand, x)

    xcut = lax.fori_loop(0, nbits, xbody, jnp.zeros((nb, 1), I32))
    xcut = jnp.where(finite_thr, xcut, -1)
    ties_past = count_where(lambda kk, off: kk == thr)
    own_tie = jnp.logical_and(jnp.logical_and(own_key == thr, finite_thr), ties_past < need)
    self_ref[...] = jnp.broadcast_to(
        jnp.where(jnp.logical_or(own_key > thr, own_tie), 1.0, 0.0), (nb, LANES))

    def sel_body(c, carry):
        off = pl.multiple_of(c * kc, kc)
        kk = keys_ref[:, pl.ds(off, kc)]
        sel = jnp.logical_or(kk > thr, jnp.logical_and(kk == thr, off + lane_k <= xcut))
        sel_ref[:, pl.ds(off, kc)] = jnp.where(sel, 1.0, 0.0)
        return carry

    lax.fori_loop(0, nck, sel_body, 0)


def _decselect(scores, qip, kib, wi):
    nb, past = scores.shape
    k_sel = min(TOPK_MAX, (past + 1) // 4)
    kc = 512 if past % 512 == 0 else PAGE_SIZE
    nbits = max(1, past.bit_length())
    full = lambda w: pl.BlockSpec((nb, w), lambda i: (0, 0))
    return pl.pallas_call(
        functools.partial(_decselect_kernel, k_sel=k_sel, kc=kc, nbits=nbits),
        grid=(1,),
        in_specs=[full(past), full(N_IDX_HEADS * LANES), full(LANES), full(LANES)],
        out_specs=[full(past), full(LANES)],
        out_shape=[jax.ShapeDtypeStruct((nb, past), F32), jax.ShapeDtypeStruct((nb, LANES), F32)],
        scratch_shapes=[pltpu.VMEM((nb, past), I32)],
        compiler_params=_cparams(("arbitrary",)),
        name="decode_select",
    )(scores, qip, kib, wi)


def _decattn_kernel(pt_ref, qp_ref, sel_ref, kn_ref, vn_ref, self_ref, *refs, n_steps):
    del pt_ref
    pg = (len(refs) - 4) // 2
    kp_refs, vp_refs = refs[:pg], refs[pg:2 * pg]
    o_ref, m_ref, l_ref, acc_ref = refs[2 * pg:]
    p = pl.program_id(1)

    @pl.when(p == 0)
    def _():
        m_ref[...] = jnp.full(m_ref.shape, NEG_BIG, F32)
        l_ref[...] = jnp.zeros(l_ref.shape, F32)
        acc_ref[...] = jnp.zeros(acc_ref.shape, F32)

    q = qp_ref[...]
    qb = q.astype(BF16)
    group = lambda refs, g: jnp.concatenate([r[:, g, :] for r in refs], axis=0).astype(BF16)
    s = sum(lax.dot_general(qb[:, g * HEAD_DIM:(g + 1) * HEAD_DIM], group(kp_refs, g), NT_DIMS,
                            preferred_element_type=F32) for g in range(N_KV_HEADS))
    s = jnp.where(sel_ref[...] > 0.0, s, -jnp.inf)
    m_old = m_ref[:, 0:1]
    m_new = jnp.maximum(m_old, jnp.max(s, axis=1, keepdims=True))
    alpha = jnp.exp(m_old - m_new)
    pr = jnp.exp(s - m_new)
    l_new = alpha * l_ref[:, 0:1] + jnp.sum(pr, axis=1, keepdims=True)
    prb = pr.astype(BF16)
    pv = jnp.concatenate([jnp.dot(prb, group(vp_refs, g), preferred_element_type=F32)
                          for g in range(N_KV_HEADS)], axis=1)
    acc_ref[...] = alpha * acc_ref[...] + pv
    m_ref[...] = jnp.broadcast_to(m_new, m_ref.shape)
    l_ref[...] = jnp.broadcast_to(l_new, l_ref.shape)

    @pl.when(p == n_steps - 1)
    def _():
        s_own = jnp.sum(q * kn_ref[...], axis=1, keepdims=True)
        s_own = jnp.where(self_ref[:, 0:1] > 0.0, s_own, -jnp.inf)
        m_o = m_ref[:, 0:1]
        m_n = jnp.maximum(m_o, s_own)
        al = jnp.exp(m_o - m_n)
        p_own = jnp.exp(s_own - m_n)
        l_n = al * l_ref[:, 0:1] + p_own
        acc = al * acc_ref[...] + p_own.astype(BF16).astype(F32) * vn_ref[...]
        o_ref[...] = acc / l_n


def _decattn(page_table, qp, sel, pool_k, pool_v, kb, vb, self_sel, layer):
    nb, n_pages = page_table.shape
    pg = math.gcd(DECODE_PAGE_GROUP, n_pages)
    n_steps = n_pages // pg
    row = lambda r: pl.BlockSpec((None, r, LANES), lambda b, p, pt: (b, 0, 0))
    pages = _page_specs((None, None, PAGE_SIZE, N_KV_HEADS, HEAD_DIM), layer, pg)
    grid_spec = pltpu.PrefetchScalarGridSpec(
        num_scalar_prefetch=1,
        grid=(nb, n_steps),
        in_specs=[row(N_HEADS),
                  pl.BlockSpec((None, None, 1, pg * PAGE_SIZE), lambda b, p, pt: (b, p, 0, 0)),
                  row(1), row(1), row(1)] + pages + pages,
        out_specs=row(N_HEADS),
        scratch_shapes=[pltpu.VMEM((N_HEADS, LANES), F32)] * 3,
    )
    return pl.pallas_call(
        functools.partial(_decattn_kernel, n_steps=n_steps),
        grid_spec=grid_spec,
        out_shape=jax.ShapeDtypeStruct((nb, N_HEADS, LANES), F32),
        compiler_params=_cparams(("parallel", "arbitrary")),
        name="decode_attend",
    )(page_table, qp, sel.reshape(nb, n_steps, 1, pg * PAGE_SIZE), kb, vb, self_sel,
      *([pool_k] * pg), *([pool_v] * pg))


def _pack_w_in(w):
    d = w.shape[0]
    pad = lambda n: jnp.zeros((d, n), w.dtype)
    cols = [w[:, 4172:7244],
            w[:, 0:2048],
            w[:, 2628:3140],
            w[:, 3140:4164],
            w[:, 2048:2304],
            w[:, 2304:2560],
            w[:, 2560:2628], pad(60),
            w[:, 4164:4172], pad(120),
            pad(N_PROJ - 7424)]
    return jnp.concatenate(cols, axis=1).astype(BF16)


def _pad_lanes(v):
    return jnp.zeros((1, LANES), F32).at[0, :v.shape[0]].set(v)


def _rope_tables(pos):
    half = HEAD_DIM // 2
    inv = ROPE_THETA ** (-jnp.arange(half, dtype=F32) / half)
    ang = pos.astype(F32)[:, None] * inv[None, :]
    cos, sin = jnp.cos(ang), jnp.sin(ang)
    cos2 = jnp.concatenate([cos, cos], axis=1)
    sin2 = jnp.concatenate([-sin, sin], axis=1)
    return jnp.tile(cos2, (1, 2)), jnp.tile(sin2, (1, 2))


def _head_halves(o):
    hpg = N_HEADS // N_KV_HEADS
    parts = [o[:, h, (h // hpg) * HEAD_DIM:(h // hpg + 1) * HEAD_DIM] for h in range(N_HEADS)]
    return jnp.concatenate(parts, axis=-1)


def kernel(x_prompt, x_sample, cache_k, cache_v, cache_kidx, state_conv, state_ssm_conv, state_ssm, page_table,
           norm1, w_in, b_gate, conv_w, q_norm, k_norm, w_pc, w_pa, ssm_conv_w, ssm_conv_b, dt_bias, a_log,
           d_skip, ssm_norm, w_ps, w_o, norm2, w_gate_dense, w_up_dense, w_down_dense, w_router, w_gate_moe,
           w_up_moe, w_down_moe):
    nb, seq, _ = x_prompt.shape
    db = x_sample.shape[0]
    depth = w_in.shape[0]
    n_pages = page_table.shape[1]
    past = n_pages * PAGE_SIZE

    hp = x_prompt.reshape(nb * seq, D_MODEL)
    hs = x_sample.reshape(db, D_MODEL)
    cos_p, sin_p = _rope_tables(jnp.tile(jnp.arange(seq), nb))
    cos_s, sin_s = _rope_tables(jnp.full((db,), past))

    outs_p = [[] for _ in range(6)]
    outs_s = [[] for _ in range(6)]
    for l in range(depth):
        wp = _pack_w_in(w_in[l])
        g1 = norm1[l].reshape(1, D_MODEL)
        g2 = norm2[l].reshape(1, D_MODEL)
        qg = jnp.tile(q_norm[l], 2).reshape(1, LANES)
        kg = jnp.tile(k_norm[l], 2).reshape(1, LANES)
        cw = ssm_conv_w[l]
        cbias = ssm_conv_b[l].reshape(1, SSM_CONV_DIM)
        dtb = _pad_lanes(dt_bias[l])
        alog = _pad_lanes(a_log[l])
        dsk = jnp.repeat(d_skip[l], SSM_HEAD_DIM).reshape(1, SSM_INNER)
        sn = ssm_norm[l].reshape(1, SSM_INNER)
        wpc, wpa, wps, wo = (w.astype(BF16) for w in (w_pc[l], w_pa[l], w_ps[l], w_o[l]))
        i = l // 2
        if l % 2 == 0:
            routed = False
            wr = jnp.zeros((D_MODEL, LANES), F32)
            wg, wu, wd = (w[i:i + 1].astype(BF16) for w in (w_gate_dense, w_up_dense, w_down_dense))
        else:
            routed = True
            wr = jnp.zeros((D_MODEL, LANES), F32).at[:, :N_EXPERTS].set(w_router[i])
            wg, wu, wd = (w[i].astype(BF16) for w in (w_gate_moe, w_up_moe, w_down_moe))

        proj = _inproj(hp, g1, wp)
        qt, kf, kb, vt, qit, kif, kib, wt = _prep(proj, cos_p, sin_p, qg, kg, True)
        attn = _dsa_prompt(qt, qit, wt, kb, vt, kib, nb, seq)
        pre_a, conv_st = _conva_prompt(proj, conv_w[l], nb, seq)
        y_ssd, h_last = _ssd_prompt(proj, cw, cbias, dtb, alog, dsk, nb, seq)
        hp = _merge(hp, pre_a, attn, y_ssd, proj, sn, b_gate[l], wpc, wpa, wps, wo)
        hp = _ffn(hp, g2, wr, wg, wu, wd, routed)
        proj3 = proj.reshape(nb, seq, N_PROJ)
        outs_p[0].append(kf.reshape(nb, seq, N_KV_HEADS, HEAD_DIM))
        outs_p[1].append(proj3[:, :, V_OFF:V_OFF + LANES].reshape(nb, seq, N_KV_HEADS, HEAD_DIM))
        outs_p[2].append(kif[:, :IDX_DIM].reshape(nb, seq, IDX_DIM))
        outs_p[3].append(conv_st)
        outs_p[4].append(proj3[:, seq - 3:, XS_OFF:XS_OFF + SSM_CONV_DIM])
        outs_p[5].append(h_last.reshape(nb, SSM_HEADS, SSM_HEAD_DIM, SSM_STATE))

        proj = _inproj(hs, g1, wp)
        qp, kf, kb, vb, qip, kif, kib, wi = _prep(proj, cos_s, sin_s, qg, kg, False)
        wrows = jnp.broadcast_to(wi[:, :N_IDX_HEADS, None], (db, N_IDX_HEADS, PAGE_SIZE))
        scores = _decscore(page_table, qip.astype(F32).reshape(db, N_IDX_HEADS, LANES), wrows, cache_kidx, l)
        sel, self_sel = _decselect(scores, qip, kib, wi)
        o = _decattn(page_table, qp.astype(F32).reshape(db, N_HEADS, LANES), sel, cache_k, cache_v,
                     kb.astype(F32).reshape(db, 1, LANES), vb.astype(F32).reshape(db, 1, LANES),
                     self_sel.reshape(db, 1, LANES), l)
        attn = _head_halves(o)
        pre_a, u, xs, bcv, dt, ea = _decmix(proj, state_conv[l], state_ssm_conv[l], conv_w[l], cw, cbias, dtb, alog)
        hpg = SSM_HEADS // SSM_GROUPS
        xs4 = xs.reshape(db, SSM_HEADS, SSM_HEAD_DIM, 1)
        bh = jnp.repeat(bcv[:, :SSM_GROUPS * SSM_STATE].reshape(db, SSM_GROUPS, 1, SSM_STATE), hpg, axis=1)
        ch = jnp.repeat(bcv[:, SSM_GROUPS * SSM_STATE:].reshape(db, SSM_GROUPS, 1, SSM_STATE), hpg, axis=1)
        y4, h_new = _decssd(state_ssm[l], dt[:, :SSM_HEADS, None, None], xs4, bh, ch, ea[:, :SSM_HEADS, None, None],
                            d_skip[l].reshape(SSM_HEADS, 1, 1))
        hs = _merge(hs, pre_a, attn, y4.reshape(db, SSM_INNER), proj, sn, b_gate[l], wpc, wpa, wps, wo)
        hs = _ffn(hs, g2, wr, wg, wu, wd, routed)
        outs_s[0].append(kf.reshape(db, 1, N_KV_HEADS, HEAD_DIM))
        outs_s[1].append(proj[:, V_OFF:V_OFF + LANES].reshape(db, 1, N_KV_HEADS, HEAD_DIM))
        outs_s[2].append(kif[:, :IDX_DIM].reshape(db, 1, IDX_DIM))
        outs_s[3].append(jnp.stack([state_conv[l][:, 1], u], axis=1))
        outs_s[4].append(jnp.concatenate(
            [state_ssm_conv[l][:, 1:], proj[:, None, XS_OFF:XS_OFF + SSM_CONV_DIM]], axis=1))
        outs_s[5].append(h_new)

    return (hp.reshape(nb, seq, D_MODEL), hs.reshape(db, 1, D_MODEL),
            *(jnp.stack(o) for o in outs_p), *(jnp.stack(o) for o in outs_s))
```

```python
import functools
import math

import jax
import jax.numpy as jnp
from jax import lax
from jax.experimental import pallas as pl
from jax.experimental.pallas import tpu as pltpu

F32 = jnp.float32
BF16 = jnp.bfloat16
I32 = jnp.int32

D_MODEL = 1024
D_CONV = 512
N_HEADS = 8
HEAD_DIM = 64
N_KV_HEADS = 2
N_IDX_HEADS = 4
IDX_DIM = 64
TOPK_MAX = 256
Q_BLOCK = 128
ROPE_THETA = 10000.0
SSM_INNER = 512
SSM_HEADS = 8
SSM_HEAD_DIM = 64
SSM_GROUPS = 2
SSM_STATE = 128
SSM_CHUNK = 128
SSM_CONV_DIM = 1024
PAGE_SIZE = 128
N_EXPERTS = 8
EPS = 1e-6

LANES = 128
SUBLANES = 8
VMEM_LIMIT = 52 * 1024 * 1024

G_OFF, CB_OFF, CC_OFF, CX_OFF, Q_OFF, Z_OFF, XS_OFF, BC_OFF = 0, 3072, 3584, 4096, 4608, 5120, 5632, 6144
K_OFF, V_OFF, QI_OFF, KW_OFF, DT_OFF = 6656, 6784, 6912, 7168, 7296
N_PROJ = 7680

INT_MIN = -2 ** 31
INT_MAX = 2 ** 31 - 1
KEY_NEG_INF = (-8388608) ^ 0x7FFFFFFF
NEG_BIG = -1e30
LOG2_E = 1.4426950408889634

NT_DIMS = (((1,), (1,)), ((), ()))


def _cparams(sem):
    return pltpu.CompilerParams(dimension_semantics=sem, vmem_limit_bytes=VMEM_LIMIT)


def _sigmoid(x):
    return 1.0 / (1.0 + jnp.exp(-x))


def _silu(x):
    return x * _sigmoid(x)


def _softplus(x):
    return jnp.maximum(x, 0.0) + jnp.log1p(jnp.exp(-jnp.abs(x)))


def _sortable(x):
    bits = pltpu.bitcast(x, I32)
    return bits ^ (lax.shift_right_arithmetic(bits, 31) & 0x7FFFFFFF)


def _rms(x, g):
    return x * lax.rsqrt(jnp.mean(x * x, axis=-1, keepdims=True) + EPS) * g


def _inproj_kernel(x_ref, g_ref, w_ref, o_ref, h_ref):
    @pl.when(pl.program_id(1) == 0)
    def _():
        h_ref[...] = _rms(x_ref[...], g_ref[...]).astype(BF16)

    o_ref[...] = jnp.dot(h_ref[...], w_ref[...], preferred_element_type=F32)


def _inproj(x, g, w):
    m = x.shape[0]
    tm = min(m, 1024)
    tn = 1536
    return pl.pallas_call(
        _inproj_kernel,
        grid=(m // tm, N_PROJ // tn),
        in_specs=[pl.BlockSpec((tm, D_MODEL), lambda i, j: (i, 0)),
                  pl.BlockSpec((1, D_MODEL), lambda i, j: (0, 0)),
                  pl.BlockSpec((D_MODEL, tn), lambda i, j: (0, j))],
        out_specs=pl.BlockSpec((tm, tn), lambda i, j: (i, j)),
        out_shape=jax.ShapeDtypeStruct((m, N_PROJ), F32),
        scratch_shapes=[pltpu.VMEM((tm, D_MODEL), BF16)],
        compiler_params=_cparams(("parallel", "arbitrary")),
        name="inproj",
    )(x, g, w)


V_T_ROWS = LANES + 16


def _prep_kernel(q_ref, k_ref, v_ref, qi_ref, kw_ref, cos_ref, sin_ref, qg_ref, kg_ref,
                 qp_ref, kf_ref, kb_ref, vb_ref, qip_ref, kif_ref, kib_ref, wo_ref, *, transposed):
    tm = cos_ref.shape[0]

    def put(ref, h, val):
        if transposed:
            ref[h * LANES:(h + 1) * LANES, :] = val.T.astype(BF16)
        else:
            ref[:, h * LANES:(h + 1) * LANES] = val.astype(BF16)

    cos = cos_ref[...]
    sin = sin_ref[...]
    lane = lax.broadcasted_iota(I32, (tm, LANES), 1)
    low_half = lane < HEAD_DIM
    first_rot = (lane % HEAD_DIM) < (HEAD_DIM // 2)
    r = lax.broadcasted_iota(I32, (LANES, LANES), 0) // HEAD_DIM
    c = lax.broadcasted_iota(I32, (LANES, LANES), 1) // HEAD_DIM
    seg = jnp.where(r == c, 1.0 / HEAD_DIM, 0.0).astype(BF16)

    def rope(x):
        fwd = pltpu.roll(x, LANES - HEAD_DIM // 2, 1)
        bwd = pltpu.roll(x, HEAD_DIM // 2, 1)
        return x * cos + jnp.where(first_rot, fwd, bwd) * sin

    def headnorm(x, g):
        s = x * x
        hi = s.astype(BF16)
        lo = (s - hi.astype(F32)).astype(BF16)
        ms = (jnp.dot(hi, seg, preferred_element_type=F32)
              + jnp.dot(lo, seg, preferred_element_type=F32))
        return x * lax.rsqrt(ms + EPS) * g

    q_scale = HEAD_DIM ** -0.5 * (LOG2_E if transposed else 1.0)
    qg = qg_ref[...]
    for s in range(N_HEADS // 2):
        slab = rope(headnorm(q_ref[:, s * LANES:(s + 1) * LANES], qg)) * q_scale
        swapped = pltpu.roll(slab, HEAD_DIM, 1)
        grp = (2 * s) // (N_HEADS // N_KV_HEADS)
        for hh in range(2):
            h = 2 * s + hh
            src = slab if hh == grp else swapped
            keep = low_half if grp == 0 else jnp.logical_not(low_half)
            put(qp_ref, h, jnp.where(keep, src, 0.0))

    k = rope(headnorm(k_ref[...], kg_ref[...]))
    kf_ref[...] = k
    kb_ref[...] = k.astype(BF16)
    if transposed:
        vb_ref[0:LANES, :] = v_ref[...].T.astype(BF16)
        vb_ref[LANES:V_T_ROWS, :] = jnp.ones((V_T_ROWS - LANES, tm), BF16)
    else:
        vb_ref[...] = v_ref[...].astype(BF16)

    for s in range(N_IDX_HEADS // 2):
        slab = rope(qi_ref[:, s * LANES:(s + 1) * LANES]) * (IDX_DIM ** -0.5)
        swapped = pltpu.roll(slab, HEAD_DIM, 1)
        for hh in range(2):
            h = 2 * s + hh
            src = slab if hh == 0 else swapped
            put(qip_ref, h, jnp.where(low_half, src, 0.0))

    kw = kw_ref[...]
    ki = rope(kw)
    kif_ref[...] = ki
    kib_ref[...] = jnp.where(low_half, ki, 0.0).astype(BF16)
    wi = pltpu.roll(kw, HEAD_DIM, 1) * (N_IDX_HEADS ** -0.5)
    if transposed:
        wo_ref[...] = wi.T[0:SUBLANES, :]
    else:
        wo_ref[...] = wi


def _prep(proj, cos, sin, qg, kg, transposed):
    m = proj.shape[0]
    tm = min(m, 512)
    row = lambda w, off: pl.BlockSpec((tm, w), lambda i: (i, off // w))
    full = lambda w: pl.BlockSpec((tm, w), lambda i: (i, 0))
    const = pl.BlockSpec((1, LANES), lambda i: (0, 0))
    if transposed:
        feat = lambda r: pl.BlockSpec((r, tm), lambda i: (0, i))
        fshape = lambda r, dt: jax.ShapeDtypeStruct((r, m), dt)
    else:
        feat = full
        fshape = lambda r, dt: jax.ShapeDtypeStruct((m, r), dt)
    return pl.pallas_call(
        functools.partial(_prep_kernel, transposed=transposed),
        grid=(m // tm,),
        in_specs=[row(512, Q_OFF), row(LANES, K_OFF), row(LANES, V_OFF), row(256, QI_OFF), row(LANES, KW_OFF),
                  full(LANES), full(LANES), const, const],
        out_specs=[feat(N_HEADS * LANES), full(LANES), full(LANES), feat(V_T_ROWS if transposed else LANES),
                   feat(N_IDX_HEADS * LANES), full(LANES), full(LANES), feat(SUBLANES if transposed else LANES)],
        out_shape=[fshape(N_HEADS * LANES, BF16),
                   jax.ShapeDtypeStruct((m, LANES), F32),
                   jax.ShapeDtypeStruct((m, LANES), BF16),
                   fshape(V_T_ROWS if transposed else LANES, BF16),
                   fshape(N_IDX_HEADS * LANES, BF16),
                   jax.ShapeDtypeStruct((m, LANES), F32),
                   jax.ShapeDtypeStruct((m, LANES), BF16),
                   fshape(SUBLANES if transposed else LANES, F32)],
        compiler_params=_cparams(("parallel",)),
        name="prep",
    )(proj, proj, proj, proj, proj, cos, sin, qg, kg)


def _fold_rows(f):
    while f.shape[0] > SUBLANES:
        half = f.shape[0] // 2
        f = f[:half] + f[half:]
    return f


def _dsa_kernel(qt_ref, qit_ref, wt_ref, k_ref, vt_ref, ki_ref, o_ref,
                keys_ref, x_ref, m_ref, kmax_ref, acc_ref, *, k_sel, kc, nbits):
    qb = Q_BLOCK
    j = pl.program_id(1)
    nck = (j * qb + qb + kc - 1) // kc
    qpos = j * qb + lax.broadcasted_iota(I32, (1, qb), 1)
    sub_k = lax.broadcasted_iota(I32, (kc, qb), 0)
    pair = lambda ref, s: jnp.concatenate(
        [ref[(2 * s) * LANES:(2 * s + 1) * LANES, :], ref[(2 * s + 1) * LANES:(2 * s + 2) * LANES, :]], axis=1)
    qi2 = [pair(qit_ref, s) for s in range(N_IDX_HEADS // 2)]
    q2 = [pair(qt_ref, s) for s in range(N_HEADS // 2)]

    def score_body(c, carry):
        off = pl.multiple_of(c * kc, kc)
        kic = ki_ref[pl.ds(off, kc), :]
        acc = jnp.zeros((kc, qb), F32)
        for s in range(N_IDX_HEADS // 2):
            s2 = jnp.dot(kic, qi2[s], preferred_element_type=F32)
            for hh in range(2):
                h = 2 * s + hh
                acc = acc + jnp.maximum(s2[:, hh * qb:(hh + 1) * qb], 0.0) * wt_ref[h:h + 1, :]
        acc = jnp.where(off + sub_k <= qpos, acc, -jnp.inf)
        keys_ref[pl.ds(off, kc), :] = _sortable(acc)
        return carry

    lax.fori_loop(0, nck, score_body, 0)

    def count_where(pred):
        def body(c, acc):
            off = pl.multiple_of(c * kc, kc)
            return acc + _fold_rows(jnp.where(pred(keys_ref[pl.ds(off, kc), :], off), 1.0, 0.0))

        acc = lax.fori_loop(0, nck, body, jnp.zeros((SUBLANES, qb), F32))
        return jnp.sum(acc, axis=0, keepdims=True)

    def bit_body(i, t):
        cand = t + lax.shift_left(jnp.int32(1), 31 - i)
        cnt = count_where(lambda kk, off: kk >= cand)
        return jnp.where(cnt >= k_sel, cand, t)

    thr = lax.fori_loop(0, 32, bit_body, jnp.full((1, qb), INT_MIN, I32))
    thr = jnp.maximum(thr, KEY_NEG_INF)
    finite_thr = thr > KEY_NEG_INF
    cnt_ge = count_where(lambda kk, off: kk >= thr)
    cnt_gt = count_where(lambda kk, off: kk > thr)
    need = k_sel - cnt_gt
    tie = jnp.logical_and(cnt_ge > k_sel, finite_thr)
    x_default = jnp.where(finite_thr, INT_MAX, -1)
    x_ref[...] = jnp.broadcast_to(x_default, x_ref.shape)

    @pl.when(jnp.max(jnp.where(tie, 1.0, 0.0)) > 0.0)
    def _():
        def xbody(i, x):
            cand = x + lax.shift_left(jnp.int32(1), nbits - 1 - i)
            cnt = count_where(lambda kk, off: jnp.logical_and(kk == thr, off + sub_k < cand))
            return jnp.where(cnt < need, cand, x)

        x = lax.fori_loop(0, nbits, xbody, jnp.zeros((1, qb), I32))
        x_ref[...] = jnp.broadcast_to(jnp.where(tie, x, x_default), x_ref.shape)

    xcut = x_ref[0:1, :]
    hpg = N_HEADS // N_KV_HEADS

    @pl.when(j == 0)
    def _():
        r = lax.broadcasted_iota(I32, (LANES, LANES), 0) // HEAD_DIM
        cc = lax.broadcasted_iota(I32, (LANES, LANES), 1) // HEAD_DIM
        seg = jnp.where(r == cc, 1.0, 0.0).astype(BF16)

        def body(c, mx):
            kf = k_ref[pl.ds(pl.multiple_of(c * kc, kc), kc), :].astype(F32)
            gs = jnp.dot((kf * kf).astype(BF16), seg, preferred_element_type=F32)
            while gs.shape[0] > SUBLANES:
                half = gs.shape[0] // 2
                gs = jnp.maximum(gs[:half], gs[half:])
            return jnp.maximum(mx, gs)

        mx = lax.fori_loop(0, k_ref.shape[0] // kc, body, jnp.zeros((SUBLANES, LANES), F32))
        mx = jnp.max(mx, axis=0, keepdims=True)
        lane = lax.broadcasted_iota(I32, (1, LANES), 1)
        other = pltpu.roll(mx, HEAD_DIM, 1)
        kmax_ref[0:1, :] = jnp.where(lane < HEAD_DIM, mx, other)
        kmax_ref[1:2, :] = jnp.where(lane < HEAD_DIM, other, mx)

    def chunk(c):
        off = pl.multiple_of(c * kc, kc)
        kk = keys_ref[pl.ds(off, kc), :]
        sel = jnp.logical_or(kk > thr, jnp.logical_and(kk == thr, off + sub_k <= xcut))
        return sel, k_ref[pl.ds(off, kc), :], vt_ref[:, pl.ds(off, kc)]

    bound = []
    for h in range(N_HEADS):
        qh = qt_ref[h * LANES:(h + 1) * LANES, :].astype(F32)
        qsq = jnp.sum(qh * qh, axis=0, keepdims=True)
        bound.append(jnp.sqrt(qsq * kmax_ref[h // hpg:h // hpg + 1, :]))
    acc_ref[...] = jnp.zeros(acc_ref.shape, F32)

    def fast_body(c, carry):
        sel, kch, vch = chunk(c)
        n_pair = N_HEADS // 2
        s2 = jnp.dot(kch, q2[0], preferred_element_type=F32)
        for s in range(n_pair):
            s2_next = jnp.dot(kch, q2[s + 1], preferred_element_type=F32) if s + 1 < n_pair else None
            ps = [jnp.exp2(jnp.where(sel, s2[:, hh * qb:(hh + 1) * qb] - bound[2 * s + hh], -jnp.inf)).astype(BF16)
                  for hh in range(2)]
            acc_ref[s] += jnp.dot(vch, jnp.concatenate(ps, axis=1), preferred_element_type=F32)
            s2 = s2_next
        return carry

    lax.fori_loop(0, nck, fast_body, 0)

    dens = jnp.concatenate([acc_ref[s, LANES:LANES + 1, :] for s in range(N_HEADS // 2)], axis=0)

    @pl.when(jnp.logical_not(jnp.min(dens) >= 1e-20))
    def _():
        m_ref[...] = jnp.full(m_ref.shape, NEG_BIG, F32)
        acc_ref[...] = jnp.zeros(acc_ref.shape, F32)

        def exact_body(c, carry):
            sel, kch, vch = chunk(c)
            for s in range(N_HEADS // 2):
                s2 = jnp.dot(kch, q2[s], preferred_element_type=F32)
                ps, alphas = [], []
                for hh in range(2):
                    h = 2 * s + hh
                    sh = jnp.where(sel, s2[:, hh * qb:(hh + 1) * qb], -jnp.inf)
                    m_old = m_ref[h:h + 1, :]
                    m_new = jnp.maximum(m_old, jnp.max(sh, axis=0, keepdims=True))
                    m_ref[h:h + 1, :] = m_new
                    alphas.append(jnp.exp2(m_old - m_new))
                    ps.append(jnp.exp2(sh - m_new).astype(BF16))
                o2 = jnp.dot(vch, jnp.concatenate(ps, axis=1), preferred_element_type=F32)
                acc_ref[s] = jnp.concatenate(alphas, axis=1) * acc_ref[s] + o2
            return carry

        lax.fori_loop(0, nck, exact_body, 0)

    for s in range(N_HEADS // 2):
        g = (2 * s) // hpg
        num = acc_ref[s, g * HEAD_DIM:(g + 1) * HEAD_DIM, :]
        out2 = num / acc_ref[s, LANES:LANES + 1, :]
        o_ref[:, s * LANES:(s + 1) * LANES] = jnp.concatenate([out2[:, :qb], out2[:, qb:]], axis=0).T


def _dsa_prompt(qt, qit, wt, kb, vt, kib, nb, seq):
    k_sel = min(TOPK_MAX, seq // 4)
    kc = next(c for c in (1024, 512, Q_BLOCK) if seq % c == 0)
    nbits = max(1, (seq - 1).bit_length())
    nq = seq // Q_BLOCK
    r3 = lambda a: a.reshape(nb, seq, a.shape[-1])
    qcol = lambda r: pl.BlockSpec((r, Q_BLOCK), lambda b, j: (0, b * nq + j))
    seqblk = pl.BlockSpec((None, seq, LANES), lambda b, j: (b, 0, 0))
    out = pl.pallas_call(
        functools.partial(_dsa_kernel, k_sel=k_sel, kc=kc, nbits=nbits),
        grid=(nb, nq),
        in_specs=[qcol(N_HEADS * LANES), qcol(N_IDX_HEADS * LANES), qcol(SUBLANES), seqblk,
                  pl.BlockSpec((V_T_ROWS, seq), lambda b, j: (0, b)), seqblk],
        out_specs=pl.BlockSpec((None, Q_BLOCK, N_HEADS * HEAD_DIM), lambda b, j: (b, j, 0)),
        out_shape=jax.ShapeDtypeStruct((nb, seq, N_HEADS * HEAD_DIM), F32),
        scratch_shapes=[pltpu.VMEM((seq, Q_BLOCK), I32),
                        pltpu.VMEM((SUBLANES, Q_BLOCK), I32),
                        pltpu.VMEM((N_HEADS, Q_BLOCK), F32),
                        pltpu.VMEM((SUBLANES, LANES), F32),
                        pltpu.VMEM((N_HEADS // 2, V_T_ROWS, 2 * Q_BLOCK), F32)],
        compiler_params=_cparams(("arbitrary", "arbitrary")),
        name="dsa_prompt",
    )(qt, qit, wt, r3(kb), vt, r3(kib))
    return out.reshape(nb * seq, N_HEADS * HEAD_DIM)


def _conva_kernel(cb_ref, cc_ref, cx_ref, cch_ref, cxh_ref, w_ref, o_ref, st_ref, *, tiles_per_seq):
    tm = cb_ref.shape[0]
    first = (pl.program_id(0) % tiles_per_seq) == 0
    u = cc_ref[...] * cx_ref[...]
    uh = jnp.where(first, 0.0, cch_ref[...] * cxh_ref[...])
    ext = jnp.concatenate([uh, u], axis=0)
    conv = (w_ref[2:3, :] * u + w_ref[1:2, :] * ext[SUBLANES - 1:SUBLANES - 1 + tm]
            + w_ref[0:1, :] * ext[SUBLANES - 2:SUBLANES - 2 + tm])
    o_ref[...] = cb_ref[...] * conv
    st_ref[...] = u[tm - 2:tm, :]


def _conva_prompt(proj, conv_w, nb, seq):
    m = proj.shape[0]
    tm = min(seq, 512)
    tps = seq // tm
    col = lambda off: pl.BlockSpec((tm, D_CONV), lambda i: (i, off // D_CONV))
    halo = lambda off: pl.BlockSpec(
        (SUBLANES, D_CONV), lambda i: (jnp.maximum(i * (tm // SUBLANES) - 1, 0), off // D_CONV))
    return pl.pallas_call(
        functools.partial(_conva_kernel, tiles_per_seq=tps),
        grid=(m // tm,),
        in_specs=[col(CB_OFF), col(CC_OFF), col(CX_OFF), halo(CC_OFF), halo(CX_OFF),
                  pl.BlockSpec((3, D_CONV), lambda i: (0, 0))],
        out_specs=[pl.BlockSpec((tm, D_CONV), lambda i: (i, 0)),
                   pl.BlockSpec((None, 2, D_CONV), lambda i: (i // tps, 0, 0))],
        out_shape=[jax.ShapeDtypeStruct((m, D_CONV), F32),
                   jax.ShapeDtypeStruct((nb, 2, D_CONV), F32)],
        compiler_params=_cparams(("arbitrary",)),
        name="conva_prompt",
    )(proj, proj, proj, proj, proj, conv_w)


def _ssd_kernel(xs_ref, bc_ref, dt_ref, xh_ref, bh_ref, cwx_ref, cwb_ref, cbx_ref, cbb_ref,
                dtb_ref, alog_ref, dsk_ref, y_ref, hout_ref, h_ref, *, nchunk):
    cl = SSM_CHUNK
    c = pl.program_id(1)
    first = c == 0

    @pl.when(first)
    def _():
        h_ref[...] = jnp.zeros(h_ref.shape, F32)

    def conv(cur, halo, w_ref, b_ref):
        ext = jnp.concatenate([jnp.where(first, 0.0, halo), cur], axis=0)
        out = (w_ref[3:4, :] * cur + w_ref[2:3, :] * ext[SUBLANES - 1:SUBLANES - 1 + cl]
               + w_ref[1:2, :] * ext[SUBLANES - 2:SUBLANES - 2 + cl]
               + w_ref[0:1, :] * ext[SUBLANES - 3:SUBLANES - 3 + cl] + b_ref[...])
        return _silu(out)

    xs = conv(xs_ref[...], xh_ref[...], cwx_ref, cbx_ref)
    bc = conv(bc_ref[...], bh_ref[...], cwb_ref, cbb_ref)
    dt = _softplus(dt_ref[...] + dtb_ref[...])
    a = dt * (-jnp.exp(alog_ref[...]))
    ri = lax.broadcasted_iota(I32, (cl, cl), 0)
    ci = lax.broadcasted_iota(I32, (cl, cl), 1)
    causal = ri >= ci
    cs = jnp.dot(jnp.where(causal, 1.0, 0.0), a, preferred_element_type=F32,
                 precision=lax.Precision.HIGHEST)
    cs_t = cs.T
    lane = lax.broadcasted_iota(I32, (cl, LANES), 1)
    lo = lane < SSM_HEAD_DIM
    rows_lo = lax.broadcasted_iota(I32, (LANES, 1), 0) < SSM_HEAD_DIM
    heads_per_group = SSM_HEADS // SSM_GROUPS

    cb = []
    for g in range(SSM_GROUPS):
        bg = bc[:, g * SSM_STATE:(g + 1) * SSM_STATE].astype(BF16)
        cg = bc[:, (SSM_GROUPS + g) * SSM_STATE:(SSM_GROUPS + g + 1) * SSM_STATE].astype(BF16)
        cb.append((bg, cg, lax.dot_general(cg, bg, NT_DIMS, preferred_element_type=F32)))

    for s in range(SSM_HEADS // 2):
        h0, h1 = 2 * s, 2 * s + 1
        bg, cg, cbg = cb[h0 // heads_per_group]
        sl = slice(s * LANES, (s + 1) * LANES)
        xs_s = xs[:, sl]
        col0, col1 = cs[:, h0:h0 + 1], cs[:, h1:h1 + 1]
        last0, last1 = cs[cl - 1:cl, h0:h0 + 1], cs[cl - 1:cl, h1:h1 + 1]
        xdt = xs_s * jnp.where(lo, dt[:, h0:h0 + 1], dt[:, h1:h1 + 1])
        xdt_b = xdt.astype(BF16)
        m0 = (cbg * jnp.where(causal, jnp.exp(col0 - cs_t[h0:h0 + 1, :]), 0.0)).astype(BF16)
        m1 = (cbg * jnp.where(causal, jnp.exp(col1 - cs_t[h1:h1 + 1, :]), 0.0)).astype(BF16)
        y_diag = jnp.where(lo, jnp.dot(m0, xdt_b, preferred_element_type=F32),
                           jnp.dot(m1, xdt_b, preferred_element_type=F32))
        hs = h_ref[sl, :]
        y_off = lax.dot_general(cg, hs.astype(BF16), NT_DIMS, preferred_element_type=F32)
        y_off = y_off * jnp.where(lo, jnp.exp(col0), jnp.exp(col1))
        y_ref[:, sl] = y_diag + y_off + dsk_ref[:, sl] * xs_s
        xw = xdt * jnp.where(lo, jnp.exp(last0 - col0), jnp.exp(last1 - col1))
        st = jnp.dot(xw.T.astype(BF16), bg, preferred_element_type=F32)
        h_ref[sl, :] = hs * jnp.where(rows_lo, jnp.exp(last0), jnp.exp(last1)) + st

    @pl.when(c == nchunk - 1)
    def _():
        hout_ref[...] = h_ref[...]


def _ssd_prompt(proj, cw, cbias, dtb, alog, dsk, nb, seq):
    m = proj.shape[0]
    cl = SSM_CHUNK
    nchunk = seq // cl
    blk = lambda w, off: pl.BlockSpec((cl, w), lambda b, c: (b * nchunk + c, off // w))
    halo = lambda off: pl.BlockSpec(
        (SUBLANES, 512), lambda b, c: (jnp.maximum((b * nchunk + c) * (cl // SUBLANES) - 1, 0), off // 512))
    const = lambda r, w, j: pl.BlockSpec((r, w), lambda b, c: (0, j))
    y, hout = pl.pallas_call(
        functools.partial(_ssd_kernel, nchunk=nchunk),
        grid=(nb, nchunk),
        in_specs=[blk(512, XS_OFF), blk(512, BC_OFF), blk(LANES, DT_OFF), halo(XS_OFF), halo(BC_OFF),
                  const(4, 512, 0), const(4, 512, 1), const(1, 512, 0), const(1, 512, 1),
                  const(1, LANES, 0), const(1, LANES, 0), const(1, 512, 0)],
        out_specs=[pl.BlockSpec((cl, SSM_INNER), lambda b, c: (b * nchunk + c, 0)),
                   pl.BlockSpec((None, SSM_INNER, SSM_STATE), lambda b, c: (b, 0, 0))],
        out_shape=[jax.ShapeDtypeStruct((m, SSM_INNER), F32),
                   jax.ShapeDtypeStruct((nb, SSM_INNER, SSM_STATE), F32)],
        scratch_shapes=[pltpu.VMEM((SSM_INNER, SSM_STATE), F32)],
        compiler_params=_cparams(("parallel", "arbitrary")),
        name="ssd_prompt",
    )(proj, proj, proj, proj, proj, cw, cw, cbias, cbias, dtb, alog, dsk)
    return y, hout


def _merge_kernel(x_ref, pa_ref, at_ref, ys_ref, z_ref, g0_ref, g1_ref, g2_ref, sn_ref, bg_ref,
                  wpc_ref, wpa_ref, wps_ref, wo_ref, o_ref):
    ssd = _rms(ys_ref[...] * _silu(z_ref[...]), sn_ref[...])
    ya = jnp.dot(pa_ref[...].astype(BF16), wpc_ref[...], preferred_element_type=F32)
    yb = jnp.dot(at_ref[...].astype(BF16), wpa_ref[...], preferred_element_type=F32)
    yc = jnp.dot(ssd.astype(BF16), wps_ref[...], preferred_element_type=F32)
    merged = (_sigmoid(g0_ref[...] + bg_ref[0:1, :]) * ya + _sigmoid(g1_ref[...] + bg_ref[1:2, :]) * yb
              + _sigmoid(g2_ref[...] + bg_ref[2:3, :]) * yc)
    o_ref[...] = x_ref[...] + jnp.dot(merged.astype(BF16), wo_ref[...], preferred_element_type=F32)


def _merge(x, pre_a, attn, y_ssd, proj, ssm_norm, b_gate, w_pc, w_pa, w_ps, w_o):
    m = x.shape[0]
    tm = min(m, 256)
    row = lambda w: pl.BlockSpec((tm, w), lambda i: (i, 0))
    pcol = lambda w, off: pl.BlockSpec((tm, w), lambda i: (i, off // w))
    const = lambda r, w: pl.BlockSpec((r, w), lambda i: (0, 0))
    return pl.pallas_call(
        _merge_kernel,
        grid=(m // tm,),
        in_specs=[row(D_MODEL), row(512), row(512), row(512), pcol(512, Z_OFF),
                  pcol(D_MODEL, G_OFF), pcol(D_MODEL, G_OFF + D_MODEL), pcol(D_MODEL, G_OFF + 2 * D_MODEL),
                  const(1, 512), const(3, D_MODEL),
                  const(512, D_MODEL), const(512, D_MODEL), const(512, D_MODEL), const(D_MODEL, D_MODEL)],
        out_specs=row(D_MODEL),
        out_shape=jax.ShapeDtypeStruct((m, D_MODEL), F32),
        compiler_params=_cparams(("parallel",)),
        name="merge",
    )(x, pre_a, attn, y_ssd, proj, proj, proj, proj, ssm_norm, b_gate, w_pc, w_pa, w_ps, w_o)


def _ffn_kernel(x_ref, g_ref, wr_ref, wg_ref, wu_ref, wd_ref, o_ref, h_ref, acc_ref, comb_ref,
                *, routed, n_e, n_f):
    e = pl.program_id(1)
    j = pl.program_id(2)
    tm = x_ref.shape[0]

    @pl.when(jnp.logical_and(e == 0, j == 0))
    def _():
        hf = _rms(x_ref[...], g_ref[...])
        h_ref[...] = hf.astype(BF16)
        acc_ref[...] = jnp.zeros(acc_ref.shape, F32)
        if routed:
            lane = lax.broadcasted_iota(I32, (tm, LANES), 1).astype(F32)
            logits = jnp.dot(hf, wr_ref[...], preferred_element_type=F32, precision=lax.Precision.HIGHEST)
            logits = jnp.where(lane < n_e, logits, -jnp.inf)
            m1 = jnp.max(logits, axis=1, keepdims=True)
            i1 = jnp.min(jnp.where(logits == m1, lane, float(LANES)), axis=1, keepdims=True)
            rest = jnp.where(lane == i1, -jnp.inf, logits)
            m2 = jnp.max(rest, axis=1, keepdims=True)
            i2 = jnp.min(jnp.where(rest == m2, lane, float(LANES)), axis=1, keepdims=True)
            e2 = jnp.exp(m2 - m1)
            den = 1.0 + e2
            comb_ref[...] = jnp.where(lane == i1, 1.0 / den, 0.0) + jnp.where(lane == i2, e2 / den, 0.0)

    h = h_ref[...]
    gate = jnp.dot(h, wg_ref[...], preferred_element_type=F32)
    up = jnp.dot(h, wu_ref[...], preferred_element_type=F32)
    act = _silu(gate) * up
    if routed:
        lane = lax.broadcasted_iota(I32, (tm, LANES), 1)
        act = act * jnp.sum(jnp.where(lane == e, comb_ref[...], 0.0), axis=1, keepdims=True)
    acc_ref[...] += jnp.dot(act.astype(BF16), wd_ref[...], preferred_element_type=F32)

    @pl.when(jnp.logical_and(e == n_e - 1, j == n_f - 1))
    def _():
        o_ref[...] = x_ref[...] + acc_ref[...]


def _ffn(x, g, w_router, w_gate, w_up, w_down, routed):
    m = x.shape[0]
    n_e, _, d_ff = w_gate.shape
    tm = min(m, 1024)
    tf = 256
    n_f = d_ff // tf
    return pl.pallas_call(
        functools.partial(_ffn_kernel, routed=routed, n_e=n_e, n_f=n_f),
        grid=(m // tm, n_e, n_f),
        in_specs=[pl.BlockSpec((tm, D_MODEL), lambda i, e, j: (i, 0)),
                  pl.BlockSpec((1, D_MODEL), lambda i, e, j: (0, 0)),
                  pl.BlockSpec((D_MODEL, LANES), lambda i, e, j: (0, 0)),
                  pl.BlockSpec((None, D_MODEL, tf), lambda i, e, j: (e, 0, j)),
                  pl.BlockSpec((None, D_MODEL, tf), lambda i, e, j: (e, 0, j)),
                  pl.BlockSpec((None, tf, D_MODEL), lambda i, e, j: (e, j, 0))],
        out_specs=pl.BlockSpec((tm, D_MODEL), lambda i, e, j: (i, 0)),
        out_shape=jax.ShapeDtypeStruct((m, D_MODEL), F32),
        scratch_shapes=[pltpu.VMEM((tm, D_MODEL), BF16), pltpu.VMEM((tm, D_MODEL), F32),
                        pltpu.VMEM((tm, LANES), F32)],
        compiler_params=_cparams(("parallel", "arbitrary", "arbitrary")),
        name="moe" if routed else "ffn",
    )(x, g, w_router, w_gate, w_up, w_down)


def _decmix_kernel(cb_ref, cc_ref, cx_ref, xs_ref, bc_ref, dt_ref, sc0_ref, sc1_ref,
                   sx0_ref, sx1_ref, sx2_ref, sb0_ref, sb1_ref, sb2_ref,
                   cw_ref, cwx_ref, cwb_ref, cbx_ref, cbb_ref, dtb_ref, alog_ref,
                   pa_ref, u_ref, xso_ref, bco_ref, dto_ref, ea_ref):
    u = cc_ref[...] * cx_ref[...]
    u_ref[...] = u
    pa_ref[...] = cb_ref[...] * (cw_ref[0:1, :] * sc0_ref[...] + cw_ref[1:2, :] * sc1_ref[...] + cw_ref[2:3, :] * u)
    xso_ref[...] = _silu(cwx_ref[0:1, :] * sx0_ref[...] + cwx_ref[1:2, :] * sx1_ref[...]
                         + cwx_ref[2:3, :] * sx2_ref[...] + cwx_ref[3:4, :] * xs_ref[...] + cbx_ref[...])
    bco_ref[...] = _silu(cwb_ref[0:1, :] * sb0_ref[...] + cwb_ref[1:2, :] * sb1_ref[...]
                         + cwb_ref[2:3, :] * sb2_ref[...] + cwb_ref[3:4, :] * bc_ref[...] + cbb_ref[...])
    dt = _softplus(dt_ref[...] + dtb_ref[...])
    dto_ref[...] = dt
    ea_ref[...] = jnp.exp(dt * (-jnp.exp(alog_ref[...])))


def _decmix(proj, st_conv, st_ssm_conv, conv_w, cw, cbias, dtb, alog):
    nb = proj.shape[0]
    pcol = lambda w, off: pl.BlockSpec((nb, w), lambda i: (0, off // w))
    full = lambda w: pl.BlockSpec((nb, w), lambda i: (0, 0))
    const = lambda r, w, j: pl.BlockSpec((r, w), lambda i: (0, j))
    sx = [st_ssm_conv[:, t, :512] for t in range(3)]
    sb = [st_ssm_conv[:, t, 512:] for t in range(3)]
    return pl.pallas_call(
        _decmix_kernel,
        grid=(1,),
        in_specs=[pcol(512, CB_OFF), pcol(512, CC_OFF), pcol(512, CX_OFF), pcol(512, XS_OFF), pcol(512, BC_OFF),
                  pcol(LANES, DT_OFF)] + [full(512)] * 8
                 + [const(3, 512, 0), const(4, 512, 0), const(4, 512, 1), const(1, 512, 0), const(1, 512, 1),
                    const(1, LANES, 0), const(1, LANES, 0)],
        out_specs=[full(512), full(512), full(512), full(512), full(LANES), full(LANES)],
        out_shape=[jax.ShapeDtypeStruct((nb, 512), F32)] * 4 + [jax.ShapeDtypeStruct((nb, LANES), F32)] * 2,
        compiler_params=_cparams(("arbitrary",)),
        name="decode_mix",
    )(proj, proj, proj, proj, proj, proj, st_conv[:, 0], st_conv[:, 1], *sx, *sb,
      conv_w, cw, cw, cbias, cbias, dtb, alog)


def _decssd_kernel(h_ref, dt_ref, xs_ref, b_ref, c_ref, ea_ref, dsk_ref, y_ref, ho_ref):
    xs = xs_ref[...]
    h_new = h_ref[...] * ea_ref[...] + (xs * dt_ref[...]) * b_ref[...]
    ho_ref[...] = h_new
    y_ref[...] = jnp.sum(h_new * c_ref[...], axis=-1, keepdims=True) + dsk_ref[...] * xs


def _decssd(h0, dt, xs, bh, ch, ea, dsk):
    nb = h0.shape[0]
    blk = lambda a, b: pl.BlockSpec((None, SSM_HEADS, a, b), lambda i: (i, 0, 0, 0))
    return pl.pallas_call(
        _decssd_kernel,
        grid=(nb,),
        in_specs=[blk(SSM_HEAD_DIM, SSM_STATE), blk(1, 1), blk(SSM_HEAD_DIM, 1), blk(1, SSM_STATE),
                  blk(1, SSM_STATE), blk(1, 1), pl.BlockSpec((SSM_HEADS, 1, 1), lambda i: (0, 0, 0))],
        out_specs=[blk(SSM_HEAD_DIM, 1), blk(SSM_HEAD_DIM, SSM_STATE)],
        out_shape=[jax.ShapeDtypeStruct((nb, SSM_HEADS, SSM_HEAD_DIM, 1), F32),
                   jax.ShapeDtypeStruct((nb, SSM_HEADS, SSM_HEAD_DIM, SSM_STATE), F32)],
        compiler_params=_cparams(("parallel",)),
        name="decode_ssd",
    )(h0, dt, xs, bh, ch, ea, dsk)


DECODE_PAGE_GROUP = 16


def _decscore_kernel(pt_ref, qi_ref, w_ref, *refs):
    del pt_ref
    kid_refs, o_ref = refs[:-1], refs[-1]
    kid = jnp.concatenate([r[...].astype(BF16) for r in kid_refs], axis=0)
    s = lax.dot_general(qi_ref[:, 0:IDX_DIM].astype(BF16), kid, NT_DIMS, preferred_element_type=F32)
    sc = jnp.maximum(s, 0.0) * w_ref[:, 0:1]
    o_ref[...] = jnp.sum(sc, axis=0, keepdims=True)


def _page_specs(block, layer, pg):
    zeros = (0,) * (len(block) - 2)
    return [pl.BlockSpec(block, lambda b, p, pt, t=t: (layer, pt[b, p * pg + t]) + zeros) for t in range(pg)]


def _decscore(page_table, qip, wrows, pool_kidx, layer):
    nb, n_pages = page_table.shape
    pg = math.gcd(DECODE_PAGE_GROUP, n_pages)
    grid_spec = pltpu.PrefetchScalarGridSpec(
        num_scalar_prefetch=1,
        grid=(nb, n_pages // pg),
        in_specs=[pl.BlockSpec((None, N_IDX_HEADS, LANES), lambda b, p, pt: (b, 0, 0)),
                  pl.BlockSpec((None, N_IDX_HEADS, PAGE_SIZE), lambda b, p, pt: (b, 0, 0))]
                 + _page_specs((None, None, PAGE_SIZE, IDX_DIM), layer, pg),
        out_specs=pl.BlockSpec((None, None, 1, pg * PAGE_SIZE), lambda b, p, pt: (b, p, 0, 0)),
    )
    out = pl.pallas_call(
        _decscore_kernel,
        grid_spec=grid_spec,
        out_shape=jax.ShapeDtypeStruct((nb, n_pages // pg, 1, pg * PAGE_SIZE), F32),
        compiler_params=_cparams(("parallel", "arbitrary")),
        name="decode_scores",
    )(page_table, qip, wrows, *([pool_kidx] * pg))
    return out.reshape(nb, n_pages * PAGE_SIZE)


def _decselect_kernel(sc_ref, qi_ref, ki_ref, w_ref, sel_ref, self_ref, keys_ref, *, k_sel, kc, nbits):
    nb, past = sc_ref.shape
    nck = past // kc
    lane_k = lax.broadcasted_iota(I32, (nb, kc), 1)
    ki = ki_ref[...].astype(F32)
    own = jnp.zeros((nb, 1), F32)
    for h in range(N_IDX_HEADS):
        s = jnp.sum(qi_ref[:, h * LANES:(h + 1) * LANES].astype(F32) * ki, axis=1, keepdims=True)
        own = own + jnp.maximum(s, 0.0) * w_ref[:, h:h + 1]
    own_key = _sortable(own)

    def key_body(c, carry):
        off = pl.multiple_of(c * kc, kc)
        keys_ref[:, pl.ds(off, kc)] = _sortable(sc_ref[:, pl.ds(off, kc)])
        return carry

    lax.fori_loop(0, nck, key_body, 0)

    def count_where(pred):
        def body(c, acc):
            off = pl.multiple_of(c * kc, kc)
            f = jnp.where(pred(keys_ref[:, pl.ds(off, kc)], off), 1.0, 0.0)
            part = f[:, 0:LANES]
            for t in range(1, kc // LANES):
                part = part + f[:, t * LANES:(t + 1) * LANES]
            return acc + part

        acc = lax.fori_loop(0, nck, body, jnp.zeros((nb, LANES), F32))
        return jnp.sum(acc, axis=1, keepdims=True)

    def bit_body(i, t):
        cand = t + lax.shift_left(jnp.int32(1), 31 - i)
        cnt = count_where(lambda kk, off: kk >= cand) + jnp.where(own_key >= cand, 1.0, 0.0)
        return jnp.where(cnt >= k_sel, cand, t)

    thr = lax.fori_loop(0, 32, bit_body, jnp.full((nb, 1), INT_MIN, I32))
    thr = jnp.maximum(thr, KEY_NEG_INF)
    finite_thr = thr > KEY_NEG_INF
    cnt_gt = count_where(lambda kk, off: kk > thr) + jnp.where(own_key > thr, 1.0, 0.0)
    need = k_sel - cnt_gt

    def xbody(i, x):
        cand = x + lax.shift_left(jnp.int32(1), nbits - 1 - i)
        cnt = count_where(lambda kk, off: jnp.logical_and(kk == thr, off + lane_k < cand))
        return jnp.where(cnt < need, cand, x)

    xcut = lax.fori_loop(0, nbits, xbody, jnp.zeros((nb, 1), I32))
    xcut = jnp.where(finite_thr, xcut, -1)
    ties_past = count_where(lambda kk, off: kk == thr)
    own_tie = jnp.logical_and(jnp.logical_and(own_key == thr, finite_thr), ties_past < need)
    self_ref[...] = jnp.broadcast_to(
        jnp.where(jnp.logical_or(own_key > thr, own_tie), 1.0, 0.0), (nb, LANES))

    def sel_body(c, carry):
        off = pl.multiple_of(c * kc, kc)
        kk = keys_ref[:, pl.ds(off, kc)]
        sel = jnp.logical_or(kk > thr, jnp.logical_and(kk == thr, off + lane_k <= xcut))
        sel_ref[:, pl.ds(off, kc)] = jnp.where(sel, 1.0, 0.0)
        return carry

    lax.fori_loop(0, nck, sel_body, 0)


def _decselect(scores, qip, kib, wi):
    nb, past = scores.shape
    k_sel = min(TOPK_MAX, (past + 1) // 4)
    kc = 512 if past % 512 == 0 else PAGE_SIZE
    nbits = max(1, past.bit_length())
    full = lambda w: pl.BlockSpec((nb, w), lambda i: (0, 0))
    return pl.pallas_call(
        functools.partial(_decselect_kernel, k_sel=k_sel, kc=kc, nbits=nbits),
        grid=(1,),
        in_specs=[full(past), full(N_IDX_HEADS * LANES), full(LANES), full(LANES)],
        out_specs=[full(past), full(LANES)],
        out_shape=[jax.ShapeDtypeStruct((nb, past), F32), jax.ShapeDtypeStruct((nb, LANES), F32)],
        scratch_shapes=[pltpu.VMEM((nb, past), I32)],
        compiler_params=_cparams(("arbitrary",)),
        name="decode_select",
    )(scores, qip, kib, wi)


def _decattn_kernel(pt_ref, qp_ref, sel_ref, kn_ref, vn_ref, self_ref, *refs, n_steps):
    del pt_ref
    pg = (len(refs) - 4) // 2
    kp_refs, vp_refs = refs[:pg], refs[pg:2 * pg]
    o_ref, m_ref, l_ref, acc_ref = refs[2 * pg:]
    p = pl.program_id(1)

    @pl.when(p == 0)
    def _():
        m_ref[...] = jnp.full(m_ref.shape, NEG_BIG, F32)
        l_ref[...] = jnp.zeros(l_ref.shape, F32)
        acc_ref[...] = jnp.zeros(acc_ref.shape, F32)

    q = qp_ref[...]
    kcat = jnp.concatenate([r[...].astype(BF16) for r in kp_refs], axis=0)
    vcat = jnp.concatenate([r[...].astype(BF16) for r in vp_refs], axis=0)
    s = lax.dot_general(q.astype(BF16), kcat, NT_DIMS, preferred_element_type=F32)
    s = jnp.where(sel_ref[...] > 0.0, s, -jnp.inf)
    m_old = m_ref[:, 0:1]
    m_new = jnp.maximum(m_old, jnp.max(s, axis=1, keepdims=True))
    alpha = jnp.exp(m_old - m_new)
    pr = jnp.exp(s - m_new)
    l_new = alpha * l_ref[:, 0:1] + jnp.sum(pr, axis=1, keepdims=True)
    acc_ref[...] = alpha * acc_ref[...] + jnp.dot(pr.astype(BF16), vcat, preferred_element_type=F32)
    m_ref[...] = jnp.broadcast_to(m_new, m_ref.shape)
    l_ref[...] = jnp.broadcast_to(l_new, l_ref.shape)

    @pl.when(p == n_steps - 1)
    def _():
        s_own = jnp.sum(q * kn_ref[...], axis=1, keepdims=True)
        s_own = jnp.where(self_ref[:, 0:1] > 0.0, s_own, -jnp.inf)
        m_o = m_ref[:, 0:1]
        m_n = jnp.maximum(m_o, s_own)
        al = jnp.exp(m_o - m_n)
        p_own = jnp.exp(s_own - m_n)
        l_n = al * l_ref[:, 0:1] + p_own
        acc = al * acc_ref[...] + p_own.astype(BF16).astype(F32) * vn_ref[...]
        o_ref[...] = acc / l_n


def _decattn(page_table, qp, sel, pool_k, pool_v, kb, vb, self_sel, layer):
    nb, n_pages = page_table.shape
    pg = math.gcd(DECODE_PAGE_GROUP, n_pages)
    n_steps = n_pages // pg
    row = lambda r: pl.BlockSpec((None, r, LANES), lambda b, p, pt: (b, 0, 0))
    pages = _page_specs((None, None, PAGE_SIZE, LANES), layer, pg)
    grid_spec = pltpu.PrefetchScalarGridSpec(
        num_scalar_prefetch=1,
        grid=(nb, n_steps),
        in_specs=[row(N_HEADS),
                  pl.BlockSpec((None, None, 1, pg * PAGE_SIZE), lambda b, p, pt: (b, p, 0, 0)),
                  row(1), row(1), row(1)] + pages + pages,
        out_specs=row(N_HEADS),
        scratch_shapes=[pltpu.VMEM((N_HEADS, LANES), F32)] * 3,
    )
    return pl.pallas_call(
        functools.partial(_decattn_kernel, n_steps=n_steps),
        grid_spec=grid_spec,
        out_shape=jax.ShapeDtypeStruct((nb, N_HEADS, LANES), F32),
        compiler_params=_cparams(("parallel", "arbitrary")),
        name="decode_attend",
    )(page_table, qp, sel.reshape(nb, n_steps, 1, pg * PAGE_SIZE), kb, vb, self_sel,
      *([pool_k] * pg), *([pool_v] * pg))


def _pack_w_in(w):
    d = w.shape[0]
    pad = lambda n: jnp.zeros((d, n), w.dtype)
    cols = [w[:, 4172:7244],
            w[:, 0:2048],
            w[:, 2628:3140],
            w[:, 3140:4164],
            w[:, 2048:2304],
            w[:, 2304:2560],
            w[:, 2560:2628], pad(60),
            w[:, 4164:4172], pad(120),
            pad(N_PROJ - 7424)]
    return jnp.concatenate(cols, axis=1).astype(BF16)


def _pad_lanes(v):
    return jnp.zeros((1, LANES), F32).at[0, :v.shape[0]].set(v)


def _rope_tables(pos):
    half = HEAD_DIM // 2
    inv = ROPE_THETA ** (-jnp.arange(half, dtype=F32) / half)
    ang = pos.astype(F32)[:, None] * inv[None, :]
    cos, sin = jnp.cos(ang), jnp.sin(ang)
    cos2 = jnp.concatenate([cos, cos], axis=1)
    sin2 = jnp.concatenate([-sin, sin], axis=1)
    return jnp.tile(cos2, (1, 2)), jnp.tile(sin2, (1, 2))


def _head_halves(o):
    hpg = N_HEADS // N_KV_HEADS
    parts = [o[:, h, (h // hpg) * HEAD_DIM:(h // hpg + 1) * HEAD_DIM] for h in range(N_HEADS)]
    return jnp.concatenate(parts, axis=-1)


def kernel(x_prompt, x_sample, cache_k, cache_v, cache_kidx, state_conv, state_ssm_conv, state_ssm, page_table,
           norm1, w_in, b_gate, conv_w, q_norm, k_norm, w_pc, w_pa, ssm_conv_w, ssm_conv_b, dt_bias, a_log,
           d_skip, ssm_norm, w_ps, w_o, norm2, w_gate_dense, w_up_dense, w_down_dense, w_router, w_gate_moe,
           w_up_moe, w_down_moe):
    nb, seq, _ = x_prompt.shape
    db = x_sample.shape[0]
    depth = w_in.shape[0]
    n_pages = page_table.shape[1]
    past = n_pages * PAGE_SIZE

    hp = x_prompt.reshape(nb * seq, D_MODEL)
    hs = x_sample.reshape(db, D_MODEL)
    cos_p, sin_p = _rope_tables(jnp.tile(jnp.arange(seq), nb))
    cos_s, sin_s = _rope_tables(jnp.full((db,), past))

    pool_k = cache_k.reshape(*cache_k.shape[:3], LANES)
    pool_v = cache_v.reshape(*cache_v.shape[:3], LANES)

    outs_p = [[] for _ in range(6)]
    outs_s = [[] for _ in range(6)]
    for l in range(depth):
        wp = _pack_w_in(w_in[l])
        g1 = norm1[l].reshape(1, D_MODEL)
        g2 = norm2[l].reshape(1, D_MODEL)
        qg = jnp.tile(q_norm[l], 2).reshape(1, LANES)
        kg = jnp.tile(k_norm[l], 2).reshape(1, LANES)
        cw = ssm_conv_w[l]
        cbias = ssm_conv_b[l].reshape(1, SSM_CONV_DIM)
        dtb = _pad_lanes(dt_bias[l])
        alog = _pad_lanes(a_log[l])
        dsk = jnp.repeat(d_skip[l], SSM_HEAD_DIM).reshape(1, SSM_INNER)
        sn = ssm_norm[l].reshape(1, SSM_INNER)
        wpc, wpa, wps, wo = (w.astype(BF16) for w in (w_pc[l], w_pa[l], w_ps[l], w_o[l]))
        i = l // 2
        if l % 2 == 0:
            routed = False
            wr = jnp.zeros((D_MODEL, LANES), F32)
            wg, wu, wd = (w[i:i + 1].astype(BF16) for w in (w_gate_dense, w_up_dense, w_down_dense))
        else:
            routed = True
            wr = jnp.zeros((D_MODEL, LANES), F32).at[:, :N_EXPERTS].set(w_router[i])
            wg, wu, wd = (w[i].astype(BF16) for w in (w_gate_moe, w_up_moe, w_down_moe))

        proj = _inproj(hp, g1, wp)
        qt, kf, kb, vt, qit, kif, kib, wt = _prep(proj, cos_p, sin_p, qg, kg, True)
        attn = _dsa_prompt(qt, qit, wt, kb, vt, kib, nb, seq)
        pre_a, conv_st = _conva_prompt(proj, conv_w[l], nb, seq)
        y_ssd, h_last = _ssd_prompt(proj, cw, cbias, dtb, alog, dsk, nb, seq)
        hp = _merge(hp, pre_a, attn, y_ssd, proj, sn, b_gate[l], wpc, wpa, wps, wo)
        hp = _ffn(hp, g2, wr, wg, wu, wd, routed)
        proj3 = proj.reshape(nb, seq, N_PROJ)
        outs_p[0].append(kf.reshape(nb, seq, N_KV_HEADS, HEAD_DIM))
        outs_p[1].append(proj3[:, :, V_OFF:V_OFF + LANES].reshape(nb, seq, N_KV_HEADS, HEAD_DIM))
        outs_p[2].append(kif[:, :IDX_DIM].reshape(nb, seq, IDX_DIM))
        outs_p[3].append(conv_st)
        outs_p[4].append(proj3[:, seq - 3:, XS_OFF:XS_OFF + SSM_CONV_DIM])
        outs_p[5].append(h_last.reshape(nb, SSM_HEADS, SSM_HEAD_DIM, SSM_STATE))

        proj = _inproj(hs, g1, wp)
        qp, kf, kb, vb, qip, kif, kib, wi = _prep(proj, cos_s, sin_s, qg, kg, False)
        wrows = jnp.broadcast_to(wi[:, :N_IDX_HEADS, None], (db, N_IDX_HEADS, PAGE_SIZE))
        scores = _decscore(page_table, qip.astype(F32).reshape(db, N_IDX_HEADS, LANES), wrows, cache_kidx, l)
        sel, self_sel = _decselect(scores, qip, kib, wi)
        o = _decattn(page_table, qp.astype(F32).reshape(db, N_HEADS, LANES), sel, pool_k, pool_v,
                     kb.astype(F32).reshape(db, 1, LANES), vb.astype(F32).reshape(db, 1, LANES),
                     self_sel.reshape(db, 1, LANES), l)
        attn = _head_halves(o)
        pre_a, u, xs, bcv, dt, ea = _decmix(proj, state_conv[l], state_ssm_conv[l], conv_w[l], cw, cbias, dtb, alog)
        hpg = SSM_HEADS // SSM_GROUPS
        xs4 = xs.reshape(db, SSM_HEADS, SSM_HEAD_DIM, 1)
        bh = jnp.repeat(bcv[:, :SSM_GROUPS * SSM_STATE].reshape(db, SSM_GROUPS, 1, SSM_STATE), hpg, axis=1)
        ch = jnp.repeat(bcv[:, SSM_GROUPS * SSM_STATE:].reshape(db, SSM_GROUPS, 1, SSM_STATE), hpg, axis=1)
        y4, h_new = _decssd(state_ssm[l], dt[:, :SSM_HEADS, None, None], xs4, bh, ch, ea[:, :SSM_HEADS, None, None],
                            d_skip[l].reshape(SSM_HEADS, 1, 1))
        hs = _merge(hs, pre_a, attn, y4.reshape(db, SSM_INNER), proj, sn, b_gate[l], wpc, wpa, wps, wo)
        hs = _ffn(hs, g2, wr, wg, wu, wd, routed)
        outs_s[0].append(kf.reshape(db, 1, N_KV_HEADS, HEAD_DIM))
        outs_s[1].append(proj[:, V_OFF:V_OFF + LANES].reshape(db, 1, N_KV_HEADS, HEAD_DIM))
        outs_s[2].append(kif[:, :IDX_DIM].reshape(db, 1, IDX_DIM))
        outs_s[3].append(jnp.stack([state_conv[l][:, 1], u], axis=1))
        outs_s[4].append(jnp.concatenate(
            [state_ssm_conv[l][:, 1:], proj[:, None, XS_OFF:XS_OFF + SSM_CONV_DIM]], axis=1))
        outs_s[5].append(h_new)

    return (hp.reshape(nb, seq, D_MODEL), hs.reshape(db, 1, D_MODEL),
            *(jnp.stack(o) for o in outs_p), *(jnp.stack(o) for o in outs_s))
```

```python
import functools
import math

import jax
import jax.numpy as jnp
from jax import lax
from jax.experimental import pallas as pl
from jax.experimental.pallas import tpu as pltpu

F32 = jnp.float32
BF16 = jnp.bfloat16
I32 = jnp.int32

D_MODEL = 1024
D_CONV = 512
N_HEADS = 8
HEAD_DIM = 64
N_KV_HEADS = 2
N_IDX_HEADS = 4
IDX_DIM = 64
TOPK_MAX = 256
Q_BLOCK = 128
ROPE_THETA = 10000.0
SSM_INNER = 512
SSM_HEADS = 8
SSM_HEAD_DIM = 64
SSM_GROUPS = 2
SSM_STATE = 128
SSM_CHUNK = 128
SSM_CONV_DIM = 1024
PAGE_SIZE = 128
N_EXPERTS = 8
EPS = 1e-6

LANES = 128
SUBLANES = 8
VMEM_LIMIT = 52 * 1024 * 1024

G_OFF, CB_OFF, CC_OFF, CX_OFF, Q_OFF, Z_OFF, XS_OFF, BC_OFF = 0, 3072, 3584, 4096, 4608, 5120, 5632, 6144
K_OFF, V_OFF, QI_OFF, KW_OFF, DT_OFF = 6656, 6784, 6912, 7168, 7296
N_PROJ = 7680

INT_MIN = -2 ** 31
INT_MAX = 2 ** 31 - 1
KEY_NEG_INF = (-8388608) ^ 0x7FFFFFFF
NEG_BIG = -1e30
LOG2_E = 1.4426950408889634

NT_DIMS = (((1,), (1,)), ((), ()))


def _cparams(sem):
    return pltpu.CompilerParams(dimension_semantics=sem, vmem_limit_bytes=VMEM_LIMIT)


def _sigmoid(x):
    return 1.0 / (1.0 + jnp.exp(-x))


def _silu(x):
    return x * _sigmoid(x)


def _softplus(x):
    return jnp.maximum(x, 0.0) + jnp.log1p(jnp.exp(-jnp.abs(x)))


def _sortable(x):
    bits = pltpu.bitcast(x, I32)
    return bits ^ (lax.shift_right_arithmetic(bits, 31) & 0x7FFFFFFF)


def _rms(x, g):
    return x * lax.rsqrt(jnp.mean(x * x, axis=-1, keepdims=True) + EPS) * g


def _inproj_kernel(x_ref, g_ref, w_ref, o_ref, h_ref):
    @pl.when(pl.program_id(1) == 0)
    def _():
        h_ref[...] = _rms(x_ref[...], g_ref[...]).astype(BF16)

    o_ref[...] = jnp.dot(h_ref[...], w_ref[...], preferred_element_type=F32)


def _inproj(x, g, w):
    m = x.shape[0]
    tm = min(m, 1024)
    tn = 1536
    return pl.pallas_call(
        _inproj_kernel,
        grid=(m // tm, N_PROJ // tn),
        in_specs=[pl.BlockSpec((tm, D_MODEL), lambda i, j: (i, 0)),
                  pl.BlockSpec((1, D_MODEL), lambda i, j: (0, 0)),
                  pl.BlockSpec((D_MODEL, tn), lambda i, j: (0, j))],
        out_specs=pl.BlockSpec((tm, tn), lambda i, j: (i, j)),
        out_shape=jax.ShapeDtypeStruct((m, N_PROJ), F32),
        scratch_shapes=[pltpu.VMEM((tm, D_MODEL), BF16)],
        compiler_params=_cparams(("parallel", "arbitrary")),
        name="inproj",
    )(x, g, w)


V_T_ROWS = LANES + 16


def _prep_kernel(q_ref, k_ref, v_ref, qi_ref, kw_ref, cos_ref, sin_ref, qg_ref, kg_ref,
                 qp_ref, kf_ref, kb_ref, vb_ref, qip_ref, kif_ref, kib_ref, wo_ref, *, transposed):
    tm = cos_ref.shape[0]

    def put(ref, h, val):
        if transposed:
            ref[h * LANES:(h + 1) * LANES, :] = val.T.astype(BF16)
        else:
            ref[:, h * LANES:(h + 1) * LANES] = val.astype(BF16)

    cos = cos_ref[...]
    sin = sin_ref[...]
    lane = lax.broadcasted_iota(I32, (tm, LANES), 1)
    low_half = lane < HEAD_DIM
    first_rot = (lane % HEAD_DIM) < (HEAD_DIM // 2)
    r = lax.broadcasted_iota(I32, (LANES, LANES), 0) // HEAD_DIM
    c = lax.broadcasted_iota(I32, (LANES, LANES), 1) // HEAD_DIM
    seg = jnp.where(r == c, 1.0 / HEAD_DIM, 0.0).astype(BF16)

    def rope(x):
        fwd = pltpu.roll(x, LANES - HEAD_DIM // 2, 1)
        bwd = pltpu.roll(x, HEAD_DIM // 2, 1)
        return x * cos + jnp.where(first_rot, fwd, bwd) * sin

    def headnorm(x, g):
        s = x * x
        hi = s.astype(BF16)
        lo = (s - hi.astype(F32)).astype(BF16)
        ms = (jnp.dot(hi, seg, preferred_element_type=F32)
              + jnp.dot(lo, seg, preferred_element_type=F32))
        return x * lax.rsqrt(ms + EPS) * g

    q_scale = HEAD_DIM ** -0.5 * (LOG2_E if transposed else 1.0)
    qg = qg_ref[...]
    for s in range(N_HEADS // 2):
        slab = rope(headnorm(q_ref[:, s * LANES:(s + 1) * LANES], qg)) * q_scale
        swapped = pltpu.roll(slab, HEAD_DIM, 1)
        grp = (2 * s) // (N_HEADS // N_KV_HEADS)
        for hh in range(2):
            h = 2 * s + hh
            src = slab if hh == grp else swapped
            keep = low_half if grp == 0 else jnp.logical_not(low_half)
            put(qp_ref, h, jnp.where(keep, src, 0.0))

    k = rope(headnorm(k_ref[...], kg_ref[...]))
    kf_ref[...] = k
    kb_ref[...] = k.astype(BF16)
    if transposed:
        vb_ref[0:LANES, :] = v_ref[...].T.astype(BF16)
        vb_ref[LANES:V_T_ROWS, :] = jnp.ones((V_T_ROWS - LANES, tm), BF16)
    else:
        vb_ref[...] = v_ref[...].astype(BF16)

    for s in range(N_IDX_HEADS // 2):
        slab = rope(qi_ref[:, s * LANES:(s + 1) * LANES]) * (IDX_DIM ** -0.5)
        swapped = pltpu.roll(slab, HEAD_DIM, 1)
        for hh in range(2):
            h = 2 * s + hh
            src = slab if hh == 0 else swapped
            put(qip_ref, h, jnp.where(low_half, src, 0.0))

    kw = kw_ref[...]
    ki = rope(kw)
    kif_ref[...] = ki
    kib_ref[...] = jnp.where(low_half, ki, 0.0).astype(BF16)
    wi = pltpu.roll(kw, HEAD_DIM, 1) * (N_IDX_HEADS ** -0.5)
    if transposed:
        wo_ref[...] = wi.T[0:SUBLANES, :]
    else:
        wo_ref[...] = wi


def _prep(proj, cos, sin, qg, kg, transposed):
    m = proj.shape[0]
    tm = min(m, 512)
    row = lambda w, off: pl.BlockSpec((tm, w), lambda i: (i, off // w))
    full = lambda w: pl.BlockSpec((tm, w), lambda i: (i, 0))
    const = pl.BlockSpec((1, LANES), lambda i: (0, 0))
    if transposed:
        feat = lambda r: pl.BlockSpec((r, tm), lambda i: (0, i))
        fshape = lambda r, dt: jax.ShapeDtypeStruct((r, m), dt)
    else:
        feat = full
        fshape = lambda r, dt: jax.ShapeDtypeStruct((m, r), dt)
    return pl.pallas_call(
        functools.partial(_prep_kernel, transposed=transposed),
        grid=(m // tm,),
        in_specs=[row(512, Q_OFF), row(LANES, K_OFF), row(LANES, V_OFF), row(256, QI_OFF), row(LANES, KW_OFF),
                  full(LANES), full(LANES), const, const],
        out_specs=[feat(N_HEADS * LANES), full(LANES), full(LANES), feat(V_T_ROWS if transposed else LANES),
                   feat(N_IDX_HEADS * LANES), full(LANES), full(LANES), feat(SUBLANES if transposed else LANES)],
        out_shape=[fshape(N_HEADS * LANES, BF16),
                   jax.ShapeDtypeStruct((m, LANES), F32),
                   jax.ShapeDtypeStruct((m, LANES), BF16),
                   fshape(V_T_ROWS if transposed else LANES, BF16),
                   fshape(N_IDX_HEADS * LANES, BF16),
                   jax.ShapeDtypeStruct((m, LANES), F32),
                   jax.ShapeDtypeStruct((m, LANES), BF16),
                   fshape(SUBLANES if transposed else LANES, F32)],
        compiler_params=_cparams(("parallel",)),
        name="prep",
    )(proj, proj, proj, proj, proj, cos, sin, qg, kg)


def _fold_rows(f):
    while f.shape[0] > SUBLANES:
        half = f.shape[0] // 2
        f = f[:half] + f[half:]
    return f


def _dsa_kernel(qt_ref, qit_ref, wt_ref, k_ref, vt_ref, ki_ref, o_ref,
                keys_ref, x_ref, m_ref, kmax_ref, acc_ref, *, k_sel, kc, nbits):
    qb = Q_BLOCK
    j = pl.program_id(1)
    nck = (j * qb + qb + kc - 1) // kc
    qpos = j * qb + lax.broadcasted_iota(I32, (1, qb), 1)
    sub_k = lax.broadcasted_iota(I32, (kc, qb), 0)
    pair = lambda ref, s: jnp.concatenate(
        [ref[(2 * s) * LANES:(2 * s + 1) * LANES, :], ref[(2 * s + 1) * LANES:(2 * s + 2) * LANES, :]], axis=1)
    qi2 = [pair(qit_ref, s) for s in range(N_IDX_HEADS // 2)]
    q2 = [pair(qt_ref, s) for s in range(N_HEADS // 2)]

    def score_body(c, carry):
        off = pl.multiple_of(c * kc, kc)
        kic = ki_ref[pl.ds(off, kc), :]
        acc = jnp.zeros((kc, qb), F32)
        for s in range(N_IDX_HEADS // 2):
            s2 = jnp.dot(kic, qi2[s], preferred_element_type=F32)
            for hh in range(2):
                h = 2 * s + hh
                acc = acc + jnp.maximum(s2[:, hh * qb:(hh + 1) * qb], 0.0) * wt_ref[h:h + 1, :]
        acc = jnp.where(off + sub_k <= qpos, acc, -jnp.inf)
        keys_ref[pl.ds(off, kc), :] = _sortable(acc)
        return carry

    lax.fori_loop(0, nck, score_body, 0)

    def count_where(pred):
        def body(c, acc):
            off = pl.multiple_of(c * kc, kc)
            return acc + _fold_rows(jnp.where(pred(keys_ref[pl.ds(off, kc), :], off), 1.0, 0.0))

        acc = lax.fori_loop(0, nck, body, jnp.zeros((SUBLANES, qb), F32))
        return jnp.sum(acc, axis=0, keepdims=True)

    def bit_body(i, t):
        cand = t + lax.shift_left(jnp.int32(1), 31 - i)
        cnt = count_where(lambda kk, off: kk >= cand)
        return jnp.where(cnt >= k_sel, cand, t)

    thr = lax.fori_loop(0, 32, bit_body, jnp.full((1, qb), INT_MIN, I32))
    thr = jnp.maximum(thr, KEY_NEG_INF)
    finite_thr = thr > KEY_NEG_INF
    cnt_ge = count_where(lambda kk, off: kk >= thr)
    cnt_gt = count_where(lambda kk, off: kk > thr)
    need = k_sel - cnt_gt
    tie = jnp.logical_and(cnt_ge > k_sel, finite_thr)
    x_default = jnp.where(finite_thr, INT_MAX, -1)
    x_ref[...] = jnp.broadcast_to(x_default, x_ref.shape)

    @pl.when(jnp.max(jnp.where(tie, 1.0, 0.0)) > 0.0)
    def _():
        def xbody(i, x):
            cand = x + lax.shift_left(jnp.int32(1), nbits - 1 - i)
            cnt = count_where(lambda kk, off: jnp.logical_and(kk == thr, off + sub_k < cand))
            return jnp.where(cnt < need, cand, x)

        x = lax.fori_loop(0, nbits, xbody, jnp.zeros((1, qb), I32))
        x_ref[...] = jnp.broadcast_to(jnp.where(tie, x, x_default), x_ref.shape)

    xcut = x_ref[0:1, :]
    hpg = N_HEADS // N_KV_HEADS

    @pl.when(j == 0)
    def _():
        r = lax.broadcasted_iota(I32, (LANES, LANES), 0) // HEAD_DIM
        cc = lax.broadcasted_iota(I32, (LANES, LANES), 1) // HEAD_DIM
        seg = jnp.where(r == cc, 1.0, 0.0).astype(BF16)

        def body(c, mx):
            kf = k_ref[pl.ds(pl.multiple_of(c * kc, kc), kc), :].astype(F32)
            gs = jnp.dot((kf * kf).astype(BF16), seg, preferred_element_type=F32)
            while gs.shape[0] > SUBLANES:
                half = gs.shape[0] // 2
                gs = jnp.maximum(gs[:half], gs[half:])
            return jnp.maximum(mx, gs)

        mx = lax.fori_loop(0, k_ref.shape[0] // kc, body, jnp.zeros((SUBLANES, LANES), F32))
        mx = jnp.max(mx, axis=0, keepdims=True)
        lane = lax.broadcasted_iota(I32, (1, LANES), 1)
        other = pltpu.roll(mx, HEAD_DIM, 1)
        kmax_ref[0:1, :] = jnp.where(lane < HEAD_DIM, mx, other)
        kmax_ref[1:2, :] = jnp.where(lane < HEAD_DIM, other, mx)

    def chunk(c):
        off = pl.multiple_of(c * kc, kc)
        kk = keys_ref[pl.ds(off, kc), :]
        sel = jnp.logical_or(kk > thr, jnp.logical_and(kk == thr, off + sub_k <= xcut))
        return sel, k_ref[pl.ds(off, kc), :], vt_ref[:, pl.ds(off, kc)]

    bound = []
    for h in range(N_HEADS):
        qh = qt_ref[h * LANES:(h + 1) * LANES, :].astype(F32)
        qsq = jnp.sum(qh * qh, axis=0, keepdims=True)
        bound.append(jnp.sqrt(qsq * kmax_ref[h // hpg:h // hpg + 1, :]))
    acc_ref[...] = jnp.zeros(acc_ref.shape, F32)

    def fast_body(c, carry):
        sel, kch, vch = chunk(c)
        n_pair = N_HEADS // 2
        s2 = jnp.dot(kch, q2[0], preferred_element_type=F32)
        for s in range(n_pair):
            s2_next = jnp.dot(kch, q2[s + 1], preferred_element_type=F32) if s + 1 < n_pair else None
            ps = [jnp.exp2(jnp.where(sel, s2[:, hh * qb:(hh + 1) * qb] - bound[2 * s + hh], -jnp.inf)).astype(BF16)
                  for hh in range(2)]
            acc_ref[s] += jnp.dot(vch, jnp.concatenate(ps, axis=1), preferred_element_type=F32)
            s2 = s2_next
        return carry

    lax.fori_loop(0, nck, fast_body, 0)

    dens = jnp.concatenate([acc_ref[s, LANES:LANES + 1, :] for s in range(N_HEADS // 2)], axis=0)

    @pl.when(jnp.logical_not(jnp.min(dens) >= 1e-20))
    def _():
        m_ref[...] = jnp.full(m_ref.shape, NEG_BIG, F32)
        acc_ref[...] = jnp.zeros(acc_ref.shape, F32)

        def exact_body(c, carry):
            sel, kch, vch = chunk(c)
            for s in range(N_HEADS // 2):
                s2 = jnp.dot(kch, q2[s], preferred_element_type=F32)
                ps, alphas = [], []
                for hh in range(2):
                    h = 2 * s + hh
                    sh = jnp.where(sel, s2[:, hh * qb:(hh + 1) * qb], -jnp.inf)
                    m_old = m_ref[h:h + 1, :]
                    m_new = jnp.maximum(m_old, jnp.max(sh, axis=0, keepdims=True))
                    m_ref[h:h + 1, :] = m_new
                    alphas.append(jnp.exp2(m_old - m_new))
                    ps.append(jnp.exp2(sh - m_new).astype(BF16))
                o2 = jnp.dot(vch, jnp.concatenate(ps, axis=1), preferred_element_type=F32)
                acc_ref[s] = jnp.concatenate(alphas, axis=1) * acc_ref[s] + o2
            return carry

        lax.fori_loop(0, nck, exact_body, 0)

    for s in range(N_HEADS // 2):
        g = (2 * s) // hpg
        num = acc_ref[s, g * HEAD_DIM:(g + 1) * HEAD_DIM, :]
        out2 = num / acc_ref[s, LANES:LANES + 1, :]
        o_ref[:, s * LANES:(s + 1) * LANES] = jnp.concatenate([out2[:, :qb], out2[:, qb:]], axis=0).T


def _dsa_prompt(qt, qit, wt, kb, vt, kib, nb, seq):
    k_sel = min(TOPK_MAX, seq // 4)
    kc = next(c for c in (512, Q_BLOCK) if seq % c == 0)
    nbits = max(1, (seq - 1).bit_length())
    nq = seq // Q_BLOCK
    r3 = lambda a: a.reshape(nb, seq, a.shape[-1])
    qcol = lambda r: pl.BlockSpec((r, Q_BLOCK), lambda b, j: (0, b * nq + j))
    seqblk = pl.BlockSpec((None, seq, LANES), lambda b, j: (b, 0, 0))
    out = pl.pallas_call(
        functools.partial(_dsa_kernel, k_sel=k_sel, kc=kc, nbits=nbits),
        grid=(nb, nq),
        in_specs=[qcol(N_HEADS * LANES), qcol(N_IDX_HEADS * LANES), qcol(SUBLANES), seqblk,
                  pl.BlockSpec((V_T_ROWS, seq), lambda b, j: (0, b)), seqblk],
        out_specs=pl.BlockSpec((None, Q_BLOCK, N_HEADS * HEAD_DIM), lambda b, j: (b, j, 0)),
        out_shape=jax.ShapeDtypeStruct((nb, seq, N_HEADS * HEAD_DIM), F32),
        scratch_shapes=[pltpu.VMEM((seq, Q_BLOCK), I32),
                        pltpu.VMEM((SUBLANES, Q_BLOCK), I32),
                        pltpu.VMEM((N_HEADS, Q_BLOCK), F32),
                        pltpu.VMEM((SUBLANES, LANES), F32),
                        pltpu.VMEM((N_HEADS // 2, V_T_ROWS, 2 * Q_BLOCK), F32)],
        compiler_params=_cparams(("arbitrary", "arbitrary")),
        name="dsa_prompt",
    )(qt, qit, wt, r3(kb), vt, r3(kib))
    return out.reshape(nb * seq, N_HEADS * HEAD_DIM)


def _conva_kernel(cb_ref, cc_ref, cx_ref, cch_ref, cxh_ref, w_ref, o_ref, st_ref, *, tiles_per_seq):
    tm = cb_ref.shape[0]
    first = (pl.program_id(0) % tiles_per_seq) == 0
    u = cc_ref[...] * cx_ref[...]
    uh = jnp.where(first, 0.0, cch_ref[...] * cxh_ref[...])
    ext = jnp.concatenate([uh, u], axis=0)
    conv = (w_ref[2:3, :] * u + w_ref[1:2, :] * ext[SUBLANES - 1:SUBLANES - 1 + tm]
            + w_ref[0:1, :] * ext[SUBLANES - 2:SUBLANES - 2 + tm])
    o_ref[...] = cb_ref[...] * conv
    st_ref[...] = u[tm - 2:tm, :]


def _conva_prompt(proj, conv_w, nb, seq):
    m = proj.shape[0]
    tm = min(seq, 512)
    tps = seq // tm
    col = lambda off: pl.BlockSpec((tm, D_CONV), lambda i: (i, off // D_CONV))
    halo = lambda off: pl.BlockSpec(
        (SUBLANES, D_CONV), lambda i: (jnp.maximum(i * (tm // SUBLANES) - 1, 0), off // D_CONV))
    return pl.pallas_call(
        functools.partial(_conva_kernel, tiles_per_seq=tps),
        grid=(m // tm,),
        in_specs=[col(CB_OFF), col(CC_OFF), col(CX_OFF), halo(CC_OFF), halo(CX_OFF),
                  pl.BlockSpec((3, D_CONV), lambda i: (0, 0))],
        out_specs=[pl.BlockSpec((tm, D_CONV), lambda i: (i, 0)),
                   pl.BlockSpec((None, 2, D_CONV), lambda i: (i // tps, 0, 0))],
        out_shape=[jax.ShapeDtypeStruct((m, D_CONV), F32),
                   jax.ShapeDtypeStruct((nb, 2, D_CONV), F32)],
        compiler_params=_cparams(("arbitrary",)),
        name="conva_prompt",
    )(proj, proj, proj, proj, proj, conv_w)


def _ssd_kernel(xs_ref, bc_ref, dt_ref, xh_ref, bh_ref, cwx_ref, cwb_ref, cbx_ref, cbb_ref,
                dtb_ref, alog_ref, dsk_ref, y_ref, hout_ref, h_ref, *, nchunk):
    cl = SSM_CHUNK
    c = pl.program_id(1)
    first = c == 0

    @pl.when(first)
    def _():
        h_ref[...] = jnp.zeros(h_ref.shape, F32)

    def conv(cur, halo, w_ref, b_ref):
        ext = jnp.concatenate([jnp.where(first, 0.0, halo), cur], axis=0)
        out = (w_ref[3:4, :] * cur + w_ref[2:3, :] * ext[SUBLANES - 1:SUBLANES - 1 + cl]
               + w_ref[1:2, :] * ext[SUBLANES - 2:SUBLANES - 2 + cl]
               + w_ref[0:1, :] * ext[SUBLANES - 3:SUBLANES - 3 + cl] + b_ref[...])
        return _silu(out)

    xs = conv(xs_ref[...], xh_ref[...], cwx_ref, cbx_ref)
    bc = conv(bc_ref[...], bh_ref[...], cwb_ref, cbb_ref)
    dt = _softplus(dt_ref[...] + dtb_ref[...])
    a = dt * (-jnp.exp(alog_ref[...]))
    ri = lax.broadcasted_iota(I32, (cl, cl), 0)
    ci = lax.broadcasted_iota(I32, (cl, cl), 1)
    causal = ri >= ci
    cs = jnp.dot(jnp.where(causal, 1.0, 0.0), a, preferred_element_type=F32,
                 precision=lax.Precision.HIGHEST)
    cs_t = cs.T
    lane = lax.broadcasted_iota(I32, (cl, LANES), 1)
    lo = lane < SSM_HEAD_DIM
    rows_lo = lax.broadcasted_iota(I32, (LANES, 1), 0) < SSM_HEAD_DIM
    heads_per_group = SSM_HEADS // SSM_GROUPS

    cb = []
    for g in range(SSM_GROUPS):
        bg = bc[:, g * SSM_STATE:(g + 1) * SSM_STATE].astype(BF16)
        cg = bc[:, (SSM_GROUPS + g) * SSM_STATE:(SSM_GROUPS + g + 1) * SSM_STATE].astype(BF16)
        cb.append((bg, cg, lax.dot_general(cg, bg, NT_DIMS, preferred_element_type=F32)))

    for s in range(SSM_HEADS // 2):
        h0, h1 = 2 * s, 2 * s + 1
        bg, cg, cbg = cb[h0 // heads_per_group]
        sl = slice(s * LANES, (s + 1) * LANES)
        xs_s = xs[:, sl]
        col0, col1 = cs[:, h0:h0 + 1], cs[:, h1:h1 + 1]
        last0, last1 = cs[cl - 1:cl, h0:h0 + 1], cs[cl - 1:cl, h1:h1 + 1]
        xdt = xs_s * jnp.where(lo, dt[:, h0:h0 + 1], dt[:, h1:h1 + 1])
        xdt_b = xdt.astype(BF16)
        m0 = (cbg * jnp.where(causal, jnp.exp(col0 - cs_t[h0:h0 + 1, :]), 0.0)).astype(BF16)
        m1 = (cbg * jnp.where(causal, jnp.exp(col1 - cs_t[h1:h1 + 1, :]), 0.0)).astype(BF16)
        y_diag = jnp.where(lo, jnp.dot(m0, xdt_b, preferred_element_type=F32),
                           jnp.dot(m1, xdt_b, preferred_element_type=F32))
        hs = h_ref[sl, :]
        y_off = lax.dot_general(cg, hs.astype(BF16), NT_DIMS, preferred_element_type=F32)
        y_off = y_off * jnp.where(lo, jnp.exp(col0), jnp.exp(col1))
        y_ref[:, sl] = y_diag + y_off + dsk_ref[:, sl] * xs_s
        xw = xdt * jnp.where(lo, jnp.exp(last0 - col0), jnp.exp(last1 - col1))
        st = jnp.dot(xw.T.astype(BF16), bg, preferred_element_type=F32)
        h_ref[sl, :] = hs * jnp.where(rows_lo, jnp.exp(last0), jnp.exp(last1)) + st

    @pl.when(c == nchunk - 1)
    def _():
        hout_ref[...] = h_ref[...]


def _ssd_prompt(proj, cw, cbias, dtb, alog, dsk, nb, seq):
    m = proj.shape[0]
    cl = SSM_CHUNK
    nchunk = seq // cl
    blk = lambda w, off: pl.BlockSpec((cl, w), lambda b, c: (b * nchunk + c, off // w))
    halo = lambda off: pl.BlockSpec(
        (SUBLANES, 512), lambda b, c: (jnp.maximum((b * nchunk + c) * (cl // SUBLANES) - 1, 0), off // 512))
    const = lambda r, w, j: pl.BlockSpec((r, w), lambda b, c: (0, j))
    y, hout = pl.pallas_call(
        functools.partial(_ssd_kernel, nchunk=nchunk),
        grid=(nb, nchunk),
        in_specs=[blk(512, XS_OFF), blk(512, BC_OFF), blk(LANES, DT_OFF), halo(XS_OFF), halo(BC_OFF),
                  const(4, 512, 0), const(4, 512, 1), const(1, 512, 0), const(1, 512, 1),
                  const(1, LANES, 0), const(1, LANES, 0), const(1, 512, 0)],
        out_specs=[pl.BlockSpec((cl, SSM_INNER), lambda b, c: (b * nchunk + c, 0)),
                   pl.BlockSpec((None, SSM_INNER, SSM_STATE), lambda b, c: (b, 0, 0))],
        out_shape=[jax.ShapeDtypeStruct((m, SSM_INNER), F32),
                   jax.ShapeDtypeStruct((nb, SSM_INNER, SSM_STATE), F32)],
        scratch_shapes=[pltpu.VMEM((SSM_INNER, SSM_STATE), F32)],
        compiler_params=_cparams(("parallel", "arbitrary")),
        name="ssd_prompt",
    )(proj, proj, proj, proj, proj, cw, cw, cbias, cbias, dtb, alog, dsk)
    return y, hout


def _merge_kernel(x_ref, pa_ref, at_ref, ys_ref, z_ref, g0_ref, g1_ref, g2_ref, sn_ref, bg_ref,
                  wpc_ref, wpa_ref, wps_ref, wo_ref, o_ref):
    ssd = _rms(ys_ref[...] * _silu(z_ref[...]), sn_ref[...])
    ya = jnp.dot(pa_ref[...].astype(BF16), wpc_ref[...], preferred_element_type=F32)
    yb = jnp.dot(at_ref[...].astype(BF16), wpa_ref[...], preferred_element_type=F32)
    yc = jnp.dot(ssd.astype(BF16), wps_ref[...], preferred_element_type=F32)
    merged = (_sigmoid(g0_ref[...] + bg_ref[0:1, :]) * ya + _sigmoid(g1_ref[...] + bg_ref[1:2, :]) * yb
              + _sigmoid(g2_ref[...] + bg_ref[2:3, :]) * yc)
    o_ref[...] = x_ref[...] + jnp.dot(merged.astype(BF16), wo_ref[...], preferred_element_type=F32)


def _merge(x, pre_a, attn, y_ssd, proj, ssm_norm, b_gate, w_pc, w_pa, w_ps, w_o):
    m = x.shape[0]
    tm = min(m, 256)
    row = lambda w: pl.BlockSpec((tm, w), lambda i: (i, 0))
    pcol = lambda w, off: pl.BlockSpec((tm, w), lambda i: (i, off // w))
    const = lambda r, w: pl.BlockSpec((r, w), lambda i: (0, 0))
    return pl.pallas_call(
        _merge_kernel,
        grid=(m // tm,),
        in_specs=[row(D_MODEL), row(512), row(512), row(512), pcol(512, Z_OFF),
                  pcol(D_MODEL, G_OFF), pcol(D_MODEL, G_OFF + D_MODEL), pcol(D_MODEL, G_OFF + 2 * D_MODEL),
                  const(1, 512), const(3, D_MODEL),
                  const(512, D_MODEL), const(512, D_MODEL), const(512, D_MODEL), const(D_MODEL, D_MODEL)],
        out_specs=row(D_MODEL),
        out_shape=jax.ShapeDtypeStruct((m, D_MODEL), F32),
        compiler_params=_cparams(("parallel",)),
        name="merge",
    )(x, pre_a, attn, y_ssd, proj, proj, proj, proj, ssm_norm, b_gate, w_pc, w_pa, w_ps, w_o)


def _ffn_kernel(x_ref, g_ref, wr_ref, wg_ref, wu_ref, wd_ref, o_ref, h_ref, acc_ref, comb_ref,
                *, routed, n_e, n_f):
    e = pl.program_id(1)
    j = pl.program_id(2)
    tm = x_ref.shape[0]

    @pl.when(jnp.logical_and(e == 0, j == 0))
    def _():
        hf = _rms(x_ref[...], g_ref[...])
        h_ref[...] = hf.astype(BF16)
        acc_ref[...] = jnp.zeros(acc_ref.shape, F32)
        if routed:
            lane = lax.broadcasted_iota(I32, (tm, LANES), 1).astype(F32)
            logits = jnp.dot(hf, wr_ref[...], preferred_element_type=F32, precision=lax.Precision.HIGHEST)
            logits = jnp.where(lane < n_e, logits, -jnp.inf)
            m1 = jnp.max(logits, axis=1, keepdims=True)
            i1 = jnp.min(jnp.where(logits == m1, lane, float(LANES)), axis=1, keepdims=True)
            rest = jnp.where(lane == i1, -jnp.inf, logits)
            m2 = jnp.max(rest, axis=1, keepdims=True)
            i2 = jnp.min(jnp.where(rest == m2, lane, float(LANES)), axis=1, keepdims=True)
            e2 = jnp.exp(m2 - m1)
            den = 1.0 + e2
            comb_ref[...] = jnp.where(lane == i1, 1.0 / den, 0.0) + jnp.where(lane == i2, e2 / den, 0.0)

    h = h_ref[...]
    gate = jnp.dot(h, wg_ref[...], preferred_element_type=F32)
    up = jnp.dot(h, wu_ref[...], preferred_element_type=F32)
    act = _silu(gate) * up
    if routed:
        lane = lax.broadcasted_iota(I32, (tm, LANES), 1)
        act = act * jnp.sum(jnp.where(lane == e, comb_ref[...], 0.0), axis=1, keepdims=True)
    acc_ref[...] += jnp.dot(act.astype(BF16), wd_ref[...], preferred_element_type=F32)

    @pl.when(jnp.logical_and(e == n_e - 1, j == n_f - 1))
    def _():
        o_ref[...] = x_ref[...] + acc_ref[...]


def _ffn(x, g, w_router, w_gate, w_up, w_down, routed):
    m = x.shape[0]
    n_e, _, d_ff = w_gate.shape
    tm = min(m, 1024)
    tf = 256
    n_f = d_ff // tf
    return pl.pallas_call(
        functools.partial(_ffn_kernel, routed=routed, n_e=n_e, n_f=n_f),
        grid=(m // tm, n_e, n_f),
        in_specs=[pl.BlockSpec((tm, D_MODEL), lambda i, e, j: (i, 0)),
                  pl.BlockSpec((1, D_MODEL), lambda i, e, j: (0, 0)),
                  pl.BlockSpec((D_MODEL, LANES), lambda i, e, j: (0, 0)),
                  pl.BlockSpec((None, D_MODEL, tf), lambda i, e, j: (e, 0, j)),
                  pl.BlockSpec((None, D_MODEL, tf), lambda i, e, j: (e, 0, j)),
                  pl.BlockSpec((None, tf, D_MODEL), lambda i, e, j: (e, j, 0))],
        out_specs=pl.BlockSpec((tm, D_MODEL), lambda i, e, j: (i, 0)),
        out_shape=jax.ShapeDtypeStruct((m, D_MODEL), F32),
        scratch_shapes=[pltpu.VMEM((tm, D_MODEL), BF16), pltpu.VMEM((tm, D_MODEL), F32),
                        pltpu.VMEM((tm, LANES), F32)],
        compiler_params=_cparams(("parallel", "arbitrary", "arbitrary")),
        name="moe" if routed else "ffn",
    )(x, g, w_router, w_gate, w_up, w_down)


MOE_BLOCK = 2048
MOE_TILE = 256
TOP_K = 2


def _moe_tiles(tb):
    return TOP_K * tb // MOE_TILE + N_EXPERTS


def _route_kernel(x_ref, g_ref, wr_ref, hb_ref, row_ref, col_ref, meta_ref):
    tb = x_ref.shape[0]
    hf = _rms(x_ref[...], g_ref[...])
    hb_ref[...] = hf.astype(BF16)
    lane_i = lax.broadcasted_iota(I32, (tb, LANES), 1)
    lane = lane_i.astype(F32)
    logits = jnp.dot(hf, wr_ref[...], preferred_element_type=F32, precision=lax.Precision.HIGHEST)
    logits = jnp.where(lane_i < N_EXPERTS, logits, -jnp.inf)
    m1 = jnp.max(logits, axis=1, keepdims=True)
    i1 = jnp.min(jnp.where(logits == m1, lane, float(LANES)), axis=1, keepdims=True)
    rest = jnp.where(lane == i1, -jnp.inf, logits)
    m2 = jnp.max(rest, axis=1, keepdims=True)
    i2 = jnp.min(jnp.where(rest == m2, lane, float(LANES)), axis=1, keepdims=True)
    e2 = jnp.exp(m2 - m1)
    p1 = 1.0 / (1.0 + e2)
    p2 = e2 / (1.0 + e2)
    oh1 = lane == i1
    oh2 = lane == i2
    both = jnp.where(jnp.logical_or(oh1, oh2), 1.0, 0.0).astype(BF16)

    sb = MOE_TILE
    ri = lax.broadcasted_iota(I32, (sb, sb), 0)
    ci = lax.broadcasted_iota(I32, (sb, sb), 1)
    strict = jnp.where(ri > ci, 1.0, 0.0).astype(BF16)
    carry = jnp.zeros((1, LANES), F32)
    ranks = []
    for s in range(tb // sb):
        rows = both[s * sb:(s + 1) * sb]
        ranks.append(jnp.dot(strict, rows, preferred_element_type=F32) + carry)
        carry = carry + jnp.sum(rows.astype(F32), axis=0, keepdims=True)
    rank = jnp.concatenate(ranks, axis=0)
    seg = jnp.floor((carry + (MOE_TILE - 1)) * (1.0 / MOE_TILE)) * MOE_TILE
    ui = lax.broadcasted_iota(I32, (LANES, LANES), 0)
    uj = lax.broadcasted_iota(I32, (LANES, LANES), 1)
    before = jnp.where(ui < uj, 1.0, 0.0)
    off = jnp.dot(jnp.broadcast_to(seg, (SUBLANES, LANES)), before, preferred_element_type=F32,
                  precision=lax.Precision.HIGHEST)[0:1, :]
    dest = off + rank
    d1 = jnp.sum(jnp.where(oh1, dest, 0.0), axis=1, keepdims=True)
    d2 = jnp.sum(jnp.where(oh2, dest, 0.0), axis=1, keepdims=True)
    col = jnp.where(lane_i == 0, d1, jnp.where(lane_i == 1, d2, jnp.where(lane_i == 2, p1,
                                                                          jnp.where(lane_i == 3, p2, 0.0))))
    col_ref[...] = col
    row_ref[...] = col.T[0:SUBLANES, :]

    end_rows = jnp.broadcast_to(off + seg, (LANES, LANES)).T
    start = (lax.broadcasted_iota(I32, (LANES, LANES), 1) * MOE_TILE).astype(F32)
    is_expert = lax.broadcasted_iota(I32, (LANES, LANES), 0) < N_EXPERTS
    done = jnp.where(jnp.logical_and(is_expert, end_rows <= start), 1.0, 0.0)
    expert = jnp.minimum(jnp.sum(done, axis=0, keepdims=True), float(N_EXPERTS - 1))
    total = jnp.sum(jnp.where(lane_i[0:1] < N_EXPERTS, seg, 0.0), axis=1, keepdims=True)
    valid = jnp.where(start[0:1] < total, 1.0, 0.0)
    sub = lax.broadcasted_iota(I32, (SUBLANES, LANES), 0)
    meta_ref[...] = jnp.where(sub == 0, expert, jnp.where(sub == 1, valid, 0.0)).astype(I32)


def _route(x, g, w_router, tb):
    m = x.shape[0]
    nblk = m // tb
    return pl.pallas_call(
        _route_kernel,
        grid=(nblk,),
        in_specs=[pl.BlockSpec((tb, D_MODEL), lambda b: (b, 0)),
                  pl.BlockSpec((1, D_MODEL), lambda b: (0, 0)),
                  pl.BlockSpec((D_MODEL, LANES), lambda b: (0, 0))],
        out_specs=[pl.BlockSpec((tb, D_MODEL), lambda b: (b, 0)),
                   pl.BlockSpec((SUBLANES, tb), lambda b: (0, b)),
                   pl.BlockSpec((tb, LANES), lambda b: (b, 0)),
                   pl.BlockSpec((None, SUBLANES, LANES), lambda b: (b, 0, 0))],
        out_shape=[jax.ShapeDtypeStruct((m, D_MODEL), BF16),
                   jax.ShapeDtypeStruct((SUBLANES, m), F32),
                   jax.ShapeDtypeStruct((m, LANES), F32),
                   jax.ShapeDtypeStruct((nblk, SUBLANES, LANES), I32)],
        compiler_params=_cparams(("parallel",)),
        name="moe_route",
    )(x, g, w_router)


def _gffn_kernel(te_ref, tv_ref, hb_ref, row_ref, wg_ref, wu_ref, wd_ref, ys_ref, xg_ref, acc_ref, *, nt, n_f):
    del te_ref
    b, t, j = pl.program_id(0), pl.program_id(1), pl.program_id(2)
    valid = tv_ref[b * nt + t] > 0
    tb = hb_ref.shape[0]

    @pl.when(jnp.logical_and(valid, j == 0))
    def _():
        slot = (t * MOE_TILE + lax.broadcasted_iota(I32, (MOE_TILE, tb), 0)).astype(F32)
        hit = jnp.logical_or(row_ref[0:1, :] == slot, row_ref[1:2, :] == slot)
        onehot = jnp.where(hit, 1.0, 0.0).astype(BF16)
        xg_ref[...] = jnp.dot(onehot, hb_ref[...], preferred_element_type=F32).astype(BF16)
        acc_ref[...] = jnp.zeros(acc_ref.shape, F32)

    @pl.when(valid)
    def _():
        xg = xg_ref[...]
        gate = jnp.dot(xg, wg_ref[...], preferred_element_type=F32)
        up = jnp.dot(xg, wu_ref[...], preferred_element_type=F32)
        acc_ref[...] += jnp.dot((_silu(gate) * up).astype(BF16), wd_ref[...], preferred_element_type=F32)

    @pl.when(j == n_f - 1)
    def _():
        ys_ref[...] = jnp.where(valid, acc_ref[...], 0.0).astype(BF16)


def _gffn(tile_expert, tile_valid, hb, rowinfo, w_gate, w_up, w_down, tb):
    m = hb.shape[0]
    nblk = m // tb
    nt = _moe_tiles(tb)
    d_ff = w_gate.shape[2]
    tf = d_ff // 2
    n_f = d_ff // tf
    fidx = lambda b, t, j, te, tv: jnp.where(tv[b * nt + t] > 0, j, n_f - 1)
    grid_spec = pltpu.PrefetchScalarGridSpec(
        num_scalar_prefetch=2,
        grid=(nblk, nt, n_f),
        in_specs=[pl.BlockSpec((tb, D_MODEL), lambda b, t, j, te, tv: (b, 0)),
                  pl.BlockSpec((SUBLANES, tb), lambda b, t, j, te, tv: (0, b)),
                  pl.BlockSpec((None, D_MODEL, tf), lambda b, t, j, te, tv: (te[b * nt + t], 0, fidx(b, t, j, te, tv))),
                  pl.BlockSpec((None, D_MODEL, tf), lambda b, t, j, te, tv: (te[b * nt + t], 0, fidx(b, t, j, te, tv))),
                  pl.BlockSpec((None, tf, D_MODEL), lambda b, t, j, te, tv: (te[b * nt + t], fidx(b, t, j, te, tv), 0))],
        out_specs=pl.BlockSpec((MOE_TILE, D_MODEL), lambda b, t, j, te, tv: (b * nt + t, 0)),
        scratch_shapes=[pltpu.VMEM((MOE_TILE, D_MODEL), BF16), pltpu.VMEM((MOE_TILE, D_MODEL), F32)],
    )
    return pl.pallas_call(
        functools.partial(_gffn_kernel, nt=nt, n_f=n_f),
        grid_spec=grid_spec,
        out_shape=jax.ShapeDtypeStruct((nblk * nt * MOE_TILE, D_MODEL), BF16),
        compiler_params=_cparams(("parallel", "arbitrary", "arbitrary")),
        name="moe_experts",
    )(tile_expert, tile_valid, hb, rowinfo, w_gate, w_up, w_down)


def _combine_kernel(tv_ref, x_ref, col_ref, ys_ref, o_ref, *, nt):
    b, t = pl.program_id(0), pl.program_id(1)

    @pl.when(t == 0)
    def _():
        o_ref[...] = x_ref[...]

    @pl.when(tv_ref[b * nt + t] > 0)
    def _():
        tb = x_ref.shape[0]
        slot = (t * MOE_TILE + lax.broadcasted_iota(I32, (tb, MOE_TILE), 1)).astype(F32)
        weight = (jnp.where(col_ref[:, 0:1] == slot, col_ref[:, 2:3], 0.0)
                  + jnp.where(col_ref[:, 1:2] == slot, col_ref[:, 3:4], 0.0))
        o_ref[...] += jnp.dot(weight.astype(BF16), ys_ref[...], preferred_element_type=F32)


def _combine(tile_valid, x, colinfo, ys, tb):
    m = x.shape[0]
    nblk = m // tb
    nt = _moe_tiles(tb)
    grid_spec = pltpu.PrefetchScalarGridSpec(
        num_scalar_prefetch=1,
        grid=(nblk, nt),
        in_specs=[pl.BlockSpec((tb, D_MODEL), lambda b, t, tv: (b, 0)),
                  pl.BlockSpec((tb, LANES), lambda b, t, tv: (b, 0)),
                  pl.BlockSpec((MOE_TILE, D_MODEL), lambda b, t, tv: (b * nt + t, 0))],
        out_specs=pl.BlockSpec((tb, D_MODEL), lambda b, t, tv: (b, 0)),
    )
    return pl.pallas_call(
        functools.partial(_combine_kernel, nt=nt),
        grid_spec=grid_spec,
        out_shape=jax.ShapeDtypeStruct((m, D_MODEL), F32),
        compiler_params=_cparams(("parallel", "arbitrary")),
        name="moe_combine",
    )(tile_valid, x, colinfo, ys)


def _moe_grouped(x, g, w_router, w_gate, w_up, w_down):
    m = x.shape[0]
    tb = min(MOE_BLOCK, m)
    nt = _moe_tiles(tb)
    hb, rowinfo, colinfo, meta = _route(x, g, w_router, tb)
    tile_expert = meta[:, 0, :nt].reshape(-1)
    tile_valid = meta[:, 1, :nt].reshape(-1)
    ys = _gffn(tile_expert, tile_valid, hb, rowinfo, w_gate, w_up, w_down, tb)
    return _combine(tile_valid, x, colinfo, ys, tb)


def _decmix_kernel(cb_ref, cc_ref, cx_ref, xs_ref, bc_ref, dt_ref, sc0_ref, sc1_ref,
                   sx0_ref, sx1_ref, sx2_ref, sb0_ref, sb1_ref, sb2_ref,
                   cw_ref, cwx_ref, cwb_ref, cbx_ref, cbb_ref, dtb_ref, alog_ref,
                   pa_ref, u_ref, xso_ref, bco_ref, dto_ref, ea_ref):
    u = cc_ref[...] * cx_ref[...]
    u_ref[...] = u
    pa_ref[...] = cb_ref[...] * (cw_ref[0:1, :] * sc0_ref[...] + cw_ref[1:2, :] * sc1_ref[...] + cw_ref[2:3, :] * u)
    xso_ref[...] = _silu(cwx_ref[0:1, :] * sx0_ref[...] + cwx_ref[1:2, :] * sx1_ref[...]
                         + cwx_ref[2:3, :] * sx2_ref[...] + cwx_ref[3:4, :] * xs_ref[...] + cbx_ref[...])
    bco_ref[...] = _silu(cwb_ref[0:1, :] * sb0_ref[...] + cwb_ref[1:2, :] * sb1_ref[...]
                         + cwb_ref[2:3, :] * sb2_ref[...] + cwb_ref[3:4, :] * bc_ref[...] + cbb_ref[...])
    dt = _softplus(dt_ref[...] + dtb_ref[...])
    dto_ref[...] = dt
    ea_ref[...] = jnp.exp(dt * (-jnp.exp(alog_ref[...])))


def _decmix(proj, st_conv, st_ssm_conv, conv_w, cw, cbias, dtb, alog):
    nb = proj.shape[0]
    pcol = lambda w, off: pl.BlockSpec((nb, w), lambda i: (0, off // w))
    full = lambda w: pl.BlockSpec((nb, w), lambda i: (0, 0))
    const = lambda r, w, j: pl.BlockSpec((r, w), lambda i: (0, j))
    sx = [st_ssm_conv[:, t, :512] for t in range(3)]
    sb = [st_ssm_conv[:, t, 512:] for t in range(3)]
    return pl.pallas_call(
        _decmix_kernel,
        grid=(1,),
        in_specs=[pcol(512, CB_OFF), pcol(512, CC_OFF), pcol(512, CX_OFF), pcol(512, XS_OFF), pcol(512, BC_OFF),
                  pcol(LANES, DT_OFF)] + [full(512)] * 8
                 + [const(3, 512, 0), const(4, 512, 0), const(4, 512, 1), const(1, 512, 0), const(1, 512, 1),
                    const(1, LANES, 0), const(1, LANES, 0)],
        out_specs=[full(512), full(512), full(512), full(512), full(LANES), full(LANES)],
        out_shape=[jax.ShapeDtypeStruct((nb, 512), F32)] * 4 + [jax.ShapeDtypeStruct((nb, LANES), F32)] * 2,
        compiler_params=_cparams(("arbitrary",)),
        name="decode_mix",
    )(proj, proj, proj, proj, proj, proj, st_conv[:, 0], st_conv[:, 1], *sx, *sb,
      conv_w, cw, cw, cbias, cbias, dtb, alog)


def _decssd_kernel(h_ref, dt_ref, xs_ref, b_ref, c_ref, ea_ref, dsk_ref, y_ref, ho_ref):
    xs = xs_ref[...]
    h_new = h_ref[...] * ea_ref[...] + (xs * dt_ref[...]) * b_ref[...]
    ho_ref[...] = h_new
    y_ref[...] = jnp.sum(h_new * c_ref[...], axis=-1, keepdims=True) + dsk_ref[...] * xs


def _decssd(h0, dt, xs, bh, ch, ea, dsk):
    nb = h0.shape[0]
    blk = lambda a, b: pl.BlockSpec((None, SSM_HEADS, a, b), lambda i: (i, 0, 0, 0))
    return pl.pallas_call(
        _decssd_kernel,
        grid=(nb,),
        in_specs=[blk(SSM_HEAD_DIM, SSM_STATE), blk(1, 1), blk(SSM_HEAD_DIM, 1), blk(1, SSM_STATE),
                  blk(1, SSM_STATE), blk(1, 1), pl.BlockSpec((SSM_HEADS, 1, 1), lambda i: (0, 0, 0))],
        out_specs=[blk(SSM_HEAD_DIM, 1), blk(SSM_HEAD_DIM, SSM_STATE)],
        out_shape=[jax.ShapeDtypeStruct((nb, SSM_HEADS, SSM_HEAD_DIM, 1), F32),
                   jax.ShapeDtypeStruct((nb, SSM_HEADS, SSM_HEAD_DIM, SSM_STATE), F32)],
        compiler_params=_cparams(("parallel",)),
        name="decode_ssd",
    )(h0, dt, xs, bh, ch, ea, dsk)


DECODE_PAGE_GROUP = 16


def _decscore_kernel(pt_ref, qi_ref, w_ref, *refs):
    del pt_ref
    kid_refs, o_ref = refs[:-1], refs[-1]
    kid = jnp.concatenate([r[...].astype(BF16) for r in kid_refs], axis=0)
    s = lax.dot_general(qi_ref[:, 0:IDX_DIM].astype(BF16), kid, NT_DIMS, preferred_element_type=F32)
    sc = jnp.maximum(s, 0.0) * w_ref[:, 0:1]
    o_ref[...] = jnp.sum(sc, axis=0, keepdims=True)


def _page_specs(block, layer, pg):
    zeros = (0,) * (len(block) - 2)
    return [pl.BlockSpec(block, lambda b, p, pt, t=t: (layer, pt[b, p * pg + t]) + zeros) for t in range(pg)]


def _decscore(page_table, qip, wrows, pool_kidx, layer):
    nb, n_pages = page_table.shape
    pg = math.gcd(DECODE_PAGE_GROUP, n_pages)
    grid_spec = pltpu.PrefetchScalarGridSpec(
        num_scalar_prefetch=1,
        grid=(nb, n_pages // pg),
        in_specs=[pl.BlockSpec((None, N_IDX_HEADS, LANES), lambda b, p, pt: (b, 0, 0)),
                  pl.BlockSpec((None, N_IDX_HEADS, PAGE_SIZE), lambda b, p, pt: (b, 0, 0))]
                 + _page_specs((None, None, PAGE_SIZE, IDX_DIM), layer, pg),
        out_specs=pl.BlockSpec((None, None, 1, pg * PAGE_SIZE), lambda b, p, pt: (b, p, 0, 0)),
    )
    out = pl.pallas_call(
        _decscore_kernel,
        grid_spec=grid_spec,
        out_shape=jax.ShapeDtypeStruct((nb, n_pages // pg, 1, pg * PAGE_SIZE), F32),
        compiler_params=_cparams(("parallel", "arbitrary")),
        name="decode_scores",
    )(page_table, qip, wrows, *([pool_kidx] * pg))
    return out.reshape(nb, n_pages * PAGE_SIZE)


def _decselect_kernel(sc_ref, qi_ref, ki_ref, w_ref, sel_ref, self_ref, keys_ref, *, k_sel, kc, nbits):
    nb, past = sc_ref.shape
    nck = past // kc
    lane_k = lax.broadcasted_iota(I32, (nb, kc), 1)
    ki = ki_ref[...].astype(F32)
    own = jnp.zeros((nb, 1), F32)
    for h in range(N_IDX_HEADS):
        s = jnp.sum(qi_ref[:, h * LANES:(h + 1) * LANES].astype(F32) * ki, axis=1, keepdims=True)
        own = own + jnp.maximum(s, 0.0) * w_ref[:, h:h + 1]
    own_key = _sortable(own)

    def key_body(c, carry):
        off = pl.multiple_of(c * kc, kc)
        keys_ref[:, pl.ds(off, kc)] = _sortable(sc_ref[:, pl.ds(off, kc)])
        return carry

    lax.fori_loop(0, nck, key_body, 0)

    def count_where(pred):
        def body(c, acc):
            off = pl.multiple_of(c * kc, kc)
            f = jnp.where(pred(keys_ref[:, pl.ds(off, kc)], off), 1.0, 0.0)
            part = f[:, 0:LANES]
            for t in range(1, kc // LANES):
                part = part + f[:, t * LANES:(t + 1) * LANES]
            return acc + part

        acc = lax.fori_loop(0, nck, body, jnp.zeros((nb, LANES), F32))
        return jnp.sum(acc, axis=1, keepdims=True)

    def bit_body(i, t):
        cand = t + lax.shift_left(jnp.int32(1), 31 - i)
        cnt = count_where(lambda kk, off: kk >= cand) + jnp.where(own_key >= cand, 1.0, 0.0)
        return jnp.where(cnt >= k_sel, cand, t)

    thr = lax.fori_loop(0, 32, bit_body, jnp.full((nb, 1), INT_MIN, I32))
    thr = jnp.maximum(thr, KEY_NEG_INF)
    finite_thr = thr > KEY_NEG_INF
    cnt_gt = count_where(lambda kk, off: kk > thr) + jnp.where(own_key > thr, 1.0, 0.0)
    need = k_sel - cnt_gt

    def xbody(i, x):
        cand = x + lax.shift_left(jnp.int32(1), nbits - 1 - i)
        cnt = count_where(lambda kk, off: jnp.logical_and(kk == thr, off + lane_k < cand))
        return jnp.where(cnt < need, cand, x)

    xcut = lax.fori_loop(0, nbits, xbody, jnp.zeros((nb, 1), I32))
    xcut = jnp.where(finite_thr, xcut, -1)
    ties_past = count_where(lambda kk, off: kk == thr)
    own_tie = jnp.logical_and(jnp.logical_and(own_key == thr, finite_thr), ties_past < need)
    self_ref[...] = jnp.broadcast_to(
        jnp.where(jnp.logical_or(own_key > thr, own_tie), 1.0, 0.0), (nb, LANES))

    def sel_body(c, carry):
        off = pl.multiple_of(c * kc, kc)
        kk = keys_ref[:, pl.ds(off, kc)]
        sel = jnp.logical_or(kk > thr, jnp.logical_and(kk == thr, off + lane_k <= xcut))
        sel_ref[:, pl.ds(off, kc)] = jnp.where(sel, 1.0, 0.0)
        return carry

    lax.fori_loop(0, nck, sel_body, 0)


def _decselect(scores, qip, kib, wi):
    nb, past = scores.shape
    k_sel = min(TOPK_MAX, (past + 1) // 4)
    kc = 512 if past % 512 == 0 else PAGE_SIZE
    nbits = max(1, past.bit_length())
    full = lambda w: pl.BlockSpec((nb, w), lambda i: (0, 0))
    return pl.pallas_call(
        functools.partial(_decselect_kernel, k_sel=k_sel, kc=kc, nbits=nbits),
        grid=(1,),
        in_specs=[full(past), full(N_IDX_HEADS * LANES), full(LANES), full(LANES)],
        out_specs=[full(past), full(LANES)],
        out_shape=[jax.ShapeDtypeStruct((nb, past), F32), jax.ShapeDtypeStruct((nb, LANES), F32)],
        scratch_shapes=[pltpu.VMEM((nb, past), I32)],
        compiler_params=_cparams(("arbitrary",)),
        name="decode_select",
    )(scores, qip, kib, wi)


def _decattn_kernel(pt_ref, qp_ref, sel_ref, kn_ref, vn_ref, self_ref, *refs, n_steps):
    del pt_ref
    pg = (len(refs) - 4) // 2
    kp_refs, vp_refs = refs[:pg], refs[pg:2 * pg]
    o_ref, m_ref, l_ref, acc_ref = refs[2 * pg:]
    p = pl.program_id(1)

    @pl.when(p == 0)
    def _():
        m_ref[...] = jnp.full(m_ref.shape, NEG_BIG, F32)
        l_ref[...] = jnp.zeros(l_ref.shape, F32)
        acc_ref[...] = jnp.zeros(acc_ref.shape, F32)

    q = qp_ref[...]
    kcat = jnp.concatenate([r[...].astype(BF16) for r in kp_refs], axis=0)
    vcat = jnp.concatenate([r[...].astype(BF16) for r in vp_refs], axis=0)
    s = lax.dot_general(q.astype(BF16), kcat, NT_DIMS, preferred_element_type=F32)
    s = jnp.where(sel_ref[...] > 0.0, s, -jnp.inf)
    m_old = m_ref[:, 0:1]
    m_new = jnp.maximum(m_old, jnp.max(s, axis=1, keepdims=True))
    alpha = jnp.exp(m_old - m_new)
    pr = jnp.exp(s - m_new)
    l_new = alpha * l_ref[:, 0:1] + jnp.sum(pr, axis=1, keepdims=True)
    acc_ref[...] = alpha * acc_ref[...] + jnp.dot(pr.astype(BF16), vcat, preferred_element_type=F32)
    m_ref[...] = jnp.broadcast_to(m_new, m_ref.shape)
    l_ref[...] = jnp.broadcast_to(l_new, l_ref.shape)

    @pl.when(p == n_steps - 1)
    def _():
        s_own = jnp.sum(q * kn_ref[...], axis=1, keepdims=True)
        s_own = jnp.where(self_ref[:, 0:1] > 0.0, s_own, -jnp.inf)
        m_o = m_ref[:, 0:1]
        m_n = jnp.maximum(m_o, s_own)
        al = jnp.exp(m_o - m_n)
        p_own = jnp.exp(s_own - m_n)
        l_n = al * l_ref[:, 0:1] + p_own
        acc = al * acc_ref[...] + p_own.astype(BF16).astype(F32) * vn_ref[...]
        o_ref[...] = acc / l_n


def _decattn(page_table, qp, sel, pool_k, pool_v, kb, vb, self_sel, layer):
    nb, n_pages = page_table.shape
    pg = math.gcd(DECODE_PAGE_GROUP, n_pages)
    n_steps = n_pages // pg
    row = lambda r: pl.BlockSpec((None, r, LANES), lambda b, p, pt: (b, 0, 0))
    pages = _page_specs((None, None, PAGE_SIZE, LANES), layer, pg)
    grid_spec = pltpu.PrefetchScalarGridSpec(
        num_scalar_prefetch=1,
        grid=(nb, n_steps),
        in_specs=[row(N_HEADS),
                  pl.BlockSpec((None, None, 1, pg * PAGE_SIZE), lambda b, p, pt: (b, p, 0, 0)),
                  row(1), row(1), row(1)] + pages + pages,
        out_specs=row(N_HEADS),
        scratch_shapes=[pltpu.VMEM((N_HEADS, LANES), F32)] * 3,
    )
    return pl.pallas_call(
        functools.partial(_decattn_kernel, n_steps=n_steps),
        grid_spec=grid_spec,
        out_shape=jax.ShapeDtypeStruct((nb, N_HEADS, LANES), F32),
        compiler_params=_cparams(("parallel", "arbitrary")),
        name="decode_attend",
    )(page_table, qp, sel.reshape(nb, n_steps, 1, pg * PAGE_SIZE), kb, vb, self_sel,
      *([pool_k] * pg), *([pool_v] * pg))


def _pack_w_in(w):
    d = w.shape[0]
    pad = lambda n: jnp.zeros((d, n), w.dtype)
    cols = [w[:, 4172:7244],
            w[:, 0:2048],
            w[:, 2628:3140],
            w[:, 3140:4164],
            w[:, 2048:2304],
            w[:, 2304:2560],
            w[:, 2560:2628], pad(60),
            w[:, 4164:4172], pad(120),
            pad(N_PROJ - 7424)]
    return jnp.concatenate(cols, axis=1).astype(BF16)


def _pad_lanes(v):
    return jnp.zeros((1, LANES), F32).at[0, :v.shape[0]].set(v)


def _rope_tables(pos):
    half = HEAD_DIM // 2
    inv = ROPE_THETA ** (-jnp.arange(half, dtype=F32) / half)
    ang = pos.astype(F32)[:, None] * inv[None, :]
    cos, sin = jnp.cos(ang), jnp.sin(ang)
    cos2 = jnp.concatenate([cos, cos], axis=1)
    sin2 = jnp.concatenate([-sin, sin], axis=1)
    return jnp.tile(cos2, (1, 2)), jnp.tile(sin2, (1, 2))


def _head_halves(o):
    hpg = N_HEADS // N_KV_HEADS
    parts = [o[:, h, (h // hpg) * HEAD_DIM:(h // hpg + 1) * HEAD_DIM] for h in range(N_HEADS)]
    return jnp.concatenate(parts, axis=-1)


def kernel(x_prompt, x_sample, cache_k, cache_v, cache_kidx, state_conv, state_ssm_conv, state_ssm, page_table,
           norm1, w_in, b_gate, conv_w, q_norm, k_norm, w_pc, w_pa, ssm_conv_w, ssm_conv_b, dt_bias, a_log,
           d_skip, ssm_norm, w_ps, w_o, norm2, w_gate_dense, w_up_dense, w_down_dense, w_router, w_gate_moe,
           w_up_moe, w_down_moe):
    nb, seq, _ = x_prompt.shape
    db = x_sample.shape[0]
    depth = w_in.shape[0]
    n_pages = page_table.shape[1]
    past = n_pages * PAGE_SIZE

    hp = x_prompt.reshape(nb * seq, D_MODEL)
    hs = x_sample.reshape(db, D_MODEL)
    cos_p, sin_p = _rope_tables(jnp.tile(jnp.arange(seq), nb))
    cos_s, sin_s = _rope_tables(jnp.full((db,), past))

    pool_k = cache_k.reshape(*cache_k.shape[:3], LANES)
    pool_v = cache_v.reshape(*cache_v.shape[:3], LANES)

    outs_p = [[] for _ in range(6)]
    outs_s = [[] for _ in range(6)]
    for l in range(depth):
        wp = _pack_w_in(w_in[l])
        g1 = norm1[l].reshape(1, D_MODEL)
        g2 = norm2[l].reshape(1, D_MODEL)
        qg = jnp.tile(q_norm[l], 2).reshape(1, LANES)
        kg = jnp.tile(k_norm[l], 2).reshape(1, LANES)
        cw = ssm_conv_w[l]
        cbias = ssm_conv_b[l].reshape(1, SSM_CONV_DIM)
        dtb = _pad_lanes(dt_bias[l])
        alog = _pad_lanes(a_log[l])
        dsk = jnp.repeat(d_skip[l], SSM_HEAD_DIM).reshape(1, SSM_INNER)
        sn = ssm_norm[l].reshape(1, SSM_INNER)
        wpc, wpa, wps, wo = (w.astype(BF16) for w in (w_pc[l], w_pa[l], w_ps[l], w_o[l]))
        i = l // 2
        if l % 2 == 0:
            routed = False
            wr = jnp.zeros((D_MODEL, LANES), F32)
            wg, wu, wd = (w[i:i + 1].astype(BF16) for w in (w_gate_dense, w_up_dense, w_down_dense))
        else:
            routed = True
            wr = jnp.zeros((D_MODEL, LANES), F32).at[:, :N_EXPERTS].set(w_router[i])
            wg, wu, wd = (w[i].astype(BF16) for w in (w_gate_moe, w_up_moe, w_down_moe))

        proj = _inproj(hp, g1, wp)
        qt, kf, kb, vt, qit, kif, kib, wt = _prep(proj, cos_p, sin_p, qg, kg, True)
        attn = _dsa_prompt(qt, qit, wt, kb, vt, kib, nb, seq)
        pre_a, conv_st = _conva_prompt(proj, conv_w[l], nb, seq)
        y_ssd, h_last = _ssd_prompt(proj, cw, cbias, dtb, alog, dsk, nb, seq)
        hp = _merge(hp, pre_a, attn, y_ssd, proj, sn, b_gate[l], wpc, wpa, wps, wo)
        if routed and hp.shape[0] % MOE_TILE == 0:
            hp = _moe_grouped(hp, g2, wr, wg, wu, wd)
        else:
            hp = _ffn(hp, g2, wr, wg, wu, wd, routed)
        proj3 = proj.reshape(nb, seq, N_PROJ)
        outs_p[0].append(kf.reshape(nb, seq, N_KV_HEADS, HEAD_DIM))
        outs_p[1].append(proj3[:, :, V_OFF:V_OFF + LANES].reshape(nb, seq, N_KV_HEADS, HEAD_DIM))
        outs_p[2].append(kif[:, :IDX_DIM].reshape(nb, seq, IDX_DIM))
        outs_p[3].append(conv_st)
        outs_p[4].append(proj3[:, seq - 3:, XS_OFF:XS_OFF + SSM_CONV_DIM])
        outs_p[5].append(h_last.reshape(nb, SSM_HEADS, SSM_HEAD_DIM, SSM_STATE))

        proj = _inproj(hs, g1, wp)
        qp, kf, kb, vb, qip, kif, kib, wi = _prep(proj, cos_s, sin_s, qg, kg, False)
        wrows = jnp.broadcast_to(wi[:, :N_IDX_HEADS, None], (db, N_IDX_HEADS, PAGE_SIZE))
        scores = _decscore(page_table, qip.astype(F32).reshape(db, N_IDX_HEADS, LANES), wrows, cache_kidx, l)
        sel, self_sel = _decselect(scores, qip, kib, wi)
        o = _decattn(page_table, qp.astype(F32).reshape(db, N_HEADS, LANES), sel, pool_k, pool_v,
                     kb.astype(F32).reshape(db, 1, LANES), vb.astype(F32).reshape(db, 1, LANES),
                     self_sel.reshape(db, 1, LANES), l)
        attn = _head_halves(o)
        pre_a, u, xs, bcv, dt, ea = _decmix(proj, state_conv[l], state_ssm_conv[l], conv_w[l], cw, cbias, dtb, alog)
        hpg = SSM_HEADS // SSM_GROUPS
        xs4 = xs.reshape(db, SSM_HEADS, SSM_HEAD_DIM, 1)
        bh = jnp.repeat(bcv[:, :SSM_GROUPS * SSM_STATE].reshape(db, SSM_GROUPS, 1, SSM_STATE), hpg, axis=1)
        ch = jnp.repeat(bcv[:, SSM_GROUPS * SSM_STATE:].reshape(db, SSM_GROUPS, 1, SSM_STATE), hpg, axis=1)
        y4, h_new = _decssd(state_ssm[l], dt[:, :SSM_HEADS, None, None], xs4, bh, ch, ea[:, :SSM_HEADS, None, None],
                            d_skip[l].reshape(SSM_HEADS, 1, 1))
        hs = _merge(hs, pre_a, attn, y4.reshape(db, SSM_INNER), proj, sn, b_gate[l], wpc, wpa, wps, wo)
        hs = _ffn(hs, g2, wr, wg, wu, wd, routed)
        outs_s[0].append(kf.reshape(db, 1, N_KV_HEADS, HEAD_DIM))
        outs_s[1].append(proj[:, V_OFF:V_OFF + LANES].reshape(db, 1, N_KV_HEADS, HEAD_DIM))
        outs_s[2].append(kif[:, :IDX_DIM].reshape(db, 1, IDX_DIM))
        outs_s[3].append(jnp.stack([state_conv[l][:, 1], u], axis=1))
        outs_s[4].append(jnp.concatenate(
            [state_ssm_conv[l][:, 1:], proj[:, None, XS_OFF:XS_OFF + SSM_CONV_DIM]], axis=1))
        outs_s[5].append(h_new)

    return (hp.reshape(nb, seq, D_MODEL), hs.reshape(db, 1, D_MODEL),
            *(jnp.stack(o) for o in outs_p), *(jnp.stack(o) for o in outs_s))
```

```python
import functools
import math

import jax
import jax.numpy as jnp
from jax import lax
from jax.experimental import pallas as pl
from jax.experimental.pallas import tpu as pltpu

F32 = jnp.float32
BF16 = jnp.bfloat16
I32 = jnp.int32

D_MODEL = 1024
D_CONV = 512
N_HEADS = 8
HEAD_DIM = 64
N_KV_HEADS = 2
N_IDX_HEADS = 4
IDX_DIM = 64
TOPK_MAX = 256
Q_BLOCK = 128
ROPE_THETA = 10000.0
SSM_INNER = 512
SSM_HEADS = 8
SSM_HEAD_DIM = 64
SSM_GROUPS = 2
SSM_STATE = 128
SSM_CHUNK = 128
SSM_CONV_DIM = 1024
PAGE_SIZE = 128
N_EXPERTS = 8
EPS = 1e-6

LANES = 128
SUBLANES = 8
VMEM_LIMIT = 52 * 1024 * 1024

G_OFF, CB_OFF, CC_OFF, CX_OFF, Q_OFF, Z_OFF, XS_OFF, BC_OFF = 0, 3072, 3584, 4096, 4608, 5120, 5632, 6144
K_OFF, V_OFF, QI_OFF, KW_OFF, DT_OFF = 6656, 6784, 6912, 7168, 7296
N_PROJ = 7680

INT_MIN = -2 ** 31
INT_MAX = 2 ** 31 - 1
KEY_NEG_INF = (-8388608) ^ 0x7FFFFFFF
NEG_BIG = -1e30
LOG2_E = 1.4426950408889634

NT_DIMS = (((1,), (1,)), ((), ()))


def _cparams(sem):
    return pltpu.CompilerParams(dimension_semantics=sem, vmem_limit_bytes=VMEM_LIMIT)


def _sigmoid(x):
    return 1.0 / (1.0 + jnp.exp(-x))


def _silu(x):
    return x * _sigmoid(x)


def _softplus(x):
    return jnp.maximum(x, 0.0) + jnp.log1p(jnp.exp(-jnp.abs(x)))


def _sortable(x):
    bits = pltpu.bitcast(x, I32)
    return bits ^ (lax.shift_right_arithmetic(bits, 31) & 0x7FFFFFFF)


def _rms(x, g):
    return x * lax.rsqrt(jnp.mean(x * x, axis=-1, keepdims=True) + EPS) * g


def _inproj_kernel(x_ref, g_ref, w_ref, o_ref, h_ref):
    @pl.when(pl.program_id(1) == 0)
    def _():
        h_ref[...] = _rms(x_ref[...], g_ref[...]).astype(BF16)

    o_ref[...] = jnp.dot(h_ref[...], w_ref[...], preferred_element_type=F32)


def _inproj(x, g, w):
    m = x.shape[0]
    tm = min(m, 1024)
    tn = 1536
    return pl.pallas_call(
        _inproj_kernel,
        grid=(m // tm, N_PROJ // tn),
        in_specs=[pl.BlockSpec((tm, D_MODEL), lambda i, j: (i, 0)),
                  pl.BlockSpec((1, D_MODEL), lambda i, j: (0, 0)),
                  pl.BlockSpec((D_MODEL, tn), lambda i, j: (0, j))],
        out_specs=pl.BlockSpec((tm, tn), lambda i, j: (i, j)),
        out_shape=jax.ShapeDtypeStruct((m, N_PROJ), F32),
        scratch_shapes=[pltpu.VMEM((tm, D_MODEL), BF16)],
        compiler_params=_cparams(("parallel", "arbitrary")),
        name="inproj",
    )(x, g, w)


V_T_ROWS = LANES + 16


def _prep_kernel(q_ref, k_ref, v_ref, qi_ref, kw_ref, cos_ref, sin_ref, qg_ref, kg_ref,
                 qp_ref, kf_ref, kb_ref, vb_ref, qip_ref, kif_ref, kib_ref, wo_ref, *, transposed):
    tm = cos_ref.shape[0]

    def put(ref, h, val):
        if transposed:
            vt = val.T.astype(BF16)
            for blk in range(tm // Q_BLOCK):
                ref[blk, h * LANES:(h + 1) * LANES, :] = vt[:, blk * Q_BLOCK:(blk + 1) * Q_BLOCK]
        else:
            ref[:, h * LANES:(h + 1) * LANES] = val.astype(BF16)

    cos = cos_ref[...]
    sin = sin_ref[...]
    lane = lax.broadcasted_iota(I32, (tm, LANES), 1)
    low_half = lane < HEAD_DIM
    first_rot = (lane % HEAD_DIM) < (HEAD_DIM // 2)
    r = lax.broadcasted_iota(I32, (LANES, LANES), 0) // HEAD_DIM
    c = lax.broadcasted_iota(I32, (LANES, LANES), 1) // HEAD_DIM
    seg = jnp.where(r == c, 1.0 / HEAD_DIM, 0.0).astype(BF16)

    def rope(x):
        fwd = pltpu.roll(x, LANES - HEAD_DIM // 2, 1)
        bwd = pltpu.roll(x, HEAD_DIM // 2, 1)
        return x * cos + jnp.where(first_rot, fwd, bwd) * sin

    def headnorm(x, g):
        s = x * x
        hi = s.astype(BF16)
        lo = (s - hi.astype(F32)).astype(BF16)
        ms = (jnp.dot(hi, seg, preferred_element_type=F32)
              + jnp.dot(lo, seg, preferred_element_type=F32))
        return x * lax.rsqrt(ms + EPS) * g

    q_scale = HEAD_DIM ** -0.5 * (LOG2_E if transposed else 1.0)
    qg = qg_ref[...]
    for s in range(N_HEADS // 2):
        slab = rope(headnorm(q_ref[:, s * LANES:(s + 1) * LANES], qg)) * q_scale
        swapped = pltpu.roll(slab, HEAD_DIM, 1)
        grp = (2 * s) // (N_HEADS // N_KV_HEADS)
        for hh in range(2):
            h = 2 * s + hh
            src = slab if hh == grp else swapped
            keep = low_half if grp == 0 else jnp.logical_not(low_half)
            put(qp_ref, h, jnp.where(keep, src, 0.0))

    k = rope(headnorm(k_ref[...], kg_ref[...]))
    kf_ref[...] = k
    kb_ref[...] = k.astype(BF16)
    if transposed:
        vb_ref[0:LANES, :] = v_ref[...].T.astype(BF16)
        vb_ref[LANES:V_T_ROWS, :] = jnp.ones((V_T_ROWS - LANES, tm), BF16)
    else:
        vb_ref[...] = v_ref[...].astype(BF16)

    for s in range(N_IDX_HEADS // 2):
        slab = rope(qi_ref[:, s * LANES:(s + 1) * LANES]) * (IDX_DIM ** -0.5)
        swapped = pltpu.roll(slab, HEAD_DIM, 1)
        for hh in range(2):
            h = 2 * s + hh
            src = slab if hh == 0 else swapped
            put(qip_ref, h, jnp.where(low_half, src, 0.0))

    kw = kw_ref[...]
    ki = rope(kw)
    kif_ref[...] = ki
    kib_ref[...] = jnp.where(low_half, ki, 0.0).astype(BF16)
    wi = pltpu.roll(kw, HEAD_DIM, 1) * (N_IDX_HEADS ** -0.5)
    if transposed:
        wt = wi.T[0:SUBLANES, :]
        for blk in range(tm // Q_BLOCK):
            wo_ref[blk] = wt[:, blk * Q_BLOCK:(blk + 1) * Q_BLOCK]
    else:
        wo_ref[...] = wi


def _prep(proj, cos, sin, qg, kg, transposed):
    m = proj.shape[0]
    tm = min(m, 512)
    row = lambda w, off: pl.BlockSpec((tm, w), lambda i: (i, off // w))
    full = lambda w: pl.BlockSpec((tm, w), lambda i: (i, 0))
    const = pl.BlockSpec((1, LANES), lambda i: (0, 0))
    if transposed:
        qblocks = tm // Q_BLOCK
        feat = lambda r: pl.BlockSpec((qblocks, r, Q_BLOCK), lambda i: (i, 0, 0))
        fshape = lambda r, dt: jax.ShapeDtypeStruct((m // Q_BLOCK, r, Q_BLOCK), dt)
        vspec = pl.BlockSpec((V_T_ROWS, tm), lambda i: (0, i))
        vshape = jax.ShapeDtypeStruct((V_T_ROWS, m), BF16)
    else:
        feat = full
        fshape = lambda r, dt: jax.ShapeDtypeStruct((m, r), dt)
        vspec = full(LANES)
        vshape = jax.ShapeDtypeStruct((m, LANES), BF16)
    return pl.pallas_call(
        functools.partial(_prep_kernel, transposed=transposed),
        grid=(m // tm,),
        in_specs=[row(512, Q_OFF), row(LANES, K_OFF), row(LANES, V_OFF), row(256, QI_OFF), row(LANES, KW_OFF),
                  full(LANES), full(LANES), const, const],
        out_specs=[feat(N_HEADS * LANES), full(LANES), full(LANES), vspec,
                   feat(N_IDX_HEADS * LANES), full(LANES), full(LANES), feat(SUBLANES if transposed else LANES)],
        out_shape=[fshape(N_HEADS * LANES, BF16),
                   jax.ShapeDtypeStruct((m, LANES), F32),
                   jax.ShapeDtypeStruct((m, LANES), BF16),
                   vshape,
                   fshape(N_IDX_HEADS * LANES, BF16),
                   jax.ShapeDtypeStruct((m, LANES), F32),
                   jax.ShapeDtypeStruct((m, LANES), BF16),
                   fshape(SUBLANES if transposed else LANES, F32)],
        compiler_params=_cparams(("parallel",)),
        name="prep",
    )(proj, proj, proj, proj, proj, cos, sin, qg, kg)


def _fold_rows(f):
    while f.shape[0] > SUBLANES:
        half = f.shape[0] // 2
        f = f[:half] + f[half:]
    return f


def _dsa_kernel(qt_ref, qit_ref, wt_ref, k_ref, vt_ref, ki_ref, o_ref,
                keys_ref, x_ref, m_ref, kmax_ref, acc_ref, *, k_sel, kc, nbits):
    qb = Q_BLOCK
    j = pl.program_id(1)
    nck = (j * qb + qb + kc - 1) // kc
    qpos = j * qb + lax.broadcasted_iota(I32, (1, qb), 1)
    sub_k = lax.broadcasted_iota(I32, (kc, qb), 0)
    pair = lambda ref, s: jnp.concatenate(
        [ref[(2 * s) * LANES:(2 * s + 1) * LANES, :], ref[(2 * s + 1) * LANES:(2 * s + 2) * LANES, :]], axis=1)
    qi2 = [pair(qit_ref, s) for s in range(N_IDX_HEADS // 2)]
    q2 = [pair(qt_ref, s) for s in range(N_HEADS // 2)]

    def score_body(c, carry):
        off = pl.multiple_of(c * kc, kc)
        kic = ki_ref[pl.ds(off, kc), :]
        acc = jnp.zeros((kc, qb), F32)
        for s in range(N_IDX_HEADS // 2):
            s2 = jnp.dot(kic, qi2[s], preferred_element_type=F32)
            for hh in range(2):
                h = 2 * s + hh
                acc = acc + jnp.maximum(s2[:, hh * qb:(hh + 1) * qb], 0.0) * wt_ref[h:h + 1, :]
        acc = jnp.where(off + sub_k <= qpos, acc, -jnp.inf)
        keys_ref[pl.ds(off, kc), :] = _sortable(acc)
        return carry

    lax.fori_loop(0, nck, score_body, 0)

    def count_where(pred):
        def body(c, acc):
            off = pl.multiple_of(c * kc, kc)
            return acc + _fold_rows(jnp.where(pred(keys_ref[pl.ds(off, kc), :], off), 1.0, 0.0))

        acc = lax.fori_loop(0, nck, body, jnp.zeros((SUBLANES, qb), F32))
        return jnp.sum(acc, axis=0, keepdims=True)

    def bit_body(i, t):
        cand = t + lax.shift_left(jnp.int32(1), 31 - i)
        cnt = count_where(lambda kk, off: kk >= cand)
        return jnp.where(cnt >= k_sel, cand, t)

    thr = lax.fori_loop(0, 32, bit_body, jnp.full((1, qb), INT_MIN, I32))
    thr = jnp.maximum(thr, KEY_NEG_INF)
    finite_thr = thr > KEY_NEG_INF
    cnt_ge = count_where(lambda kk, off: kk >= thr)
    cnt_gt = count_where(lambda kk, off: kk > thr)
    need = k_sel - cnt_gt
    tie = jnp.logical_and(cnt_ge > k_sel, finite_thr)
    x_default = jnp.where(finite_thr, INT_MAX, -1)
    x_ref[...] = jnp.broadcast_to(x_default, x_ref.shape)

    @pl.when(jnp.max(jnp.where(tie, 1.0, 0.0)) > 0.0)
    def _():
        def xbody(i, x):
            cand = x + lax.shift_left(jnp.int32(1), nbits - 1 - i)
            cnt = count_where(lambda kk, off: jnp.logical_and(kk == thr, off + sub_k < cand))
            return jnp.where(cnt < need, cand, x)

        x = lax.fori_loop(0, nbits, xbody, jnp.zeros((1, qb), I32))
        x_ref[...] = jnp.broadcast_to(jnp.where(tie, x, x_default), x_ref.shape)

    xcut = x_ref[0:1, :]
    hpg = N_HEADS // N_KV_HEADS

    @pl.when(j == 0)
    def _():
        r = lax.broadcasted_iota(I32, (LANES, LANES), 0) // HEAD_DIM
        cc = lax.broadcasted_iota(I32, (LANES, LANES), 1) // HEAD_DIM
        seg = jnp.where(r == cc, 1.0, 0.0).astype(BF16)

        def body(c, mx):
            kf = k_ref[pl.ds(pl.multiple_of(c * kc, kc), kc), :].astype(F32)
            gs = jnp.dot((kf * kf).astype(BF16), seg, preferred_element_type=F32)
            while gs.shape[0] > SUBLANES:
                half = gs.shape[0] // 2
                gs = jnp.maximum(gs[:half], gs[half:])
            return jnp.maximum(mx, gs)

        mx = lax.fori_loop(0, k_ref.shape[0] // kc, body, jnp.zeros((SUBLANES, LANES), F32))
        mx = jnp.max(mx, axis=0, keepdims=True)
        lane = lax.broadcasted_iota(I32, (1, LANES), 1)
        other = pltpu.roll(mx, HEAD_DIM, 1)
        kmax_ref[0:1, :] = jnp.where(lane < HEAD_DIM, mx, other)
        kmax_ref[1:2, :] = jnp.where(lane < HEAD_DIM, other, mx)

    def chunk(c):
        off = pl.multiple_of(c * kc, kc)
        kk = keys_ref[pl.ds(off, kc), :]
        sel = jnp.logical_or(kk > thr, jnp.logical_and(kk == thr, off + sub_k <= xcut))
        return sel, k_ref[pl.ds(off, kc), :], vt_ref[:, pl.ds(off, kc)]

    bound = []
    for h in range(N_HEADS):
        qh = qt_ref[h * LANES:(h + 1) * LANES, :].astype(F32)
        qsq = jnp.sum(qh * qh, axis=0, keepdims=True)
        bound.append(jnp.sqrt(qsq * kmax_ref[h // hpg:h // hpg + 1, :]))
    acc_ref[...] = jnp.zeros(acc_ref.shape, F32)

    def fast_body(c, carry):
        sel, kch, vch = chunk(c)
        n_pair = N_HEADS // 2
        s2 = jnp.dot(kch, q2[0], preferred_element_type=F32)
        for s in range(n_pair):
            s2_next = jnp.dot(kch, q2[s + 1], preferred_element_type=F32) if s + 1 < n_pair else None
            ps = [jnp.exp2(jnp.where(sel, s2[:, hh * qb:(hh + 1) * qb] - bound[2 * s + hh], -jnp.inf)).astype(BF16)
                  for hh in range(2)]
            acc_ref[s] += jnp.dot(vch, jnp.concatenate(ps, axis=1), preferred_element_type=F32)
            s2 = s2_next
        return carry

    lax.fori_loop(0, nck, fast_body, 0)

    dens = jnp.concatenate([acc_ref[s, LANES:LANES + 1, :] for s in range(N_HEADS // 2)], axis=0)

    @pl.when(jnp.logical_not(jnp.min(dens) >= 1e-20))
    def _():
        m_ref[...] = jnp.full(m_ref.shape, NEG_BIG, F32)
        acc_ref[...] = jnp.zeros(acc_ref.shape, F32)

        def exact_body(c, carry):
            sel, kch, vch = chunk(c)
            for s in range(N_HEADS // 2):
                s2 = jnp.dot(kch, q2[s], preferred_element_type=F32)
                ps, alphas = [], []
                for hh in range(2):
                    h = 2 * s + hh
                    sh = jnp.where(sel, s2[:, hh * qb:(hh + 1) * qb], -jnp.inf)
                    m_old = m_ref[h:h + 1, :]
                    m_new = jnp.maximum(m_old, jnp.max(sh, axis=0, keepdims=True))
                    m_ref[h:h + 1, :] = m_new
                    alphas.append(jnp.exp2(m_old - m_new))
                    ps.append(jnp.exp2(sh - m_new).astype(BF16))
                o2 = jnp.dot(vch, jnp.concatenate(ps, axis=1), preferred_element_type=F32)
                acc_ref[s] = jnp.concatenate(alphas, axis=1) * acc_ref[s] + o2
            return carry

        lax.fori_loop(0, nck, exact_body, 0)

    for s in range(N_HEADS // 2):
        g = (2 * s) // hpg
        num = acc_ref[s, g * HEAD_DIM:(g + 1) * HEAD_DIM, :]
        out2 = num / acc_ref[s, LANES:LANES + 1, :]
        o_ref[:, s * LANES:(s + 1) * LANES] = jnp.concatenate([out2[:, :qb], out2[:, qb:]], axis=0).T


def _dsa_prompt(qt, qit, wt, kb, vt, kib, nb, seq):
    k_sel = min(TOPK_MAX, seq // 4)
    kc = next(c for c in (512, Q_BLOCK) if seq % c == 0)
    nbits = max(1, (seq - 1).bit_length())
    nq = seq // Q_BLOCK
    r3 = lambda a: a.reshape(nb, seq, a.shape[-1])
    qcol = lambda r: pl.BlockSpec((None, r, Q_BLOCK), lambda b, j: (b * nq + j, 0, 0))
    seqblk = pl.BlockSpec((None, seq, LANES), lambda b, j: (b, 0, 0))
    out = pl.pallas_call(
        functools.partial(_dsa_kernel, k_sel=k_sel, kc=kc, nbits=nbits),
        grid=(nb, nq),
        in_specs=[qcol(N_HEADS * LANES), qcol(N_IDX_HEADS * LANES), qcol(SUBLANES), seqblk,
                  pl.BlockSpec((V_T_ROWS, seq), lambda b, j: (0, b)), seqblk],
        out_specs=pl.BlockSpec((None, Q_BLOCK, N_HEADS * HEAD_DIM), lambda b, j: (b, j, 0)),
        out_shape=jax.ShapeDtypeStruct((nb, seq, N_HEADS * HEAD_DIM), F32),
        scratch_shapes=[pltpu.VMEM((seq, Q_BLOCK), I32),
                        pltpu.VMEM((SUBLANES, Q_BLOCK), I32),
                        pltpu.VMEM((N_HEADS, Q_BLOCK), F32),
                        pltpu.VMEM((SUBLANES, LANES), F32),
                        pltpu.VMEM((N_HEADS // 2, V_T_ROWS, 2 * Q_BLOCK), F32)],
        compiler_params=_cparams(("arbitrary", "arbitrary")),
        name="dsa_prompt",
    )(qt, qit, wt, r3(kb), vt, r3(kib))
    return out.reshape(nb * seq, N_HEADS * HEAD_DIM)


def _conva_kernel(cb_ref, cc_ref, cx_ref, cch_ref, cxh_ref, w_ref, o_ref, st_ref, *, tiles_per_seq):
    tm = cb_ref.shape[0]
    first = (pl.program_id(0) % tiles_per_seq) == 0
    u = cc_ref[...] * cx_ref[...]
    uh = jnp.where(first, 0.0, cch_ref[...] * cxh_ref[...])
    ext = jnp.concatenate([uh, u], axis=0)
    conv = (w_ref[2:3, :] * u + w_ref[1:2, :] * ext[SUBLANES - 1:SUBLANES - 1 + tm]
            + w_ref[0:1, :] * ext[SUBLANES - 2:SUBLANES - 2 + tm])
    o_ref[...] = cb_ref[...] * conv
    st_ref[...] = u[tm - 2:tm, :]


def _conva_prompt(proj, conv_w, nb, seq):
    m = proj.shape[0]
    tm = min(seq, 512)
    tps = seq // tm
    col = lambda off: pl.BlockSpec((tm, D_CONV), lambda i: (i, off // D_CONV))
    halo = lambda off: pl.BlockSpec(
        (SUBLANES, D_CONV), lambda i: (jnp.maximum(i * (tm // SUBLANES) - 1, 0), off // D_CONV))
    return pl.pallas_call(
        functools.partial(_conva_kernel, tiles_per_seq=tps),
        grid=(m // tm,),
        in_specs=[col(CB_OFF), col(CC_OFF), col(CX_OFF), halo(CC_OFF), halo(CX_OFF),
                  pl.BlockSpec((3, D_CONV), lambda i: (0, 0))],
        out_specs=[pl.BlockSpec((tm, D_CONV), lambda i: (i, 0)),
                   pl.BlockSpec((None, 2, D_CONV), lambda i: (i // tps, 0, 0))],
        out_shape=[jax.ShapeDtypeStruct((m, D_CONV), F32),
                   jax.ShapeDtypeStruct((nb, 2, D_CONV), F32)],
        compiler_params=_cparams(("arbitrary",)),
        name="conva_prompt",
    )(proj, proj, proj, proj, proj, conv_w)


def _ssd_kernel(xs_ref, bc_ref, dt_ref, xh_ref, bh_ref, cwx_ref, cwb_ref, cbx_ref, cbb_ref,
                dtb_ref, alog_ref, dsk_ref, y_ref, hout_ref, h_ref, *, nchunk):
    cl = SSM_CHUNK
    c = pl.program_id(1)
    first = c == 0

    @pl.when(first)
    def _():
        h_ref[...] = jnp.zeros(h_ref.shape, F32)

    def conv(cur, halo, w_ref, b_ref):
        ext = jnp.concatenate([jnp.where(first, 0.0, halo), cur], axis=0)
        out = (w_ref[3:4, :] * cur + w_ref[2:3, :] * ext[SUBLANES - 1:SUBLANES - 1 + cl]
               + w_ref[1:2, :] * ext[SUBLANES - 2:SUBLANES - 2 + cl]
               + w_ref[0:1, :] * ext[SUBLANES - 3:SUBLANES - 3 + cl] + b_ref[...])
        return _silu(out)

    xs = conv(xs_ref[...], xh_ref[...], cwx_ref, cbx_ref)
    bc = conv(bc_ref[...], bh_ref[...], cwb_ref, cbb_ref)
    dt = _softplus(dt_ref[...] + dtb_ref[...])
    a = dt * (-jnp.exp(alog_ref[...]))
    ri = lax.broadcasted_iota(I32, (cl, cl), 0)
    ci = lax.broadcasted_iota(I32, (cl, cl), 1)
    causal = ri >= ci
    cs = jnp.dot(jnp.where(causal, 1.0, 0.0), a, preferred_element_type=F32,
                 precision=lax.Precision.HIGHEST)
    cs_t = cs.T
    lane = lax.broadcasted_iota(I32, (cl, LANES), 1)
    lo = lane < SSM_HEAD_DIM
    rows_lo = lax.broadcasted_iota(I32, (LANES, 1), 0) < SSM_HEAD_DIM
    heads_per_group = SSM_HEADS // SSM_GROUPS

    cb = []
    for g in range(SSM_GROUPS):
        bg = bc[:, g * SSM_STATE:(g + 1) * SSM_STATE].astype(BF16)
        cg = bc[:, (SSM_GROUPS + g) * SSM_STATE:(SSM_GROUPS + g + 1) * SSM_STATE].astype(BF16)
        cb.append((bg, cg, lax.dot_general(cg, bg, NT_DIMS, preferred_element_type=F32)))

    for s in range(SSM_HEADS // 2):
        h0, h1 = 2 * s, 2 * s + 1
        bg, cg, cbg = cb[h0 // heads_per_group]
        sl = slice(s * LANES, (s + 1) * LANES)
        xs_s = xs[:, sl]
        col0, col1 = cs[:, h0:h0 + 1], cs[:, h1:h1 + 1]
        last0, last1 = cs[cl - 1:cl, h0:h0 + 1], cs[cl - 1:cl, h1:h1 + 1]
        xdt = xs_s * jnp.where(lo, dt[:, h0:h0 + 1], dt[:, h1:h1 + 1])
        xdt_b = xdt.astype(BF16)
        m0 = (cbg * jnp.where(causal, jnp.exp(col0 - cs_t[h0:h0 + 1, :]), 0.0)).astype(BF16)
        m1 = (cbg * jnp.where(causal, jnp.exp(col1 - cs_t[h1:h1 + 1, :]), 0.0)).astype(BF16)
        y_diag = jnp.where(lo, jnp.dot(m0, xdt_b, preferred_element_type=F32),
                           jnp.dot(m1, xdt_b, preferred_element_type=F32))
        hs = h_ref[sl, :]
        y_off = lax.dot_general(cg, hs.astype(BF16), NT_DIMS, preferred_element_type=F32)
        y_off = y_off * jnp.where(lo, jnp.exp(col0), jnp.exp(col1))
        y_ref[:, sl] = y_diag + y_off + dsk_ref[:, sl] * xs_s
        xw = xdt * jnp.where(lo, jnp.exp(last0 - col0), jnp.exp(last1 - col1))
        st = jnp.dot(xw.T.astype(BF16), bg, preferred_element_type=F32)
        h_ref[sl, :] = hs * jnp.where(rows_lo, jnp.exp(last0), jnp.exp(last1)) + st

    @pl.when(c == nchunk - 1)
    def _():
        hout_ref[...] = h_ref[...]


def _ssd_prompt(proj, cw, cbias, dtb, alog, dsk, nb, seq):
    m = proj.shape[0]
    cl = SSM_CHUNK
    nchunk = seq // cl
    blk = lambda w, off: pl.BlockSpec((cl, w), lambda b, c: (b * nchunk + c, off // w))
    halo = lambda off: pl.BlockSpec(
        (SUBLANES, 512), lambda b, c: (jnp.maximum((b * nchunk + c) * (cl // SUBLANES) - 1, 0), off // 512))
    const = lambda r, w, j: pl.BlockSpec((r, w), lambda b, c: (0, j))
    y, hout = pl.pallas_call(
        functools.partial(_ssd_kernel, nchunk=nchunk),
        grid=(nb, nchunk),
        in_specs=[blk(512, XS_OFF), blk(512, BC_OFF), blk(LANES, DT_OFF), halo(XS_OFF), halo(BC_OFF),
                  const(4, 512, 0), const(4, 512, 1), const(1, 512, 0), const(1, 512, 1),
                  const(1, LANES, 0), const(1, LANES, 0), const(1, 512, 0)],
        out_specs=[pl.BlockSpec((cl, SSM_INNER), lambda b, c: (b * nchunk + c, 0)),
                   pl.BlockSpec((None, SSM_INNER, SSM_STATE), lambda b, c: (b, 0, 0))],
        out_shape=[jax.ShapeDtypeStruct((m, SSM_INNER), F32),
                   jax.ShapeDtypeStruct((nb, SSM_INNER, SSM_STATE), F32)],
        scratch_shapes=[pltpu.VMEM((SSM_INNER, SSM_STATE), F32)],
        compiler_params=_cparams(("parallel", "arbitrary")),
        name="ssd_prompt",
    )(proj, proj, proj, proj, proj, cw, cw, cbias, cbias, dtb, alog, dsk)
    return y, hout


def _merge_kernel(x_ref, pa_ref, at_ref, ys_ref, z_ref, g0_ref, g1_ref, g2_ref, sn_ref, bg_ref,
                  wpc_ref, wpa_ref, wps_ref, wo_ref, o_ref):
    ssd = _rms(ys_ref[...] * _silu(z_ref[...]), sn_ref[...])
    ya = jnp.dot(pa_ref[...].astype(BF16), wpc_ref[...], preferred_element_type=F32)
    yb = jnp.dot(at_ref[...].astype(BF16), wpa_ref[...], preferred_element_type=F32)
    yc = jnp.dot(ssd.astype(BF16), wps_ref[...], preferred_element_type=F32)
    merged = (_sigmoid(g0_ref[...] + bg_ref[0:1, :]) * ya + _sigmoid(g1_ref[...] + bg_ref[1:2, :]) * yb
              + _sigmoid(g2_ref[...] + bg_ref[2:3, :]) * yc)
    o_ref[...] = x_ref[...] + jnp.dot(merged.astype(BF16), wo_ref[...], preferred_element_type=F32)


def _merge(x, pre_a, attn, y_ssd, proj, ssm_norm, b_gate, w_pc, w_pa, w_ps, w_o):
    m = x.shape[0]
    tm = min(m, 256)
    row = lambda w: pl.BlockSpec((tm, w), lambda i: (i, 0))
    pcol = lambda w, off: pl.BlockSpec((tm, w), lambda i: (i, off // w))
    const = lambda r, w: pl.BlockSpec((r, w), lambda i: (0, 0))
    return pl.pallas_call(
        _merge_kernel,
        grid=(m // tm,),
        in_specs=[row(D_MODEL), row(512), row(512), row(512), pcol(512, Z_OFF),
                  pcol(D_MODEL, G_OFF), pcol(D_MODEL, G_OFF + D_MODEL), pcol(D_MODEL, G_OFF + 2 * D_MODEL),
                  const(1, 512), const(3, D_MODEL),
                  const(512, D_MODEL), const(512, D_MODEL), const(512, D_MODEL), const(D_MODEL, D_MODEL)],
        out_specs=row(D_MODEL),
        out_shape=jax.ShapeDtypeStruct((m, D_MODEL), F32),
        compiler_params=_cparams(("parallel",)),
        name="merge",
    )(x, pre_a, attn, y_ssd, proj, proj, proj, proj, ssm_norm, b_gate, w_pc, w_pa, w_ps, w_o)


def _ffn_kernel(x_ref, g_ref, wr_ref, wg_ref, wu_ref, wd_ref, o_ref, h_ref, acc_ref, comb_ref,
                *, routed, n_e, n_f):
    e = pl.program_id(1)
    j = pl.program_id(2)
    tm = x_ref.shape[0]

    @pl.when(jnp.logical_and(e == 0, j == 0))
    def _():
        hf = _rms(x_ref[...], g_ref[...])
        h_ref[...] = hf.astype(BF16)
        acc_ref[...] = jnp.zeros(acc_ref.shape, F32)
        if routed:
            lane = lax.broadcasted_iota(I32, (tm, LANES), 1).astype(F32)
            logits = jnp.dot(hf, wr_ref[...], preferred_element_type=F32, precision=lax.Precision.HIGHEST)
            logits = jnp.where(lane < n_e, logits, -jnp.inf)
            m1 = jnp.max(logits, axis=1, keepdims=True)
            i1 = jnp.min(jnp.where(logits == m1, lane, float(LANES)), axis=1, keepdims=True)
            rest = jnp.where(lane == i1, -jnp.inf, logits)
            m2 = jnp.max(rest, axis=1, keepdims=True)
            i2 = jnp.min(jnp.where(rest == m2, lane, float(LANES)), axis=1, keepdims=True)
            e2 = jnp.exp(m2 - m1)
            den = 1.0 + e2
            comb_ref[...] = jnp.where(lane == i1, 1.0 / den, 0.0) + jnp.where(lane == i2, e2 / den, 0.0)

    h = h_ref[...]
    gate = jnp.dot(h, wg_ref[...], preferred_element_type=F32)
    up = jnp.dot(h, wu_ref[...], preferred_element_type=F32)
    act = _silu(gate) * up
    if routed:
        lane = lax.broadcasted_iota(I32, (tm, LANES), 1)
        act = act * jnp.sum(jnp.where(lane == e, comb_ref[...], 0.0), axis=1, keepdims=True)
    acc_ref[...] += jnp.dot(act.astype(BF16), wd_ref[...], preferred_element_type=F32)

    @pl.when(jnp.logical_and(e == n_e - 1, j == n_f - 1))
    def _():
        o_ref[...] = x_ref[...] + acc_ref[...]


def _ffn(x, g, w_router, w_gate, w_up, w_down, routed):
    m = x.shape[0]
    n_e, _, d_ff = w_gate.shape
    tm = min(m, 1024)
    tf = 256
    n_f = d_ff // tf
    return pl.pallas_call(
        functools.partial(_ffn_kernel, routed=routed, n_e=n_e, n_f=n_f),
        grid=(m // tm, n_e, n_f),
        in_specs=[pl.BlockSpec((tm, D_MODEL), lambda i, e, j: (i, 0)),
                  pl.BlockSpec((1, D_MODEL), lambda i, e, j: (0, 0)),
                  pl.BlockSpec((D_MODEL, LANES), lambda i, e, j: (0, 0)),
                  pl.BlockSpec((None, D_MODEL, tf), lambda i, e, j: (e, 0, j)),
                  pl.BlockSpec((None, D_MODEL, tf), lambda i, e, j: (e, 0, j)),
                  pl.BlockSpec((None, tf, D_MODEL), lambda i, e, j: (e, j, 0))],
        out_specs=pl.BlockSpec((tm, D_MODEL), lambda i, e, j: (i, 0)),
        out_shape=jax.ShapeDtypeStruct((m, D_MODEL), F32),
        scratch_shapes=[pltpu.VMEM((tm, D_MODEL), BF16), pltpu.VMEM((tm, D_MODEL), F32),
                        pltpu.VMEM((tm, LANES), F32)],
        compiler_params=_cparams(("parallel", "arbitrary", "arbitrary")),
        name="moe" if routed else "ffn",
    )(x, g, w_router, w_gate, w_up, w_down)


MOE_BLOCK = 2048
MOE_TILE = 256
TOP_K = 2


def _moe_tiles(tb):
    return TOP_K * tb // MOE_TILE + N_EXPERTS


def _route_kernel(x_ref, g_ref, wr_ref, hb_ref, row_ref, col_ref, meta_ref):
    tb = x_ref.shape[0]
    hf = _rms(x_ref[...], g_ref[...])
    hb_ref[...] = hf.astype(BF16)
    lane_i = lax.broadcasted_iota(I32, (tb, LANES), 1)
    lane = lane_i.astype(F32)
    logits = jnp.dot(hf, wr_ref[...], preferred_element_type=F32, precision=lax.Precision.HIGHEST)
    logits = jnp.where(lane_i < N_EXPERTS, logits, -jnp.inf)
    m1 = jnp.max(logits, axis=1, keepdims=True)
    i1 = jnp.min(jnp.where(logits == m1, lane, float(LANES)), axis=1, keepdims=True)
    rest = jnp.where(lane == i1, -jnp.inf, logits)
    m2 = jnp.max(rest, axis=1, keepdims=True)
    i2 = jnp.min(jnp.where(rest == m2, lane, float(LANES)), axis=1, keepdims=True)
    e2 = jnp.exp(m2 - m1)
    p1 = 1.0 / (1.0 + e2)
    p2 = e2 / (1.0 + e2)
    oh1 = lane == i1
    oh2 = lane == i2
    both = jnp.where(jnp.logical_or(oh1, oh2), 1.0, 0.0).astype(BF16)

    sb = MOE_TILE
    ri = lax.broadcasted_iota(I32, (sb, sb), 0)
    ci = lax.broadcasted_iota(I32, (sb, sb), 1)
    strict = jnp.where(ri > ci, 1.0, 0.0).astype(BF16)
    carry = jnp.zeros((1, LANES), F32)
    ranks = []
    for s in range(tb // sb):
        rows = both[s * sb:(s + 1) * sb]
        ranks.append(jnp.dot(strict, rows, preferred_element_type=F32) + carry)
        carry = carry + jnp.sum(rows.astype(F32), axis=0, keepdims=True)
    rank = jnp.concatenate(ranks, axis=0)
    seg = jnp.floor((carry + (MOE_TILE - 1)) * (1.0 / MOE_TILE)) * MOE_TILE
    ui = lax.broadcasted_iota(I32, (LANES, LANES), 0)
    uj = lax.broadcasted_iota(I32, (LANES, LANES), 1)
    before = jnp.where(ui < uj, 1.0, 0.0)
    off = jnp.dot(jnp.broadcast_to(seg, (SUBLANES, LANES)), before, preferred_element_type=F32,
                  precision=lax.Precision.HIGHEST)[0:1, :]
    dest = off + rank
    d1 = jnp.sum(jnp.where(oh1, dest, 0.0), axis=1, keepdims=True)
    d2 = jnp.sum(jnp.where(oh2, dest, 0.0), axis=1, keepdims=True)
    col = jnp.where(lane_i == 0, d1, jnp.where(lane_i == 1, d2, jnp.where(lane_i == 2, p1,
                                                                          jnp.where(lane_i == 3, p2, 0.0))))
    col_ref[...] = col
    row_ref[...] = col.T[0:SUBLANES, :]

    end_rows = jnp.broadcast_to(off + seg, (LANES, LANES)).T
    start = (lax.broadcasted_iota(I32, (LANES, LANES), 1) * MOE_TILE).astype(F32)
    is_expert = lax.broadcasted_iota(I32, (LANES, LANES), 0) < N_EXPERTS
    done = jnp.where(jnp.logical_and(is_expert, end_rows <= start), 1.0, 0.0)
    expert = jnp.minimum(jnp.sum(done, axis=0, keepdims=True), float(N_EXPERTS - 1))
    total = jnp.sum(jnp.where(lane_i[0:1] < N_EXPERTS, seg, 0.0), axis=1, keepdims=True)
    valid = jnp.where(start[0:1] < total, 1.0, 0.0)
    sub = lax.broadcasted_iota(I32, (SUBLANES, LANES), 0)
    meta_ref[...] = jnp.where(sub == 0, expert, jnp.where(sub == 1, valid, 0.0)).astype(I32)


def _route(x, g, w_router, tb):
    m = x.shape[0]
    nblk = m // tb
    return pl.pallas_call(
        _route_kernel,
        grid=(nblk,),
        in_specs=[pl.BlockSpec((tb, D_MODEL), lambda b: (b, 0)),
                  pl.BlockSpec((1, D_MODEL), lambda b: (0, 0)),
                  pl.BlockSpec((D_MODEL, LANES), lambda b: (0, 0))],
        out_specs=[pl.BlockSpec((tb, D_MODEL), lambda b: (b, 0)),
                   pl.BlockSpec((SUBLANES, tb), lambda b: (0, b)),
                   pl.BlockSpec((tb, LANES), lambda b: (b, 0)),
                   pl.BlockSpec((None, SUBLANES, LANES), lambda b: (b, 0, 0))],
        out_shape=[jax.ShapeDtypeStruct((m, D_MODEL), BF16),
                   jax.ShapeDtypeStruct((SUBLANES, m), F32),
                   jax.ShapeDtypeStruct((m, LANES), F32),
                   jax.ShapeDtypeStruct((nblk, SUBLANES, LANES), I32)],
        compiler_params=_cparams(("parallel",)),
        name="moe_route",
    )(x, g, w_router)


def _gffn_kernel(te_ref, tv_ref, hb_ref, row_ref, wg_ref, wu_ref, wd_ref, ys_ref, xg_ref, acc_ref, *, nt, n_f):
    del te_ref
    b, t, j = pl.program_id(0), pl.program_id(1), pl.program_id(2)
    valid = tv_ref[b * nt + t] > 0
    tb = hb_ref.shape[0]

    @pl.when(jnp.logical_and(valid, j == 0))
    def _():
        slot = (t * MOE_TILE + lax.broadcasted_iota(I32, (MOE_TILE, tb), 0)).astype(F32)
        hit = jnp.logical_or(row_ref[0:1, :] == slot, row_ref[1:2, :] == slot)
        onehot = jnp.where(hit, 1.0, 0.0).astype(BF16)
        xg_ref[...] = jnp.dot(onehot, hb_ref[...], preferred_element_type=F32).astype(BF16)
        acc_ref[...] = jnp.zeros(acc_ref.shape, F32)

    @pl.when(valid)
    def _():
        xg = xg_ref[...]
        gate = jnp.dot(xg, wg_ref[...], preferred_element_type=F32)
        up = jnp.dot(xg, wu_ref[...], preferred_element_type=F32)
        acc_ref[...] += jnp.dot((_silu(gate) * up).astype(BF16), wd_ref[...], preferred_element_type=F32)

    @pl.when(j == n_f - 1)
    def _():
        ys_ref[...] = jnp.where(valid, acc_ref[...], 0.0).astype(BF16)


def _gffn(tile_expert, tile_valid, hb, rowinfo, w_gate, w_up, w_down, tb):
    m = hb.shape[0]
    nblk = m // tb
    nt = _moe_tiles(tb)
    d_ff = w_gate.shape[2]
    tf = d_ff // 2
    n_f = d_ff // tf
    fidx = lambda b, t, j, te, tv: jnp.where(tv[b * nt + t] > 0, j, n_f - 1)
    grid_spec = pltpu.PrefetchScalarGridSpec(
        num_scalar_prefetch=2,
        grid=(nblk, nt, n_f),
        in_specs=[pl.BlockSpec((tb, D_MODEL), lambda b, t, j, te, tv: (b, 0)),
                  pl.BlockSpec((SUBLANES, tb), lambda b, t, j, te, tv: (0, b)),
                  pl.BlockSpec((None, D_MODEL, tf), lambda b, t, j, te, tv: (te[b * nt + t], 0, fidx(b, t, j, te, tv))),
                  pl.BlockSpec((None, D_MODEL, tf), lambda b, t, j, te, tv: (te[b * nt + t], 0, fidx(b, t, j, te, tv))),
                  pl.BlockSpec((None, tf, D_MODEL), lambda b, t, j, te, tv: (te[b * nt + t], fidx(b, t, j, te, tv), 0))],
        out_specs=pl.BlockSpec((MOE_TILE, D_MODEL), lambda b, t, j, te, tv: (b * nt + t, 0)),
        scratch_shapes=[pltpu.VMEM((MOE_TILE, D_MODEL), BF16), pltpu.VMEM((MOE_TILE, D_MODEL), F32)],
    )
    return pl.pallas_call(
        functools.partial(_gffn_kernel, nt=nt, n_f=n_f),
        grid_spec=grid_spec,
        out_shape=jax.ShapeDtypeStruct((nblk * nt * MOE_TILE, D_MODEL), BF16),
        compiler_params=_cparams(("parallel", "arbitrary", "arbitrary")),
        name="moe_experts",
    )(tile_expert, tile_valid, hb, rowinfo, w_gate, w_up, w_down)


def _combine_kernel(tv_ref, x_ref, col_ref, ys_ref, o_ref, *, nt):
    b, t = pl.program_id(0), pl.program_id(1)

    @pl.when(t == 0)
    def _():
        o_ref[...] = x_ref[...]

    @pl.when(tv_ref[b * nt + t] > 0)
    def _():
        tb = x_ref.shape[0]
        slot = (t * MOE_TILE + lax.broadcasted_iota(I32, (tb, MOE_TILE), 1)).astype(F32)
        weight = (jnp.where(col_ref[:, 0:1] == slot, col_ref[:, 2:3], 0.0)
                  + jnp.where(col_ref[:, 1:2] == slot, col_ref[:, 3:4], 0.0))
        o_ref[...] += jnp.dot(weight.astype(BF16), ys_ref[...], preferred_element_type=F32)


def _combine(tile_valid, x, colinfo, ys, tb):
    m = x.shape[0]
    nblk = m // tb
    nt = _moe_tiles(tb)
    grid_spec = pltpu.PrefetchScalarGridSpec(
        num_scalar_prefetch=1,
        grid=(nblk, nt),
        in_specs=[pl.BlockSpec((tb, D_MODEL), lambda b, t, tv: (b, 0)),
                  pl.BlockSpec((tb, LANES), lambda b, t, tv: (b, 0)),
                  pl.BlockSpec((MOE_TILE, D_MODEL), lambda b, t, tv: (b * nt + t, 0))],
        out_specs=pl.BlockSpec((tb, D_MODEL), lambda b, t, tv: (b, 0)),
    )
    return pl.pallas_call(
        functools.partial(_combine_kernel, nt=nt),
        grid_spec=grid_spec,
        out_shape=jax.ShapeDtypeStruct((m, D_MODEL), F32),
        compiler_params=_cparams(("parallel", "arbitrary")),
        name="moe_combine",
    )(tile_valid, x, colinfo, ys)


def _moe_grouped(x, g, w_router, w_gate, w_up, w_down):
    m = x.shape[0]
    tb = min(MOE_BLOCK, m)
    nt = _moe_tiles(tb)
    hb, rowinfo, colinfo, meta = _route(x, g, w_router, tb)
    tile_expert = meta[:, 0, :nt].reshape(-1)
    tile_valid = meta[:, 1, :nt].reshape(-1)
    ys = _gffn(tile_expert, tile_valid, hb, rowinfo, w_gate, w_up, w_down, tb)
    return _combine(tile_valid, x, colinfo, ys, tb)


def _decmix_kernel(cb_ref, cc_ref, cx_ref, xs_ref, bc_ref, dt_ref, sc0_ref, sc1_ref,
                   sx0_ref, sx1_ref, sx2_ref, sb0_ref, sb1_ref, sb2_ref,
                   cw_ref, cwx_ref, cwb_ref, cbx_ref, cbb_ref, dtb_ref, alog_ref,
                   pa_ref, u_ref, xso_ref, bco_ref, dto_ref, ea_ref):
    u = cc_ref[...] * cx_ref[...]
    u_ref[...] = u
    pa_ref[...] = cb_ref[...] * (cw_ref[0:1, :] * sc0_ref[...] + cw_ref[1:2, :] * sc1_ref[...] + cw_ref[2:3, :] * u)
    xso_ref[...] = _silu(cwx_ref[0:1, :] * sx0_ref[...] + cwx_ref[1:2, :] * sx1_ref[...]
                         + cwx_ref[2:3, :] * sx2_ref[...] + cwx_ref[3:4, :] * xs_ref[...] + cbx_ref[...])
    bco_ref[...] = _silu(cwb_ref[0:1, :] * sb0_ref[...] + cwb_ref[1:2, :] * sb1_ref[...]
                         + cwb_ref[2:3, :] * sb2_ref[...] + cwb_ref[3:4, :] * bc_ref[...] + cbb_ref[...])
    dt = _softplus(dt_ref[...] + dtb_ref[...])
    dto_ref[...] = dt
    ea_ref[...] = jnp.exp(dt * (-jnp.exp(alog_ref[...])))


def _decmix(proj, st_conv, st_ssm_conv, conv_w, cw, cbias, dtb, alog):
    nb = proj.shape[0]
    pcol = lambda w, off: pl.BlockSpec((nb, w), lambda i: (0, off // w))
    full = lambda w: pl.BlockSpec((nb, w), lambda i: (0, 0))
    const = lambda r, w, j: pl.BlockSpec((r, w), lambda i: (0, j))
    sx = [st_ssm_conv[:, t, :512] for t in range(3)]
    sb = [st_ssm_conv[:, t, 512:] for t in range(3)]
    return pl.pallas_call(
        _decmix_kernel,
        grid=(1,),
        in_specs=[pcol(512, CB_OFF), pcol(512, CC_OFF), pcol(512, CX_OFF), pcol(512, XS_OFF), pcol(512, BC_OFF),
                  pcol(LANES, DT_OFF)] + [full(512)] * 8
                 + [const(3, 512, 0), const(4, 512, 0), const(4, 512, 1), const(1, 512, 0), const(1, 512, 1),
                    const(1, LANES, 0), const(1, LANES, 0)],
        out_specs=[full(512), full(512), full(512), full(512), full(LANES), full(LANES)],
        out_shape=[jax.ShapeDtypeStruct((nb, 512), F32)] * 4 + [jax.ShapeDtypeStruct((nb, LANES), F32)] * 2,
        compiler_params=_cparams(("arbitrary",)),
        name="decode_mix",
    )(proj, proj, proj, proj, proj, proj, st_conv[:, 0], st_conv[:, 1], *sx, *sb,
      conv_w, cw, cw, cbias, cbias, dtb, alog)


def _decssd_kernel(h_ref, dt_ref, xs_ref, b_ref, c_ref, ea_ref, dsk_ref, y_ref, ho_ref):
    xs = xs_ref[...]
    h_new = h_ref[...] * ea_ref[...] + (xs * dt_ref[...]) * b_ref[...]
    ho_ref[...] = h_new
    y_ref[...] = jnp.sum(h_new * c_ref[...], axis=-1, keepdims=True) + dsk_ref[...] * xs


def _decssd(h0, dt, xs, bh, ch, ea, dsk):
    nb = h0.shape[0]
    blk = lambda a, b: pl.BlockSpec((None, SSM_HEADS, a, b), lambda i: (i, 0, 0, 0))
    return pl.pallas_call(
        _decssd_kernel,
        grid=(nb,),
        in_specs=[blk(SSM_HEAD_DIM, SSM_STATE), blk(1, 1), blk(SSM_HEAD_DIM, 1), blk(1, SSM_STATE),
                  blk(1, SSM_STATE), blk(1, 1), pl.BlockSpec((SSM_HEADS, 1, 1), lambda i: (0, 0, 0))],
        out_specs=[blk(SSM_HEAD_DIM, 1), blk(SSM_HEAD_DIM, SSM_STATE)],
        out_shape=[jax.ShapeDtypeStruct((nb, SSM_HEADS, SSM_HEAD_DIM, 1), F32),
                   jax.ShapeDtypeStruct((nb, SSM_HEADS, SSM_HEAD_DIM, SSM_STATE), F32)],
        compiler_params=_cparams(("parallel",)),
        name="decode_ssd",
    )(h0, dt, xs, bh, ch, ea, dsk)


DECODE_PAGE_GROUP = 16


def _decscore_kernel(pt_ref, qi_ref, w_ref, *refs):
    del pt_ref
    kid_refs, o_ref = refs[:-1], refs[-1]
    kid = jnp.concatenate([r[...].astype(BF16) for r in kid_refs], axis=0)
    s = lax.dot_general(qi_ref[:, 0:IDX_DIM].astype(BF16), kid, NT_DIMS, preferred_element_type=F32)
    sc = jnp.maximum(s, 0.0) * w_ref[:, 0:1]
    o_ref[...] = jnp.sum(sc, axis=0, keepdims=True)


def _page_specs(block, layer, pg):
    zeros = (0,) * (len(block) - 2)
    return [pl.BlockSpec(block, lambda b, p, pt, t=t: (layer, pt[b, p * pg + t]) + zeros) for t in range(pg)]


def _decscore(page_table, qip, wrows, pool_kidx, layer):
    nb, n_pages = page_table.shape
    pg = math.gcd(DECODE_PAGE_GROUP, n_pages)
    grid_spec = pltpu.PrefetchScalarGridSpec(
        num_scalar_prefetch=1,
        grid=(nb, n_pages // pg),
        in_specs=[pl.BlockSpec((None, N_IDX_HEADS, LANES), lambda b, p, pt: (b, 0, 0)),
                  pl.BlockSpec((None, N_IDX_HEADS, PAGE_SIZE), lambda b, p, pt: (b, 0, 0))]
                 + _page_specs((None, None, PAGE_SIZE, IDX_DIM), layer, pg),
        out_specs=pl.BlockSpec((None, None, 1, pg * PAGE_SIZE), lambda b, p, pt: (b, p, 0, 0)),
    )
    out = pl.pallas_call(
        _decscore_kernel,
        grid_spec=grid_spec,
        out_shape=jax.ShapeDtypeStruct((nb, n_pages // pg, 1, pg * PAGE_SIZE), F32),
        compiler_params=_cparams(("parallel", "arbitrary")),
        name="decode_scores",
    )(page_table, qip, wrows, *([pool_kidx] * pg))
    return out.reshape(nb, n_pages * PAGE_SIZE)


def _decselect_kernel(sc_ref, qi_ref, ki_ref, w_ref, sel_ref, self_ref, keys_ref, *, k_sel, kc, nbits):
    nb, past = sc_ref.shape
    nck = past // kc
    lane_k = lax.broadcasted_iota(I32, (nb, kc), 1)
    ki = ki_ref[...].astype(F32)
    own = jnp.zeros((nb, 1), F32)
    for h in range(N_IDX_HEADS):
        s = jnp.sum(qi_ref[:, h * LANES:(h + 1) * LANES].astype(F32) * ki, axis=1, keepdims=True)
        own = own + jnp.maximum(s, 0.0) * w_ref[:, h:h + 1]
    own_key = _sortable(own)

    def key_body(c, carry):
        off = pl.multiple_of(c * kc, kc)
        keys_ref[:, pl.ds(off, kc)] = _sortable(sc_ref[:, pl.ds(off, kc)])
        return carry

    lax.fori_loop(0, nck, key_body, 0)

    def count_where(pred):
        def body(c, acc):
            off = pl.multiple_of(c * kc, kc)
            f = jnp.where(pred(keys_ref[:, pl.ds(off, kc)], off), 1.0, 0.0)
            part = f[:, 0:LANES]
            for t in range(1, kc // LANES):
                part = part + f[:, t * LANES:(t + 1) * LANES]
            return acc + part

        acc = lax.fori_loop(0, nck, body, jnp.zeros((nb, LANES), F32))
        return jnp.sum(acc, axis=1, keepdims=True)

    def bit_body(i, t):
        cand = t + lax.shift_left(jnp.int32(1), 31 - i)
        cnt = count_where(lambda kk, off: kk >= cand) + jnp.where(own_key >= cand, 1.0, 0.0)
        return jnp.where(cnt >= k_sel, cand, t)

    thr = lax.fori_loop(0, 32, bit_body, jnp.full((nb, 1), INT_MIN, I32))
    thr = jnp.maximum(thr, KEY_NEG_INF)
    finite_thr = thr > KEY_NEG_INF
    cnt_gt = count_where(lambda kk, off: kk > thr) + jnp.where(own_key > thr, 1.0, 0.0)
    need = k_sel - cnt_gt

    def xbody(i, x):
        cand = x + lax.shift_left(jnp.int32(1), nbits - 1 - i)
        cnt = count_where(lambda kk, off: jnp.logical_and(kk == thr, off + lane_k < cand))
        return jnp.where(cnt < need, cand, x)

    xcut = lax.fori_loop(0, nbits, xbody, jnp.zeros((nb, 1), I32))
    xcut = jnp.where(finite_thr, xcut, -1)
    ties_past = count_where(lambda kk, off: kk == thr)
    own_tie = jnp.logical_and(jnp.logical_and(own_key == thr, finite_thr), ties_past < need)
    self_ref[...] = jnp.broadcast_to(
        jnp.where(jnp.logical_or(own_key > thr, own_tie), 1.0, 0.0), (nb, LANES))

    def sel_body(c, carry):
        off = pl.multiple_of(c * kc, kc)
        kk = keys_ref[:, pl.ds(off, kc)]
        sel = jnp.logical_or(kk > thr, jnp.logical_and(kk == thr, off + lane_k <= xcut))
        sel_ref[:, pl.ds(off, kc)] = jnp.where(sel, 1.0, 0.0)
        return carry

    lax.fori_loop(0, nck, sel_body, 0)


def _decselect(scores, qip, kib, wi):
    nb, past = scores.shape
    k_sel = min(TOPK_MAX, (past + 1) // 4)
    kc = 512 if past % 512 == 0 else PAGE_SIZE
    nbits = max(1, past.bit_length())
    full = lambda w: pl.BlockSpec((nb, w), lambda i: (0, 0))
    return pl.pallas_call(
        functools.partial(_decselect_kernel, k_sel=k_sel, kc=kc, nbits=nbits),
        grid=(1,),
        in_specs=[full(past), full(N_IDX_HEADS * LANES), full(LANES), full(LANES)],
        out_specs=[full(past), full(LANES)],
        out_shape=[jax.ShapeDtypeStruct((nb, past), F32), jax.ShapeDtypeStruct((nb, LANES), F32)],
        scratch_shapes=[pltpu.VMEM((nb, past), I32)],
        compiler_params=_cparams(("arbitrary",)),
        name="decode_select",
    )(scores, qip, kib, wi)


def _decattn_kernel(pt_ref, qp_ref, sel_ref, kn_ref, vn_ref, self_ref, *refs, n_steps):
    del pt_ref
    pg = (len(refs) - 4) // 2
    kp_refs, vp_refs = refs[:pg], refs[pg:2 * pg]
    o_ref, m_ref, l_ref, acc_ref = refs[2 * pg:]
    p = pl.program_id(1)

    @pl.when(p == 0)
    def _():
        m_ref[...] = jnp.full(m_ref.shape, NEG_BIG, F32)
        l_ref[...] = jnp.zeros(l_ref.shape, F32)
        acc_ref[...] = jnp.zeros(acc_ref.shape, F32)

    q = qp_ref[...]
    kcat = jnp.concatenate([r[...].astype(BF16) for r in kp_refs], axis=0)
    vcat = jnp.concatenate([r[...].astype(BF16) for r in vp_refs], axis=0)
    s = lax.dot_general(q.astype(BF16), kcat, NT_DIMS, preferred_element_type=F32)
    s = jnp.where(sel_ref[...] > 0.0, s, -jnp.inf)
    m_old = m_ref[:, 0:1]
    m_new = jnp.maximum(m_old, jnp.max(s, axis=1, keepdims=True))
    alpha = jnp.exp(m_old - m_new)
    pr = jnp.exp(s - m_new)
    l_new = alpha * l_ref[:, 0:1] + jnp.sum(pr, axis=1, keepdims=True)
    acc_ref[...] = alpha * acc_ref[...] + jnp.dot(pr.astype(BF16), vcat, preferred_element_type=F32)
    m_ref[...] = jnp.broadcast_to(m_new, m_ref.shape)
    l_ref[...] = jnp.broadcast_to(l_new, l_ref.shape)

    @pl.when(p == n_steps - 1)
    def _():
        s_own = jnp.sum(q * kn_ref[...], axis=1, keepdims=True)
        s_own = jnp.where(self_ref[:, 0:1] > 0.0, s_own, -jnp.inf)
        m_o = m_ref[:, 0:1]
        m_n = jnp.maximum(m_o, s_own)
        al = jnp.exp(m_o - m_n)
        p_own = jnp.exp(s_own - m_n)
        l_n = al * l_ref[:, 0:1] + p_own
        acc = al * acc_ref[...] + p_own.astype(BF16).astype(F32) * vn_ref[...]
        o_ref[...] = acc / l_n


def _decattn(page_table, qp, sel, pool_k, pool_v, kb, vb, self_sel, layer):
    nb, n_pages = page_table.shape
    pg = math.gcd(DECODE_PAGE_GROUP, n_pages)
    n_steps = n_pages // pg
    row = lambda r: pl.BlockSpec((None, r, LANES), lambda b, p, pt: (b, 0, 0))
    pages = _page_specs((None, None, PAGE_SIZE, LANES), layer, pg)
    grid_spec = pltpu.PrefetchScalarGridSpec(
        num_scalar_prefetch=1,
        grid=(nb, n_steps),
        in_specs=[row(N_HEADS),
                  pl.BlockSpec((None, None, 1, pg * PAGE_SIZE), lambda b, p, pt: (b, p, 0, 0)),
                  row(1), row(1), row(1)] + pages + pages,
        out_specs=row(N_HEADS),
        scratch_shapes=[pltpu.VMEM((N_HEADS, LANES), F32)] * 3,
    )
    return pl.pallas_call(
        functools.partial(_decattn_kernel, n_steps=n_steps),
        grid_spec=grid_spec,
        out_shape=jax.ShapeDtypeStruct((nb, N_HEADS, LANES), F32),
        compiler_params=_cparams(("parallel", "arbitrary")),
        name="decode_attend",
    )(page_table, qp, sel.reshape(nb, n_steps, 1, pg * PAGE_SIZE), kb, vb, self_sel,
      *([pool_k] * pg), *([pool_v] * pg))


def _pack_w_in(w):
    d = w.shape[0]
    pad = lambda n: jnp.zeros((d, n), w.dtype)
    cols = [w[:, 4172:7244],
            w[:, 0:2048],
            w[:, 2628:3140],
            w[:, 3140:4164],
            w[:, 2048:2304],
            w[:, 2304:2560],
            w[:, 2560:2628], pad(60),
            w[:, 4164:4172], pad(120),
            pad(N_PROJ - 7424)]
    return jnp.concatenate(cols, axis=1).astype(BF16)


def _pad_lanes(v):
    return jnp.zeros((1, LANES), F32).at[0, :v.shape[0]].set(v)


def _rope_tables(pos):
    half = HEAD_DIM // 2
    inv = ROPE_THETA ** (-jnp.arange(half, dtype=F32) / half)
    ang = pos.astype(F32)[:, None] * inv[None, :]
    cos, sin = jnp.cos(ang), jnp.sin(ang)
    cos2 = jnp.concatenate([cos, cos], axis=1)
    sin2 = jnp.concatenate([-sin, sin], axis=1)
    return jnp.tile(cos2, (1, 2)), jnp.tile(sin2, (1, 2))


def _head_halves(o):
    hpg = N_HEADS // N_KV_HEADS
    parts = [o[:, h, (h // hpg) * HEAD_DIM:(h // hpg + 1) * HEAD_DIM] for h in range(N_HEADS)]
    return jnp.concatenate(parts, axis=-1)


def kernel(x_prompt, x_sample, cache_k, cache_v, cache_kidx, state_conv, state_ssm_conv, state_ssm, page_table,
           norm1, w_in, b_gate, conv_w, q_norm, k_norm, w_pc, w_pa, ssm_conv_w, ssm_conv_b, dt_bias, a_log,
           d_skip, ssm_norm, w_ps, w_o, norm2, w_gate_dense, w_up_dense, w_down_dense, w_router, w_gate_moe,
           w_up_moe, w_down_moe):
    nb, seq, _ = x_prompt.shape
    db = x_sample.shape[0]
    depth = w_in.shape[0]
    n_pages = page_table.shape[1]
    past = n_pages * PAGE_SIZE

    hp = x_prompt.reshape(nb * seq, D_MODEL)
    hs = x_sample.reshape(db, D_MODEL)
    cos_p, sin_p = _rope_tables(jnp.tile(jnp.arange(seq), nb))
    cos_s, sin_s = _rope_tables(jnp.full((db,), past))

    pool_k = cache_k.reshape(*cache_k.shape[:3], LANES)
    pool_v = cache_v.reshape(*cache_v.shape[:3], LANES)

    outs_p = [[] for _ in range(6)]
    outs_s = [[] for _ in range(6)]
    for l in range(depth):
        wp = _pack_w_in(w_in[l])
        g1 = norm1[l].reshape(1, D_MODEL)
        g2 = norm2[l].reshape(1, D_MODEL)
        qg = jnp.tile(q_norm[l], 2).reshape(1, LANES)
        kg = jnp.tile(k_norm[l], 2).reshape(1, LANES)
        cw = ssm_conv_w[l]
        cbias = ssm_conv_b[l].reshape(1, SSM_CONV_DIM)
        dtb = _pad_lanes(dt_bias[l])
        alog = _pad_lanes(a_log[l])
        dsk = jnp.repeat(d_skip[l], SSM_HEAD_DIM).reshape(1, SSM_INNER)
        sn = ssm_norm[l].reshape(1, SSM_INNER)
        wpc, wpa, wps, wo = (w.astype(BF16) for w in (w_pc[l], w_pa[l], w_ps[l], w_o[l]))
        i = l // 2
        if l % 2 == 0:
            routed = False
            wr = jnp.zeros((D_MODEL, LANES), F32)
            wg, wu, wd = (w[i:i + 1].astype(BF16) for w in (w_gate_dense, w_up_dense, w_down_dense))
        else:
            routed = True
            wr = jnp.zeros((D_MODEL, LANES), F32).at[:, :N_EXPERTS].set(w_router[i])
            wg, wu, wd = (w[i].astype(BF16) for w in (w_gate_moe, w_up_moe, w_down_moe))

        proj = _inproj(hp, g1, wp)
        qt, kf, kb, vt, qit, kif, kib, wt = _prep(proj, cos_p, sin_p, qg, kg, True)
        attn = _dsa_prompt(qt, qit, wt, kb, vt, kib, nb, seq)
        pre_a, conv_st = _conva_prompt(proj, conv_w[l], nb, seq)
        y_ssd, h_last = _ssd_prompt(proj, cw, cbias, dtb, alog, dsk, nb, seq)
        hp = _merge(hp, pre_a, attn, y_ssd, proj, sn, b_gate[l], wpc, wpa, wps, wo)
        if routed and hp.shape[0] % MOE_TILE == 0:
            hp = _moe_grouped(hp, g2, wr, wg, wu, wd)
        else:
            hp = _ffn(hp, g2, wr, wg, wu, wd, routed)
        proj3 = proj.reshape(nb, seq, N_PROJ)
        outs_p[0].append(kf.reshape(nb, seq, N_KV_HEADS, HEAD_DIM))
        outs_p[1].append(proj3[:, :, V_OFF:V_OFF + LANES].reshape(nb, seq, N_KV_HEADS, HEAD_DIM))
        outs_p[2].append(kif[:, :IDX_DIM].reshape(nb, seq, IDX_DIM))
        outs_p[3].append(conv_st)
        outs_p[4].append(proj3[:, seq - 3:, XS_OFF:XS_OFF + SSM_CONV_DIM])
        outs_p[5].append(h_last.reshape(nb, SSM_HEADS, SSM_HEAD_DIM, SSM_STATE))

        proj = _inproj(hs, g1, wp)
        qp, kf, kb, vb, qip, kif, kib, wi = _prep(proj, cos_s, sin_s, qg, kg, False)
        wrows = jnp.broadcast_to(wi[:, :N_IDX_HEADS, None], (db, N_IDX_HEADS, PAGE_SIZE))
        scores = _decscore(page_table, qip.astype(F32).reshape(db, N_IDX_HEADS, LANES), wrows, cache_kidx, l)
        sel, self_sel = _decselect(scores, qip, kib, wi)
        o = _decattn(page_table, qp.astype(F32).reshape(db, N_HEADS, LANES), sel, pool_k, pool_v,
                     kb.astype(F32).reshape(db, 1, LANES), vb.astype(F32).reshape(db, 1, LANES),
                     self_sel.reshape(db, 1, LANES), l)
        attn = _head_halves(o)
        pre_a, u, xs, bcv, dt, ea = _decmix(proj, state_conv[l], state_ssm_conv[l], conv_w[l], cw, cbias, dtb, alog)
        hpg = SSM_HEADS // SSM_GROUPS
        xs4 = xs.reshape(db, SSM_HEADS, SSM_HEAD_DIM, 1)
        bh = jnp.repeat(bcv[:, :SSM_GROUPS * SSM_STATE].reshape(db, SSM_GROUPS, 1, SSM_STATE), hpg, axis=1)
        ch = jnp.repeat(bcv[:, SSM_GROUPS * SSM_STATE:].reshape(db, SSM_GROUPS, 1, SSM_STATE), hpg, axis=1)
        y4, h_new = _decssd(state_ssm[l], dt[:, :SSM_HEADS, None, None], xs4, bh, ch, ea[:, :SSM_HEADS, None, None],
                            d_skip[l].reshape(SSM_HEADS, 1, 1))
        hs = _merge(hs, pre_a, attn, y4.reshape(db, SSM_INNER), proj, sn, b_gate[l], wpc, wpa, wps, wo)
        hs = _ffn(hs, g2, wr, wg, wu, wd, routed)
        outs_s[0].append(kf.reshape(db, 1, N_KV_HEADS, HEAD_DIM))
        outs_s[1].append(proj[:, V_OFF:V_OFF + LANES].reshape(db, 1, N_KV_HEADS, HEAD_DIM))
        outs_s[2].append(kif[:, :IDX_DIM].reshape(db, 1, IDX_DIM))
        outs_s[3].append(jnp.stack([state_conv[l][:, 1], u], axis=1))
        outs_s[4].append(jnp.concatenate(
            [state_ssm_conv[l][:, 1:], proj[:, None, XS_OFF:XS_OFF + SSM_CONV_DIM]], axis=1))
        outs_s[5].append(h_new)

    return (hp.reshape(nb, seq, D_MODEL), hs.reshape(db, 1, D_MODEL),
            *(jnp.stack(o) for o in outs_p), *(jnp.stack(o) for o in outs_s))
```

```python
import functools
import math

import jax
import jax.numpy as jnp
from jax import lax
from jax.experimental import pallas as pl
from jax.experimental.pallas import tpu as pltpu

F32 = jnp.float32
BF16 = jnp.bfloat16
I32 = jnp.int32

D_MODEL = 1024
D_CONV = 512
N_HEADS = 8
HEAD_DIM = 64
N_KV_HEADS = 2
N_IDX_HEADS = 4
IDX_DIM = 64
TOPK_MAX = 256
Q_BLOCK = 128
ROPE_THETA = 10000.0
SSM_INNER = 512
SSM_HEADS = 8
SSM_HEAD_DIM = 64
SSM_GROUPS = 2
SSM_STATE = 128
SSM_CHUNK = 128
SSM_CONV_DIM = 1024
PAGE_SIZE = 128
N_EXPERTS = 8
EPS = 1e-6

LANES = 128
SUBLANES = 8
VMEM_LIMIT = 52 * 1024 * 1024

G_OFF, CB_OFF, CC_OFF, CX_OFF, Q_OFF, Z_OFF, XS_OFF, BC_OFF = 0, 3072, 3584, 4096, 4608, 5120, 5632, 6144
K_OFF, V_OFF, QI_OFF, KW_OFF, DT_OFF = 6656, 6784, 6912, 7168, 7296
N_PROJ = 7680

INT_MIN = -2 ** 31
INT_MAX = 2 ** 31 - 1
KEY_NEG_INF = (-8388608) ^ 0x7FFFFFFF
NEG_BIG = -1e30
LOG2_E = 1.4426950408889634

NT_DIMS = (((1,), (1,)), ((), ()))


def _cparams(sem):
    return pltpu.CompilerParams(dimension_semantics=sem, vmem_limit_bytes=VMEM_LIMIT)


def _sigmoid(x):
    return 1.0 / (1.0 + jnp.exp(-x))


def _silu(x):
    return x * _sigmoid(x)


def _softplus(x):
    return jnp.maximum(x, 0.0) + jnp.log1p(jnp.exp(-jnp.abs(x)))


def _sortable(x):
    bits = pltpu.bitcast(x, I32)
    return bits ^ (lax.shift_right_arithmetic(bits, 31) & 0x7FFFFFFF)


def _rms(x, g):
    return x * lax.rsqrt(jnp.mean(x * x, axis=-1, keepdims=True) + EPS) * g


def _inproj_kernel(x_ref, g_ref, w_ref, o_ref, h_ref):
    @pl.when(pl.program_id(1) == 0)
    def _():
        h_ref[...] = _rms(x_ref[...], g_ref[...]).astype(BF16)

    o_ref[...] = jnp.dot(h_ref[...], w_ref[...], preferred_element_type=F32)


def _inproj(x, g, w):
    m = x.shape[0]
    tm = min(m, 1024)
    tn = 1536
    return pl.pallas_call(
        _inproj_kernel,
        grid=(m // tm, N_PROJ // tn),
        in_specs=[pl.BlockSpec((tm, D_MODEL), lambda i, j: (i, 0)),
                  pl.BlockSpec((1, D_MODEL), lambda i, j: (0, 0)),
                  pl.BlockSpec((D_MODEL, tn), lambda i, j: (0, j))],
        out_specs=pl.BlockSpec((tm, tn), lambda i, j: (i, j)),
        out_shape=jax.ShapeDtypeStruct((m, N_PROJ), F32),
        scratch_shapes=[pltpu.VMEM((tm, D_MODEL), BF16)],
        compiler_params=_cparams(("parallel", "arbitrary")),
        name="inproj",
    )(x, g, w)


V_T_ROWS = LANES + 16


def _prep_kernel(q_ref, k_ref, v_ref, qi_ref, kw_ref, cos_ref, sin_ref, qg_ref, kg_ref,
                 qp_ref, kf_ref, kb_ref, vb_ref, qip_ref, kif_ref, kib_ref, wo_ref, *, transposed):
    tm = cos_ref.shape[0]

    def put(ref, h, val):
        if transposed:
            vt = val.T.astype(BF16)
            for blk in range(tm // Q_BLOCK):
                ref[blk, h * LANES:(h + 1) * LANES, :] = vt[:, blk * Q_BLOCK:(blk + 1) * Q_BLOCK]
        else:
            ref[:, h * LANES:(h + 1) * LANES] = val.astype(BF16)

    cos = cos_ref[...]
    sin = sin_ref[...]
    lane = lax.broadcasted_iota(I32, (tm, LANES), 1)
    low_half = lane < HEAD_DIM
    first_rot = (lane % HEAD_DIM) < (HEAD_DIM // 2)
    r = lax.broadcasted_iota(I32, (LANES, LANES), 0) // HEAD_DIM
    c = lax.broadcasted_iota(I32, (LANES, LANES), 1) // HEAD_DIM
    seg = jnp.where(r == c, 1.0 / HEAD_DIM, 0.0).astype(BF16)

    def rope(x):
        fwd = pltpu.roll(x, LANES - HEAD_DIM // 2, 1)
        bwd = pltpu.roll(x, HEAD_DIM // 2, 1)
        return x * cos + jnp.where(first_rot, fwd, bwd) * sin

    def headnorm(x, g):
        s = x * x
        hi = s.astype(BF16)
        lo = (s - hi.astype(F32)).astype(BF16)
        ms = (jnp.dot(hi, seg, preferred_element_type=F32)
              + jnp.dot(lo, seg, preferred_element_type=F32))
        return x * lax.rsqrt(ms + EPS) * g

    q_scale = HEAD_DIM ** -0.5 * (LOG2_E if transposed else 1.0)
    qg = qg_ref[...]
    for s in range(N_HEADS // 2):
        slab = rope(headnorm(q_ref[:, s * LANES:(s + 1) * LANES], qg)) * q_scale
        swapped = pltpu.roll(slab, HEAD_DIM, 1)
        grp = (2 * s) // (N_HEADS // N_KV_HEADS)
        for hh in range(2):
            h = 2 * s + hh
            src = slab if hh == grp else swapped
            keep = low_half if grp == 0 else jnp.logical_not(low_half)
            put(qp_ref, h, jnp.where(keep, src, 0.0))

    k = rope(headnorm(k_ref[...], kg_ref[...]))
    kf_ref[...] = k
    kb_ref[...] = k.astype(BF16)
    if transposed:
        vb_ref[0:LANES, :] = v_ref[...].T.astype(BF16)
        vb_ref[LANES:V_T_ROWS, :] = jnp.ones((V_T_ROWS - LANES, tm), BF16)
    else:
        vb_ref[...] = v_ref[...].astype(BF16)

    for s in range(N_IDX_HEADS // 2):
        slab = rope(qi_ref[:, s * LANES:(s + 1) * LANES]) * (IDX_DIM ** -0.5)
        swapped = pltpu.roll(slab, HEAD_DIM, 1)
        for hh in range(2):
            h = 2 * s + hh
            src = slab if hh == 0 else swapped
            put(qip_ref, h, jnp.where(low_half, src, 0.0))

    kw = kw_ref[...]
    ki = rope(kw)
    kif_ref[...] = ki
    kib_ref[...] = jnp.where(low_half, ki, 0.0).astype(BF16)
    wi = pltpu.roll(kw, HEAD_DIM, 1) * (N_IDX_HEADS ** -0.5)
    if transposed:
        wt = wi.T[0:SUBLANES, :]
        for blk in range(tm // Q_BLOCK):
            wo_ref[blk] = wt[:, blk * Q_BLOCK:(blk + 1) * Q_BLOCK]
    else:
        wo_ref[...] = wi


def _prep(proj, cos, sin, qg, kg, transposed):
    m = proj.shape[0]
    tm = min(m, 512)
    row = lambda w, off: pl.BlockSpec((tm, w), lambda i: (i, off // w))
    full = lambda w: pl.BlockSpec((tm, w), lambda i: (i, 0))
    const = pl.BlockSpec((1, LANES), lambda i: (0, 0))
    if transposed:
        qblocks = tm // Q_BLOCK
        feat = lambda r: pl.BlockSpec((qblocks, r, Q_BLOCK), lambda i: (i, 0, 0))
        fshape = lambda r, dt: jax.ShapeDtypeStruct((m // Q_BLOCK, r, Q_BLOCK), dt)
        vspec = pl.BlockSpec((V_T_ROWS, tm), lambda i: (0, i))
        vshape = jax.ShapeDtypeStruct((V_T_ROWS, m), BF16)
    else:
        feat = full
        fshape = lambda r, dt: jax.ShapeDtypeStruct((m, r), dt)
        vspec = full(LANES)
        vshape = jax.ShapeDtypeStruct((m, LANES), BF16)
    return pl.pallas_call(
        functools.partial(_prep_kernel, transposed=transposed),
        grid=(m // tm,),
        in_specs=[row(512, Q_OFF), row(LANES, K_OFF), row(LANES, V_OFF), row(256, QI_OFF), row(LANES, KW_OFF),
                  full(LANES), full(LANES), const, const],
        out_specs=[feat(N_HEADS * LANES), full(LANES), full(LANES), vspec,
                   feat(N_IDX_HEADS * LANES), full(LANES), full(LANES), feat(SUBLANES if transposed else LANES)],
        out_shape=[fshape(N_HEADS * LANES, BF16),
                   jax.ShapeDtypeStruct((m, LANES), F32),
                   jax.ShapeDtypeStruct((m, LANES), BF16),
                   vshape,
                   fshape(N_IDX_HEADS * LANES, BF16),
                   jax.ShapeDtypeStruct((m, LANES), F32),
                   jax.ShapeDtypeStruct((m, LANES), BF16),
                   fshape(SUBLANES if transposed else LANES, F32)],
        compiler_params=_cparams(("parallel",)),
        name="prep",
    )(proj, proj, proj, proj, proj, cos, sin, qg, kg)


def _fold_rows(f):
    while f.shape[0] > SUBLANES:
        half = f.shape[0] // 2
        f = f[:half] + f[half:]
    return f


def _dsa_kernel(qt_ref, qit_ref, wt_ref, k_ref, vt_ref, ki_ref, o_ref,
                keys_ref, tpos_ref, x_ref, m_ref, kmax_ref, acc_ref, *, k_sel, kc, nbits):
    qb = Q_BLOCK
    j = pl.program_id(1)
    nck = (j * qb + qb + kc - 1) // kc
    qpos = j * qb + lax.broadcasted_iota(I32, (1, qb), 1)
    sub_k = lax.broadcasted_iota(I32, (kc, qb), 0)
    pair = lambda ref, s: jnp.concatenate(
        [ref[(2 * s) * LANES:(2 * s + 1) * LANES, :], ref[(2 * s + 1) * LANES:(2 * s + 2) * LANES, :]], axis=1)
    qi2 = [pair(qit_ref, s) for s in range(N_IDX_HEADS // 2)]
    q2 = [pair(qt_ref, s) for s in range(N_HEADS // 2)]

    def score_body(c, carry):
        off = pl.multiple_of(c * kc, kc)
        kic = ki_ref[pl.ds(off, kc), :]
        acc = jnp.zeros((kc, qb), F32)
        for s in range(N_IDX_HEADS // 2):
            s2 = jnp.dot(kic, qi2[s], preferred_element_type=F32)
            for hh in range(2):
                h = 2 * s + hh
                acc = acc + jnp.maximum(s2[:, hh * qb:(hh + 1) * qb], 0.0) * wt_ref[h:h + 1, :]
        acc = jnp.where(off + sub_k <= qpos, acc, -jnp.inf)
        keys_ref[pl.ds(off, kc), :] = _sortable(acc)
        return carry

    lax.fori_loop(0, nck, score_body, 0)

    def count_where(pred, ref=keys_ref):
        def body(c, acc):
            off = pl.multiple_of(c * kc, kc)
            return acc + _fold_rows(jnp.where(pred(ref[pl.ds(off, kc), :], off), 1.0, 0.0))

        acc = lax.fori_loop(0, nck, body, jnp.zeros((SUBLANES, qb), F32))
        return jnp.sum(acc, axis=0, keepdims=True)

    def bit_body(i, carry):
        t, cnt_acc, cnt_rej = carry
        cand = t + lax.shift_left(jnp.int32(1), 31 - i)
        cnt = count_where(lambda kk, off: kk >= cand)
        ok = cnt >= k_sel
        return jnp.where(ok, cand, t), jnp.where(ok, cnt, cnt_acc), jnp.where(ok, cnt_rej, cnt)

    zero = jnp.zeros((1, qb), F32)
    thr, cnt_ge, cnt_gt = lax.fori_loop(0, 32, bit_body, (jnp.full((1, qb), INT_MIN, I32), zero, zero))
    thr = jnp.maximum(thr, KEY_NEG_INF)
    finite_thr = thr > KEY_NEG_INF
    need = k_sel - cnt_gt
    tie = jnp.logical_and(cnt_ge > k_sel, finite_thr)
    x_default = jnp.where(finite_thr, INT_MAX, -1)
    x_ref[...] = jnp.broadcast_to(x_default, x_ref.shape)

    @pl.when(jnp.max(jnp.where(tie, 1.0, 0.0)) > 0.0)
    def _():
        def tie_pos_body(c, carry):
            off = pl.multiple_of(c * kc, kc)
            tpos_ref[pl.ds(off, kc), :] = jnp.where(keys_ref[pl.ds(off, kc), :] == thr, off + sub_k, INT_MAX)
            return carry

        lax.fori_loop(0, nck, tie_pos_body, 0)

        def xbody(i, x):
            cand = x + lax.shift_left(jnp.int32(1), nbits - 1 - i)
            cnt = count_where(lambda tp, off: tp < cand, tpos_ref)
            return jnp.where(cnt < need, cand, x)

        x = lax.fori_loop(0, nbits, xbody, jnp.zeros((1, qb), I32))
        x_ref[...] = jnp.broadcast_to(jnp.where(tie, x, x_default), x_ref.shape)

    xcut = x_ref[0:1, :]
    hpg = N_HEADS // N_KV_HEADS

    @pl.when(j == 0)
    def _():
        r = lax.broadcasted_iota(I32, (LANES, LANES), 0) // HEAD_DIM
        cc = lax.broadcasted_iota(I32, (LANES, LANES), 1) // HEAD_DIM
        seg = jnp.where(r == cc, 1.0, 0.0).astype(BF16)

        def body(c, mx):
            kf = k_ref[pl.ds(pl.multiple_of(c * kc, kc), kc), :].astype(F32)
            gs = jnp.dot((kf * kf).astype(BF16), seg, preferred_element_type=F32)
            while gs.shape[0] > SUBLANES:
                half = gs.shape[0] // 2
                gs = jnp.maximum(gs[:half], gs[half:])
            return jnp.maximum(mx, gs)

        mx = lax.fori_loop(0, k_ref.shape[0] // kc, body, jnp.zeros((SUBLANES, LANES), F32))
        mx = jnp.max(mx, axis=0, keepdims=True)
        lane = lax.broadcasted_iota(I32, (1, LANES), 1)
        other = pltpu.roll(mx, HEAD_DIM, 1)
        kmax_ref[0:1, :] = jnp.where(lane < HEAD_DIM, mx, other)
        kmax_ref[1:2, :] = jnp.where(lane < HEAD_DIM, other, mx)

    def chunk(c):
        off = pl.multiple_of(c * kc, kc)
        kk = keys_ref[pl.ds(off, kc), :]
        sel = jnp.logical_or(kk > thr, jnp.logical_and(kk == thr, off + sub_k <= xcut))
        return sel, k_ref[pl.ds(off, kc), :], vt_ref[:, pl.ds(off, kc)]

    bound = []
    for h in range(N_HEADS):
        qh = qt_ref[h * LANES:(h + 1) * LANES, :].astype(F32)
        qsq = jnp.sum(qh * qh, axis=0, keepdims=True)
        bound.append(jnp.sqrt(qsq * kmax_ref[h // hpg:h // hpg + 1, :]))
    acc_ref[...] = jnp.zeros(acc_ref.shape, F32)

    def fast_body(c, carry):
        sel, kch, vch = chunk(c)
        n_pair = N_HEADS // 2
        s2 = jnp.dot(kch, q2[0], preferred_element_type=F32)
        for s in range(n_pair):
            s2_next = jnp.dot(kch, q2[s + 1], preferred_element_type=F32) if s + 1 < n_pair else None
            ps = [jnp.exp2(jnp.where(sel, s2[:, hh * qb:(hh + 1) * qb] - bound[2 * s + hh], -jnp.inf)).astype(BF16)
                  for hh in range(2)]
            acc_ref[s] += jnp.dot(vch, jnp.concatenate(ps, axis=1), preferred_element_type=F32)
            s2 = s2_next
        return carry

    lax.fori_loop(0, nck, fast_body, 0)

    dens = jnp.concatenate([acc_ref[s, LANES:LANES + 1, :] for s in range(N_HEADS // 2)], axis=0)

    @pl.when(jnp.logical_not(jnp.min(dens) >= 1e-20))
    def _():
        m_ref[...] = jnp.full(m_ref.shape, NEG_BIG, F32)
        acc_ref[...] = jnp.zeros(acc_ref.shape, F32)

        def exact_body(c, carry):
            sel, kch, vch = chunk(c)
            for s in range(N_HEADS // 2):
                s2 = jnp.dot(kch, q2[s], preferred_element_type=F32)
                ps, alphas = [], []
                for hh in range(2):
                    h = 2 * s + hh
                    sh = jnp.where(sel, s2[:, hh * qb:(hh + 1) * qb], -jnp.inf)
                    m_old = m_ref[h:h + 1, :]
                    m_new = jnp.maximum(m_old, jnp.max(sh, axis=0, keepdims=True))
                    m_ref[h:h + 1, :] = m_new
                    alphas.append(jnp.exp2(m_old - m_new))
                    ps.append(jnp.exp2(sh - m_new).astype(BF16))
                o2 = jnp.dot(vch, jnp.concatenate(ps, axis=1), preferred_element_type=F32)
                acc_ref[s] = jnp.concatenate(alphas, axis=1) * acc_ref[s] + o2
            return carry

        lax.fori_loop(0, nck, exact_body, 0)

    for s in range(N_HEADS // 2):
        g = (2 * s) // hpg
        num = acc_ref[s, g * HEAD_DIM:(g + 1) * HEAD_DIM, :]
        out2 = num / acc_ref[s, LANES:LANES + 1, :]
        o_ref[:, s * LANES:(s + 1) * LANES] = jnp.concatenate([out2[:, :qb], out2[:, qb:]], axis=0).T


def _dsa_prompt(qt, qit, wt, kb, vt, kib, nb, seq):
    k_sel = min(TOPK_MAX, seq // 4)
    kc = next(c for c in (512, Q_BLOCK) if seq % c == 0)
    nbits = max(1, (seq - 1).bit_length())
    nq = seq // Q_BLOCK
    r3 = lambda a: a.reshape(nb, seq, a.shape[-1])
    qcol = lambda r: pl.BlockSpec((None, r, Q_BLOCK), lambda b, j: (b * nq + j, 0, 0))
    seqblk = pl.BlockSpec((None, seq, LANES), lambda b, j: (b, 0, 0))
    out = pl.pallas_call(
        functools.partial(_dsa_kernel, k_sel=k_sel, kc=kc, nbits=nbits),
        grid=(nb, nq),
        in_specs=[qcol(N_HEADS * LANES), qcol(N_IDX_HEADS * LANES), qcol(SUBLANES), seqblk,
                  pl.BlockSpec((V_T_ROWS, seq), lambda b, j: (0, b)), seqblk],
        out_specs=pl.BlockSpec((None, Q_BLOCK, N_HEADS * HEAD_DIM), lambda b, j: (b, j, 0)),
        out_shape=jax.ShapeDtypeStruct((nb, seq, N_HEADS * HEAD_DIM), F32),
        scratch_shapes=[pltpu.VMEM((seq, Q_BLOCK), I32),
                        pltpu.VMEM((seq, Q_BLOCK), I32),
                        pltpu.VMEM((SUBLANES, Q_BLOCK), I32),
                        pltpu.VMEM((N_HEADS, Q_BLOCK), F32),
                        pltpu.VMEM((SUBLANES, LANES), F32),
                        pltpu.VMEM((N_HEADS // 2, V_T_ROWS, 2 * Q_BLOCK), F32)],
        compiler_params=_cparams(("arbitrary", "arbitrary")),
        name="dsa_prompt",
    )(qt, qit, wt, r3(kb), vt, r3(kib))
    return out.reshape(nb * seq, N_HEADS * HEAD_DIM)


def _conva_kernel(cb_ref, cc_ref, cx_ref, cch_ref, cxh_ref, w_ref, o_ref, st_ref, *, tiles_per_seq):
    tm = cb_ref.shape[0]
    first = (pl.program_id(0) % tiles_per_seq) == 0
    u = cc_ref[...] * cx_ref[...]
    uh = jnp.where(first, 0.0, cch_ref[...] * cxh_ref[...])
    ext = jnp.concatenate([uh, u], axis=0)
    conv = (w_ref[2:3, :] * u + w_ref[1:2, :] * ext[SUBLANES - 1:SUBLANES - 1 + tm]
            + w_ref[0:1, :] * ext[SUBLANES - 2:SUBLANES - 2 + tm])
    o_ref[...] = cb_ref[...] * conv
    st_ref[...] = u[tm - 2:tm, :]


def _conva_prompt(proj, conv_w, nb, seq):
    m = proj.shape[0]
    tm = min(seq, 512)
    tps = seq // tm
    col = lambda off: pl.BlockSpec((tm, D_CONV), lambda i: (i, off // D_CONV))
    halo = lambda off: pl.BlockSpec(
        (SUBLANES, D_CONV), lambda i: (jnp.maximum(i * (tm // SUBLANES) - 1, 0), off // D_CONV))
    return pl.pallas_call(
        functools.partial(_conva_kernel, tiles_per_seq=tps),
        grid=(m // tm,),
        in_specs=[col(CB_OFF), col(CC_OFF), col(CX_OFF), halo(CC_OFF), halo(CX_OFF),
                  pl.BlockSpec((3, D_CONV), lambda i: (0, 0))],
        out_specs=[pl.BlockSpec((tm, D_CONV), lambda i: (i, 0)),
                   pl.BlockSpec((None, 2, D_CONV), lambda i: (i // tps, 0, 0))],
        out_shape=[jax.ShapeDtypeStruct((m, D_CONV), F32),
                   jax.ShapeDtypeStruct((nb, 2, D_CONV), F32)],
        compiler_params=_cparams(("arbitrary",)),
        name="conva_prompt",
    )(proj, proj, proj, proj, proj, conv_w)


def _ssd_kernel(xs_ref, bc_ref, dt_ref, xh_ref, bh_ref, cwx_ref, cwb_ref, cbx_ref, cbb_ref,
                dtb_ref, alog_ref, dsk_ref, y_ref, hout_ref, h_ref, *, nchunk):
    cl = SSM_CHUNK
    c = pl.program_id(1)
    first = c == 0

    @pl.when(first)
    def _():
        h_ref[...] = jnp.zeros(h_ref.shape, F32)

    def conv(cur, halo, w_ref, b_ref):
        ext = jnp.concatenate([jnp.where(first, 0.0, halo), cur], axis=0)
        out = (w_ref[3:4, :] * cur + w_ref[2:3, :] * ext[SUBLANES - 1:SUBLANES - 1 + cl]
               + w_ref[1:2, :] * ext[SUBLANES - 2:SUBLANES - 2 + cl]
               + w_ref[0:1, :] * ext[SUBLANES - 3:SUBLANES - 3 + cl] + b_ref[...])
        return _silu(out)

    xs = conv(xs_ref[...], xh_ref[...], cwx_ref, cbx_ref)
    bc = conv(bc_ref[...], bh_ref[...], cwb_ref, cbb_ref)
    dt = _softplus(dt_ref[...] + dtb_ref[...])
    a = dt * (-jnp.exp(alog_ref[...]))
    ri = lax.broadcasted_iota(I32, (cl, cl), 0)
    ci = lax.broadcasted_iota(I32, (cl, cl), 1)
    causal = ri >= ci
    cs = jnp.dot(jnp.where(causal, 1.0, 0.0), a, preferred_element_type=F32,
                 precision=lax.Precision.HIGHEST)
    cs_t = cs.T
    lane = lax.broadcasted_iota(I32, (cl, LANES), 1)
    lo = lane < SSM_HEAD_DIM
    rows_lo = lax.broadcasted_iota(I32, (LANES, 1), 0) < SSM_HEAD_DIM
    heads_per_group = SSM_HEADS // SSM_GROUPS

    cb = []
    for g in range(SSM_GROUPS):
        bg = bc[:, g * SSM_STATE:(g + 1) * SSM_STATE].astype(BF16)
        cg = bc[:, (SSM_GROUPS + g) * SSM_STATE:(SSM_GROUPS + g + 1) * SSM_STATE].astype(BF16)
        cb.append((bg, cg, lax.dot_general(cg, bg, NT_DIMS, preferred_element_type=F32)))

    for s in range(SSM_HEADS // 2):
        h0, h1 = 2 * s, 2 * s + 1
        bg, cg, cbg = cb[h0 // heads_per_group]
        sl = slice(s * LANES, (s + 1) * LANES)
        xs_s = xs[:, sl]
        col0, col1 = cs[:, h0:h0 + 1], cs[:, h1:h1 + 1]
        last0, last1 = cs[cl - 1:cl, h0:h0 + 1], cs[cl - 1:cl, h1:h1 + 1]
        xdt = xs_s * jnp.where(lo, dt[:, h0:h0 + 1], dt[:, h1:h1 + 1])
        xdt_b = xdt.astype(BF16)
        m0 = (cbg * jnp.where(causal, jnp.exp(col0 - cs_t[h0:h0 + 1, :]), 0.0)).astype(BF16)
        m1 = (cbg * jnp.where(causal, jnp.exp(col1 - cs_t[h1:h1 + 1, :]), 0.0)).astype(BF16)
        y_diag = jnp.where(lo, jnp.dot(m0, xdt_b, preferred_element_type=F32),
                           jnp.dot(m1, xdt_b, preferred_element_type=F32))
        hs = h_ref[sl, :]
        y_off = lax.dot_general(cg, hs.astype(BF16), NT_DIMS, preferred_element_type=F32)
        y_off = y_off * jnp.where(lo, jnp.exp(col0), jnp.exp(col1))
        y_ref[:, sl] = y_diag + y_off + dsk_ref[:, sl] * xs_s
        xw = xdt * jnp.where(lo, jnp.exp(last0 - col0), jnp.exp(last1 - col1))
        st = jnp.dot(xw.T.astype(BF16), bg, preferred_element_type=F32)
        h_ref[sl, :] = hs * jnp.where(rows_lo, jnp.exp(last0), jnp.exp(last1)) + st

    @pl.when(c == nchunk - 1)
    def _():
        hout_ref[...] = h_ref[...]


def _ssd_prompt(proj, cw, cbias, dtb, alog, dsk, nb, seq):
    m = proj.shape[0]
    cl = SSM_CHUNK
    nchunk = seq // cl
    blk = lambda w, off: pl.BlockSpec((cl, w), lambda b, c: (b * nchunk + c, off // w))
    halo = lambda off: pl.BlockSpec(
        (SUBLANES, 512), lambda b, c: (jnp.maximum((b * nchunk + c) * (cl // SUBLANES) - 1, 0), off // 512))
    const = lambda r, w, j: pl.BlockSpec((r, w), lambda b, c: (0, j))
    y, hout = pl.pallas_call(
        functools.partial(_ssd_kernel, nchunk=nchunk),
        grid=(nb, nchunk),
        in_specs=[blk(512, XS_OFF), blk(512, BC_OFF), blk(LANES, DT_OFF), halo(XS_OFF), halo(BC_OFF),
                  const(4, 512, 0), const(4, 512, 1), const(1, 512, 0), const(1, 512, 1),
                  const(1, LANES, 0), const(1, LANES, 0), const(1, 512, 0)],
        out_specs=[pl.BlockSpec((cl, SSM_INNER), lambda b, c: (b * nchunk + c, 0)),
                   pl.BlockSpec((None, SSM_INNER, SSM_STATE), lambda b, c: (b, 0, 0))],
        out_shape=[jax.ShapeDtypeStruct((m, SSM_INNER), F32),
                   jax.ShapeDtypeStruct((nb, SSM_INNER, SSM_STATE), F32)],
        scratch_shapes=[pltpu.VMEM((SSM_INNER, SSM_STATE), F32)],
        compiler_params=_cparams(("parallel", "arbitrary")),
        name="ssd_prompt",
    )(proj, proj, proj, proj, proj, cw, cw, cbias, cbias, dtb, alog, dsk)
    return y, hout


def _merge_kernel(x_ref, pa_ref, at_ref, ys_ref, z_ref, g0_ref, g1_ref, g2_ref, sn_ref, bg_ref,
                  wpc_ref, wpa_ref, wps_ref, wo_ref, o_ref):
    ssd = _rms(ys_ref[...] * _silu(z_ref[...]), sn_ref[...])
    ya = jnp.dot(pa_ref[...].astype(BF16), wpc_ref[...], preferred_element_type=F32)
    yb = jnp.dot(at_ref[...].astype(BF16), wpa_ref[...], preferred_element_type=F32)
    yc = jnp.dot(ssd.astype(BF16), wps_ref[...], preferred_element_type=F32)
    merged = (_sigmoid(g0_ref[...] + bg_ref[0:1, :]) * ya + _sigmoid(g1_ref[...] + bg_ref[1:2, :]) * yb
              + _sigmoid(g2_ref[...] + bg_ref[2:3, :]) * yc)
    o_ref[...] = x_ref[...] + jnp.dot(merged.astype(BF16), wo_ref[...], preferred_element_type=F32)


def _merge(x, pre_a, attn, y_ssd, proj, ssm_norm, b_gate, w_pc, w_pa, w_ps, w_o):
    m = x.shape[0]
    tm = min(m, 256)
    row = lambda w: pl.BlockSpec((tm, w), lambda i: (i, 0))
    pcol = lambda w, off: pl.BlockSpec((tm, w), lambda i: (i, off // w))
    const = lambda r, w: pl.BlockSpec((r, w), lambda i: (0, 0))
    return pl.pallas_call(
        _merge_kernel,
        grid=(m // tm,),
        in_specs=[row(D_MODEL), row(512), row(512), row(512), pcol(512, Z_OFF),
                  pcol(D_MODEL, G_OFF), pcol(D_MODEL, G_OFF + D_MODEL), pcol(D_MODEL, G_OFF + 2 * D_MODEL),
                  const(1, 512), const(3, D_MODEL),
                  const(512, D_MODEL), const(512, D_MODEL), const(512, D_MODEL), const(D_MODEL, D_MODEL)],
        out_specs=row(D_MODEL),
        out_shape=jax.ShapeDtypeStruct((m, D_MODEL), F32),
        compiler_params=_cparams(("parallel",)),
        name="merge",
    )(x, pre_a, attn, y_ssd, proj, proj, proj, proj, ssm_norm, b_gate, w_pc, w_pa, w_ps, w_o)


def _ffn_kernel(x_ref, g_ref, wr_ref, wg_ref, wu_ref, wd_ref, o_ref, h_ref, acc_ref, comb_ref,
                *, routed, n_e, n_f):
    e = pl.program_id(1)
    j = pl.program_id(2)
    tm = x_ref.shape[0]

    @pl.when(jnp.logical_and(e == 0, j == 0))
    def _():
        hf = _rms(x_ref[...], g_ref[...])
        h_ref[...] = hf.astype(BF16)
        acc_ref[...] = jnp.zeros(acc_ref.shape, F32)
        if routed:
            lane = lax.broadcasted_iota(I32, (tm, LANES), 1).astype(F32)
            logits = jnp.dot(hf, wr_ref[...], preferred_element_type=F32, precision=lax.Precision.HIGHEST)
            logits = jnp.where(lane < n_e, logits, -jnp.inf)
            m1 = jnp.max(logits, axis=1, keepdims=True)
            i1 = jnp.min(jnp.where(logits == m1, lane, float(LANES)), axis=1, keepdims=True)
            rest = jnp.where(lane == i1, -jnp.inf, logits)
            m2 = jnp.max(rest, axis=1, keepdims=True)
            i2 = jnp.min(jnp.where(rest == m2, lane, float(LANES)), axis=1, keepdims=True)
            e2 = jnp.exp(m2 - m1)
            den = 1.0 + e2
            comb_ref[...] = jnp.where(lane == i1, 1.0 / den, 0.0) + jnp.where(lane == i2, e2 / den, 0.0)

    h = h_ref[...]
    gate = jnp.dot(h, wg_ref[...], preferred_element_type=F32)
    up = jnp.dot(h, wu_ref[...], preferred_element_type=F32)
    act = _silu(gate) * up
    if routed:
        lane = lax.broadcasted_iota(I32, (tm, LANES), 1)
        act = act * jnp.sum(jnp.where(lane == e, comb_ref[...], 0.0), axis=1, keepdims=True)
    acc_ref[...] += jnp.dot(act.astype(BF16), wd_ref[...], preferred_element_type=F32)

    @pl.when(jnp.logical_and(e == n_e - 1, j == n_f - 1))
    def _():
        o_ref[...] = x_ref[...] + acc_ref[...]


def _ffn(x, g, w_router, w_gate, w_up, w_down, routed):
    m = x.shape[0]
    n_e, _, d_ff = w_gate.shape
    tm = min(m, 1024)
    tf = 256
    n_f = d_ff // tf
    return pl.pallas_call(
        functools.partial(_ffn_kernel, routed=routed, n_e=n_e, n_f=n_f),
        grid=(m // tm, n_e, n_f),
        in_specs=[pl.BlockSpec((tm, D_MODEL), lambda i, e, j: (i, 0)),
                  pl.BlockSpec((1, D_MODEL), lambda i, e, j: (0, 0)),
                  pl.BlockSpec((D_MODEL, LANES), lambda i, e, j: (0, 0)),
                  pl.BlockSpec((None, D_MODEL, tf), lambda i, e, j: (e, 0, j)),
                  pl.BlockSpec((None, D_MODEL, tf), lambda i, e, j: (e, 0, j)),
                  pl.BlockSpec((None, tf, D_MODEL), lambda i, e, j: (e, j, 0))],
        out_specs=pl.BlockSpec((tm, D_MODEL), lambda i, e, j: (i, 0)),
        out_shape=jax.ShapeDtypeStruct((m, D_MODEL), F32),
        scratch_shapes=[pltpu.VMEM((tm, D_MODEL), BF16), pltpu.VMEM((tm, D_MODEL), F32),
                        pltpu.VMEM((tm, LANES), F32)],
        compiler_params=_cparams(("parallel", "arbitrary", "arbitrary")),
        name="moe" if routed else "ffn",
    )(x, g, w_router, w_gate, w_up, w_down)


MOE_BLOCK = 2048
MOE_TILE = 256
TOP_K = 2


def _moe_tiles(tb):
    return TOP_K * tb // MOE_TILE + N_EXPERTS


def _route_kernel(x_ref, g_ref, wr_ref, hb_ref, row_ref, col_ref, meta_ref):
    tb = x_ref.shape[0]
    hf = _rms(x_ref[...], g_ref[...])
    hb_ref[...] = hf.astype(BF16)
    lane_i = lax.broadcasted_iota(I32, (tb, LANES), 1)
    lane = lane_i.astype(F32)
    logits = jnp.dot(hf, wr_ref[...], preferred_element_type=F32, precision=lax.Precision.HIGHEST)
    logits = jnp.where(lane_i < N_EXPERTS, logits, -jnp.inf)
    m1 = jnp.max(logits, axis=1, keepdims=True)
    i1 = jnp.min(jnp.where(logits == m1, lane, float(LANES)), axis=1, keepdims=True)
    rest = jnp.where(lane == i1, -jnp.inf, logits)
    m2 = jnp.max(rest, axis=1, keepdims=True)
    i2 = jnp.min(jnp.where(rest == m2, lane, float(LANES)), axis=1, keepdims=True)
    e2 = jnp.exp(m2 - m1)
    p1 = 1.0 / (1.0 + e2)
    p2 = e2 / (1.0 + e2)
    oh1 = lane == i1
    oh2 = lane == i2
    both = jnp.where(jnp.logical_or(oh1, oh2), 1.0, 0.0).astype(BF16)

    sb = MOE_TILE
    ri = lax.broadcasted_iota(I32, (sb, sb), 0)
    ci = lax.broadcasted_iota(I32, (sb, sb), 1)
    strict = jnp.where(ri > ci, 1.0, 0.0).astype(BF16)
    carry = jnp.zeros((1, LANES), F32)
    ranks = []
    for s in range(tb // sb):
        rows = both[s * sb:(s + 1) * sb]
        ranks.append(jnp.dot(strict, rows, preferred_element_type=F32) + carry)
        carry = carry + jnp.sum(rows.astype(F32), axis=0, keepdims=True)
    rank = jnp.concatenate(ranks, axis=0)
    seg = jnp.floor((carry + (MOE_TILE - 1)) * (1.0 / MOE_TILE)) * MOE_TILE
    ui = lax.broadcasted_iota(I32, (LANES, LANES), 0)
    uj = lax.broadcasted_iota(I32, (LANES, LANES), 1)
    before = jnp.where(ui < uj, 1.0, 0.0)
    off = jnp.dot(jnp.broadcast_to(seg, (SUBLANES, LANES)), before, preferred_element_type=F32,
                  precision=lax.Precision.HIGHEST)[0:1, :]
    dest = off + rank
    d1 = jnp.sum(jnp.where(oh1, dest, 0.0), axis=1, keepdims=True)
    d2 = jnp.sum(jnp.where(oh2, dest, 0.0), axis=1, keepdims=True)
    col = jnp.where(lane_i == 0, d1, jnp.where(lane_i == 1, d2, jnp.where(lane_i == 2, p1,
                                                                          jnp.where(lane_i == 3, p2, 0.0))))
    col_ref[...] = col
    row_ref[...] = col.T[0:SUBLANES, :]

    end_rows = jnp.broadcast_to(off + seg, (LANES, LANES)).T
    start = (lax.broadcasted_iota(I32, (LANES, LANES), 1) * MOE_TILE).astype(F32)
    is_expert = lax.broadcasted_iota(I32, (LANES, LANES), 0) < N_EXPERTS
    done = jnp.where(jnp.logical_and(is_expert, end_rows <= start), 1.0, 0.0)
    expert = jnp.minimum(jnp.sum(done, axis=0, keepdims=True), float(N_EXPERTS - 1))
    total = jnp.sum(jnp.where(lane_i[0:1] < N_EXPERTS, seg, 0.0), axis=1, keepdims=True)
    valid = jnp.where(start[0:1] < total, 1.0, 0.0)
    sub = lax.broadcasted_iota(I32, (SUBLANES, LANES), 0)
    meta_ref[...] = jnp.where(sub == 0, expert, jnp.where(sub == 1, valid, 0.0)).astype(I32)


def _route(x, g, w_router, tb):
    m = x.shape[0]
    nblk = m // tb
    return pl.pallas_call(
        _route_kernel,
        grid=(nblk,),
        in_specs=[pl.BlockSpec((tb, D_MODEL), lambda b: (b, 0)),
                  pl.BlockSpec((1, D_MODEL), lambda b: (0, 0)),
                  pl.BlockSpec((D_MODEL, LANES), lambda b: (0, 0))],
        out_specs=[pl.BlockSpec((tb, D_MODEL), lambda b: (b, 0)),
                   pl.BlockSpec((SUBLANES, tb), lambda b: (0, b)),
                   pl.BlockSpec((tb, LANES), lambda b: (b, 0)),
                   pl.BlockSpec((None, SUBLANES, LANES), lambda b: (b, 0, 0))],
        out_shape=[jax.ShapeDtypeStruct((m, D_MODEL), BF16),
                   jax.ShapeDtypeStruct((SUBLANES, m), F32),
                   jax.ShapeDtypeStruct((m, LANES), F32),
                   jax.ShapeDtypeStruct((nblk, SUBLANES, LANES), I32)],
        compiler_params=_cparams(("parallel",)),
        name="moe_route",
    )(x, g, w_router)


def _gffn_kernel(te_ref, tv_ref, hb_ref, row_ref, wg_ref, wu_ref, wd_ref, ys_ref, xg_ref, acc_ref, *, nt, n_f):
    del te_ref
    b, t, j = pl.program_id(0), pl.program_id(1), pl.program_id(2)
    valid = tv_ref[b * nt + t] > 0
    tb = hb_ref.shape[0]

    @pl.when(jnp.logical_and(valid, j == 0))
    def _():
        slot = (t * MOE_TILE + lax.broadcasted_iota(I32, (MOE_TILE, tb), 0)).astype(F32)
        hit = jnp.logical_or(row_ref[0:1, :] == slot, row_ref[1:2, :] == slot)
        onehot = jnp.where(hit, 1.0, 0.0).astype(BF16)
        xg_ref[...] = jnp.dot(onehot, hb_ref[...], preferred_element_type=F32).astype(BF16)
        acc_ref[...] = jnp.zeros(acc_ref.shape, F32)

    @pl.when(valid)
    def _():
        xg = xg_ref[...]
        gate = jnp.dot(xg, wg_ref[...], preferred_element_type=F32)
        up = jnp.dot(xg, wu_ref[...], preferred_element_type=F32)
        acc_ref[...] += jnp.dot((_silu(gate) * up).astype(BF16), wd_ref[...], preferred_element_type=F32)

    @pl.when(j == n_f - 1)
    def _():
        ys_ref[...] = jnp.where(valid, acc_ref[...], 0.0).astype(BF16)


def _gffn(tile_expert, tile_valid, hb, rowinfo, w_gate, w_up, w_down, tb):
    m = hb.shape[0]
    nblk = m // tb
    nt = _moe_tiles(tb)
    d_ff = w_gate.shape[2]
    tf = d_ff // 2
    n_f = d_ff // tf
    fidx = lambda b, t, j, te, tv: jnp.where(tv[b * nt + t] > 0, j, n_f - 1)
    grid_spec = pltpu.PrefetchScalarGridSpec(
        num_scalar_prefetch=2,
        grid=(nblk, nt, n_f),
        in_specs=[pl.BlockSpec((tb, D_MODEL), lambda b, t, j, te, tv: (b, 0)),
                  pl.BlockSpec((SUBLANES, tb), lambda b, t, j, te, tv: (0, b)),
                  pl.BlockSpec((None, D_MODEL, tf), lambda b, t, j, te, tv: (te[b * nt + t], 0, fidx(b, t, j, te, tv))),
                  pl.BlockSpec((None, D_MODEL, tf), lambda b, t, j, te, tv: (te[b * nt + t], 0, fidx(b, t, j, te, tv))),
                  pl.BlockSpec((None, tf, D_MODEL), lambda b, t, j, te, tv: (te[b * nt + t], fidx(b, t, j, te, tv), 0))],
        out_specs=pl.BlockSpec((MOE_TILE, D_MODEL), lambda b, t, j, te, tv: (b * nt + t, 0)),
        scratch_shapes=[pltpu.VMEM((MOE_TILE, D_MODEL), BF16), pltpu.VMEM((MOE_TILE, D_MODEL), F32)],
    )
    return pl.pallas_call(
        functools.partial(_gffn_kernel, nt=nt, n_f=n_f),
        grid_spec=grid_spec,
        out_shape=jax.ShapeDtypeStruct((nblk * nt * MOE_TILE, D_MODEL), BF16),
        compiler_params=_cparams(("parallel", "arbitrary", "arbitrary")),
        name="moe_experts",
    )(tile_expert, tile_valid, hb, rowinfo, w_gate, w_up, w_down)


def _combine_kernel(tv_ref, x_ref, col_ref, ys_ref, o_ref, *, nt):
    b, t = pl.program_id(0), pl.program_id(1)

    @pl.when(t == 0)
    def _():
        o_ref[...] = x_ref[...]

    @pl.when(tv_ref[b * nt + t] > 0)
    def _():
        tb = x_ref.shape[0]
        slot = (t * MOE_TILE + lax.broadcasted_iota(I32, (tb, MOE_TILE), 1)).astype(F32)
        weight = (jnp.where(col_ref[:, 0:1] == slot, col_ref[:, 2:3], 0.0)
                  + jnp.where(col_ref[:, 1:2] == slot, col_ref[:, 3:4], 0.0))
        o_ref[...] += jnp.dot(weight.astype(BF16), ys_ref[...], preferred_element_type=F32)


def _combine(tile_valid, x, colinfo, ys, tb):
    m = x.shape[0]
    nblk = m // tb
    nt = _moe_tiles(tb)
    grid_spec = pltpu.PrefetchScalarGridSpec(
        num_scalar_prefetch=1,
        grid=(nblk, nt),
        in_specs=[pl.BlockSpec((tb, D_MODEL), lambda b, t, tv: (b, 0)),
                  pl.BlockSpec((tb, LANES), lambda b, t, tv: (b, 0)),
                  pl.BlockSpec((MOE_TILE, D_MODEL), lambda b, t, tv: (b * nt + t, 0))],
        out_specs=pl.BlockSpec((tb, D_MODEL), lambda b, t, tv: (b, 0)),
    )
    return pl.pallas_call(
        functools.partial(_combine_kernel, nt=nt),
        grid_spec=grid_spec,
        out_shape=jax.ShapeDtypeStruct((m, D_MODEL), F32),
        compiler_params=_cparams(("parallel", "arbitrary")),
        name="moe_combine",
    )(tile_valid, x, colinfo, ys)


def _moe_grouped(x, g, w_router, w_gate, w_up, w_down):
    m = x.shape[0]
    tb = min(MOE_BLOCK, m)
    nt = _moe_tiles(tb)
    hb, rowinfo, colinfo, meta = _route(x, g, w_router, tb)
    tile_expert = meta[:, 0, :nt].reshape(-1)
    tile_valid = meta[:, 1, :nt].reshape(-1)
    ys = _gffn(tile_expert, tile_valid, hb, rowinfo, w_gate, w_up, w_down, tb)
    return _combine(tile_valid, x, colinfo, ys, tb)


def _decmix_kernel(cb_ref, cc_ref, cx_ref, xs_ref, bc_ref, dt_ref, sc0_ref, sc1_ref,
                   sx0_ref, sx1_ref, sx2_ref, sb0_ref, sb1_ref, sb2_ref,
                   cw_ref, cwx_ref, cwb_ref, cbx_ref, cbb_ref, dtb_ref, alog_ref,
                   pa_ref, u_ref, xso_ref, bco_ref, dto_ref, ea_ref):
    u = cc_ref[...] * cx_ref[...]
    u_ref[...] = u
    pa_ref[...] = cb_ref[...] * (cw_ref[0:1, :] * sc0_ref[...] + cw_ref[1:2, :] * sc1_ref[...] + cw_ref[2:3, :] * u)
    xso_ref[...] = _silu(cwx_ref[0:1, :] * sx0_ref[...] + cwx_ref[1:2, :] * sx1_ref[...]
                         + cwx_ref[2:3, :] * sx2_ref[...] + cwx_ref[3:4, :] * xs_ref[...] + cbx_ref[...])
    bco_ref[...] = _silu(cwb_ref[0:1, :] * sb0_ref[...] + cwb_ref[1:2, :] * sb1_ref[...]
                         + cwb_ref[2:3, :] * sb2_ref[...] + cwb_ref[3:4, :] * bc_ref[...] + cbb_ref[...])
    dt = _softplus(dt_ref[...] + dtb_ref[...])
    dto_ref[...] = dt
    ea_ref[...] = jnp.exp(dt * (-jnp.exp(alog_ref[...])))


def _decmix(proj, st_conv, st_ssm_conv, conv_w, cw, cbias, dtb, alog):
    nb = proj.shape[0]
    pcol = lambda w, off: pl.BlockSpec((nb, w), lambda i: (0, off // w))
    full = lambda w: pl.BlockSpec((nb, w), lambda i: (0, 0))
    const = lambda r, w, j: pl.BlockSpec((r, w), lambda i: (0, j))
    sx = [st_ssm_conv[:, t, :512] for t in range(3)]
    sb = [st_ssm_conv[:, t, 512:] for t in range(3)]
    return pl.pallas_call(
        _decmix_kernel,
        grid=(1,),
        in_specs=[pcol(512, CB_OFF), pcol(512, CC_OFF), pcol(512, CX_OFF), pcol(512, XS_OFF), pcol(512, BC_OFF),
                  pcol(LANES, DT_OFF)] + [full(512)] * 8
                 + [const(3, 512, 0), const(4, 512, 0), const(4, 512, 1), const(1, 512, 0), const(1, 512, 1),
                    const(1, LANES, 0), const(1, LANES, 0)],
        out_specs=[full(512), full(512), full(512), full(512), full(LANES), full(LANES)],
        out_shape=[jax.ShapeDtypeStruct((nb, 512), F32)] * 4 + [jax.ShapeDtypeStruct((nb, LANES), F32)] * 2,
        compiler_params=_cparams(("arbitrary",)),
        name="decode_mix",
    )(proj, proj, proj, proj, proj, proj, st_conv[:, 0], st_conv[:, 1], *sx, *sb,
      conv_w, cw, cw, cbias, cbias, dtb, alog)


def _decssd_kernel(h_ref, dt_ref, xs_ref, b_ref, c_ref, ea_ref, dsk_ref, y_ref, ho_ref):
    xs = xs_ref[...]
    h_new = h_ref[...] * ea_ref[...] + (xs * dt_ref[...]) * b_ref[...]
    ho_ref[...] = h_new
    y_ref[...] = jnp.sum(h_new * c_ref[...], axis=-1, keepdims=True) + dsk_ref[...] * xs


def _decssd(h0, dt, xs, bh, ch, ea, dsk):
    nb = h0.shape[0]
    blk = lambda a, b: pl.BlockSpec((None, SSM_HEADS, a, b), lambda i: (i, 0, 0, 0))
    return pl.pallas_call(
        _decssd_kernel,
        grid=(nb,),
        in_specs=[blk(SSM_HEAD_DIM, SSM_STATE), blk(1, 1), blk(SSM_HEAD_DIM, 1), blk(1, SSM_STATE),
                  blk(1, SSM_STATE), blk(1, 1), pl.BlockSpec((SSM_HEADS, 1, 1), lambda i: (0, 0, 0))],
        out_specs=[blk(SSM_HEAD_DIM, 1), blk(SSM_HEAD_DIM, SSM_STATE)],
        out_shape=[jax.ShapeDtypeStruct((nb, SSM_HEADS, SSM_HEAD_DIM, 1), F32),
                   jax.ShapeDtypeStruct((nb, SSM_HEADS, SSM_HEAD_DIM, SSM_STATE), F32)],
        compiler_params=_cparams(("parallel",)),
        name="decode_ssd",
    )(h0, dt, xs, bh, ch, ea, dsk)


DECODE_PAGE_GROUP = 16


def _decscore_kernel(pt_ref, qi_ref, w_ref, *refs):
    del pt_ref
    kid_refs, o_ref = refs[:-1], refs[-1]
    kid = jnp.concatenate([r[...].astype(BF16) for r in kid_refs], axis=0)
    s = lax.dot_general(qi_ref[:, 0:IDX_DIM].astype(BF16), kid, NT_DIMS, preferred_element_type=F32)
    sc = jnp.maximum(s, 0.0) * w_ref[:, 0:1]
    o_ref[...] = jnp.sum(sc, axis=0, keepdims=True)


def _page_specs(block, layer, pg):
    zeros = (0,) * (len(block) - 2)
    return [pl.BlockSpec(block, lambda b, p, pt, t=t: (layer, pt[b, p * pg + t]) + zeros) for t in range(pg)]


def _decscore(page_table, qip, wrows, pool_kidx, layer):
    nb, n_pages = page_table.shape
    pg = math.gcd(DECODE_PAGE_GROUP, n_pages)
    grid_spec = pltpu.PrefetchScalarGridSpec(
        num_scalar_prefetch=1,
        grid=(nb, n_pages // pg),
        in_specs=[pl.BlockSpec((None, N_IDX_HEADS, LANES), lambda b, p, pt: (b, 0, 0)),
                  pl.BlockSpec((None, N_IDX_HEADS, PAGE_SIZE), lambda b, p, pt: (b, 0, 0))]
                 + _page_specs((None, None, PAGE_SIZE, IDX_DIM), layer, pg),
        out_specs=pl.BlockSpec((None, None, 1, pg * PAGE_SIZE), lambda b, p, pt: (b, p, 0, 0)),
    )
    out = pl.pallas_call(
        _decscore_kernel,
        grid_spec=grid_spec,
        out_shape=jax.ShapeDtypeStruct((nb, n_pages // pg, 1, pg * PAGE_SIZE), F32),
        compiler_params=_cparams(("parallel", "arbitrary")),
        name="decode_scores",
    )(page_table, qip, wrows, *([pool_kidx] * pg))
    return out.reshape(nb, n_pages * PAGE_SIZE)


def _decselect_kernel(sc_ref, qi_ref, ki_ref, w_ref, sel_ref, self_ref, keys_ref, *, k_sel, kc, nbits):
    nb, past = sc_ref.shape
    nck = past // kc
    lane_k = lax.broadcasted_iota(I32, (nb, kc), 1)
    ki = ki_ref[...].astype(F32)
    own = jnp.zeros((nb, 1), F32)
    for h in range(N_IDX_HEADS):
        s = jnp.sum(qi_ref[:, h * LANES:(h + 1) * LANES].astype(F32) * ki, axis=1, keepdims=True)
        own = own + jnp.maximum(s, 0.0) * w_ref[:, h:h + 1]
    own_key = _sortable(own)

    def key_body(c, carry):
        off = pl.multiple_of(c * kc, kc)
        keys_ref[:, pl.ds(off, kc)] = _sortable(sc_ref[:, pl.ds(off, kc)])
        return carry

    lax.fori_loop(0, nck, key_body, 0)

    def count_where(pred):
        def body(c, acc):
            off = pl.multiple_of(c * kc, kc)
            f = jnp.where(pred(keys_ref[:, pl.ds(off, kc)], off), 1.0, 0.0)
            part = f[:, 0:LANES]
            for t in range(1, kc // LANES):
                part = part + f[:, t * LANES:(t + 1) * LANES]
            return acc + part

        acc = lax.fori_loop(0, nck, body, jnp.zeros((nb, LANES), F32))
        return jnp.sum(acc, axis=1, keepdims=True)

    def bit_body(i, t):
        cand = t + lax.shift_left(jnp.int32(1), 31 - i)
        cnt = count_where(lambda kk, off: kk >= cand) + jnp.where(own_key >= cand, 1.0, 0.0)
        return jnp.where(cnt >= k_sel, cand, t)

    thr = lax.fori_loop(0, 32, bit_body, jnp.full((nb, 1), INT_MIN, I32))
    thr = jnp.maximum(thr, KEY_NEG_INF)
    finite_thr = thr > KEY_NEG_INF
    cnt_gt = count_where(lambda kk, off: kk > thr) + jnp.where(own_key > thr, 1.0, 0.0)
    need = k_sel - cnt_gt

    def xbody(i, x):
        cand = x + lax.shift_left(jnp.int32(1), nbits - 1 - i)
        cnt = count_where(lambda kk, off: jnp.logical_and(kk == thr, off + lane_k < cand))
        return jnp.where(cnt < need, cand, x)

    xcut = lax.fori_loop(0, nbits, xbody, jnp.zeros((nb, 1), I32))
    xcut = jnp.where(finite_thr, xcut, -1)
    ties_past = count_where(lambda kk, off: kk == thr)
    own_tie = jnp.logical_and(jnp.logical_and(own_key == thr, finite_thr), ties_past < need)
    self_ref[...] = jnp.broadcast_to(
        jnp.where(jnp.logical_or(own_key > thr, own_tie), 1.0, 0.0), (nb, LANES))

    def sel_body(c, carry):
        off = pl.multiple_of(c * kc, kc)
        kk = keys_ref[:, pl.ds(off, kc)]
        sel = jnp.logical_or(kk > thr, jnp.logical_and(kk == thr, off + lane_k <= xcut))
        sel_ref[:, pl.ds(off, kc)] = jnp.where(sel, 1.0, 0.0)
        return carry

    lax.fori_loop(0, nck, sel_body, 0)


def _decselect(scores, qip, kib, wi):
    nb, past = scores.shape
    k_sel = min(TOPK_MAX, (past + 1) // 4)
    kc = 512 if past % 512 == 0 else PAGE_SIZE
    nbits = max(1, past.bit_length())
    full = lambda w: pl.BlockSpec((nb, w), lambda i: (0, 0))
    return pl.pallas_call(
        functools.partial(_decselect_kernel, k_sel=k_sel, kc=kc, nbits=nbits),
        grid=(1,),
        in_specs=[full(past), full(N_IDX_HEADS * LANES), full(LANES), full(LANES)],
        out_specs=[full(past), full(LANES)],
        out_shape=[jax.ShapeDtypeStruct((nb, past), F32), jax.ShapeDtypeStruct((nb, LANES), F32)],
        scratch_shapes=[pltpu.VMEM((nb, past), I32)],
        compiler_params=_cparams(("arbitrary",)),
        name="decode_select",
    )(scores, qip, kib, wi)


def _decattn_kernel(pt_ref, qp_ref, sel_ref, kn_ref, vn_ref, self_ref, *refs, n_steps):
    del pt_ref
    pg = (len(refs) - 4) // 2
    kp_refs, vp_refs = refs[:pg], refs[pg:2 * pg]
    o_ref, m_ref, l_ref, acc_ref = refs[2 * pg:]
    p = pl.program_id(1)

    @pl.when(p == 0)
    def _():
        m_ref[...] = jnp.full(m_ref.shape, NEG_BIG, F32)
        l_ref[...] = jnp.zeros(l_ref.shape, F32)
        acc_ref[...] = jnp.zeros(acc_ref.shape, F32)

    q = qp_ref[...]
    kcat = jnp.concatenate([r[...].astype(BF16) for r in kp_refs], axis=0)
    vcat = jnp.concatenate([r[...].astype(BF16) for r in vp_refs], axis=0)
    s = lax.dot_general(q.astype(BF16), kcat, NT_DIMS, preferred_element_type=F32)
    s = jnp.where(sel_ref[...] > 0.0, s, -jnp.inf)
    m_old = m_ref[:, 0:1]
    m_new = jnp.maximum(m_old, jnp.max(s, axis=1, keepdims=True))
    alpha = jnp.exp(m_old - m_new)
    pr = jnp.exp(s - m_new)
    l_new = alpha * l_ref[:, 0:1] + jnp.sum(pr, axis=1, keepdims=True)
    acc_ref[...] = alpha * acc_ref[...] + jnp.dot(pr.astype(BF16), vcat, preferred_element_type=F32)
    m_ref[...] = jnp.broadcast_to(m_new, m_ref.shape)
    l_ref[...] = jnp.broadcast_to(l_new, l_ref.shape)

    @pl.when(p == n_steps - 1)
    def _():
        s_own = jnp.sum(q * kn_ref[...], axis=1, keepdims=True)
        s_own = jnp.where(self_ref[:, 0:1] > 0.0, s_own, -jnp.inf)
        m_o = m_ref[:, 0:1]
        m_n = jnp.maximum(m_o, s_own)
        al = jnp.exp(m_o - m_n)
        p_own = jnp.exp(s_own - m_n)
        l_n = al * l_ref[:, 0:1] + p_own
        acc = al * acc_ref[...] + p_own.astype(BF16).astype(F32) * vn_ref[...]
        o_ref[...] = acc / l_n


def _decattn(page_table, qp, sel, pool_k, pool_v, kb, vb, self_sel, layer):
    nb, n_pages = page_table.shape
    pg = math.gcd(DECODE_PAGE_GROUP, n_pages)
    n_steps = n_pages // pg
    row = lambda r: pl.BlockSpec((None, r, LANES), lambda b, p, pt: (b, 0, 0))
    pages = _page_specs((None, None, PAGE_SIZE, LANES), layer, pg)
    grid_spec = pltpu.PrefetchScalarGridSpec(
        num_scalar_prefetch=1,
        grid=(nb, n_steps),
        in_specs=[row(N_HEADS),
                  pl.BlockSpec((None, None, 1, pg * PAGE_SIZE), lambda b, p, pt: (b, p, 0, 0)),
                  row(1), row(1), row(1)] + pages + pages,
        out_specs=row(N_HEADS),
        scratch_shapes=[pltpu.VMEM((N_HEADS, LANES), F32)] * 3,
    )
    return pl.pallas_call(
        functools.partial(_decattn_kernel, n_steps=n_steps),
        grid_spec=grid_spec,
        out_shape=jax.ShapeDtypeStruct((nb, N_HEADS, LANES), F32),
        compiler_params=_cparams(("parallel", "arbitrary")),
        name="decode_attend",
    )(page_table, qp, sel.reshape(nb, n_steps, 1, pg * PAGE_SIZE), kb, vb, self_sel,
      *([pool_k] * pg), *([pool_v] * pg))


def _pack_w_in(w):
    d = w.shape[0]
    pad = lambda n: jnp.zeros((d, n), w.dtype)
    cols = [w[:, 4172:7244],
            w[:, 0:2048],
            w[:, 2628:3140],
            w[:, 3140:4164],
            w[:, 2048:2304],
            w[:, 2304:2560],
            w[:, 2560:2628], pad(60),
            w[:, 4164:4172], pad(120),
            pad(N_PROJ - 7424)]
    return jnp.concatenate(cols, axis=1).astype(BF16)


def _pad_lanes(v):
    return jnp.zeros((1, LANES), F32).at[0, :v.shape[0]].set(v)


def _rope_tables(pos):
    half = HEAD_DIM // 2
    inv = ROPE_THETA ** (-jnp.arange(half, dtype=F32) / half)
    ang = pos.astype(F32)[:, None] * inv[None, :]
    cos, sin = jnp.cos(ang), jnp.sin(ang)
    cos2 = jnp.concatenate([cos, cos], axis=1)
    sin2 = jnp.concatenate([-sin, sin], axis=1)
    return jnp.tile(cos2, (1, 2)), jnp.tile(sin2, (1, 2))


def _head_halves(o):
    hpg = N_HEADS // N_KV_HEADS
    parts = [o[:, h, (h // hpg) * HEAD_DIM:(h // hpg + 1) * HEAD_DIM] for h in range(N_HEADS)]
    return jnp.concatenate(parts, axis=-1)


def kernel(x_prompt, x_sample, cache_k, cache_v, cache_kidx, state_conv, state_ssm_conv, state_ssm, page_table,
           norm1, w_in, b_gate, conv_w, q_norm, k_norm, w_pc, w_pa, ssm_conv_w, ssm_conv_b, dt_bias, a_log,
           d_skip, ssm_norm, w_ps, w_o, norm2, w_gate_dense, w_up_dense, w_down_dense, w_router, w_gate_moe,
           w_up_moe, w_down_moe):
    nb, seq, _ = x_prompt.shape
    db = x_sample.shape[0]
    depth = w_in.shape[0]
    n_pages = page_table.shape[1]
    past = n_pages * PAGE_SIZE

    hp = x_prompt.reshape(nb * seq, D_MODEL)
    hs = x_sample.reshape(db, D_MODEL)
    cos_p, sin_p = _rope_tables(jnp.tile(jnp.arange(seq), nb))
    cos_s, sin_s = _rope_tables(jnp.full((db,), past))

    pool_k = cache_k.reshape(*cache_k.shape[:3], LANES)
    pool_v = cache_v.reshape(*cache_v.shape[:3], LANES)

    outs_p = [[] for _ in range(6)]
    outs_s = [[] for _ in range(6)]
    for l in range(depth):
        wp = _pack_w_in(w_in[l])
        g1 = norm1[l].reshape(1, D_MODEL)
        g2 = norm2[l].reshape(1, D_MODEL)
        qg = jnp.tile(q_norm[l], 2).reshape(1, LANES)
        kg = jnp.tile(k_norm[l], 2).reshape(1, LANES)
        cw = ssm_conv_w[l]
        cbias = ssm_conv_b[l].reshape(1, SSM_CONV_DIM)
        dtb = _pad_lanes(dt_bias[l])
        alog = _pad_lanes(a_log[l])
        dsk = jnp.repeat(d_skip[l], SSM_HEAD_DIM).reshape(1, SSM_INNER)
        sn = ssm_norm[l].reshape(1, SSM_INNER)
        wpc, wpa, wps, wo = (w.astype(BF16) for w in (w_pc[l], w_pa[l], w_ps[l], w_o[l]))
        i = l // 2
        if l % 2 == 0:
            routed = False
            wr = jnp.zeros((D_MODEL, LANES), F32)
            wg, wu, wd = (w[i:i + 1].astype(BF16) for w in (w_gate_dense, w_up_dense, w_down_dense))
        else:
            routed = True
            wr = jnp.zeros((D_MODEL, LANES), F32).at[:, :N_EXPERTS].set(w_router[i])
            wg, wu, wd = (w[i].astype(BF16) for w in (w_gate_moe, w_up_moe, w_down_moe))

        proj = _inproj(hp, g1, wp)
        qt, kf, kb, vt, qit, kif, kib, wt = _prep(proj, cos_p, sin_p, qg, kg, True)
        attn = _dsa_prompt(qt, qit, wt, kb, vt, kib, nb, seq)
        pre_a, conv_st = _conva_prompt(proj, conv_w[l], nb, seq)
        y_ssd, h_last = _ssd_prompt(proj, cw, cbias, dtb, alog, dsk, nb, seq)
        hp = _merge(hp, pre_a, attn, y_ssd, proj, sn, b_gate[l], wpc, wpa, wps, wo)
        if routed and hp.shape[0] % MOE_TILE == 0:
            hp = _moe_grouped(hp, g2, wr, wg, wu, wd)
        else:
            hp = _ffn(hp, g2, wr, wg, wu, wd, routed)
        proj3 = proj.reshape(nb, seq, N_PROJ)
        outs_p[0].append(kf.reshape(nb, seq, N_KV_HEADS, HEAD_DIM))
        outs_p[1].append(proj3[:, :, V_OFF:V_OFF + LANES].reshape(nb, seq, N_KV_HEADS, HEAD_DIM))
        outs_p[2].append(kif[:, :IDX_DIM].reshape(nb, seq, IDX_DIM))
        outs_p[3].append(conv_st)
        outs_p[4].append(proj3[:, seq - 3:, XS_OFF:XS_OFF + SSM_CONV_DIM])
        outs_p[5].append(h_last.reshape(nb, SSM_HEADS, SSM_HEAD_DIM, SSM_STATE))

        proj = _inproj(hs, g1, wp)
        qp, kf, kb, vb, qip, kif, kib, wi = _prep(proj, cos_s, sin_s, qg, kg, False)
        wrows = jnp.broadcast_to(wi[:, :N_IDX_HEADS, None], (db, N_IDX_HEADS, PAGE_SIZE))
        scores = _decscore(page_table, qip.astype(F32).reshape(db, N_IDX_HEADS, LANES), wrows, cache_kidx, l)
        sel, self_sel = _decselect(scores, qip, kib, wi)
        o = _decattn(page_table, qp.astype(F32).reshape(db, N_HEADS, LANES), sel, pool_k, pool_v,
                     kb.astype(F32).reshape(db, 1, LANES), vb.astype(F32).reshape(db, 1, LANES),
                     self_sel.reshape(db, 1, LANES), l)
        attn = _head_halves(o)
        pre_a, u, xs, bcv, dt, ea = _decmix(proj, state_conv[l], state_ssm_conv[l], conv_w[l], cw, cbias, dtb, alog)
        hpg = SSM_HEADS // SSM_GROUPS
        xs4 = xs.reshape(db, SSM_HEADS, SSM_HEAD_DIM, 1)
        bh = jnp.repeat(bcv[:, :SSM_GROUPS * SSM_STATE].reshape(db, SSM_GROUPS, 1, SSM_STATE), hpg, axis=1)
        ch = jnp.repeat(bcv[:, SSM_GROUPS * SSM_STATE:].reshape(db, SSM_GROUPS, 1, SSM_STATE), hpg, axis=1)
        y4, h_new = _decssd(state_ssm[l], dt[:, :SSM_HEADS, None, None], xs4, bh, ch, ea[:, :SSM_HEADS, None, None],
                            d_skip[l].reshape(SSM_HEADS, 1, 1))
        hs = _merge(hs, pre_a, attn, y4.reshape(db, SSM_INNER), proj, sn, b_gate[l], wpc, wpa, wps, wo)
        hs = _ffn(hs, g2, wr, wg, wu, wd, routed)
        outs_s[0].append(kf.reshape(db, 1, N_KV_HEADS, HEAD_DIM))
        outs_s[1].append(proj[:, V_OFF:V_OFF + LANES].reshape(db, 1, N_KV_HEADS, HEAD_DIM))
        outs_s[2].append(kif[:, :IDX_DIM].reshape(db, 1, IDX_DIM))
        outs_s[3].append(jnp.stack([state_conv[l][:, 1], u], axis=1))
        outs_s[4].append(jnp.concatenate(
            [state_ssm_conv[l][:, 1:], proj[:, None, XS_OFF:XS_OFF + SSM_CONV_DIM]], axis=1))
        outs_s[5].append(h_new)

    return (hp.reshape(nb, seq, D_MODEL), hs.reshape(db, 1, D_MODEL),
            *(jnp.stack(o) for o in outs_p), *(jnp.stack(o) for o in outs_s))
```

```python
import functools
import math

import jax
import jax.numpy as jnp
from jax import lax
from jax.experimental import pallas as pl
from jax.experimental.pallas import tpu as pltpu

F32 = jnp.float32
BF16 = jnp.bfloat16
I32 = jnp.int32

D_MODEL = 1024
D_CONV = 512
N_HEADS = 8
HEAD_DIM = 64
N_KV_HEADS = 2
N_IDX_HEADS = 4
IDX_DIM = 64
TOPK_MAX = 256
Q_BLOCK = 128
ROPE_THETA = 10000.0
SSM_INNER = 512
SSM_HEADS = 8
SSM_HEAD_DIM = 64
SSM_GROUPS = 2
SSM_STATE = 128
SSM_CHUNK = 128
SSM_CONV_DIM = 1024
PAGE_SIZE = 128
N_EXPERTS = 8
EPS = 1e-6

LANES = 128
SUBLANES = 8
VMEM_LIMIT = 52 * 1024 * 1024

G_OFF, CB_OFF, CC_OFF, CX_OFF, Q_OFF, Z_OFF, XS_OFF, BC_OFF = 0, 3072, 3584, 4096, 4608, 5120, 5632, 6144
K_OFF, V_OFF, QI_OFF, KW_OFF, DT_OFF = 6656, 6784, 6912, 7168, 7296
N_PROJ = 7680

INT_MIN = -2 ** 31
INT_MAX = 2 ** 31 - 1
KEY_NEG_INF = (-8388608) ^ 0x7FFFFFFF
NEG_BIG = -1e30
LOG2_E = 1.4426950408889634

NT_DIMS = (((1,), (1,)), ((), ()))


def _cparams(sem):
    return pltpu.CompilerParams(dimension_semantics=sem, vmem_limit_bytes=VMEM_LIMIT)


def _sigmoid(x):
    return 1.0 / (1.0 + jnp.exp(-x))


def _silu(x):
    return x * _sigmoid(x)


def _softplus(x):
    return jnp.maximum(x, 0.0) + jnp.log1p(jnp.exp(-jnp.abs(x)))


def _sortable(x):
    bits = pltpu.bitcast(x, I32)
    return bits ^ (lax.shift_right_arithmetic(bits, 31) & 0x7FFFFFFF)


def _rms(x, g):
    return x * lax.rsqrt(jnp.mean(x * x, axis=-1, keepdims=True) + EPS) * g


def _inproj_kernel(x_ref, g_ref, w_ref, o_ref, h_ref):
    @pl.when(pl.program_id(1) == 0)
    def _():
        h_ref[...] = _rms(x_ref[...], g_ref[...]).astype(BF16)

    o_ref[...] = jnp.dot(h_ref[...], w_ref[...], preferred_element_type=F32)


def _inproj(x, g, w):
    m = x.shape[0]
    tm = min(m, 1024)
    tn = 1536
    return pl.pallas_call(
        _inproj_kernel,
        grid=(m // tm, N_PROJ // tn),
        in_specs=[pl.BlockSpec((tm, D_MODEL), lambda i, j: (i, 0)),
                  pl.BlockSpec((1, D_MODEL), lambda i, j: (0, 0)),
                  pl.BlockSpec((D_MODEL, tn), lambda i, j: (0, j))],
        out_specs=pl.BlockSpec((tm, tn), lambda i, j: (i, j)),
        out_shape=jax.ShapeDtypeStruct((m, N_PROJ), F32),
        scratch_shapes=[pltpu.VMEM((tm, D_MODEL), BF16)],
        compiler_params=_cparams(("parallel", "arbitrary")),
        name="inproj",
    )(x, g, w)


V_T_ROWS = LANES + 16


def _prep_kernel(q_ref, k_ref, v_ref, qi_ref, kw_ref, cos_ref, sin_ref, qg_ref, kg_ref,
                 qp_ref, kf_ref, kb_ref, vb_ref, qip_ref, kif_ref, kib_ref, wo_ref, *, transposed):
    tm = cos_ref.shape[0]

    def put(ref, h, val):
        if transposed:
            vt = val.T.astype(BF16)
            for blk in range(tm // Q_BLOCK):
                ref[blk, h * LANES:(h + 1) * LANES, :] = vt[:, blk * Q_BLOCK:(blk + 1) * Q_BLOCK]
        else:
            ref[:, h * LANES:(h + 1) * LANES] = val.astype(BF16)

    cos = cos_ref[...]
    sin = sin_ref[...]
    lane = lax.broadcasted_iota(I32, (tm, LANES), 1)
    low_half = lane < HEAD_DIM
    first_rot = (lane % HEAD_DIM) < (HEAD_DIM // 2)
    r = lax.broadcasted_iota(I32, (LANES, LANES), 0) // HEAD_DIM
    c = lax.broadcasted_iota(I32, (LANES, LANES), 1) // HEAD_DIM
    seg = jnp.where(r == c, 1.0 / HEAD_DIM, 0.0).astype(BF16)

    def rope(x):
        fwd = pltpu.roll(x, LANES - HEAD_DIM // 2, 1)
        bwd = pltpu.roll(x, HEAD_DIM // 2, 1)
        return x * cos + jnp.where(first_rot, fwd, bwd) * sin

    def headnorm(x, g):
        s = x * x
        hi = s.astype(BF16)
        lo = (s - hi.astype(F32)).astype(BF16)
        ms = (jnp.dot(hi, seg, preferred_element_type=F32)
              + jnp.dot(lo, seg, preferred_element_type=F32))
        return x * lax.rsqrt(ms + EPS) * g

    q_scale = HEAD_DIM ** -0.5 * (LOG2_E if transposed else 1.0)
    qg = qg_ref[...]
    for s in range(N_HEADS // 2):
        slab = rope(headnorm(q_ref[:, s * LANES:(s + 1) * LANES], qg)) * q_scale
        swapped = pltpu.roll(slab, HEAD_DIM, 1)
        grp = (2 * s) // (N_HEADS // N_KV_HEADS)
        for hh in range(2):
            h = 2 * s + hh
            src = slab if hh == grp else swapped
            keep = low_half if grp == 0 else jnp.logical_not(low_half)
            put(qp_ref, h, jnp.where(keep, src, 0.0))

    k = rope(headnorm(k_ref[...], kg_ref[...]))
    kf_ref[...] = k
    kb_ref[...] = k.astype(BF16)
    if transposed:
        vb_ref[0:LANES, :] = v_ref[...].T.astype(BF16)
        vb_ref[LANES:V_T_ROWS, :] = jnp.ones((V_T_ROWS - LANES, tm), BF16)
    else:
        vb_ref[...] = v_ref[...].astype(BF16)

    for s in range(N_IDX_HEADS // 2):
        slab = rope(qi_ref[:, s * LANES:(s + 1) * LANES]) * (IDX_DIM ** -0.5)
        swapped = pltpu.roll(slab, HEAD_DIM, 1)
        for hh in range(2):
            h = 2 * s + hh
            src = slab if hh == 0 else swapped
            put(qip_ref, h, jnp.where(low_half, src, 0.0))

    kw = kw_ref[...]
    ki = rope(kw)
    kif_ref[...] = ki
    kib_ref[...] = jnp.where(low_half, ki, 0.0).astype(BF16)
    wi = pltpu.roll(kw, HEAD_DIM, 1) * (N_IDX_HEADS ** -0.5)
    if transposed:
        wt = wi.T[0:SUBLANES, :]
        for blk in range(tm // Q_BLOCK):
            wo_ref[blk] = wt[:, blk * Q_BLOCK:(blk + 1) * Q_BLOCK]
    else:
        wo_ref[...] = wi


def _prep(proj, cos, sin, qg, kg, transposed):
    m = proj.shape[0]
    tm = min(m, 512)
    row = lambda w, off: pl.BlockSpec((tm, w), lambda i: (i, off // w))
    full = lambda w: pl.BlockSpec((tm, w), lambda i: (i, 0))
    const = pl.BlockSpec((1, LANES), lambda i: (0, 0))
    if transposed:
        qblocks = tm // Q_BLOCK
        feat = lambda r: pl.BlockSpec((qblocks, r, Q_BLOCK), lambda i: (i, 0, 0))
        fshape = lambda r, dt: jax.ShapeDtypeStruct((m // Q_BLOCK, r, Q_BLOCK), dt)
        vspec = pl.BlockSpec((V_T_ROWS, tm), lambda i: (0, i))
        vshape = jax.ShapeDtypeStruct((V_T_ROWS, m), BF16)
    else:
        feat = full
        fshape = lambda r, dt: jax.ShapeDtypeStruct((m, r), dt)
        vspec = full(LANES)
        vshape = jax.ShapeDtypeStruct((m, LANES), BF16)
    return pl.pallas_call(
        functools.partial(_prep_kernel, transposed=transposed),
        grid=(m // tm,),
        in_specs=[row(512, Q_OFF), row(LANES, K_OFF), row(LANES, V_OFF), row(256, QI_OFF), row(LANES, KW_OFF),
                  full(LANES), full(LANES), const, const],
        out_specs=[feat(N_HEADS * LANES), full(LANES), full(LANES), vspec,
                   feat(N_IDX_HEADS * LANES), full(LANES), full(LANES), feat(SUBLANES if transposed else LANES)],
        out_shape=[fshape(N_HEADS * LANES, BF16),
                   jax.ShapeDtypeStruct((m, LANES), F32),
                   jax.ShapeDtypeStruct((m, LANES), BF16),
                   vshape,
                   fshape(N_IDX_HEADS * LANES, BF16),
                   jax.ShapeDtypeStruct((m, LANES), F32),
                   jax.ShapeDtypeStruct((m, LANES), BF16),
                   fshape(SUBLANES if transposed else LANES, F32)],
        compiler_params=_cparams(("parallel",)),
        name="prep",
    )(proj, proj, proj, proj, proj, cos, sin, qg, kg)


def _fold_rows(f):
    while f.shape[0] > SUBLANES:
        half = f.shape[0] // 2
        f = f[:half] + f[half:]
    return f


def _dsa_kernel(qt_ref, qit_ref, wt_ref, k_ref, vt_ref, ki_ref, o_ref,
                keys_ref, tpos_ref, x_ref, m_ref, kmax_ref, acc_ref, *, k_sel, kc, nbits):
    qb = Q_BLOCK
    n_keys = k_ref.shape[0]
    assert n_keys < 2 ** 22
    j = pl.program_id(1)
    nck = (j * qb + qb + kc - 1) // kc
    qpos = j * qb + lax.broadcasted_iota(I32, (1, qb), 1)
    sub_k = lax.broadcasted_iota(I32, (kc, qb), 0)
    pair = lambda ref, s: jnp.concatenate(
        [ref[(2 * s) * LANES:(2 * s + 1) * LANES, :], ref[(2 * s + 1) * LANES:(2 * s + 2) * LANES, :]], axis=1)
    qi2 = [pair(qit_ref, s) for s in range(N_IDX_HEADS // 2)]
    q2 = [pair(qt_ref, s) for s in range(N_HEADS // 2)]

    def score_body(c, carry):
        off = pl.multiple_of(c * kc, kc)
        kic = ki_ref[pl.ds(off, kc), :]
        acc = jnp.zeros((kc, qb), F32)
        for s in range(N_IDX_HEADS // 2):
            s2 = jnp.dot(kic, qi2[s], preferred_element_type=F32)
            for hh in range(2):
                h = 2 * s + hh
                acc = acc + jnp.maximum(s2[:, hh * qb:(hh + 1) * qb], 0.0) * wt_ref[h:h + 1, :]
        kpos = off + sub_k
        key = _sortable(jnp.where(kpos <= qpos, acc, -jnp.inf))
        key = jnp.where(key == 0, (n_keys - 1) - kpos, jnp.where(key == -1, -1 - kpos, key))
        keys_ref[pl.ds(off, kc), :] = key
        return carry

    lax.fori_loop(0, nck, score_body, 0)

    def count_where(pred, ref=keys_ref):
        def body(c, acc):
            off = pl.multiple_of(c * kc, kc)
            return acc + _fold_rows(jnp.where(pred(ref[pl.ds(off, kc), :], off), 1.0, 0.0))

        acc = lax.fori_loop(0, nck, body, jnp.zeros((SUBLANES, qb), F32))
        return jnp.sum(acc, axis=0, keepdims=True)

    def bit_body(i, carry):
        t, cnt_acc, cnt_rej = carry
        cand = t + lax.shift_left(jnp.int32(1), 31 - i)
        cnt = count_where(lambda kk, off: kk >= cand)
        ok = cnt >= k_sel
        return jnp.where(ok, cand, t), jnp.where(ok, cnt, cnt_acc), jnp.where(ok, cnt_rej, cnt)

    zero = jnp.zeros((1, qb), F32)
    thr, cnt_ge, cnt_gt = lax.fori_loop(0, 32, bit_body, (jnp.full((1, qb), INT_MIN, I32), zero, zero))
    thr = jnp.maximum(thr, KEY_NEG_INF)
    finite_thr = thr > KEY_NEG_INF
    need = k_sel - cnt_gt
    tie = jnp.logical_and(cnt_ge > k_sel, finite_thr)
    x_default = jnp.where(finite_thr, INT_MAX, -1)
    x_ref[...] = jnp.broadcast_to(x_default, x_ref.shape)

    @pl.when(jnp.max(jnp.where(tie, 1.0, 0.0)) > 0.0)
    def _():
        def tie_pos_body(c, carry):
            off = pl.multiple_of(c * kc, kc)
            tpos_ref[pl.ds(off, kc), :] = jnp.where(keys_ref[pl.ds(off, kc), :] == thr, off + sub_k, INT_MAX)
            return carry

        lax.fori_loop(0, nck, tie_pos_body, 0)

        def xbody(i, x):
            cand = x + lax.shift_left(jnp.int32(1), nbits - 1 - i)
            cnt = count_where(lambda tp, off: tp < cand, tpos_ref)
            return jnp.where(cnt < need, cand, x)

        x = lax.fori_loop(0, nbits, xbody, jnp.zeros((1, qb), I32))
        x_ref[...] = jnp.broadcast_to(jnp.where(tie, x, x_default), x_ref.shape)

    xcut = x_ref[0:1, :]
    hpg = N_HEADS // N_KV_HEADS

    @pl.when(j == 0)
    def _():
        r = lax.broadcasted_iota(I32, (LANES, LANES), 0) // HEAD_DIM
        cc = lax.broadcasted_iota(I32, (LANES, LANES), 1) // HEAD_DIM
        seg = jnp.where(r == cc, 1.0, 0.0).astype(BF16)

        def body(c, mx):
            kf = k_ref[pl.ds(pl.multiple_of(c * kc, kc), kc), :].astype(F32)
            gs = jnp.dot((kf * kf).astype(BF16), seg, preferred_element_type=F32)
            while gs.shape[0] > SUBLANES:
                half = gs.shape[0] // 2
                gs = jnp.maximum(gs[:half], gs[half:])
            return jnp.maximum(mx, gs)

        mx = lax.fori_loop(0, k_ref.shape[0] // kc, body, jnp.zeros((SUBLANES, LANES), F32))
        mx = jnp.max(mx, axis=0, keepdims=True)
        lane = lax.broadcasted_iota(I32, (1, LANES), 1)
        other = pltpu.roll(mx, HEAD_DIM, 1)
        kmax_ref[0:1, :] = jnp.where(lane < HEAD_DIM, mx, other)
        kmax_ref[1:2, :] = jnp.where(lane < HEAD_DIM, other, mx)

    def chunk(c):
        off = pl.multiple_of(c * kc, kc)
        kk = keys_ref[pl.ds(off, kc), :]
        sel = jnp.logical_or(kk > thr, jnp.logical_and(kk == thr, off + sub_k <= xcut))
        return sel, k_ref[pl.ds(off, kc), :], vt_ref[:, pl.ds(off, kc)]

    bound = []
    for h in range(N_HEADS):
        qh = qt_ref[h * LANES:(h + 1) * LANES, :].astype(F32)
        qsq = jnp.sum(qh * qh, axis=0, keepdims=True)
        bound.append(jnp.sqrt(qsq * kmax_ref[h // hpg:h // hpg + 1, :]))
    acc_ref[...] = jnp.zeros(acc_ref.shape, F32)

    def fast_body(c, carry):
        sel, kch, vch = chunk(c)
        n_pair = N_HEADS // 2
        s2 = jnp.dot(kch, q2[0], preferred_element_type=F32)
        for s in range(n_pair):
            s2_next = jnp.dot(kch, q2[s + 1], preferred_element_type=F32) if s + 1 < n_pair else None
            ps = [jnp.exp2(jnp.where(sel, s2[:, hh * qb:(hh + 1) * qb] - bound[2 * s + hh], -jnp.inf)).astype(BF16)
                  for hh in range(2)]
            acc_ref[s] += jnp.dot(vch, jnp.concatenate(ps, axis=1), preferred_element_type=F32)
            s2 = s2_next
        return carry

    lax.fori_loop(0, nck, fast_body, 0)

    dens = jnp.concatenate([acc_ref[s, LANES:LANES + 1, :] for s in range(N_HEADS // 2)], axis=0)

    @pl.when(jnp.logical_not(jnp.min(dens) >= 1e-20))
    def _():
        m_ref[...] = jnp.full(m_ref.shape, NEG_BIG, F32)
        acc_ref[...] = jnp.zeros(acc_ref.shape, F32)

        def exact_body(c, carry):
            sel, kch, vch = chunk(c)
            for s in range(N_HEADS // 2):
                s2 = jnp.dot(kch, q2[s], preferred_element_type=F32)
                ps, alphas = [], []
                for hh in range(2):
                    h = 2 * s + hh
                    sh = jnp.where(sel, s2[:, hh * qb:(hh + 1) * qb], -jnp.inf)
                    m_old = m_ref[h:h + 1, :]
                    m_new = jnp.maximum(m_old, jnp.max(sh, axis=0, keepdims=True))
                    m_ref[h:h + 1, :] = m_new
                    alphas.append(jnp.exp2(m_old - m_new))
                    ps.append(jnp.exp2(sh - m_new).astype(BF16))
                o2 = jnp.dot(vch, jnp.concatenate(ps, axis=1), preferred_element_type=F32)
                acc_ref[s] = jnp.concatenate(alphas, axis=1) * acc_ref[s] + o2
            return carry

        lax.fori_loop(0, nck, exact_body, 0)

    for s in range(N_HEADS // 2):
        g = (2 * s) // hpg
        num = acc_ref[s, g * HEAD_DIM:(g + 1) * HEAD_DIM, :]
        out2 = num / acc_ref[s, LANES:LANES + 1, :]
        o_ref[:, s * LANES:(s + 1) * LANES] = jnp.concatenate([out2[:, :qb], out2[:, qb:]], axis=0).T


def _dsa_prompt(qt, qit, wt, kb, vt, kib, nb, seq):
    k_sel = min(TOPK_MAX, seq // 4)
    kc = next(c for c in (512, Q_BLOCK) if seq % c == 0)
    nbits = max(1, (seq - 1).bit_length())
    nq = seq // Q_BLOCK
    r3 = lambda a: a.reshape(nb, seq, a.shape[-1])
    qcol = lambda r: pl.BlockSpec((None, r, Q_BLOCK), lambda b, j: (b * nq + j, 0, 0))
    seqblk = pl.BlockSpec((None, seq, LANES), lambda b, j: (b, 0, 0))
    out = pl.pallas_call(
        functools.partial(_dsa_kernel, k_sel=k_sel, kc=kc, nbits=nbits),
        grid=(nb, nq),
        in_specs=[qcol(N_HEADS * LANES), qcol(N_IDX_HEADS * LANES), qcol(SUBLANES), seqblk,
                  pl.BlockSpec((V_T_ROWS, seq), lambda b, j: (0, b)), seqblk],
        out_specs=pl.BlockSpec((None, Q_BLOCK, N_HEADS * HEAD_DIM), lambda b, j: (b, j, 0)),
        out_shape=jax.ShapeDtypeStruct((nb, seq, N_HEADS * HEAD_DIM), F32),
        scratch_shapes=[pltpu.VMEM((seq, Q_BLOCK), I32),
                        pltpu.VMEM((seq, Q_BLOCK), I32),
                        pltpu.VMEM((SUBLANES, Q_BLOCK), I32),
                        pltpu.VMEM((N_HEADS, Q_BLOCK), F32),
                        pltpu.VMEM((SUBLANES, LANES), F32),
                        pltpu.VMEM((N_HEADS // 2, V_T_ROWS, 2 * Q_BLOCK), F32)],
        compiler_params=_cparams(("arbitrary", "arbitrary")),
        name="dsa_prompt",
    )(qt, qit, wt, r3(kb), vt, r3(kib))
    return out.reshape(nb * seq, N_HEADS * HEAD_DIM)


def _conva_kernel(cb_ref, cc_ref, cx_ref, cch_ref, cxh_ref, w_ref, o_ref, st_ref, *, tiles_per_seq):
    tm = cb_ref.shape[0]
    first = (pl.program_id(0) % tiles_per_seq) == 0
    u = cc_ref[...] * cx_ref[...]
    uh = jnp.where(first, 0.0, cch_ref[...] * cxh_ref[...])
    ext = jnp.concatenate([uh, u], axis=0)
    conv = (w_ref[2:3, :] * u + w_ref[1:2, :] * ext[SUBLANES - 1:SUBLANES - 1 + tm]
            + w_ref[0:1, :] * ext[SUBLANES - 2:SUBLANES - 2 + tm])
    o_ref[...] = cb_ref[...] * conv
    st_ref[...] = u[tm - 2:tm, :]


def _conva_prompt(proj, conv_w, nb, seq):
    m = proj.shape[0]
    tm = min(seq, 512)
    tps = seq // tm
    col = lambda off: pl.BlockSpec((tm, D_CONV), lambda i: (i, off // D_CONV))
    halo = lambda off: pl.BlockSpec(
        (SUBLANES, D_CONV), lambda i: (jnp.maximum(i * (tm // SUBLANES) - 1, 0), off // D_CONV))
    return pl.pallas_call(
        functools.partial(_conva_kernel, tiles_per_seq=tps),
        grid=(m // tm,),
        in_specs=[col(CB_OFF), col(CC_OFF), col(CX_OFF), halo(CC_OFF), halo(CX_OFF),
                  pl.BlockSpec((3, D_CONV), lambda i: (0, 0))],
        out_specs=[pl.BlockSpec((tm, D_CONV), lambda i: (i, 0)),
                   pl.BlockSpec((None, 2, D_CONV), lambda i: (i // tps, 0, 0))],
        out_shape=[jax.ShapeDtypeStruct((m, D_CONV), F32),
                   jax.ShapeDtypeStruct((nb, 2, D_CONV), F32)],
        compiler_params=_cparams(("arbitrary",)),
        name="conva_prompt",
    )(proj, proj, proj, proj, proj, conv_w)


def _ssd_kernel(xs_ref, bc_ref, dt_ref, xh_ref, bh_ref, cwx_ref, cwb_ref, cbx_ref, cbb_ref,
                dtb_ref, alog_ref, dsk_ref, y_ref, hout_ref, h_ref, *, nchunk):
    cl = SSM_CHUNK
    c = pl.program_id(1)
    first = c == 0

    @pl.when(first)
    def _():
        h_ref[...] = jnp.zeros(h_ref.shape, F32)

    def conv(cur, halo, w_ref, b_ref):
        ext = jnp.concatenate([jnp.where(first, 0.0, halo), cur], axis=0)
        out = (w_ref[3:4, :] * cur + w_ref[2:3, :] * ext[SUBLANES - 1:SUBLANES - 1 + cl]
               + w_ref[1:2, :] * ext[SUBLANES - 2:SUBLANES - 2 + cl]
               + w_ref[0:1, :] * ext[SUBLANES - 3:SUBLANES - 3 + cl] + b_ref[...])
        return _silu(out)

    xs = conv(xs_ref[...], xh_ref[...], cwx_ref, cbx_ref)
    bc = conv(bc_ref[...], bh_ref[...], cwb_ref, cbb_ref)
    dt = _softplus(dt_ref[...] + dtb_ref[...])
    a = dt * (-jnp.exp(alog_ref[...]))
    ri = lax.broadcasted_iota(I32, (cl, cl), 0)
    ci = lax.broadcasted_iota(I32, (cl, cl), 1)
    causal = ri >= ci
    cs = jnp.dot(jnp.where(causal, 1.0, 0.0), a, preferred_element_type=F32,
                 precision=lax.Precision.HIGHEST)
    cs_t = cs.T
    lane = lax.broadcasted_iota(I32, (cl, LANES), 1)
    lo = lane < SSM_HEAD_DIM
    rows_lo = lax.broadcasted_iota(I32, (LANES, 1), 0) < SSM_HEAD_DIM
    heads_per_group = SSM_HEADS // SSM_GROUPS

    cb = []
    for g in range(SSM_GROUPS):
        bg = bc[:, g * SSM_STATE:(g + 1) * SSM_STATE].astype(BF16)
        cg = bc[:, (SSM_GROUPS + g) * SSM_STATE:(SSM_GROUPS + g + 1) * SSM_STATE].astype(BF16)
        cb.append((bg, cg, lax.dot_general(cg, bg, NT_DIMS, preferred_element_type=F32)))

    for s in range(SSM_HEADS // 2):
        h0, h1 = 2 * s, 2 * s + 1
        bg, cg, cbg = cb[h0 // heads_per_group]
        sl = slice(s * LANES, (s + 1) * LANES)
        xs_s = xs[:, sl]
        col0, col1 = cs[:, h0:h0 + 1], cs[:, h1:h1 + 1]
        last0, last1 = cs[cl - 1:cl, h0:h0 + 1], cs[cl - 1:cl, h1:h1 + 1]
        xdt = xs_s * jnp.where(lo, dt[:, h0:h0 + 1], dt[:, h1:h1 + 1])
        xdt_b = xdt.astype(BF16)
        m0 = (cbg * jnp.where(causal, jnp.exp(col0 - cs_t[h0:h0 + 1, :]), 0.0)).astype(BF16)
        m1 = (cbg * jnp.where(causal, jnp.exp(col1 - cs_t[h1:h1 + 1, :]), 0.0)).astype(BF16)
        y_diag = jnp.where(lo, jnp.dot(m0, xdt_b, preferred_element_type=F32),
                           jnp.dot(m1, xdt_b, preferred_element_type=F32))
        hs = h_ref[sl, :]
        y_off = lax.dot_general(cg, hs.astype(BF16), NT_DIMS, preferred_element_type=F32)
        y_off = y_off * jnp.where(lo, jnp.exp(col0), jnp.exp(col1))
        y_ref[:, sl] = y_diag + y_off + dsk_ref[:, sl] * xs_s
        xw = xdt * jnp.where(lo, jnp.exp(last0 - col0), jnp.exp(last1 - col1))
        st = jnp.dot(xw.T.astype(BF16), bg, preferred_element_type=F32)
        h_ref[sl, :] = hs * jnp.where(rows_lo, jnp.exp(last0), jnp.exp(last1)) + st

    @pl.when(c == nchunk - 1)
    def _():
        hout_ref[...] = h_ref[...]


def _ssd_prompt(proj, cw, cbias, dtb, alog, dsk, nb, seq):
    m = proj.shape[0]
    cl = SSM_CHUNK
    nchunk = seq // cl
    blk = lambda w, off: pl.BlockSpec((cl, w), lambda b, c: (b * nchunk + c, off // w))
    halo = lambda off: pl.BlockSpec(
        (SUBLANES, 512), lambda b, c: (jnp.maximum((b * nchunk + c) * (cl // SUBLANES) - 1, 0), off // 512))
    const = lambda r, w, j: pl.BlockSpec((r, w), lambda b, c: (0, j))
    y, hout = pl.pallas_call(
        functools.partial(_ssd_kernel, nchunk=nchunk),
        grid=(nb, nchunk),
        in_specs=[blk(512, XS_OFF), blk(512, BC_OFF), blk(LANES, DT_OFF), halo(XS_OFF), halo(BC_OFF),
                  const(4, 512, 0), const(4, 512, 1), const(1, 512, 0), const(1, 512, 1),
                  const(1, LANES, 0), const(1, LANES, 0), const(1, 512, 0)],
        out_specs=[pl.BlockSpec((cl, SSM_INNER), lambda b, c: (b * nchunk + c, 0)),
                   pl.BlockSpec((None, SSM_INNER, SSM_STATE), lambda b, c: (b, 0, 0))],
        out_shape=[jax.ShapeDtypeStruct((m, SSM_INNER), F32),
                   jax.ShapeDtypeStruct((nb, SSM_INNER, SSM_STATE), F32)],
        scratch_shapes=[pltpu.VMEM((SSM_INNER, SSM_STATE), F32)],
        compiler_params=_cparams(("parallel", "arbitrary")),
        name="ssd_prompt",
    )(proj, proj, proj, proj, proj, cw, cw, cbias, cbias, dtb, alog, dsk)
    return y, hout


def _merge_kernel(x_ref, pa_ref, at_ref, ys_ref, z_ref, g0_ref, g1_ref, g2_ref, sn_ref, bg_ref,
                  wpc_ref, wpa_ref, wps_ref, wo_ref, o_ref):
    ssd = _rms(ys_ref[...] * _silu(z_ref[...]), sn_ref[...])
    ya = jnp.dot(pa_ref[...].astype(BF16), wpc_ref[...], preferred_element_type=F32)
    yb = jnp.dot(at_ref[...].astype(BF16), wpa_ref[...], preferred_element_type=F32)
    yc = jnp.dot(ssd.astype(BF16), wps_ref[...], preferred_element_type=F32)
    merged = (_sigmoid(g0_ref[...] + bg_ref[0:1, :]) * ya + _sigmoid(g1_ref[...] + bg_ref[1:2, :]) * yb
              + _sigmoid(g2_ref[...] + bg_ref[2:3, :]) * yc)
    o_ref[...] = x_ref[...] + jnp.dot(merged.astype(BF16), wo_ref[...], preferred_element_type=F32)


def _merge(x, pre_a, attn, y_ssd, proj, ssm_norm, b_gate, w_pc, w_pa, w_ps, w_o):
    m = x.shape[0]
    tm = min(m, 256)
    row = lambda w: pl.BlockSpec((tm, w), lambda i: (i, 0))
    pcol = lambda w, off: pl.BlockSpec((tm, w), lambda i: (i, off // w))
    const = lambda r, w: pl.BlockSpec((r, w), lambda i: (0, 0))
    return pl.pallas_call(
        _merge_kernel,
        grid=(m // tm,),
        in_specs=[row(D_MODEL), row(512), row(512), row(512), pcol(512, Z_OFF),
                  pcol(D_MODEL, G_OFF), pcol(D_MODEL, G_OFF + D_MODEL), pcol(D_MODEL, G_OFF + 2 * D_MODEL),
                  const(1, 512), const(3, D_MODEL),
                  const(512, D_MODEL), const(512, D_MODEL), const(512, D_MODEL), const(D_MODEL, D_MODEL)],
        out_specs=row(D_MODEL),
        out_shape=jax.ShapeDtypeStruct((m, D_MODEL), F32),
        compiler_params=_cparams(("parallel",)),
        name="merge",
    )(x, pre_a, attn, y_ssd, proj, proj, proj, proj, ssm_norm, b_gate, w_pc, w_pa, w_ps, w_o)


def _ffn_kernel(x_ref, g_ref, wr_ref, wg_ref, wu_ref, wd_ref, o_ref, h_ref, acc_ref, comb_ref,
                *, routed, n_e, n_f):
    e = pl.program_id(1)
    j = pl.program_id(2)
    tm = x_ref.shape[0]

    @pl.when(jnp.logical_and(e == 0, j == 0))
    def _():
        hf = _rms(x_ref[...], g_ref[...])
        h_ref[...] = hf.astype(BF16)
        acc_ref[...] = jnp.zeros(acc_ref.shape, F32)
        if routed:
            lane = lax.broadcasted_iota(I32, (tm, LANES), 1).astype(F32)
            logits = jnp.dot(hf, wr_ref[...], preferred_element_type=F32, precision=lax.Precision.HIGHEST)
            logits = jnp.where(lane < n_e, logits, -jnp.inf)
            m1 = jnp.max(logits, axis=1, keepdims=True)
            i1 = jnp.min(jnp.where(logits == m1, lane, float(LANES)), axis=1, keepdims=True)
            rest = jnp.where(lane == i1, -jnp.inf, logits)
            m2 = jnp.max(rest, axis=1, keepdims=True)
            i2 = jnp.min(jnp.where(rest == m2, lane, float(LANES)), axis=1, keepdims=True)
            e2 = jnp.exp(m2 - m1)
            den = 1.0 + e2
            comb_ref[...] = jnp.where(lane == i1, 1.0 / den, 0.0) + jnp.where(lane == i2, e2 / den, 0.0)

    h = h_ref[...]
    gate = jnp.dot(h, wg_ref[...], preferred_element_type=F32)
    up = jnp.dot(h, wu_ref[...], preferred_element_type=F32)
    act = _silu(gate) * up
    if routed:
        lane = lax.broadcasted_iota(I32, (tm, LANES), 1)
        act = act * jnp.sum(jnp.where(lane == e, comb_ref[...], 0.0), axis=1, keepdims=True)
    acc_ref[...] += jnp.dot(act.astype(BF16), wd_ref[...], preferred_element_type=F32)

    @pl.when(jnp.logical_and(e == n_e - 1, j == n_f - 1))
    def _():
        o_ref[...] = x_ref[...] + acc_ref[...]


def _ffn(x, g, w_router, w_gate, w_up, w_down, routed):
    m = x.shape[0]
    n_e, _, d_ff = w_gate.shape
    tm = min(m, 1024)
    tf = 256
    n_f = d_ff // tf
    return pl.pallas_call(
        functools.partial(_ffn_kernel, routed=routed, n_e=n_e, n_f=n_f),
        grid=(m // tm, n_e, n_f),
        in_specs=[pl.BlockSpec((tm, D_MODEL), lambda i, e, j: (i, 0)),
                  pl.BlockSpec((1, D_MODEL), lambda i, e, j: (0, 0)),
                  pl.BlockSpec((D_MODEL, LANES), lambda i, e, j: (0, 0)),
                  pl.BlockSpec((None, D_MODEL, tf), lambda i, e, j: (e, 0, j)),
                  pl.BlockSpec((None, D_MODEL, tf), lambda i, e, j: (e, 0, j)),
                  pl.BlockSpec((None, tf, D_MODEL), lambda i, e, j: (e, j, 0))],
        out_specs=pl.BlockSpec((tm, D_MODEL), lambda i, e, j: (i, 0)),
        out_shape=jax.ShapeDtypeStruct((m, D_MODEL), F32),
        scratch_shapes=[pltpu.VMEM((tm, D_MODEL), BF16), pltpu.VMEM((tm, D_MODEL), F32),
                        pltpu.VMEM((tm, LANES), F32)],
        compiler_params=_cparams(("parallel", "arbitrary", "arbitrary")),
        name="moe" if routed else "ffn",
    )(x, g, w_router, w_gate, w_up, w_down)


MOE_BLOCK = 2048
MOE_TILE = 256
TOP_K = 2


def _moe_tiles(tb):
    return TOP_K * tb // MOE_TILE + N_EXPERTS


def _route_kernel(x_ref, g_ref, wr_ref, hb_ref, row_ref, col_ref, meta_ref):
    tb = x_ref.shape[0]
    hf = _rms(x_ref[...], g_ref[...])
    hb_ref[...] = hf.astype(BF16)
    lane_i = lax.broadcasted_iota(I32, (tb, LANES), 1)
    lane = lane_i.astype(F32)
    logits = jnp.dot(hf, wr_ref[...], preferred_element_type=F32, precision=lax.Precision.HIGHEST)
    logits = jnp.where(lane_i < N_EXPERTS, logits, -jnp.inf)
    m1 = jnp.max(logits, axis=1, keepdims=True)
    i1 = jnp.min(jnp.where(logits == m1, lane, float(LANES)), axis=1, keepdims=True)
    rest = jnp.where(lane == i1, -jnp.inf, logits)
    m2 = jnp.max(rest, axis=1, keepdims=True)
    i2 = jnp.min(jnp.where(rest == m2, lane, float(LANES)), axis=1, keepdims=True)
    e2 = jnp.exp(m2 - m1)
    p1 = 1.0 / (1.0 + e2)
    p2 = e2 / (1.0 + e2)
    oh1 = lane == i1
    oh2 = lane == i2
    both = jnp.where(jnp.logical_or(oh1, oh2), 1.0, 0.0).astype(BF16)

    sb = MOE_TILE
    ri = lax.broadcasted_iota(I32, (sb, sb), 0)
    ci = lax.broadcasted_iota(I32, (sb, sb), 1)
    strict = jnp.where(ri > ci, 1.0, 0.0).astype(BF16)
    carry = jnp.zeros((1, LANES), F32)
    ranks = []
    for s in range(tb // sb):
        rows = both[s * sb:(s + 1) * sb]
        ranks.append(jnp.dot(strict, rows, preferred_element_type=F32) + carry)
        carry = carry + jnp.sum(rows.astype(F32), axis=0, keepdims=True)
    rank = jnp.concatenate(ranks, axis=0)
    seg = jnp.floor((carry + (MOE_TILE - 1)) * (1.0 / MOE_TILE)) * MOE_TILE
    ui = lax.broadcasted_iota(I32, (LANES, LANES), 0)
    uj = lax.broadcasted_iota(I32, (LANES, LANES), 1)
    before = jnp.where(ui < uj, 1.0, 0.0)
    off = jnp.dot(jnp.broadcast_to(seg, (SUBLANES, LANES)), before, preferred_element_type=F32,
                  precision=lax.Precision.HIGHEST)[0:1, :]
    dest = off + rank
    d1 = jnp.sum(jnp.where(oh1, dest, 0.0), axis=1, keepdims=True)
    d2 = jnp.sum(jnp.where(oh2, dest, 0.0), axis=1, keepdims=True)
    col = jnp.where(lane_i == 0, d1, jnp.where(lane_i == 1, d2, jnp.where(lane_i == 2, p1,
                                                                          jnp.where(lane_i == 3, p2, 0.0))))
    col_ref[...] = col
    row_ref[...] = col.T[0:SUBLANES, :]

    end_rows = jnp.broadcast_to(off + seg, (LANES, LANES)).T
    start = (lax.broadcasted_iota(I32, (LANES, LANES), 1) * MOE_TILE).astype(F32)
    is_expert = lax.broadcasted_iota(I32, (LANES, LANES), 0) < N_EXPERTS
    done = jnp.where(jnp.logical_and(is_expert, end_rows <= start), 1.0, 0.0)
    expert = jnp.minimum(jnp.sum(done, axis=0, keepdims=True), float(N_EXPERTS - 1))
    total = jnp.sum(jnp.where(lane_i[0:1] < N_EXPERTS, seg, 0.0), axis=1, keepdims=True)
    valid = jnp.where(start[0:1] < total, 1.0, 0.0)
    sub = lax.broadcasted_iota(I32, (SUBLANES, LANES), 0)
    meta_ref[...] = jnp.where(sub == 0, expert, jnp.where(sub == 1, valid, 0.0)).astype(I32)


def _route(x, g, w_router, tb):
    m = x.shape[0]
    nblk = m // tb
    return pl.pallas_call(
        _route_kernel,
        grid=(nblk,),
        in_specs=[pl.BlockSpec((tb, D_MODEL), lambda b: (b, 0)),
                  pl.BlockSpec((1, D_MODEL), lambda b: (0, 0)),
                  pl.BlockSpec((D_MODEL, LANES), lambda b: (0, 0))],
        out_specs=[pl.BlockSpec((tb, D_MODEL), lambda b: (b, 0)),
                   pl.BlockSpec((SUBLANES, tb), lambda b: (0, b)),
                   pl.BlockSpec((tb, LANES), lambda b: (b, 0)),
                   pl.BlockSpec((None, SUBLANES, LANES), lambda b: (b, 0, 0))],
        out_shape=[jax.ShapeDtypeStruct((m, D_MODEL), BF16),
                   jax.ShapeDtypeStruct((SUBLANES, m), F32),
                   jax.ShapeDtypeStruct((m, LANES), F32),
                   jax.ShapeDtypeStruct((nblk, SUBLANES, LANES), I32)],
        compiler_params=_cparams(("parallel",)),
        name="moe_route",
    )(x, g, w_router)


def _gffn_kernel(te_ref, tv_ref, hb_ref, row_ref, wg_ref, wu_ref, wd_ref, ys_ref, xg_ref, acc_ref, *, nt, n_f):
    del te_ref
    b, t, j = pl.program_id(0), pl.program_id(1), pl.program_id(2)
    valid = tv_ref[b * nt + t] > 0
    tb = hb_ref.shape[0]

    @pl.when(jnp.logical_and(valid, j == 0))
    def _():
        slot = (t * MOE_TILE + lax.broadcasted_iota(I32, (MOE_TILE, tb), 0)).astype(F32)
        hit = jnp.logical_or(row_ref[0:1, :] == slot, row_ref[1:2, :] == slot)
        onehot = jnp.where(hit, 1.0, 0.0).astype(BF16)
        xg_ref[...] = jnp.dot(onehot, hb_ref[...], preferred_element_type=F32).astype(BF16)
        acc_ref[...] = jnp.zeros(acc_ref.shape, F32)

    @pl.when(valid)
    def _():
        xg = xg_ref[...]
        gate = jnp.dot(xg, wg_ref[...], preferred_element_type=F32)
        up = jnp.dot(xg, wu_ref[...], preferred_element_type=F32)
        acc_ref[...] += jnp.dot((_silu(gate) * up).astype(BF16), wd_ref[...], preferred_element_type=F32)

    @pl.when(j == n_f - 1)
    def _():
        ys_ref[...] = jnp.where(valid, acc_ref[...], 0.0).astype(BF16)


def _gffn(tile_expert, tile_valid, hb, rowinfo, w_gate, w_up, w_down, tb):
    m = hb.shape[0]
    nblk = m // tb
    nt = _moe_tiles(tb)
    d_ff = w_gate.shape[2]
    tf = d_ff // 2
    n_f = d_ff // tf
    fidx = lambda b, t, j, te, tv: jnp.where(tv[b * nt + t] > 0, j, n_f - 1)
    grid_spec = pltpu.PrefetchScalarGridSpec(
        num_scalar_prefetch=2,
        grid=(nblk, nt, n_f),
        in_specs=[pl.BlockSpec((tb, D_MODEL), lambda b, t, j, te, tv: (b, 0)),
                  pl.BlockSpec((SUBLANES, tb), lambda b, t, j, te, tv: (0, b)),
                  pl.BlockSpec((None, D_MODEL, tf), lambda b, t, j, te, tv: (te[b * nt + t], 0, fidx(b, t, j, te, tv))),
                  pl.BlockSpec((None, D_MODEL, tf), lambda b, t, j, te, tv: (te[b * nt + t], 0, fidx(b, t, j, te, tv))),
                  pl.BlockSpec((None, tf, D_MODEL), lambda b, t, j, te, tv: (te[b * nt + t], fidx(b, t, j, te, tv), 0))],
        out_specs=pl.BlockSpec((MOE_TILE, D_MODEL), lambda b, t, j, te, tv: (b * nt + t, 0)),
        scratch_shapes=[pltpu.VMEM((MOE_TILE, D_MODEL), BF16), pltpu.VMEM((MOE_TILE, D_MODEL), F32)],
    )
    return pl.pallas_call(
        functools.partial(_gffn_kernel, nt=nt, n_f=n_f),
        grid_spec=grid_spec,
        out_shape=jax.ShapeDtypeStruct((nblk * nt * MOE_TILE, D_MODEL), BF16),
        compiler_params=_cparams(("parallel", "arbitrary", "arbitrary")),
        name="moe_experts",
    )(tile_expert, tile_valid, hb, rowinfo, w_gate, w_up, w_down)


def _combine_kernel(tv_ref, x_ref, col_ref, ys_ref, o_ref, *, nt):
    b, t = pl.program_id(0), pl.program_id(1)

    @pl.when(t == 0)
    def _():
        o_ref[...] = x_ref[...]

    @pl.when(tv_ref[b * nt + t] > 0)
    def _():
        tb = x_ref.shape[0]
        slot = (t * MOE_TILE + lax.broadcasted_iota(I32, (tb, MOE_TILE), 1)).astype(F32)
        weight = (jnp.where(col_ref[:, 0:1] == slot, col_ref[:, 2:3], 0.0)
                  + jnp.where(col_ref[:, 1:2] == slot, col_ref[:, 3:4], 0.0))
        o_ref[...] += jnp.dot(weight.astype(BF16), ys_ref[...], preferred_element_type=F32)


def _combine(tile_valid, x, colinfo, ys, tb):
    m = x.shape[0]
    nblk = m // tb
    nt = _moe_tiles(tb)
    grid_spec = pltpu.PrefetchScalarGridSpec(
        num_scalar_prefetch=1,
        grid=(nblk, nt),
        in_specs=[pl.BlockSpec((tb, D_MODEL), lambda b, t, tv: (b, 0)),
                  pl.BlockSpec((tb, LANES), lambda b, t, tv: (b, 0)),
                  pl.BlockSpec((MOE_TILE, D_MODEL), lambda b, t, tv: (b * nt + t, 0))],
        out_specs=pl.BlockSpec((tb, D_MODEL), lambda b, t, tv: (b, 0)),
    )
    return pl.pallas_call(
        functools.partial(_combine_kernel, nt=nt),
        grid_spec=grid_spec,
        out_shape=jax.ShapeDtypeStruct((m, D_MODEL), F32),
        compiler_params=_cparams(("parallel", "arbitrary")),
        name="moe_combine",
    )(tile_valid, x, colinfo, ys)


def _moe_grouped(x, g, w_router, w_gate, w_up, w_down):
    m = x.shape[0]
    tb = min(MOE_BLOCK, m)
    nt = _moe_tiles(tb)
    hb, rowinfo, colinfo, meta = _route(x, g, w_router, tb)
    tile_expert = meta[:, 0, :nt].reshape(-1)
    tile_valid = meta[:, 1, :nt].reshape(-1)
    ys = _gffn(tile_expert, tile_valid, hb, rowinfo, w_gate, w_up, w_down, tb)
    return _combine(tile_valid, x, colinfo, ys, tb)


def _decmix_kernel(cb_ref, cc_ref, cx_ref, xs_ref, bc_ref, dt_ref, sc0_ref, sc1_ref,
                   sx0_ref, sx1_ref, sx2_ref, sb0_ref, sb1_ref, sb2_ref,
                   cw_ref, cwx_ref, cwb_ref, cbx_ref, cbb_ref, dtb_ref, alog_ref,
                   pa_ref, u_ref, xso_ref, bco_ref, dto_ref, ea_ref):
    u = cc_ref[...] * cx_ref[...]
    u_ref[...] = u
    pa_ref[...] = cb_ref[...] * (cw_ref[0:1, :] * sc0_ref[...] + cw_ref[1:2, :] * sc1_ref[...] + cw_ref[2:3, :] * u)
    xso_ref[...] = _silu(cwx_ref[0:1, :] * sx0_ref[...] + cwx_ref[1:2, :] * sx1_ref[...]
                         + cwx_ref[2:3, :] * sx2_ref[...] + cwx_ref[3:4, :] * xs_ref[...] + cbx_ref[...])
    bco_ref[...] = _silu(cwb_ref[0:1, :] * sb0_ref[...] + cwb_ref[1:2, :] * sb1_ref[...]
                         + cwb_ref[2:3, :] * sb2_ref[...] + cwb_ref[3:4, :] * bc_ref[...] + cbb_ref[...])
    dt = _softplus(dt_ref[...] + dtb_ref[...])
    dto_ref[...] = dt
    ea_ref[...] = jnp.exp(dt * (-jnp.exp(alog_ref[...])))


def _decmix(proj, st_conv, st_ssm_conv, conv_w, cw, cbias, dtb, alog):
    nb = proj.shape[0]
    pcol = lambda w, off: pl.BlockSpec((nb, w), lambda i: (0, off // w))
    full = lambda w: pl.BlockSpec((nb, w), lambda i: (0, 0))
    const = lambda r, w, j: pl.BlockSpec((r, w), lambda i: (0, j))
    sx = [st_ssm_conv[:, t, :512] for t in range(3)]
    sb = [st_ssm_conv[:, t, 512:] for t in range(3)]
    return pl.pallas_call(
        _decmix_kernel,
        grid=(1,),
        in_specs=[pcol(512, CB_OFF), pcol(512, CC_OFF), pcol(512, CX_OFF), pcol(512, XS_OFF), pcol(512, BC_OFF),
                  pcol(LANES, DT_OFF)] + [full(512)] * 8
                 + [const(3, 512, 0), const(4, 512, 0), const(4, 512, 1), const(1, 512, 0), const(1, 512, 1),
                    const(1, LANES, 0), const(1, LANES, 0)],
        out_specs=[full(512), full(512), full(512), full(512), full(LANES), full(LANES)],
        out_shape=[jax.ShapeDtypeStruct((nb, 512), F32)] * 4 + [jax.ShapeDtypeStruct((nb, LANES), F32)] * 2,
        compiler_params=_cparams(("arbitrary",)),
        name="decode_mix",
    )(proj, proj, proj, proj, proj, proj, st_conv[:, 0], st_conv[:, 1], *sx, *sb,
      conv_w, cw, cw, cbias, cbias, dtb, alog)


def _decssd_kernel(h_ref, dt_ref, xs_ref, b_ref, c_ref, ea_ref, dsk_ref, y_ref, ho_ref):
    xs = xs_ref[...]
    h_new = h_ref[...] * ea_ref[...] + (xs * dt_ref[...]) * b_ref[...]
    ho_ref[...] = h_new
    y_ref[...] = jnp.sum(h_new * c_ref[...], axis=-1, keepdims=True) + dsk_ref[...] * xs


def _decssd(h0, dt, xs, bh, ch, ea, dsk):
    nb = h0.shape[0]
    blk = lambda a, b: pl.BlockSpec((None, SSM_HEADS, a, b), lambda i: (i, 0, 0, 0))
    return pl.pallas_call(
        _decssd_kernel,
        grid=(nb,),
        in_specs=[blk(SSM_HEAD_DIM, SSM_STATE), blk(1, 1), blk(SSM_HEAD_DIM, 1), blk(1, SSM_STATE),
                  blk(1, SSM_STATE), blk(1, 1), pl.BlockSpec((SSM_HEADS, 1, 1), lambda i: (0, 0, 0))],
        out_specs=[blk(SSM_HEAD_DIM, 1), blk(SSM_HEAD_DIM, SSM_STATE)],
        out_shape=[jax.ShapeDtypeStruct((nb, SSM_HEADS, SSM_HEAD_DIM, 1), F32),
                   jax.ShapeDtypeStruct((nb, SSM_HEADS, SSM_HEAD_DIM, SSM_STATE), F32)],
        compiler_params=_cparams(("parallel",)),
        name="decode_ssd",
    )(h0, dt, xs, bh, ch, ea, dsk)


DECODE_PAGE_GROUP = 16


def _decscore_kernel(pt_ref, qi_ref, w_ref, *refs):
    del pt_ref
    kid_refs, o_ref = refs[:-1], refs[-1]
    kid = jnp.concatenate([r[...].astype(BF16) for r in kid_refs], axis=0)
    s = lax.dot_general(qi_ref[:, 0:IDX_DIM].astype(BF16), kid, NT_DIMS, preferred_element_type=F32)
    sc = jnp.maximum(s, 0.0) * w_ref[:, 0:1]
    o_ref[...] = jnp.sum(sc, axis=0, keepdims=True)


def _page_specs(block, layer, pg):
    zeros = (0,) * (len(block) - 2)
    return [pl.BlockSpec(block, lambda b, p, pt, t=t: (layer, pt[b, p * pg + t]) + zeros) for t in range(pg)]


def _decscore(page_table, qip, wrows, pool_kidx, layer):
    nb, n_pages = page_table.shape
    pg = math.gcd(DECODE_PAGE_GROUP, n_pages)
    grid_spec = pltpu.PrefetchScalarGridSpec(
        num_scalar_prefetch=1,
        grid=(nb, n_pages // pg),
        in_specs=[pl.BlockSpec((None, N_IDX_HEADS, LANES), lambda b, p, pt: (b, 0, 0)),
                  pl.BlockSpec((None, N_IDX_HEADS, PAGE_SIZE), lambda b, p, pt: (b, 0, 0))]
                 + _page_specs((None, None, PAGE_SIZE, IDX_DIM), layer, pg),
        out_specs=pl.BlockSpec((None, None, 1, pg * PAGE_SIZE), lambda b, p, pt: (b, p, 0, 0)),
    )
    out = pl.pallas_call(
        _decscore_kernel,
        grid_spec=grid_spec,
        out_shape=jax.ShapeDtypeStruct((nb, n_pages // pg, 1, pg * PAGE_SIZE), F32),
        compiler_params=_cparams(("parallel", "arbitrary")),
        name="decode_scores",
    )(page_table, qip, wrows, *([pool_kidx] * pg))
    return out.reshape(nb, n_pages * PAGE_SIZE)


def _decselect_kernel(sc_ref, qi_ref, ki_ref, w_ref, sel_ref, self_ref, keys_ref, *, k_sel, kc, nbits):
    nb, past = sc_ref.shape
    nck = past // kc
    lane_k = lax.broadcasted_iota(I32, (nb, kc), 1)
    ki = ki_ref[...].astype(F32)
    own = jnp.zeros((nb, 1), F32)
    for h in range(N_IDX_HEADS):
        s = jnp.sum(qi_ref[:, h * LANES:(h + 1) * LANES].astype(F32) * ki, axis=1, keepdims=True)
        own = own + jnp.maximum(s, 0.0) * w_ref[:, h:h + 1]
    own_key = _sortable(own)

    def key_body(c, carry):
        off = pl.multiple_of(c * kc, kc)
        keys_ref[:, pl.ds(off, kc)] = _sortable(sc_ref[:, pl.ds(off, kc)])
        return carry

    lax.fori_loop(0, nck, key_body, 0)

    def count_where(pred):
        def body(c, acc):
            off = pl.multiple_of(c * kc, kc)
            f = jnp.where(pred(keys_ref[:, pl.ds(off, kc)], off), 1.0, 0.0)
            part = f[:, 0:LANES]
            for t in range(1, kc // LANES):
                part = part + f[:, t * LANES:(t + 1) * LANES]
            return acc + part

        acc = lax.fori_loop(0, nck, body, jnp.zeros((nb, LANES), F32))
        return jnp.sum(acc, axis=1, keepdims=True)

    def bit_body(i, t):
        cand = t + lax.shift_left(jnp.int32(1), 31 - i)
        cnt = count_where(lambda kk, off: kk >= cand) + jnp.where(own_key >= cand, 1.0, 0.0)
        return jnp.where(cnt >= k_sel, cand, t)

    thr = lax.fori_loop(0, 32, bit_body, jnp.full((nb, 1), INT_MIN, I32))
    thr = jnp.maximum(thr, KEY_NEG_INF)
    finite_thr = thr > KEY_NEG_INF
    cnt_gt = count_where(lambda kk, off: kk > thr) + jnp.where(own_key > thr, 1.0, 0.0)
    need = k_sel - cnt_gt

    def xbody(i, x):
        cand = x + lax.shift_left(jnp.int32(1), nbits - 1 - i)
        cnt = count_where(lambda kk, off: jnp.logical_and(kk == thr, off + lane_k < cand))
        return jnp.where(cnt < need, cand, x)

    xcut = lax.fori_loop(0, nbits, xbody, jnp.zeros((nb, 1), I32))
    xcut = jnp.where(finite_thr, xcut, -1)
    ties_past = count_where(lambda kk, off: kk == thr)
    own_tie = jnp.logical_and(jnp.logical_and(own_key == thr, finite_thr), ties_past < need)
    self_ref[...] = jnp.broadcast_to(
        jnp.where(jnp.logical_or(own_key > thr, own_tie), 1.0, 0.0), (nb, LANES))

    def sel_body(c, carry):
        off = pl.multiple_of(c * kc, kc)
        kk = keys_ref[:, pl.ds(off, kc)]
        sel = jnp.logical_or(kk > thr, jnp.logical_and(kk == thr, off + lane_k <= xcut))
        sel_ref[:, pl.ds(off, kc)] = jnp.where(sel, 1.0, 0.0)
        return carry

    lax.fori_loop(0, nck, sel_body, 0)


def _decselect(scores, qip, kib, wi):
    nb, past = scores.shape
    k_sel = min(TOPK_MAX, (past + 1) // 4)
    kc = 512 if past % 512 == 0 else PAGE_SIZE
    nbits = max(1, past.bit_length())
    full = lambda w: pl.BlockSpec((nb, w), lambda i: (0, 0))
    return pl.pallas_call(
        functools.partial(_decselect_kernel, k_sel=k_sel, kc=kc, nbits=nbits),
        grid=(1,),
        in_specs=[full(past), full(N_IDX_HEADS * LANES), full(LANES), full(LANES)],
        out_specs=[full(past), full(LANES)],
        out_shape=[jax.ShapeDtypeStruct((nb, past), F32), jax.ShapeDtypeStruct((nb, LANES), F32)],
        scratch_shapes=[pltpu.VMEM((nb, past), I32)],
        compiler_params=_cparams(("arbitrary",)),
        name="decode_select",
    )(scores, qip, kib, wi)


def _decattn_kernel(pt_ref, qp_ref, sel_ref, kn_ref, vn_ref, self_ref, *refs, n_steps):
    del pt_ref
    pg = (len(refs) - 4) // 2
    kp_refs, vp_refs = refs[:pg], refs[pg:2 * pg]
    o_ref, m_ref, l_ref, acc_ref = refs[2 * pg:]
    p = pl.program_id(1)

    @pl.when(p == 0)
    def _():
        m_ref[...] = jnp.full(m_ref.shape, NEG_BIG, F32)
        l_ref[...] = jnp.zeros(l_ref.shape, F32)
        acc_ref[...] = jnp.zeros(acc_ref.shape, F32)

    q = qp_ref[...]
    kcat = jnp.concatenate([r[...].astype(BF16) for r in kp_refs], axis=0)
    vcat = jnp.concatenate([r[...].astype(BF16) for r in vp_refs], axis=0)
    s = lax.dot_general(q.astype(BF16), kcat, NT_DIMS, preferred_element_type=F32)
    s = jnp.where(sel_ref[...] > 0.0, s, -jnp.inf)
    m_old = m_ref[:, 0:1]
    m_new = jnp.maximum(m_old, jnp.max(s, axis=1, keepdims=True))
    alpha = jnp.exp(m_old - m_new)
    pr = jnp.exp(s - m_new)
    l_new = alpha * l_ref[:, 0:1] + jnp.sum(pr, axis=1, keepdims=True)
    acc_ref[...] = alpha * acc_ref[...] + jnp.dot(pr.astype(BF16), vcat, preferred_element_type=F32)
    m_ref[...] = jnp.broadcast_to(m_new, m_ref.shape)
    l_ref[...] = jnp.broadcast_to(l_new, l_ref.shape)

    @pl.when(p == n_steps - 1)
    def _():
        s_own = jnp.sum(q * kn_ref[...], axis=1, keepdims=True)
        s_own = jnp.where(self_ref[:, 0:1] > 0.0, s_own, -jnp.inf)
        m_o = m_ref[:, 0:1]
        m_n = jnp.maximum(m_o, s_own)
        al = jnp.exp(m_o - m_n)
        p_own = jnp.exp(s_own - m_n)
        l_n = al * l_ref[:, 0:1] + p_own
        acc = al * acc_ref[...] + p_own.astype(BF16).astype(F32) * vn_ref[...]
        o_ref[...] = acc / l_n


def _decattn(page_table, qp, sel, pool_k, pool_v, kb, vb, self_sel, layer):
    nb, n_pages = page_table.shape
    pg = math.gcd(DECODE_PAGE_GROUP, n_pages)
    n_steps = n_pages // pg
    row = lambda r: pl.BlockSpec((None, r, LANES), lambda b, p, pt: (b, 0, 0))
    pages = _page_specs((None, None, PAGE_SIZE, LANES), layer, pg)
    grid_spec = pltpu.PrefetchScalarGridSpec(
        num_scalar_prefetch=1,
        grid=(nb, n_steps),
        in_specs=[row(N_HEADS),
                  pl.BlockSpec((None, None, 1, pg * PAGE_SIZE), lambda b, p, pt: (b, p, 0, 0)),
                  row(1), row(1), row(1)] + pages + pages,
        out_specs=row(N_HEADS),
        scratch_shapes=[pltpu.VMEM((N_HEADS, LANES), F32)] * 3,
    )
    return pl.pallas_call(
        functools.partial(_decattn_kernel, n_steps=n_steps),
        grid_spec=grid_spec,
        out_shape=jax.ShapeDtypeStruct((nb, N_HEADS, LANES), F32),
        compiler_params=_cparams(("parallel", "arbitrary")),
        name="decode_attend",
    )(page_table, qp, sel.reshape(nb, n_steps, 1, pg * PAGE_SIZE), kb, vb, self_sel,
      *([pool_k] * pg), *([pool_v] * pg))


def _pack_w_in(w):
    d = w.shape[0]
    pad = lambda n: jnp.zeros((d, n), w.dtype)
    cols = [w[:, 4172:7244],
            w[:, 0:2048],
            w[:, 2628:3140],
            w[:, 3140:4164],
            w[:, 2048:2304],
            w[:, 2304:2560],
            w[:, 2560:2628], pad(60),
            w[:, 4164:4172], pad(120),
            pad(N_PROJ - 7424)]
    return jnp.concatenate(cols, axis=1).astype(BF16)


def _pad_lanes(v):
    return jnp.zeros((1, LANES), F32).at[0, :v.shape[0]].set(v)


def _rope_tables(pos):
    half = HEAD_DIM // 2
    inv = ROPE_THETA ** (-jnp.arange(half, dtype=F32) / half)
    ang = pos.astype(F32)[:, None] * inv[None, :]
    cos, sin = jnp.cos(ang), jnp.sin(ang)
    cos2 = jnp.concatenate([cos, cos], axis=1)
    sin2 = jnp.concatenate([-sin, sin], axis=1)
    return jnp.tile(cos2, (1, 2)), jnp.tile(sin2, (1, 2))


def _head_halves(o):
    hpg = N_HEADS // N_KV_HEADS
    parts = [o[:, h, (h // hpg) * HEAD_DIM:(h // hpg + 1) * HEAD_DIM] for h in range(N_HEADS)]
    return jnp.concatenate(parts, axis=-1)


def kernel(x_prompt, x_sample, cache_k, cache_v, cache_kidx, state_conv, state_ssm_conv, state_ssm, page_table,
           norm1, w_in, b_gate, conv_w, q_norm, k_norm, w_pc, w_pa, ssm_conv_w, ssm_conv_b, dt_bias, a_log,
           d_skip, ssm_norm, w_ps, w_o, norm2, w_gate_dense, w_up_dense, w_down_dense, w_router, w_gate_moe,
           w_up_moe, w_down_moe):
    nb, seq, _ = x_prompt.shape
    db = x_sample.shape[0]
    depth = w_in.shape[0]
    n_pages = page_table.shape[1]
    past = n_pages * PAGE_SIZE

    hp = x_prompt.reshape(nb * seq, D_MODEL)
    hs = x_sample.reshape(db, D_MODEL)
    cos_p, sin_p = _rope_tables(jnp.tile(jnp.arange(seq), nb))
    cos_s, sin_s = _rope_tables(jnp.full((db,), past))

    pool_k = cache_k.reshape(*cache_k.shape[:3], LANES)
    pool_v = cache_v.reshape(*cache_v.shape[:3], LANES)

    outs_p = [[] for _ in range(6)]
    outs_s = [[] for _ in range(6)]
    for l in range(depth):
        wp = _pack_w_in(w_in[l])
        g1 = norm1[l].reshape(1, D_MODEL)
        g2 = norm2[l].reshape(1, D_MODEL)
        qg = jnp.tile(q_norm[l], 2).reshape(1, LANES)
        kg = jnp.tile(k_norm[l], 2).reshape(1, LANES)
        cw = ssm_conv_w[l]
        cbias = ssm_conv_b[l].reshape(1, SSM_CONV_DIM)
        dtb = _pad_lanes(dt_bias[l])
        alog = _pad_lanes(a_log[l])
        dsk = jnp.repeat(d_skip[l], SSM_HEAD_DIM).reshape(1, SSM_INNER)
        sn = ssm_norm[l].reshape(1, SSM_INNER)
        wpc, wpa, wps, wo = (w.astype(BF16) for w in (w_pc[l], w_pa[l], w_ps[l], w_o[l]))
        i = l // 2
        if l % 2 == 0:
            routed = False
            wr = jnp.zeros((D_MODEL, LANES), F32)
            wg, wu, wd = (w[i:i + 1].astype(BF16) for w in (w_gate_dense, w_up_dense, w_down_dense))
        else:
            routed = True
            wr = jnp.zeros((D_MODEL, LANES), F32).at[:, :N_EXPERTS].set(w_router[i])
            wg, wu, wd = (w[i].astype(BF16) for w in (w_gate_moe, w_up_moe, w_down_moe))

        proj = _inproj(hp, g1, wp)
        qt, kf, kb, vt, qit, kif, kib, wt = _prep(proj, cos_p, sin_p, qg, kg, True)
        attn = _dsa_prompt(qt, qit, wt, kb, vt, kib, nb, seq)
        pre_a, conv_st = _conva_prompt(proj, conv_w[l], nb, seq)
        y_ssd, h_last = _ssd_prompt(proj, cw, cbias, dtb, alog, dsk, nb, seq)
        hp = _merge(hp, pre_a, attn, y_ssd, proj, sn, b_gate[l], wpc, wpa, wps, wo)
        if routed and hp.shape[0] % MOE_TILE == 0:
            hp = _moe_grouped(hp, g2, wr, wg, wu, wd)
        else:
            hp = _ffn(hp, g2, wr, wg, wu, wd, routed)
        proj3 = proj.reshape(nb, seq, N_PROJ)
        outs_p[0].append(kf.reshape(nb, seq, N_KV_HEADS, HEAD_DIM))
        outs_p[1].append(proj3[:, :, V_OFF:V_OFF + LANES].reshape(nb, seq, N_KV_HEADS, HEAD_DIM))
        outs_p[2].append(kif[:, :IDX_DIM].reshape(nb, seq, IDX_DIM))
        outs_p[3].append(conv_st)
        outs_p[4].append(proj3[:, seq - 3:, XS_OFF:XS_OFF + SSM_CONV_DIM])
        outs_p[5].append(h_last.reshape(nb, SSM_HEADS, SSM_HEAD_DIM, SSM_STATE))

        proj = _inproj(hs, g1, wp)
        qp, kf, kb, vb, qip, kif, kib, wi = _prep(proj, cos_s, sin_s, qg, kg, False)
        wrows = jnp.broadcast_to(wi[:, :N_IDX_HEADS, None], (db, N_IDX_HEADS, PAGE_SIZE))
        scores = _decscore(page_table, qip.astype(F32).reshape(db, N_IDX_HEADS, LANES), wrows, cache_kidx, l)
        sel, self_sel = _decselect(scores, qip, kib, wi)
        o = _decattn(page_table, qp.astype(F32).reshape(db, N_HEADS, LANES), sel, pool_k, pool_v,
                     kb.astype(F32).reshape(db, 1, LANES), vb.astype(F32).reshape(db, 1, LANES),
                     self_sel.reshape(db, 1, LANES), l)
        attn = _head_halves(o)
        pre_a, u, xs, bcv, dt, ea = _decmix(proj, state_conv[l], state_ssm_conv[l], conv_w[l], cw, cbias, dtb, alog)
        hpg = SSM_HEADS // SSM_GROUPS
        xs4 = xs.reshape(db, SSM_HEADS, SSM_HEAD_DIM, 1)
        bh = jnp.repeat(bcv[:, :SSM_GROUPS * SSM_STATE].reshape(db, SSM_GROUPS, 1, SSM_STATE), hpg, axis=1)
        ch = jnp.repeat(bcv[:, SSM_GROUPS * SSM_STATE:].reshape(db, SSM_GROUPS, 1, SSM_STATE), hpg, axis=1)
        y4, h_new = _decssd(state_ssm[l], dt[:, :SSM_HEADS, None, None], xs4, bh, ch, ea[:, :SSM_HEADS, None, None],
                            d_skip[l].reshape(SSM_HEADS, 1, 1))
        hs = _merge(hs, pre_a, attn, y4.reshape(db, SSM_INNER), proj, sn, b_gate[l], wpc, wpa, wps, wo)
        hs = _ffn(hs, g2, wr, wg, wu, wd, routed)
        outs_s[0].append(kf.reshape(db, 1, N_KV_HEADS, HEAD_DIM))
        outs_s[1].append(proj[:, V_OFF:V_OFF + LANES].reshape(db, 1, N_KV_HEADS, HEAD_DIM))
        outs_s[2].append(kif[:, :IDX_DIM].reshape(db, 1, IDX_DIM))
        outs_s[3].append(jnp.stack([state_conv[l][:, 1], u], axis=1))
        outs_s[4].append(jnp.concatenate(
            [state_ssm_conv[l][:, 1:], proj[:, None, XS_OFF:XS_OFF + SSM_CONV_DIM]], axis=1))
        outs_s[5].append(h_new)

    return (hp.reshape(nb, seq, D_MODEL), hs.reshape(db, 1, D_MODEL),
            *(jnp.stack(o) for o in outs_p), *(jnp.stack(o) for o in outs_s))
```

```python
import functools
import math

import jax
import jax.numpy as jnp
from jax import lax
from jax.experimental import pallas as pl
from jax.experimental.pallas import tpu as pltpu

F32 = jnp.float32
BF16 = jnp.bfloat16
I32 = jnp.int32

D_MODEL = 1024
D_CONV = 512
N_HEADS = 8
HEAD_DIM = 64
N_KV_HEADS = 2
N_IDX_HEADS = 4
IDX_DIM = 64
TOPK_MAX = 256
Q_BLOCK = 128
ROPE_THETA = 10000.0
SSM_INNER = 512
SSM_HEADS = 8
SSM_HEAD_DIM = 64
SSM_GROUPS = 2
SSM_STATE = 128
SSM_CHUNK = 128
SSM_CONV_DIM = 1024
PAGE_SIZE = 128
N_EXPERTS = 8
EPS = 1e-6

LANES = 128
SUBLANES = 8
VMEM_LIMIT = 52 * 1024 * 1024

G_OFF, CB_OFF, CC_OFF, CX_OFF, Q_OFF, Z_OFF, XS_OFF, BC_OFF = 0, 3072, 3584, 4096, 4608, 5120, 5632, 6144
K_OFF, V_OFF, QI_OFF, KW_OFF, DT_OFF = 6656, 6784, 6912, 7168, 7296
N_PROJ = 7680

INT_MIN = -2 ** 31
INT_MAX = 2 ** 31 - 1
KEY_NEG_INF = (-8388608) ^ 0x7FFFFFFF
NEG_BIG = -1e30
LOG2_E = 1.4426950408889634

NT_DIMS = (((1,), (1,)), ((), ()))


def _cparams(sem):
    return pltpu.CompilerParams(dimension_semantics=sem, vmem_limit_bytes=VMEM_LIMIT)


def _sigmoid(x):
    return 1.0 / (1.0 + jnp.exp(-x))


def _silu(x):
    return x * _sigmoid(x)


def _softplus(x):
    return jnp.maximum(x, 0.0) + jnp.log1p(jnp.exp(-jnp.abs(x)))


def _sortable(x):
    bits = pltpu.bitcast(x, I32)
    return bits ^ (lax.shift_right_arithmetic(bits, 31) & 0x7FFFFFFF)


def _rms(x, g):
    return x * lax.rsqrt(jnp.mean(x * x, axis=-1, keepdims=True) + EPS) * g


def _inproj_kernel(x_ref, g_ref, w_ref, o_ref, h_ref):
    @pl.when(pl.program_id(1) == 0)
    def _():
        h_ref[...] = _rms(x_ref[...], g_ref[...]).astype(BF16)

    o_ref[...] = jnp.dot(h_ref[...], w_ref[...], preferred_element_type=F32)


def _inproj(x, g, w):
    m = x.shape[0]
    tm = min(m, 1024)
    tn = 1536
    return pl.pallas_call(
        _inproj_kernel,
        grid=(m // tm, N_PROJ // tn),
        in_specs=[pl.BlockSpec((tm, D_MODEL), lambda i, j: (i, 0)),
                  pl.BlockSpec((1, D_MODEL), lambda i, j: (0, 0)),
                  pl.BlockSpec((D_MODEL, tn), lambda i, j: (0, j))],
        out_specs=pl.BlockSpec((tm, tn), lambda i, j: (i, j)),
        out_shape=jax.ShapeDtypeStruct((m, N_PROJ), F32),
        scratch_shapes=[pltpu.VMEM((tm, D_MODEL), BF16)],
        compiler_params=_cparams(("parallel", "arbitrary")),
        name="inproj",
    )(x, g, w)


V_T_ROWS = LANES + 16


def _prep_kernel(q_ref, k_ref, v_ref, qi_ref, kw_ref, cos_ref, sin_ref, qg_ref, kg_ref,
                 qp_ref, kf_ref, kb_ref, vb_ref, qip_ref, kif_ref, kib_ref, wo_ref, *, transposed):
    tm = cos_ref.shape[0]

    def put(ref, h, val):
        if transposed:
            vt = val.T.astype(BF16)
            for blk in range(tm // Q_BLOCK):
                ref[blk, h * LANES:(h + 1) * LANES, :] = vt[:, blk * Q_BLOCK:(blk + 1) * Q_BLOCK]
        else:
            ref[:, h * LANES:(h + 1) * LANES] = val.astype(BF16)

    cos = cos_ref[...]
    sin = sin_ref[...]
    lane = lax.broadcasted_iota(I32, (tm, LANES), 1)
    low_half = lane < HEAD_DIM
    first_rot = (lane % HEAD_DIM) < (HEAD_DIM // 2)
    r = lax.broadcasted_iota(I32, (LANES, LANES), 0) // HEAD_DIM
    c = lax.broadcasted_iota(I32, (LANES, LANES), 1) // HEAD_DIM
    seg = jnp.where(r == c, 1.0 / HEAD_DIM, 0.0).astype(BF16)

    def rope(x):
        fwd = pltpu.roll(x, LANES - HEAD_DIM // 2, 1)
        bwd = pltpu.roll(x, HEAD_DIM // 2, 1)
        return x * cos + jnp.where(first_rot, fwd, bwd) * sin

    def headnorm(x, g):
        s = x * x
        hi = s.astype(BF16)
        lo = (s - hi.astype(F32)).astype(BF16)
        ms = (jnp.dot(hi, seg, preferred_element_type=F32)
              + jnp.dot(lo, seg, preferred_element_type=F32))
        return x * lax.rsqrt(ms + EPS) * g

    q_scale = HEAD_DIM ** -0.5 * (LOG2_E if transposed else 1.0)
    qg = qg_ref[...]
    for s in range(N_HEADS // 2):
        slab = rope(headnorm(q_ref[:, s * LANES:(s + 1) * LANES], qg)) * q_scale
        swapped = pltpu.roll(slab, HEAD_DIM, 1)
        grp = (2 * s) // (N_HEADS // N_KV_HEADS)
        for hh in range(2):
            h = 2 * s + hh
            src = slab if hh == grp else swapped
            keep = low_half if grp == 0 else jnp.logical_not(low_half)
            put(qp_ref, h, jnp.where(keep, src, 0.0))

    k = rope(headnorm(k_ref[...], kg_ref[...]))
    kf_ref[...] = k
    kb_ref[...] = k.astype(BF16)
    if transposed:
        vb_ref[0:LANES, :] = v_ref[...].T.astype(BF16)
        vb_ref[LANES:V_T_ROWS, :] = jnp.ones((V_T_ROWS - LANES, tm), BF16)
    else:
        vb_ref[...] = v_ref[...].astype(BF16)

    for s in range(N_IDX_HEADS // 2):
        slab = rope(qi_ref[:, s * LANES:(s + 1) * LANES]) * (IDX_DIM ** -0.5)
        swapped = pltpu.roll(slab, HEAD_DIM, 1)
        for hh in range(2):
            h = 2 * s + hh
            src = slab if hh == 0 else swapped
            put(qip_ref, h, jnp.where(low_half, src, 0.0))

    kw = kw_ref[...]
    ki = rope(kw)
    kif_ref[...] = ki
    kib_ref[...] = jnp.where(low_half, ki, 0.0).astype(BF16)
    wi = pltpu.roll(kw, HEAD_DIM, 1) * (N_IDX_HEADS ** -0.5)
    if transposed:
        wt = wi.T[0:SUBLANES, :]
        for blk in range(tm // Q_BLOCK):
            wo_ref[blk] = wt[:, blk * Q_BLOCK:(blk + 1) * Q_BLOCK]
    else:
        wo_ref[...] = wi


def _prep(proj, cos, sin, qg, kg, transposed):
    m = proj.shape[0]
    tm = min(m, 512)
    row = lambda w, off: pl.BlockSpec((tm, w), lambda i: (i, off // w))
    full = lambda w: pl.BlockSpec((tm, w), lambda i: (i, 0))
    const = pl.BlockSpec((1, LANES), lambda i: (0, 0))
    if transposed:
        qblocks = tm // Q_BLOCK
        feat = lambda r: pl.BlockSpec((qblocks, r, Q_BLOCK), lambda i: (i, 0, 0))
        fshape = lambda r, dt: jax.ShapeDtypeStruct((m // Q_BLOCK, r, Q_BLOCK), dt)
        vspec = pl.BlockSpec((V_T_ROWS, tm), lambda i: (0, i))
        vshape = jax.ShapeDtypeStruct((V_T_ROWS, m), BF16)
    else:
        feat = full
        fshape = lambda r, dt: jax.ShapeDtypeStruct((m, r), dt)
        vspec = full(LANES)
        vshape = jax.ShapeDtypeStruct((m, LANES), BF16)
    return pl.pallas_call(
        functools.partial(_prep_kernel, transposed=transposed),
        grid=(m // tm,),
        in_specs=[row(512, Q_OFF), row(LANES, K_OFF), row(LANES, V_OFF), row(256, QI_OFF), row(LANES, KW_OFF),
                  full(LANES), full(LANES), const, const],
        out_specs=[feat(N_HEADS * LANES), full(LANES), full(LANES), vspec,
                   feat(N_IDX_HEADS * LANES), full(LANES), full(LANES), feat(SUBLANES if transposed else LANES)],
        out_shape=[fshape(N_HEADS * LANES, BF16),
                   jax.ShapeDtypeStruct((m, LANES), F32),
                   jax.ShapeDtypeStruct((m, LANES), BF16),
                   vshape,
                   fshape(N_IDX_HEADS * LANES, BF16),
                   jax.ShapeDtypeStruct((m, LANES), F32),
                   jax.ShapeDtypeStruct((m, LANES), BF16),
                   fshape(SUBLANES if transposed else LANES, F32)],
        compiler_params=_cparams(("parallel",)),
        name="prep",
    )(proj, proj, proj, proj, proj, cos, sin, qg, kg)


def _fold_rows(f):
    while f.shape[0] > SUBLANES:
        half = f.shape[0] // 2
        f = f[:half] + f[half:]
    return f


def _dsa_kernel(qt_ref, qit_ref, wt_ref, k_ref, vt_ref, ki_ref, o_ref,
                keys_ref, tpos_ref, x_ref, m_ref, kmax_ref, acc_ref, *, k_sel, kc, nbits):
    qb = Q_BLOCK
    n_keys = k_ref.shape[0]
    assert n_keys < 2 ** 22
    j = pl.program_id(1)
    nck = (j * qb + qb + kc - 1) // kc
    qpos = j * qb + lax.broadcasted_iota(I32, (1, qb), 1)
    sub_k = lax.broadcasted_iota(I32, (kc, qb), 0)
    pair = lambda ref, s: jnp.concatenate(
        [ref[(2 * s) * LANES:(2 * s + 1) * LANES, :], ref[(2 * s + 1) * LANES:(2 * s + 2) * LANES, :]], axis=1)
    qi2 = [pair(qit_ref, s) for s in range(N_IDX_HEADS // 2)]
    q2 = [pair(qt_ref, s) for s in range(N_HEADS // 2)]

    def score_body(c, carry):
        off = pl.multiple_of(c * kc, kc)
        kic = ki_ref[pl.ds(off, kc), :]
        acc = jnp.zeros((kc, qb), F32)
        for s in range(N_IDX_HEADS // 2):
            s2 = jnp.dot(kic, qi2[s], preferred_element_type=F32)
            for hh in range(2):
                h = 2 * s + hh
                acc = acc + jnp.maximum(s2[:, hh * qb:(hh + 1) * qb], 0.0) * wt_ref[h:h + 1, :]
        kpos = off + sub_k
        key = _sortable(jnp.where(kpos <= qpos, acc, -jnp.inf))
        key = jnp.where(key == 0, (n_keys - 1) - kpos, jnp.where(key == -1, -1 - kpos, key))
        keys_ref[pl.ds(off, kc), :] = key
        return carry

    lax.fori_loop(0, nck, score_body, 0)

    def count_where(pred, ref=keys_ref):
        def body(c, acc):
            off = pl.multiple_of(c * kc, kc)
            return acc + _fold_rows(jnp.where(pred(ref[pl.ds(off, kc), :], off), 1.0, 0.0))

        acc = lax.fori_loop(0, nck, body, jnp.zeros((SUBLANES, qb), F32))
        return jnp.sum(acc, axis=0, keepdims=True)

    def bit_body(i, carry):
        t, cnt_acc, cnt_rej = carry
        cand = t + lax.shift_left(jnp.int32(1), 31 - i)
        cnt = count_where(lambda kk, off: kk >= cand)
        ok = cnt >= k_sel
        return jnp.where(ok, cand, t), jnp.where(ok, cnt, cnt_acc), jnp.where(ok, cnt_rej, cnt)

    zero = jnp.zeros((1, qb), F32)
    thr, cnt_ge, cnt_gt = lax.fori_loop(0, 32, bit_body, (jnp.full((1, qb), INT_MIN, I32), zero, zero))
    thr = jnp.maximum(thr, KEY_NEG_INF)
    finite_thr = thr > KEY_NEG_INF
    need = k_sel - cnt_gt
    tie = jnp.logical_and(cnt_ge > k_sel, finite_thr)
    x_default = jnp.where(finite_thr, INT_MAX, -1)
    x_ref[...] = jnp.broadcast_to(x_default, x_ref.shape)

    @pl.when(jnp.max(jnp.where(tie, 1.0, 0.0)) > 0.0)
    def _():
        def tie_pos_body(c, carry):
            off = pl.multiple_of(c * kc, kc)
            tpos_ref[pl.ds(off, kc), :] = jnp.where(keys_ref[pl.ds(off, kc), :] == thr, off + sub_k, INT_MAX)
            return carry

        lax.fori_loop(0, nck, tie_pos_body, 0)

        def xbody(i, x):
            cand = x + lax.shift_left(jnp.int32(1), nbits - 1 - i)
            cnt = count_where(lambda tp, off: tp < cand, tpos_ref)
            return jnp.where(cnt < need, cand, x)

        x = lax.fori_loop(0, nbits, xbody, jnp.zeros((1, qb), I32))
        x_ref[...] = jnp.broadcast_to(jnp.where(tie, x, x_default), x_ref.shape)

    xcut = x_ref[0:1, :]
    hpg = N_HEADS // N_KV_HEADS

    @pl.when(j == 0)
    def _():
        r = lax.broadcasted_iota(I32, (LANES, LANES), 0) // HEAD_DIM
        cc = lax.broadcasted_iota(I32, (LANES, LANES), 1) // HEAD_DIM
        seg = jnp.where(r == cc, 1.0, 0.0).astype(BF16)

        def body(c, mx):
            kf = k_ref[pl.ds(pl.multiple_of(c * kc, kc), kc), :].astype(F32)
            gs = jnp.dot((kf * kf).astype(BF16), seg, preferred_element_type=F32)
            while gs.shape[0] > SUBLANES:
                half = gs.shape[0] // 2
                gs = jnp.maximum(gs[:half], gs[half:])
            return jnp.maximum(mx, gs)

        mx = lax.fori_loop(0, k_ref.shape[0] // kc, body, jnp.zeros((SUBLANES, LANES), F32))
        mx = jnp.max(mx, axis=0, keepdims=True)
        lane = lax.broadcasted_iota(I32, (1, LANES), 1)
        other = pltpu.roll(mx, HEAD_DIM, 1)
        kmax_ref[0:1, :] = jnp.where(lane < HEAD_DIM, mx, other)
        kmax_ref[1:2, :] = jnp.where(lane < HEAD_DIM, other, mx)

    def chunk(c):
        off = pl.multiple_of(c * kc, kc)
        kk = keys_ref[pl.ds(off, kc), :]
        sel = jnp.logical_or(kk > thr, jnp.logical_and(kk == thr, off + sub_k <= xcut))
        return sel, k_ref[pl.ds(off, kc), :], vt_ref[:, pl.ds(off, kc)]

    bound = []
    for h in range(N_HEADS):
        qh = qt_ref[h * LANES:(h + 1) * LANES, :].astype(F32)
        qsq = jnp.sum(qh * qh, axis=0, keepdims=True)
        bound.append(jnp.sqrt(qsq * kmax_ref[h // hpg:h // hpg + 1, :]))
    acc_ref[...] = jnp.zeros(acc_ref.shape, F32)

    def fast_body(c, carry):
        sel, kch, vch = chunk(c)
        n_pair = N_HEADS // 2
        s2 = jnp.dot(kch, q2[0], preferred_element_type=F32)
        for s in range(n_pair):
            s2_next = jnp.dot(kch, q2[s + 1], preferred_element_type=F32) if s + 1 < n_pair else None
            ps = [jnp.exp2(jnp.where(sel, s2[:, hh * qb:(hh + 1) * qb] - bound[2 * s + hh], -jnp.inf)).astype(BF16)
                  for hh in range(2)]
            acc_ref[s] += jnp.dot(vch, jnp.concatenate(ps, axis=1), preferred_element_type=F32)
            s2 = s2_next
        return carry

    lax.fori_loop(0, nck, fast_body, 0)

    dens = jnp.concatenate([acc_ref[s, LANES:LANES + 1, :] for s in range(N_HEADS // 2)], axis=0)

    @pl.when(jnp.logical_not(jnp.min(dens) >= 1e-20))
    def _():
        m_ref[...] = jnp.full(m_ref.shape, NEG_BIG, F32)
        acc_ref[...] = jnp.zeros(acc_ref.shape, F32)

        def exact_body(c, carry):
            sel, kch, vch = chunk(c)
            for s in range(N_HEADS // 2):
                s2 = jnp.dot(kch, q2[s], preferred_element_type=F32)
                ps, alphas = [], []
                for hh in range(2):
                    h = 2 * s + hh
                    sh = jnp.where(sel, s2[:, hh * qb:(hh + 1) * qb], -jnp.inf)
                    m_old = m_ref[h:h + 1, :]
                    m_new = jnp.maximum(m_old, jnp.max(sh, axis=0, keepdims=True))
                    m_ref[h:h + 1, :] = m_new
                    alphas.append(jnp.exp2(m_old - m_new))
                    ps.append(jnp.exp2(sh - m_new).astype(BF16))
                o2 = jnp.dot(vch, jnp.concatenate(ps, axis=1), preferred_element_type=F32)
                acc_ref[s] = jnp.concatenate(alphas, axis=1) * acc_ref[s] + o2
            return carry

        lax.fori_loop(0, nck, exact_body, 0)

    for s in range(N_HEADS // 2):
        g = (2 * s) // hpg
        num = acc_ref[s, g * HEAD_DIM:(g + 1) * HEAD_DIM, :]
        out2 = num / acc_ref[s, LANES:LANES + 1, :]
        o_ref[:, s * LANES:(s + 1) * LANES] = jnp.concatenate([out2[:, :qb], out2[:, qb:]], axis=0).T


def _dsa_prompt(qt, qit, wt, kb, vt, kib, nb, seq):
    k_sel = min(TOPK_MAX, seq // 4)
    kc = next(c for c in (512, Q_BLOCK) if seq % c == 0)
    nbits = max(1, (seq - 1).bit_length())
    nq = seq // Q_BLOCK
    r3 = lambda a: a.reshape(nb, seq, a.shape[-1])
    qcol = lambda r: pl.BlockSpec((None, r, Q_BLOCK), lambda b, j: (b * nq + j, 0, 0))
    seqblk = pl.BlockSpec((None, seq, LANES), lambda b, j: (b, 0, 0))
    out = pl.pallas_call(
        functools.partial(_dsa_kernel, k_sel=k_sel, kc=kc, nbits=nbits),
        grid=(nb, nq),
        in_specs=[qcol(N_HEADS * LANES), qcol(N_IDX_HEADS * LANES), qcol(SUBLANES), seqblk,
                  pl.BlockSpec((V_T_ROWS, seq), lambda b, j: (0, b)), seqblk],
        out_specs=pl.BlockSpec((None, Q_BLOCK, N_HEADS * HEAD_DIM), lambda b, j: (b, j, 0)),
        out_shape=jax.ShapeDtypeStruct((nb, seq, N_HEADS * HEAD_DIM), F32),
        scratch_shapes=[pltpu.VMEM((seq, Q_BLOCK), I32),
                        pltpu.VMEM((seq, Q_BLOCK), I32),
                        pltpu.VMEM((SUBLANES, Q_BLOCK), I32),
                        pltpu.VMEM((N_HEADS, Q_BLOCK), F32),
                        pltpu.VMEM((SUBLANES, LANES), F32),
                        pltpu.VMEM((N_HEADS // 2, V_T_ROWS, 2 * Q_BLOCK), F32)],
        compiler_params=_cparams(("arbitrary", "arbitrary")),
        name="dsa_prompt",
    )(qt, qit, wt, r3(kb), vt, r3(kib))
    return out.reshape(nb * seq, N_HEADS * HEAD_DIM)


def _conva_kernel(cb_ref, cc_ref, cx_ref, cch_ref, cxh_ref, w_ref, o_ref, st_ref, *, tiles_per_seq):
    tm = cb_ref.shape[0]
    first = (pl.program_id(0) % tiles_per_seq) == 0
    u = cc_ref[...] * cx_ref[...]
    uh = jnp.where(first, 0.0, cch_ref[...] * cxh_ref[...])
    ext = jnp.concatenate([uh, u], axis=0)
    conv = (w_ref[2:3, :] * u + w_ref[1:2, :] * ext[SUBLANES - 1:SUBLANES - 1 + tm]
            + w_ref[0:1, :] * ext[SUBLANES - 2:SUBLANES - 2 + tm])
    o_ref[...] = cb_ref[...] * conv
    st_ref[...] = u[tm - 2:tm, :]


def _conva_prompt(proj, conv_w, nb, seq):
    m = proj.shape[0]
    tm = min(seq, 512)
    tps = seq // tm
    col = lambda off: pl.BlockSpec((tm, D_CONV), lambda i: (i, off // D_CONV))
    halo = lambda off: pl.BlockSpec(
        (SUBLANES, D_CONV), lambda i: (jnp.maximum(i * (tm // SUBLANES) - 1, 0), off // D_CONV))
    return pl.pallas_call(
        functools.partial(_conva_kernel, tiles_per_seq=tps),
        grid=(m // tm,),
        in_specs=[col(CB_OFF), col(CC_OFF), col(CX_OFF), halo(CC_OFF), halo(CX_OFF),
                  pl.BlockSpec((3, D_CONV), lambda i: (0, 0))],
        out_specs=[pl.BlockSpec((tm, D_CONV), lambda i: (i, 0)),
                   pl.BlockSpec((None, 2, D_CONV), lambda i: (i // tps, 0, 0))],
        out_shape=[jax.ShapeDtypeStruct((m, D_CONV), F32),
                   jax.ShapeDtypeStruct((nb, 2, D_CONV), F32)],
        compiler_params=_cparams(("arbitrary",)),
        name="conva_prompt",
    )(proj, proj, proj, proj, proj, conv_w)


def _ssd_kernel(xs_ref, bc_ref, dt_ref, xh_ref, bh_ref, cwx_ref, cwb_ref, cbx_ref, cbb_ref,
                dtb_ref, alog_ref, dsk_ref, y_ref, hout_ref, h_ref, *, nchunk):
    cl = SSM_CHUNK
    c = pl.program_id(1)
    first = c == 0

    @pl.when(first)
    def _():
        h_ref[...] = jnp.zeros(h_ref.shape, F32)

    def conv(cur, halo, w_ref, b_ref):
        ext = jnp.concatenate([jnp.where(first, 0.0, halo), cur], axis=0)
        out = (w_ref[3:4, :] * cur + w_ref[2:3, :] * ext[SUBLANES - 1:SUBLANES - 1 + cl]
               + w_ref[1:2, :] * ext[SUBLANES - 2:SUBLANES - 2 + cl]
               + w_ref[0:1, :] * ext[SUBLANES - 3:SUBLANES - 3 + cl] + b_ref[...])
        return _silu(out)

    xs = conv(xs_ref[...], xh_ref[...], cwx_ref, cbx_ref)
    bc = conv(bc_ref[...], bh_ref[...], cwb_ref, cbb_ref)
    dt = _softplus(dt_ref[...] + dtb_ref[...])
    a = dt * (-jnp.exp(alog_ref[...]))
    ri = lax.broadcasted_iota(I32, (cl, cl), 0)
    ci = lax.broadcasted_iota(I32, (cl, cl), 1)
    causal = ri >= ci
    cs = jnp.dot(jnp.where(causal, 1.0, 0.0), a, preferred_element_type=F32,
                 precision=lax.Precision.HIGHEST)
    cs_t = cs.T
    lane = lax.broadcasted_iota(I32, (cl, LANES), 1)
    lo = lane < SSM_HEAD_DIM
    rows_lo = lax.broadcasted_iota(I32, (LANES, 1), 0) < SSM_HEAD_DIM
    heads_per_group = SSM_HEADS // SSM_GROUPS

    cb = []
    for g in range(SSM_GROUPS):
        bg = bc[:, g * SSM_STATE:(g + 1) * SSM_STATE].astype(BF16)
        cg = bc[:, (SSM_GROUPS + g) * SSM_STATE:(SSM_GROUPS + g + 1) * SSM_STATE].astype(BF16)
        cb.append((bg, cg, lax.dot_general(cg, bg, NT_DIMS, preferred_element_type=F32)))

    for s in range(SSM_HEADS // 2):
        h0, h1 = 2 * s, 2 * s + 1
        bg, cg, cbg = cb[h0 // heads_per_group]
        sl = slice(s * LANES, (s + 1) * LANES)
        xs_s = xs[:, sl]
        col0, col1 = cs[:, h0:h0 + 1], cs[:, h1:h1 + 1]
        last0, last1 = cs[cl - 1:cl, h0:h0 + 1], cs[cl - 1:cl, h1:h1 + 1]
        xdt = xs_s * jnp.where(lo, dt[:, h0:h0 + 1], dt[:, h1:h1 + 1])
        xdt_b = xdt.astype(BF16)
        m0 = (cbg * jnp.where(causal, jnp.exp(col0 - cs_t[h0:h0 + 1, :]), 0.0)).astype(BF16)
        m1 = (cbg * jnp.where(causal, jnp.exp(col1 - cs_t[h1:h1 + 1, :]), 0.0)).astype(BF16)
        y_diag = jnp.where(lo, jnp.dot(m0, xdt_b, preferred_element_type=F32),
                           jnp.dot(m1, xdt_b, preferred_element_type=F32))
        hs = h_ref[sl, :]
        y_off = lax.dot_general(cg, hs.astype(BF16), NT_DIMS, preferred_element_type=F32)
        y_off = y_off * jnp.where(lo, jnp.exp(col0), jnp.exp(col1))
        y_ref[:, sl] = y_diag + y_off + dsk_ref[:, sl] * xs_s
        xw = xdt * jnp.where(lo, jnp.exp(last0 - col0), jnp.exp(last1 - col1))
        st = jnp.dot(xw.T.astype(BF16), bg, preferred_element_type=F32)
        h_ref[sl, :] = hs * jnp.where(rows_lo, jnp.exp(last0), jnp.exp(last1)) + st

    @pl.when(c == nchunk - 1)
    def _():
        hout_ref[...] = h_ref[...]


def _ssd_prompt(proj, cw, cbias, dtb, alog, dsk, nb, seq):
    m = proj.shape[0]
    cl = SSM_CHUNK
    nchunk = seq // cl
    blk = lambda w, off: pl.BlockSpec((cl, w), lambda b, c: (b * nchunk + c, off // w))
    halo = lambda off: pl.BlockSpec(
        (SUBLANES, 512), lambda b, c: (jnp.maximum((b * nchunk + c) * (cl // SUBLANES) - 1, 0), off // 512))
    const = lambda r, w, j: pl.BlockSpec((r, w), lambda b, c: (0, j))
    y, hout = pl.pallas_call(
        functools.partial(_ssd_kernel, nchunk=nchunk),
        grid=(nb, nchunk),
        in_specs=[blk(512, XS_OFF), blk(512, BC_OFF), blk(LANES, DT_OFF), halo(XS_OFF), halo(BC_OFF),
                  const(4, 512, 0), const(4, 512, 1), const(1, 512, 0), const(1, 512, 1),
                  const(1, LANES, 0), const(1, LANES, 0), const(1, 512, 0)],
        out_specs=[pl.BlockSpec((cl, SSM_INNER), lambda b, c: (b * nchunk + c, 0)),
                   pl.BlockSpec((None, SSM_INNER, SSM_STATE), lambda b, c: (b, 0, 0))],
        out_shape=[jax.ShapeDtypeStruct((m, SSM_INNER), F32),
                   jax.ShapeDtypeStruct((nb, SSM_INNER, SSM_STATE), F32)],
        scratch_shapes=[pltpu.VMEM((SSM_INNER, SSM_STATE), F32)],
        compiler_params=_cparams(("parallel", "arbitrary")),
        name="ssd_prompt",
    )(proj, proj, proj, proj, proj, cw, cw, cbias, cbias, dtb, alog, dsk)
    return y, hout


def _merge_kernel(x_ref, pa_ref, at_ref, ys_ref, z_ref, g0_ref, g1_ref, g2_ref, sn_ref, bg_ref,
                  wpc_ref, wpa_ref, wps_ref, wo_ref, o_ref):
    ssd = _rms(ys_ref[...] * _silu(z_ref[...]), sn_ref[...])
    ya = jnp.dot(pa_ref[...].astype(BF16), wpc_ref[...], preferred_element_type=F32)
    yb = jnp.dot(at_ref[...].astype(BF16), wpa_ref[...], preferred_element_type=F32)
    yc = jnp.dot(ssd.astype(BF16), wps_ref[...], preferred_element_type=F32)
    merged = (_sigmoid(g0_ref[...] + bg_ref[0:1, :]) * ya + _sigmoid(g1_ref[...] + bg_ref[1:2, :]) * yb
              + _sigmoid(g2_ref[...] + bg_ref[2:3, :]) * yc)
    o_ref[...] = x_ref[...] + jnp.dot(merged.astype(BF16), wo_ref[...], preferred_element_type=F32)


def _merge(x, pre_a, attn, y_ssd, proj, ssm_norm, b_gate, w_pc, w_pa, w_ps, w_o):
    m = x.shape[0]
    tm = min(m, 256)
    row = lambda w: pl.BlockSpec((tm, w), lambda i: (i, 0))
    pcol = lambda w, off: pl.BlockSpec((tm, w), lambda i: (i, off // w))
    const = lambda r, w: pl.BlockSpec((r, w), lambda i: (0, 0))
    return pl.pallas_call(
        _merge_kernel,
        grid=(m // tm,),
        in_specs=[row(D_MODEL), row(512), row(512), row(512), pcol(512, Z_OFF),
                  pcol(D_MODEL, G_OFF), pcol(D_MODEL, G_OFF + D_MODEL), pcol(D_MODEL, G_OFF + 2 * D_MODEL),
                  const(1, 512), const(3, D_MODEL),
                  const(512, D_MODEL), const(512, D_MODEL), const(512, D_MODEL), const(D_MODEL, D_MODEL)],
        out_specs=row(D_MODEL),
        out_shape=jax.ShapeDtypeStruct((m, D_MODEL), F32),
        compiler_params=_cparams(("parallel",)),
        name="merge",
    )(x, pre_a, attn, y_ssd, proj, proj, proj, proj, ssm_norm, b_gate, w_pc, w_pa, w_ps, w_o)


def _ffn_kernel(x_ref, g_ref, wr_ref, wg_ref, wu_ref, wd_ref, o_ref, h_ref, acc_ref, comb_ref,
                *, routed, n_e, n_f):
    e = pl.program_id(1)
    j = pl.program_id(2)
    tm = x_ref.shape[0]

    @pl.when(jnp.logical_and(e == 0, j == 0))
    def _():
        hf = _rms(x_ref[...], g_ref[...])
        h_ref[...] = hf.astype(BF16)
        acc_ref[...] = jnp.zeros(acc_ref.shape, F32)
        if routed:
            lane = lax.broadcasted_iota(I32, (tm, LANES), 1).astype(F32)
            logits = jnp.dot(hf, wr_ref[...], preferred_element_type=F32, precision=lax.Precision.HIGHEST)
            logits = jnp.where(lane < n_e, logits, -jnp.inf)
            m1 = jnp.max(logits, axis=1, keepdims=True)
            i1 = jnp.min(jnp.where(logits == m1, lane, float(LANES)), axis=1, keepdims=True)
            rest = jnp.where(lane == i1, -jnp.inf, logits)
            m2 = jnp.max(rest, axis=1, keepdims=True)
            i2 = jnp.min(jnp.where(rest == m2, lane, float(LANES)), axis=1, keepdims=True)
            e2 = jnp.exp(m2 - m1)
            den = 1.0 + e2
            comb_ref[...] = jnp.where(lane == i1, 1.0 / den, 0.0) + jnp.where(lane == i2, e2 / den, 0.0)

    h = h_ref[...]
    gate = jnp.dot(h, wg_ref[...], preferred_element_type=F32)
    up = jnp.dot(h, wu_ref[...], preferred_element_type=F32)
    act = _silu(gate) * up
    if routed:
        lane = lax.broadcasted_iota(I32, (tm, LANES), 1)
        act = act * jnp.sum(jnp.where(lane == e, comb_ref[...], 0.0), axis=1, keepdims=True)
    acc_ref[...] += jnp.dot(act.astype(BF16), wd_ref[...], preferred_element_type=F32)

    @pl.when(jnp.logical_and(e == n_e - 1, j == n_f - 1))
    def _():
        o_ref[...] = x_ref[...] + acc_ref[...]


def _ffn(x, g, w_router, w_gate, w_up, w_down, routed):
    m = x.shape[0]
    n_e, _, d_ff = w_gate.shape
    tm = min(m, 1024)
    tf = 256
    n_f = d_ff // tf
    return pl.pallas_call(
        functools.partial(_ffn_kernel, routed=routed, n_e=n_e, n_f=n_f),
        grid=(m // tm, n_e, n_f),
        in_specs=[pl.BlockSpec((tm, D_MODEL), lambda i, e, j: (i, 0)),
                  pl.BlockSpec((1, D_MODEL), lambda i, e, j: (0, 0)),
                  pl.BlockSpec((D_MODEL, LANES), lambda i, e, j: (0, 0)),
                  pl.BlockSpec((None, D_MODEL, tf), lambda i, e, j: (e, 0, j)),
                  pl.BlockSpec((None, D_MODEL, tf), lambda i, e, j: (e, 0, j)),
                  pl.BlockSpec((None, tf, D_MODEL), lambda i, e, j: (e, j, 0))],
        out_specs=pl.BlockSpec((tm, D_MODEL), lambda i, e, j: (i, 0)),
        out_shape=jax.ShapeDtypeStruct((m, D_MODEL), F32),
        scratch_shapes=[pltpu.VMEM((tm, D_MODEL), BF16), pltpu.VMEM((tm, D_MODEL), F32),
                        pltpu.VMEM((tm, LANES), F32)],
        compiler_params=_cparams(("parallel", "arbitrary", "arbitrary")),
        name="moe" if routed else "ffn",
    )(x, g, w_router, w_gate, w_up, w_down)


MOE_BLOCK = 2048
MOE_TILE = 256
TOP_K = 2


def _moe_tiles(tb):
    return TOP_K * tb // MOE_TILE + N_EXPERTS


def _route_kernel(x_ref, g_ref, wr_ref, hb_ref, row_ref, col_ref, meta_ref):
    tb = x_ref.shape[0]
    hf = _rms(x_ref[...], g_ref[...])
    hb_ref[...] = hf.astype(BF16)
    lane_i = lax.broadcasted_iota(I32, (tb, LANES), 1)
    lane = lane_i.astype(F32)
    logits = jnp.dot(hf, wr_ref[...], preferred_element_type=F32, precision=lax.Precision.HIGHEST)
    logits = jnp.where(lane_i < N_EXPERTS, logits, -jnp.inf)
    m1 = jnp.max(logits, axis=1, keepdims=True)
    i1 = jnp.min(jnp.where(logits == m1, lane, float(LANES)), axis=1, keepdims=True)
    rest = jnp.where(lane == i1, -jnp.inf, logits)
    m2 = jnp.max(rest, axis=1, keepdims=True)
    i2 = jnp.min(jnp.where(rest == m2, lane, float(LANES)), axis=1, keepdims=True)
    e2 = jnp.exp(m2 - m1)
    p1 = 1.0 / (1.0 + e2)
    p2 = e2 / (1.0 + e2)
    oh1 = lane == i1
    oh2 = lane == i2
    both = jnp.where(jnp.logical_or(oh1, oh2), 1.0, 0.0).astype(BF16)

    sb = MOE_TILE
    ri = lax.broadcasted_iota(I32, (sb, sb), 0)
    ci = lax.broadcasted_iota(I32, (sb, sb), 1)
    strict = jnp.where(ri > ci, 1.0, 0.0).astype(BF16)
    carry = jnp.zeros((1, LANES), F32)
    ranks = []
    for s in range(tb // sb):
        rows = both[s * sb:(s + 1) * sb]
        ranks.append(jnp.dot(strict, rows, preferred_element_type=F32) + carry)
        carry = carry + jnp.sum(rows.astype(F32), axis=0, keepdims=True)
    rank = jnp.concatenate(ranks, axis=0)
    seg = jnp.floor((carry + (MOE_TILE - 1)) * (1.0 / MOE_TILE)) * MOE_TILE
    ui = lax.broadcasted_iota(I32, (LANES, LANES), 0)
    uj = lax.broadcasted_iota(I32, (LANES, LANES), 1)
    before = jnp.where(ui < uj, 1.0, 0.0)
    off = jnp.dot(jnp.broadcast_to(seg, (SUBLANES, LANES)), before, preferred_element_type=F32,
                  precision=lax.Precision.HIGHEST)[0:1, :]
    dest = off + rank
    d1 = jnp.sum(jnp.where(oh1, dest, 0.0), axis=1, keepdims=True)
    d2 = jnp.sum(jnp.where(oh2, dest, 0.0), axis=1, keepdims=True)
    col = jnp.where(lane_i == 0, d1, jnp.where(lane_i == 1, d2, jnp.where(lane_i == 2, p1,
                                                                          jnp.where(lane_i == 3, p2, 0.0))))
    col_ref[...] = col
    row_ref[...] = col.T[0:SUBLANES, :]

    end_rows = jnp.broadcast_to(off + seg, (LANES, LANES)).T
    start = (lax.broadcasted_iota(I32, (LANES, LANES), 1) * MOE_TILE).astype(F32)
    is_expert = lax.broadcasted_iota(I32, (LANES, LANES), 0) < N_EXPERTS
    done = jnp.where(jnp.logical_and(is_expert, end_rows <= start), 1.0, 0.0)
    expert = jnp.minimum(jnp.sum(done, axis=0, keepdims=True), float(N_EXPERTS - 1))
    total = jnp.sum(jnp.where(lane_i[0:1] < N_EXPERTS, seg, 0.0), axis=1, keepdims=True)
    valid = jnp.where(start[0:1] < total, 1.0, 0.0)
    sub = lax.broadcasted_iota(I32, (SUBLANES, LANES), 0)
    meta_ref[...] = jnp.where(sub == 0, expert, jnp.where(sub == 1, valid, 0.0)).astype(I32)


def _route(x, g, w_router, tb):
    m = x.shape[0]
    nblk = m // tb
    return pl.pallas_call(
        _route_kernel,
        grid=(nblk,),
        in_specs=[pl.BlockSpec((tb, D_MODEL), lambda b: (b, 0)),
                  pl.BlockSpec((1, D_MODEL), lambda b: (0, 0)),
                  pl.BlockSpec((D_MODEL, LANES), lambda b: (0, 0))],
        out_specs=[pl.BlockSpec((tb, D_MODEL), lambda b: (b, 0)),
                   pl.BlockSpec((SUBLANES, tb), lambda b: (0, b)),
                   pl.BlockSpec((tb, LANES), lambda b: (b, 0)),
                   pl.BlockSpec((None, SUBLANES, LANES), lambda b: (b, 0, 0))],
        out_shape=[jax.ShapeDtypeStruct((m, D_MODEL), BF16),
                   jax.ShapeDtypeStruct((SUBLANES, m), F32),
                   jax.ShapeDtypeStruct((m, LANES), F32),
                   jax.ShapeDtypeStruct((nblk, SUBLANES, LANES), I32)],
        compiler_params=_cparams(("parallel",)),
        name="moe_route",
    )(x, g, w_router)


def _gffn_kernel(te_ref, tv_ref, hb_ref, row_ref, wg_ref, wu_ref, wd_ref, ys_ref, xg_ref, acc_ref, *, nt, n_f):
    del te_ref
    b, t, j = pl.program_id(0), pl.program_id(1), pl.program_id(2)
    valid = tv_ref[b * nt + t] > 0
    tb = hb_ref.shape[0]

    @pl.when(jnp.logical_and(valid, j == 0))
    def _():
        slot = (t * MOE_TILE + lax.broadcasted_iota(I32, (MOE_TILE, tb), 0)).astype(F32)
        hit = jnp.logical_or(row_ref[0:1, :] == slot, row_ref[1:2, :] == slot)
        onehot = jnp.where(hit, 1.0, 0.0).astype(BF16)
        xg_ref[...] = jnp.dot(onehot, hb_ref[...], preferred_element_type=F32).astype(BF16)
        acc_ref[...] = jnp.zeros(acc_ref.shape, F32)

    @pl.when(valid)
    def _():
        xg = xg_ref[...]
        gate = jnp.dot(xg, wg_ref[...], preferred_element_type=F32)
        up = jnp.dot(xg, wu_ref[...], preferred_element_type=F32)
        acc_ref[...] += jnp.dot((_silu(gate) * up).astype(BF16), wd_ref[...], preferred_element_type=F32)

    @pl.when(j == n_f - 1)
    def _():
        ys_ref[...] = jnp.where(valid, acc_ref[...], 0.0).astype(BF16)


def _gffn(tile_expert, tile_valid, hb, rowinfo, w_gate, w_up, w_down, tb):
    m = hb.shape[0]
    nblk = m // tb
    nt = _moe_tiles(tb)
    d_ff = w_gate.shape[2]
    tf = d_ff // 2
    n_f = d_ff // tf
    def fidx(b, t, j, te, tv):
        odd = t % 2 == 1
        return jnp.where(tv[b * nt + t] > 0, jnp.where(odd, n_f - 1 - j, j), jnp.where(odd, n_f - 1, 0))

    grid_spec = pltpu.PrefetchScalarGridSpec(
        num_scalar_prefetch=2,
        grid=(nblk, nt, n_f),
        in_specs=[pl.BlockSpec((tb, D_MODEL), lambda b, t, j, te, tv: (b, 0)),
                  pl.BlockSpec((SUBLANES, tb), lambda b, t, j, te, tv: (0, b)),
                  pl.BlockSpec((None, D_MODEL, tf), lambda b, t, j, te, tv: (te[b * nt + t], 0, fidx(b, t, j, te, tv))),
                  pl.BlockSpec((None, D_MODEL, tf), lambda b, t, j, te, tv: (te[b * nt + t], 0, fidx(b, t, j, te, tv))),
                  pl.BlockSpec((None, tf, D_MODEL), lambda b, t, j, te, tv: (te[b * nt + t], fidx(b, t, j, te, tv), 0))],
        out_specs=pl.BlockSpec((MOE_TILE, D_MODEL), lambda b, t, j, te, tv: (b * nt + t, 0)),
        scratch_shapes=[pltpu.VMEM((MOE_TILE, D_MODEL), BF16), pltpu.VMEM((MOE_TILE, D_MODEL), F32)],
    )
    return pl.pallas_call(
        functools.partial(_gffn_kernel, nt=nt, n_f=n_f),
        grid_spec=grid_spec,
        out_shape=jax.ShapeDtypeStruct((nblk * nt * MOE_TILE, D_MODEL), BF16),
        compiler_params=_cparams(("parallel", "arbitrary", "arbitrary")),
        name="moe_experts",
    )(tile_expert, tile_valid, hb, rowinfo, w_gate, w_up, w_down)


def _combine_kernel(tv_ref, x_ref, col_ref, ys_ref, o_ref, *, nt):
    b, t = pl.program_id(0), pl.program_id(1)

    @pl.when(t == 0)
    def _():
        o_ref[...] = x_ref[...]

    @pl.when(tv_ref[b * nt + t] > 0)
    def _():
        tb = x_ref.shape[0]
        slot = (t * MOE_TILE + lax.broadcasted_iota(I32, (tb, MOE_TILE), 1)).astype(F32)
        weight = (jnp.where(col_ref[:, 0:1] == slot, col_ref[:, 2:3], 0.0)
                  + jnp.where(col_ref[:, 1:2] == slot, col_ref[:, 3:4], 0.0))
        o_ref[...] += jnp.dot(weight.astype(BF16), ys_ref[...], preferred_element_type=F32)


def _combine(tile_valid, x, colinfo, ys, tb):
    m = x.shape[0]
    nblk = m // tb
    nt = _moe_tiles(tb)
    grid_spec = pltpu.PrefetchScalarGridSpec(
        num_scalar_prefetch=1,
        grid=(nblk, nt),
        in_specs=[pl.BlockSpec((tb, D_MODEL), lambda b, t, tv: (b, 0)),
                  pl.BlockSpec((tb, LANES), lambda b, t, tv: (b, 0)),
                  pl.BlockSpec((MOE_TILE, D_MODEL), lambda b, t, tv: (b * nt + t, 0))],
        out_specs=pl.BlockSpec((tb, D_MODEL), lambda b, t, tv: (b, 0)),
    )
    return pl.pallas_call(
        functools.partial(_combine_kernel, nt=nt),
        grid_spec=grid_spec,
        out_shape=jax.ShapeDtypeStruct((m, D_MODEL), F32),
        compiler_params=_cparams(("parallel", "arbitrary")),
        name="moe_combine",
    )(tile_valid, x, colinfo, ys)


def _moe_grouped(x, g, w_router, w_gate, w_up, w_down):
    m = x.shape[0]
    tb = min(MOE_BLOCK, m)
    nt = _moe_tiles(tb)
    hb, rowinfo, colinfo, meta = _route(x, g, w_router, tb)
    tile_expert = meta[:, 0, :nt].reshape(-1)
    tile_valid = meta[:, 1, :nt].reshape(-1)
    ys = _gffn(tile_expert, tile_valid, hb, rowinfo, w_gate, w_up, w_down, tb)
    return _combine(tile_valid, x, colinfo, ys, tb)


def _decmix_kernel(cb_ref, cc_ref, cx_ref, xs_ref, bc_ref, dt_ref, sc0_ref, sc1_ref,
                   sx0_ref, sx1_ref, sx2_ref, sb0_ref, sb1_ref, sb2_ref,
                   cw_ref, cwx_ref, cwb_ref, cbx_ref, cbb_ref, dtb_ref, alog_ref,
                   pa_ref, u_ref, xso_ref, bco_ref, dto_ref, ea_ref):
    u = cc_ref[...] * cx_ref[...]
    u_ref[...] = u
    pa_ref[...] = cb_ref[...] * (cw_ref[0:1, :] * sc0_ref[...] + cw_ref[1:2, :] * sc1_ref[...] + cw_ref[2:3, :] * u)
    xso_ref[...] = _silu(cwx_ref[0:1, :] * sx0_ref[...] + cwx_ref[1:2, :] * sx1_ref[...]
                         + cwx_ref[2:3, :] * sx2_ref[...] + cwx_ref[3:4, :] * xs_ref[...] + cbx_ref[...])
    bco_ref[...] = _silu(cwb_ref[0:1, :] * sb0_ref[...] + cwb_ref[1:2, :] * sb1_ref[...]
                         + cwb_ref[2:3, :] * sb2_ref[...] + cwb_ref[3:4, :] * bc_ref[...] + cbb_ref[...])
    dt = _softplus(dt_ref[...] + dtb_ref[...])
    dto_ref[...] = dt
    ea_ref[...] = jnp.exp(dt * (-jnp.exp(alog_ref[...])))


def _decmix(proj, st_conv, st_ssm_conv, conv_w, cw, cbias, dtb, alog):
    nb = proj.shape[0]
    pcol = lambda w, off: pl.BlockSpec((nb, w), lambda i: (0, off // w))
    full = lambda w: pl.BlockSpec((nb, w), lambda i: (0, 0))
    const = lambda r, w, j: pl.BlockSpec((r, w), lambda i: (0, j))
    sx = [st_ssm_conv[:, t, :512] for t in range(3)]
    sb = [st_ssm_conv[:, t, 512:] for t in range(3)]
    return pl.pallas_call(
        _decmix_kernel,
        grid=(1,),
        in_specs=[pcol(512, CB_OFF), pcol(512, CC_OFF), pcol(512, CX_OFF), pcol(512, XS_OFF), pcol(512, BC_OFF),
                  pcol(LANES, DT_OFF)] + [full(512)] * 8
                 + [const(3, 512, 0), const(4, 512, 0), const(4, 512, 1), const(1, 512, 0), const(1, 512, 1),
                    const(1, LANES, 0), const(1, LANES, 0)],
        out_specs=[full(512), full(512), full(512), full(512), full(LANES), full(LANES)],
        out_shape=[jax.ShapeDtypeStruct((nb, 512), F32)] * 4 + [jax.ShapeDtypeStruct((nb, LANES), F32)] * 2,
        compiler_params=_cparams(("arbitrary",)),
        name="decode_mix",
    )(proj, proj, proj, proj, proj, proj, st_conv[:, 0], st_conv[:, 1], *sx, *sb,
      conv_w, cw, cw, cbias, cbias, dtb, alog)


def _decssd_kernel(h_ref, dt_ref, xs_ref, b_ref, c_ref, ea_ref, dsk_ref, y_ref, ho_ref):
    xs = xs_ref[...]
    h_new = h_ref[...] * ea_ref[...] + (xs * dt_ref[...]) * b_ref[...]
    ho_ref[...] = h_new
    y_ref[...] = jnp.sum(h_new * c_ref[...], axis=-1, keepdims=True) + dsk_ref[...] * xs


def _decssd(h0, dt, xs, bh, ch, ea, dsk):
    nb = h0.shape[0]
    blk = lambda a, b: pl.BlockSpec((None, SSM_HEADS, a, b), lambda i: (i, 0, 0, 0))
    return pl.pallas_call(
        _decssd_kernel,
        grid=(nb,),
        in_specs=[blk(SSM_HEAD_DIM, SSM_STATE), blk(1, 1), blk(SSM_HEAD_DIM, 1), blk(1, SSM_STATE),
                  blk(1, SSM_STATE), blk(1, 1), pl.BlockSpec((SSM_HEADS, 1, 1), lambda i: (0, 0, 0))],
        out_specs=[blk(SSM_HEAD_DIM, 1), blk(SSM_HEAD_DIM, SSM_STATE)],
        out_shape=[jax.ShapeDtypeStruct((nb, SSM_HEADS, SSM_HEAD_DIM, 1), F32),
                   jax.ShapeDtypeStruct((nb, SSM_HEADS, SSM_HEAD_DIM, SSM_STATE), F32)],
        compiler_params=_cparams(("parallel",)),
        name="decode_ssd",
    )(h0, dt, xs, bh, ch, ea, dsk)


DECODE_PAGE_GROUP = 32


def _decscore_kernel(pt_ref, qi_ref, w_ref, *refs):
    del pt_ref
    kid_refs, o_ref = refs[:-1], refs[-1]
    kid = jnp.concatenate([r[...].astype(BF16) for r in kid_refs], axis=0)
    s = lax.dot_general(qi_ref[:, 0:IDX_DIM].astype(BF16), kid, NT_DIMS, preferred_element_type=F32)
    sc = jnp.maximum(s, 0.0) * w_ref[:, 0:1]
    o_ref[...] = jnp.sum(sc, axis=0, keepdims=True)


def _page_specs(block, layer, pg):
    zeros = (0,) * (len(block) - 2)
    return [pl.BlockSpec(block, lambda b, p, pt, t=t: (layer, pt[b, p * pg + t]) + zeros) for t in range(pg)]


def _decscore(page_table, qip, wrows, pool_kidx, layer):
    nb, n_pages = page_table.shape
    pg = math.gcd(DECODE_PAGE_GROUP, n_pages)
    grid_spec = pltpu.PrefetchScalarGridSpec(
        num_scalar_prefetch=1,
        grid=(nb, n_pages // pg),
        in_specs=[pl.BlockSpec((None, N_IDX_HEADS, LANES), lambda b, p, pt: (b, 0, 0)),
                  pl.BlockSpec((None, N_IDX_HEADS, PAGE_SIZE), lambda b, p, pt: (b, 0, 0))]
                 + _page_specs((None, None, PAGE_SIZE, IDX_DIM), layer, pg),
        out_specs=pl.BlockSpec((None, None, 1, pg * PAGE_SIZE), lambda b, p, pt: (b, p, 0, 0)),
    )
    out = pl.pallas_call(
        _decscore_kernel,
        grid_spec=grid_spec,
        out_shape=jax.ShapeDtypeStruct((nb, n_pages // pg, 1, pg * PAGE_SIZE), F32),
        compiler_params=_cparams(("parallel", "arbitrary")),
        name="decode_scores",
    )(page_table, qip, wrows, *([pool_kidx] * pg))
    return out.reshape(nb, n_pages * PAGE_SIZE)


def _decselect_kernel(sc_ref, qi_ref, ki_ref, w_ref, sel_ref, self_ref, keys_ref, *, k_sel, kc, nbits):
    nb, past = sc_ref.shape
    nck = past // kc
    lane_k = lax.broadcasted_iota(I32, (nb, kc), 1)
    ki = ki_ref[...].astype(F32)
    own = jnp.zeros((nb, 1), F32)
    for h in range(N_IDX_HEADS):
        s = jnp.sum(qi_ref[:, h * LANES:(h + 1) * LANES].astype(F32) * ki, axis=1, keepdims=True)
        own = own + jnp.maximum(s, 0.0) * w_ref[:, h:h + 1]
    own_key = _sortable(own)

    def key_body(c, carry):
        off = pl.multiple_of(c * kc, kc)
        keys_ref[:, pl.ds(off, kc)] = _sortable(sc_ref[:, pl.ds(off, kc)])
        return carry

    lax.fori_loop(0, nck, key_body, 0)

    def count_where(pred):
        def body(c, acc):
            off = pl.multiple_of(c * kc, kc)
            f = jnp.where(pred(keys_ref[:, pl.ds(off, kc)], off), 1.0, 0.0)
            part = f[:, 0:LANES]
            for t in range(1, kc // LANES):
                part = part + f[:, t * LANES:(t + 1) * LANES]
            return acc + part

        acc = lax.fori_loop(0, nck, body, jnp.zeros((nb, LANES), F32))
        return jnp.sum(acc, axis=1, keepdims=True)

    def bit_body(i, t):
        cand = t + lax.shift_left(jnp.int32(1), 31 - i)
        cnt = count_where(lambda kk, off: kk >= cand) + jnp.where(own_key >= cand, 1.0, 0.0)
        return jnp.where(cnt >= k_sel, cand, t)

    thr = lax.fori_loop(0, 32, bit_body, jnp.full((nb, 1), INT_MIN, I32))
    thr = jnp.maximum(thr, KEY_NEG_INF)
    finite_thr = thr > KEY_NEG_INF
    cnt_gt = count_where(lambda kk, off: kk > thr) + jnp.where(own_key > thr, 1.0, 0.0)
    need = k_sel - cnt_gt

    def xbody(i, x):
        cand = x + lax.shift_left(jnp.int32(1), nbits - 1 - i)
        cnt = count_where(lambda kk, off: jnp.logical_and(kk == thr, off + lane_k < cand))
        return jnp.where(cnt < need, cand, x)

    xcut = lax.fori_loop(0, nbits, xbody, jnp.zeros((nb, 1), I32))
    xcut = jnp.where(finite_thr, xcut, -1)
    ties_past = count_where(lambda kk, off: kk == thr)
    own_tie = jnp.logical_and(jnp.logical_and(own_key == thr, finite_thr), ties_past < need)
    self_ref[...] = jnp.broadcast_to(
        jnp.where(jnp.logical_or(own_key > thr, own_tie), 1.0, 0.0), (nb, LANES))

    def sel_body(c, carry):
        off = pl.multiple_of(c * kc, kc)
        kk = keys_ref[:, pl.ds(off, kc)]
        sel = jnp.logical_or(kk > thr, jnp.logical_and(kk == thr, off + lane_k <= xcut))
        sel_ref[:, pl.ds(off, kc)] = jnp.where(sel, 1.0, 0.0)
        return carry

    lax.fori_loop(0, nck, sel_body, 0)


def _decselect(scores, qip, kib, wi):
    nb, past = scores.shape
    k_sel = min(TOPK_MAX, (past + 1) // 4)
    kc = next(c for c in (4096, 2048, 1024, 512, PAGE_SIZE) if past % c == 0)
    nbits = max(1, past.bit_length())
    full = lambda w: pl.BlockSpec((nb, w), lambda i: (0, 0))
    return pl.pallas_call(
        functools.partial(_decselect_kernel, k_sel=k_sel, kc=kc, nbits=nbits),
        grid=(1,),
        in_specs=[full(past), full(N_IDX_HEADS * LANES), full(LANES), full(LANES)],
        out_specs=[full(past), full(LANES)],
        out_shape=[jax.ShapeDtypeStruct((nb, past), F32), jax.ShapeDtypeStruct((nb, LANES), F32)],
        scratch_shapes=[pltpu.VMEM((nb, past), I32)],
        compiler_params=_cparams(("arbitrary",)),
        name="decode_select",
    )(scores, qip, kib, wi)


def _decattn_kernel(pt_ref, qp_ref, sel_ref, kn_ref, vn_ref, self_ref, *refs, n_steps):
    del pt_ref
    pg = (len(refs) - 4) // 2
    kp_refs, vp_refs = refs[:pg], refs[pg:2 * pg]
    o_ref, m_ref, l_ref, acc_ref = refs[2 * pg:]
    p = pl.program_id(1)

    @pl.when(p == 0)
    def _():
        m_ref[...] = jnp.full(m_ref.shape, NEG_BIG, F32)
        l_ref[...] = jnp.zeros(l_ref.shape, F32)
        acc_ref[...] = jnp.zeros(acc_ref.shape, F32)

    q = qp_ref[...]
    kcat = jnp.concatenate([r[...].astype(BF16) for r in kp_refs], axis=0)
    vcat = jnp.concatenate([r[...].astype(BF16) for r in vp_refs], axis=0)
    s = lax.dot_general(q.astype(BF16), kcat, NT_DIMS, preferred_element_type=F32)
    s = jnp.where(sel_ref[...] > 0.0, s, -jnp.inf)
    m_old = m_ref[:, 0:1]
    m_new = jnp.maximum(m_old, jnp.max(s, axis=1, keepdims=True))
    alpha = jnp.exp(m_old - m_new)
    pr = jnp.exp(s - m_new)
    l_new = alpha * l_ref[:, 0:1] + jnp.sum(pr, axis=1, keepdims=True)
    acc_ref[...] = alpha * acc_ref[...] + jnp.dot(pr.astype(BF16), vcat, preferred_element_type=F32)
    m_ref[...] = jnp.broadcast_to(m_new, m_ref.shape)
    l_ref[...] = jnp.broadcast_to(l_new, l_ref.shape)

    @pl.when(p == n_steps - 1)
    def _():
        s_own = jnp.sum(q * kn_ref[...], axis=1, keepdims=True)
        s_own = jnp.where(self_ref[:, 0:1] > 0.0, s_own, -jnp.inf)
        m_o = m_ref[:, 0:1]
        m_n = jnp.maximum(m_o, s_own)
        al = jnp.exp(m_o - m_n)
        p_own = jnp.exp(s_own - m_n)
        l_n = al * l_ref[:, 0:1] + p_own
        acc = al * acc_ref[...] + p_own.astype(BF16).astype(F32) * vn_ref[...]
        o_ref[...] = acc / l_n


def _decattn(page_table, qp, sel, pool_k, pool_v, kb, vb, self_sel, layer):
    nb, n_pages = page_table.shape
    pg = math.gcd(DECODE_PAGE_GROUP, n_pages)
    n_steps = n_pages // pg
    row = lambda r: pl.BlockSpec((None, r, LANES), lambda b, p, pt: (b, 0, 0))
    pages = _page_specs((None, None, PAGE_SIZE, LANES), layer, pg)
    grid_spec = pltpu.PrefetchScalarGridSpec(
        num_scalar_prefetch=1,
        grid=(nb, n_steps),
        in_specs=[row(N_HEADS),
                  pl.BlockSpec((None, None, 1, pg * PAGE_SIZE), lambda b, p, pt: (b, p, 0, 0)),
                  row(1), row(1), row(1)] + pages + pages,
        out_specs=row(N_HEADS),
        scratch_shapes=[pltpu.VMEM((N_HEADS, LANES), F32)] * 3,
    )
    return pl.pallas_call(
        functools.partial(_decattn_kernel, n_steps=n_steps),
        grid_spec=grid_spec,
        out_shape=jax.ShapeDtypeStruct((nb, N_HEADS, LANES), F32),
        compiler_params=_cparams(("parallel", "arbitrary")),
        name="decode_attend",
    )(page_table, qp, sel.reshape(nb, n_steps, 1, pg * PAGE_SIZE), kb, vb, self_sel,
      *([pool_k] * pg), *([pool_v] * pg))


def _pack_w_in(w):
    d = w.shape[0]
    pad = lambda n: jnp.zeros((d, n), w.dtype)
    cols = [w[:, 4172:7244],
            w[:, 0:2048],
            w[:, 2628:3140],
            w[:, 3140:4164],
            w[:, 2048:2304],
            w[:, 2304:2560],
            w[:, 2560:2628], pad(60),
            w[:, 4164:4172], pad(120),
            pad(N_PROJ - 7424)]
    return jnp.concatenate(cols, axis=1).astype(BF16)


def _pad_lanes(v):
    return jnp.zeros((1, LANES), F32).at[0, :v.shape[0]].set(v)


def _rope_tables(pos):
    half = HEAD_DIM // 2
    inv = ROPE_THETA ** (-jnp.arange(half, dtype=F32) / half)
    ang = pos.astype(F32)[:, None] * inv[None, :]
    cos, sin = jnp.cos(ang), jnp.sin(ang)
    cos2 = jnp.concatenate([cos, cos], axis=1)
    sin2 = jnp.concatenate([-sin, sin], axis=1)
    return jnp.tile(cos2, (1, 2)), jnp.tile(sin2, (1, 2))


def _head_halves(o):
    hpg = N_HEADS // N_KV_HEADS
    parts = [o[:, h, (h // hpg) * HEAD_DIM:(h // hpg + 1) * HEAD_DIM] for h in range(N_HEADS)]
    return jnp.concatenate(parts, axis=-1)


def kernel(x_prompt, x_sample, cache_k, cache_v, cache_kidx, state_conv, state_ssm_conv, state_ssm, page_table,
           norm1, w_in, b_gate, conv_w, q_norm, k_norm, w_pc, w_pa, ssm_conv_w, ssm_conv_b, dt_bias, a_log,
           d_skip, ssm_norm, w_ps, w_o, norm2, w_gate_dense, w_up_dense, w_down_dense, w_router, w_gate_moe,
           w_up_moe, w_down_moe):
    nb, seq, _ = x_prompt.shape
    db = x_sample.shape[0]
    depth = w_in.shape[0]
    n_pages = page_table.shape[1]
    past = n_pages * PAGE_SIZE

    hp = x_prompt.reshape(nb * seq, D_MODEL)
    hs = x_sample.reshape(db, D_MODEL)
    cos_p, sin_p = _rope_tables(jnp.tile(jnp.arange(seq), nb))
    cos_s, sin_s = _rope_tables(jnp.full((db,), past))

    pool_k = cache_k.reshape(*cache_k.shape[:3], LANES)
    pool_v = cache_v.reshape(*cache_v.shape[:3], LANES)

    outs_p = [[] for _ in range(6)]
    outs_s = [[] for _ in range(6)]
    for l in range(depth):
        wp = _pack_w_in(w_in[l])
        g1 = norm1[l].reshape(1, D_MODEL)
        g2 = norm2[l].reshape(1, D_MODEL)
        qg = jnp.tile(q_norm[l], 2).reshape(1, LANES)
        kg = jnp.tile(k_norm[l], 2).reshape(1, LANES)
        cw = ssm_conv_w[l]
        cbias = ssm_conv_b[l].reshape(1, SSM_CONV_DIM)
        dtb = _pad_lanes(dt_bias[l])
        alog = _pad_lanes(a_log[l])
        dsk = jnp.repeat(d_skip[l], SSM_HEAD_DIM).reshape(1, SSM_INNER)
        sn = ssm_norm[l].reshape(1, SSM_INNER)
        wpc, wpa, wps, wo = (w.astype(BF16) for w in (w_pc[l], w_pa[l], w_ps[l], w_o[l]))
        i = l // 2
        if l % 2 == 0:
            routed = False
            wr = jnp.zeros((D_MODEL, LANES), F32)
            wg, wu, wd = (w[i:i + 1].astype(BF16) for w in (w_gate_dense, w_up_dense, w_down_dense))
        else:
            routed = True
            wr = jnp.zeros((D_MODEL, LANES), F32).at[:, :N_EXPERTS].set(w_router[i])
            wg, wu, wd = (w[i].astype(BF16) for w in (w_gate_moe, w_up_moe, w_down_moe))

        proj = _inproj(hp, g1, wp)
        qt, kf, kb, vt, qit, kif, kib, wt = _prep(proj, cos_p, sin_p, qg, kg, True)
        attn = _dsa_prompt(qt, qit, wt, kb, vt, kib, nb, seq)
        pre_a, conv_st = _conva_prompt(proj, conv_w[l], nb, seq)
        y_ssd, h_last = _ssd_prompt(proj, cw, cbias, dtb, alog, dsk, nb, seq)
        hp = _merge(hp, pre_a, attn, y_ssd, proj, sn, b_gate[l], wpc, wpa, wps, wo)
        if routed and hp.shape[0] % MOE_TILE == 0:
            hp = _moe_grouped(hp, g2, wr, wg, wu, wd)
        else:
            hp = _ffn(hp, g2, wr, wg, wu, wd, routed)
        proj3 = proj.reshape(nb, seq, N_PROJ)
        outs_p[0].append(kf.reshape(nb, seq, N_KV_HEADS, HEAD_DIM))
        outs_p[1].append(proj3[:, :, V_OFF:V_OFF + LANES].reshape(nb, seq, N_KV_HEADS, HEAD_DIM))
        outs_p[2].append(kif[:, :IDX_DIM].reshape(nb, seq, IDX_DIM))
        outs_p[3].append(conv_st)
        outs_p[4].append(proj3[:, seq - 3:, XS_OFF:XS_OFF + SSM_CONV_DIM])
        outs_p[5].append(h_last.reshape(nb, SSM_HEADS, SSM_HEAD_DIM, SSM_STATE))

        proj = _inproj(hs, g1, wp)
        qp, kf, kb, vb, qip, kif, kib, wi = _prep(proj, cos_s, sin_s, qg, kg, False)
        wrows = jnp.broadcast_to(wi[:, :N_IDX_HEADS, None], (db, N_IDX_HEADS, PAGE_SIZE))
        scores = _decscore(page_table, qip.astype(F32).reshape(db, N_IDX_HEADS, LANES), wrows, cache_kidx, l)
        sel, self_sel = _decselect(scores, qip, kib, wi)
        o = _decattn(page_table, qp.astype(F32).reshape(db, N_HEADS, LANES), sel, pool_k, pool_v,
                     kb.astype(F32).reshape(db, 1, LANES), vb.astype(F32).reshape(db, 1, LANES),
                     self_sel.reshape(db, 1, LANES), l)
        attn = _head_halves(o)
        pre_a, u, xs, bcv, dt, ea = _decmix(proj, state_conv[l], state_ssm_conv[l], conv_w[l], cw, cbias, dtb, alog)
        hpg = SSM_HEADS // SSM_GROUPS
        xs4 = xs.reshape(db, SSM_HEADS, SSM_HEAD_DIM, 1)
        bh = jnp.repeat(bcv[:, :SSM_GROUPS * SSM_STATE].reshape(db, SSM_GROUPS, 1, SSM_STATE), hpg, axis=1)
        ch = jnp.repeat(bcv[:, SSM_GROUPS * SSM_STATE:].reshape(db, SSM_GROUPS, 1, SSM_STATE), hpg, axis=1)
        y4, h_new = _decssd(state_ssm[l], dt[:, :SSM_HEADS, None, None], xs4, bh, ch, ea[:, :SSM_HEADS, None, None],
                            d_skip[l].reshape(SSM_HEADS, 1, 1))
        hs = _merge(hs, pre_a, attn, y4.reshape(db, SSM_INNER), proj, sn, b_gate[l], wpc, wpa, wps, wo)
        hs = _ffn(hs, g2, wr, wg, wu, wd, routed)
        outs_s[0].append(kf.reshape(db, 1, N_KV_HEADS, HEAD_DIM))
        outs_s[1].append(proj[:, V_OFF:V_OFF + LANES].reshape(db, 1, N_KV_HEADS, HEAD_DIM))
        outs_s[2].append(kif[:, :IDX_DIM].reshape(db, 1, IDX_DIM))
        outs_s[3].append(jnp.stack([state_conv[l][:, 1], u], axis=1))
        outs_s[4].append(jnp.concatenate(
            [state_ssm_conv[l][:, 1:], proj[:, None, XS_OFF:XS_OFF + SSM_CONV_DIM]], axis=1))
        outs_s[5].append(h_new)

    return (hp.reshape(nb, seq, D_MODEL), hs.reshape(db, 1, D_MODEL),
            *(jnp.stack(o) for o in outs_p), *(jnp.stack(o) for o in outs_s))
```

```python
import functools
import math

import jax
import jax.numpy as jnp
from jax import lax
from jax.experimental import pallas as pl
from jax.experimental.pallas import tpu as pltpu

F32 = jnp.float32
BF16 = jnp.bfloat16
I32 = jnp.int32

D_MODEL = 1024
D_CONV = 512
N_HEADS = 8
HEAD_DIM = 64
N_KV_HEADS = 2
N_IDX_HEADS = 4
IDX_DIM = 64
TOPK_MAX = 256
Q_BLOCK = 128
ROPE_THETA = 10000.0
SSM_INNER = 512
SSM_HEADS = 8
SSM_HEAD_DIM = 64
SSM_GROUPS = 2
SSM_STATE = 128
SSM_CHUNK = 128
SSM_CONV_DIM = 1024
PAGE_SIZE = 128
N_EXPERTS = 8
EPS = 1e-6

LANES = 128
SUBLANES = 8
VMEM_LIMIT = 52 * 1024 * 1024

G_OFF, CB_OFF, CC_OFF, CX_OFF, Q_OFF, Z_OFF, XS_OFF, BC_OFF = 0, 3072, 3584, 4096, 4608, 5120, 5632, 6144
K_OFF, V_OFF, QI_OFF, KW_OFF, DT_OFF = 6656, 6784, 6912, 7168, 7296
N_PROJ = 7680

INT_MIN = -2 ** 31
INT_MAX = 2 ** 31 - 1
KEY_NEG_INF = (-8388608) ^ 0x7FFFFFFF
NEG_BIG = -1e30
LOG2_E = 1.4426950408889634

NT_DIMS = (((1,), (1,)), ((), ()))


def _cparams(sem):
    return pltpu.CompilerParams(dimension_semantics=sem, vmem_limit_bytes=VMEM_LIMIT)


def _sigmoid(x):
    return 1.0 / (1.0 + jnp.exp(-x))


def _silu(x):
    return x * _sigmoid(x)


def _softplus(x):
    return jnp.maximum(x, 0.0) + jnp.log1p(jnp.exp(-jnp.abs(x)))


def _sortable(x):
    bits = pltpu.bitcast(x, I32)
    return bits ^ (lax.shift_right_arithmetic(bits, 31) & 0x7FFFFFFF)


def _rms(x, g):
    return x * lax.rsqrt(jnp.mean(x * x, axis=-1, keepdims=True) + EPS) * g


def _inproj_kernel(x_ref, g_ref, w_ref, o_ref, h_ref):
    @pl.when(pl.program_id(1) == 0)
    def _():
        h_ref[...] = _rms(x_ref[...], g_ref[...]).astype(BF16)

    o_ref[...] = jnp.dot(h_ref[...], w_ref[...], preferred_element_type=F32)


def _inproj(x, g, w):
    m = x.shape[0]
    tm = min(m, 1024)
    tn = 1536
    return pl.pallas_call(
        _inproj_kernel,
        grid=(m // tm, N_PROJ // tn),
        in_specs=[pl.BlockSpec((tm, D_MODEL), lambda i, j: (i, 0)),
                  pl.BlockSpec((1, D_MODEL), lambda i, j: (0, 0)),
                  pl.BlockSpec((D_MODEL, tn), lambda i, j: (0, j))],
        out_specs=pl.BlockSpec((tm, tn), lambda i, j: (i, j)),
        out_shape=jax.ShapeDtypeStruct((m, N_PROJ), F32),
        scratch_shapes=[pltpu.VMEM((tm, D_MODEL), BF16)],
        compiler_params=_cparams(("parallel", "arbitrary")),
        name="inproj",
    )(x, g, w)


V_T_ROWS = LANES + 16


def _prep_kernel(q_ref, k_ref, v_ref, qi_ref, kw_ref, cos_ref, sin_ref, qg_ref, kg_ref,
                 qp_ref, kf_ref, kb_ref, vb_ref, qip_ref, kif_ref, kib_ref, wo_ref, *, transposed):
    tm = cos_ref.shape[0]

    def put(ref, h, val):
        if transposed:
            vt = val.T.astype(BF16)
            for blk in range(tm // Q_BLOCK):
                ref[blk, h * LANES:(h + 1) * LANES, :] = vt[:, blk * Q_BLOCK:(blk + 1) * Q_BLOCK]
        else:
            ref[:, h * LANES:(h + 1) * LANES] = val.astype(BF16)

    cos = cos_ref[...]
    sin = sin_ref[...]
    lane = lax.broadcasted_iota(I32, (tm, LANES), 1)
    low_half = lane < HEAD_DIM
    first_rot = (lane % HEAD_DIM) < (HEAD_DIM // 2)
    r = lax.broadcasted_iota(I32, (LANES, LANES), 0) // HEAD_DIM
    c = lax.broadcasted_iota(I32, (LANES, LANES), 1) // HEAD_DIM
    seg = jnp.where(r == c, 1.0 / HEAD_DIM, 0.0).astype(BF16)

    def rope(x):
        fwd = pltpu.roll(x, LANES - HEAD_DIM // 2, 1)
        bwd = pltpu.roll(x, HEAD_DIM // 2, 1)
        return x * cos + jnp.where(first_rot, fwd, bwd) * sin

    def headnorm(x, g):
        s = x * x
        hi = s.astype(BF16)
        lo = (s - hi.astype(F32)).astype(BF16)
        ms = (jnp.dot(hi, seg, preferred_element_type=F32)
              + jnp.dot(lo, seg, preferred_element_type=F32))
        return x * lax.rsqrt(ms + EPS) * g

    q_scale = HEAD_DIM ** -0.5 * (LOG2_E if transposed else 1.0)
    qg = qg_ref[...]
    for s in range(N_HEADS // 2):
        slab = rope(headnorm(q_ref[:, s * LANES:(s + 1) * LANES], qg)) * q_scale
        swapped = pltpu.roll(slab, HEAD_DIM, 1)
        grp = (2 * s) // (N_HEADS // N_KV_HEADS)
        for hh in range(2):
            h = 2 * s + hh
            src = slab if hh == grp else swapped
            keep = low_half if grp == 0 else jnp.logical_not(low_half)
            put(qp_ref, h, jnp.where(keep, src, 0.0))

    k = rope(headnorm(k_ref[...], kg_ref[...]))
    kf_ref[...] = k
    kb_ref[...] = k.astype(BF16)
    if transposed:
        vb_ref[0:LANES, :] = v_ref[...].T.astype(BF16)
        vb_ref[LANES:V_T_ROWS, :] = jnp.ones((V_T_ROWS - LANES, tm), BF16)
    else:
        vb_ref[...] = v_ref[...].astype(BF16)

    for s in range(N_IDX_HEADS // 2):
        slab = rope(qi_ref[:, s * LANES:(s + 1) * LANES]) * (IDX_DIM ** -0.5)
        swapped = pltpu.roll(slab, HEAD_DIM, 1)
        for hh in range(2):
            h = 2 * s + hh
            src = slab if hh == 0 else swapped
            put(qip_ref, h, jnp.where(low_half, src, 0.0))

    kw = kw_ref[...]
    ki = rope(kw)
    kif_ref[...] = ki
    kib_ref[...] = jnp.where(low_half, ki, 0.0).astype(BF16)
    wi = pltpu.roll(kw, HEAD_DIM, 1) * (N_IDX_HEADS ** -0.5)
    if transposed:
        wt = wi.T[0:SUBLANES, :]
        for blk in range(tm // Q_BLOCK):
            wo_ref[blk] = wt[:, blk * Q_BLOCK:(blk + 1) * Q_BLOCK]
    else:
        wo_ref[...] = wi


def _prep(proj, cos, sin, qg, kg, transposed):
    m = proj.shape[0]
    tm = min(m, 512)
    row = lambda w, off: pl.BlockSpec((tm, w), lambda i: (i, off // w))
    full = lambda w: pl.BlockSpec((tm, w), lambda i: (i, 0))
    const = pl.BlockSpec((1, LANES), lambda i: (0, 0))
    if transposed:
        qblocks = tm // Q_BLOCK
        feat = lambda r: pl.BlockSpec((qblocks, r, Q_BLOCK), lambda i: (i, 0, 0))
        fshape = lambda r, dt: jax.ShapeDtypeStruct((m // Q_BLOCK, r, Q_BLOCK), dt)
        vspec = pl.BlockSpec((V_T_ROWS, tm), lambda i: (0, i))
        vshape = jax.ShapeDtypeStruct((V_T_ROWS, m), BF16)
    else:
        feat = full
        fshape = lambda r, dt: jax.ShapeDtypeStruct((m, r), dt)
        vspec = full(LANES)
        vshape = jax.ShapeDtypeStruct((m, LANES), BF16)
    return pl.pallas_call(
        functools.partial(_prep_kernel, transposed=transposed),
        grid=(m // tm,),
        in_specs=[row(512, Q_OFF), row(LANES, K_OFF), row(LANES, V_OFF), row(256, QI_OFF), row(LANES, KW_OFF),
                  full(LANES), full(LANES), const, const],
        out_specs=[feat(N_HEADS * LANES), full(LANES), full(LANES), vspec,
                   feat(N_IDX_HEADS * LANES), full(LANES), full(LANES), feat(SUBLANES if transposed else LANES)],
        out_shape=[fshape(N_HEADS * LANES, BF16),
                   jax.ShapeDtypeStruct((m, LANES), F32),
                   jax.ShapeDtypeStruct((m, LANES), BF16),
                   vshape,
                   fshape(N_IDX_HEADS * LANES, BF16),
                   jax.ShapeDtypeStruct((m, LANES), F32),
                   jax.ShapeDtypeStruct((m, LANES), BF16),
                   fshape(SUBLANES if transposed else LANES, F32)],
        compiler_params=_cparams(("parallel",)),
        name="prep",
    )(proj, proj, proj, proj, proj, cos, sin, qg, kg)


def _fold_rows(f):
    while f.shape[0] > SUBLANES:
        half = f.shape[0] // 2
        f = f[:half] + f[half:]
    return f


def _dsa_kernel(qt_ref, qit_ref, wt_ref, k_ref, vt_ref, ki_ref, o_ref,
                keys_ref, tpos_ref, x_ref, m_ref, kmax_ref, acc_ref, *, k_sel, kc, nbits):
    qb = Q_BLOCK
    n_keys = k_ref.shape[0]
    assert n_keys < 2 ** 22
    j = pl.program_id(1)
    nck = (j * qb + qb + kc - 1) // kc
    qpos = j * qb + lax.broadcasted_iota(I32, (1, qb), 1)
    sub_k = lax.broadcasted_iota(I32, (kc, qb), 0)
    pair = lambda ref, s: jnp.concatenate(
        [ref[(2 * s) * LANES:(2 * s + 1) * LANES, :], ref[(2 * s + 1) * LANES:(2 * s + 2) * LANES, :]], axis=1)
    qi2 = [pair(qit_ref, s) for s in range(N_IDX_HEADS // 2)]
    q2 = [pair(qt_ref, s) for s in range(N_HEADS // 2)]

    def score_body(c, carry):
        off = pl.multiple_of(c * kc, kc)
        kic = ki_ref[pl.ds(off, kc), :]
        acc = jnp.zeros((kc, qb), F32)
        for s in range(N_IDX_HEADS // 2):
            s2 = jnp.dot(kic, qi2[s], preferred_element_type=F32)
            for hh in range(2):
                h = 2 * s + hh
                acc = acc + jnp.maximum(s2[:, hh * qb:(hh + 1) * qb], 0.0) * wt_ref[h:h + 1, :]
        kpos = off + sub_k
        key = _sortable(jnp.where(kpos <= qpos, acc, -jnp.inf))
        key = jnp.where(key == 0, (n_keys - 1) - kpos, jnp.where(key == -1, -1 - kpos, key))
        keys_ref[pl.ds(off, kc), :] = key
        return carry

    lax.fori_loop(0, nck, score_body, 0)

    def count_where(pred, ref=keys_ref):
        def body(c, acc):
            off = pl.multiple_of(c * kc, kc)
            return acc + _fold_rows(jnp.where(pred(ref[pl.ds(off, kc), :], off), 1.0, 0.0))

        acc = lax.fori_loop(0, nck, body, jnp.zeros((SUBLANES, qb), F32))
        return jnp.sum(acc, axis=0, keepdims=True)

    def bit_body(i, carry):
        t, cnt_acc, cnt_rej = carry
        cand = t + lax.shift_left(jnp.int32(1), 31 - i)
        cnt = count_where(lambda kk, off: kk >= cand)
        ok = cnt >= k_sel
        return jnp.where(ok, cand, t), jnp.where(ok, cnt, cnt_acc), jnp.where(ok, cnt_rej, cnt)

    zero = jnp.zeros((1, qb), F32)
    thr, cnt_ge, cnt_gt = lax.fori_loop(0, 32, bit_body, (jnp.full((1, qb), INT_MIN, I32), zero, zero))
    thr = jnp.maximum(thr, KEY_NEG_INF)
    finite_thr = thr > KEY_NEG_INF
    need = k_sel - cnt_gt
    tie = jnp.logical_and(cnt_ge > k_sel, finite_thr)
    x_default = jnp.where(finite_thr, INT_MAX, -1)
    x_ref[...] = jnp.broadcast_to(x_default, x_ref.shape)

    @pl.when(jnp.max(jnp.where(tie, 1.0, 0.0)) > 0.0)
    def _():
        def tie_pos_body(c, carry):
            off = pl.multiple_of(c * kc, kc)
            tpos_ref[pl.ds(off, kc), :] = jnp.where(keys_ref[pl.ds(off, kc), :] == thr, off + sub_k, INT_MAX)
            return carry

        lax.fori_loop(0, nck, tie_pos_body, 0)

        def xbody(i, x):
            cand = x + lax.shift_left(jnp.int32(1), nbits - 1 - i)
            cnt = count_where(lambda tp, off: tp < cand, tpos_ref)
            return jnp.where(cnt < need, cand, x)

        x = lax.fori_loop(0, nbits, xbody, jnp.zeros((1, qb), I32))
        x_ref[...] = jnp.broadcast_to(jnp.where(tie, x, x_default), x_ref.shape)

    xcut = x_ref[0:1, :]
    hpg = N_HEADS // N_KV_HEADS

    @pl.when(j == 0)
    def _():
        r = lax.broadcasted_iota(I32, (LANES, LANES), 0) // HEAD_DIM
        cc = lax.broadcasted_iota(I32, (LANES, LANES), 1) // HEAD_DIM
        seg = jnp.where(r == cc, 1.0, 0.0).astype(BF16)

        def body(c, mx):
            kf = k_ref[pl.ds(pl.multiple_of(c * kc, kc), kc), :].astype(F32)
            gs = jnp.dot((kf * kf).astype(BF16), seg, preferred_element_type=F32)
            while gs.shape[0] > SUBLANES:
                half = gs.shape[0] // 2
                gs = jnp.maximum(gs[:half], gs[half:])
            return jnp.maximum(mx, gs)

        mx = lax.fori_loop(0, k_ref.shape[0] // kc, body, jnp.zeros((SUBLANES, LANES), F32))
        mx = jnp.max(mx, axis=0, keepdims=True)
        lane = lax.broadcasted_iota(I32, (1, LANES), 1)
        other = pltpu.roll(mx, HEAD_DIM, 1)
        kmax_ref[0:1, :] = jnp.where(lane < HEAD_DIM, mx, other)
        kmax_ref[1:2, :] = jnp.where(lane < HEAD_DIM, other, mx)

    def chunk(c):
        off = pl.multiple_of(c * kc, kc)
        kk = keys_ref[pl.ds(off, kc), :]
        sel = jnp.logical_or(kk > thr, jnp.logical_and(kk == thr, off + sub_k <= xcut))
        return sel, k_ref[pl.ds(off, kc), :], vt_ref[:, pl.ds(off, kc)]

    bound = []
    for h in range(N_HEADS):
        qh = qt_ref[h * LANES:(h + 1) * LANES, :].astype(F32)
        qsq = jnp.sum(qh * qh, axis=0, keepdims=True)
        bound.append(jnp.sqrt(qsq * kmax_ref[h // hpg:h // hpg + 1, :]))
    acc_ref[...] = jnp.zeros(acc_ref.shape, F32)

    def fast_body(c, carry):
        sel, kch, vch = chunk(c)
        n_pair = N_HEADS // 2
        s2 = jnp.dot(kch, q2[0], preferred_element_type=F32)
        for s in range(n_pair):
            s2_next = jnp.dot(kch, q2[s + 1], preferred_element_type=F32) if s + 1 < n_pair else None
            ps = [jnp.exp2(jnp.where(sel, s2[:, hh * qb:(hh + 1) * qb] - bound[2 * s + hh], -jnp.inf)).astype(BF16)
                  for hh in range(2)]
            acc_ref[s] += jnp.dot(vch, jnp.concatenate(ps, axis=1), preferred_element_type=F32)
            s2 = s2_next
        return carry

    lax.fori_loop(0, nck, fast_body, 0)

    dens = jnp.concatenate([acc_ref[s, LANES:LANES + 1, :] for s in range(N_HEADS // 2)], axis=0)

    @pl.when(jnp.logical_not(jnp.min(dens) >= 1e-20))
    def _():
        m_ref[...] = jnp.full(m_ref.shape, NEG_BIG, F32)
        acc_ref[...] = jnp.zeros(acc_ref.shape, F32)

        def exact_body(c, carry):
            sel, kch, vch = chunk(c)
            for s in range(N_HEADS // 2):
                s2 = jnp.dot(kch, q2[s], preferred_element_type=F32)
                ps, alphas = [], []
                for hh in range(2):
                    h = 2 * s + hh
                    sh = jnp.where(sel, s2[:, hh * qb:(hh + 1) * qb], -jnp.inf)
                    m_old = m_ref[h:h + 1, :]
                    m_new = jnp.maximum(m_old, jnp.max(sh, axis=0, keepdims=True))
                    m_ref[h:h + 1, :] = m_new
                    alphas.append(jnp.exp2(m_old - m_new))
                    ps.append(jnp.exp2(sh - m_new).astype(BF16))
                o2 = jnp.dot(vch, jnp.concatenate(ps, axis=1), preferred_element_type=F32)
                acc_ref[s] = jnp.concatenate(alphas, axis=1) * acc_ref[s] + o2
            return carry

        lax.fori_loop(0, nck, exact_body, 0)

    for s in range(N_HEADS // 2):
        g = (2 * s) // hpg
        num = acc_ref[s, g * HEAD_DIM:(g + 1) * HEAD_DIM, :]
        out2 = num / acc_ref[s, LANES:LANES + 1, :]
        o_ref[:, s * LANES:(s + 1) * LANES] = jnp.concatenate([out2[:, :qb], out2[:, qb:]], axis=0).T


def _dsa_prompt(qt, qit, wt, kb, vt, kib, nb, seq):
    k_sel = min(TOPK_MAX, seq // 4)
    kc = next(c for c in (512, Q_BLOCK) if seq % c == 0)
    nbits = max(1, (seq - 1).bit_length())
    nq = seq // Q_BLOCK
    r3 = lambda a: a.reshape(nb, seq, a.shape[-1])
    qcol = lambda r: pl.BlockSpec((None, r, Q_BLOCK), lambda b, j: (b * nq + j, 0, 0))
    seqblk = pl.BlockSpec((None, seq, LANES), lambda b, j: (b, 0, 0))
    out = pl.pallas_call(
        functools.partial(_dsa_kernel, k_sel=k_sel, kc=kc, nbits=nbits),
        grid=(nb, nq),
        in_specs=[qcol(N_HEADS * LANES), qcol(N_IDX_HEADS * LANES), qcol(SUBLANES), seqblk,
                  pl.BlockSpec((V_T_ROWS, seq), lambda b, j: (0, b)), seqblk],
        out_specs=pl.BlockSpec((None, Q_BLOCK, N_HEADS * HEAD_DIM), lambda b, j: (b, j, 0)),
        out_shape=jax.ShapeDtypeStruct((nb, seq, N_HEADS * HEAD_DIM), F32),
        scratch_shapes=[pltpu.VMEM((seq, Q_BLOCK), I32),
                        pltpu.VMEM((seq, Q_BLOCK), I32),
                        pltpu.VMEM((SUBLANES, Q_BLOCK), I32),
                        pltpu.VMEM((N_HEADS, Q_BLOCK), F32),
                        pltpu.VMEM((SUBLANES, LANES), F32),
                        pltpu.VMEM((N_HEADS // 2, V_T_ROWS, 2 * Q_BLOCK), F32)],
        compiler_params=_cparams(("arbitrary", "arbitrary")),
        name="dsa_prompt",
    )(qt, qit, wt, r3(kb), vt, r3(kib))
    return out.reshape(nb * seq, N_HEADS * HEAD_DIM)


def _conva_kernel(cb_ref, cc_ref, cx_ref, cch_ref, cxh_ref, w_ref, o_ref, st_ref, *, tiles_per_seq):
    tm = cb_ref.shape[0]
    first = (pl.program_id(0) % tiles_per_seq) == 0
    u = cc_ref[...] * cx_ref[...]
    uh = jnp.where(first, 0.0, cch_ref[...] * cxh_ref[...])
    ext = jnp.concatenate([uh, u], axis=0)
    conv = (w_ref[2:3, :] * u + w_ref[1:2, :] * ext[SUBLANES - 1:SUBLANES - 1 + tm]
            + w_ref[0:1, :] * ext[SUBLANES - 2:SUBLANES - 2 + tm])
    o_ref[...] = cb_ref[...] * conv
    st_ref[...] = u[tm - 2:tm, :]


def _conva_prompt(proj, conv_w, nb, seq):
    m = proj.shape[0]
    tm = min(seq, 512)
    tps = seq // tm
    col = lambda off: pl.BlockSpec((tm, D_CONV), lambda i: (i, off // D_CONV))
    halo = lambda off: pl.BlockSpec(
        (SUBLANES, D_CONV), lambda i: (jnp.maximum(i * (tm // SUBLANES) - 1, 0), off // D_CONV))
    return pl.pallas_call(
        functools.partial(_conva_kernel, tiles_per_seq=tps),
        grid=(m // tm,),
        in_specs=[col(CB_OFF), col(CC_OFF), col(CX_OFF), halo(CC_OFF), halo(CX_OFF),
                  pl.BlockSpec((3, D_CONV), lambda i: (0, 0))],
        out_specs=[pl.BlockSpec((tm, D_CONV), lambda i: (i, 0)),
                   pl.BlockSpec((None, 2, D_CONV), lambda i: (i // tps, 0, 0))],
        out_shape=[jax.ShapeDtypeStruct((m, D_CONV), F32),
                   jax.ShapeDtypeStruct((nb, 2, D_CONV), F32)],
        compiler_params=_cparams(("arbitrary",)),
        name="conva_prompt",
    )(proj, proj, proj, proj, proj, conv_w)


def _ssd_kernel(xs_ref, bc_ref, dt_ref, xh_ref, bh_ref, cwx_ref, cwb_ref, cbx_ref, cbb_ref,
                dtb_ref, alog_ref, dsk_ref, y_ref, hout_ref, h_ref, *, nchunk):
    cl = SSM_CHUNK
    c = pl.program_id(1)
    first = c == 0

    @pl.when(first)
    def _():
        h_ref[...] = jnp.zeros(h_ref.shape, F32)

    def conv(cur, halo, w_ref, b_ref):
        ext = jnp.concatenate([jnp.where(first, 0.0, halo), cur], axis=0)
        out = (w_ref[3:4, :] * cur + w_ref[2:3, :] * ext[SUBLANES - 1:SUBLANES - 1 + cl]
               + w_ref[1:2, :] * ext[SUBLANES - 2:SUBLANES - 2 + cl]
               + w_ref[0:1, :] * ext[SUBLANES - 3:SUBLANES - 3 + cl] + b_ref[...])
        return _silu(out)

    xs = conv(xs_ref[...], xh_ref[...], cwx_ref, cbx_ref)
    bc = conv(bc_ref[...], bh_ref[...], cwb_ref, cbb_ref)
    dt = _softplus(dt_ref[...] + dtb_ref[...])
    a = dt * (-jnp.exp(alog_ref[...]))
    ri = lax.broadcasted_iota(I32, (cl, cl), 0)
    ci = lax.broadcasted_iota(I32, (cl, cl), 1)
    causal = ri >= ci
    cs = jnp.dot(jnp.where(causal, 1.0, 0.0), a, preferred_element_type=F32,
                 precision=lax.Precision.HIGHEST)
    cs_t = cs.T
    lane = lax.broadcasted_iota(I32, (cl, LANES), 1)
    lo = lane < SSM_HEAD_DIM
    rows_lo = lax.broadcasted_iota(I32, (LANES, 1), 0) < SSM_HEAD_DIM
    heads_per_group = SSM_HEADS // SSM_GROUPS

    cb = []
    for g in range(SSM_GROUPS):
        bg = bc[:, g * SSM_STATE:(g + 1) * SSM_STATE].astype(BF16)
        cg = bc[:, (SSM_GROUPS + g) * SSM_STATE:(SSM_GROUPS + g + 1) * SSM_STATE].astype(BF16)
        cb.append((bg, cg, lax.dot_general(cg, bg, NT_DIMS, preferred_element_type=F32)))

    for s in range(SSM_HEADS // 2):
        h0, h1 = 2 * s, 2 * s + 1
        bg, cg, cbg = cb[h0 // heads_per_group]
        sl = slice(s * LANES, (s + 1) * LANES)
        xs_s = xs[:, sl]
        col0, col1 = cs[:, h0:h0 + 1], cs[:, h1:h1 + 1]
        last0, last1 = cs[cl - 1:cl, h0:h0 + 1], cs[cl - 1:cl, h1:h1 + 1]
        xdt = xs_s * jnp.where(lo, dt[:, h0:h0 + 1], dt[:, h1:h1 + 1])
        xdt_b = xdt.astype(BF16)
        m0 = (cbg * jnp.where(causal, jnp.exp(col0 - cs_t[h0:h0 + 1, :]), 0.0)).astype(BF16)
        m1 = (cbg * jnp.where(causal, jnp.exp(col1 - cs_t[h1:h1 + 1, :]), 0.0)).astype(BF16)
        y_diag = jnp.where(lo, jnp.dot(m0, xdt_b, preferred_element_type=F32),
                           jnp.dot(m1, xdt_b, preferred_element_type=F32))
        hs = h_ref[sl, :]
        y_off = lax.dot_general(cg, hs.astype(BF16), NT_DIMS, preferred_element_type=F32)
        y_off = y_off * jnp.where(lo, jnp.exp(col0), jnp.exp(col1))
        y_ref[:, sl] = y_diag + y_off + dsk_ref[:, sl] * xs_s
        xw = xdt * jnp.where(lo, jnp.exp(last0 - col0), jnp.exp(last1 - col1))
        st = jnp.dot(xw.T.astype(BF16), bg, preferred_element_type=F32)
        h_ref[sl, :] = hs * jnp.where(rows_lo, jnp.exp(last0), jnp.exp(last1)) + st

    @pl.when(c == nchunk - 1)
    def _():
        hout_ref[...] = h_ref[...]


def _ssd_prompt(proj, cw, cbias, dtb, alog, dsk, nb, seq):
    m = proj.shape[0]
    cl = SSM_CHUNK
    nchunk = seq // cl
    blk = lambda w, off: pl.BlockSpec((cl, w), lambda b, c: (b * nchunk + c, off // w))
    halo = lambda off: pl.BlockSpec(
        (SUBLANES, 512), lambda b, c: (jnp.maximum((b * nchunk + c) * (cl // SUBLANES) - 1, 0), off // 512))
    const = lambda r, w, j: pl.BlockSpec((r, w), lambda b, c: (0, j))
    y, hout = pl.pallas_call(
        functools.partial(_ssd_kernel, nchunk=nchunk),
        grid=(nb, nchunk),
        in_specs=[blk(512, XS_OFF), blk(512, BC_OFF), blk(LANES, DT_OFF), halo(XS_OFF), halo(BC_OFF),
                  const(4, 512, 0), const(4, 512, 1), const(1, 512, 0), const(1, 512, 1),
                  const(1, LANES, 0), const(1, LANES, 0), const(1, 512, 0)],
        out_specs=[pl.BlockSpec((cl, SSM_INNER), lambda b, c: (b * nchunk + c, 0)),
                   pl.BlockSpec((None, SSM_INNER, SSM_STATE), lambda b, c: (b, 0, 0))],
        out_shape=[jax.ShapeDtypeStruct((m, SSM_INNER), F32),
                   jax.ShapeDtypeStruct((nb, SSM_INNER, SSM_STATE), F32)],
        scratch_shapes=[pltpu.VMEM((SSM_INNER, SSM_STATE), F32)],
        compiler_params=_cparams(("parallel", "arbitrary")),
        name="ssd_prompt",
    )(proj, proj, proj, proj, proj, cw, cw, cbias, cbias, dtb, alog, dsk)
    return y, hout


def _merge_kernel(x_ref, pa_ref, at_ref, ys_ref, z_ref, g0_ref, g1_ref, g2_ref, sn_ref, bg_ref,
                  wpc_ref, wpa_ref, wps_ref, wo_ref, o_ref):
    ssd = _rms(ys_ref[...] * _silu(z_ref[...]), sn_ref[...])
    ya = jnp.dot(pa_ref[...].astype(BF16), wpc_ref[...], preferred_element_type=F32)
    yb = jnp.dot(at_ref[...].astype(BF16), wpa_ref[...], preferred_element_type=F32)
    yc = jnp.dot(ssd.astype(BF16), wps_ref[...], preferred_element_type=F32)
    merged = (_sigmoid(g0_ref[...] + bg_ref[0:1, :]) * ya + _sigmoid(g1_ref[...] + bg_ref[1:2, :]) * yb
              + _sigmoid(g2_ref[...] + bg_ref[2:3, :]) * yc)
    o_ref[...] = x_ref[...] + jnp.dot(merged.astype(BF16), wo_ref[...], preferred_element_type=F32)


def _merge(x, pre_a, attn, y_ssd, proj, ssm_norm, b_gate, w_pc, w_pa, w_ps, w_o):
    m = x.shape[0]
    tm = min(m, 256)
    row = lambda w: pl.BlockSpec((tm, w), lambda i: (i, 0))
    pcol = lambda w, off: pl.BlockSpec((tm, w), lambda i: (i, off // w))
    const = lambda r, w: pl.BlockSpec((r, w), lambda i: (0, 0))
    return pl.pallas_call(
        _merge_kernel,
        grid=(m // tm,),
        in_specs=[row(D_MODEL), row(512), row(512), row(512), pcol(512, Z_OFF),
                  pcol(D_MODEL, G_OFF), pcol(D_MODEL, G_OFF + D_MODEL), pcol(D_MODEL, G_OFF + 2 * D_MODEL),
                  const(1, 512), const(3, D_MODEL),
                  const(512, D_MODEL), const(512, D_MODEL), const(512, D_MODEL), const(D_MODEL, D_MODEL)],
        out_specs=row(D_MODEL),
        out_shape=jax.ShapeDtypeStruct((m, D_MODEL), F32),
        compiler_params=_cparams(("parallel",)),
        name="merge",
    )(x, pre_a, attn, y_ssd, proj, proj, proj, proj, ssm_norm, b_gate, w_pc, w_pa, w_ps, w_o)


def _ffn_kernel(x_ref, g_ref, wr_ref, wg_ref, wu_ref, wd_ref, o_ref, h_ref, acc_ref, comb_ref,
                *, routed, n_e, n_f):
    e = pl.program_id(1)
    j = pl.program_id(2)
    tm = x_ref.shape[0]

    @pl.when(jnp.logical_and(e == 0, j == 0))
    def _():
        hf = _rms(x_ref[...], g_ref[...])
        h_ref[...] = hf.astype(BF16)
        acc_ref[...] = jnp.zeros(acc_ref.shape, F32)
        if routed:
            lane = lax.broadcasted_iota(I32, (tm, LANES), 1).astype(F32)
            logits = jnp.dot(hf, wr_ref[...], preferred_element_type=F32, precision=lax.Precision.HIGHEST)
            logits = jnp.where(lane < n_e, logits, -jnp.inf)
            m1 = jnp.max(logits, axis=1, keepdims=True)
            i1 = jnp.min(jnp.where(logits == m1, lane, float(LANES)), axis=1, keepdims=True)
            rest = jnp.where(lane == i1, -jnp.inf, logits)
            m2 = jnp.max(rest, axis=1, keepdims=True)
            i2 = jnp.min(jnp.where(rest == m2, lane, float(LANES)), axis=1, keepdims=True)
            e2 = jnp.exp(m2 - m1)
            den = 1.0 + e2
            comb_ref[...] = jnp.where(lane == i1, 1.0 / den, 0.0) + jnp.where(lane == i2, e2 / den, 0.0)

    h = h_ref[...]
    gate = jnp.dot(h, wg_ref[...], preferred_element_type=F32)
    up = jnp.dot(h, wu_ref[...], preferred_element_type=F32)
    act = _silu(gate) * up
    if routed:
        lane = lax.broadcasted_iota(I32, (tm, LANES), 1)
        act = act * jnp.sum(jnp.where(lane == e, comb_ref[...], 0.0), axis=1, keepdims=True)
    acc_ref[...] += jnp.dot(act.astype(BF16), wd_ref[...], preferred_element_type=F32)

    @pl.when(jnp.logical_and(e == n_e - 1, j == n_f - 1))
    def _():
        o_ref[...] = x_ref[...] + acc_ref[...]


def _ffn(x, g, w_router, w_gate, w_up, w_down, routed):
    m = x.shape[0]
    n_e, _, d_ff = w_gate.shape
    tm = min(m, 1024)
    tf = 256
    n_f = d_ff // tf
    return pl.pallas_call(
        functools.partial(_ffn_kernel, routed=routed, n_e=n_e, n_f=n_f),
        grid=(m // tm, n_e, n_f),
        in_specs=[pl.BlockSpec((tm, D_MODEL), lambda i, e, j: (i, 0)),
                  pl.BlockSpec((1, D_MODEL), lambda i, e, j: (0, 0)),
                  pl.BlockSpec((D_MODEL, LANES), lambda i, e, j: (0, 0)),
                  pl.BlockSpec((None, D_MODEL, tf), lambda i, e, j: (e, 0, j)),
                  pl.BlockSpec((None, D_MODEL, tf), lambda i, e, j: (e, 0, j)),
                  pl.BlockSpec((None, tf, D_MODEL), lambda i, e, j: (e, j, 0))],
        out_specs=pl.BlockSpec((tm, D_MODEL), lambda i, e, j: (i, 0)),
        out_shape=jax.ShapeDtypeStruct((m, D_MODEL), F32),
        scratch_shapes=[pltpu.VMEM((tm, D_MODEL), BF16), pltpu.VMEM((tm, D_MODEL), F32),
                        pltpu.VMEM((tm, LANES), F32)],
        compiler_params=_cparams(("parallel", "arbitrary", "arbitrary")),
        name="moe" if routed else "ffn",
    )(x, g, w_router, w_gate, w_up, w_down)


MOE_BLOCK = 2048
MOE_TILE = 256
TOP_K = 2


def _moe_tiles(tb):
    return TOP_K * tb // MOE_TILE + N_EXPERTS


def _route_kernel(x_ref, g_ref, wr_ref, hb_ref, row_ref, col_ref, meta_ref):
    tb = x_ref.shape[0]
    hf = _rms(x_ref[...], g_ref[...])
    hb_ref[...] = hf.astype(BF16)
    lane_i = lax.broadcasted_iota(I32, (tb, LANES), 1)
    lane = lane_i.astype(F32)
    logits = jnp.dot(hf, wr_ref[...], preferred_element_type=F32, precision=lax.Precision.HIGHEST)
    logits = jnp.where(lane_i < N_EXPERTS, logits, -jnp.inf)
    m1 = jnp.max(logits, axis=1, keepdims=True)
    i1 = jnp.min(jnp.where(logits == m1, lane, float(LANES)), axis=1, keepdims=True)
    rest = jnp.where(lane == i1, -jnp.inf, logits)
    m2 = jnp.max(rest, axis=1, keepdims=True)
    i2 = jnp.min(jnp.where(rest == m2, lane, float(LANES)), axis=1, keepdims=True)
    e2 = jnp.exp(m2 - m1)
    p1 = 1.0 / (1.0 + e2)
    p2 = e2 / (1.0 + e2)
    oh1 = lane == i1
    oh2 = lane == i2
    both = jnp.where(jnp.logical_or(oh1, oh2), 1.0, 0.0).astype(BF16)

    sb = MOE_TILE
    ri = lax.broadcasted_iota(I32, (sb, sb), 0)
    ci = lax.broadcasted_iota(I32, (sb, sb), 1)
    strict = jnp.where(ri > ci, 1.0, 0.0).astype(BF16)
    carry = jnp.zeros((1, LANES), F32)
    ranks = []
    for s in range(tb // sb):
        rows = both[s * sb:(s + 1) * sb]
        ranks.append(jnp.dot(strict, rows, preferred_element_type=F32) + carry)
        carry = carry + jnp.sum(rows.astype(F32), axis=0, keepdims=True)
    rank = jnp.concatenate(ranks, axis=0)
    seg = jnp.floor((carry + (MOE_TILE - 1)) * (1.0 / MOE_TILE)) * MOE_TILE
    ui = lax.broadcasted_iota(I32, (LANES, LANES), 0)
    uj = lax.broadcasted_iota(I32, (LANES, LANES), 1)
    before = jnp.where(ui < uj, 1.0, 0.0)
    off = jnp.dot(jnp.broadcast_to(seg, (SUBLANES, LANES)), before, preferred_element_type=F32,
                  precision=lax.Precision.HIGHEST)[0:1, :]
    dest = off + rank
    d1 = jnp.sum(jnp.where(oh1, dest, 0.0), axis=1, keepdims=True)
    d2 = jnp.sum(jnp.where(oh2, dest, 0.0), axis=1, keepdims=True)
    col = jnp.where(lane_i == 0, d1, jnp.where(lane_i == 1, d2, jnp.where(lane_i == 2, p1,
                                                                          jnp.where(lane_i == 3, p2, 0.0))))
    col_ref[...] = col
    row_ref[...] = col.T[0:SUBLANES, :]

    end_rows = jnp.broadcast_to(off + seg, (LANES, LANES)).T
    start = (lax.broadcasted_iota(I32, (LANES, LANES), 1) * MOE_TILE).astype(F32)
    is_expert = lax.broadcasted_iota(I32, (LANES, LANES), 0) < N_EXPERTS
    done = jnp.where(jnp.logical_and(is_expert, end_rows <= start), 1.0, 0.0)
    expert = jnp.minimum(jnp.sum(done, axis=0, keepdims=True), float(N_EXPERTS - 1))
    total = jnp.sum(jnp.where(lane_i[0:1] < N_EXPERTS, seg, 0.0), axis=1, keepdims=True)
    valid = jnp.where(start[0:1] < total, 1.0, 0.0)
    sub = lax.broadcasted_iota(I32, (SUBLANES, LANES), 0)
    meta_ref[...] = jnp.where(sub == 0, expert, jnp.where(sub == 1, valid, 0.0)).astype(I32)


def _route(x, g, w_router, tb):
    m = x.shape[0]
    nblk = m // tb
    return pl.pallas_call(
        _route_kernel,
        grid=(nblk,),
        in_specs=[pl.BlockSpec((tb, D_MODEL), lambda b: (b, 0)),
                  pl.BlockSpec((1, D_MODEL), lambda b: (0, 0)),
                  pl.BlockSpec((D_MODEL, LANES), lambda b: (0, 0))],
        out_specs=[pl.BlockSpec((tb, D_MODEL), lambda b: (b, 0)),
                   pl.BlockSpec((SUBLANES, tb), lambda b: (0, b)),
                   pl.BlockSpec((tb, LANES), lambda b: (b, 0)),
                   pl.BlockSpec((None, SUBLANES, LANES), lambda b: (b, 0, 0))],
        out_shape=[jax.ShapeDtypeStruct((m, D_MODEL), BF16),
                   jax.ShapeDtypeStruct((SUBLANES, m), F32),
                   jax.ShapeDtypeStruct((m, LANES), F32),
                   jax.ShapeDtypeStruct((nblk, SUBLANES, LANES), I32)],
        compiler_params=_cparams(("parallel",)),
        name="moe_route",
    )(x, g, w_router)


def _gffn_kernel(te_ref, tv_ref, hb_ref, row_ref, wg_ref, wu_ref, wd_ref, ys_ref, xg_ref, acc_ref, *, nt, n_f):
    del te_ref
    b, t, j = pl.program_id(0), pl.program_id(1), pl.program_id(2)
    valid = tv_ref[b * nt + t] > 0
    tb = hb_ref.shape[0]

    @pl.when(jnp.logical_and(valid, j == 0))
    def _():
        slot = (t * MOE_TILE + lax.broadcasted_iota(I32, (MOE_TILE, tb), 0)).astype(F32)
        hit = jnp.logical_or(row_ref[0:1, :] == slot, row_ref[1:2, :] == slot)
        onehot = jnp.where(hit, 1.0, 0.0).astype(BF16)
        xg_ref[...] = jnp.dot(onehot, hb_ref[...], preferred_element_type=F32).astype(BF16)
        acc_ref[...] = jnp.zeros(acc_ref.shape, F32)

    @pl.when(valid)
    def _():
        xg = xg_ref[...]
        gate = jnp.dot(xg, wg_ref[...], preferred_element_type=F32)
        up = jnp.dot(xg, wu_ref[...], preferred_element_type=F32)
        acc_ref[...] += jnp.dot((_silu(gate) * up).astype(BF16), wd_ref[...], preferred_element_type=F32)

    @pl.when(j == n_f - 1)
    def _():
        ys_ref[...] = jnp.where(valid, acc_ref[...], 0.0).astype(BF16)


def _gffn(tile_expert, tile_valid, hb, rowinfo, w_gate, w_up, w_down, tb):
    m = hb.shape[0]
    nblk = m // tb
    nt = _moe_tiles(tb)
    d_ff = w_gate.shape[2]
    tf = d_ff // 2
    n_f = d_ff // tf
    def fidx(b, t, j, te, tv):
        odd = t % 2 == 1
        return jnp.where(tv[b * nt + t] > 0, jnp.where(odd, n_f - 1 - j, j), jnp.where(odd, n_f - 1, 0))

    grid_spec = pltpu.PrefetchScalarGridSpec(
        num_scalar_prefetch=2,
        grid=(nblk, nt, n_f),
        in_specs=[pl.BlockSpec((tb, D_MODEL), lambda b, t, j, te, tv: (b, 0)),
                  pl.BlockSpec((SUBLANES, tb), lambda b, t, j, te, tv: (0, b)),
                  pl.BlockSpec((None, D_MODEL, tf), lambda b, t, j, te, tv: (te[b * nt + t], 0, fidx(b, t, j, te, tv))),
                  pl.BlockSpec((None, D_MODEL, tf), lambda b, t, j, te, tv: (te[b * nt + t], 0, fidx(b, t, j, te, tv))),
                  pl.BlockSpec((None, tf, D_MODEL), lambda b, t, j, te, tv: (te[b * nt + t], fidx(b, t, j, te, tv), 0))],
        out_specs=pl.BlockSpec((MOE_TILE, D_MODEL), lambda b, t, j, te, tv: (b * nt + t, 0)),
        scratch_shapes=[pltpu.VMEM((MOE_TILE, D_MODEL), BF16), pltpu.VMEM((MOE_TILE, D_MODEL), F32)],
    )
    return pl.pallas_call(
        functools.partial(_gffn_kernel, nt=nt, n_f=n_f),
        grid_spec=grid_spec,
        out_shape=jax.ShapeDtypeStruct((nblk * nt * MOE_TILE, D_MODEL), BF16),
        compiler_params=_cparams(("parallel", "arbitrary", "arbitrary")),
        name="moe_experts",
    )(tile_expert, tile_valid, hb, rowinfo, w_gate, w_up, w_down)


def _combine_kernel(x_ref, col_ref, ys_ref, o_ref):
    rows, slots = x_ref.shape[0], ys_ref.shape[0]
    slot = lax.broadcasted_iota(I32, (rows, slots), 1).astype(F32)
    weight = (jnp.where(col_ref[:, 0:1] == slot, col_ref[:, 2:3], 0.0)
              + jnp.where(col_ref[:, 1:2] == slot, col_ref[:, 3:4], 0.0))
    o_ref[...] = x_ref[...] + jnp.dot(weight.astype(BF16), ys_ref[...], preferred_element_type=F32)


def _combine(x, colinfo, ys, tb):
    m = x.shape[0]
    slots = _moe_tiles(tb) * MOE_TILE
    rows = MOE_TILE
    sub = tb // rows
    return pl.pallas_call(
        _combine_kernel,
        grid=(m // tb, sub),
        in_specs=[pl.BlockSpec((rows, D_MODEL), lambda b, r: (b * sub + r, 0)),
                  pl.BlockSpec((rows, LANES), lambda b, r: (b * sub + r, 0)),
                  pl.BlockSpec((slots, D_MODEL), lambda b, r: (b, 0))],
        out_specs=pl.BlockSpec((rows, D_MODEL), lambda b, r: (b * sub + r, 0)),
        out_shape=jax.ShapeDtypeStruct((m, D_MODEL), F32),
        compiler_params=_cparams(("parallel", "arbitrary")),
        name="moe_combine",
    )(x, colinfo, ys)


def _moe_grouped(x, g, w_router, w_gate, w_up, w_down):
    m = x.shape[0]
    tb = min(MOE_BLOCK, m)
    nt = _moe_tiles(tb)
    hb, rowinfo, colinfo, meta = _route(x, g, w_router, tb)
    tile_expert = meta[:, 0, :nt].reshape(-1)
    tile_valid = meta[:, 1, :nt].reshape(-1)
    ys = _gffn(tile_expert, tile_valid, hb, rowinfo, w_gate, w_up, w_down, tb)
    return _combine(x, colinfo, ys, tb)


def _decmix_kernel(cb_ref, cc_ref, cx_ref, xs_ref, bc_ref, dt_ref, sc0_ref, sc1_ref,
                   sx0_ref, sx1_ref, sx2_ref, sb0_ref, sb1_ref, sb2_ref,
                   cw_ref, cwx_ref, cwb_ref, cbx_ref, cbb_ref, dtb_ref, alog_ref,
                   pa_ref, u_ref, xso_ref, bco_ref, dto_ref, ea_ref):
    u = cc_ref[...] * cx_ref[...]
    u_ref[...] = u
    pa_ref[...] = cb_ref[...] * (cw_ref[0:1, :] * sc0_ref[...] + cw_ref[1:2, :] * sc1_ref[...] + cw_ref[2:3, :] * u)
    xso_ref[...] = _silu(cwx_ref[0:1, :] * sx0_ref[...] + cwx_ref[1:2, :] * sx1_ref[...]
                         + cwx_ref[2:3, :] * sx2_ref[...] + cwx_ref[3:4, :] * xs_ref[...] + cbx_ref[...])
    bco_ref[...] = _silu(cwb_ref[0:1, :] * sb0_ref[...] + cwb_ref[1:2, :] * sb1_ref[...]
                         + cwb_ref[2:3, :] * sb2_ref[...] + cwb_ref[3:4, :] * bc_ref[...] + cbb_ref[...])
    dt = _softplus(dt_ref[...] + dtb_ref[...])
    dto_ref[...] = dt
    ea_ref[...] = jnp.exp(dt * (-jnp.exp(alog_ref[...])))


def _decmix(proj, st_conv, st_ssm_conv, conv_w, cw, cbias, dtb, alog):
    nb = proj.shape[0]
    pcol = lambda w, off: pl.BlockSpec((nb, w), lambda i: (0, off // w))
    full = lambda w: pl.BlockSpec((nb, w), lambda i: (0, 0))
    const = lambda r, w, j: pl.BlockSpec((r, w), lambda i: (0, j))
    sx = [st_ssm_conv[:, t, :512] for t in range(3)]
    sb = [st_ssm_conv[:, t, 512:] for t in range(3)]
    return pl.pallas_call(
        _decmix_kernel,
        grid=(1,),
        in_specs=[pcol(512, CB_OFF), pcol(512, CC_OFF), pcol(512, CX_OFF), pcol(512, XS_OFF), pcol(512, BC_OFF),
                  pcol(LANES, DT_OFF)] + [full(512)] * 8
                 + [const(3, 512, 0), const(4, 512, 0), const(4, 512, 1), const(1, 512, 0), const(1, 512, 1),
                    const(1, LANES, 0), const(1, LANES, 0)],
        out_specs=[full(512), full(512), full(512), full(512), full(LANES), full(LANES)],
        out_shape=[jax.ShapeDtypeStruct((nb, 512), F32)] * 4 + [jax.ShapeDtypeStruct((nb, LANES), F32)] * 2,
        compiler_params=_cparams(("arbitrary",)),
        name="decode_mix",
    )(proj, proj, proj, proj, proj, proj, st_conv[:, 0], st_conv[:, 1], *sx, *sb,
      conv_w, cw, cw, cbias, cbias, dtb, alog)


def _decssd_kernel(h_ref, dt_ref, xs_ref, b_ref, c_ref, ea_ref, dsk_ref, y_ref, ho_ref):
    xs = xs_ref[...]
    h_new = h_ref[...] * ea_ref[...] + (xs * dt_ref[...]) * b_ref[...]
    ho_ref[...] = h_new
    y_ref[...] = jnp.sum(h_new * c_ref[...], axis=-1, keepdims=True) + dsk_ref[...] * xs


def _decssd(h0, dt, xs, bh, ch, ea, dsk):
    nb = h0.shape[0]
    blk = lambda a, b: pl.BlockSpec((None, SSM_HEADS, a, b), lambda i: (i, 0, 0, 0))
    return pl.pallas_call(
        _decssd_kernel,
        grid=(nb,),
        in_specs=[blk(SSM_HEAD_DIM, SSM_STATE), blk(1, 1), blk(SSM_HEAD_DIM, 1), blk(1, SSM_STATE),
                  blk(1, SSM_STATE), blk(1, 1), pl.BlockSpec((SSM_HEADS, 1, 1), lambda i: (0, 0, 0))],
        out_specs=[blk(SSM_HEAD_DIM, 1), blk(SSM_HEAD_DIM, SSM_STATE)],
        out_shape=[jax.ShapeDtypeStruct((nb, SSM_HEADS, SSM_HEAD_DIM, 1), F32),
                   jax.ShapeDtypeStruct((nb, SSM_HEADS, SSM_HEAD_DIM, SSM_STATE), F32)],
        compiler_params=_cparams(("parallel",)),
        name="decode_ssd",
    )(h0, dt, xs, bh, ch, ea, dsk)


DECODE_PAGE_GROUP = 32


def _decscore_kernel(pt_ref, qi_ref, w_ref, *refs):
    del pt_ref
    kid_refs, o_ref = refs[:-1], refs[-1]
    kid = jnp.concatenate([r[...].astype(BF16) for r in kid_refs], axis=0)
    s = lax.dot_general(qi_ref[:, 0:IDX_DIM].astype(BF16), kid, NT_DIMS, preferred_element_type=F32)
    sc = jnp.maximum(s, 0.0) * w_ref[:, 0:1]
    o_ref[...] = jnp.sum(sc, axis=0, keepdims=True)


def _page_specs(block, layer, pg):
    zeros = (0,) * (len(block) - 2)
    return [pl.BlockSpec(block, lambda b, p, pt, t=t: (layer, pt[b, p * pg + t]) + zeros) for t in range(pg)]


def _decscore(page_table, qip, wrows, pool_kidx, layer):
    nb, n_pages = page_table.shape
    pg = math.gcd(DECODE_PAGE_GROUP, n_pages)
    grid_spec = pltpu.PrefetchScalarGridSpec(
        num_scalar_prefetch=1,
        grid=(nb, n_pages // pg),
        in_specs=[pl.BlockSpec((None, N_IDX_HEADS, LANES), lambda b, p, pt: (b, 0, 0)),
                  pl.BlockSpec((None, N_IDX_HEADS, PAGE_SIZE), lambda b, p, pt: (b, 0, 0))]
                 + _page_specs((None, None, PAGE_SIZE, IDX_DIM), layer, pg),
        out_specs=pl.BlockSpec((None, None, 1, pg * PAGE_SIZE), lambda b, p, pt: (b, p, 0, 0)),
    )
    out = pl.pallas_call(
        _decscore_kernel,
        grid_spec=grid_spec,
        out_shape=jax.ShapeDtypeStruct((nb, n_pages // pg, 1, pg * PAGE_SIZE), F32),
        compiler_params=_cparams(("parallel", "arbitrary")),
        name="decode_scores",
    )(page_table, qip, wrows, *([pool_kidx] * pg))
    return out.reshape(nb, n_pages * PAGE_SIZE)


def _decselect_kernel(sc_ref, qi_ref, ki_ref, w_ref, sel_ref, self_ref, keys_ref, *, k_sel, kc, nbits):
    nb, past = sc_ref.shape
    nck = past // kc
    lane_k = lax.broadcasted_iota(I32, (nb, kc), 1)
    ki = ki_ref[...].astype(F32)
    own = jnp.zeros((nb, 1), F32)
    for h in range(N_IDX_HEADS):
        s = jnp.sum(qi_ref[:, h * LANES:(h + 1) * LANES].astype(F32) * ki, axis=1, keepdims=True)
        own = own + jnp.maximum(s, 0.0) * w_ref[:, h:h + 1]
    own_key = _sortable(own)

    def key_body(c, carry):
        off = pl.multiple_of(c * kc, kc)
        keys_ref[:, pl.ds(off, kc)] = _sortable(sc_ref[:, pl.ds(off, kc)])
        return carry

    lax.fori_loop(0, nck, key_body, 0)

    def count_where(pred):
        def body(c, acc):
            off = pl.multiple_of(c * kc, kc)
            f = jnp.where(pred(keys_ref[:, pl.ds(off, kc)], off), 1.0, 0.0)
            part = f[:, 0:LANES]
            for t in range(1, kc // LANES):
                part = part + f[:, t * LANES:(t + 1) * LANES]
            return acc + part

        acc = lax.fori_loop(0, nck, body, jnp.zeros((nb, LANES), F32))
        return jnp.sum(acc, axis=1, keepdims=True)

    def bit_body(i, t):
        cand = t + lax.shift_left(jnp.int32(1), 31 - i)
        cnt = count_where(lambda kk, off: kk >= cand) + jnp.where(own_key >= cand, 1.0, 0.0)
        return jnp.where(cnt >= k_sel, cand, t)

    thr = lax.fori_loop(0, 32, bit_body, jnp.full((nb, 1), INT_MIN, I32))
    thr = jnp.maximum(thr, KEY_NEG_INF)
    finite_thr = thr > KEY_NEG_INF
    cnt_gt = count_where(lambda kk, off: kk > thr) + jnp.where(own_key > thr, 1.0, 0.0)
    need = k_sel - cnt_gt

    def xbody(i, x):
        cand = x + lax.shift_left(jnp.int32(1), nbits - 1 - i)
        cnt = count_where(lambda kk, off: jnp.logical_and(kk == thr, off + lane_k < cand))
        return jnp.where(cnt < need, cand, x)

    xcut = lax.fori_loop(0, nbits, xbody, jnp.zeros((nb, 1), I32))
    xcut = jnp.where(finite_thr, xcut, -1)
    ties_past = count_where(lambda kk, off: kk == thr)
    own_tie = jnp.logical_and(jnp.logical_and(own_key == thr, finite_thr), ties_past < need)
    self_ref[...] = jnp.broadcast_to(
        jnp.where(jnp.logical_or(own_key > thr, own_tie), 1.0, 0.0), (nb, LANES))

    def sel_body(c, carry):
        off = pl.multiple_of(c * kc, kc)
        kk = keys_ref[:, pl.ds(off, kc)]
        sel = jnp.logical_or(kk > thr, jnp.logical_and(kk == thr, off + lane_k <= xcut))
        sel_ref[:, pl.ds(off, kc)] = jnp.where(sel, 1.0, 0.0)
        return carry

    lax.fori_loop(0, nck, sel_body, 0)


def _decselect(scores, qip, kib, wi):
    nb, past = scores.shape
    k_sel = min(TOPK_MAX, (past + 1) // 4)
    kc = next(c for c in (4096, 2048, 1024, 512, PAGE_SIZE) if past % c == 0)
    nbits = max(1, past.bit_length())
    full = lambda w: pl.BlockSpec((nb, w), lambda i: (0, 0))
    return pl.pallas_call(
        functools.partial(_decselect_kernel, k_sel=k_sel, kc=kc, nbits=nbits),
        grid=(1,),
        in_specs=[full(past), full(N_IDX_HEADS * LANES), full(LANES), full(LANES)],
        out_specs=[full(past), full(LANES)],
        out_shape=[jax.ShapeDtypeStruct((nb, past), F32), jax.ShapeDtypeStruct((nb, LANES), F32)],
        scratch_shapes=[pltpu.VMEM((nb, past), I32)],
        compiler_params=_cparams(("arbitrary",)),
        name="decode_select",
    )(scores, qip, kib, wi)


def _decattn_kernel(pt_ref, qp_ref, sel_ref, kn_ref, vn_ref, self_ref, *refs, n_steps):
    del pt_ref
    pg = (len(refs) - 4) // 2
    kp_refs, vp_refs = refs[:pg], refs[pg:2 * pg]
    o_ref, m_ref, l_ref, acc_ref = refs[2 * pg:]
    p = pl.program_id(1)

    @pl.when(p == 0)
    def _():
        m_ref[...] = jnp.full(m_ref.shape, NEG_BIG, F32)
        l_ref[...] = jnp.zeros(l_ref.shape, F32)
        acc_ref[...] = jnp.zeros(acc_ref.shape, F32)

    q = qp_ref[...]
    kcat = jnp.concatenate([r[...].astype(BF16) for r in kp_refs], axis=0)
    vcat = jnp.concatenate([r[...].astype(BF16) for r in vp_refs], axis=0)
    s = lax.dot_general(q.astype(BF16), kcat, NT_DIMS, preferred_element_type=F32)
    s = jnp.where(sel_ref[...] > 0.0, s, -jnp.inf)
    m_old = m_ref[:, 0:1]
    m_new = jnp.maximum(m_old, jnp.max(s, axis=1, keepdims=True))
    alpha = jnp.exp(m_old - m_new)
    pr = jnp.exp(s - m_new)
    l_new = alpha * l_ref[:, 0:1] + jnp.sum(pr, axis=1, keepdims=True)
    acc_ref[...] = alpha * acc_ref[...] + jnp.dot(pr.astype(BF16), vcat, preferred_element_type=F32)
    m_ref[...] = jnp.broadcast_to(m_new, m_ref.shape)
    l_ref[...] = jnp.broadcast_to(l_new, l_ref.shape)

    @pl.when(p == n_steps - 1)
    def _():
        s_own = jnp.sum(q * kn_ref[...], axis=1, keepdims=True)
        s_own = jnp.where(self_ref[:, 0:1] > 0.0, s_own, -jnp.inf)
        m_o = m_ref[:, 0:1]
        m_n = jnp.maximum(m_o, s_own)
        al = jnp.exp(m_o - m_n)
        p_own = jnp.exp(s_own - m_n)
        l_n = al * l_ref[:, 0:1] + p_own
        acc = al * acc_ref[...] + p_own.astype(BF16).astype(F32) * vn_ref[...]
        o_ref[...] = acc / l_n


def _decattn(page_table, qp, sel, pool_k, pool_v, kb, vb, self_sel, layer):
    nb, n_pages = page_table.shape
    pg = math.gcd(DECODE_PAGE_GROUP, n_pages)
    n_steps = n_pages // pg
    row = lambda r: pl.BlockSpec((None, r, LANES), lambda b, p, pt: (b, 0, 0))
    pages = _page_specs((None, None, PAGE_SIZE, LANES), layer, pg)
    grid_spec = pltpu.PrefetchScalarGridSpec(
        num_scalar_prefetch=1,
        grid=(nb, n_steps),
        in_specs=[row(N_HEADS),
                  pl.BlockSpec((None, None, 1, pg * PAGE_SIZE), lambda b, p, pt: (b, p, 0, 0)),
                  row(1), row(1), row(1)] + pages + pages,
        out_specs=row(N_HEADS),
        scratch_shapes=[pltpu.VMEM((N_HEADS, LANES), F32)] * 3,
    )
    return pl.pallas_call(
        functools.partial(_decattn_kernel, n_steps=n_steps),
        grid_spec=grid_spec,
        out_shape=jax.ShapeDtypeStruct((nb, N_HEADS, LANES), F32),
        compiler_params=_cparams(("parallel", "arbitrary")),
        name="decode_attend",
    )(page_table, qp, sel.reshape(nb, n_steps, 1, pg * PAGE_SIZE), kb, vb, self_sel,
      *([pool_k] * pg), *([pool_v] * pg))


def _pack_w_in(w):
    d = w.shape[0]
    pad = lambda n: jnp.zeros((d, n), w.dtype)
    cols = [w[:, 4172:7244],
            w[:, 0:2048],
            w[:, 2628:3140],
            w[:, 3140:4164],
            w[:, 2048:2304],
            w[:, 2304:2560],
            w[:, 2560:2628], pad(60),
            w[:, 4164:4172], pad(120),
            pad(N_PROJ - 7424)]
    return jnp.concatenate(cols, axis=1).astype(BF16)


def _pad_lanes(v):
    return jnp.zeros((1, LANES), F32).at[0, :v.shape[0]].set(v)


def _rope_tables(pos):
    half = HEAD_DIM // 2
    inv = ROPE_THETA ** (-jnp.arange(half, dtype=F32) / half)
    ang = pos.astype(F32)[:, None] * inv[None, :]
    cos, sin = jnp.cos(ang), jnp.sin(ang)
    cos2 = jnp.concatenate([cos, cos], axis=1)
    sin2 = jnp.concatenate([-sin, sin], axis=1)
    return jnp.tile(cos2, (1, 2)), jnp.tile(sin2, (1, 2))


def _head_halves(o):
    hpg = N_HEADS // N_KV_HEADS
    parts = [o[:, h, (h // hpg) * HEAD_DIM:(h // hpg + 1) * HEAD_DIM] for h in range(N_HEADS)]
    return jnp.concatenate(parts, axis=-1)


def kernel(x_prompt, x_sample, cache_k, cache_v, cache_kidx, state_conv, state_ssm_conv, state_ssm, page_table,
           norm1, w_in, b_gate, conv_w, q_norm, k_norm, w_pc, w_pa, ssm_conv_w, ssm_conv_b, dt_bias, a_log,
           d_skip, ssm_norm, w_ps, w_o, norm2, w_gate_dense, w_up_dense, w_down_dense, w_router, w_gate_moe,
           w_up_moe, w_down_moe):
    nb, seq, _ = x_prompt.shape
    db = x_sample.shape[0]
    depth = w_in.shape[0]
    n_pages = page_table.shape[1]
    past = n_pages * PAGE_SIZE

    hp = x_prompt.reshape(nb * seq, D_MODEL)
    hs = x_sample.reshape(db, D_MODEL)
    cos_p, sin_p = _rope_tables(jnp.tile(jnp.arange(seq), nb))
    cos_s, sin_s = _rope_tables(jnp.full((db,), past))

    pool_k = cache_k.reshape(*cache_k.shape[:3], LANES)
    pool_v = cache_v.reshape(*cache_v.shape[:3], LANES)

    outs_p = [[] for _ in range(6)]
    outs_s = [[] for _ in range(6)]
    for l in range(depth):
        wp = _pack_w_in(w_in[l])
        g1 = norm1[l].reshape(1, D_MODEL)
        g2 = norm2[l].reshape(1, D_MODEL)
        qg = jnp.tile(q_norm[l], 2).reshape(1, LANES)
        kg = jnp.tile(k_norm[l], 2).reshape(1, LANES)
        cw = ssm_conv_w[l]
        cbias = ssm_conv_b[l].reshape(1, SSM_CONV_DIM)
        dtb = _pad_lanes(dt_bias[l])
        alog = _pad_lanes(a_log[l])
        dsk = jnp.repeat(d_skip[l], SSM_HEAD_DIM).reshape(1, SSM_INNER)
        sn = ssm_norm[l].reshape(1, SSM_INNER)
        wpc, wpa, wps, wo = (w.astype(BF16) for w in (w_pc[l], w_pa[l], w_ps[l], w_o[l]))
        i = l // 2
        if l % 2 == 0:
            routed = False
            wr = jnp.zeros((D_MODEL, LANES), F32)
            wg, wu, wd = (w[i:i + 1].astype(BF16) for w in (w_gate_dense, w_up_dense, w_down_dense))
        else:
            routed = True
            wr = jnp.zeros((D_MODEL, LANES), F32).at[:, :N_EXPERTS].set(w_router[i])
            wg, wu, wd = (w[i].astype(BF16) for w in (w_gate_moe, w_up_moe, w_down_moe))

        proj = _inproj(hp, g1, wp)
        qt, kf, kb, vt, qit, kif, kib, wt = _prep(proj, cos_p, sin_p, qg, kg, True)
        attn = _dsa_prompt(qt, qit, wt, kb, vt, kib, nb, seq)
        pre_a, conv_st = _conva_prompt(proj, conv_w[l], nb, seq)
        y_ssd, h_last = _ssd_prompt(proj, cw, cbias, dtb, alog, dsk, nb, seq)
        hp = _merge(hp, pre_a, attn, y_ssd, proj, sn, b_gate[l], wpc, wpa, wps, wo)
        if routed and hp.shape[0] % MOE_TILE == 0:
            hp = _moe_grouped(hp, g2, wr, wg, wu, wd)
        else:
            hp = _ffn(hp, g2, wr, wg, wu, wd, routed)
        proj3 = proj.reshape(nb, seq, N_PROJ)
        outs_p[0].append(kf.reshape(nb, seq, N_KV_HEADS, HEAD_DIM))
        outs_p[1].append(proj3[:, :, V_OFF:V_OFF + LANES].reshape(nb, seq, N_KV_HEADS, HEAD_DIM))
        outs_p[2].append(kif[:, :IDX_DIM].reshape(nb, seq, IDX_DIM))
        outs_p[3].append(conv_st)
        outs_p[4].append(proj3[:, seq - 3:, XS_OFF:XS_OFF + SSM_CONV_DIM])
        outs_p[5].append(h_last.reshape(nb, SSM_HEADS, SSM_HEAD_DIM, SSM_STATE))

        proj = _inproj(hs, g1, wp)
        qp, kf, kb, vb, qip, kif, kib, wi = _prep(proj, cos_s, sin_s, qg, kg, False)
        wrows = jnp.broadcast_to(wi[:, :N_IDX_HEADS, None], (db, N_IDX_HEADS, PAGE_SIZE))
        scores = _decscore(page_table, qip.astype(F32).reshape(db, N_IDX_HEADS, LANES), wrows, cache_kidx, l)
        sel, self_sel = _decselect(scores, qip, kib, wi)
        o = _decattn(page_table, qp.astype(F32).reshape(db, N_HEADS, LANES), sel, pool_k, pool_v,
                     kb.astype(F32).reshape(db, 1, LANES), vb.astype(F32).reshape(db, 1, LANES),
                     self_sel.reshape(db, 1, LANES), l)
        attn = _head_halves(o)
        pre_a, u, xs, bcv, dt, ea = _decmix(proj, state_conv[l], state_ssm_conv[l], conv_w[l], cw, cbias, dtb, alog)
        hpg = SSM_HEADS // SSM_GROUPS
        xs4 = xs.reshape(db, SSM_HEADS, SSM_HEAD_DIM, 1)
        bh = jnp.repeat(bcv[:, :SSM_GROUPS * SSM_STATE].reshape(db, SSM_GROUPS, 1, SSM_STATE), hpg, axis=1)
        ch = jnp.repeat(bcv[:, SSM_GROUPS * SSM_STATE:].reshape(db, SSM_GROUPS, 1, SSM_STATE), hpg, axis=1)
        y4, h_new = _decssd(state_ssm[l], dt[:, :SSM_HEADS, None, None], xs4, bh, ch, ea[:, :SSM_HEADS, None, None],
                            d_skip[l].reshape(SSM_HEADS, 1, 1))
        hs = _merge(hs, pre_a, attn, y4.reshape(db, SSM_INNER), proj, sn, b_gate[l], wpc, wpa, wps, wo)
        hs = _ffn(hs, g2, wr, wg, wu, wd, routed)
        outs_s[0].append(kf.reshape(db, 1, N_KV_HEADS, HEAD_DIM))
        outs_s[1].append(proj[:, V_OFF:V_OFF + LANES].reshape(db, 1, N_KV_HEADS, HEAD_DIM))
        outs_s[2].append(kif[:, :IDX_DIM].reshape(db, 1, IDX_DIM))
        outs_s[3].append(jnp.stack([state_conv[l][:, 1], u], axis=1))
        outs_s[4].append(jnp.concatenate(
            [state_ssm_conv[l][:, 1:], proj[:, None, XS_OFF:XS_OFF + SSM_CONV_DIM]], axis=1))
        outs_s[5].append(h_new)

    return (hp.reshape(nb, seq, D_MODEL), hs.reshape(db, 1, D_MODEL),
            *(jnp.stack(o) for o in outs_p), *(jnp.stack(o) for o in outs_s))
```

```python
import functools
import math

import jax
import jax.numpy as jnp
from jax import lax
from jax.experimental import pallas as pl
from jax.experimental.pallas import tpu as pltpu

F32 = jnp.float32
BF16 = jnp.bfloat16
I32 = jnp.int32

D_MODEL = 1024
D_CONV = 512
N_HEADS = 8
HEAD_DIM = 64
N_KV_HEADS = 2
N_IDX_HEADS = 4
IDX_DIM = 64
TOPK_MAX = 256
Q_BLOCK = 128
ROPE_THETA = 10000.0
SSM_INNER = 512
SSM_HEADS = 8
SSM_HEAD_DIM = 64
SSM_GROUPS = 2
SSM_STATE = 128
SSM_CHUNK = 128
SSM_CONV_DIM = 1024
PAGE_SIZE = 128
N_EXPERTS = 8
EPS = 1e-6

LANES = 128
SUBLANES = 8
VMEM_LIMIT = 52 * 1024 * 1024

G_OFF, CB_OFF, CC_OFF, CX_OFF, Q_OFF, Z_OFF, XS_OFF, BC_OFF = 0, 3072, 3584, 4096, 4608, 5120, 5632, 6144
K_OFF, V_OFF, QI_OFF, KW_OFF, DT_OFF = 6656, 6784, 6912, 7168, 7296
N_PROJ = 7680

INT_MIN = -2 ** 31
INT_MAX = 2 ** 31 - 1
KEY_NEG_INF = (-8388608) ^ 0x7FFFFFFF
NEG_BIG = -1e30
LOG2_E = 1.4426950408889634

NT_DIMS = (((1,), (1,)), ((), ()))


def _cparams(sem):
    return pltpu.CompilerParams(dimension_semantics=sem, vmem_limit_bytes=VMEM_LIMIT)


def _sigmoid(x):
    return 1.0 / (1.0 + jnp.exp(-x))


def _silu(x):
    return x * _sigmoid(x)


def _softplus(x):
    return jnp.maximum(x, 0.0) + jnp.log1p(jnp.exp(-jnp.abs(x)))


def _sortable(x):
    bits = pltpu.bitcast(x, I32)
    return bits ^ (lax.shift_right_arithmetic(bits, 31) & 0x7FFFFFFF)


def _rms(x, g):
    return x * lax.rsqrt(jnp.mean(x * x, axis=-1, keepdims=True) + EPS) * g


def _inproj_kernel(x_ref, g_ref, w_ref, o_ref, h_ref):
    @pl.when(pl.program_id(1) == 0)
    def _():
        h_ref[...] = _rms(x_ref[...], g_ref[...]).astype(BF16)

    o_ref[...] = jnp.dot(h_ref[...], w_ref[...], preferred_element_type=F32)


def _inproj(x, g, w):
    m = x.shape[0]
    tm = min(m, 1024)
    tn = 1536
    return pl.pallas_call(
        _inproj_kernel,
        grid=(m // tm, N_PROJ // tn),
        in_specs=[pl.BlockSpec((tm, D_MODEL), lambda i, j: (i, 0)),
                  pl.BlockSpec((1, D_MODEL), lambda i, j: (0, 0)),
                  pl.BlockSpec((D_MODEL, tn), lambda i, j: (0, j))],
        out_specs=pl.BlockSpec((tm, tn), lambda i, j: (i, j)),
        out_shape=jax.ShapeDtypeStruct((m, N_PROJ), F32),
        scratch_shapes=[pltpu.VMEM((tm, D_MODEL), BF16)],
        compiler_params=_cparams(("parallel", "arbitrary")),
        name="inproj",
    )(x, g, w)


V_T_ROWS = LANES + 16


def _prep_kernel(q_ref, k_ref, v_ref, qi_ref, kw_ref, cos_ref, sin_ref, qg_ref, kg_ref,
                 qp_ref, kf_ref, kb_ref, vb_ref, qip_ref, kif_ref, kib_ref, wo_ref, *, transposed):
    tm = cos_ref.shape[0]

    def put(ref, h, val):
        if transposed:
            vt = val.T.astype(BF16)
            for blk in range(tm // Q_BLOCK):
                ref[blk, h * LANES:(h + 1) * LANES, :] = vt[:, blk * Q_BLOCK:(blk + 1) * Q_BLOCK]
        else:
            ref[:, h * LANES:(h + 1) * LANES] = val.astype(BF16)

    cos = cos_ref[...]
    sin = sin_ref[...]
    lane = lax.broadcasted_iota(I32, (tm, LANES), 1)
    low_half = lane < HEAD_DIM
    first_rot = (lane % HEAD_DIM) < (HEAD_DIM // 2)
    r = lax.broadcasted_iota(I32, (LANES, LANES), 0) // HEAD_DIM
    c = lax.broadcasted_iota(I32, (LANES, LANES), 1) // HEAD_DIM
    seg = jnp.where(r == c, 1.0 / HEAD_DIM, 0.0).astype(BF16)

    def rope(x):
        fwd = pltpu.roll(x, LANES - HEAD_DIM // 2, 1)
        bwd = pltpu.roll(x, HEAD_DIM // 2, 1)
        return x * cos + jnp.where(first_rot, fwd, bwd) * sin

    def headnorm(x, g):
        s = x * x
        hi = s.astype(BF16)
        lo = (s - hi.astype(F32)).astype(BF16)
        ms = (jnp.dot(hi, seg, preferred_element_type=F32)
              + jnp.dot(lo, seg, preferred_element_type=F32))
        return x * lax.rsqrt(ms + EPS) * g

    q_scale = HEAD_DIM ** -0.5 * (LOG2_E if transposed else 1.0)
    qg = qg_ref[...]
    for s in range(N_HEADS // 2):
        slab = rope(headnorm(q_ref[:, s * LANES:(s + 1) * LANES], qg)) * q_scale
        swapped = pltpu.roll(slab, HEAD_DIM, 1)
        grp = (2 * s) // (N_HEADS // N_KV_HEADS)
        for hh in range(2):
            h = 2 * s + hh
            src = slab if hh == grp else swapped
            keep = low_half if grp == 0 else jnp.logical_not(low_half)
            put(qp_ref, h, jnp.where(keep, src, 0.0))

    k = rope(headnorm(k_ref[...], kg_ref[...]))
    kf_ref[...] = k
    kb_ref[...] = k.astype(BF16)
    if transposed:
        vb_ref[0:LANES, :] = v_ref[...].T.astype(BF16)
        vb_ref[LANES:V_T_ROWS, :] = jnp.ones((V_T_ROWS - LANES, tm), BF16)
    else:
        vb_ref[...] = v_ref[...].astype(BF16)

    for s in range(N_IDX_HEADS // 2):
        slab = rope(qi_ref[:, s * LANES:(s + 1) * LANES]) * (IDX_DIM ** -0.5)
        swapped = pltpu.roll(slab, HEAD_DIM, 1)
        for hh in range(2):
            h = 2 * s + hh
            src = slab if hh == 0 else swapped
            put(qip_ref, h, jnp.where(low_half, src, 0.0))

    kw = kw_ref[...]
    ki = rope(kw)
    kif_ref[...] = ki
    kib_ref[...] = jnp.where(low_half, ki, 0.0).astype(BF16)
    wi = pltpu.roll(kw, HEAD_DIM, 1) * (N_IDX_HEADS ** -0.5)
    if transposed:
        wt = wi.T[0:SUBLANES, :]
        for blk in range(tm // Q_BLOCK):
            wo_ref[blk] = wt[:, blk * Q_BLOCK:(blk + 1) * Q_BLOCK]
    else:
        wo_ref[...] = wi


def _prep(proj, cos, sin, qg, kg, transposed):
    m = proj.shape[0]
    tm = min(m, 512)
    row = lambda w, off: pl.BlockSpec((tm, w), lambda i: (i, off // w))
    full = lambda w: pl.BlockSpec((tm, w), lambda i: (i, 0))
    const = pl.BlockSpec((1, LANES), lambda i: (0, 0))
    if transposed:
        qblocks = tm // Q_BLOCK
        feat = lambda r: pl.BlockSpec((qblocks, r, Q_BLOCK), lambda i: (i, 0, 0))
        fshape = lambda r, dt: jax.ShapeDtypeStruct((m // Q_BLOCK, r, Q_BLOCK), dt)
        vspec = pl.BlockSpec((V_T_ROWS, tm), lambda i: (0, i))
        vshape = jax.ShapeDtypeStruct((V_T_ROWS, m), BF16)
    else:
        feat = full
        fshape = lambda r, dt: jax.ShapeDtypeStruct((m, r), dt)
        vspec = full(LANES)
        vshape = jax.ShapeDtypeStruct((m, LANES), BF16)
    return pl.pallas_call(
        functools.partial(_prep_kernel, transposed=transposed),
        grid=(m // tm,),
        in_specs=[row(512, Q_OFF), row(LANES, K_OFF), row(LANES, V_OFF), row(256, QI_OFF), row(LANES, KW_OFF),
                  full(LANES), full(LANES), const, const],
        out_specs=[feat(N_HEADS * LANES), full(LANES), full(LANES), vspec,
                   feat(N_IDX_HEADS * LANES), full(LANES), full(LANES), feat(SUBLANES if transposed else LANES)],
        out_shape=[fshape(N_HEADS * LANES, BF16),
                   jax.ShapeDtypeStruct((m, LANES), F32),
                   jax.ShapeDtypeStruct((m, LANES), BF16),
                   vshape,
                   fshape(N_IDX_HEADS * LANES, BF16),
                   jax.ShapeDtypeStruct((m, LANES), F32),
                   jax.ShapeDtypeStruct((m, LANES), BF16),
                   fshape(SUBLANES if transposed else LANES, F32)],
        compiler_params=_cparams(("parallel",)),
        name="prep",
    )(proj, proj, proj, proj, proj, cos, sin, qg, kg)


def _fold_rows(f):
    while f.shape[0] > SUBLANES:
        half = f.shape[0] // 2
        f = f[:half] + f[half:]
    return f


def _dsa_kernel(qt_ref, qit_ref, wt_ref, k_ref, vt_ref, ki_ref, o_ref,
                keys_ref, tpos_ref, x_ref, m_ref, kmax_ref, acc_ref, *, k_sel, kc, nbits):
    qb = Q_BLOCK
    n_keys = k_ref.shape[0]
    assert n_keys < 2 ** 22
    j = pl.program_id(1)
    nck = (j * qb + qb + kc - 1) // kc
    qpos = j * qb + lax.broadcasted_iota(I32, (1, qb), 1)
    sub_k = lax.broadcasted_iota(I32, (kc, qb), 0)
    pair = lambda ref, s: jnp.concatenate(
        [ref[(2 * s) * LANES:(2 * s + 1) * LANES, :], ref[(2 * s + 1) * LANES:(2 * s + 2) * LANES, :]], axis=1)
    qi2 = [pair(qit_ref, s) for s in range(N_IDX_HEADS // 2)]
    q2 = [pair(qt_ref, s) for s in range(N_HEADS // 2)]

    def score_body(c, carry):
        off = pl.multiple_of(c * kc, kc)
        kic = ki_ref[pl.ds(off, kc), :]
        acc = jnp.zeros((kc, qb), F32)
        for s in range(N_IDX_HEADS // 2):
            s2 = jnp.dot(kic, qi2[s], preferred_element_type=F32)
            for hh in range(2):
                h = 2 * s + hh
                acc = acc + jnp.maximum(s2[:, hh * qb:(hh + 1) * qb], 0.0) * wt_ref[h:h + 1, :]
        kpos = off + sub_k
        key = _sortable(jnp.where(kpos <= qpos, acc, -jnp.inf))
        key = jnp.where(key == 0, (n_keys - 1) - kpos, jnp.where(key == -1, -1 - kpos, key))
        keys_ref[pl.ds(off, kc), :] = key
        return carry

    lax.fori_loop(0, nck, score_body, 0)

    def count_where(pred, ref=keys_ref):
        def body(c, acc):
            off = pl.multiple_of(c * kc, kc)
            return acc + _fold_rows(jnp.where(pred(ref[pl.ds(off, kc), :], off), 1.0, 0.0))

        acc = lax.fori_loop(0, nck, body, jnp.zeros((SUBLANES, qb), F32))
        return jnp.sum(acc, axis=0, keepdims=True)

    def bit_body(i, carry):
        t, cnt_acc, cnt_rej = carry
        cand = t + lax.shift_left(jnp.int32(1), 31 - i)
        cnt = count_where(lambda kk, off: kk >= cand)
        ok = cnt >= k_sel
        return jnp.where(ok, cand, t), jnp.where(ok, cnt, cnt_acc), jnp.where(ok, cnt_rej, cnt)

    zero = jnp.zeros((1, qb), F32)
    thr, cnt_ge, cnt_gt = lax.fori_loop(0, 32, bit_body, (jnp.full((1, qb), INT_MIN, I32), zero, zero))
    thr = jnp.maximum(thr, KEY_NEG_INF)
    finite_thr = thr > KEY_NEG_INF
    need = k_sel - cnt_gt
    tie = jnp.logical_and(cnt_ge > k_sel, finite_thr)
    x_default = jnp.where(finite_thr, INT_MAX, -1)
    x_ref[...] = jnp.broadcast_to(x_default, x_ref.shape)

    @pl.when(jnp.max(jnp.where(tie, 1.0, 0.0)) > 0.0)
    def _():
        def tie_pos_body(c, carry):
            off = pl.multiple_of(c * kc, kc)
            tpos_ref[pl.ds(off, kc), :] = jnp.where(keys_ref[pl.ds(off, kc), :] == thr, off + sub_k, INT_MAX)
            return carry

        lax.fori_loop(0, nck, tie_pos_body, 0)

        def xbody(i, x):
            cand = x + lax.shift_left(jnp.int32(1), nbits - 1 - i)
            cnt = count_where(lambda tp, off: tp < cand, tpos_ref)
            return jnp.where(cnt < need, cand, x)

        x = lax.fori_loop(0, nbits, xbody, jnp.zeros((1, qb), I32))
        x_ref[...] = jnp.broadcast_to(jnp.where(tie, x, x_default), x_ref.shape)

    xcut = x_ref[0:1, :]
    hpg = N_HEADS // N_KV_HEADS

    @pl.when(j == 0)
    def _():
        r = lax.broadcasted_iota(I32, (LANES, LANES), 0) // HEAD_DIM
        cc = lax.broadcasted_iota(I32, (LANES, LANES), 1) // HEAD_DIM
        seg = jnp.where(r == cc, 1.0, 0.0).astype(BF16)

        def body(c, mx):
            kf = k_ref[pl.ds(pl.multiple_of(c * kc, kc), kc), :].astype(F32)
            gs = jnp.dot((kf * kf).astype(BF16), seg, preferred_element_type=F32)
            while gs.shape[0] > SUBLANES:
                half = gs.shape[0] // 2
                gs = jnp.maximum(gs[:half], gs[half:])
            return jnp.maximum(mx, gs)

        mx = lax.fori_loop(0, k_ref.shape[0] // kc, body, jnp.zeros((SUBLANES, LANES), F32))
        mx = jnp.max(mx, axis=0, keepdims=True)
        lane = lax.broadcasted_iota(I32, (1, LANES), 1)
        other = pltpu.roll(mx, HEAD_DIM, 1)
        kmax_ref[0:1, :] = jnp.where(lane < HEAD_DIM, mx, other)
        kmax_ref[1:2, :] = jnp.where(lane < HEAD_DIM, other, mx)

    def chunk(c):
        off = pl.multiple_of(c * kc, kc)
        kk = keys_ref[pl.ds(off, kc), :]
        sel = jnp.logical_or(kk > thr, jnp.logical_and(kk == thr, off + sub_k <= xcut))
        return sel, k_ref[pl.ds(off, kc), :], vt_ref[:, pl.ds(off, kc)]

    bound = []
    for h in range(N_HEADS):
        qh = qt_ref[h * LANES:(h + 1) * LANES, :].astype(F32)
        qsq = jnp.sum(qh * qh, axis=0, keepdims=True)
        bound.append(jnp.sqrt(qsq * kmax_ref[h // hpg:h // hpg + 1, :]))
    acc_ref[...] = jnp.zeros(acc_ref.shape, F32)

    def fast_body(c, carry):
        sel, kch, vch = chunk(c)
        n_pair = N_HEADS // 2
        s2 = jnp.dot(kch, q2[0], preferred_element_type=F32)
        for s in range(n_pair):
            s2_next = jnp.dot(kch, q2[s + 1], preferred_element_type=F32) if s + 1 < n_pair else None
            ps = [jnp.exp2(jnp.where(sel, s2[:, hh * qb:(hh + 1) * qb] - bound[2 * s + hh], -jnp.inf)).astype(BF16)
                  for hh in range(2)]
            acc_ref[s] += jnp.dot(vch, jnp.concatenate(ps, axis=1), preferred_element_type=F32)
            s2 = s2_next
        return carry

    lax.fori_loop(0, nck, fast_body, 0)

    dens = jnp.concatenate([acc_ref[s, LANES:LANES + 1, :] for s in range(N_HEADS // 2)], axis=0)

    @pl.when(jnp.logical_not(jnp.min(dens) >= 1e-20))
    def _():
        m_ref[...] = jnp.full(m_ref.shape, NEG_BIG, F32)
        acc_ref[...] = jnp.zeros(acc_ref.shape, F32)

        def exact_body(c, carry):
            sel, kch, vch = chunk(c)
            for s in range(N_HEADS // 2):
                s2 = jnp.dot(kch, q2[s], preferred_element_type=F32)
                ps, alphas = [], []
                for hh in range(2):
                    h = 2 * s + hh
                    sh = jnp.where(sel, s2[:, hh * qb:(hh + 1) * qb], -jnp.inf)
                    m_old = m_ref[h:h + 1, :]
                    m_new = jnp.maximum(m_old, jnp.max(sh, axis=0, keepdims=True))
                    m_ref[h:h + 1, :] = m_new
                    alphas.append(jnp.exp2(m_old - m_new))
                    ps.append(jnp.exp2(sh - m_new).astype(BF16))
                o2 = jnp.dot(vch, jnp.concatenate(ps, axis=1), preferred_element_type=F32)
                acc_ref[s] = jnp.concatenate(alphas, axis=1) * acc_ref[s] + o2
            return carry

        lax.fori_loop(0, nck, exact_body, 0)

    for s in range(N_HEADS // 2):
        g = (2 * s) // hpg
        num = acc_ref[s, g * HEAD_DIM:(g + 1) * HEAD_DIM, :]
        out2 = num / acc_ref[s, LANES:LANES + 1, :]
        o_ref[:, s * LANES:(s + 1) * LANES] = jnp.concatenate([out2[:, :qb], out2[:, qb:]], axis=0).T


def _dsa_prompt(qt, qit, wt, kb, vt, kib, nb, seq):
    k_sel = min(TOPK_MAX, seq // 4)
    kc = next(c for c in (512, Q_BLOCK) if seq % c == 0)
    nbits = max(1, (seq - 1).bit_length())
    nq = seq // Q_BLOCK
    r3 = lambda a: a.reshape(nb, seq, a.shape[-1])
    qcol = lambda r: pl.BlockSpec((None, r, Q_BLOCK), lambda b, j: (b * nq + j, 0, 0))
    seqblk = pl.BlockSpec((None, seq, LANES), lambda b, j: (b, 0, 0))
    out = pl.pallas_call(
        functools.partial(_dsa_kernel, k_sel=k_sel, kc=kc, nbits=nbits),
        grid=(nb, nq),
        in_specs=[qcol(N_HEADS * LANES), qcol(N_IDX_HEADS * LANES), qcol(SUBLANES), seqblk,
                  pl.BlockSpec((V_T_ROWS, seq), lambda b, j: (0, b)), seqblk],
        out_specs=pl.BlockSpec((None, Q_BLOCK, N_HEADS * HEAD_DIM), lambda b, j: (b, j, 0)),
        out_shape=jax.ShapeDtypeStruct((nb, seq, N_HEADS * HEAD_DIM), F32),
        scratch_shapes=[pltpu.VMEM((seq, Q_BLOCK), I32),
                        pltpu.VMEM((seq, Q_BLOCK), I32),
                        pltpu.VMEM((SUBLANES, Q_BLOCK), I32),
                        pltpu.VMEM((N_HEADS, Q_BLOCK), F32),
                        pltpu.VMEM((SUBLANES, LANES), F32),
                        pltpu.VMEM((N_HEADS // 2, V_T_ROWS, 2 * Q_BLOCK), F32)],
        compiler_params=_cparams(("arbitrary", "arbitrary")),
        name="dsa_prompt",
    )(qt, qit, wt, r3(kb), vt, r3(kib))
    return out.reshape(nb * seq, N_HEADS * HEAD_DIM)


def _conva_kernel(cb_ref, cc_ref, cx_ref, cch_ref, cxh_ref, w_ref, o_ref, st_ref, *, tiles_per_seq):
    tm = cb_ref.shape[0]
    first = (pl.program_id(0) % tiles_per_seq) == 0
    u = cc_ref[...] * cx_ref[...]
    uh = jnp.where(first, 0.0, cch_ref[...] * cxh_ref[...])
    ext = jnp.concatenate([uh, u], axis=0)
    conv = (w_ref[2:3, :] * u + w_ref[1:2, :] * ext[SUBLANES - 1:SUBLANES - 1 + tm]
            + w_ref[0:1, :] * ext[SUBLANES - 2:SUBLANES - 2 + tm])
    o_ref[...] = cb_ref[...] * conv
    st_ref[...] = u[tm - 2:tm, :]


def _conva_prompt(proj, conv_w, nb, seq):
    m = proj.shape[0]
    tm = min(seq, 512)
    tps = seq // tm
    col = lambda off: pl.BlockSpec((tm, D_CONV), lambda i: (i, off // D_CONV))
    halo = lambda off: pl.BlockSpec(
        (SUBLANES, D_CONV), lambda i: (jnp.maximum(i * (tm // SUBLANES) - 1, 0), off // D_CONV))
    return pl.pallas_call(
        functools.partial(_conva_kernel, tiles_per_seq=tps),
        grid=(m // tm,),
        in_specs=[col(CB_OFF), col(CC_OFF), col(CX_OFF), halo(CC_OFF), halo(CX_OFF),
                  pl.BlockSpec((3, D_CONV), lambda i: (0, 0))],
        out_specs=[pl.BlockSpec((tm, D_CONV), lambda i: (i, 0)),
                   pl.BlockSpec((None, 2, D_CONV), lambda i: (i // tps, 0, 0))],
        out_shape=[jax.ShapeDtypeStruct((m, D_CONV), F32),
                   jax.ShapeDtypeStruct((nb, 2, D_CONV), F32)],
        compiler_params=_cparams(("arbitrary",)),
        name="conva_prompt",
    )(proj, proj, proj, proj, proj, conv_w)


def _ssd_kernel(xs_ref, bc_ref, dt_ref, xh_ref, bh_ref, cwx_ref, cwb_ref, cbx_ref, cbb_ref,
                dtb_ref, alog_ref, dsk_ref, y_ref, hout_ref, h_ref, *, nchunk):
    cl = SSM_CHUNK
    c = pl.program_id(1)
    first = c == 0

    @pl.when(first)
    def _():
        h_ref[...] = jnp.zeros(h_ref.shape, F32)

    def conv(cur, halo, w_ref, b_ref):
        ext = jnp.concatenate([jnp.where(first, 0.0, halo), cur], axis=0)
        out = (w_ref[3:4, :] * cur + w_ref[2:3, :] * ext[SUBLANES - 1:SUBLANES - 1 + cl]
               + w_ref[1:2, :] * ext[SUBLANES - 2:SUBLANES - 2 + cl]
               + w_ref[0:1, :] * ext[SUBLANES - 3:SUBLANES - 3 + cl] + b_ref[...])
        return _silu(out)

    xs = conv(xs_ref[...], xh_ref[...], cwx_ref, cbx_ref)
    bc = conv(bc_ref[...], bh_ref[...], cwb_ref, cbb_ref)
    dt = _softplus(dt_ref[...] + dtb_ref[...])
    a = dt * (-jnp.exp(alog_ref[...]))
    ri = lax.broadcasted_iota(I32, (cl, cl), 0)
    ci = lax.broadcasted_iota(I32, (cl, cl), 1)
    causal = ri >= ci
    cs = jnp.dot(jnp.where(causal, 1.0, 0.0), a, preferred_element_type=F32,
                 precision=lax.Precision.HIGHEST)
    cs_t = cs.T
    lane = lax.broadcasted_iota(I32, (cl, LANES), 1)
    lo = lane < SSM_HEAD_DIM
    rows_lo = lax.broadcasted_iota(I32, (LANES, 1), 0) < SSM_HEAD_DIM
    heads_per_group = SSM_HEADS // SSM_GROUPS

    cb = []
    for g in range(SSM_GROUPS):
        bg = bc[:, g * SSM_STATE:(g + 1) * SSM_STATE].astype(BF16)
        cg = bc[:, (SSM_GROUPS + g) * SSM_STATE:(SSM_GROUPS + g + 1) * SSM_STATE].astype(BF16)
        cb.append((bg, cg, lax.dot_general(cg, bg, NT_DIMS, preferred_element_type=F32)))

    for s in range(SSM_HEADS // 2):
        h0, h1 = 2 * s, 2 * s + 1
        bg, cg, cbg = cb[h0 // heads_per_group]
        sl = slice(s * LANES, (s + 1) * LANES)
        xs_s = xs[:, sl]
        col0, col1 = cs[:, h0:h0 + 1], cs[:, h1:h1 + 1]
        last0, last1 = cs[cl - 1:cl, h0:h0 + 1], cs[cl - 1:cl, h1:h1 + 1]
        xdt = xs_s * jnp.where(lo, dt[:, h0:h0 + 1], dt[:, h1:h1 + 1])
        xdt_b = xdt.astype(BF16)
        m0 = (cbg * jnp.where(causal, jnp.exp(col0 - cs_t[h0:h0 + 1, :]), 0.0)).astype(BF16)
        m1 = (cbg * jnp.where(causal, jnp.exp(col1 - cs_t[h1:h1 + 1, :]), 0.0)).astype(BF16)
        y_diag = jnp.where(lo, jnp.dot(m0, xdt_b, preferred_element_type=F32),
                           jnp.dot(m1, xdt_b, preferred_element_type=F32))
        hs = h_ref[sl, :]
        y_off = lax.dot_general(cg, hs.astype(BF16), NT_DIMS, preferred_element_type=F32)
        y_off = y_off * jnp.where(lo, jnp.exp(col0), jnp.exp(col1))
        y_ref[:, sl] = y_diag + y_off + dsk_ref[:, sl] * xs_s
        xw = xdt * jnp.where(lo, jnp.exp(last0 - col0), jnp.exp(last1 - col1))
        st = jnp.dot(xw.T.astype(BF16), bg, preferred_element_type=F32)
        h_ref[sl, :] = hs * jnp.where(rows_lo, jnp.exp(last0), jnp.exp(last1)) + st

    @pl.when(c == nchunk - 1)
    def _():
        hout_ref[...] = h_ref[...]


def _ssd_prompt(proj, cw, cbias, dtb, alog, dsk, nb, seq):
    m = proj.shape[0]
    cl = SSM_CHUNK
    nchunk = seq // cl
    blk = lambda w, off: pl.BlockSpec((cl, w), lambda b, c: (b * nchunk + c, off // w))
    halo = lambda off: pl.BlockSpec(
        (SUBLANES, 512), lambda b, c: (jnp.maximum((b * nchunk + c) * (cl // SUBLANES) - 1, 0), off // 512))
    const = lambda r, w, j: pl.BlockSpec((r, w), lambda b, c: (0, j))
    y, hout = pl.pallas_call(
        functools.partial(_ssd_kernel, nchunk=nchunk),
        grid=(nb, nchunk),
        in_specs=[blk(512, XS_OFF), blk(512, BC_OFF), blk(LANES, DT_OFF), halo(XS_OFF), halo(BC_OFF),
                  const(4, 512, 0), const(4, 512, 1), const(1, 512, 0), const(1, 512, 1),
                  const(1, LANES, 0), const(1, LANES, 0), const(1, 512, 0)],
        out_specs=[pl.BlockSpec((cl, SSM_INNER), lambda b, c: (b * nchunk + c, 0)),
                   pl.BlockSpec((None, SSM_INNER, SSM_STATE), lambda b, c: (b, 0, 0))],
        out_shape=[jax.ShapeDtypeStruct((m, SSM_INNER), F32),
                   jax.ShapeDtypeStruct((nb, SSM_INNER, SSM_STATE), F32)],
        scratch_shapes=[pltpu.VMEM((SSM_INNER, SSM_STATE), F32)],
        compiler_params=_cparams(("parallel", "arbitrary")),
        name="ssd_prompt",
    )(proj, proj, proj, proj, proj, cw, cw, cbias, cbias, dtb, alog, dsk)
    return y, hout


def _merge_kernel(x_ref, pa_ref, at_ref, ys_ref, z_ref, g0_ref, g1_ref, g2_ref, sn_ref, bg_ref,
                  wpc_ref, wpa_ref, wps_ref, wo_ref, o_ref):
    ssd = _rms(ys_ref[...] * _silu(z_ref[...]), sn_ref[...])
    ya = jnp.dot(pa_ref[...].astype(BF16), wpc_ref[...], preferred_element_type=F32)
    yb = jnp.dot(at_ref[...].astype(BF16), wpa_ref[...], preferred_element_type=F32)
    yc = jnp.dot(ssd.astype(BF16), wps_ref[...], preferred_element_type=F32)
    merged = (_sigmoid(g0_ref[...] + bg_ref[0:1, :]) * ya + _sigmoid(g1_ref[...] + bg_ref[1:2, :]) * yb
              + _sigmoid(g2_ref[...] + bg_ref[2:3, :]) * yc)
    o_ref[...] = x_ref[...] + jnp.dot(merged.astype(BF16), wo_ref[...], preferred_element_type=F32)


def _merge(x, pre_a, attn, y_ssd, proj, ssm_norm, b_gate, w_pc, w_pa, w_ps, w_o):
    m = x.shape[0]
    tm = min(m, 256)
    row = lambda w: pl.BlockSpec((tm, w), lambda i: (i, 0))
    pcol = lambda w, off: pl.BlockSpec((tm, w), lambda i: (i, off // w))
    const = lambda r, w: pl.BlockSpec((r, w), lambda i: (0, 0))
    return pl.pallas_call(
        _merge_kernel,
        grid=(m // tm,),
        in_specs=[row(D_MODEL), row(512), row(512), row(512), pcol(512, Z_OFF),
                  pcol(D_MODEL, G_OFF), pcol(D_MODEL, G_OFF + D_MODEL), pcol(D_MODEL, G_OFF + 2 * D_MODEL),
                  const(1, 512), const(3, D_MODEL),
                  const(512, D_MODEL), const(512, D_MODEL), const(512, D_MODEL), const(D_MODEL, D_MODEL)],
        out_specs=row(D_MODEL),
        out_shape=jax.ShapeDtypeStruct((m, D_MODEL), F32),
        compiler_params=_cparams(("parallel",)),
        name="merge",
    )(x, pre_a, attn, y_ssd, proj, proj, proj, proj, ssm_norm, b_gate, w_pc, w_pa, w_ps, w_o)


def _ffn_kernel(x_ref, g_ref, wr_ref, wg_ref, wu_ref, wd_ref, o_ref, h_ref, acc_ref, comb_ref,
                *, routed, n_e, n_f):
    e = pl.program_id(1)
    j = pl.program_id(2)
    tm = x_ref.shape[0]

    @pl.when(jnp.logical_and(e == 0, j == 0))
    def _():
        hf = _rms(x_ref[...], g_ref[...])
        h_ref[...] = hf.astype(BF16)
        acc_ref[...] = jnp.zeros(acc_ref.shape, F32)
        if routed:
            lane = lax.broadcasted_iota(I32, (tm, LANES), 1).astype(F32)
            logits = jnp.dot(hf, wr_ref[...], preferred_element_type=F32, precision=lax.Precision.HIGHEST)
            logits = jnp.where(lane < n_e, logits, -jnp.inf)
            m1 = jnp.max(logits, axis=1, keepdims=True)
            i1 = jnp.min(jnp.where(logits == m1, lane, float(LANES)), axis=1, keepdims=True)
            rest = jnp.where(lane == i1, -jnp.inf, logits)
            m2 = jnp.max(rest, axis=1, keepdims=True)
            i2 = jnp.min(jnp.where(rest == m2, lane, float(LANES)), axis=1, keepdims=True)
            e2 = jnp.exp(m2 - m1)
            den = 1.0 + e2
            comb_ref[...] = jnp.where(lane == i1, 1.0 / den, 0.0) + jnp.where(lane == i2, e2 / den, 0.0)

    h = h_ref[...]
    gate = jnp.dot(h, wg_ref[...], preferred_element_type=F32)
    up = jnp.dot(h, wu_ref[...], preferred_element_type=F32)
    act = _silu(gate) * up
    if routed:
        lane = lax.broadcasted_iota(I32, (tm, LANES), 1)
        act = act * jnp.sum(jnp.where(lane == e, comb_ref[...], 0.0), axis=1, keepdims=True)
    acc_ref[...] += jnp.dot(act.astype(BF16), wd_ref[...], preferred_element_type=F32)

    @pl.when(jnp.logical_and(e == n_e - 1, j == n_f - 1))
    def _():
        o_ref[...] = x_ref[...] + acc_ref[...]


def _ffn(x, g, w_router, w_gate, w_up, w_down, routed):
    m = x.shape[0]
    n_e, _, d_ff = w_gate.shape
    tm = min(m, 1024)
    tf = 256
    n_f = d_ff // tf
    return pl.pallas_call(
        functools.partial(_ffn_kernel, routed=routed, n_e=n_e, n_f=n_f),
        grid=(m // tm, n_e, n_f),
        in_specs=[pl.BlockSpec((tm, D_MODEL), lambda i, e, j: (i, 0)),
                  pl.BlockSpec((1, D_MODEL), lambda i, e, j: (0, 0)),
                  pl.BlockSpec((D_MODEL, LANES), lambda i, e, j: (0, 0)),
                  pl.BlockSpec((None, D_MODEL, tf), lambda i, e, j: (e, 0, j)),
                  pl.BlockSpec((None, D_MODEL, tf), lambda i, e, j: (e, 0, j)),
                  pl.BlockSpec((None, tf, D_MODEL), lambda i, e, j: (e, j, 0))],
        out_specs=pl.BlockSpec((tm, D_MODEL), lambda i, e, j: (i, 0)),
        out_shape=jax.ShapeDtypeStruct((m, D_MODEL), F32),
        scratch_shapes=[pltpu.VMEM((tm, D_MODEL), BF16), pltpu.VMEM((tm, D_MODEL), F32),
                        pltpu.VMEM((tm, LANES), F32)],
        compiler_params=_cparams(("parallel", "arbitrary", "arbitrary")),
        name="moe" if routed else "ffn",
    )(x, g, w_router, w_gate, w_up, w_down)


MOE_BLOCK = 2048
MOE_TILE = 256
TOP_K = 2


def _moe_tiles(tb):
    return TOP_K * tb // MOE_TILE + N_EXPERTS


def _route_kernel(x_ref, g_ref, wr_ref, hb_ref, row_ref, col_ref, meta_ref):
    tb = x_ref.shape[0]
    hf = _rms(x_ref[...], g_ref[...])
    hb_ref[...] = hf.astype(BF16)
    lane_i = lax.broadcasted_iota(I32, (tb, LANES), 1)
    lane = lane_i.astype(F32)
    logits = jnp.dot(hf, wr_ref[...], preferred_element_type=F32, precision=lax.Precision.HIGHEST)
    logits = jnp.where(lane_i < N_EXPERTS, logits, -jnp.inf)
    m1 = jnp.max(logits, axis=1, keepdims=True)
    i1 = jnp.min(jnp.where(logits == m1, lane, float(LANES)), axis=1, keepdims=True)
    rest = jnp.where(lane == i1, -jnp.inf, logits)
    m2 = jnp.max(rest, axis=1, keepdims=True)
    i2 = jnp.min(jnp.where(rest == m2, lane, float(LANES)), axis=1, keepdims=True)
    e2 = jnp.exp(m2 - m1)
    p1 = 1.0 / (1.0 + e2)
    p2 = e2 / (1.0 + e2)
    oh1 = lane == i1
    oh2 = lane == i2
    both = jnp.where(jnp.logical_or(oh1, oh2), 1.0, 0.0).astype(BF16)

    sb = MOE_TILE
    ri = lax.broadcasted_iota(I32, (sb, sb), 0)
    ci = lax.broadcasted_iota(I32, (sb, sb), 1)
    strict = jnp.where(ri > ci, 1.0, 0.0).astype(BF16)
    carry = jnp.zeros((1, LANES), F32)
    ranks = []
    for s in range(tb // sb):
        rows = both[s * sb:(s + 1) * sb]
        ranks.append(jnp.dot(strict, rows, preferred_element_type=F32) + carry)
        carry = carry + jnp.sum(rows.astype(F32), axis=0, keepdims=True)
    rank = jnp.concatenate(ranks, axis=0)
    seg = jnp.floor((carry + (MOE_TILE - 1)) * (1.0 / MOE_TILE)) * MOE_TILE
    ui = lax.broadcasted_iota(I32, (LANES, LANES), 0)
    uj = lax.broadcasted_iota(I32, (LANES, LANES), 1)
    before = jnp.where(ui < uj, 1.0, 0.0)
    off = jnp.dot(jnp.broadcast_to(seg, (SUBLANES, LANES)), before, preferred_element_type=F32,
                  precision=lax.Precision.HIGHEST)[0:1, :]
    dest = off + rank
    d1 = jnp.sum(jnp.where(oh1, dest, 0.0), axis=1, keepdims=True)
    d2 = jnp.sum(jnp.where(oh2, dest, 0.0), axis=1, keepdims=True)
    col = jnp.where(lane_i == 0, d1, jnp.where(lane_i == 1, d2, jnp.where(lane_i == 2, p1,
                                                                          jnp.where(lane_i == 3, p2, 0.0))))
    col_ref[...] = col
    row_ref[...] = col.T[0:SUBLANES, :]

    end_rows = jnp.broadcast_to(off + seg, (LANES, LANES)).T
    start = (lax.broadcasted_iota(I32, (LANES, LANES), 1) * MOE_TILE).astype(F32)
    is_expert = lax.broadcasted_iota(I32, (LANES, LANES), 0) < N_EXPERTS
    done = jnp.where(jnp.logical_and(is_expert, end_rows <= start), 1.0, 0.0)
    expert = jnp.minimum(jnp.sum(done, axis=0, keepdims=True), float(N_EXPERTS - 1))
    total = jnp.sum(jnp.where(lane_i[0:1] < N_EXPERTS, seg, 0.0), axis=1, keepdims=True)
    valid = jnp.where(start[0:1] < total, 1.0, 0.0)
    sub = lax.broadcasted_iota(I32, (SUBLANES, LANES), 0)
    meta_ref[...] = jnp.where(sub == 0, expert, jnp.where(sub == 1, valid, 0.0)).astype(I32)


def _route(x, g, w_router, tb):
    m = x.shape[0]
    nblk = m // tb
    return pl.pallas_call(
        _route_kernel,
        grid=(nblk,),
        in_specs=[pl.BlockSpec((tb, D_MODEL), lambda b: (b, 0)),
                  pl.BlockSpec((1, D_MODEL), lambda b: (0, 0)),
                  pl.BlockSpec((D_MODEL, LANES), lambda b: (0, 0))],
        out_specs=[pl.BlockSpec((tb, D_MODEL), lambda b: (b, 0)),
                   pl.BlockSpec((SUBLANES, tb), lambda b: (0, b)),
                   pl.BlockSpec((tb, LANES), lambda b: (b, 0)),
                   pl.BlockSpec((None, SUBLANES, LANES), lambda b: (b, 0, 0))],
        out_shape=[jax.ShapeDtypeStruct((m, D_MODEL), BF16),
                   jax.ShapeDtypeStruct((SUBLANES, m), F32),
                   jax.ShapeDtypeStruct((m, LANES), F32),
                   jax.ShapeDtypeStruct((nblk, SUBLANES, LANES), I32)],
        compiler_params=_cparams(("parallel",)),
        name="moe_route",
    )(x, g, w_router)


def _gffn_kernel(te_ref, tv_ref, hb_ref, row_ref, wg_ref, wu_ref, wd_ref, ys_ref, xg_ref, acc_ref, *, nt, n_f):
    del te_ref
    b, t, j = pl.program_id(0), pl.program_id(1), pl.program_id(2)
    valid = tv_ref[b * nt + t] > 0
    tb = hb_ref.shape[0]

    @pl.when(jnp.logical_and(valid, j == 0))
    def _():
        slot = (t * MOE_TILE + lax.broadcasted_iota(I32, (MOE_TILE, tb), 0)).astype(F32)
        hit = jnp.logical_or(row_ref[0:1, :] == slot, row_ref[1:2, :] == slot)
        onehot = jnp.where(hit, 1.0, 0.0).astype(BF16)
        xg_ref[...] = jnp.dot(onehot, hb_ref[...], preferred_element_type=F32).astype(BF16)
        acc_ref[...] = jnp.zeros(acc_ref.shape, F32)

    @pl.when(valid)
    def _():
        xg = xg_ref[...]
        gate = jnp.dot(xg, wg_ref[...], preferred_element_type=F32)
        up = jnp.dot(xg, wu_ref[...], preferred_element_type=F32)
        acc_ref[...] += jnp.dot((_silu(gate) * up).astype(BF16), wd_ref[...], preferred_element_type=F32)

    @pl.when(j == n_f - 1)
    def _():
        ys_ref[...] = jnp.where(valid, acc_ref[...], 0.0).astype(BF16)


def _gffn(tile_expert, tile_valid, hb, rowinfo, w_gate, w_up, w_down, tb):
    m = hb.shape[0]
    nblk = m // tb
    nt = _moe_tiles(tb)
    d_ff = w_gate.shape[2]
    tf = d_ff // 2
    n_f = d_ff // tf
    def fidx(b, t, j, te, tv):
        odd = t % 2 == 1
        return jnp.where(tv[b * nt + t] > 0, jnp.where(odd, n_f - 1 - j, j), jnp.where(odd, n_f - 1, 0))

    grid_spec = pltpu.PrefetchScalarGridSpec(
        num_scalar_prefetch=2,
        grid=(nblk, nt, n_f),
        in_specs=[pl.BlockSpec((tb, D_MODEL), lambda b, t, j, te, tv: (b, 0)),
                  pl.BlockSpec((SUBLANES, tb), lambda b, t, j, te, tv: (0, b)),
                  pl.BlockSpec((None, D_MODEL, tf), lambda b, t, j, te, tv: (te[b * nt + t], 0, fidx(b, t, j, te, tv))),
                  pl.BlockSpec((None, D_MODEL, tf), lambda b, t, j, te, tv: (te[b * nt + t], 0, fidx(b, t, j, te, tv))),
                  pl.BlockSpec((None, tf, D_MODEL), lambda b, t, j, te, tv: (te[b * nt + t], fidx(b, t, j, te, tv), 0))],
        out_specs=pl.BlockSpec((MOE_TILE, D_MODEL), lambda b, t, j, te, tv: (b * nt + t, 0)),
        scratch_shapes=[pltpu.VMEM((MOE_TILE, D_MODEL), BF16), pltpu.VMEM((MOE_TILE, D_MODEL), F32)],
    )
    return pl.pallas_call(
        functools.partial(_gffn_kernel, nt=nt, n_f=n_f),
        grid_spec=grid_spec,
        out_shape=jax.ShapeDtypeStruct((nblk * nt * MOE_TILE, D_MODEL), BF16),
        compiler_params=_cparams(("parallel", "arbitrary", "arbitrary")),
        name="moe_experts",
    )(tile_expert, tile_valid, hb, rowinfo, w_gate, w_up, w_down)


def _combine_kernel(x_ref, col_ref, ys_ref, o_ref):
    rows, slots = x_ref.shape[0], ys_ref.shape[0]
    slot = lax.broadcasted_iota(I32, (rows, slots), 1).astype(F32)
    weight = (jnp.where(col_ref[:, 0:1] == slot, col_ref[:, 2:3], 0.0)
              + jnp.where(col_ref[:, 1:2] == slot, col_ref[:, 3:4], 0.0))
    o_ref[...] = x_ref[...] + jnp.dot(weight.astype(BF16), ys_ref[...], preferred_element_type=F32)


def _combine(x, colinfo, ys, tb):
    m = x.shape[0]
    slots = _moe_tiles(tb) * MOE_TILE
    rows = MOE_TILE
    sub = tb // rows
    return pl.pallas_call(
        _combine_kernel,
        grid=(m // tb, sub),
        in_specs=[pl.BlockSpec((rows, D_MODEL), lambda b, r: (b * sub + r, 0)),
                  pl.BlockSpec((rows, LANES), lambda b, r: (b * sub + r, 0)),
                  pl.BlockSpec((slots, D_MODEL), lambda b, r: (b, 0))],
        out_specs=pl.BlockSpec((rows, D_MODEL), lambda b, r: (b * sub + r, 0)),
        out_shape=jax.ShapeDtypeStruct((m, D_MODEL), F32),
        compiler_params=_cparams(("parallel", "arbitrary")),
        name="moe_combine",
    )(x, colinfo, ys)


def _moe_grouped(x, g, w_router, w_gate, w_up, w_down):
    m = x.shape[0]
    tb = min(MOE_BLOCK, m)
    nt = _moe_tiles(tb)
    hb, rowinfo, colinfo, meta = _route(x, g, w_router, tb)
    tile_expert = meta[:, 0, :nt].reshape(-1)
    tile_valid = meta[:, 1, :nt].reshape(-1)
    ys = _gffn(tile_expert, tile_valid, hb, rowinfo, w_gate, w_up, w_down, tb)
    return _combine(x, colinfo, ys, tb)


def _decmix_kernel(cb_ref, cc_ref, cx_ref, xs_ref, bc_ref, dt_ref, sc0_ref, sc1_ref,
                   sx0_ref, sx1_ref, sx2_ref, sb0_ref, sb1_ref, sb2_ref,
                   cw_ref, cwx_ref, cwb_ref, cbx_ref, cbb_ref, dtb_ref, alog_ref,
                   pa_ref, u_ref, xso_ref, bco_ref, dto_ref, ea_ref):
    u = cc_ref[...] * cx_ref[...]
    u_ref[...] = u
    pa_ref[...] = cb_ref[...] * (cw_ref[0:1, :] * sc0_ref[...] + cw_ref[1:2, :] * sc1_ref[...] + cw_ref[2:3, :] * u)
    xso_ref[...] = _silu(cwx_ref[0:1, :] * sx0_ref[...] + cwx_ref[1:2, :] * sx1_ref[...]
                         + cwx_ref[2:3, :] * sx2_ref[...] + cwx_ref[3:4, :] * xs_ref[...] + cbx_ref[...])
    bco_ref[...] = _silu(cwb_ref[0:1, :] * sb0_ref[...] + cwb_ref[1:2, :] * sb1_ref[...]
                         + cwb_ref[2:3, :] * sb2_ref[...] + cwb_ref[3:4, :] * bc_ref[...] + cbb_ref[...])
    dt = _softplus(dt_ref[...] + dtb_ref[...])
    dto_ref[...] = dt
    ea_ref[...] = jnp.exp(dt * (-jnp.exp(alog_ref[...])))


def _decmix(proj, st_conv, st_ssm_conv, conv_w, cw, cbias, dtb, alog):
    nb = proj.shape[0]
    pcol = lambda w, off: pl.BlockSpec((nb, w), lambda i: (0, off // w))
    full = lambda w: pl.BlockSpec((nb, w), lambda i: (0, 0))
    const = lambda r, w, j: pl.BlockSpec((r, w), lambda i: (0, j))
    sx = [st_ssm_conv[:, t, :512] for t in range(3)]
    sb = [st_ssm_conv[:, t, 512:] for t in range(3)]
    return pl.pallas_call(
        _decmix_kernel,
        grid=(1,),
        in_specs=[pcol(512, CB_OFF), pcol(512, CC_OFF), pcol(512, CX_OFF), pcol(512, XS_OFF), pcol(512, BC_OFF),
                  pcol(LANES, DT_OFF)] + [full(512)] * 8
                 + [const(3, 512, 0), const(4, 512, 0), const(4, 512, 1), const(1, 512, 0), const(1, 512, 1),
                    const(1, LANES, 0), const(1, LANES, 0)],
        out_specs=[full(512), full(512), full(512), full(512), full(LANES), full(LANES)],
        out_shape=[jax.ShapeDtypeStruct((nb, 512), F32)] * 4 + [jax.ShapeDtypeStruct((nb, LANES), F32)] * 2,
        compiler_params=_cparams(("arbitrary",)),
        name="decode_mix",
    )(proj, proj, proj, proj, proj, proj, st_conv[:, 0], st_conv[:, 1], *sx, *sb,
      conv_w, cw, cw, cbias, cbias, dtb, alog)


def _decssd_kernel(h_ref, dt_ref, xs_ref, b_ref, c_ref, ea_ref, dsk_ref, y_ref, ho_ref):
    xs = xs_ref[...]
    h_new = h_ref[...] * ea_ref[...] + (xs * dt_ref[...]) * b_ref[...]
    ho_ref[...] = h_new
    y_ref[...] = jnp.sum(h_new * c_ref[...], axis=-1, keepdims=True) + dsk_ref[...] * xs


def _decssd(h0, dt, xs, bh, ch, ea, dsk):
    nb = h0.shape[0]
    blk = lambda a, b: pl.BlockSpec((None, SSM_HEADS, a, b), lambda i: (i, 0, 0, 0))
    return pl.pallas_call(
        _decssd_kernel,
        grid=(nb,),
        in_specs=[blk(SSM_HEAD_DIM, SSM_STATE), blk(1, 1), blk(SSM_HEAD_DIM, 1), blk(1, SSM_STATE),
                  blk(1, SSM_STATE), blk(1, 1), pl.BlockSpec((SSM_HEADS, 1, 1), lambda i: (0, 0, 0))],
        out_specs=[blk(SSM_HEAD_DIM, 1), blk(SSM_HEAD_DIM, SSM_STATE)],
        out_shape=[jax.ShapeDtypeStruct((nb, SSM_HEADS, SSM_HEAD_DIM, 1), F32),
                   jax.ShapeDtypeStruct((nb, SSM_HEADS, SSM_HEAD_DIM, SSM_STATE), F32)],
        compiler_params=_cparams(("parallel",)),
        name="decode_ssd",
    )(h0, dt, xs, bh, ch, ea, dsk)


DECODE_PAGE_GROUP = 32


def _decscore_kernel(pt_ref, qi_ref, w_ref, *refs):
    del pt_ref
    kid_refs, o_ref = refs[:-1], refs[-1]
    kid = jnp.concatenate([r[...] for r in kid_refs], axis=1).astype(BF16)
    s = jnp.dot(qi_ref[:, 0:IDX_DIM].astype(BF16), kid, preferred_element_type=F32)
    sc = jnp.maximum(s, 0.0) * w_ref[:, 0:1]
    o_ref[...] = jnp.sum(sc, axis=0, keepdims=True)


def _page_specs(block, layer, pg):
    zeros = (0,) * (len(block) - 2)
    return [pl.BlockSpec(block, lambda b, p, pt, t=t: (layer, pt[b, p * pg + t]) + zeros) for t in range(pg)]


def _decscore(page_table, qip, wrows, pool_kidx, layer):
    nb, n_pages = page_table.shape
    pg = math.gcd(DECODE_PAGE_GROUP, n_pages)
    grid_spec = pltpu.PrefetchScalarGridSpec(
        num_scalar_prefetch=1,
        grid=(nb, n_pages // pg),
        in_specs=[pl.BlockSpec((None, N_IDX_HEADS, LANES), lambda b, p, pt: (b, 0, 0)),
                  pl.BlockSpec((None, N_IDX_HEADS, PAGE_SIZE), lambda b, p, pt: (b, 0, 0))]
                 + _page_specs((None, None, IDX_DIM, PAGE_SIZE), layer, pg),
        out_specs=pl.BlockSpec((None, None, 1, pg * PAGE_SIZE), lambda b, p, pt: (b, p, 0, 0)),
    )
    out = pl.pallas_call(
        _decscore_kernel,
        grid_spec=grid_spec,
        out_shape=jax.ShapeDtypeStruct((nb, n_pages // pg, 1, pg * PAGE_SIZE), F32),
        compiler_params=_cparams(("parallel", "arbitrary")),
        name="decode_scores",
    )(page_table, qip, wrows, *([pool_kidx] * pg))
    return out.reshape(nb, n_pages * PAGE_SIZE)


def _decselect_kernel(sc_ref, qi_ref, ki_ref, w_ref, sel_ref, self_ref, keys_ref, *, k_sel, kc, nbits):
    nb, past = sc_ref.shape
    nck = past // kc
    lane_k = lax.broadcasted_iota(I32, (nb, kc), 1)
    ki = ki_ref[...].astype(F32)
    own = jnp.zeros((nb, 1), F32)
    for h in range(N_IDX_HEADS):
        s = jnp.sum(qi_ref[:, h * LANES:(h + 1) * LANES].astype(F32) * ki, axis=1, keepdims=True)
        own = own + jnp.maximum(s, 0.0) * w_ref[:, h:h + 1]
    own_key = _sortable(own)

    def key_body(c, carry):
        off = pl.multiple_of(c * kc, kc)
        keys_ref[:, pl.ds(off, kc)] = _sortable(sc_ref[:, pl.ds(off, kc)])
        return carry

    lax.fori_loop(0, nck, key_body, 0)

    def count_where(pred):
        def body(c, acc):
            off = pl.multiple_of(c * kc, kc)
            f = jnp.where(pred(keys_ref[:, pl.ds(off, kc)], off), 1.0, 0.0)
            part = f[:, 0:LANES]
            for t in range(1, kc // LANES):
                part = part + f[:, t * LANES:(t + 1) * LANES]
            return acc + part

        acc = lax.fori_loop(0, nck, body, jnp.zeros((nb, LANES), F32))
        return jnp.sum(acc, axis=1, keepdims=True)

    def bit_body(i, t):
        cand = t + lax.shift_left(jnp.int32(1), 31 - i)
        cnt = count_where(lambda kk, off: kk >= cand) + jnp.where(own_key >= cand, 1.0, 0.0)
        return jnp.where(cnt >= k_sel, cand, t)

    thr = lax.fori_loop(0, 32, bit_body, jnp.full((nb, 1), INT_MIN, I32))
    thr = jnp.maximum(thr, KEY_NEG_INF)
    finite_thr = thr > KEY_NEG_INF
    cnt_gt = count_where(lambda kk, off: kk > thr) + jnp.where(own_key > thr, 1.0, 0.0)
    need = k_sel - cnt_gt

    def xbody(i, x):
        cand = x + lax.shift_left(jnp.int32(1), nbits - 1 - i)
        cnt = count_where(lambda kk, off: jnp.logical_and(kk == thr, off + lane_k < cand))
        return jnp.where(cnt < need, cand, x)

    xcut = lax.fori_loop(0, nbits, xbody, jnp.zeros((nb, 1), I32))
    xcut = jnp.where(finite_thr, xcut, -1)
    ties_past = count_where(lambda kk, off: kk == thr)
    own_tie = jnp.logical_and(jnp.logical_and(own_key == thr, finite_thr), ties_past < need)
    self_ref[...] = jnp.broadcast_to(
        jnp.where(jnp.logical_or(own_key > thr, own_tie), 1.0, 0.0), (nb, LANES))

    def sel_body(c, carry):
        off = pl.multiple_of(c * kc, kc)
        kk = keys_ref[:, pl.ds(off, kc)]
        sel = jnp.logical_or(kk > thr, jnp.logical_and(kk == thr, off + lane_k <= xcut))
        sel_ref[:, pl.ds(off, kc)] = jnp.where(sel, 1.0, 0.0)
        return carry

    lax.fori_loop(0, nck, sel_body, 0)


def _decselect(scores, qip, kib, wi):
    nb, past = scores.shape
    k_sel = min(TOPK_MAX, (past + 1) // 4)
    kc = next(c for c in (4096, 2048, 1024, 512, PAGE_SIZE) if past % c == 0)
    nbits = max(1, past.bit_length())
    full = lambda w: pl.BlockSpec((nb, w), lambda i: (0, 0))
    return pl.pallas_call(
        functools.partial(_decselect_kernel, k_sel=k_sel, kc=kc, nbits=nbits),
        grid=(1,),
        in_specs=[full(past), full(N_IDX_HEADS * LANES), full(LANES), full(LANES)],
        out_specs=[full(past), full(LANES)],
        out_shape=[jax.ShapeDtypeStruct((nb, past), F32), jax.ShapeDtypeStruct((nb, LANES), F32)],
        scratch_shapes=[pltpu.VMEM((nb, past), I32)],
        compiler_params=_cparams(("arbitrary",)),
        name="decode_select",
    )(scores, qip, kib, wi)


def _decattn_kernel(pt_ref, q_ref, sel_ref, kn_ref, vn_ref, self_ref, *refs, n_steps):
    del pt_ref
    pg = (len(refs) - 4) // 2
    kp_refs, vp_refs = refs[:pg], refs[pg:2 * pg]
    o_ref, m_ref, l_ref, acc_ref = refs[2 * pg:]
    p = pl.program_id(1)
    hpg = N_HEADS // N_KV_HEADS

    @pl.when(p == 0)
    def _():
        m_ref[...] = jnp.full(m_ref.shape, NEG_BIG, F32)
        l_ref[...] = jnp.zeros(l_ref.shape, F32)
        acc_ref[...] = jnp.zeros(acc_ref.shape, F32)

    q = q_ref[...]
    qb = q.astype(BF16)
    slab = lambda refs_, g: jnp.concatenate([r[g] for r in refs_], axis=1).astype(BF16)
    s = jnp.concatenate([jnp.dot(qb[g * hpg:(g + 1) * hpg], slab(kp_refs, g), preferred_element_type=F32)
                         for g in range(N_KV_HEADS)], axis=0)
    s = jnp.where(sel_ref[...] > 0.0, s, -jnp.inf)
    m_old = m_ref[:, 0:1]
    m_new = jnp.maximum(m_old, jnp.max(s, axis=1, keepdims=True))
    alpha = jnp.exp(m_old - m_new)
    pr = jnp.exp(s - m_new)
    l_new = alpha * l_ref[:, 0:1] + jnp.sum(pr, axis=1, keepdims=True)
    prb = pr.astype(BF16)
    pv = jnp.concatenate([lax.dot_general(prb[g * hpg:(g + 1) * hpg], slab(vp_refs, g), NT_DIMS,
                                          preferred_element_type=F32) for g in range(N_KV_HEADS)], axis=0)
    acc_ref[...] = alpha * acc_ref[...] + pv
    m_ref[...] = jnp.broadcast_to(m_new, m_ref.shape)
    l_ref[...] = jnp.broadcast_to(l_new, l_ref.shape)

    @pl.when(p == n_steps - 1)
    def _():
        s_own = jnp.sum(q * kn_ref[...], axis=1, keepdims=True)
        s_own = jnp.where(self_ref[:, 0:1] > 0.0, s_own, -jnp.inf)
        m_o = m_ref[:, 0:1]
        m_n = jnp.maximum(m_o, s_own)
        al = jnp.exp(m_o - m_n)
        p_own = jnp.exp(s_own - m_n)
        l_n = al * l_ref[:, 0:1] + p_own
        acc = al * acc_ref[...] + p_own.astype(BF16).astype(F32) * vn_ref[...]
        o_ref[...] = acc / l_n


def _decattn(page_table, q, sel, pool_kt, pool_vt, kn, vn, self_sel, layer):
    nb, n_pages = page_table.shape
    pg = math.gcd(DECODE_PAGE_GROUP, n_pages)
    n_steps = n_pages // pg
    heads = pl.BlockSpec((None, N_HEADS, HEAD_DIM), lambda b, p, pt: (b, 0, 0))
    pages = _page_specs((None, None, N_KV_HEADS, HEAD_DIM, PAGE_SIZE), layer, pg)
    grid_spec = pltpu.PrefetchScalarGridSpec(
        num_scalar_prefetch=1,
        grid=(nb, n_steps),
        in_specs=[heads,
                  pl.BlockSpec((None, None, 1, pg * PAGE_SIZE), lambda b, p, pt: (b, p, 0, 0)),
                  heads, heads, pl.BlockSpec((None, 1, LANES), lambda b, p, pt: (b, 0, 0))] + pages + pages,
        out_specs=heads,
        scratch_shapes=[pltpu.VMEM((N_HEADS, LANES), F32), pltpu.VMEM((N_HEADS, LANES), F32),
                        pltpu.VMEM((N_HEADS, HEAD_DIM), F32)],
    )
    return pl.pallas_call(
        functools.partial(_decattn_kernel, n_steps=n_steps),
        grid_spec=grid_spec,
        out_shape=jax.ShapeDtypeStruct((nb, N_HEADS, HEAD_DIM), F32),
        compiler_params=_cparams(("parallel", "arbitrary")),
        name="decode_attend",
    )(page_table, q, sel.reshape(nb, n_steps, 1, pg * PAGE_SIZE), kn, vn, self_sel,
      *([pool_kt] * pg), *([pool_vt] * pg))


def _pack_w_in(w):
    d = w.shape[0]
    pad = lambda n: jnp.zeros((d, n), w.dtype)
    cols = [w[:, 4172:7244],
            w[:, 0:2048],
            w[:, 2628:3140],
            w[:, 3140:4164],
            w[:, 2048:2304],
            w[:, 2304:2560],
            w[:, 2560:2628], pad(60),
            w[:, 4164:4172], pad(120),
            pad(N_PROJ - 7424)]
    return jnp.concatenate(cols, axis=1).astype(BF16)


def _pad_lanes(v):
    return jnp.zeros((1, LANES), F32).at[0, :v.shape[0]].set(v)


def _rope_tables(pos):
    half = HEAD_DIM // 2
    inv = ROPE_THETA ** (-jnp.arange(half, dtype=F32) / half)
    ang = pos.astype(F32)[:, None] * inv[None, :]
    cos, sin = jnp.cos(ang), jnp.sin(ang)
    cos2 = jnp.concatenate([cos, cos], axis=1)
    sin2 = jnp.concatenate([-sin, sin], axis=1)
    return jnp.tile(cos2, (1, 2)), jnp.tile(sin2, (1, 2))


def _head_rows(qp):
    hpg = N_HEADS // N_KV_HEADS
    q3 = qp.reshape(qp.shape[0], N_HEADS, LANES)
    return jnp.stack([q3[:, h, (h // hpg) * HEAD_DIM:(h // hpg + 1) * HEAD_DIM] for h in range(N_HEADS)], axis=1)


def kernel(x_prompt, x_sample, cache_k, cache_v, cache_kidx, state_conv, state_ssm_conv, state_ssm, page_table,
           norm1, w_in, b_gate, conv_w, q_norm, k_norm, w_pc, w_pa, ssm_conv_w, ssm_conv_b, dt_bias, a_log,
           d_skip, ssm_norm, w_ps, w_o, norm2, w_gate_dense, w_up_dense, w_down_dense, w_router, w_gate_moe,
           w_up_moe, w_down_moe):
    nb, seq, _ = x_prompt.shape
    db = x_sample.shape[0]
    depth = w_in.shape[0]
    n_pages = page_table.shape[1]
    past = n_pages * PAGE_SIZE

    hp = x_prompt.reshape(nb * seq, D_MODEL)
    hs = x_sample.reshape(db, D_MODEL)
    cos_p, sin_p = _rope_tables(jnp.tile(jnp.arange(seq), nb))
    cos_s, sin_s = _rope_tables(jnp.full((db,), past))

    pool_kt = jnp.transpose(cache_k, (0, 1, 3, 4, 2))
    pool_vt = jnp.transpose(cache_v, (0, 1, 3, 4, 2))
    pool_kidxt = jnp.transpose(cache_kidx, (0, 1, 3, 2))

    outs_p = [[] for _ in range(6)]
    outs_s = [[] for _ in range(6)]
    for l in range(depth):
        wp = _pack_w_in(w_in[l])
        g1 = norm1[l].reshape(1, D_MODEL)
        g2 = norm2[l].reshape(1, D_MODEL)
        qg = jnp.tile(q_norm[l], 2).reshape(1, LANES)
        kg = jnp.tile(k_norm[l], 2).reshape(1, LANES)
        cw = ssm_conv_w[l]
        cbias = ssm_conv_b[l].reshape(1, SSM_CONV_DIM)
        dtb = _pad_lanes(dt_bias[l])
        alog = _pad_lanes(a_log[l])
        dsk = jnp.repeat(d_skip[l], SSM_HEAD_DIM).reshape(1, SSM_INNER)
        sn = ssm_norm[l].reshape(1, SSM_INNER)
        wpc, wpa, wps, wo = (w.astype(BF16) for w in (w_pc[l], w_pa[l], w_ps[l], w_o[l]))
        i = l // 2
        if l % 2 == 0:
            routed = False
            wr = jnp.zeros((D_MODEL, LANES), F32)
            wg, wu, wd = (w[i:i + 1].astype(BF16) for w in (w_gate_dense, w_up_dense, w_down_dense))
        else:
            routed = True
            wr = jnp.zeros((D_MODEL, LANES), F32).at[:, :N_EXPERTS].set(w_router[i])
            wg, wu, wd = (w[i].astype(BF16) for w in (w_gate_moe, w_up_moe, w_down_moe))

        proj = _inproj(hp, g1, wp)
        qt, kf, kb, vt, qit, kif, kib, wt = _prep(proj, cos_p, sin_p, qg, kg, True)
        attn = _dsa_prompt(qt, qit, wt, kb, vt, kib, nb, seq)
        pre_a, conv_st = _conva_prompt(proj, conv_w[l], nb, seq)
        y_ssd, h_last = _ssd_prompt(proj, cw, cbias, dtb, alog, dsk, nb, seq)
        hp = _merge(hp, pre_a, attn, y_ssd, proj, sn, b_gate[l], wpc, wpa, wps, wo)
        if routed and hp.shape[0] % MOE_TILE == 0:
            hp = _moe_grouped(hp, g2, wr, wg, wu, wd)
        else:
            hp = _ffn(hp, g2, wr, wg, wu, wd, routed)
        proj3 = proj.reshape(nb, seq, N_PROJ)
        outs_p[0].append(kf.reshape(nb, seq, N_KV_HEADS, HEAD_DIM))
        outs_p[1].append(proj3[:, :, V_OFF:V_OFF + LANES].reshape(nb, seq, N_KV_HEADS, HEAD_DIM))
        outs_p[2].append(kif[:, :IDX_DIM].reshape(nb, seq, IDX_DIM))
        outs_p[3].append(conv_st)
        outs_p[4].append(proj3[:, seq - 3:, XS_OFF:XS_OFF + SSM_CONV_DIM])
        outs_p[5].append(h_last.reshape(nb, SSM_HEADS, SSM_HEAD_DIM, SSM_STATE))

        proj = _inproj(hs, g1, wp)
        qp, kf, kb, vb, qip, kif, kib, wi = _prep(proj, cos_s, sin_s, qg, kg, False)
        wrows = jnp.broadcast_to(wi[:, :N_IDX_HEADS, None], (db, N_IDX_HEADS, PAGE_SIZE))
        scores = _decscore(page_table, qip.astype(F32).reshape(db, N_IDX_HEADS, LANES), wrows, pool_kidxt, l)
        sel, self_sel = _decselect(scores, qip, kib, wi)
        per_head = lambda a: jnp.repeat(a.astype(F32).reshape(db, N_KV_HEADS, HEAD_DIM), N_HEADS // N_KV_HEADS, axis=1)
        o = _decattn(page_table, _head_rows(qp.astype(F32)), sel, pool_kt, pool_vt, per_head(kb), per_head(vb),
                     self_sel.reshape(db, 1, LANES), l)
        attn = o.reshape(db, N_HEADS * HEAD_DIM)
        pre_a, u, xs, bcv, dt, ea = _decmix(proj, state_conv[l], state_ssm_conv[l], conv_w[l], cw, cbias, dtb, alog)
        hpg = SSM_HEADS // SSM_GROUPS
        xs4 = xs.reshape(db, SSM_HEADS, SSM_HEAD_DIM, 1)
        bh = jnp.repeat(bcv[:, :SSM_GROUPS * SSM_STATE].reshape(db, SSM_GROUPS, 1, SSM_STATE), hpg, axis=1)
        ch = jnp.repeat(bcv[:, SSM_GROUPS * SSM_STATE:].reshape(db, SSM_GROUPS, 1, SSM_STATE), hpg, axis=1)
        y4, h_new = _decssd(state_ssm[l], dt[:, :SSM_HEADS, None, None], xs4, bh, ch, ea[:, :SSM_HEADS, None, None],
                            d_skip[l].reshape(SSM_HEADS, 1, 1))
        hs = _merge(hs, pre_a, attn, y4.reshape(db, SSM_INNER), proj, sn, b_gate[l], wpc, wpa, wps, wo)
        hs = _ffn(hs, g2, wr, wg, wu, wd, routed)
        outs_s[0].append(kf.reshape(db, 1, N_KV_HEADS, HEAD_DIM))
        outs_s[1].append(proj[:, V_OFF:V_OFF + LANES].reshape(db, 1, N_KV_HEADS, HEAD_DIM))
        outs_s[2].append(kif[:, :IDX_DIM].reshape(db, 1, IDX_DIM))
        outs_s[3].append(jnp.stack([state_conv[l][:, 1], u], axis=1))
        outs_s[4].append(jnp.concatenate(
            [state_ssm_conv[l][:, 1:], proj[:, None, XS_OFF:XS_OFF + SSM_CONV_DIM]], axis=1))
        outs_s[5].append(h_new)

    return (hp.reshape(nb, seq, D_MODEL), hs.reshape(db, 1, D_MODEL),
            *(jnp.stack(o) for o in outs_p), *(jnp.stack(o) for o in outs_s))
```

```python
import functools
import math

import jax
import jax.numpy as jnp
from jax import lax
from jax.experimental import pallas as pl
from jax.experimental.pallas import tpu as pltpu

F32 = jnp.float32
BF16 = jnp.bfloat16
I32 = jnp.int32

D_MODEL = 1024
D_CONV = 512
N_HEADS = 8
HEAD_DIM = 64
N_KV_HEADS = 2
N_IDX_HEADS = 4
IDX_DIM = 64
TOPK_MAX = 256
Q_BLOCK = 128
ROPE_THETA = 10000.0
SSM_INNER = 512
SSM_HEADS = 8
SSM_HEAD_DIM = 64
SSM_GROUPS = 2
SSM_STATE = 128
SSM_CHUNK = 128
SSM_CONV_DIM = 1024
PAGE_SIZE = 128
N_EXPERTS = 8
EPS = 1e-6

LANES = 128
SUBLANES = 8
VMEM_LIMIT = 52 * 1024 * 1024

G_OFF, CB_OFF, CC_OFF, CX_OFF, Q_OFF, Z_OFF, XS_OFF, BC_OFF = 0, 3072, 3584, 4096, 4608, 5120, 5632, 6144
K_OFF, V_OFF, QI_OFF, KW_OFF, DT_OFF = 6656, 6784, 6912, 7168, 7296
N_PROJ = 7680

INT_MIN = -2 ** 31
INT_MAX = 2 ** 31 - 1
KEY_NEG_INF = (-8388608) ^ 0x7FFFFFFF
NEG_BIG = -1e30
LOG2_E = 1.4426950408889634

NT_DIMS = (((1,), (1,)), ((), ()))


def _cparams(sem):
    return pltpu.CompilerParams(dimension_semantics=sem, vmem_limit_bytes=VMEM_LIMIT)


def _sigmoid(x):
    return 1.0 / (1.0 + jnp.exp(-x))


def _silu(x):
    return x * _sigmoid(x)


def _softplus(x):
    return jnp.maximum(x, 0.0) + jnp.log1p(jnp.exp(-jnp.abs(x)))


def _sortable(x):
    bits = pltpu.bitcast(x, I32)
    return bits ^ (lax.shift_right_arithmetic(bits, 31) & 0x7FFFFFFF)


def _rms(x, g):
    return x * lax.rsqrt(jnp.mean(x * x, axis=-1, keepdims=True) + EPS) * g


def _inproj_kernel(x_ref, g_ref, w_ref, o_ref, h_ref):
    @pl.when(pl.program_id(1) == 0)
    def _():
        h_ref[...] = _rms(x_ref[...], g_ref[...]).astype(BF16)

    o_ref[...] = jnp.dot(h_ref[...], w_ref[...], preferred_element_type=F32)


def _inproj(x, g, w):
    m = x.shape[0]
    tm = min(m, 1024)
    tn = 1536
    return pl.pallas_call(
        _inproj_kernel,
        grid=(m // tm, N_PROJ // tn),
        in_specs=[pl.BlockSpec((tm, D_MODEL), lambda i, j: (i, 0)),
                  pl.BlockSpec((1, D_MODEL), lambda i, j: (0, 0)),
                  pl.BlockSpec((D_MODEL, tn), lambda i, j: (0, j))],
        out_specs=pl.BlockSpec((tm, tn), lambda i, j: (i, j)),
        out_shape=jax.ShapeDtypeStruct((m, N_PROJ), F32),
        scratch_shapes=[pltpu.VMEM((tm, D_MODEL), BF16)],
        compiler_params=_cparams(("parallel", "arbitrary")),
        name="inproj",
    )(x, g, w)


V_T_ROWS = LANES + 16


def _prep_kernel(q_ref, k_ref, v_ref, qi_ref, kw_ref, cos_ref, sin_ref, qg_ref, kg_ref,
                 qp_ref, kf_ref, kb_ref, vb_ref, qip_ref, kif_ref, kib_ref, wo_ref, *, transposed):
    tm = cos_ref.shape[0]

    def put(ref, h, val):
        if transposed:
            vt = val.T.astype(BF16)
            for blk in range(tm // Q_BLOCK):
                ref[blk, h * LANES:(h + 1) * LANES, :] = vt[:, blk * Q_BLOCK:(blk + 1) * Q_BLOCK]
        else:
            ref[:, h * LANES:(h + 1) * LANES] = val.astype(BF16)

    cos = cos_ref[...]
    sin = sin_ref[...]
    lane = lax.broadcasted_iota(I32, (tm, LANES), 1)
    low_half = lane < HEAD_DIM
    first_rot = (lane % HEAD_DIM) < (HEAD_DIM // 2)
    r = lax.broadcasted_iota(I32, (LANES, LANES), 0) // HEAD_DIM
    c = lax.broadcasted_iota(I32, (LANES, LANES), 1) // HEAD_DIM
    seg = jnp.where(r == c, 1.0 / HEAD_DIM, 0.0).astype(BF16)

    def rope(x):
        fwd = pltpu.roll(x, LANES - HEAD_DIM // 2, 1)
        bwd = pltpu.roll(x, HEAD_DIM // 2, 1)
        return x * cos + jnp.where(first_rot, fwd, bwd) * sin

    def headnorm(x, g):
        s = x * x
        hi = s.astype(BF16)
        lo = (s - hi.astype(F32)).astype(BF16)
        ms = (jnp.dot(hi, seg, preferred_element_type=F32)
              + jnp.dot(lo, seg, preferred_element_type=F32))
        return x * lax.rsqrt(ms + EPS) * g

    q_scale = HEAD_DIM ** -0.5 * (LOG2_E if transposed else 1.0)
    qg = qg_ref[...]
    for s in range(N_HEADS // 2):
        slab = rope(headnorm(q_ref[:, s * LANES:(s + 1) * LANES], qg)) * q_scale
        swapped = pltpu.roll(slab, HEAD_DIM, 1)
        grp = (2 * s) // (N_HEADS // N_KV_HEADS)
        for hh in range(2):
            h = 2 * s + hh
            src = slab if hh == grp else swapped
            keep = low_half if grp == 0 else jnp.logical_not(low_half)
            put(qp_ref, h, jnp.where(keep, src, 0.0))

    k = rope(headnorm(k_ref[...], kg_ref[...]))
    kf_ref[...] = k
    kb_ref[...] = k.astype(BF16)
    if transposed:
        vb_ref[0:LANES, :] = v_ref[...].T.astype(BF16)
        vb_ref[LANES:V_T_ROWS, :] = jnp.ones((V_T_ROWS - LANES, tm), BF16)
    else:
        vb_ref[...] = v_ref[...].astype(BF16)

    for s in range(N_IDX_HEADS // 2):
        slab = rope(qi_ref[:, s * LANES:(s + 1) * LANES]) * (IDX_DIM ** -0.5)
        swapped = pltpu.roll(slab, HEAD_DIM, 1)
        for hh in range(2):
            h = 2 * s + hh
            src = slab if hh == 0 else swapped
            put(qip_ref, h, jnp.where(low_half, src, 0.0))

    kw = kw_ref[...]
    ki = rope(kw)
    kif_ref[...] = ki
    kib_ref[...] = jnp.where(low_half, ki, 0.0).astype(BF16)
    wi = pltpu.roll(kw, HEAD_DIM, 1) * (N_IDX_HEADS ** -0.5)
    if transposed:
        wt = wi.T[0:SUBLANES, :]
        for blk in range(tm // Q_BLOCK):
            wo_ref[blk] = wt[:, blk * Q_BLOCK:(blk + 1) * Q_BLOCK]
    else:
        wo_ref[...] = wi


def _prep(proj, cos, sin, qg, kg, transposed):
    m = proj.shape[0]
    tm = min(m, 512)
    row = lambda w, off: pl.BlockSpec((tm, w), lambda i: (i, off // w))
    full = lambda w: pl.BlockSpec((tm, w), lambda i: (i, 0))
    const = pl.BlockSpec((1, LANES), lambda i: (0, 0))
    if transposed:
        qblocks = tm // Q_BLOCK
        feat = lambda r: pl.BlockSpec((qblocks, r, Q_BLOCK), lambda i: (i, 0, 0))
        fshape = lambda r, dt: jax.ShapeDtypeStruct((m // Q_BLOCK, r, Q_BLOCK), dt)
        vspec = pl.BlockSpec((V_T_ROWS, tm), lambda i: (0, i))
        vshape = jax.ShapeDtypeStruct((V_T_ROWS, m), BF16)
    else:
        feat = full
        fshape = lambda r, dt: jax.ShapeDtypeStruct((m, r), dt)
        vspec = full(LANES)
        vshape = jax.ShapeDtypeStruct((m, LANES), BF16)
    return pl.pallas_call(
        functools.partial(_prep_kernel, transposed=transposed),
        grid=(m // tm,),
        in_specs=[row(512, Q_OFF), row(LANES, K_OFF), row(LANES, V_OFF), row(256, QI_OFF), row(LANES, KW_OFF),
                  full(LANES), full(LANES), const, const],
        out_specs=[feat(N_HEADS * LANES), full(LANES), full(LANES), vspec,
                   feat(N_IDX_HEADS * LANES), full(LANES), full(LANES), feat(SUBLANES if transposed else LANES)],
        out_shape=[fshape(N_HEADS * LANES, BF16),
                   jax.ShapeDtypeStruct((m, LANES), F32),
                   jax.ShapeDtypeStruct((m, LANES), BF16),
                   vshape,
                   fshape(N_IDX_HEADS * LANES, BF16),
                   jax.ShapeDtypeStruct((m, LANES), F32),
                   jax.ShapeDtypeStruct((m, LANES), BF16),
                   fshape(SUBLANES if transposed else LANES, F32)],
        compiler_params=_cparams(("parallel",)),
        name="prep",
    )(proj, proj, proj, proj, proj, cos, sin, qg, kg)


def _fold_rows(f):
    while f.shape[0] > SUBLANES:
        half = f.shape[0] // 2
        f = f[:half] + f[half:]
    return f


def _dsa_kernel(qt_ref, qit_ref, wt_ref, k_ref, vt_ref, ki_ref, o_ref,
                keys_ref, tpos_ref, x_ref, m_ref, kmax_ref, acc_ref, *, k_sel, kc, nbits):
    qb = Q_BLOCK
    n_keys = k_ref.shape[0]
    assert n_keys < 2 ** 22
    j = pl.program_id(1)
    nck = (j * qb + qb + kc - 1) // kc
    qpos = j * qb + lax.broadcasted_iota(I32, (1, qb), 1)
    sub_k = lax.broadcasted_iota(I32, (kc, qb), 0)
    pair = lambda ref, s: jnp.concatenate(
        [ref[(2 * s) * LANES:(2 * s + 1) * LANES, :], ref[(2 * s + 1) * LANES:(2 * s + 2) * LANES, :]], axis=1)
    qi2 = [pair(qit_ref, s) for s in range(N_IDX_HEADS // 2)]
    q2 = [pair(qt_ref, s) for s in range(N_HEADS // 2)]

    def score_body(c, carry):
        off = pl.multiple_of(c * kc, kc)
        kic = ki_ref[pl.ds(off, kc), :]
        acc = jnp.zeros((kc, qb), F32)
        for s in range(N_IDX_HEADS // 2):
            s2 = jnp.dot(kic, qi2[s], preferred_element_type=F32)
            for hh in range(2):
                h = 2 * s + hh
                acc = acc + jnp.maximum(s2[:, hh * qb:(hh + 1) * qb], 0.0) * wt_ref[h:h + 1, :]
        kpos = off + sub_k
        key = _sortable(jnp.where(kpos <= qpos, acc, -jnp.inf))
        key = jnp.where(key == 0, (n_keys - 1) - kpos, jnp.where(key == -1, -1 - kpos, key))
        keys_ref[pl.ds(off, kc), :] = key
        return carry

    lax.fori_loop(0, nck, score_body, 0)

    def count_where(pred, ref=keys_ref):
        def body(c, acc):
            off = pl.multiple_of(c * kc, kc)
            return acc + _fold_rows(jnp.where(pred(ref[pl.ds(off, kc), :], off), 1.0, 0.0))

        acc = lax.fori_loop(0, nck, body, jnp.zeros((SUBLANES, qb), F32))
        return jnp.sum(acc, axis=0, keepdims=True)

    def bit_body(i, carry):
        t, cnt_acc, cnt_rej = carry
        cand = t + lax.shift_left(jnp.int32(1), 31 - i)
        cnt = count_where(lambda kk, off: kk >= cand)
        ok = cnt >= k_sel
        return jnp.where(ok, cand, t), jnp.where(ok, cnt, cnt_acc), jnp.where(ok, cnt_rej, cnt)

    zero = jnp.zeros((1, qb), F32)
    thr, cnt_ge, cnt_gt = lax.fori_loop(0, 32, bit_body, (jnp.full((1, qb), INT_MIN, I32), zero, zero))
    thr = jnp.maximum(thr, KEY_NEG_INF)
    finite_thr = thr > KEY_NEG_INF
    need = k_sel - cnt_gt
    tie = jnp.logical_and(cnt_ge > k_sel, finite_thr)
    x_default = jnp.where(finite_thr, INT_MAX, -1)
    x_ref[...] = jnp.broadcast_to(x_default, x_ref.shape)

    @pl.when(jnp.max(jnp.where(tie, 1.0, 0.0)) > 0.0)
    def _():
        def tie_pos_body(c, carry):
            off = pl.multiple_of(c * kc, kc)
            tpos_ref[pl.ds(off, kc), :] = jnp.where(keys_ref[pl.ds(off, kc), :] == thr, off + sub_k, INT_MAX)
            return carry

        lax.fori_loop(0, nck, tie_pos_body, 0)

        def xbody(i, x):
            cand = x + lax.shift_left(jnp.int32(1), nbits - 1 - i)
            cnt = count_where(lambda tp, off: tp < cand, tpos_ref)
            return jnp.where(cnt < need, cand, x)

        x = lax.fori_loop(0, nbits, xbody, jnp.zeros((1, qb), I32))
        x_ref[...] = jnp.broadcast_to(jnp.where(tie, x, x_default), x_ref.shape)

    xcut = x_ref[0:1, :]
    hpg = N_HEADS // N_KV_HEADS

    @pl.when(j == 0)
    def _():
        r = lax.broadcasted_iota(I32, (LANES, LANES), 0) // HEAD_DIM
        cc = lax.broadcasted_iota(I32, (LANES, LANES), 1) // HEAD_DIM
        seg = jnp.where(r == cc, 1.0, 0.0).astype(BF16)

        def body(c, mx):
            kf = k_ref[pl.ds(pl.multiple_of(c * kc, kc), kc), :].astype(F32)
            gs = jnp.dot((kf * kf).astype(BF16), seg, preferred_element_type=F32)
            while gs.shape[0] > SUBLANES:
                half = gs.shape[0] // 2
                gs = jnp.maximum(gs[:half], gs[half:])
            return jnp.maximum(mx, gs)

        mx = lax.fori_loop(0, k_ref.shape[0] // kc, body, jnp.zeros((SUBLANES, LANES), F32))
        mx = jnp.max(mx, axis=0, keepdims=True)
        lane = lax.broadcasted_iota(I32, (1, LANES), 1)
        other = pltpu.roll(mx, HEAD_DIM, 1)
        kmax_ref[0:1, :] = jnp.where(lane < HEAD_DIM, mx, other)
        kmax_ref[1:2, :] = jnp.where(lane < HEAD_DIM, other, mx)

    def chunk(c):
        off = pl.multiple_of(c * kc, kc)
        kk = keys_ref[pl.ds(off, kc), :]
        sel = jnp.logical_or(kk > thr, jnp.logical_and(kk == thr, off + sub_k <= xcut))
        return sel, k_ref[pl.ds(off, kc), :], vt_ref[:, pl.ds(off, kc)]

    bound = []
    for h in range(N_HEADS):
        qh = qt_ref[h * LANES:(h + 1) * LANES, :].astype(F32)
        qsq = jnp.sum(qh * qh, axis=0, keepdims=True)
        bound.append(jnp.sqrt(qsq * kmax_ref[h // hpg:h // hpg + 1, :]))
    acc_ref[...] = jnp.zeros(acc_ref.shape, F32)

    def fast_body(c, carry):
        sel, kch, vch = chunk(c)
        n_pair = N_HEADS // 2
        s2 = jnp.dot(kch, q2[0], preferred_element_type=F32)
        for s in range(n_pair):
            s2_next = jnp.dot(kch, q2[s + 1], preferred_element_type=F32) if s + 1 < n_pair else None
            ps = [jnp.exp2(jnp.where(sel, s2[:, hh * qb:(hh + 1) * qb] - bound[2 * s + hh], -jnp.inf)).astype(BF16)
                  for hh in range(2)]
            acc_ref[s] += jnp.dot(vch, jnp.concatenate(ps, axis=1), preferred_element_type=F32)
            s2 = s2_next
        return carry

    lax.fori_loop(0, nck, fast_body, 0)

    dens = jnp.concatenate([acc_ref[s, LANES:LANES + 1, :] for s in range(N_HEADS // 2)], axis=0)

    @pl.when(jnp.logical_not(jnp.min(dens) >= 1e-20))
    def _():
        m_ref[...] = jnp.full(m_ref.shape, NEG_BIG, F32)
        acc_ref[...] = jnp.zeros(acc_ref.shape, F32)

        def exact_body(c, carry):
            sel, kch, vch = chunk(c)
            for s in range(N_HEADS // 2):
                s2 = jnp.dot(kch, q2[s], preferred_element_type=F32)
                ps, alphas = [], []
                for hh in range(2):
                    h = 2 * s + hh
                    sh = jnp.where(sel, s2[:, hh * qb:(hh + 1) * qb], -jnp.inf)
                    m_old = m_ref[h:h + 1, :]
                    m_new = jnp.maximum(m_old, jnp.max(sh, axis=0, keepdims=True))
                    m_ref[h:h + 1, :] = m_new
                    alphas.append(jnp.exp2(m_old - m_new))
                    ps.append(jnp.exp2(sh - m_new).astype(BF16))
                o2 = jnp.dot(vch, jnp.concatenate(ps, axis=1), preferred_element_type=F32)
                acc_ref[s] = jnp.concatenate(alphas, axis=1) * acc_ref[s] + o2
            return carry

        lax.fori_loop(0, nck, exact_body, 0)

    for s in range(N_HEADS // 2):
        g = (2 * s) // hpg
        num = acc_ref[s, g * HEAD_DIM:(g + 1) * HEAD_DIM, :]
        out2 = num / acc_ref[s, LANES:LANES + 1, :]
        o_ref[:, s * LANES:(s + 1) * LANES] = jnp.concatenate([out2[:, :qb], out2[:, qb:]], axis=0).T


def _dsa_prompt(qt, qit, wt, kb, vt, kib, nb, seq):
    k_sel = min(TOPK_MAX, seq // 4)
    kc = next(c for c in (512, Q_BLOCK) if seq % c == 0)
    nbits = max(1, (seq - 1).bit_length())
    nq = seq // Q_BLOCK
    r3 = lambda a: a.reshape(nb, seq, a.shape[-1])
    qcol = lambda r: pl.BlockSpec((None, r, Q_BLOCK), lambda b, j: (b * nq + j, 0, 0))
    seqblk = pl.BlockSpec((None, seq, LANES), lambda b, j: (b, 0, 0))
    out = pl.pallas_call(
        functools.partial(_dsa_kernel, k_sel=k_sel, kc=kc, nbits=nbits),
        grid=(nb, nq),
        in_specs=[qcol(N_HEADS * LANES), qcol(N_IDX_HEADS * LANES), qcol(SUBLANES), seqblk,
                  pl.BlockSpec((V_T_ROWS, seq), lambda b, j: (0, b)), seqblk],
        out_specs=pl.BlockSpec((None, Q_BLOCK, N_HEADS * HEAD_DIM), lambda b, j: (b, j, 0)),
        out_shape=jax.ShapeDtypeStruct((nb, seq, N_HEADS * HEAD_DIM), F32),
        scratch_shapes=[pltpu.VMEM((seq, Q_BLOCK), I32),
                        pltpu.VMEM((seq, Q_BLOCK), I32),
                        pltpu.VMEM((SUBLANES, Q_BLOCK), I32),
                        pltpu.VMEM((N_HEADS, Q_BLOCK), F32),
                        pltpu.VMEM((SUBLANES, LANES), F32),
                        pltpu.VMEM((N_HEADS // 2, V_T_ROWS, 2 * Q_BLOCK), F32)],
        compiler_params=_cparams(("arbitrary", "arbitrary")),
        name="dsa_prompt",
    )(qt, qit, wt, r3(kb), vt, r3(kib))
    return out.reshape(nb * seq, N_HEADS * HEAD_DIM)


def _conva_kernel(cb_ref, cc_ref, cx_ref, cch_ref, cxh_ref, w_ref, o_ref, st_ref, *, tiles_per_seq):
    tm = cb_ref.shape[0]
    first = (pl.program_id(0) % tiles_per_seq) == 0
    u = cc_ref[...] * cx_ref[...]
    uh = jnp.where(first, 0.0, cch_ref[...] * cxh_ref[...])
    ext = jnp.concatenate([uh, u], axis=0)
    conv = (w_ref[2:3, :] * u + w_ref[1:2, :] * ext[SUBLANES - 1:SUBLANES - 1 + tm]
            + w_ref[0:1, :] * ext[SUBLANES - 2:SUBLANES - 2 + tm])
    o_ref[...] = cb_ref[...] * conv
    st_ref[...] = u[tm - 2:tm, :]


def _conva_prompt(proj, conv_w, nb, seq):
    m = proj.shape[0]
    tm = min(seq, 512)
    tps = seq // tm
    col = lambda off: pl.BlockSpec((tm, D_CONV), lambda i: (i, off // D_CONV))
    halo = lambda off: pl.BlockSpec(
        (SUBLANES, D_CONV), lambda i: (jnp.maximum(i * (tm // SUBLANES) - 1, 0), off // D_CONV))
    return pl.pallas_call(
        functools.partial(_conva_kernel, tiles_per_seq=tps),
        grid=(m // tm,),
        in_specs=[col(CB_OFF), col(CC_OFF), col(CX_OFF), halo(CC_OFF), halo(CX_OFF),
                  pl.BlockSpec((3, D_CONV), lambda i: (0, 0))],
        out_specs=[pl.BlockSpec((tm, D_CONV), lambda i: (i, 0)),
                   pl.BlockSpec((None, 2, D_CONV), lambda i: (i // tps, 0, 0))],
        out_shape=[jax.ShapeDtypeStruct((m, D_CONV), F32),
                   jax.ShapeDtypeStruct((nb, 2, D_CONV), F32)],
        compiler_params=_cparams(("arbitrary",)),
        name="conva_prompt",
    )(proj, proj, proj, proj, proj, conv_w)


def _ssd_kernel(xs_ref, bc_ref, dt_ref, xh_ref, bh_ref, cwx_ref, cwb_ref, cbx_ref, cbb_ref,
                dtb_ref, alog_ref, dsk_ref, y_ref, hout_ref, h_ref, *, nchunk):
    cl = SSM_CHUNK
    c = pl.program_id(1)
    first = c == 0

    @pl.when(first)
    def _():
        h_ref[...] = jnp.zeros(h_ref.shape, F32)

    def conv(cur, halo, w_ref, b_ref):
        ext = jnp.concatenate([jnp.where(first, 0.0, halo), cur], axis=0)
        out = (w_ref[3:4, :] * cur + w_ref[2:3, :] * ext[SUBLANES - 1:SUBLANES - 1 + cl]
               + w_ref[1:2, :] * ext[SUBLANES - 2:SUBLANES - 2 + cl]
               + w_ref[0:1, :] * ext[SUBLANES - 3:SUBLANES - 3 + cl] + b_ref[...])
        return _silu(out)

    xs = conv(xs_ref[...], xh_ref[...], cwx_ref, cbx_ref)
    bc = conv(bc_ref[...], bh_ref[...], cwb_ref, cbb_ref)
    dt = _softplus(dt_ref[...] + dtb_ref[...])
    a = dt * (-jnp.exp(alog_ref[...]))
    ri = lax.broadcasted_iota(I32, (cl, cl), 0)
    ci = lax.broadcasted_iota(I32, (cl, cl), 1)
    causal = ri >= ci
    cs = jnp.dot(jnp.where(causal, 1.0, 0.0), a, preferred_element_type=F32,
                 precision=lax.Precision.HIGHEST)
    cs_t = cs.T
    lane = lax.broadcasted_iota(I32, (cl, LANES), 1)
    lo = lane < SSM_HEAD_DIM
    rows_lo = lax.broadcasted_iota(I32, (LANES, 1), 0) < SSM_HEAD_DIM
    heads_per_group = SSM_HEADS // SSM_GROUPS

    cb = []
    for g in range(SSM_GROUPS):
        bg = bc[:, g * SSM_STATE:(g + 1) * SSM_STATE].astype(BF16)
        cg = bc[:, (SSM_GROUPS + g) * SSM_STATE:(SSM_GROUPS + g + 1) * SSM_STATE].astype(BF16)
        cb.append((bg, cg, lax.dot_general(cg, bg, NT_DIMS, preferred_element_type=F32)))

    for s in range(SSM_HEADS // 2):
        h0, h1 = 2 * s, 2 * s + 1
        bg, cg, cbg = cb[h0 // heads_per_group]
        sl = slice(s * LANES, (s + 1) * LANES)
        xs_s = xs[:, sl]
        col0, col1 = cs[:, h0:h0 + 1], cs[:, h1:h1 + 1]
        last0, last1 = cs[cl - 1:cl, h0:h0 + 1], cs[cl - 1:cl, h1:h1 + 1]
        xdt = xs_s * jnp.where(lo, dt[:, h0:h0 + 1], dt[:, h1:h1 + 1])
        xdt_b = xdt.astype(BF16)
        m0 = (cbg * jnp.where(causal, jnp.exp(col0 - cs_t[h0:h0 + 1, :]), 0.0)).astype(BF16)
        m1 = (cbg * jnp.where(causal, jnp.exp(col1 - cs_t[h1:h1 + 1, :]), 0.0)).astype(BF16)
        y_diag = jnp.where(lo, jnp.dot(m0, xdt_b, preferred_element_type=F32),
                           jnp.dot(m1, xdt_b, preferred_element_type=F32))
        hs = h_ref[sl, :]
        y_off = lax.dot_general(cg, hs.astype(BF16), NT_DIMS, preferred_element_type=F32)
        y_off = y_off * jnp.where(lo, jnp.exp(col0), jnp.exp(col1))
        y_ref[:, sl] = y_diag + y_off + dsk_ref[:, sl] * xs_s
        xw = xdt * jnp.where(lo, jnp.exp(last0 - col0), jnp.exp(last1 - col1))
        st = jnp.dot(xw.T.astype(BF16), bg, preferred_element_type=F32)
        h_ref[sl, :] = hs * jnp.where(rows_lo, jnp.exp(last0), jnp.exp(last1)) + st

    @pl.when(c == nchunk - 1)
    def _():
        hout_ref[...] = h_ref[...]


def _ssd_prompt(proj, cw, cbias, dtb, alog, dsk, nb, seq):
    m = proj.shape[0]
    cl = SSM_CHUNK
    nchunk = seq // cl
    blk = lambda w, off: pl.BlockSpec((cl, w), lambda b, c: (b * nchunk + c, off // w))
    halo = lambda off: pl.BlockSpec(
        (SUBLANES, 512), lambda b, c: (jnp.maximum((b * nchunk + c) * (cl // SUBLANES) - 1, 0), off // 512))
    const = lambda r, w, j: pl.BlockSpec((r, w), lambda b, c: (0, j))
    y, hout = pl.pallas_call(
        functools.partial(_ssd_kernel, nchunk=nchunk),
        grid=(nb, nchunk),
        in_specs=[blk(512, XS_OFF), blk(512, BC_OFF), blk(LANES, DT_OFF), halo(XS_OFF), halo(BC_OFF),
                  const(4, 512, 0), const(4, 512, 1), const(1, 512, 0), const(1, 512, 1),
                  const(1, LANES, 0), const(1, LANES, 0), const(1, 512, 0)],
        out_specs=[pl.BlockSpec((cl, SSM_INNER), lambda b, c: (b * nchunk + c, 0)),
                   pl.BlockSpec((None, SSM_INNER, SSM_STATE), lambda b, c: (b, 0, 0))],
        out_shape=[jax.ShapeDtypeStruct((m, SSM_INNER), F32),
                   jax.ShapeDtypeStruct((nb, SSM_INNER, SSM_STATE), F32)],
        scratch_shapes=[pltpu.VMEM((SSM_INNER, SSM_STATE), F32)],
        compiler_params=_cparams(("parallel", "arbitrary")),
        name="ssd_prompt",
    )(proj, proj, proj, proj, proj, cw, cw, cbias, cbias, dtb, alog, dsk)
    return y, hout


def _merge_kernel(x_ref, pa_ref, at_ref, ys_ref, z_ref, g0_ref, g1_ref, g2_ref, sn_ref, bg_ref,
                  wpc_ref, wpa_ref, wps_ref, wo_ref, o_ref):
    ssd = _rms(ys_ref[...] * _silu(z_ref[...]), sn_ref[...])
    ya = jnp.dot(pa_ref[...].astype(BF16), wpc_ref[...], preferred_element_type=F32)
    yb = jnp.dot(at_ref[...].astype(BF16), wpa_ref[...], preferred_element_type=F32)
    yc = jnp.dot(ssd.astype(BF16), wps_ref[...], preferred_element_type=F32)
    merged = (_sigmoid(g0_ref[...] + bg_ref[0:1, :]) * ya + _sigmoid(g1_ref[...] + bg_ref[1:2, :]) * yb
              + _sigmoid(g2_ref[...] + bg_ref[2:3, :]) * yc)
    o_ref[...] = x_ref[...] + jnp.dot(merged.astype(BF16), wo_ref[...], preferred_element_type=F32)


def _merge(x, pre_a, attn, y_ssd, proj, ssm_norm, b_gate, w_pc, w_pa, w_ps, w_o):
    m = x.shape[0]
    tm = min(m, 256)
    row = lambda w: pl.BlockSpec((tm, w), lambda i: (i, 0))
    pcol = lambda w, off: pl.BlockSpec((tm, w), lambda i: (i, off // w))
    const = lambda r, w: pl.BlockSpec((r, w), lambda i: (0, 0))
    return pl.pallas_call(
        _merge_kernel,
        grid=(m // tm,),
        in_specs=[row(D_MODEL), row(512), row(512), row(512), pcol(512, Z_OFF),
                  pcol(D_MODEL, G_OFF), pcol(D_MODEL, G_OFF + D_MODEL), pcol(D_MODEL, G_OFF + 2 * D_MODEL),
                  const(1, 512), const(3, D_MODEL),
                  const(512, D_MODEL), const(512, D_MODEL), const(512, D_MODEL), const(D_MODEL, D_MODEL)],
        out_specs=row(D_MODEL),
        out_shape=jax.ShapeDtypeStruct((m, D_MODEL), F32),
        compiler_params=_cparams(("parallel",)),
        name="merge",
    )(x, pre_a, attn, y_ssd, proj, proj, proj, proj, ssm_norm, b_gate, w_pc, w_pa, w_ps, w_o)


def _ffn_kernel(x_ref, g_ref, wr_ref, wg_ref, wu_ref, wd_ref, o_ref, h_ref, acc_ref, comb_ref,
                *, routed, n_e, n_f):
    e = pl.program_id(1)
    j = pl.program_id(2)
    tm = x_ref.shape[0]

    @pl.when(jnp.logical_and(e == 0, j == 0))
    def _():
        hf = _rms(x_ref[...], g_ref[...])
        h_ref[...] = hf.astype(BF16)
        acc_ref[...] = jnp.zeros(acc_ref.shape, F32)
        if routed:
            lane = lax.broadcasted_iota(I32, (tm, LANES), 1).astype(F32)
            logits = jnp.dot(hf, wr_ref[...], preferred_element_type=F32, precision=lax.Precision.HIGHEST)
            logits = jnp.where(lane < n_e, logits, -jnp.inf)
            m1 = jnp.max(logits, axis=1, keepdims=True)
            i1 = jnp.min(jnp.where(logits == m1, lane, float(LANES)), axis=1, keepdims=True)
            rest = jnp.where(lane == i1, -jnp.inf, logits)
            m2 = jnp.max(rest, axis=1, keepdims=True)
            i2 = jnp.min(jnp.where(rest == m2, lane, float(LANES)), axis=1, keepdims=True)
            e2 = jnp.exp(m2 - m1)
            den = 1.0 + e2
            comb_ref[...] = jnp.where(lane == i1, 1.0 / den, 0.0) + jnp.where(lane == i2, e2 / den, 0.0)

    h = h_ref[...]
    gate = jnp.dot(h, wg_ref[...], preferred_element_type=F32)
    up = jnp.dot(h, wu_ref[...], preferred_element_type=F32)
    act = _silu(gate) * up
    if routed:
        lane = lax.broadcasted_iota(I32, (tm, LANES), 1)
        act = act * jnp.sum(jnp.where(lane == e, comb_ref[...], 0.0), axis=1, keepdims=True)
    acc_ref[...] += jnp.dot(act.astype(BF16), wd_ref[...], preferred_element_type=F32)

    @pl.when(jnp.logical_and(e == n_e - 1, j == n_f - 1))
    def _():
        o_ref[...] = x_ref[...] + acc_ref[...]


def _ffn(x, g, w_router, w_gate, w_up, w_down, routed):
    m = x.shape[0]
    n_e, _, d_ff = w_gate.shape
    tm = min(m, 1024)
    tf = 256
    n_f = d_ff // tf
    return pl.pallas_call(
        functools.partial(_ffn_kernel, routed=routed, n_e=n_e, n_f=n_f),
        grid=(m // tm, n_e, n_f),
        in_specs=[pl.BlockSpec((tm, D_MODEL), lambda i, e, j: (i, 0)),
                  pl.BlockSpec((1, D_MODEL), lambda i, e, j: (0, 0)),
                  pl.BlockSpec((D_MODEL, LANES), lambda i, e, j: (0, 0)),
                  pl.BlockSpec((None, D_MODEL, tf), lambda i, e, j: (e, 0, j)),
                  pl.BlockSpec((None, D_MODEL, tf), lambda i, e, j: (e, 0, j)),
                  pl.BlockSpec((None, tf, D_MODEL), lambda i, e, j: (e, j, 0))],
        out_specs=pl.BlockSpec((tm, D_MODEL), lambda i, e, j: (i, 0)),
        out_shape=jax.ShapeDtypeStruct((m, D_MODEL), F32),
        scratch_shapes=[pltpu.VMEM((tm, D_MODEL), BF16), pltpu.VMEM((tm, D_MODEL), F32),
                        pltpu.VMEM((tm, LANES), F32)],
        compiler_params=_cparams(("parallel", "arbitrary", "arbitrary")),
        name="moe" if routed else "ffn",
    )(x, g, w_router, w_gate, w_up, w_down)


MOE_BLOCK = 2048
MOE_TILE = 256
TOP_K = 2


def _moe_tiles(tb):
    return TOP_K * tb // MOE_TILE + N_EXPERTS


def _route_kernel(x_ref, g_ref, wr_ref, hb_ref, row_ref, col_ref, meta_ref):
    tb = x_ref.shape[0]
    hf = _rms(x_ref[...], g_ref[...])
    hb_ref[...] = hf.astype(BF16)
    lane_i = lax.broadcasted_iota(I32, (tb, LANES), 1)
    lane = lane_i.astype(F32)
    logits = jnp.dot(hf, wr_ref[...], preferred_element_type=F32, precision=lax.Precision.HIGHEST)
    logits = jnp.where(lane_i < N_EXPERTS, logits, -jnp.inf)
    m1 = jnp.max(logits, axis=1, keepdims=True)
    i1 = jnp.min(jnp.where(logits == m1, lane, float(LANES)), axis=1, keepdims=True)
    rest = jnp.where(lane == i1, -jnp.inf, logits)
    m2 = jnp.max(rest, axis=1, keepdims=True)
    i2 = jnp.min(jnp.where(rest == m2, lane, float(LANES)), axis=1, keepdims=True)
    e2 = jnp.exp(m2 - m1)
    p1 = 1.0 / (1.0 + e2)
    p2 = e2 / (1.0 + e2)
    oh1 = lane == i1
    oh2 = lane == i2
    both = jnp.where(jnp.logical_or(oh1, oh2), 1.0, 0.0).astype(BF16)

    sb = MOE_TILE
    ri = lax.broadcasted_iota(I32, (sb, sb), 0)
    ci = lax.broadcasted_iota(I32, (sb, sb), 1)
    strict = jnp.where(ri > ci, 1.0, 0.0).astype(BF16)
    carry = jnp.zeros((1, LANES), F32)
    ranks = []
    for s in range(tb // sb):
        rows = both[s * sb:(s + 1) * sb]
        ranks.append(jnp.dot(strict, rows, preferred_element_type=F32) + carry)
        carry = carry + jnp.sum(rows.astype(F32), axis=0, keepdims=True)
    rank = jnp.concatenate(ranks, axis=0)
    seg = jnp.floor((carry + (MOE_TILE - 1)) * (1.0 / MOE_TILE)) * MOE_TILE
    ui = lax.broadcasted_iota(I32, (LANES, LANES), 0)
    uj = lax.broadcasted_iota(I32, (LANES, LANES), 1)
    before = jnp.where(ui < uj, 1.0, 0.0)
    off = jnp.dot(jnp.broadcast_to(seg, (SUBLANES, LANES)), before, preferred_element_type=F32,
                  precision=lax.Precision.HIGHEST)[0:1, :]
    dest = off + rank
    d1 = jnp.sum(jnp.where(oh1, dest, 0.0), axis=1, keepdims=True)
    d2 = jnp.sum(jnp.where(oh2, dest, 0.0), axis=1, keepdims=True)
    col = jnp.where(lane_i == 0, d1, jnp.where(lane_i == 1, d2, jnp.where(lane_i == 2, p1,
                                                                          jnp.where(lane_i == 3, p2, 0.0))))
    col_ref[...] = col
    row_ref[...] = col.T[0:SUBLANES, :]

    end_rows = jnp.broadcast_to(off + seg, (LANES, LANES)).T
    start = (lax.broadcasted_iota(I32, (LANES, LANES), 1) * MOE_TILE).astype(F32)
    is_expert = lax.broadcasted_iota(I32, (LANES, LANES), 0) < N_EXPERTS
    done = jnp.where(jnp.logical_and(is_expert, end_rows <= start), 1.0, 0.0)
    expert = jnp.minimum(jnp.sum(done, axis=0, keepdims=True), float(N_EXPERTS - 1))
    total = jnp.sum(jnp.where(lane_i[0:1] < N_EXPERTS, seg, 0.0), axis=1, keepdims=True)
    valid = jnp.where(start[0:1] < total, 1.0, 0.0)
    sub = lax.broadcasted_iota(I32, (SUBLANES, LANES), 0)
    meta_ref[...] = jnp.where(sub == 0, expert, jnp.where(sub == 1, valid, 0.0)).astype(I32)


def _route(x, g, w_router, tb):
    m = x.shape[0]
    nblk = m // tb
    return pl.pallas_call(
        _route_kernel,
        grid=(nblk,),
        in_specs=[pl.BlockSpec((tb, D_MODEL), lambda b: (b, 0)),
                  pl.BlockSpec((1, D_MODEL), lambda b: (0, 0)),
                  pl.BlockSpec((D_MODEL, LANES), lambda b: (0, 0))],
        out_specs=[pl.BlockSpec((tb, D_MODEL), lambda b: (b, 0)),
                   pl.BlockSpec((SUBLANES, tb), lambda b: (0, b)),
                   pl.BlockSpec((tb, LANES), lambda b: (b, 0)),
                   pl.BlockSpec((None, SUBLANES, LANES), lambda b: (b, 0, 0))],
        out_shape=[jax.ShapeDtypeStruct((m, D_MODEL), BF16),
                   jax.ShapeDtypeStruct((SUBLANES, m), F32),
                   jax.ShapeDtypeStruct((m, LANES), F32),
                   jax.ShapeDtypeStruct((nblk, SUBLANES, LANES), I32)],
        compiler_params=_cparams(("parallel",)),
        name="moe_route",
    )(x, g, w_router)


def _gffn_kernel(te_ref, tv_ref, hb_ref, row_ref, wg_ref, wu_ref, wd_ref, ys_ref, xg_ref, acc_ref, *, nt, n_f):
    del te_ref
    b, t, j = pl.program_id(0), pl.program_id(1), pl.program_id(2)
    valid = tv_ref[b * nt + t] > 0
    tb = hb_ref.shape[0]

    @pl.when(jnp.logical_and(valid, j == 0))
    def _():
        slot = (t * MOE_TILE + lax.broadcasted_iota(I32, (MOE_TILE, tb), 0)).astype(F32)
        hit = jnp.logical_or(row_ref[0:1, :] == slot, row_ref[1:2, :] == slot)
        onehot = jnp.where(hit, 1.0, 0.0).astype(BF16)
        xg_ref[...] = jnp.dot(onehot, hb_ref[...], preferred_element_type=F32).astype(BF16)
        acc_ref[...] = jnp.zeros(acc_ref.shape, F32)

    @pl.when(valid)
    def _():
        xg = xg_ref[...]
        gate = jnp.dot(xg, wg_ref[...], preferred_element_type=F32)
        up = jnp.dot(xg, wu_ref[...], preferred_element_type=F32)
        acc_ref[...] += jnp.dot((_silu(gate) * up).astype(BF16), wd_ref[...], preferred_element_type=F32)

    @pl.when(j == n_f - 1)
    def _():
        ys_ref[...] = jnp.where(valid, acc_ref[...], 0.0).astype(BF16)


def _gffn(tile_expert, tile_valid, hb, rowinfo, w_gate, w_up, w_down, tb):
    m = hb.shape[0]
    nblk = m // tb
    nt = _moe_tiles(tb)
    d_ff = w_gate.shape[2]
    tf = d_ff // 2
    n_f = d_ff // tf
    def fidx(b, t, j, te, tv):
        odd = t % 2 == 1
        return jnp.where(tv[b * nt + t] > 0, jnp.where(odd, n_f - 1 - j, j), jnp.where(odd, n_f - 1, 0))

    grid_spec = pltpu.PrefetchScalarGridSpec(
        num_scalar_prefetch=2,
        grid=(nblk, nt, n_f),
        in_specs=[pl.BlockSpec((tb, D_MODEL), lambda b, t, j, te, tv: (b, 0)),
                  pl.BlockSpec((SUBLANES, tb), lambda b, t, j, te, tv: (0, b)),
                  pl.BlockSpec((None, D_MODEL, tf), lambda b, t, j, te, tv: (te[b * nt + t], 0, fidx(b, t, j, te, tv))),
                  pl.BlockSpec((None, D_MODEL, tf), lambda b, t, j, te, tv: (te[b * nt + t], 0, fidx(b, t, j, te, tv))),
                  pl.BlockSpec((None, tf, D_MODEL), lambda b, t, j, te, tv: (te[b * nt + t], fidx(b, t, j, te, tv), 0))],
        out_specs=pl.BlockSpec((MOE_TILE, D_MODEL), lambda b, t, j, te, tv: (b * nt + t, 0)),
        scratch_shapes=[pltpu.VMEM((MOE_TILE, D_MODEL), BF16), pltpu.VMEM((MOE_TILE, D_MODEL), F32)],
    )
    return pl.pallas_call(
        functools.partial(_gffn_kernel, nt=nt, n_f=n_f),
        grid_spec=grid_spec,
        out_shape=jax.ShapeDtypeStruct((nblk * nt * MOE_TILE, D_MODEL), BF16),
        compiler_params=_cparams(("parallel", "arbitrary", "arbitrary")),
        name="moe_experts",
    )(tile_expert, tile_valid, hb, rowinfo, w_gate, w_up, w_down)


def _combine_kernel(x_ref, col_ref, ys_ref, o_ref):
    rows, slots = x_ref.shape[0], ys_ref.shape[0]
    slot = lax.broadcasted_iota(I32, (rows, slots), 1).astype(F32)
    weight = (jnp.where(col_ref[:, 0:1] == slot, col_ref[:, 2:3], 0.0)
              + jnp.where(col_ref[:, 1:2] == slot, col_ref[:, 3:4], 0.0))
    o_ref[...] = x_ref[...] + jnp.dot(weight.astype(BF16), ys_ref[...], preferred_element_type=F32)


def _combine(x, colinfo, ys, tb):
    m = x.shape[0]
    slots = _moe_tiles(tb) * MOE_TILE
    rows = MOE_TILE
    sub = tb // rows
    return pl.pallas_call(
        _combine_kernel,
        grid=(m // tb, sub),
        in_specs=[pl.BlockSpec((rows, D_MODEL), lambda b, r: (b * sub + r, 0)),
                  pl.BlockSpec((rows, LANES), lambda b, r: (b * sub + r, 0)),
                  pl.BlockSpec((slots, D_MODEL), lambda b, r: (b, 0))],
        out_specs=pl.BlockSpec((rows, D_MODEL), lambda b, r: (b * sub + r, 0)),
        out_shape=jax.ShapeDtypeStruct((m, D_MODEL), F32),
        compiler_params=_cparams(("parallel", "arbitrary")),
        name="moe_combine",
    )(x, colinfo, ys)


def _moe_grouped(x, g, w_router, w_gate, w_up, w_down):
    m = x.shape[0]
    tb = min(MOE_BLOCK, m)
    nt = _moe_tiles(tb)
    hb, rowinfo, colinfo, meta = _route(x, g, w_router, tb)
    tile_expert = meta[:, 0, :nt].reshape(-1)
    tile_valid = meta[:, 1, :nt].reshape(-1)
    ys = _gffn(tile_expert, tile_valid, hb, rowinfo, w_gate, w_up, w_down, tb)
    return _combine(x, colinfo, ys, tb)


def _decmix_kernel(cb_ref, cc_ref, cx_ref, xs_ref, bc_ref, dt_ref, sc0_ref, sc1_ref,
                   sx0_ref, sx1_ref, sx2_ref, sb0_ref, sb1_ref, sb2_ref,
                   cw_ref, cwx_ref, cwb_ref, cbx_ref, cbb_ref, dtb_ref, alog_ref,
                   pa_ref, u_ref, xso_ref, bco_ref, dto_ref, ea_ref):
    u = cc_ref[...] * cx_ref[...]
    u_ref[...] = u
    pa_ref[...] = cb_ref[...] * (cw_ref[0:1, :] * sc0_ref[...] + cw_ref[1:2, :] * sc1_ref[...] + cw_ref[2:3, :] * u)
    xso_ref[...] = _silu(cwx_ref[0:1, :] * sx0_ref[...] + cwx_ref[1:2, :] * sx1_ref[...]
                         + cwx_ref[2:3, :] * sx2_ref[...] + cwx_ref[3:4, :] * xs_ref[...] + cbx_ref[...])
    bco_ref[...] = _silu(cwb_ref[0:1, :] * sb0_ref[...] + cwb_ref[1:2, :] * sb1_ref[...]
                         + cwb_ref[2:3, :] * sb2_ref[...] + cwb_ref[3:4, :] * bc_ref[...] + cbb_ref[...])
    dt = _softplus(dt_ref[...] + dtb_ref[...])
    dto_ref[...] = dt
    ea_ref[...] = jnp.exp(dt * (-jnp.exp(alog_ref[...])))


def _decmix(proj, st_conv, st_ssm_conv, conv_w, cw, cbias, dtb, alog):
    nb = proj.shape[0]
    pcol = lambda w, off: pl.BlockSpec((nb, w), lambda i: (0, off // w))
    full = lambda w: pl.BlockSpec((nb, w), lambda i: (0, 0))
    const = lambda r, w, j: pl.BlockSpec((r, w), lambda i: (0, j))
    sx = [st_ssm_conv[:, t, :512] for t in range(3)]
    sb = [st_ssm_conv[:, t, 512:] for t in range(3)]
    return pl.pallas_call(
        _decmix_kernel,
        grid=(1,),
        in_specs=[pcol(512, CB_OFF), pcol(512, CC_OFF), pcol(512, CX_OFF), pcol(512, XS_OFF), pcol(512, BC_OFF),
                  pcol(LANES, DT_OFF)] + [full(512)] * 8
                 + [const(3, 512, 0), const(4, 512, 0), const(4, 512, 1), const(1, 512, 0), const(1, 512, 1),
                    const(1, LANES, 0), const(1, LANES, 0)],
        out_specs=[full(512), full(512), full(512), full(512), full(LANES), full(LANES)],
        out_shape=[jax.ShapeDtypeStruct((nb, 512), F32)] * 4 + [jax.ShapeDtypeStruct((nb, LANES), F32)] * 2,
        compiler_params=_cparams(("arbitrary",)),
        name="decode_mix",
    )(proj, proj, proj, proj, proj, proj, st_conv[:, 0], st_conv[:, 1], *sx, *sb,
      conv_w, cw, cw, cbias, cbias, dtb, alog)


def _decssd_kernel(h_ref, dt_ref, xs_ref, b_ref, c_ref, ea_ref, dsk_ref, y_ref, ho_ref):
    xs = xs_ref[...]
    h_new = h_ref[...] * ea_ref[...] + (xs * dt_ref[...]) * b_ref[...]
    ho_ref[...] = h_new
    y_ref[...] = jnp.sum(h_new * c_ref[...], axis=-1, keepdims=True) + dsk_ref[...] * xs


def _decssd(h0, dt, xs, bh, ch, ea, dsk):
    nb = h0.shape[0]
    blk = lambda a, b: pl.BlockSpec((None, SSM_HEADS, a, b), lambda i: (i, 0, 0, 0))
    return pl.pallas_call(
        _decssd_kernel,
        grid=(nb,),
        in_specs=[blk(SSM_HEAD_DIM, SSM_STATE), blk(1, 1), blk(SSM_HEAD_DIM, 1), blk(1, SSM_STATE),
                  blk(1, SSM_STATE), blk(1, 1), pl.BlockSpec((SSM_HEADS, 1, 1), lambda i: (0, 0, 0))],
        out_specs=[blk(SSM_HEAD_DIM, 1), blk(SSM_HEAD_DIM, SSM_STATE)],
        out_shape=[jax.ShapeDtypeStruct((nb, SSM_HEADS, SSM_HEAD_DIM, 1), F32),
                   jax.ShapeDtypeStruct((nb, SSM_HEADS, SSM_HEAD_DIM, SSM_STATE), F32)],
        compiler_params=_cparams(("parallel",)),
        name="decode_ssd",
    )(h0, dt, xs, bh, ch, ea, dsk)


DECODE_PAGE_GROUP = 32


def _decscore_kernel(pt_ref, qi_ref, w_ref, *refs):
    del pt_ref
    kid_refs, o_ref = refs[:-1], refs[-1]
    kid = jnp.concatenate([r[...].astype(BF16) for r in kid_refs], axis=0)
    s = lax.dot_general(qi_ref[:, 0:IDX_DIM].astype(BF16), kid, NT_DIMS, preferred_element_type=F32)
    sc = jnp.maximum(s, 0.0) * w_ref[:, 0:1]
    o_ref[...] = jnp.sum(sc, axis=0, keepdims=True)


def _page_specs(block, layer, pg):
    zeros = (0,) * (len(block) - 2)
    return [pl.BlockSpec(block, lambda b, p, pt, t=t: (layer, pt[b, p * pg + t]) + zeros) for t in range(pg)]


def _decscore(page_table, qip, wrows, pool_kidx, layer):
    nb, n_pages = page_table.shape
    pg = math.gcd(DECODE_PAGE_GROUP, n_pages)
    grid_spec = pltpu.PrefetchScalarGridSpec(
        num_scalar_prefetch=1,
        grid=(nb, n_pages // pg),
        in_specs=[pl.BlockSpec((None, N_IDX_HEADS, LANES), lambda b, p, pt: (b, 0, 0)),
                  pl.BlockSpec((None, N_IDX_HEADS, PAGE_SIZE), lambda b, p, pt: (b, 0, 0))]
                 + _page_specs((None, None, PAGE_SIZE, IDX_DIM), layer, pg),
        out_specs=pl.BlockSpec((None, None, 1, pg * PAGE_SIZE), lambda b, p, pt: (b, p, 0, 0)),
    )
    out = pl.pallas_call(
        _decscore_kernel,
        grid_spec=grid_spec,
        out_shape=jax.ShapeDtypeStruct((nb, n_pages // pg, 1, pg * PAGE_SIZE), F32),
        compiler_params=_cparams(("parallel", "arbitrary")),
        name="decode_scores",
    )(page_table, qip, wrows, *([pool_kidx] * pg))
    return out.reshape(nb, n_pages * PAGE_SIZE)


def _decselect_kernel(sc_ref, qi_ref, ki_ref, w_ref, sel_ref, self_ref, keys_ref, *, k_sel, kc, nbits):
    nb, past = sc_ref.shape
    nck = past // kc
    lane_k = lax.broadcasted_iota(I32, (nb, kc), 1)
    ki = ki_ref[...].astype(F32)
    own = jnp.zeros((nb, 1), F32)
    for h in range(N_IDX_HEADS):
        s = jnp.sum(qi_ref[:, h * LANES:(h + 1) * LANES].astype(F32) * ki, axis=1, keepdims=True)
        own = own + jnp.maximum(s, 0.0) * w_ref[:, h:h + 1]
    own_key = _sortable(own)

    def key_body(c, carry):
        off = pl.multiple_of(c * kc, kc)
        keys_ref[:, pl.ds(off, kc)] = _sortable(sc_ref[:, pl.ds(off, kc)])
        return carry

    lax.fori_loop(0, nck, key_body, 0)

    def count_where(pred):
        def body(c, acc):
            off = pl.multiple_of(c * kc, kc)
            f = jnp.where(pred(keys_ref[:, pl.ds(off, kc)], off), 1.0, 0.0)
            part = f[:, 0:LANES]
            for t in range(1, kc // LANES):
                part = part + f[:, t * LANES:(t + 1) * LANES]
            return acc + part

        acc = lax.fori_loop(0, nck, body, jnp.zeros((nb, LANES), F32))
        return jnp.sum(acc, axis=1, keepdims=True)

    def bit_body(i, t):
        cand = t + lax.shift_left(jnp.int32(1), 31 - i)
        cnt = count_where(lambda kk, off: kk >= cand) + jnp.where(own_key >= cand, 1.0, 0.0)
        return jnp.where(cnt >= k_sel, cand, t)

    thr = lax.fori_loop(0, 32, bit_body, jnp.full((nb, 1), INT_MIN, I32))
    thr = jnp.maximum(thr, KEY_NEG_INF)
    finite_thr = thr > KEY_NEG_INF
    cnt_gt = count_where(lambda kk, off: kk > thr) + jnp.where(own_key > thr, 1.0, 0.0)
    need = k_sel - cnt_gt

    def xbody(i, x):
        cand = x + lax.shift_left(jnp.int32(1), nbits - 1 - i)
        cnt = count_where(lambda kk, off: jnp.logical_and(kk == thr, off + lane_k < cand))
        return jnp.where(cnt < need, cand, x)

    xcut = lax.fori_loop(0, nbits, xbody, jnp.zeros((nb, 1), I32))
    xcut = jnp.where(finite_thr, xcut, -1)
    ties_past = count_where(lambda kk, off: kk == thr)
    own_tie = jnp.logical_and(jnp.logical_and(own_key == thr, finite_thr), ties_past < need)
    self_ref[...] = jnp.broadcast_to(
        jnp.where(jnp.logical_or(own_key > thr, own_tie), 1.0, 0.0), (nb, LANES))

    def sel_body(c, carry):
        off = pl.multiple_of(c * kc, kc)
        kk = keys_ref[:, pl.ds(off, kc)]
        sel = jnp.logical_or(kk > thr, jnp.logical_and(kk == thr, off + lane_k <= xcut))
        sel_ref[:, pl.ds(off, kc)] = jnp.where(sel, 1.0, 0.0)
        return carry

    lax.fori_loop(0, nck, sel_body, 0)


def _decselect(scores, qip, kib, wi):
    nb, past = scores.shape
    k_sel = min(TOPK_MAX, (past + 1) // 4)
    kc = next(c for c in (4096, 2048, 1024, 512, PAGE_SIZE) if past % c == 0)
    nbits = max(1, past.bit_length())
    full = lambda w: pl.BlockSpec((nb, w), lambda i: (0, 0))
    return pl.pallas_call(
        functools.partial(_decselect_kernel, k_sel=k_sel, kc=kc, nbits=nbits),
        grid=(1,),
        in_specs=[full(past), full(N_IDX_HEADS * LANES), full(LANES), full(LANES)],
        out_specs=[full(past), full(LANES)],
        out_shape=[jax.ShapeDtypeStruct((nb, past), F32), jax.ShapeDtypeStruct((nb, LANES), F32)],
        scratch_shapes=[pltpu.VMEM((nb, past), I32)],
        compiler_params=_cparams(("arbitrary",)),
        name="decode_select",
    )(scores, qip, kib, wi)


def _decattn_kernel(pt_ref, q_ref, sel_ref, kn_ref, vn_ref, self_ref, *refs, n_steps):
    del pt_ref
    pg = (len(refs) - 4) // 2
    kp_refs, vp_refs = refs[:pg], refs[pg:2 * pg]
    o_ref, m_ref, l_ref, acc_ref = refs[2 * pg:]
    p = pl.program_id(1)

    @pl.when(p == 0)
    def _():
        m_ref[...] = jnp.full(m_ref.shape, NEG_BIG, F32)
        l_ref[...] = jnp.zeros(l_ref.shape, F32)
        acc_ref[...] = jnp.zeros(acc_ref.shape, F32)

    q = q_ref[...]
    slab = lambda refs_: jnp.concatenate(
        [jnp.concatenate([r[g] for r in refs_], axis=1) for g in range(N_KV_HEADS)], axis=0).astype(BF16)
    s = jnp.dot(q.astype(BF16), slab(kp_refs), preferred_element_type=F32)
    s = jnp.where(sel_ref[...] > 0.0, s, -jnp.inf)
    m_old = m_ref[:, 0:1]
    m_new = jnp.maximum(m_old, jnp.max(s, axis=1, keepdims=True))
    alpha = jnp.exp(m_old - m_new)
    pr = jnp.exp(s - m_new)
    l_new = alpha * l_ref[:, 0:1] + jnp.sum(pr, axis=1, keepdims=True)
    pv = lax.dot_general(pr.astype(BF16), slab(vp_refs), NT_DIMS, preferred_element_type=F32)
    acc_ref[...] = alpha * acc_ref[...] + pv
    m_ref[...] = jnp.broadcast_to(m_new, m_ref.shape)
    l_ref[...] = jnp.broadcast_to(l_new, l_ref.shape)

    @pl.when(p == n_steps - 1)
    def _():
        s_own = jnp.sum(q * kn_ref[...], axis=1, keepdims=True)
        s_own = jnp.where(self_ref[:, 0:1] > 0.0, s_own, -jnp.inf)
        m_o = m_ref[:, 0:1]
        m_n = jnp.maximum(m_o, s_own)
        al = jnp.exp(m_o - m_n)
        p_own = jnp.exp(s_own - m_n)
        l_n = al * l_ref[:, 0:1] + p_own
        acc = al * acc_ref[...] + p_own.astype(BF16).astype(F32) * vn_ref[...]
        o_ref[...] = acc / l_n


def _decattn(page_table, q, sel, pool_kt, pool_vt, kn, vn, self_sel, layer):
    nb, n_pages = page_table.shape
    pg = math.gcd(DECODE_PAGE_GROUP, n_pages)
    n_steps = n_pages // pg
    row = lambda r: pl.BlockSpec((None, r, LANES), lambda b, p, pt: (b, 0, 0))
    pages = _page_specs((None, None, N_KV_HEADS, HEAD_DIM, PAGE_SIZE), layer, pg)
    grid_spec = pltpu.PrefetchScalarGridSpec(
        num_scalar_prefetch=1,
        grid=(nb, n_steps),
        in_specs=[row(N_HEADS),
                  pl.BlockSpec((None, None, 1, pg * PAGE_SIZE), lambda b, p, pt: (b, p, 0, 0)),
                  row(1), row(1), row(1)] + pages + pages,
        out_specs=row(N_HEADS),
        scratch_shapes=[pltpu.VMEM((N_HEADS, LANES), F32)] * 3,
    )
    return pl.pallas_call(
        functools.partial(_decattn_kernel, n_steps=n_steps),
        grid_spec=grid_spec,
        out_shape=jax.ShapeDtypeStruct((nb, N_HEADS, LANES), F32),
        compiler_params=_cparams(("parallel", "arbitrary")),
        name="decode_attend",
    )(page_table, q, sel.reshape(nb, n_steps, 1, pg * PAGE_SIZE), kn, vn, self_sel,
      *([pool_kt] * pg), *([pool_vt] * pg))


def _pack_w_in(w):
    d = w.shape[0]
    pad = lambda n: jnp.zeros((d, n), w.dtype)
    cols = [w[:, 4172:7244],
            w[:, 0:2048],
            w[:, 2628:3140],
            w[:, 3140:4164],
            w[:, 2048:2304],
            w[:, 2304:2560],
            w[:, 2560:2628], pad(60),
            w[:, 4164:4172], pad(120),
            pad(N_PROJ - 7424)]
    return jnp.concatenate(cols, axis=1).astype(BF16)


def _pad_lanes(v):
    return jnp.zeros((1, LANES), F32).at[0, :v.shape[0]].set(v)


def _rope_tables(pos):
    half = HEAD_DIM // 2
    inv = ROPE_THETA ** (-jnp.arange(half, dtype=F32) / half)
    ang = pos.astype(F32)[:, None] * inv[None, :]
    cos, sin = jnp.cos(ang), jnp.sin(ang)
    cos2 = jnp.concatenate([cos, cos], axis=1)
    sin2 = jnp.concatenate([-sin, sin], axis=1)
    return jnp.tile(cos2, (1, 2)), jnp.tile(sin2, (1, 2))


def _head_halves(o):
    hpg = N_HEADS // N_KV_HEADS
    parts = [o[:, h, (h // hpg) * HEAD_DIM:(h // hpg + 1) * HEAD_DIM] for h in range(N_HEADS)]
    return jnp.concatenate(parts, axis=-1)


def kernel(x_prompt, x_sample, cache_k, cache_v, cache_kidx, state_conv, state_ssm_conv, state_ssm, page_table,
           norm1, w_in, b_gate, conv_w, q_norm, k_norm, w_pc, w_pa, ssm_conv_w, ssm_conv_b, dt_bias, a_log,
           d_skip, ssm_norm, w_ps, w_o, norm2, w_gate_dense, w_up_dense, w_down_dense, w_router, w_gate_moe,
           w_up_moe, w_down_moe):
    nb, seq, _ = x_prompt.shape
    db = x_sample.shape[0]
    depth = w_in.shape[0]
    n_pages = page_table.shape[1]
    past = n_pages * PAGE_SIZE

    hp = x_prompt.reshape(nb * seq, D_MODEL)
    hs = x_sample.reshape(db, D_MODEL)
    cos_p, sin_p = _rope_tables(jnp.tile(jnp.arange(seq), nb))
    cos_s, sin_s = _rope_tables(jnp.full((db,), past))

    pool_kt = jnp.transpose(cache_k, (0, 1, 3, 4, 2))
    pool_vt = jnp.transpose(cache_v, (0, 1, 3, 4, 2))

    outs_p = [[] for _ in range(6)]
    outs_s = [[] for _ in range(6)]
    for l in range(depth):
        wp = _pack_w_in(w_in[l])
        g1 = norm1[l].reshape(1, D_MODEL)
        g2 = norm2[l].reshape(1, D_MODEL)
        qg = jnp.tile(q_norm[l], 2).reshape(1, LANES)
        kg = jnp.tile(k_norm[l], 2).reshape(1, LANES)
        cw = ssm_conv_w[l]
        cbias = ssm_conv_b[l].reshape(1, SSM_CONV_DIM)
        dtb = _pad_lanes(dt_bias[l])
        alog = _pad_lanes(a_log[l])
        dsk = jnp.repeat(d_skip[l], SSM_HEAD_DIM).reshape(1, SSM_INNER)
        sn = ssm_norm[l].reshape(1, SSM_INNER)
        wpc, wpa, wps, wo = (w.astype(BF16) for w in (w_pc[l], w_pa[l], w_ps[l], w_o[l]))
        i = l // 2
        if l % 2 == 0:
            routed = False
            wr = jnp.zeros((D_MODEL, LANES), F32)
            wg, wu, wd = (w[i:i + 1].astype(BF16) for w in (w_gate_dense, w_up_dense, w_down_dense))
        else:
            routed = True
            wr = jnp.zeros((D_MODEL, LANES), F32).at[:, :N_EXPERTS].set(w_router[i])
            wg, wu, wd = (w[i].astype(BF16) for w in (w_gate_moe, w_up_moe, w_down_moe))

        proj = _inproj(hp, g1, wp)
        qt, kf, kb, vt, qit, kif, kib, wt = _prep(proj, cos_p, sin_p, qg, kg, True)
        attn = _dsa_prompt(qt, qit, wt, kb, vt, kib, nb, seq)
        pre_a, conv_st = _conva_prompt(proj, conv_w[l], nb, seq)
        y_ssd, h_last = _ssd_prompt(proj, cw, cbias, dtb, alog, dsk, nb, seq)
        hp = _merge(hp, pre_a, attn, y_ssd, proj, sn, b_gate[l], wpc, wpa, wps, wo)
        if routed and hp.shape[0] % MOE_TILE == 0:
            hp = _moe_grouped(hp, g2, wr, wg, wu, wd)
        else:
            hp = _ffn(hp, g2, wr, wg, wu, wd, routed)
        proj3 = proj.reshape(nb, seq, N_PROJ)
        outs_p[0].append(kf.reshape(nb, seq, N_KV_HEADS, HEAD_DIM))
        outs_p[1].append(proj3[:, :, V_OFF:V_OFF + LANES].reshape(nb, seq, N_KV_HEADS, HEAD_DIM))
        outs_p[2].append(kif[:, :IDX_DIM].reshape(nb, seq, IDX_DIM))
        outs_p[3].append(conv_st)
        outs_p[4].append(proj3[:, seq - 3:, XS_OFF:XS_OFF + SSM_CONV_DIM])
        outs_p[5].append(h_last.reshape(nb, SSM_HEADS, SSM_HEAD_DIM, SSM_STATE))

        proj = _inproj(hs, g1, wp)
        qp, kf, kb, vb, qip, kif, kib, wi = _prep(proj, cos_s, sin_s, qg, kg, False)
        wrows = jnp.broadcast_to(wi[:, :N_IDX_HEADS, None], (db, N_IDX_HEADS, PAGE_SIZE))
        scores = _decscore(page_table, qip.astype(F32).reshape(db, N_IDX_HEADS, LANES), wrows, cache_kidx, l)
        sel, self_sel = _decselect(scores, qip, kib, wi)
        o = _decattn(page_table, qp.astype(F32).reshape(db, N_HEADS, LANES), sel, pool_kt, pool_vt,
                     kb.astype(F32).reshape(db, 1, LANES), vb.astype(F32).reshape(db, 1, LANES),
                     self_sel.reshape(db, 1, LANES), l)
        attn = _head_halves(o)
        pre_a, u, xs, bcv, dt, ea = _decmix(proj, state_conv[l], state_ssm_conv[l], conv_w[l], cw, cbias, dtb, alog)
        hpg = SSM_HEADS // SSM_GROUPS
        xs4 = xs.reshape(db, SSM_HEADS, SSM_HEAD_DIM, 1)
        bh = jnp.repeat(bcv[:, :SSM_GROUPS * SSM_STATE].reshape(db, SSM_GROUPS, 1, SSM_STATE), hpg, axis=1)
        ch = jnp.repeat(bcv[:, SSM_GROUPS * SSM_STATE:].reshape(db, SSM_GROUPS, 1, SSM_STATE), hpg, axis=1)
        y4, h_new = _decssd(state_ssm[l], dt[:, :SSM_HEADS, None, None], xs4, bh, ch, ea[:, :SSM_HEADS, None, None],
                            d_skip[l].reshape(SSM_HEADS, 1, 1))
        hs = _merge(hs, pre_a, attn, y4.reshape(db, SSM_INNER), proj, sn, b_gate[l], wpc, wpa, wps, wo)
        hs = _ffn(hs, g2, wr, wg, wu, wd, routed)
        outs_s[0].append(kf.reshape(db, 1, N_KV_HEADS, HEAD_DIM))
        outs_s[1].append(proj[:, V_OFF:V_OFF + LANES].reshape(db, 1, N_KV_HEADS, HEAD_DIM))
        outs_s[2].append(kif[:, :IDX_DIM].reshape(db, 1, IDX_DIM))
        outs_s[3].append(jnp.stack([state_conv[l][:, 1], u], axis=1))
        outs_s[4].append(jnp.concatenate(
            [state_ssm_conv[l][:, 1:], proj[:, None, XS_OFF:XS_OFF + SSM_CONV_DIM]], axis=1))
        outs_s[5].append(h_new)

    return (hp.reshape(nb, seq, D_MODEL), hs.reshape(db, 1, D_MODEL),
            *(jnp.stack(o) for o in outs_p), *(jnp.stack(o) for o in outs_s))
```

```python
import functools
import math

import jax
import jax.numpy as jnp
from jax import lax
from jax.experimental import pallas as pl
from jax.experimental.pallas import tpu as pltpu

F32 = jnp.float32
BF16 = jnp.bfloat16
I32 = jnp.int32

D_MODEL = 1024
D_CONV = 512
N_HEADS = 8
HEAD_DIM = 64
N_KV_HEADS = 2
N_IDX_HEADS = 4
IDX_DIM = 64
TOPK_MAX = 256
Q_BLOCK = 128
ROPE_THETA = 10000.0
SSM_INNER = 512
SSM_HEADS = 8
SSM_HEAD_DIM = 64
SSM_GROUPS = 2
SSM_STATE = 128
SSM_CHUNK = 128
SSM_CONV_DIM = 1024
PAGE_SIZE = 128
N_EXPERTS = 8
EPS = 1e-6

LANES = 128
SUBLANES = 8
VMEM_LIMIT = 52 * 1024 * 1024

G_OFF, CB_OFF, CC_OFF, CX_OFF, Q_OFF, Z_OFF, XS_OFF, BC_OFF = 0, 3072, 3584, 4096, 4608, 5120, 5632, 6144
K_OFF, V_OFF, QI_OFF, KW_OFF, DT_OFF = 6656, 6784, 6912, 7168, 7296
N_PROJ = 7680

INT_MIN = -2 ** 31
INT_MAX = 2 ** 31 - 1
KEY_NEG_INF = (-8388608) ^ 0x7FFFFFFF
NEG_BIG = -1e30
LOG2_E = 1.4426950408889634

NT_DIMS = (((1,), (1,)), ((), ()))


def _cparams(sem):
    return pltpu.CompilerParams(dimension_semantics=sem, vmem_limit_bytes=VMEM_LIMIT)


def _sigmoid(x):
    return 1.0 / (1.0 + jnp.exp(-x))


def _silu(x):
    return x * _sigmoid(x)


def _softplus(x):
    return jnp.maximum(x, 0.0) + jnp.log1p(jnp.exp(-jnp.abs(x)))


def _sortable(x):
    bits = pltpu.bitcast(x, I32)
    return bits ^ (lax.shift_right_arithmetic(bits, 31) & 0x7FFFFFFF)


def _rms(x, g):
    return x * lax.rsqrt(jnp.mean(x * x, axis=-1, keepdims=True) + EPS) * g


def _inproj_kernel(x_ref, g_ref, w_ref, o_ref, h_ref):
    @pl.when(pl.program_id(1) == 0)
    def _():
        h_ref[...] = _rms(x_ref[...], g_ref[...]).astype(BF16)

    o_ref[...] = jnp.dot(h_ref[...], w_ref[...], preferred_element_type=F32)


def _inproj(x, g, w):
    m = x.shape[0]
    tm = min(m, 1024)
    tn = 1536
    return pl.pallas_call(
        _inproj_kernel,
        grid=(m // tm, N_PROJ // tn),
        in_specs=[pl.BlockSpec((tm, D_MODEL), lambda i, j: (i, 0)),
                  pl.BlockSpec((1, D_MODEL), lambda i, j: (0, 0)),
                  pl.BlockSpec((D_MODEL, tn), lambda i, j: (0, j))],
        out_specs=pl.BlockSpec((tm, tn), lambda i, j: (i, j)),
        out_shape=jax.ShapeDtypeStruct((m, N_PROJ), F32),
        scratch_shapes=[pltpu.VMEM((tm, D_MODEL), BF16)],
        compiler_params=_cparams(("parallel", "arbitrary")),
        name="inproj",
    )(x, g, w)


V_T_ROWS = LANES + 16


def _prep_kernel(q_ref, k_ref, v_ref, qi_ref, kw_ref, cos_ref, sin_ref, qg_ref, kg_ref,
                 qp_ref, kf_ref, kb_ref, vb_ref, qip_ref, kif_ref, kib_ref, wo_ref, *, transposed):
    tm = cos_ref.shape[0]

    def put(ref, h, val):
        if transposed:
            vt = val.T.astype(BF16)
            for blk in range(tm // Q_BLOCK):
                ref[blk, h * LANES:(h + 1) * LANES, :] = vt[:, blk * Q_BLOCK:(blk + 1) * Q_BLOCK]
        else:
            ref[:, h * LANES:(h + 1) * LANES] = val.astype(BF16)

    cos = cos_ref[...]
    sin = sin_ref[...]
    lane = lax.broadcasted_iota(I32, (tm, LANES), 1)
    low_half = lane < HEAD_DIM
    first_rot = (lane % HEAD_DIM) < (HEAD_DIM // 2)
    r = lax.broadcasted_iota(I32, (LANES, LANES), 0) // HEAD_DIM
    c = lax.broadcasted_iota(I32, (LANES, LANES), 1) // HEAD_DIM
    seg = jnp.where(r == c, 1.0 / HEAD_DIM, 0.0).astype(BF16)

    def rope(x):
        fwd = pltpu.roll(x, LANES - HEAD_DIM // 2, 1)
        bwd = pltpu.roll(x, HEAD_DIM // 2, 1)
        return x * cos + jnp.where(first_rot, fwd, bwd) * sin

    def headnorm(x, g):
        s = x * x
        hi = s.astype(BF16)
        lo = (s - hi.astype(F32)).astype(BF16)
        ms = (jnp.dot(hi, seg, preferred_element_type=F32)
              + jnp.dot(lo, seg, preferred_element_type=F32))
        return x * lax.rsqrt(ms + EPS) * g

    q_scale = HEAD_DIM ** -0.5 * (LOG2_E if transposed else 1.0)
    qg = qg_ref[...]
    for s in range(N_HEADS // 2):
        slab = rope(headnorm(q_ref[:, s * LANES:(s + 1) * LANES], qg)) * q_scale
        swapped = pltpu.roll(slab, HEAD_DIM, 1)
        grp = (2 * s) // (N_HEADS // N_KV_HEADS)
        for hh in range(2):
            h = 2 * s + hh
            src = slab if hh == grp else swapped
            keep = low_half if grp == 0 else jnp.logical_not(low_half)
            put(qp_ref, h, jnp.where(keep, src, 0.0))

    k = rope(headnorm(k_ref[...], kg_ref[...]))
    kf_ref[...] = k
    kb_ref[...] = k.astype(BF16)
    if transposed:
        vb_ref[0:LANES, :] = v_ref[...].T.astype(BF16)
        vb_ref[LANES:V_T_ROWS, :] = jnp.ones((V_T_ROWS - LANES, tm), BF16)
    else:
        vb_ref[...] = v_ref[...].astype(BF16)

    for s in range(N_IDX_HEADS // 2):
        slab = rope(qi_ref[:, s * LANES:(s + 1) * LANES]) * (IDX_DIM ** -0.5)
        swapped = pltpu.roll(slab, HEAD_DIM, 1)
        for hh in range(2):
            h = 2 * s + hh
            src = slab if hh == 0 else swapped
            put(qip_ref, h, jnp.where(low_half, src, 0.0))

    kw = kw_ref[...]
    ki = rope(kw)
    kif_ref[...] = ki
    kib_ref[...] = jnp.where(low_half, ki, 0.0).astype(BF16)
    wi = pltpu.roll(kw, HEAD_DIM, 1) * (N_IDX_HEADS ** -0.5)
    if transposed:
        wt = wi.T[0:SUBLANES, :]
        for blk in range(tm // Q_BLOCK):
            wo_ref[blk] = wt[:, blk * Q_BLOCK:(blk + 1) * Q_BLOCK]
    else:
        wo_ref[...] = wi


def _prep(proj, cos, sin, qg, kg, transposed):
    m = proj.shape[0]
    tm = min(m, 512)
    row = lambda w, off: pl.BlockSpec((tm, w), lambda i: (i, off // w))
    full = lambda w: pl.BlockSpec((tm, w), lambda i: (i, 0))
    const = pl.BlockSpec((1, LANES), lambda i: (0, 0))
    if transposed:
        qblocks = tm // Q_BLOCK
        feat = lambda r: pl.BlockSpec((qblocks, r, Q_BLOCK), lambda i: (i, 0, 0))
        fshape = lambda r, dt: jax.ShapeDtypeStruct((m // Q_BLOCK, r, Q_BLOCK), dt)
        vspec = pl.BlockSpec((V_T_ROWS, tm), lambda i: (0, i))
        vshape = jax.ShapeDtypeStruct((V_T_ROWS, m), BF16)
    else:
        feat = full
        fshape = lambda r, dt: jax.ShapeDtypeStruct((m, r), dt)
        vspec = full(LANES)
        vshape = jax.ShapeDtypeStruct((m, LANES), BF16)
    return pl.pallas_call(
        functools.partial(_prep_kernel, transposed=transposed),
        grid=(m // tm,),
        in_specs=[row(512, Q_OFF), row(LANES, K_OFF), row(LANES, V_OFF), row(256, QI_OFF), row(LANES, KW_OFF),
                  full(LANES), full(LANES), const, const],
        out_specs=[feat(N_HEADS * LANES), full(LANES), full(LANES), vspec,
                   feat(N_IDX_HEADS * LANES), full(LANES), full(LANES), feat(SUBLANES if transposed else LANES)],
        out_shape=[fshape(N_HEADS * LANES, BF16),
                   jax.ShapeDtypeStruct((m, LANES), F32),
                   jax.ShapeDtypeStruct((m, LANES), BF16),
                   vshape,
                   fshape(N_IDX_HEADS * LANES, BF16),
                   jax.ShapeDtypeStruct((m, LANES), F32),
                   jax.ShapeDtypeStruct((m, LANES), BF16),
                   fshape(SUBLANES if transposed else LANES, F32)],
        compiler_params=_cparams(("parallel",)),
        name="prep",
    )(proj, proj, proj, proj, proj, cos, sin, qg, kg)


def _fold_rows(f):
    while f.shape[0] > SUBLANES:
        half = f.shape[0] // 2
        f = f[:half] + f[half:]
    return f


def _dsa_kernel(qt_ref, qit_ref, wt_ref, k_ref, vt_ref, ki_ref, o_ref,
                keys_ref, tpos_ref, x_ref, m_ref, kmax_ref, acc_ref, *, k_sel, kc, nbits):
    qb = Q_BLOCK
    n_keys = k_ref.shape[0]
    assert n_keys < 2 ** 22
    j = pl.program_id(1)
    nck = (j * qb + qb + kc - 1) // kc
    qpos = j * qb + lax.broadcasted_iota(I32, (1, qb), 1)
    sub_k = lax.broadcasted_iota(I32, (kc, qb), 0)
    pair = lambda ref, s: jnp.concatenate(
        [ref[(2 * s) * LANES:(2 * s + 1) * LANES, :], ref[(2 * s + 1) * LANES:(2 * s + 2) * LANES, :]], axis=1)
    qi2 = [pair(qit_ref, s) for s in range(N_IDX_HEADS // 2)]
    q2 = [pair(qt_ref, s) for s in range(N_HEADS // 2)]

    def score_body(c, carry):
        off = pl.multiple_of(c * kc, kc)
        kic = ki_ref[pl.ds(off, kc), :]
        acc = jnp.zeros((kc, qb), F32)
        for s in range(N_IDX_HEADS // 2):
            s2 = jnp.dot(kic, qi2[s], preferred_element_type=F32)
            for hh in range(2):
                h = 2 * s + hh
                acc = acc + jnp.maximum(s2[:, hh * qb:(hh + 1) * qb], 0.0) * wt_ref[h:h + 1, :]
        kpos = off + sub_k
        key = _sortable(jnp.where(kpos <= qpos, acc, -jnp.inf))
        key = jnp.where(key == 0, (n_keys - 1) - kpos, jnp.where(key == -1, -1 - kpos, key))
        keys_ref[pl.ds(off, kc), :] = key
        return carry

    lax.fori_loop(0, nck, score_body, 0)

    def count_where(pred, ref=keys_ref):
        def body(c, acc):
            off = pl.multiple_of(c * kc, kc)
            return acc + _fold_rows(jnp.where(pred(ref[pl.ds(off, kc), :], off), 1.0, 0.0))

        acc = lax.fori_loop(0, nck, body, jnp.zeros((SUBLANES, qb), F32))
        return jnp.sum(acc, axis=0, keepdims=True)

    def bit_body(i, carry):
        t, cnt_acc, cnt_rej = carry
        cand = t + lax.shift_left(jnp.int32(1), 31 - i)
        cnt = count_where(lambda kk, off: kk >= cand)
        ok = cnt >= k_sel
        return jnp.where(ok, cand, t), jnp.where(ok, cnt, cnt_acc), jnp.where(ok, cnt_rej, cnt)

    zero = jnp.zeros((1, qb), F32)
    thr, cnt_ge, cnt_gt = lax.fori_loop(0, 32, bit_body, (jnp.full((1, qb), INT_MIN, I32), zero, zero))
    thr = jnp.maximum(thr, KEY_NEG_INF)
    finite_thr = thr > KEY_NEG_INF
    need = k_sel - cnt_gt
    tie = jnp.logical_and(cnt_ge > k_sel, finite_thr)
    x_default = jnp.where(finite_thr, INT_MAX, -1)
    x_ref[...] = jnp.broadcast_to(x_default, x_ref.shape)

    @pl.when(jnp.max(jnp.where(tie, 1.0, 0.0)) > 0.0)
    def _():
        def tie_pos_body(c, carry):
            off = pl.multiple_of(c * kc, kc)
            tpos_ref[pl.ds(off, kc), :] = jnp.where(keys_ref[pl.ds(off, kc), :] == thr, off + sub_k, INT_MAX)
            return carry

        lax.fori_loop(0, nck, tie_pos_body, 0)

        def xbody(i, x):
            cand = x + lax.shift_left(jnp.int32(1), nbits - 1 - i)
            cnt = count_where(lambda tp, off: tp < cand, tpos_ref)
            return jnp.where(cnt < need, cand, x)

        x = lax.fori_loop(0, nbits, xbody, jnp.zeros((1, qb), I32))
        x_ref[...] = jnp.broadcast_to(jnp.where(tie, x, x_default), x_ref.shape)

    xcut = x_ref[0:1, :]
    hpg = N_HEADS // N_KV_HEADS

    @pl.when(j == 0)
    def _():
        r = lax.broadcasted_iota(I32, (LANES, LANES), 0) // HEAD_DIM
        cc = lax.broadcasted_iota(I32, (LANES, LANES), 1) // HEAD_DIM
        seg = jnp.where(r == cc, 1.0, 0.0).astype(BF16)

        def body(c, mx):
            kf = k_ref[pl.ds(pl.multiple_of(c * kc, kc), kc), :].astype(F32)
            gs = jnp.dot((kf * kf).astype(BF16), seg, preferred_element_type=F32)
            while gs.shape[0] > SUBLANES:
                half = gs.shape[0] // 2
                gs = jnp.maximum(gs[:half], gs[half:])
            return jnp.maximum(mx, gs)

        mx = lax.fori_loop(0, k_ref.shape[0] // kc, body, jnp.zeros((SUBLANES, LANES), F32))
        mx = jnp.max(mx, axis=0, keepdims=True)
        lane = lax.broadcasted_iota(I32, (1, LANES), 1)
        other = pltpu.roll(mx, HEAD_DIM, 1)
        kmax_ref[0:1, :] = jnp.where(lane < HEAD_DIM, mx, other)
        kmax_ref[1:2, :] = jnp.where(lane < HEAD_DIM, other, mx)

    def chunk(c):
        off = pl.multiple_of(c * kc, kc)
        kk = keys_ref[pl.ds(off, kc), :]
        sel = jnp.logical_or(kk > thr, jnp.logical_and(kk == thr, off + sub_k <= xcut))
        return sel, k_ref[pl.ds(off, kc), :], vt_ref[:, pl.ds(off, kc)]

    bound = []
    for h in range(N_HEADS):
        qh = qt_ref[h * LANES:(h + 1) * LANES, :].astype(F32)
        qsq = jnp.sum(qh * qh, axis=0, keepdims=True)
        bound.append(jnp.sqrt(qsq * kmax_ref[h // hpg:h // hpg + 1, :]))
    acc_ref[...] = jnp.zeros(acc_ref.shape, F32)

    def fast_body(c, carry):
        sel, kch, vch = chunk(c)
        n_pair = N_HEADS // 2
        s2 = jnp.dot(kch, q2[0], preferred_element_type=F32)
        for s in range(n_pair):
            s2_next = jnp.dot(kch, q2[s + 1], preferred_element_type=F32) if s + 1 < n_pair else None
            ps = [jnp.exp2(jnp.where(sel, s2[:, hh * qb:(hh + 1) * qb] - bound[2 * s + hh], -jnp.inf)).astype(BF16)
                  for hh in range(2)]
            acc_ref[s] += jnp.dot(vch, jnp.concatenate(ps, axis=1), preferred_element_type=F32)
            s2 = s2_next
        return carry

    lax.fori_loop(0, nck, fast_body, 0)

    dens = jnp.concatenate([acc_ref[s, LANES:LANES + 1, :] for s in range(N_HEADS // 2)], axis=0)

    @pl.when(jnp.logical_not(jnp.min(dens) >= 1e-20))
    def _():
        m_ref[...] = jnp.full(m_ref.shape, NEG_BIG, F32)
        acc_ref[...] = jnp.zeros(acc_ref.shape, F32)

        def exact_body(c, carry):
            sel, kch, vch = chunk(c)
            for s in range(N_HEADS // 2):
                s2 = jnp.dot(kch, q2[s], preferred_element_type=F32)
                ps, alphas = [], []
                for hh in range(2):
                    h = 2 * s + hh
                    sh = jnp.where(sel, s2[:, hh * qb:(hh + 1) * qb], -jnp.inf)
                    m_old = m_ref[h:h + 1, :]
                    m_new = jnp.maximum(m_old, jnp.max(sh, axis=0, keepdims=True))
                    m_ref[h:h + 1, :] = m_new
                    alphas.append(jnp.exp2(m_old - m_new))
                    ps.append(jnp.exp2(sh - m_new).astype(BF16))
                o2 = jnp.dot(vch, jnp.concatenate(ps, axis=1), preferred_element_type=F32)
                acc_ref[s] = jnp.concatenate(alphas, axis=1) * acc_ref[s] + o2
            return carry

        lax.fori_loop(0, nck, exact_body, 0)

    for s in range(N_HEADS // 2):
        g = (2 * s) // hpg
        num = acc_ref[s, g * HEAD_DIM:(g + 1) * HEAD_DIM, :]
        out2 = num / acc_ref[s, LANES:LANES + 1, :]
        o_ref[:, s * LANES:(s + 1) * LANES] = jnp.concatenate([out2[:, :qb], out2[:, qb:]], axis=0).T


def _dsa_prompt(qt, qit, wt, kb, vt, kib, nb, seq):
    k_sel = min(TOPK_MAX, seq // 4)
    kc = next(c for c in (512, Q_BLOCK) if seq % c == 0)
    nbits = max(1, (seq - 1).bit_length())
    nq = seq // Q_BLOCK
    r3 = lambda a: a.reshape(nb, seq, a.shape[-1])
    qcol = lambda r: pl.BlockSpec((None, r, Q_BLOCK), lambda b, j: (b * nq + j, 0, 0))
    seqblk = pl.BlockSpec((None, seq, LANES), lambda b, j: (b, 0, 0))
    out = pl.pallas_call(
        functools.partial(_dsa_kernel, k_sel=k_sel, kc=kc, nbits=nbits),
        grid=(nb, nq),
        in_specs=[qcol(N_HEADS * LANES), qcol(N_IDX_HEADS * LANES), qcol(SUBLANES), seqblk,
                  pl.BlockSpec((V_T_ROWS, seq), lambda b, j: (0, b)), seqblk],
        out_specs=pl.BlockSpec((None, Q_BLOCK, N_HEADS * HEAD_DIM), lambda b, j: (b, j, 0)),
        out_shape=jax.ShapeDtypeStruct((nb, seq, N_HEADS * HEAD_DIM), F32),
        scratch_shapes=[pltpu.VMEM((seq, Q_BLOCK), I32),
                        pltpu.VMEM((seq, Q_BLOCK), I32),
                        pltpu.VMEM((SUBLANES, Q_BLOCK), I32),
                        pltpu.VMEM((N_HEADS, Q_BLOCK), F32),
                        pltpu.VMEM((SUBLANES, LANES), F32),
                        pltpu.VMEM((N_HEADS // 2, V_T_ROWS, 2 * Q_BLOCK), F32)],
        compiler_params=_cparams(("arbitrary", "arbitrary")),
        name="dsa_prompt",
    )(qt, qit, wt, r3(kb), vt, r3(kib))
    return out.reshape(nb * seq, N_HEADS * HEAD_DIM)


def _conva_kernel(cb_ref, cc_ref, cx_ref, cch_ref, cxh_ref, w_ref, o_ref, st_ref, *, tiles_per_seq):
    tm = cb_ref.shape[0]
    first = (pl.program_id(0) % tiles_per_seq) == 0
    u = cc_ref[...] * cx_ref[...]
    uh = jnp.where(first, 0.0, cch_ref[...] * cxh_ref[...])
    ext = jnp.concatenate([uh, u], axis=0)
    conv = (w_ref[2:3, :] * u + w_ref[1:2, :] * ext[SUBLANES - 1:SUBLANES - 1 + tm]
            + w_ref[0:1, :] * ext[SUBLANES - 2:SUBLANES - 2 + tm])
    o_ref[...] = cb_ref[...] * conv
    st_ref[...] = u[tm - 2:tm, :]


def _conva_prompt(proj, conv_w, nb, seq):
    m = proj.shape[0]
    tm = min(seq, 512)
    tps = seq // tm
    col = lambda off: pl.BlockSpec((tm, D_CONV), lambda i: (i, off // D_CONV))
    halo = lambda off: pl.BlockSpec(
        (SUBLANES, D_CONV), lambda i: (jnp.maximum(i * (tm // SUBLANES) - 1, 0), off // D_CONV))
    return pl.pallas_call(
        functools.partial(_conva_kernel, tiles_per_seq=tps),
        grid=(m // tm,),
        in_specs=[col(CB_OFF), col(CC_OFF), col(CX_OFF), halo(CC_OFF), halo(CX_OFF),
                  pl.BlockSpec((3, D_CONV), lambda i: (0, 0))],
        out_specs=[pl.BlockSpec((tm, D_CONV), lambda i: (i, 0)),
                   pl.BlockSpec((None, 2, D_CONV), lambda i: (i // tps, 0, 0))],
        out_shape=[jax.ShapeDtypeStruct((m, D_CONV), F32),
                   jax.ShapeDtypeStruct((nb, 2, D_CONV), F32)],
        compiler_params=_cparams(("arbitrary",)),
        name="conva_prompt",
    )(proj, proj, proj, proj, proj, conv_w)


def _ssd_kernel(xs_ref, bc_ref, dt_ref, xh_ref, bh_ref, cwx_ref, cwb_ref, cbx_ref, cbb_ref,
                dtb_ref, alog_ref, dsk_ref, y_ref, hout_ref, h_ref, *, nchunk):
    cl = SSM_CHUNK
    c = pl.program_id(1)
    first = c == 0

    @pl.when(first)
    def _():
        h_ref[...] = jnp.zeros(h_ref.shape, F32)

    def conv(cur, halo, w_ref, b_ref):
        ext = jnp.concatenate([jnp.where(first, 0.0, halo), cur], axis=0)
        out = (w_ref[3:4, :] * cur + w_ref[2:3, :] * ext[SUBLANES - 1:SUBLANES - 1 + cl]
               + w_ref[1:2, :] * ext[SUBLANES - 2:SUBLANES - 2 + cl]
               + w_ref[0:1, :] * ext[SUBLANES - 3:SUBLANES - 3 + cl] + b_ref[...])
        return _silu(out)

    xs = conv(xs_ref[...], xh_ref[...], cwx_ref, cbx_ref)
    bc = conv(bc_ref[...], bh_ref[...], cwb_ref, cbb_ref)
    dt = _softplus(dt_ref[...] + dtb_ref[...])
    a = dt * (-jnp.exp(alog_ref[...]))
    ri = lax.broadcasted_iota(I32, (cl, cl), 0)
    ci = lax.broadcasted_iota(I32, (cl, cl), 1)
    causal = ri >= ci
    cs = jnp.dot(jnp.where(causal, 1.0, 0.0), a, preferred_element_type=F32,
                 precision=lax.Precision.HIGHEST)
    cs_t = cs.T
    lane = lax.broadcasted_iota(I32, (cl, LANES), 1)
    lo = lane < SSM_HEAD_DIM
    rows_lo = lax.broadcasted_iota(I32, (LANES, 1), 0) < SSM_HEAD_DIM
    heads_per_group = SSM_HEADS // SSM_GROUPS

    cb = []
    for g in range(SSM_GROUPS):
        bg = bc[:, g * SSM_STATE:(g + 1) * SSM_STATE].astype(BF16)
        cg = bc[:, (SSM_GROUPS + g) * SSM_STATE:(SSM_GROUPS + g + 1) * SSM_STATE].astype(BF16)
        cb.append((bg, cg, lax.dot_general(cg, bg, NT_DIMS, preferred_element_type=F32)))

    for s in range(SSM_HEADS // 2):
        h0, h1 = 2 * s, 2 * s + 1
        bg, cg, cbg = cb[h0 // heads_per_group]
        sl = slice(s * LANES, (s + 1) * LANES)
        xs_s = xs[:, sl]
        col0, col1 = cs[:, h0:h0 + 1], cs[:, h1:h1 + 1]
        last0, last1 = cs[cl - 1:cl, h0:h0 + 1], cs[cl - 1:cl, h1:h1 + 1]
        xdt = xs_s * jnp.where(lo, dt[:, h0:h0 + 1], dt[:, h1:h1 + 1])
        xdt_b = xdt.astype(BF16)
        m0 = (cbg * jnp.where(causal, jnp.exp(col0 - cs_t[h0:h0 + 1, :]), 0.0)).astype(BF16)
        m1 = (cbg * jnp.where(causal, jnp.exp(col1 - cs_t[h1:h1 + 1, :]), 0.0)).astype(BF16)
        y_diag = jnp.where(lo, jnp.dot(m0, xdt_b, preferred_element_type=F32),
                           jnp.dot(m1, xdt_b, preferred_element_type=F32))
        hs = h_ref[sl, :]
        y_off = lax.dot_general(cg, hs.astype(BF16), NT_DIMS, preferred_element_type=F32)
        y_off = y_off * jnp.where(lo, jnp.exp(col0), jnp.exp(col1))
        y_ref[:, sl] = y_diag + y_off + dsk_ref[:, sl] * xs_s
        xw = xdt * jnp.where(lo, jnp.exp(last0 - col0), jnp.exp(last1 - col1))
        st = jnp.dot(xw.T.astype(BF16), bg, preferred_element_type=F32)
        h_ref[sl, :] = hs * jnp.where(rows_lo, jnp.exp(last0), jnp.exp(last1)) + st

    @pl.when(c == nchunk - 1)
    def _():
        hout_ref[...] = h_ref[...]


def _ssd_prompt(proj, cw, cbias, dtb, alog, dsk, nb, seq):
    m = proj.shape[0]
    cl = SSM_CHUNK
    nchunk = seq // cl
    blk = lambda w, off: pl.BlockSpec((cl, w), lambda b, c: (b * nchunk + c, off // w))
    halo = lambda off: pl.BlockSpec(
        (SUBLANES, 512), lambda b, c: (jnp.maximum((b * nchunk + c) * (cl // SUBLANES) - 1, 0), off // 512))
    const = lambda r, w, j: pl.BlockSpec((r, w), lambda b, c: (0, j))
    y, hout = pl.pallas_call(
        functools.partial(_ssd_kernel, nchunk=nchunk),
        grid=(nb, nchunk),
        in_specs=[blk(512, XS_OFF), blk(512, BC_OFF), blk(LANES, DT_OFF), halo(XS_OFF), halo(BC_OFF),
                  const(4, 512, 0), const(4, 512, 1), const(1, 512, 0), const(1, 512, 1),
                  const(1, LANES, 0), const(1, LANES, 0), const(1, 512, 0)],
        out_specs=[pl.BlockSpec((cl, SSM_INNER), lambda b, c: (b * nchunk + c, 0)),
                   pl.BlockSpec((None, SSM_INNER, SSM_STATE), lambda b, c: (b, 0, 0))],
        out_shape=[jax.ShapeDtypeStruct((m, SSM_INNER), F32),
                   jax.ShapeDtypeStruct((nb, SSM_INNER, SSM_STATE), F32)],
        scratch_shapes=[pltpu.VMEM((SSM_INNER, SSM_STATE), F32)],
        compiler_params=_cparams(("parallel", "arbitrary")),
        name="ssd_prompt",
    )(proj, proj, proj, proj, proj, cw, cw, cbias, cbias, dtb, alog, dsk)
    return y, hout


def _merge_kernel(x_ref, pa_ref, at_ref, ys_ref, z_ref, g0_ref, g1_ref, g2_ref, sn_ref, bg_ref,
                  wpc_ref, wpa_ref, wps_ref, wo_ref, o_ref):
    ssd = _rms(ys_ref[...] * _silu(z_ref[...]), sn_ref[...])
    ya = jnp.dot(pa_ref[...].astype(BF16), wpc_ref[...], preferred_element_type=F32)
    yb = jnp.dot(at_ref[...].astype(BF16), wpa_ref[...], preferred_element_type=F32)
    yc = jnp.dot(ssd.astype(BF16), wps_ref[...], preferred_element_type=F32)
    merged = (_sigmoid(g0_ref[...] + bg_ref[0:1, :]) * ya + _sigmoid(g1_ref[...] + bg_ref[1:2, :]) * yb
              + _sigmoid(g2_ref[...] + bg_ref[2:3, :]) * yc)
    o_ref[...] = x_ref[...] + jnp.dot(merged.astype(BF16), wo_ref[...], preferred_element_type=F32)


def _merge(x, pre_a, attn, y_ssd, proj, ssm_norm, b_gate, w_pc, w_pa, w_ps, w_o):
    m = x.shape[0]
    tm = min(m, 256)
    row = lambda w: pl.BlockSpec((tm, w), lambda i: (i, 0))
    pcol = lambda w, off: pl.BlockSpec((tm, w), lambda i: (i, off // w))
    const = lambda r, w: pl.BlockSpec((r, w), lambda i: (0, 0))
    return pl.pallas_call(
        _merge_kernel,
        grid=(m // tm,),
        in_specs=[row(D_MODEL), row(512), row(512), row(512), pcol(512, Z_OFF),
                  pcol(D_MODEL, G_OFF), pcol(D_MODEL, G_OFF + D_MODEL), pcol(D_MODEL, G_OFF + 2 * D_MODEL),
                  const(1, 512), const(3, D_MODEL),
                  const(512, D_MODEL), const(512, D_MODEL), const(512, D_MODEL), const(D_MODEL, D_MODEL)],
        out_specs=row(D_MODEL),
        out_shape=jax.ShapeDtypeStruct((m, D_MODEL), F32),
        compiler_params=_cparams(("parallel",)),
        name="merge",
    )(x, pre_a, attn, y_ssd, proj, proj, proj, proj, ssm_norm, b_gate, w_pc, w_pa, w_ps, w_o)


def _ffn_kernel(x_ref, g_ref, wr_ref, wg_ref, wu_ref, wd_ref, o_ref, h_ref, acc_ref, comb_ref,
                *, routed, n_e, n_f):
    e = pl.program_id(1)
    j = pl.program_id(2)
    tm = x_ref.shape[0]

    @pl.when(jnp.logical_and(e == 0, j == 0))
    def _():
        hf = _rms(x_ref[...], g_ref[...])
        h_ref[...] = hf.astype(BF16)
        acc_ref[...] = jnp.zeros(acc_ref.shape, F32)
        if routed:
            lane = lax.broadcasted_iota(I32, (tm, LANES), 1).astype(F32)
            logits = jnp.dot(hf, wr_ref[...], preferred_element_type=F32, precision=lax.Precision.HIGHEST)
            logits = jnp.where(lane < n_e, logits, -jnp.inf)
            m1 = jnp.max(logits, axis=1, keepdims=True)
            i1 = jnp.min(jnp.where(logits == m1, lane, float(LANES)), axis=1, keepdims=True)
            rest = jnp.where(lane == i1, -jnp.inf, logits)
            m2 = jnp.max(rest, axis=1, keepdims=True)
            i2 = jnp.min(jnp.where(rest == m2, lane, float(LANES)), axis=1, keepdims=True)
            e2 = jnp.exp(m2 - m1)
            den = 1.0 + e2
            comb_ref[...] = jnp.where(lane == i1, 1.0 / den, 0.0) + jnp.where(lane == i2, e2 / den, 0.0)

    h = h_ref[...]
    gate = jnp.dot(h, wg_ref[...], preferred_element_type=F32)
    up = jnp.dot(h, wu_ref[...], preferred_element_type=F32)
    act = _silu(gate) * up
    if routed:
        lane = lax.broadcasted_iota(I32, (tm, LANES), 1)
        act = act * jnp.sum(jnp.where(lane == e, comb_ref[...], 0.0), axis=1, keepdims=True)
    acc_ref[...] += jnp.dot(act.astype(BF16), wd_ref[...], preferred_element_type=F32)

    @pl.when(jnp.logical_and(e == n_e - 1, j == n_f - 1))
    def _():
        o_ref[...] = x_ref[...] + acc_ref[...]


def _ffn(x, g, w_router, w_gate, w_up, w_down, routed):
    m = x.shape[0]
    n_e, _, d_ff = w_gate.shape
    tm = min(m, 1024)
    tf = 256
    n_f = d_ff // tf
    return pl.pallas_call(
        functools.partial(_ffn_kernel, routed=routed, n_e=n_e, n_f=n_f),
        grid=(m // tm, n_e, n_f),
        in_specs=[pl.BlockSpec((tm, D_MODEL), lambda i, e, j: (i, 0)),
                  pl.BlockSpec((1, D_MODEL), lambda i, e, j: (0, 0)),
                  pl.BlockSpec((D_MODEL, LANES), lambda i, e, j: (0, 0)),
                  pl.BlockSpec((None, D_MODEL, tf), lambda i, e, j: (e, 0, j)),
                  pl.BlockSpec((None, D_MODEL, tf), lambda i, e, j: (e, 0, j)),
                  pl.BlockSpec((None, tf, D_MODEL), lambda i, e, j: (e, j, 0))],
        out_specs=pl.BlockSpec((tm, D_MODEL), lambda i, e, j: (i, 0)),
        out_shape=jax.ShapeDtypeStruct((m, D_MODEL), F32),
        scratch_shapes=[pltpu.VMEM((tm, D_MODEL), BF16), pltpu.VMEM((tm, D_MODEL), F32),
                        pltpu.VMEM((tm, LANES), F32)],
        compiler_params=_cparams(("parallel", "arbitrary", "arbitrary")),
        name="moe" if routed else "ffn",
    )(x, g, w_router, w_gate, w_up, w_down)


MOE_BLOCK = 2048
MOE_TILE = 256
TOP_K = 2


def _moe_tiles(tb):
    return TOP_K * tb // MOE_TILE + N_EXPERTS


def _route_kernel(x_ref, g_ref, wr_ref, hb_ref, row_ref, col_ref, meta_ref):
    tb = x_ref.shape[0]
    hf = _rms(x_ref[...], g_ref[...])
    hb_ref[...] = hf.astype(BF16)
    lane_i = lax.broadcasted_iota(I32, (tb, LANES), 1)
    lane = lane_i.astype(F32)
    logits = jnp.dot(hf, wr_ref[...], preferred_element_type=F32, precision=lax.Precision.HIGHEST)
    logits = jnp.where(lane_i < N_EXPERTS, logits, -jnp.inf)
    m1 = jnp.max(logits, axis=1, keepdims=True)
    i1 = jnp.min(jnp.where(logits == m1, lane, float(LANES)), axis=1, keepdims=True)
    rest = jnp.where(lane == i1, -jnp.inf, logits)
    m2 = jnp.max(rest, axis=1, keepdims=True)
    i2 = jnp.min(jnp.where(rest == m2, lane, float(LANES)), axis=1, keepdims=True)
    e2 = jnp.exp(m2 - m1)
    p1 = 1.0 / (1.0 + e2)
    p2 = e2 / (1.0 + e2)
    oh1 = lane == i1
    oh2 = lane == i2
    both = jnp.where(jnp.logical_or(oh1, oh2), 1.0, 0.0).astype(BF16)

    sb = MOE_TILE
    ri = lax.broadcasted_iota(I32, (sb, sb), 0)
    ci = lax.broadcasted_iota(I32, (sb, sb), 1)
    strict = jnp.where(ri > ci, 1.0, 0.0).astype(BF16)
    carry = jnp.zeros((1, LANES), F32)
    ranks = []
    for s in range(tb // sb):
        rows = both[s * sb:(s + 1) * sb]
        ranks.append(jnp.dot(strict, rows, preferred_element_type=F32) + carry)
        carry = carry + jnp.sum(rows.astype(F32), axis=0, keepdims=True)
    rank = jnp.concatenate(ranks, axis=0)
    seg = jnp.floor((carry + (MOE_TILE - 1)) * (1.0 / MOE_TILE)) * MOE_TILE
    ui = lax.broadcasted_iota(I32, (LANES, LANES), 0)
    uj = lax.broadcasted_iota(I32, (LANES, LANES), 1)
    before = jnp.where(ui < uj, 1.0, 0.0)
    off = jnp.dot(jnp.broadcast_to(seg, (SUBLANES, LANES)), before, preferred_element_type=F32,
                  precision=lax.Precision.HIGHEST)[0:1, :]
    dest = off + rank
    d1 = jnp.sum(jnp.where(oh1, dest, 0.0), axis=1, keepdims=True)
    d2 = jnp.sum(jnp.where(oh2, dest, 0.0), axis=1, keepdims=True)
    col = jnp.where(lane_i == 0, d1, jnp.where(lane_i == 1, d2, jnp.where(lane_i == 2, p1,
                                                                          jnp.where(lane_i == 3, p2, 0.0))))
    col_ref[...] = col
    row_ref[...] = col.T[0:SUBLANES, :]

    end_rows = jnp.broadcast_to(off + seg, (LANES, LANES)).T
    start = (lax.broadcasted_iota(I32, (LANES, LANES), 1) * MOE_TILE).astype(F32)
    is_expert = lax.broadcasted_iota(I32, (LANES, LANES), 0) < N_EXPERTS
    done = jnp.where(jnp.logical_and(is_expert, end_rows <= start), 1.0, 0.0)
    expert = jnp.minimum(jnp.sum(done, axis=0, keepdims=True), float(N_EXPERTS - 1))
    total = jnp.sum(jnp.where(lane_i[0:1] < N_EXPERTS, seg, 0.0), axis=1, keepdims=True)
    valid = jnp.where(start[0:1] < total, 1.0, 0.0)
    sub = lax.broadcasted_iota(I32, (SUBLANES, LANES), 0)
    meta_ref[...] = jnp.where(sub == 0, expert, jnp.where(sub == 1, valid, 0.0)).astype(I32)


def _route(x, g, w_router, tb):
    m = x.shape[0]
    nblk = m // tb
    return pl.pallas_call(
        _route_kernel,
        grid=(nblk,),
        in_specs=[pl.BlockSpec((tb, D_MODEL), lambda b: (b, 0)),
                  pl.BlockSpec((1, D_MODEL), lambda b: (0, 0)),
                  pl.BlockSpec((D_MODEL, LANES), lambda b: (0, 0))],
        out_specs=[pl.BlockSpec((tb, D_MODEL), lambda b: (b, 0)),
                   pl.BlockSpec((SUBLANES, tb), lambda b: (0, b)),
                   pl.BlockSpec((tb, LANES), lambda b: (b, 0)),
                   pl.BlockSpec((None, SUBLANES, LANES), lambda b: (b, 0, 0))],
        out_shape=[jax.ShapeDtypeStruct((m, D_MODEL), BF16),
                   jax.ShapeDtypeStruct((SUBLANES, m), F32),
                   jax.ShapeDtypeStruct((m, LANES), F32),
                   jax.ShapeDtypeStruct((nblk, SUBLANES, LANES), I32)],
        compiler_params=_cparams(("parallel",)),
        name="moe_route",
    )(x, g, w_router)


def _gffn_kernel(te_ref, tv_ref, hb_ref, row_ref, wg_ref, wu_ref, wd_ref, ys_ref, xg_ref, acc_ref, *, nt, n_f):
    del te_ref
    b, t, j = pl.program_id(0), pl.program_id(1), pl.program_id(2)
    valid = tv_ref[b * nt + t] > 0
    tb = hb_ref.shape[0]

    @pl.when(jnp.logical_and(valid, j == 0))
    def _():
        slot = (t * MOE_TILE + lax.broadcasted_iota(I32, (MOE_TILE, tb), 0)).astype(F32)
        hit = jnp.logical_or(row_ref[0:1, :] == slot, row_ref[1:2, :] == slot)
        onehot = jnp.where(hit, 1.0, 0.0).astype(BF16)
        xg_ref[...] = jnp.dot(onehot, hb_ref[...], preferred_element_type=F32).astype(BF16)
        acc_ref[...] = jnp.zeros(acc_ref.shape, F32)

    @pl.when(valid)
    def _():
        xg = xg_ref[...]
        gate = jnp.dot(xg, wg_ref[...], preferred_element_type=F32)
        up = jnp.dot(xg, wu_ref[...], preferred_element_type=F32)
        acc_ref[...] += jnp.dot((_silu(gate) * up).astype(BF16), wd_ref[...], preferred_element_type=F32)

    @pl.when(j == n_f - 1)
    def _():
        ys_ref[...] = jnp.where(valid, acc_ref[...], 0.0).astype(BF16)


def _gffn(tile_expert, tile_valid, hb, rowinfo, w_gate, w_up, w_down, tb):
    m = hb.shape[0]
    nblk = m // tb
    nt = _moe_tiles(tb)
    d_ff = w_gate.shape[2]
    tf = d_ff // 2
    n_f = d_ff // tf
    def fidx(b, t, j, te, tv):
        odd = t % 2 == 1
        return jnp.where(tv[b * nt + t] > 0, jnp.where(odd, n_f - 1 - j, j), jnp.where(odd, n_f - 1, 0))

    grid_spec = pltpu.PrefetchScalarGridSpec(
        num_scalar_prefetch=2,
        grid=(nblk, nt, n_f),
        in_specs=[pl.BlockSpec((tb, D_MODEL), lambda b, t, j, te, tv: (b, 0)),
                  pl.BlockSpec((SUBLANES, tb), lambda b, t, j, te, tv: (0, b)),
                  pl.BlockSpec((None, D_MODEL, tf), lambda b, t, j, te, tv: (te[b * nt + t], 0, fidx(b, t, j, te, tv))),
                  pl.BlockSpec((None, D_MODEL, tf), lambda b, t, j, te, tv: (te[b * nt + t], 0, fidx(b, t, j, te, tv))),
                  pl.BlockSpec((None, tf, D_MODEL), lambda b, t, j, te, tv: (te[b * nt + t], fidx(b, t, j, te, tv), 0))],
        out_specs=pl.BlockSpec((MOE_TILE, D_MODEL), lambda b, t, j, te, tv: (b * nt + t, 0)),
        scratch_shapes=[pltpu.VMEM((MOE_TILE, D_MODEL), BF16), pltpu.VMEM((MOE_TILE, D_MODEL), F32)],
    )
    return pl.pallas_call(
        functools.partial(_gffn_kernel, nt=nt, n_f=n_f),
        grid_spec=grid_spec,
        out_shape=jax.ShapeDtypeStruct((nblk * nt * MOE_TILE, D_MODEL), BF16),
        compiler_params=_cparams(("parallel", "arbitrary", "arbitrary")),
        name="moe_experts",
    )(tile_expert, tile_valid, hb, rowinfo, w_gate, w_up, w_down)


def _combine_kernel(x_ref, col_ref, ys_ref, o_ref):
    rows, slots = x_ref.shape[0], ys_ref.shape[0]
    slot = lax.broadcasted_iota(I32, (rows, slots), 1).astype(F32)
    weight = (jnp.where(col_ref[:, 0:1] == slot, col_ref[:, 2:3], 0.0)
              + jnp.where(col_ref[:, 1:2] == slot, col_ref[:, 3:4], 0.0))
    o_ref[...] = x_ref[...] + jnp.dot(weight.astype(BF16), ys_ref[...], preferred_element_type=F32)


def _combine(x, colinfo, ys, tb):
    m = x.shape[0]
    slots = _moe_tiles(tb) * MOE_TILE
    rows = MOE_TILE
    sub = tb // rows
    return pl.pallas_call(
        _combine_kernel,
        grid=(m // tb, sub),
        in_specs=[pl.BlockSpec((rows, D_MODEL), lambda b, r: (b * sub + r, 0)),
                  pl.BlockSpec((rows, LANES), lambda b, r: (b * sub + r, 0)),
                  pl.BlockSpec((slots, D_MODEL), lambda b, r: (b, 0))],
        out_specs=pl.BlockSpec((rows, D_MODEL), lambda b, r: (b * sub + r, 0)),
        out_shape=jax.ShapeDtypeStruct((m, D_MODEL), F32),
        compiler_params=_cparams(("parallel", "arbitrary")),
        name="moe_combine",
    )(x, colinfo, ys)


def _moe_grouped(x, g, w_router, w_gate, w_up, w_down):
    m = x.shape[0]
    tb = min(MOE_BLOCK, m)
    nt = _moe_tiles(tb)
    hb, rowinfo, colinfo, meta = _route(x, g, w_router, tb)
    tile_expert = meta[:, 0, :nt].reshape(-1)
    tile_valid = meta[:, 1, :nt].reshape(-1)
    ys = _gffn(tile_expert, tile_valid, hb, rowinfo, w_gate, w_up, w_down, tb)
    return _combine(x, colinfo, ys, tb)


def _decmix_kernel(cb_ref, cc_ref, cx_ref, xs_ref, bc_ref, dt_ref, sc0_ref, sc1_ref,
                   sx0_ref, sx1_ref, sx2_ref, sb0_ref, sb1_ref, sb2_ref,
                   cw_ref, cwx_ref, cwb_ref, cbx_ref, cbb_ref, dtb_ref, alog_ref,
                   pa_ref, u_ref, xso_ref, bco_ref, dto_ref, ea_ref):
    u = cc_ref[...] * cx_ref[...]
    u_ref[...] = u
    pa_ref[...] = cb_ref[...] * (cw_ref[0:1, :] * sc0_ref[...] + cw_ref[1:2, :] * sc1_ref[...] + cw_ref[2:3, :] * u)
    xso_ref[...] = _silu(cwx_ref[0:1, :] * sx0_ref[...] + cwx_ref[1:2, :] * sx1_ref[...]
                         + cwx_ref[2:3, :] * sx2_ref[...] + cwx_ref[3:4, :] * xs_ref[...] + cbx_ref[...])
    bco_ref[...] = _silu(cwb_ref[0:1, :] * sb0_ref[...] + cwb_ref[1:2, :] * sb1_ref[...]
                         + cwb_ref[2:3, :] * sb2_ref[...] + cwb_ref[3:4, :] * bc_ref[...] + cbb_ref[...])
    dt = _softplus(dt_ref[...] + dtb_ref[...])
    dto_ref[...] = dt
    ea_ref[...] = jnp.exp(dt * (-jnp.exp(alog_ref[...])))


def _decmix(proj, st_conv, st_ssm_conv, conv_w, cw, cbias, dtb, alog):
    nb = proj.shape[0]
    pcol = lambda w, off: pl.BlockSpec((nb, w), lambda i: (0, off // w))
    full = lambda w: pl.BlockSpec((nb, w), lambda i: (0, 0))
    const = lambda r, w, j: pl.BlockSpec((r, w), lambda i: (0, j))
    sx = [st_ssm_conv[:, t, :512] for t in range(3)]
    sb = [st_ssm_conv[:, t, 512:] for t in range(3)]
    return pl.pallas_call(
        _decmix_kernel,
        grid=(1,),
        in_specs=[pcol(512, CB_OFF), pcol(512, CC_OFF), pcol(512, CX_OFF), pcol(512, XS_OFF), pcol(512, BC_OFF),
                  pcol(LANES, DT_OFF)] + [full(512)] * 8
                 + [const(3, 512, 0), const(4, 512, 0), const(4, 512, 1), const(1, 512, 0), const(1, 512, 1),
                    const(1, LANES, 0), const(1, LANES, 0)],
        out_specs=[full(512), full(512), full(512), full(512), full(LANES), full(LANES)],
        out_shape=[jax.ShapeDtypeStruct((nb, 512), F32)] * 4 + [jax.ShapeDtypeStruct((nb, LANES), F32)] * 2,
        compiler_params=_cparams(("arbitrary",)),
        name="decode_mix",
    )(proj, proj, proj, proj, proj, proj, st_conv[:, 0], st_conv[:, 1], *sx, *sb,
      conv_w, cw, cw, cbias, cbias, dtb, alog)


def _decssd_kernel(h_ref, dt_ref, xs_ref, b_ref, c_ref, ea_ref, dsk_ref, y_ref, ho_ref):
    xs = xs_ref[...]
    h_new = h_ref[...] * ea_ref[...] + (xs * dt_ref[...]) * b_ref[...]
    ho_ref[...] = h_new
    y_ref[...] = jnp.sum(h_new * c_ref[...], axis=-1, keepdims=True) + dsk_ref[...] * xs


def _decssd(h0, dt, xs, bh, ch, ea, dsk):
    nb = h0.shape[0]
    blk = lambda a, b: pl.BlockSpec((None, SSM_HEADS, a, b), lambda i: (i, 0, 0, 0))
    return pl.pallas_call(
        _decssd_kernel,
        grid=(nb,),
        in_specs=[blk(SSM_HEAD_DIM, SSM_STATE), blk(1, 1), blk(SSM_HEAD_DIM, 1), blk(1, SSM_STATE),
                  blk(1, SSM_STATE), blk(1, 1), pl.BlockSpec((SSM_HEADS, 1, 1), lambda i: (0, 0, 0))],
        out_specs=[blk(SSM_HEAD_DIM, 1), blk(SSM_HEAD_DIM, SSM_STATE)],
        out_shape=[jax.ShapeDtypeStruct((nb, SSM_HEADS, SSM_HEAD_DIM, 1), F32),
                   jax.ShapeDtypeStruct((nb, SSM_HEADS, SSM_HEAD_DIM, SSM_STATE), F32)],
        compiler_params=_cparams(("parallel",)),
        name="decode_ssd",
    )(h0, dt, xs, bh, ch, ea, dsk)


DECODE_PAGE_GROUP = 32


def _decscore_kernel(pt_ref, qi_ref, w_ref, *refs):
    del pt_ref
    kid_refs, o_ref = refs[:-1], refs[-1]
    kid = jnp.concatenate([r[...] for r in kid_refs], axis=1).astype(BF16)
    kid = jnp.concatenate([kid, jnp.zeros_like(kid)], axis=0)
    s = jnp.dot(qi_ref[...].astype(BF16), kid, preferred_element_type=F32)
    sc = jnp.maximum(s, 0.0) * w_ref[:, 0:1]
    o_ref[...] = jnp.sum(sc, axis=0, keepdims=True)


def _page_specs(block, layer, pg):
    zeros = (0,) * (len(block) - 2)
    return [pl.BlockSpec(block, lambda b, p, pt, t=t: (layer, pt[b, p * pg + t]) + zeros) for t in range(pg)]


def _decscore(page_table, qip, wrows, pool_kidx, layer):
    nb, n_pages = page_table.shape
    pg = math.gcd(DECODE_PAGE_GROUP, n_pages)
    grid_spec = pltpu.PrefetchScalarGridSpec(
        num_scalar_prefetch=1,
        grid=(nb, n_pages // pg),
        in_specs=[pl.BlockSpec((None, N_IDX_HEADS, LANES), lambda b, p, pt: (b, 0, 0)),
                  pl.BlockSpec((None, N_IDX_HEADS, PAGE_SIZE), lambda b, p, pt: (b, 0, 0))]
                 + _page_specs((None, None, IDX_DIM, PAGE_SIZE), layer, pg),
        out_specs=pl.BlockSpec((None, None, 1, pg * PAGE_SIZE), lambda b, p, pt: (b, p, 0, 0)),
    )
    out = pl.pallas_call(
        _decscore_kernel,
        grid_spec=grid_spec,
        out_shape=jax.ShapeDtypeStruct((nb, n_pages // pg, 1, pg * PAGE_SIZE), F32),
        compiler_params=_cparams(("parallel", "arbitrary")),
        name="decode_scores",
    )(page_table, qip, wrows, *([pool_kidx] * pg))
    return out.reshape(nb, n_pages * PAGE_SIZE)


def _decselect_kernel(sc_ref, qi_ref, ki_ref, w_ref, sel_ref, self_ref, keys_ref, *, k_sel, kc, nbits):
    nb, past = sc_ref.shape
    nck = past // kc
    lane_k = lax.broadcasted_iota(I32, (nb, kc), 1)
    ki = ki_ref[...].astype(F32)
    own = jnp.zeros((nb, 1), F32)
    for h in range(N_IDX_HEADS):
        s = jnp.sum(qi_ref[:, h * LANES:(h + 1) * LANES].astype(F32) * ki, axis=1, keepdims=True)
        own = own + jnp.maximum(s, 0.0) * w_ref[:, h:h + 1]
    own_key = _sortable(own)

    def key_body(c, carry):
        off = pl.multiple_of(c * kc, kc)
        keys_ref[:, pl.ds(off, kc)] = _sortable(sc_ref[:, pl.ds(off, kc)])
        return carry

    lax.fori_loop(0, nck, key_body, 0)

    def count_where(pred):
        def body(c, acc):
            off = pl.multiple_of(c * kc, kc)
            f = jnp.where(pred(keys_ref[:, pl.ds(off, kc)], off), 1.0, 0.0)
            part = f[:, 0:LANES]
            for t in range(1, kc // LANES):
                part = part + f[:, t * LANES:(t + 1) * LANES]
            return acc + part

        acc = lax.fori_loop(0, nck, body, jnp.zeros((nb, LANES), F32))
        return jnp.sum(acc, axis=1, keepdims=True)

    def bit_body(i, t):
        cand = t + lax.shift_left(jnp.int32(1), 31 - i)
        cnt = count_where(lambda kk, off: kk >= cand) + jnp.where(own_key >= cand, 1.0, 0.0)
        return jnp.where(cnt >= k_sel, cand, t)

    thr = lax.fori_loop(0, 32, bit_body, jnp.full((nb, 1), INT_MIN, I32))
    thr = jnp.maximum(thr, KEY_NEG_INF)
    finite_thr = thr > KEY_NEG_INF
    cnt_gt = count_where(lambda kk, off: kk > thr) + jnp.where(own_key > thr, 1.0, 0.0)
    need = k_sel - cnt_gt

    def xbody(i, x):
        cand = x + lax.shift_left(jnp.int32(1), nbits - 1 - i)
        cnt = count_where(lambda kk, off: jnp.logical_and(kk == thr, off + lane_k < cand))
        return jnp.where(cnt < need, cand, x)

    xcut = lax.fori_loop(0, nbits, xbody, jnp.zeros((nb, 1), I32))
    xcut = jnp.where(finite_thr, xcut, -1)
    ties_past = count_where(lambda kk, off: kk == thr)
    own_tie = jnp.logical_and(jnp.logical_and(own_key == thr, finite_thr), ties_past < need)
    self_ref[...] = jnp.broadcast_to(
        jnp.where(jnp.logical_or(own_key > thr, own_tie), 1.0, 0.0), (nb, LANES))

    def sel_body(c, carry):
        off = pl.multiple_of(c * kc, kc)
        kk = keys_ref[:, pl.ds(off, kc)]
        sel = jnp.logical_or(kk > thr, jnp.logical_and(kk == thr, off + lane_k <= xcut))
        sel_ref[:, pl.ds(off, kc)] = jnp.where(sel, 1.0, 0.0)
        return carry

    lax.fori_loop(0, nck, sel_body, 0)


def _decselect(scores, qip, kib, wi):
    nb, past = scores.shape
    k_sel = min(TOPK_MAX, (past + 1) // 4)
    kc = next(c for c in (4096, 2048, 1024, 512, PAGE_SIZE) if past % c == 0)
    nbits = max(1, past.bit_length())
    full = lambda w: pl.BlockSpec((nb, w), lambda i: (0, 0))
    return pl.pallas_call(
        functools.partial(_decselect_kernel, k_sel=k_sel, kc=kc, nbits=nbits),
        grid=(1,),
        in_specs=[full(past), full(N_IDX_HEADS * LANES), full(LANES), full(LANES)],
        out_specs=[full(past), full(LANES)],
        out_shape=[jax.ShapeDtypeStruct((nb, past), F32), jax.ShapeDtypeStruct((nb, LANES), F32)],
        scratch_shapes=[pltpu.VMEM((nb, past), I32)],
        compiler_params=_cparams(("arbitrary",)),
        name="decode_select",
    )(scores, qip, kib, wi)


def _decattn_kernel(pt_ref, q_ref, sel_ref, kn_ref, vn_ref, self_ref, *refs, n_steps):
    del pt_ref
    pg = (len(refs) - 4) // 2
    kp_refs, vp_refs = refs[:pg], refs[pg:2 * pg]
    o_ref, m_ref, l_ref, acc_ref = refs[2 * pg:]
    p = pl.program_id(1)

    @pl.when(p == 0)
    def _():
        m_ref[...] = jnp.full(m_ref.shape, NEG_BIG, F32)
        l_ref[...] = jnp.zeros(l_ref.shape, F32)
        acc_ref[...] = jnp.zeros(acc_ref.shape, F32)

    q = q_ref[...]
    slab = lambda refs_: jnp.concatenate(
        [jnp.concatenate([r[g] for r in refs_], axis=1) for g in range(N_KV_HEADS)], axis=0).astype(BF16)
    s = jnp.dot(q.astype(BF16), slab(kp_refs), preferred_element_type=F32)
    s = jnp.where(sel_ref[...] > 0.0, s, -jnp.inf)
    m_old = m_ref[:, 0:1]
    m_new = jnp.maximum(m_old, jnp.max(s, axis=1, keepdims=True))
    alpha = jnp.exp(m_old - m_new)
    pr = jnp.exp(s - m_new)
    l_new = alpha * l_ref[:, 0:1] + jnp.sum(pr, axis=1, keepdims=True)
    pv = lax.dot_general(pr.astype(BF16), slab(vp_refs), NT_DIMS, preferred_element_type=F32)
    acc_ref[...] = alpha * acc_ref[...] + pv
    m_ref[...] = jnp.broadcast_to(m_new, m_ref.shape)
    l_ref[...] = jnp.broadcast_to(l_new, l_ref.shape)

    @pl.when(p == n_steps - 1)
    def _():
        s_own = jnp.sum(q * kn_ref[...], axis=1, keepdims=True)
        s_own = jnp.where(self_ref[:, 0:1] > 0.0, s_own, -jnp.inf)
        m_o = m_ref[:, 0:1]
        m_n = jnp.maximum(m_o, s_own)
        al = jnp.exp(m_o - m_n)
        p_own = jnp.exp(s_own - m_n)
        l_n = al * l_ref[:, 0:1] + p_own
        acc = al * acc_ref[...] + p_own.astype(BF16).astype(F32) * vn_ref[...]
        o_ref[...] = acc / l_n


def _decattn(page_table, q, sel, pool_kt, pool_vt, kn, vn, self_sel, layer):
    nb, n_pages = page_table.shape
    pg = math.gcd(DECODE_PAGE_GROUP, n_pages)
    n_steps = n_pages // pg
    row = lambda r: pl.BlockSpec((None, r, LANES), lambda b, p, pt: (b, 0, 0))
    pages = _page_specs((None, None, N_KV_HEADS, HEAD_DIM, PAGE_SIZE), layer, pg)
    grid_spec = pltpu.PrefetchScalarGridSpec(
        num_scalar_prefetch=1,
        grid=(nb, n_steps),
        in_specs=[row(N_HEADS),
                  pl.BlockSpec((None, None, 1, pg * PAGE_SIZE), lambda b, p, pt: (b, p, 0, 0)),
                  row(1), row(1), row(1)] + pages + pages,
        out_specs=row(N_HEADS),
        scratch_shapes=[pltpu.VMEM((N_HEADS, LANES), F32)] * 3,
    )
    return pl.pallas_call(
        functools.partial(_decattn_kernel, n_steps=n_steps),
        grid_spec=grid_spec,
        out_shape=jax.ShapeDtypeStruct((nb, N_HEADS, LANES), F32),
        compiler_params=_cparams(("parallel", "arbitrary")),
        name="decode_attend",
    )(page_table, q, sel.reshape(nb, n_steps, 1, pg * PAGE_SIZE), kn, vn, self_sel,
      *([pool_kt] * pg), *([pool_vt] * pg))


def _pack_w_in(w):
    d = w.shape[0]
    pad = lambda n: jnp.zeros((d, n), w.dtype)
    cols = [w[:, 4172:7244],
            w[:, 0:2048],
            w[:, 2628:3140],
            w[:, 3140:4164],
            w[:, 2048:2304],
            w[:, 2304:2560],
            w[:, 2560:2628], pad(60),
            w[:, 4164:4172], pad(120),
            pad(N_PROJ - 7424)]
    return jnp.concatenate(cols, axis=1).astype(BF16)


def _pad_lanes(v):
    return jnp.zeros((1, LANES), F32).at[0, :v.shape[0]].set(v)


def _rope_tables(pos):
    half = HEAD_DIM // 2
    inv = ROPE_THETA ** (-jnp.arange(half, dtype=F32) / half)
    ang = pos.astype(F32)[:, None] * inv[None, :]
    cos, sin = jnp.cos(ang), jnp.sin(ang)
    cos2 = jnp.concatenate([cos, cos], axis=1)
    sin2 = jnp.concatenate([-sin, sin], axis=1)
    return jnp.tile(cos2, (1, 2)), jnp.tile(sin2, (1, 2))


def _head_halves(o):
    hpg = N_HEADS // N_KV_HEADS
    parts = [o[:, h, (h // hpg) * HEAD_DIM:(h // hpg + 1) * HEAD_DIM] for h in range(N_HEADS)]
    return jnp.concatenate(parts, axis=-1)


def kernel(x_prompt, x_sample, cache_k, cache_v, cache_kidx, state_conv, state_ssm_conv, state_ssm, page_table,
           norm1, w_in, b_gate, conv_w, q_norm, k_norm, w_pc, w_pa, ssm_conv_w, ssm_conv_b, dt_bias, a_log,
           d_skip, ssm_norm, w_ps, w_o, norm2, w_gate_dense, w_up_dense, w_down_dense, w_router, w_gate_moe,
           w_up_moe, w_down_moe):
    nb, seq, _ = x_prompt.shape
    db = x_sample.shape[0]
    depth = w_in.shape[0]
    n_pages = page_table.shape[1]
    past = n_pages * PAGE_SIZE

    hp = x_prompt.reshape(nb * seq, D_MODEL)
    hs = x_sample.reshape(db, D_MODEL)
    cos_p, sin_p = _rope_tables(jnp.tile(jnp.arange(seq), nb))
    cos_s, sin_s = _rope_tables(jnp.full((db,), past))

    pool_kt = jnp.transpose(cache_k, (0, 1, 3, 4, 2))
    pool_vt = jnp.transpose(cache_v, (0, 1, 3, 4, 2))
    pool_kidxt = jnp.transpose(cache_kidx, (0, 1, 3, 2))

    outs_p = [[] for _ in range(6)]
    outs_s = [[] for _ in range(6)]
    for l in range(depth):
        wp = _pack_w_in(w_in[l])
        g1 = norm1[l].reshape(1, D_MODEL)
        g2 = norm2[l].reshape(1, D_MODEL)
        qg = jnp.tile(q_norm[l], 2).reshape(1, LANES)
        kg = jnp.tile(k_norm[l], 2).reshape(1, LANES)
        cw = ssm_conv_w[l]
        cbias = ssm_conv_b[l].reshape(1, SSM_CONV_DIM)
        dtb = _pad_lanes(dt_bias[l])
        alog = _pad_lanes(a_log[l])
        dsk = jnp.repeat(d_skip[l], SSM_HEAD_DIM).reshape(1, SSM_INNER)
        sn = ssm_norm[l].reshape(1, SSM_INNER)
        wpc, wpa, wps, wo = (w.astype(BF16) for w in (w_pc[l], w_pa[l], w_ps[l], w_o[l]))
        i = l // 2
        if l % 2 == 0:
            routed = False
            wr = jnp.zeros((D_MODEL, LANES), F32)
            wg, wu, wd = (w[i:i + 1].astype(BF16) for w in (w_gate_dense, w_up_dense, w_down_dense))
        else:
            routed = True
            wr = jnp.zeros((D_MODEL, LANES), F32).at[:, :N_EXPERTS].set(w_router[i])
            wg, wu, wd = (w[i].astype(BF16) for w in (w_gate_moe, w_up_moe, w_down_moe))

        proj = _inproj(hp, g1, wp)
        qt, kf, kb, vt, qit, kif, kib, wt = _prep(proj, cos_p, sin_p, qg, kg, True)
        attn = _dsa_prompt(qt, qit, wt, kb, vt, kib, nb, seq)
        pre_a, conv_st = _conva_prompt(proj, conv_w[l], nb, seq)
        y_ssd, h_last = _ssd_prompt(proj, cw, cbias, dtb, alog, dsk, nb, seq)
        hp = _merge(hp, pre_a, attn, y_ssd, proj, sn, b_gate[l], wpc, wpa, wps, wo)
        if routed and hp.shape[0] % MOE_TILE == 0:
            hp = _moe_grouped(hp, g2, wr, wg, wu, wd)
        else:
            hp = _ffn(hp, g2, wr, wg, wu, wd, routed)
        proj3 = proj.reshape(nb, seq, N_PROJ)
        outs_p[0].append(kf.reshape(nb, seq, N_KV_HEADS, HEAD_DIM))
        outs_p[1].append(proj3[:, :, V_OFF:V_OFF + LANES].reshape(nb, seq, N_KV_HEADS, HEAD_DIM))
        outs_p[2].append(kif[:, :IDX_DIM].reshape(nb, seq, IDX_DIM))
        outs_p[3].append(conv_st)
        outs_p[4].append(proj3[:, seq - 3:, XS_OFF:XS_OFF + SSM_CONV_DIM])
        outs_p[5].append(h_last.reshape(nb, SSM_HEADS, SSM_HEAD_DIM, SSM_STATE))

        proj = _inproj(hs, g1, wp)
        qp, kf, kb, vb, qip, kif, kib, wi = _prep(proj, cos_s, sin_s, qg, kg, False)
        wrows = jnp.broadcast_to(wi[:, :N_IDX_HEADS, None], (db, N_IDX_HEADS, PAGE_SIZE))
        scores = _decscore(page_table, qip.astype(F32).reshape(db, N_IDX_HEADS, LANES), wrows, pool_kidxt, l)
        sel, self_sel = _decselect(scores, qip, kib, wi)
        o = _decattn(page_table, qp.astype(F32).reshape(db, N_HEADS, LANES), sel, pool_kt, pool_vt,
                     kb.astype(F32).reshape(db, 1, LANES), vb.astype(F32).reshape(db, 1, LANES),
                     self_sel.reshape(db, 1, LANES), l)
        attn = _head_halves(o)
        pre_a, u, xs, bcv, dt, ea = _decmix(proj, state_conv[l], state_ssm_conv[l], conv_w[l], cw, cbias, dtb, alog)
        hpg = SSM_HEADS // SSM_GROUPS
        xs4 = xs.reshape(db, SSM_HEADS, SSM_HEAD_DIM, 1)
        bh = jnp.repeat(bcv[:, :SSM_GROUPS * SSM_STATE].reshape(db, SSM_GROUPS, 1, SSM_STATE), hpg, axis=1)
        ch = jnp.repeat(bcv[:, SSM_GROUPS * SSM_STATE:].reshape(db, SSM_GROUPS, 1, SSM_STATE), hpg, axis=1)
        y4, h_new = _decssd(state_ssm[l], dt[:, :SSM_HEADS, None, None], xs4, bh, ch, ea[:, :SSM_HEADS, None, None],
                            d_skip[l].reshape(SSM_HEADS, 1, 1))
        hs = _merge(hs, pre_a, attn, y4.reshape(db, SSM_INNER), proj, sn, b_gate[l], wpc, wpa, wps, wo)
        hs = _ffn(hs, g2, wr, wg, wu, wd, routed)
        outs_s[0].append(kf.reshape(db, 1, N_KV_HEADS, HEAD_DIM))
        outs_s[1].append(proj[:, V_OFF:V_OFF + LANES].reshape(db, 1, N_KV_HEADS, HEAD_DIM))
        outs_s[2].append(kif[:, :IDX_DIM].reshape(db, 1, IDX_DIM))
        outs_s[3].append(jnp.stack([state_conv[l][:, 1], u], axis=1))
        outs_s[4].append(jnp.concatenate(
            [state_ssm_conv[l][:, 1:], proj[:, None, XS_OFF:XS_OFF + SSM_CONV_DIM]], axis=1))
        outs_s[5].append(h_new)

    return (hp.reshape(nb, seq, D_MODEL), hs.reshape(db, 1, D_MODEL),
            *(jnp.stack(o) for o in outs_p), *(jnp.stack(o) for o in outs_s))
```

```python
import functools
import math

import jax
import jax.numpy as jnp
from jax import lax
from jax.experimental import pallas as pl
from jax.experimental.pallas import tpu as pltpu

F32 = jnp.float32
BF16 = jnp.bfloat16
I32 = jnp.int32

D_MODEL = 1024
D_CONV = 512
N_HEADS = 8
HEAD_DIM = 64
N_KV_HEADS = 2
N_IDX_HEADS = 4
IDX_DIM = 64
TOPK_MAX = 256
Q_BLOCK = 128
ROPE_THETA = 10000.0
SSM_INNER = 512
SSM_HEADS = 8
SSM_HEAD_DIM = 64
SSM_GROUPS = 2
SSM_STATE = 128
SSM_CHUNK = 128
SSM_CONV_DIM = 1024
PAGE_SIZE = 128
N_EXPERTS = 8
EPS = 1e-6

LANES = 128
SUBLANES = 8
VMEM_LIMIT = 52 * 1024 * 1024

G_OFF, CB_OFF, CC_OFF, CX_OFF, Q_OFF, Z_OFF, XS_OFF, BC_OFF = 0, 3072, 3584, 4096, 4608, 5120, 5632, 6144
K_OFF, V_OFF, QI_OFF, KW_OFF, DT_OFF = 6656, 6784, 6912, 7168, 7296
N_PROJ = 7680

INT_MIN = -2 ** 31
INT_MAX = 2 ** 31 - 1
KEY_NEG_INF = (-8388608) ^ 0x7FFFFFFF
NEG_BIG = -1e30
LOG2_E = 1.4426950408889634

NT_DIMS = (((1,), (1,)), ((), ()))


def _cparams(sem):
    return pltpu.CompilerParams(dimension_semantics=sem, vmem_limit_bytes=VMEM_LIMIT)


def _sigmoid(x):
    return 1.0 / (1.0 + jnp.exp(-x))


def _silu(x):
    return x * _sigmoid(x)


def _softplus(x):
    return jnp.maximum(x, 0.0) + jnp.log1p(jnp.exp(-jnp.abs(x)))


def _sortable(x):
    bits = pltpu.bitcast(x, I32)
    return bits ^ (lax.shift_right_arithmetic(bits, 31) & 0x7FFFFFFF)


def _rms(x, g):
    return x * lax.rsqrt(jnp.mean(x * x, axis=-1, keepdims=True) + EPS) * g


def _inproj_kernel(x_ref, g_ref, w_ref, o_ref, h_ref):
    @pl.when(pl.program_id(1) == 0)
    def _():
        h_ref[...] = _rms(x_ref[...], g_ref[...]).astype(BF16)

    o_ref[...] = jnp.dot(h_ref[...], w_ref[...], preferred_element_type=F32)


def _inproj(x, g, w):
    m = x.shape[0]
    tm = min(m, 1024)
    tn = 1536
    return pl.pallas_call(
        _inproj_kernel,
        grid=(m // tm, N_PROJ // tn),
        in_specs=[pl.BlockSpec((tm, D_MODEL), lambda i, j: (i, 0)),
                  pl.BlockSpec((1, D_MODEL), lambda i, j: (0, 0)),
                  pl.BlockSpec((D_MODEL, tn), lambda i, j: (0, j))],
        out_specs=pl.BlockSpec((tm, tn), lambda i, j: (i, j)),
        out_shape=jax.ShapeDtypeStruct((m, N_PROJ), F32),
        scratch_shapes=[pltpu.VMEM((tm, D_MODEL), BF16)],
        compiler_params=_cparams(("parallel", "arbitrary")),
        name="inproj",
    )(x, g, w)


V_T_ROWS = LANES + 16


def _prep_kernel(q_ref, k_ref, v_ref, qi_ref, kw_ref, cos_ref, sin_ref, qg_ref, kg_ref,
                 qp_ref, kf_ref, kb_ref, vb_ref, qip_ref, kif_ref, kib_ref, wo_ref, *, transposed):
    tm = cos_ref.shape[0]

    def put(ref, h, val):
        if transposed:
            vt = val.T.astype(BF16)
            for blk in range(tm // Q_BLOCK):
                ref[blk, h * LANES:(h + 1) * LANES, :] = vt[:, blk * Q_BLOCK:(blk + 1) * Q_BLOCK]
        else:
            ref[:, h * LANES:(h + 1) * LANES] = val.astype(BF16)

    cos = cos_ref[...]
    sin = sin_ref[...]
    lane = lax.broadcasted_iota(I32, (tm, LANES), 1)
    low_half = lane < HEAD_DIM
    first_rot = (lane % HEAD_DIM) < (HEAD_DIM // 2)
    r = lax.broadcasted_iota(I32, (LANES, LANES), 0) // HEAD_DIM
    c = lax.broadcasted_iota(I32, (LANES, LANES), 1) // HEAD_DIM
    seg = jnp.where(r == c, 1.0 / HEAD_DIM, 0.0).astype(BF16)

    def rope(x):
        fwd = pltpu.roll(x, LANES - HEAD_DIM // 2, 1)
        bwd = pltpu.roll(x, HEAD_DIM // 2, 1)
        return x * cos + jnp.where(first_rot, fwd, bwd) * sin

    def headnorm(x, g):
        s = x * x
        hi = s.astype(BF16)
        lo = (s - hi.astype(F32)).astype(BF16)
        ms = (jnp.dot(hi, seg, preferred_element_type=F32)
              + jnp.dot(lo, seg, preferred_element_type=F32))
        return x * lax.rsqrt(ms + EPS) * g

    q_scale = HEAD_DIM ** -0.5 * (LOG2_E if transposed else 1.0)
    qg = qg_ref[...]
    for s in range(N_HEADS // 2):
        slab = rope(headnorm(q_ref[:, s * LANES:(s + 1) * LANES], qg)) * q_scale
        swapped = pltpu.roll(slab, HEAD_DIM, 1)
        grp = (2 * s) // (N_HEADS // N_KV_HEADS)
        for hh in range(2):
            h = 2 * s + hh
            src = slab if hh == grp else swapped
            keep = low_half if grp == 0 else jnp.logical_not(low_half)
            put(qp_ref, h, jnp.where(keep, src, 0.0))

    k = rope(headnorm(k_ref[...], kg_ref[...]))
    kf_ref[...] = k
    kb_ref[...] = k.astype(BF16)
    if transposed:
        vb_ref[0:LANES, :] = v_ref[...].T.astype(BF16)
        vb_ref[LANES:V_T_ROWS, :] = jnp.ones((V_T_ROWS - LANES, tm), BF16)
    else:
        vb_ref[...] = v_ref[...].astype(BF16)

    for s in range(N_IDX_HEADS // 2):
        slab = rope(qi_ref[:, s * LANES:(s + 1) * LANES]) * (IDX_DIM ** -0.5)
        swapped = pltpu.roll(slab, HEAD_DIM, 1)
        for hh in range(2):
            h = 2 * s + hh
            src = slab if hh == 0 else swapped
            put(qip_ref, h, jnp.where(low_half, src, 0.0))

    kw = kw_ref[...]
    ki = rope(kw)
    kif_ref[...] = ki
    kib_ref[...] = jnp.where(low_half, ki, 0.0).astype(BF16)
    wi = pltpu.roll(kw, HEAD_DIM, 1) * (N_IDX_HEADS ** -0.5)
    if transposed:
        wt = wi.T[0:SUBLANES, :]
        for blk in range(tm // Q_BLOCK):
            wo_ref[blk] = wt[:, blk * Q_BLOCK:(blk + 1) * Q_BLOCK]
    else:
        wo_ref[...] = wi


def _prep(proj, cos, sin, qg, kg, transposed):
    m = proj.shape[0]
    tm = min(m, 512)
    row = lambda w, off: pl.BlockSpec((tm, w), lambda i: (i, off // w))
    full = lambda w: pl.BlockSpec((tm, w), lambda i: (i, 0))
    const = pl.BlockSpec((1, LANES), lambda i: (0, 0))
    if transposed:
        qblocks = tm // Q_BLOCK
        feat = lambda r: pl.BlockSpec((qblocks, r, Q_BLOCK), lambda i: (i, 0, 0))
        fshape = lambda r, dt: jax.ShapeDtypeStruct((m // Q_BLOCK, r, Q_BLOCK), dt)
        vspec = pl.BlockSpec((V_T_ROWS, tm), lambda i: (0, i))
        vshape = jax.ShapeDtypeStruct((V_T_ROWS, m), BF16)
    else:
        feat = full
        fshape = lambda r, dt: jax.ShapeDtypeStruct((m, r), dt)
        vspec = full(LANES)
        vshape = jax.ShapeDtypeStruct((m, LANES), BF16)
    return pl.pallas_call(
        functools.partial(_prep_kernel, transposed=transposed),
        grid=(m // tm,),
        in_specs=[row(512, Q_OFF), row(LANES, K_OFF), row(LANES, V_OFF), row(256, QI_OFF), row(LANES, KW_OFF),
                  full(LANES), full(LANES), const, const],
        out_specs=[feat(N_HEADS * LANES), full(LANES), full(LANES), vspec,
                   feat(N_IDX_HEADS * LANES), full(LANES), full(LANES), feat(SUBLANES if transposed else LANES)],
        out_shape=[fshape(N_HEADS * LANES, BF16),
                   jax.ShapeDtypeStruct((m, LANES), F32),
                   jax.ShapeDtypeStruct((m, LANES), BF16),
                   vshape,
                   fshape(N_IDX_HEADS * LANES, BF16),
                   jax.ShapeDtypeStruct((m, LANES), F32),
                   jax.ShapeDtypeStruct((m, LANES), BF16),
                   fshape(SUBLANES if transposed else LANES, F32)],
        compiler_params=_cparams(("parallel",)),
        name="prep",
    )(proj, proj, proj, proj, proj, cos, sin, qg, kg)


SELECT_FIXED_PASSES = 22


def _fold_rows(f):
    while f.shape[0] > SUBLANES:
        half = f.shape[0] // 2
        f = f[:half] + f[half:]
    return f


def _dsa_kernel(qt_ref, qit_ref, wt_ref, k_ref, vt_ref, ki_ref, o_ref,
                keys_ref, tpos_ref, x_ref, m_ref, kmax_ref, acc_ref, *, k_sel, kc, nbits):
    qb = Q_BLOCK
    n_keys = k_ref.shape[0]
    assert n_keys < 2 ** 22
    j = pl.program_id(1)
    nck = (j * qb + qb + kc - 1) // kc
    qpos = j * qb + lax.broadcasted_iota(I32, (1, qb), 1)
    sub_k = lax.broadcasted_iota(I32, (kc, qb), 0)
    pair = lambda ref, s: jnp.concatenate(
        [ref[(2 * s) * LANES:(2 * s + 1) * LANES, :], ref[(2 * s + 1) * LANES:(2 * s + 2) * LANES, :]], axis=1)
    qi2 = [pair(qit_ref, s) for s in range(N_IDX_HEADS // 2)]
    q2 = [pair(qt_ref, s) for s in range(N_HEADS // 2)]

    def score_body(c, carry):
        off = pl.multiple_of(c * kc, kc)
        kic = ki_ref[pl.ds(off, kc), :]
        acc = jnp.zeros((kc, qb), F32)
        for s in range(N_IDX_HEADS // 2):
            s2 = jnp.dot(kic, qi2[s], preferred_element_type=F32)
            for hh in range(2):
                h = 2 * s + hh
                acc = acc + jnp.maximum(s2[:, hh * qb:(hh + 1) * qb], 0.0) * wt_ref[h:h + 1, :]
        kpos = off + sub_k
        key = _sortable(jnp.where(kpos <= qpos, acc, -jnp.inf))
        key = jnp.where(key == 0, (n_keys - 1) - kpos, jnp.where(key == -1, -1 - kpos, key))
        keys_ref[pl.ds(off, kc), :] = key
        return carry

    lax.fori_loop(0, nck, score_body, 0)

    def count_where(pred, ref=keys_ref):
        def body(c, acc):
            off = pl.multiple_of(c * kc, kc)
            return acc + _fold_rows(jnp.where(pred(ref[pl.ds(off, kc), :], off), 1.0, 0.0))

        acc = lax.fori_loop(0, nck, body, jnp.zeros((SUBLANES, qb), F32))
        return jnp.sum(acc, axis=0, keepdims=True)

    def bit_body(i, carry):
        t, cnt_acc, cnt_rej = carry
        cand = t + lax.shift_left(jnp.int32(1), 31 - i)
        cnt = count_where(lambda kk, off: kk >= cand)
        ok = cnt >= k_sel
        return jnp.where(ok, cand, t), jnp.where(ok, cnt, cnt_acc), jnp.where(ok, cnt_rej, cnt)

    zero = jnp.zeros((1, qb), F32)
    state = lax.fori_loop(0, SELECT_FIXED_PASSES, bit_body, (jnp.full((1, qb), INT_MIN, I32), zero, zero))

    def settled(cnt_acc):
        row_done = jnp.logical_or(cnt_acc == k_sel, nck * kc < k_sel)
        return jnp.min(jnp.where(row_done, 1.0, 0.0)) > 0.0

    def tail_body(carry):
        i, _, st = carry
        st = bit_body(i, st)
        return i + 1, settled(st[1]), st

    _, _, (thr, cnt_ge, cnt_gt) = lax.while_loop(
        lambda c: jnp.logical_and(c[0] < 32, jnp.logical_not(c[1])), tail_body,
        (jnp.int32(SELECT_FIXED_PASSES), settled(state[1]), state))
    thr = jnp.maximum(thr, KEY_NEG_INF)
    finite_thr = thr > KEY_NEG_INF
    need = k_sel - cnt_gt
    tie = jnp.logical_and(cnt_ge > k_sel, finite_thr)
    x_default = jnp.where(finite_thr, INT_MAX, -1)
    x_ref[...] = jnp.broadcast_to(x_default, x_ref.shape)

    @pl.when(jnp.max(jnp.where(tie, 1.0, 0.0)) > 0.0)
    def _():
        def tie_pos_body(c, carry):
            off = pl.multiple_of(c * kc, kc)
            tpos_ref[pl.ds(off, kc), :] = jnp.where(keys_ref[pl.ds(off, kc), :] == thr, off + sub_k, INT_MAX)
            return carry

        lax.fori_loop(0, nck, tie_pos_body, 0)

        def xbody(i, x):
            cand = x + lax.shift_left(jnp.int32(1), nbits - 1 - i)
            cnt = count_where(lambda tp, off: tp < cand, tpos_ref)
            return jnp.where(cnt < need, cand, x)

        x = lax.fori_loop(0, nbits, xbody, jnp.zeros((1, qb), I32))
        x_ref[...] = jnp.broadcast_to(jnp.where(tie, x, x_default), x_ref.shape)

    xcut = x_ref[0:1, :]
    hpg = N_HEADS // N_KV_HEADS

    @pl.when(j == 0)
    def _():
        r = lax.broadcasted_iota(I32, (LANES, LANES), 0) // HEAD_DIM
        cc = lax.broadcasted_iota(I32, (LANES, LANES), 1) // HEAD_DIM
        seg = jnp.where(r == cc, 1.0, 0.0).astype(BF16)

        def body(c, mx):
            kf = k_ref[pl.ds(pl.multiple_of(c * kc, kc), kc), :].astype(F32)
            gs = jnp.dot((kf * kf).astype(BF16), seg, preferred_element_type=F32)
            while gs.shape[0] > SUBLANES:
                half = gs.shape[0] // 2
                gs = jnp.maximum(gs[:half], gs[half:])
            return jnp.maximum(mx, gs)

        mx = lax.fori_loop(0, k_ref.shape[0] // kc, body, jnp.zeros((SUBLANES, LANES), F32))
        mx = jnp.max(mx, axis=0, keepdims=True)
        lane = lax.broadcasted_iota(I32, (1, LANES), 1)
        other = pltpu.roll(mx, HEAD_DIM, 1)
        kmax_ref[0:1, :] = jnp.where(lane < HEAD_DIM, mx, other)
        kmax_ref[1:2, :] = jnp.where(lane < HEAD_DIM, other, mx)

    def chunk(c):
        off = pl.multiple_of(c * kc, kc)
        kk = keys_ref[pl.ds(off, kc), :]
        sel = jnp.logical_or(kk > thr, jnp.logical_and(kk == thr, off + sub_k <= xcut))
        return sel, k_ref[pl.ds(off, kc), :], vt_ref[:, pl.ds(off, kc)]

    bound = []
    for h in range(N_HEADS):
        qh = qt_ref[h * LANES:(h + 1) * LANES, :].astype(F32)
        qsq = jnp.sum(qh * qh, axis=0, keepdims=True)
        bound.append(jnp.sqrt(qsq * kmax_ref[h // hpg:h // hpg + 1, :]))
    acc_ref[...] = jnp.zeros(acc_ref.shape, F32)

    def fast_body(c, carry):
        sel, kch, vch = chunk(c)
        n_pair = N_HEADS // 2
        s2 = jnp.dot(kch, q2[0], preferred_element_type=F32)
        for s in range(n_pair):
            s2_next = jnp.dot(kch, q2[s + 1], preferred_element_type=F32) if s + 1 < n_pair else None
            ps = [jnp.exp2(jnp.where(sel, s2[:, hh * qb:(hh + 1) * qb] - bound[2 * s + hh], -jnp.inf)).astype(BF16)
                  for hh in range(2)]
            acc_ref[s] += jnp.dot(vch, jnp.concatenate(ps, axis=1), preferred_element_type=F32)
            s2 = s2_next
        return carry

    lax.fori_loop(0, nck, fast_body, 0)

    dens = jnp.concatenate([acc_ref[s, LANES:LANES + 1, :] for s in range(N_HEADS // 2)], axis=0)

    @pl.when(jnp.logical_not(jnp.min(dens) >= 1e-20))
    def _():
        m_ref[...] = jnp.full(m_ref.shape, NEG_BIG, F32)
        acc_ref[...] = jnp.zeros(acc_ref.shape, F32)

        def exact_body(c, carry):
            sel, kch, vch = chunk(c)
            for s in range(N_HEADS // 2):
                s2 = jnp.dot(kch, q2[s], preferred_element_type=F32)
                ps, alphas = [], []
                for hh in range(2):
                    h = 2 * s + hh
                    sh = jnp.where(sel, s2[:, hh * qb:(hh + 1) * qb], -jnp.inf)
                    m_old = m_ref[h:h + 1, :]
                    m_new = jnp.maximum(m_old, jnp.max(sh, axis=0, keepdims=True))
                    m_ref[h:h + 1, :] = m_new
                    alphas.append(jnp.exp2(m_old - m_new))
                    ps.append(jnp.exp2(sh - m_new).astype(BF16))
                o2 = jnp.dot(vch, jnp.concatenate(ps, axis=1), preferred_element_type=F32)
                acc_ref[s] = jnp.concatenate(alphas, axis=1) * acc_ref[s] + o2
            return carry

        lax.fori_loop(0, nck, exact_body, 0)

    for s in range(N_HEADS // 2):
        g = (2 * s) // hpg
        num = acc_ref[s, g * HEAD_DIM:(g + 1) * HEAD_DIM, :]
        out2 = num / acc_ref[s, LANES:LANES + 1, :]
        o_ref[:, s * LANES:(s + 1) * LANES] = jnp.concatenate([out2[:, :qb], out2[:, qb:]], axis=0).T


def _dsa_prompt(qt, qit, wt, kb, vt, kib, nb, seq):
    k_sel = min(TOPK_MAX, seq // 4)
    kc = next(c for c in (512, Q_BLOCK) if seq % c == 0)
    nbits = max(1, (seq - 1).bit_length())
    nq = seq // Q_BLOCK
    r3 = lambda a: a.reshape(nb, seq, a.shape[-1])
    qcol = lambda r: pl.BlockSpec((None, r, Q_BLOCK), lambda b, j: (b * nq + j, 0, 0))
    seqblk = pl.BlockSpec((None, seq, LANES), lambda b, j: (b, 0, 0))
    out = pl.pallas_call(
        functools.partial(_dsa_kernel, k_sel=k_sel, kc=kc, nbits=nbits),
        grid=(nb, nq),
        in_specs=[qcol(N_HEADS * LANES), qcol(N_IDX_HEADS * LANES), qcol(SUBLANES), seqblk,
                  pl.BlockSpec((V_T_ROWS, seq), lambda b, j: (0, b)), seqblk],
        out_specs=pl.BlockSpec((None, Q_BLOCK, N_HEADS * HEAD_DIM), lambda b, j: (b, j, 0)),
        out_shape=jax.ShapeDtypeStruct((nb, seq, N_HEADS * HEAD_DIM), F32),
        scratch_shapes=[pltpu.VMEM((seq, Q_BLOCK), I32),
                        pltpu.VMEM((seq, Q_BLOCK), I32),
                        pltpu.VMEM((SUBLANES, Q_BLOCK), I32),
                        pltpu.VMEM((N_HEADS, Q_BLOCK), F32),
                        pltpu.VMEM((SUBLANES, LANES), F32),
                        pltpu.VMEM((N_HEADS // 2, V_T_ROWS, 2 * Q_BLOCK), F32)],
        compiler_params=_cparams(("arbitrary", "arbitrary")),
        name="dsa_prompt",
    )(qt, qit, wt, r3(kb), vt, r3(kib))
    return out.reshape(nb * seq, N_HEADS * HEAD_DIM)


def _conva_kernel(cb_ref, cc_ref, cx_ref, cch_ref, cxh_ref, w_ref, o_ref, st_ref, *, tiles_per_seq):
    tm = cb_ref.shape[0]
    first = (pl.program_id(0) % tiles_per_seq) == 0
    u = cc_ref[...] * cx_ref[...]
    uh = jnp.where(first, 0.0, cch_ref[...] * cxh_ref[...])
    ext = jnp.concatenate([uh, u], axis=0)
    conv = (w_ref[2:3, :] * u + w_ref[1:2, :] * ext[SUBLANES - 1:SUBLANES - 1 + tm]
            + w_ref[0:1, :] * ext[SUBLANES - 2:SUBLANES - 2 + tm])
    o_ref[...] = cb_ref[...] * conv
    st_ref[...] = u[tm - 2:tm, :]


def _conva_prompt(proj, conv_w, nb, seq):
    m = proj.shape[0]
    tm = min(seq, 512)
    tps = seq // tm
    col = lambda off: pl.BlockSpec((tm, D_CONV), lambda i: (i, off // D_CONV))
    halo = lambda off: pl.BlockSpec(
        (SUBLANES, D_CONV), lambda i: (jnp.maximum(i * (tm // SUBLANES) - 1, 0), off // D_CONV))
    return pl.pallas_call(
        functools.partial(_conva_kernel, tiles_per_seq=tps),
        grid=(m // tm,),
        in_specs=[col(CB_OFF), col(CC_OFF), col(CX_OFF), halo(CC_OFF), halo(CX_OFF),
                  pl.BlockSpec((3, D_CONV), lambda i: (0, 0))],
        out_specs=[pl.BlockSpec((tm, D_CONV), lambda i: (i, 0)),
                   pl.BlockSpec((None, 2, D_CONV), lambda i: (i // tps, 0, 0))],
        out_shape=[jax.ShapeDtypeStruct((m, D_CONV), F32),
                   jax.ShapeDtypeStruct((nb, 2, D_CONV), F32)],
        compiler_params=_cparams(("arbitrary",)),
        name="conva_prompt",
    )(proj, proj, proj, proj, proj, conv_w)


def _ssd_kernel(xs_ref, bc_ref, dt_ref, xh_ref, bh_ref, cwx_ref, cwb_ref, cbx_ref, cbb_ref,
                dtb_ref, alog_ref, dsk_ref, y_ref, hout_ref, h_ref, *, nchunk):
    cl = SSM_CHUNK
    c = pl.program_id(1)
    first = c == 0

    @pl.when(first)
    def _():
        h_ref[...] = jnp.zeros(h_ref.shape, F32)

    def conv(cur, halo, w_ref, b_ref):
        ext = jnp.concatenate([jnp.where(first, 0.0, halo), cur], axis=0)
        out = (w_ref[3:4, :] * cur + w_ref[2:3, :] * ext[SUBLANES - 1:SUBLANES - 1 + cl]
               + w_ref[1:2, :] * ext[SUBLANES - 2:SUBLANES - 2 + cl]
               + w_ref[0:1, :] * ext[SUBLANES - 3:SUBLANES - 3 + cl] + b_ref[...])
        return _silu(out)

    xs = conv(xs_ref[...], xh_ref[...], cwx_ref, cbx_ref)
    bc = conv(bc_ref[...], bh_ref[...], cwb_ref, cbb_ref)
    dt = _softplus(dt_ref[...] + dtb_ref[...])
    a = dt * (-jnp.exp(alog_ref[...]))
    ri = lax.broadcasted_iota(I32, (cl, cl), 0)
    ci = lax.broadcasted_iota(I32, (cl, cl), 1)
    causal = ri >= ci
    cs = jnp.dot(jnp.where(causal, 1.0, 0.0), a, preferred_element_type=F32,
                 precision=lax.Precision.HIGHEST)
    cs_t = cs.T
    lane = lax.broadcasted_iota(I32, (cl, LANES), 1)
    lo = lane < SSM_HEAD_DIM
    rows_lo = lax.broadcasted_iota(I32, (LANES, 1), 0) < SSM_HEAD_DIM
    heads_per_group = SSM_HEADS // SSM_GROUPS

    cb = []
    for g in range(SSM_GROUPS):
        bg = bc[:, g * SSM_STATE:(g + 1) * SSM_STATE].astype(BF16)
        cg = bc[:, (SSM_GROUPS + g) * SSM_STATE:(SSM_GROUPS + g + 1) * SSM_STATE].astype(BF16)
        cb.append((bg, cg, lax.dot_general(cg, bg, NT_DIMS, preferred_element_type=F32)))

    for s in range(SSM_HEADS // 2):
        h0, h1 = 2 * s, 2 * s + 1
        bg, cg, cbg = cb[h0 // heads_per_group]
        sl = slice(s * LANES, (s + 1) * LANES)
        xs_s = xs[:, sl]
        col0, col1 = cs[:, h0:h0 + 1], cs[:, h1:h1 + 1]
        last0, last1 = cs[cl - 1:cl, h0:h0 + 1], cs[cl - 1:cl, h1:h1 + 1]
        xdt = xs_s * jnp.where(lo, dt[:, h0:h0 + 1], dt[:, h1:h1 + 1])
        xdt_b = xdt.astype(BF16)
        m0 = (cbg * jnp.where(causal, jnp.exp(col0 - cs_t[h0:h0 + 1, :]), 0.0)).astype(BF16)
        m1 = (cbg * jnp.where(causal, jnp.exp(col1 - cs_t[h1:h1 + 1, :]), 0.0)).astype(BF16)
        y_diag = jnp.where(lo, jnp.dot(m0, xdt_b, preferred_element_type=F32),
                           jnp.dot(m1, xdt_b, preferred_element_type=F32))
        hs = h_ref[sl, :]
        y_off = lax.dot_general(cg, hs.astype(BF16), NT_DIMS, preferred_element_type=F32)
        y_off = y_off * jnp.where(lo, jnp.exp(col0), jnp.exp(col1))
        y_ref[:, sl] = y_diag + y_off + dsk_ref[:, sl] * xs_s
        xw = xdt * jnp.where(lo, jnp.exp(last0 - col0), jnp.exp(last1 - col1))
        st = jnp.dot(xw.T.astype(BF16), bg, preferred_element_type=F32)
        h_ref[sl, :] = hs * jnp.where(rows_lo, jnp.exp(last0), jnp.exp(last1)) + st

    @pl.when(c == nchunk - 1)
    def _():
        hout_ref[...] = h_ref[...]


def _ssd_prompt(proj, cw, cbias, dtb, alog, dsk, nb, seq):
    m = proj.shape[0]
    cl = SSM_CHUNK
    nchunk = seq // cl
    blk = lambda w, off: pl.BlockSpec((cl, w), lambda b, c: (b * nchunk + c, off // w))
    halo = lambda off: pl.BlockSpec(
        (SUBLANES, 512), lambda b, c: (jnp.maximum((b * nchunk + c) * (cl // SUBLANES) - 1, 0), off // 512))
    const = lambda r, w, j: pl.BlockSpec((r, w), lambda b, c: (0, j))
    y, hout = pl.pallas_call(
        functools.partial(_ssd_kernel, nchunk=nchunk),
        grid=(nb, nchunk),
        in_specs=[blk(512, XS_OFF), blk(512, BC_OFF), blk(LANES, DT_OFF), halo(XS_OFF), halo(BC_OFF),
                  const(4, 512, 0), const(4, 512, 1), const(1, 512, 0), const(1, 512, 1),
                  const(1, LANES, 0), const(1, LANES, 0), const(1, 512, 0)],
        out_specs=[pl.BlockSpec((cl, SSM_INNER), lambda b, c: (b * nchunk + c, 0)),
                   pl.BlockSpec((None, SSM_INNER, SSM_STATE), lambda b, c: (b, 0, 0))],
        out_shape=[jax.ShapeDtypeStruct((m, SSM_INNER), F32),
                   jax.ShapeDtypeStruct((nb, SSM_INNER, SSM_STATE), F32)],
        scratch_shapes=[pltpu.VMEM((SSM_INNER, SSM_STATE), F32)],
        compiler_params=_cparams(("parallel", "arbitrary")),
        name="ssd_prompt",
    )(proj, proj, proj, proj, proj, cw, cw, cbias, cbias, dtb, alog, dsk)
    return y, hout


def _merge_kernel(x_ref, pa_ref, at_ref, ys_ref, z_ref, g0_ref, g1_ref, g2_ref, sn_ref, bg_ref,
                  wpc_ref, wpa_ref, wps_ref, wo_ref, o_ref):
    ssd = _rms(ys_ref[...] * _silu(z_ref[...]), sn_ref[...])
    ya = jnp.dot(pa_ref[...].astype(BF16), wpc_ref[...], preferred_element_type=F32)
    yb = jnp.dot(at_ref[...].astype(BF16), wpa_ref[...], preferred_element_type=F32)
    yc = jnp.dot(ssd.astype(BF16), wps_ref[...], preferred_element_type=F32)
    merged = (_sigmoid(g0_ref[...] + bg_ref[0:1, :]) * ya + _sigmoid(g1_ref[...] + bg_ref[1:2, :]) * yb
              + _sigmoid(g2_ref[...] + bg_ref[2:3, :]) * yc)
    o_ref[...] = x_ref[...] + jnp.dot(merged.astype(BF16), wo_ref[...], preferred_element_type=F32)


def _merge(x, pre_a, attn, y_ssd, proj, ssm_norm, b_gate, w_pc, w_pa, w_ps, w_o):
    m = x.shape[0]
    tm = min(m, 256)
    row = lambda w: pl.BlockSpec((tm, w), lambda i: (i, 0))
    pcol = lambda w, off: pl.BlockSpec((tm, w), lambda i: (i, off // w))
    const = lambda r, w: pl.BlockSpec((r, w), lambda i: (0, 0))
    return pl.pallas_call(
        _merge_kernel,
        grid=(m // tm,),
        in_specs=[row(D_MODEL), row(512), row(512), row(512), pcol(512, Z_OFF),
                  pcol(D_MODEL, G_OFF), pcol(D_MODEL, G_OFF + D_MODEL), pcol(D_MODEL, G_OFF + 2 * D_MODEL),
                  const(1, 512), const(3, D_MODEL),
                  const(512, D_MODEL), const(512, D_MODEL), const(512, D_MODEL), const(D_MODEL, D_MODEL)],
        out_specs=row(D_MODEL),
        out_shape=jax.ShapeDtypeStruct((m, D_MODEL), F32),
        compiler_params=_cparams(("parallel",)),
        name="merge",
    )(x, pre_a, attn, y_ssd, proj, proj, proj, proj, ssm_norm, b_gate, w_pc, w_pa, w_ps, w_o)


def _ffn_kernel(x_ref, g_ref, wr_ref, wg_ref, wu_ref, wd_ref, o_ref, h_ref, acc_ref, comb_ref,
                *, routed, n_e, n_f):
    e = pl.program_id(1)
    j = pl.program_id(2)
    tm = x_ref.shape[0]

    @pl.when(jnp.logical_and(e == 0, j == 0))
    def _():
        hf = _rms(x_ref[...], g_ref[...])
        h_ref[...] = hf.astype(BF16)
        acc_ref[...] = jnp.zeros(acc_ref.shape, F32)
        if routed:
            lane = lax.broadcasted_iota(I32, (tm, LANES), 1).astype(F32)
            logits = jnp.dot(hf, wr_ref[...], preferred_element_type=F32, precision=lax.Precision.HIGHEST)
            logits = jnp.where(lane < n_e, logits, -jnp.inf)
            m1 = jnp.max(logits, axis=1, keepdims=True)
            i1 = jnp.min(jnp.where(logits == m1, lane, float(LANES)), axis=1, keepdims=True)
            rest = jnp.where(lane == i1, -jnp.inf, logits)
            m2 = jnp.max(rest, axis=1, keepdims=True)
            i2 = jnp.min(jnp.where(rest == m2, lane, float(LANES)), axis=1, keepdims=True)
            e2 = jnp.exp(m2 - m1)
            den = 1.0 + e2
            comb_ref[...] = jnp.where(lane == i1, 1.0 / den, 0.0) + jnp.where(lane == i2, e2 / den, 0.0)

    h = h_ref[...]
    gate = jnp.dot(h, wg_ref[...], preferred_element_type=F32)
    up = jnp.dot(h, wu_ref[...], preferred_element_type=F32)
    act = _silu(gate) * up
    if routed:
        lane = lax.broadcasted_iota(I32, (tm, LANES), 1)
        act = act * jnp.sum(jnp.where(lane == e, comb_ref[...], 0.0), axis=1, keepdims=True)
    acc_ref[...] += jnp.dot(act.astype(BF16), wd_ref[...], preferred_element_type=F32)

    @pl.when(jnp.logical_and(e == n_e - 1, j == n_f - 1))
    def _():
        o_ref[...] = x_ref[...] + acc_ref[...]


def _ffn(x, g, w_router, w_gate, w_up, w_down, routed):
    m = x.shape[0]
    n_e, _, d_ff = w_gate.shape
    tm = min(m, 1024)
    tf = 256
    n_f = d_ff // tf
    return pl.pallas_call(
        functools.partial(_ffn_kernel, routed=routed, n_e=n_e, n_f=n_f),
        grid=(m // tm, n_e, n_f),
        in_specs=[pl.BlockSpec((tm, D_MODEL), lambda i, e, j: (i, 0)),
                  pl.BlockSpec((1, D_MODEL), lambda i, e, j: (0, 0)),
                  pl.BlockSpec((D_MODEL, LANES), lambda i, e, j: (0, 0)),
                  pl.BlockSpec((None, D_MODEL, tf), lambda i, e, j: (e, 0, j)),
                  pl.BlockSpec((None, D_MODEL, tf), lambda i, e, j: (e, 0, j)),
                  pl.BlockSpec((None, tf, D_MODEL), lambda i, e, j: (e, j, 0))],
        out_specs=pl.BlockSpec((tm, D_MODEL), lambda i, e, j: (i, 0)),
        out_shape=jax.ShapeDtypeStruct((m, D_MODEL), F32),
        scratch_shapes=[pltpu.VMEM((tm, D_MODEL), BF16), pltpu.VMEM((tm, D_MODEL), F32),
                        pltpu.VMEM((tm, LANES), F32)],
        compiler_params=_cparams(("parallel", "arbitrary", "arbitrary")),
        name="moe" if routed else "ffn",
    )(x, g, w_router, w_gate, w_up, w_down)


MOE_BLOCK = 2048
MOE_TILE = 256
TOP_K = 2


def _moe_tiles(tb):
    return TOP_K * tb // MOE_TILE + N_EXPERTS


def _route_kernel(x_ref, g_ref, wr_ref, hb_ref, row_ref, col_ref, meta_ref):
    tb = x_ref.shape[0]
    hf = _rms(x_ref[...], g_ref[...])
    hb_ref[...] = hf.astype(BF16)
    lane_i = lax.broadcasted_iota(I32, (tb, LANES), 1)
    lane = lane_i.astype(F32)
    logits = jnp.dot(hf, wr_ref[...], preferred_element_type=F32, precision=lax.Precision.HIGHEST)
    logits = jnp.where(lane_i < N_EXPERTS, logits, -jnp.inf)
    m1 = jnp.max(logits, axis=1, keepdims=True)
    i1 = jnp.min(jnp.where(logits == m1, lane, float(LANES)), axis=1, keepdims=True)
    rest = jnp.where(lane == i1, -jnp.inf, logits)
    m2 = jnp.max(rest, axis=1, keepdims=True)
    i2 = jnp.min(jnp.where(rest == m2, lane, float(LANES)), axis=1, keepdims=True)
    e2 = jnp.exp(m2 - m1)
    p1 = 1.0 / (1.0 + e2)
    p2 = e2 / (1.0 + e2)
    oh1 = lane == i1
    oh2 = lane == i2
    both = jnp.where(jnp.logical_or(oh1, oh2), 1.0, 0.0).astype(BF16)

    sb = MOE_TILE
    ri = lax.broadcasted_iota(I32, (sb, sb), 0)
    ci = lax.broadcasted_iota(I32, (sb, sb), 1)
    strict = jnp.where(ri > ci, 1.0, 0.0).astype(BF16)
    carry = jnp.zeros((1, LANES), F32)
    ranks = []
    for s in range(tb // sb):
        rows = both[s * sb:(s + 1) * sb]
        ranks.append(jnp.dot(strict, rows, preferred_element_type=F32) + carry)
        carry = carry + jnp.sum(rows.astype(F32), axis=0, keepdims=True)
    rank = jnp.concatenate(ranks, axis=0)
    seg = jnp.floor((carry + (MOE_TILE - 1)) * (1.0 / MOE_TILE)) * MOE_TILE
    ui = lax.broadcasted_iota(I32, (LANES, LANES), 0)
    uj = lax.broadcasted_iota(I32, (LANES, LANES), 1)
    before = jnp.where(ui < uj, 1.0, 0.0)
    off = jnp.dot(jnp.broadcast_to(seg, (SUBLANES, LANES)), before, preferred_element_type=F32,
                  precision=lax.Precision.HIGHEST)[0:1, :]
    dest = off + rank
    d1 = jnp.sum(jnp.where(oh1, dest, 0.0), axis=1, keepdims=True)
    d2 = jnp.sum(jnp.where(oh2, dest, 0.0), axis=1, keepdims=True)
    col = jnp.where(lane_i == 0, d1, jnp.where(lane_i == 1, d2, jnp.where(lane_i == 2, p1,
                                                                          jnp.where(lane_i == 3, p2, 0.0))))
    col_ref[...] = col
    row_ref[...] = col.T[0:SUBLANES, :]

    end_rows = jnp.broadcast_to(off + seg, (LANES, LANES)).T
    start = (lax.broadcasted_iota(I32, (LANES, LANES), 1) * MOE_TILE).astype(F32)
    is_expert = lax.broadcasted_iota(I32, (LANES, LANES), 0) < N_EXPERTS
    done = jnp.where(jnp.logical_and(is_expert, end_rows <= start), 1.0, 0.0)
    expert = jnp.minimum(jnp.sum(done, axis=0, keepdims=True), float(N_EXPERTS - 1))
    total = jnp.sum(jnp.where(lane_i[0:1] < N_EXPERTS, seg, 0.0), axis=1, keepdims=True)
    valid = jnp.where(start[0:1] < total, 1.0, 0.0)
    sub = lax.broadcasted_iota(I32, (SUBLANES, LANES), 0)
    meta_ref[...] = jnp.where(sub == 0, expert, jnp.where(sub == 1, valid, 0.0)).astype(I32)


def _route(x, g, w_router, tb):
    m = x.shape[0]
    nblk = m // tb
    return pl.pallas_call(
        _route_kernel,
        grid=(nblk,),
        in_specs=[pl.BlockSpec((tb, D_MODEL), lambda b: (b, 0)),
                  pl.BlockSpec((1, D_MODEL), lambda b: (0, 0)),
                  pl.BlockSpec((D_MODEL, LANES), lambda b: (0, 0))],
        out_specs=[pl.BlockSpec((tb, D_MODEL), lambda b: (b, 0)),
                   pl.BlockSpec((SUBLANES, tb), lambda b: (0, b)),
                   pl.BlockSpec((tb, LANES), lambda b: (b, 0)),
                   pl.BlockSpec((None, SUBLANES, LANES), lambda b: (b, 0, 0))],
        out_shape=[jax.ShapeDtypeStruct((m, D_MODEL), BF16),
                   jax.ShapeDtypeStruct((SUBLANES, m), F32),
                   jax.ShapeDtypeStruct((m, LANES), F32),
                   jax.ShapeDtypeStruct((nblk, SUBLANES, LANES), I32)],
        compiler_params=_cparams(("parallel",)),
        name="moe_route",
    )(x, g, w_router)


def _gffn_kernel(te_ref, tv_ref, hb_ref, row_ref, wg_ref, wu_ref, wd_ref, ys_ref, xg_ref, acc_ref, *, nt, n_f):
    del te_ref
    b, t, j = pl.program_id(0), pl.program_id(1), pl.program_id(2)
    valid = tv_ref[b * nt + t] > 0
    tb = hb_ref.shape[0]

    @pl.when(jnp.logical_and(valid, j == 0))
    def _():
        slot = (t * MOE_TILE + lax.broadcasted_iota(I32, (MOE_TILE, tb), 0)).astype(F32)
        hit = jnp.logical_or(row_ref[0:1, :] == slot, row_ref[1:2, :] == slot)
        onehot = jnp.where(hit, 1.0, 0.0).astype(BF16)
        xg_ref[...] = jnp.dot(onehot, hb_ref[...], preferred_element_type=F32).astype(BF16)
        acc_ref[...] = jnp.zeros(acc_ref.shape, F32)

    @pl.when(valid)
    def _():
        xg = xg_ref[...]
        gate = jnp.dot(xg, wg_ref[...], preferred_element_type=F32)
        up = jnp.dot(xg, wu_ref[...], preferred_element_type=F32)
        acc_ref[...] += jnp.dot((_silu(gate) * up).astype(BF16), wd_ref[...], preferred_element_type=F32)

    @pl.when(j == n_f - 1)
    def _():
        ys_ref[...] = jnp.where(valid, acc_ref[...], 0.0).astype(BF16)


def _gffn(tile_expert, tile_valid, hb, rowinfo, w_gate, w_up, w_down, tb):
    m = hb.shape[0]
    nblk = m // tb
    nt = _moe_tiles(tb)
    d_ff = w_gate.shape[2]
    tf = d_ff // 2
    n_f = d_ff // tf
    def fidx(b, t, j, te, tv):
        odd = t % 2 == 1
        return jnp.where(tv[b * nt + t] > 0, jnp.where(odd, n_f - 1 - j, j), jnp.where(odd, n_f - 1, 0))

    grid_spec = pltpu.PrefetchScalarGridSpec(
        num_scalar_prefetch=2,
        grid=(nblk, nt, n_f),
        in_specs=[pl.BlockSpec((tb, D_MODEL), lambda b, t, j, te, tv: (b, 0)),
                  pl.BlockSpec((SUBLANES, tb), lambda b, t, j, te, tv: (0, b)),
                  pl.BlockSpec((None, D_MODEL, tf), lambda b, t, j, te, tv: (te[b * nt + t], 0, fidx(b, t, j, te, tv))),
                  pl.BlockSpec((None, D_MODEL, tf), lambda b, t, j, te, tv: (te[b * nt + t], 0, fidx(b, t, j, te, tv))),
                  pl.BlockSpec((None, tf, D_MODEL), lambda b, t, j, te, tv: (te[b * nt + t], fidx(b, t, j, te, tv), 0))],
        out_specs=pl.BlockSpec((MOE_TILE, D_MODEL), lambda b, t, j, te, tv: (b * nt + t, 0)),
        scratch_shapes=[pltpu.VMEM((MOE_TILE, D_MODEL), BF16), pltpu.VMEM((MOE_TILE, D_MODEL), F32)],
    )
    return pl.pallas_call(
        functools.partial(_gffn_kernel, nt=nt, n_f=n_f),
        grid_spec=grid_spec,
        out_shape=jax.ShapeDtypeStruct((nblk * nt * MOE_TILE, D_MODEL), BF16),
        compiler_params=_cparams(("parallel", "arbitrary", "arbitrary")),
        name="moe_experts",
    )(tile_expert, tile_valid, hb, rowinfo, w_gate, w_up, w_down)


def _combine_kernel(x_ref, col_ref, ys_ref, o_ref):
    rows, slots = x_ref.shape[0], ys_ref.shape[0]
    slot = lax.broadcasted_iota(I32, (rows, slots), 1).astype(F32)
    weight = (jnp.where(col_ref[:, 0:1] == slot, col_ref[:, 2:3], 0.0)
              + jnp.where(col_ref[:, 1:2] == slot, col_ref[:, 3:4], 0.0))
    o_ref[...] = x_ref[...] + jnp.dot(weight.astype(BF16), ys_ref[...], preferred_element_type=F32)


def _combine(x, colinfo, ys, tb):
    m = x.shape[0]
    slots = _moe_tiles(tb) * MOE_TILE
    rows = MOE_TILE
    sub = tb // rows
    return pl.pallas_call(
        _combine_kernel,
        grid=(m // tb, sub),
        in_specs=[pl.BlockSpec((rows, D_MODEL), lambda b, r: (b * sub + r, 0)),
                  pl.BlockSpec((rows, LANES), lambda b, r: (b * sub + r, 0)),
                  pl.BlockSpec((slots, D_MODEL), lambda b, r: (b, 0))],
        out_specs=pl.BlockSpec((rows, D_MODEL), lambda b, r: (b * sub + r, 0)),
        out_shape=jax.ShapeDtypeStruct((m, D_MODEL), F32),
        compiler_params=_cparams(("parallel", "arbitrary")),
        name="moe_combine",
    )(x, colinfo, ys)


def _moe_grouped(x, g, w_router, w_gate, w_up, w_down):
    m = x.shape[0]
    tb = min(MOE_BLOCK, m)
    nt = _moe_tiles(tb)
    hb, rowinfo, colinfo, meta = _route(x, g, w_router, tb)
    tile_expert = meta[:, 0, :nt].reshape(-1)
    tile_valid = meta[:, 1, :nt].reshape(-1)
    ys = _gffn(tile_expert, tile_valid, hb, rowinfo, w_gate, w_up, w_down, tb)
    return _combine(x, colinfo, ys, tb)


def _decmix_kernel(cb_ref, cc_ref, cx_ref, xs_ref, bc_ref, dt_ref, sc0_ref, sc1_ref,
                   sx0_ref, sx1_ref, sx2_ref, sb0_ref, sb1_ref, sb2_ref,
                   cw_ref, cwx_ref, cwb_ref, cbx_ref, cbb_ref, dtb_ref, alog_ref,
                   pa_ref, u_ref, xso_ref, bco_ref, dto_ref, ea_ref):
    u = cc_ref[...] * cx_ref[...]
    u_ref[...] = u
    pa_ref[...] = cb_ref[...] * (cw_ref[0:1, :] * sc0_ref[...] + cw_ref[1:2, :] * sc1_ref[...] + cw_ref[2:3, :] * u)
    xso_ref[...] = _silu(cwx_ref[0:1, :] * sx0_ref[...] + cwx_ref[1:2, :] * sx1_ref[...]
                         + cwx_ref[2:3, :] * sx2_ref[...] + cwx_ref[3:4, :] * xs_ref[...] + cbx_ref[...])
    bco_ref[...] = _silu(cwb_ref[0:1, :] * sb0_ref[...] + cwb_ref[1:2, :] * sb1_ref[...]
                         + cwb_ref[2:3, :] * sb2_ref[...] + cwb_ref[3:4, :] * bc_ref[...] + cbb_ref[...])
    dt = _softplus(dt_ref[...] + dtb_ref[...])
    dto_ref[...] = dt
    ea_ref[...] = jnp.exp(dt * (-jnp.exp(alog_ref[...])))


def _decmix(proj, st_conv, st_ssm_conv, conv_w, cw, cbias, dtb, alog):
    nb = proj.shape[0]
    pcol = lambda w, off: pl.BlockSpec((nb, w), lambda i: (0, off // w))
    full = lambda w: pl.BlockSpec((nb, w), lambda i: (0, 0))
    const = lambda r, w, j: pl.BlockSpec((r, w), lambda i: (0, j))
    sx = [st_ssm_conv[:, t, :512] for t in range(3)]
    sb = [st_ssm_conv[:, t, 512:] for t in range(3)]
    return pl.pallas_call(
        _decmix_kernel,
        grid=(1,),
        in_specs=[pcol(512, CB_OFF), pcol(512, CC_OFF), pcol(512, CX_OFF), pcol(512, XS_OFF), pcol(512, BC_OFF),
                  pcol(LANES, DT_OFF)] + [full(512)] * 8
                 + [const(3, 512, 0), const(4, 512, 0), const(4, 512, 1), const(1, 512, 0), const(1, 512, 1),
                    const(1, LANES, 0), const(1, LANES, 0)],
        out_specs=[full(512), full(512), full(512), full(512), full(LANES), full(LANES)],
        out_shape=[jax.ShapeDtypeStruct((nb, 512), F32)] * 4 + [jax.ShapeDtypeStruct((nb, LANES), F32)] * 2,
        compiler_params=_cparams(("arbitrary",)),
        name="decode_mix",
    )(proj, proj, proj, proj, proj, proj, st_conv[:, 0], st_conv[:, 1], *sx, *sb,
      conv_w, cw, cw, cbias, cbias, dtb, alog)


def _decssd_kernel(h_ref, dt_ref, xs_ref, b_ref, c_ref, ea_ref, dsk_ref, y_ref, ho_ref):
    xs = xs_ref[...]
    h_new = h_ref[...] * ea_ref[...] + (xs * dt_ref[...]) * b_ref[...]
    ho_ref[...] = h_new
    y_ref[...] = jnp.sum(h_new * c_ref[...], axis=-1, keepdims=True) + dsk_ref[...] * xs


def _decssd(h0, dt, xs, bh, ch, ea, dsk):
    nb = h0.shape[0]
    blk = lambda a, b: pl.BlockSpec((None, SSM_HEADS, a, b), lambda i: (i, 0, 0, 0))
    return pl.pallas_call(
        _decssd_kernel,
        grid=(nb,),
        in_specs=[blk(SSM_HEAD_DIM, SSM_STATE), blk(1, 1), blk(SSM_HEAD_DIM, 1), blk(1, SSM_STATE),
                  blk(1, SSM_STATE), blk(1, 1), pl.BlockSpec((SSM_HEADS, 1, 1), lambda i: (0, 0, 0))],
        out_specs=[blk(SSM_HEAD_DIM, 1), blk(SSM_HEAD_DIM, SSM_STATE)],
        out_shape=[jax.ShapeDtypeStruct((nb, SSM_HEADS, SSM_HEAD_DIM, 1), F32),
                   jax.ShapeDtypeStruct((nb, SSM_HEADS, SSM_HEAD_DIM, SSM_STATE), F32)],
        compiler_params=_cparams(("parallel",)),
        name="decode_ssd",
    )(h0, dt, xs, bh, ch, ea, dsk)


DECODE_PAGE_GROUP = 32


def _decscore_kernel(pt_ref, qi_ref, w_ref, *refs):
    del pt_ref
    kid_refs, o_ref = refs[:-1], refs[-1]
    kid = jnp.concatenate([r[...] for r in kid_refs], axis=1).astype(BF16)
    kid = jnp.concatenate([kid, jnp.zeros_like(kid)], axis=0)
    s = jnp.dot(qi_ref[...].astype(BF16), kid, preferred_element_type=F32)
    sc = jnp.maximum(s, 0.0) * w_ref[:, 0:1]
    o_ref[...] = jnp.sum(sc, axis=0, keepdims=True)


def _page_specs(block, layer, pg):
    zeros = (0,) * (len(block) - 2)
    return [pl.BlockSpec(block, lambda b, p, pt, t=t: (layer, pt[b, p * pg + t]) + zeros) for t in range(pg)]


def _decscore(page_table, qip, wrows, pool_kidx, layer):
    nb, n_pages = page_table.shape
    pg = math.gcd(DECODE_PAGE_GROUP, n_pages)
    grid_spec = pltpu.PrefetchScalarGridSpec(
        num_scalar_prefetch=1,
        grid=(nb, n_pages // pg),
        in_specs=[pl.BlockSpec((None, N_IDX_HEADS, LANES), lambda b, p, pt: (b, 0, 0)),
                  pl.BlockSpec((None, N_IDX_HEADS, PAGE_SIZE), lambda b, p, pt: (b, 0, 0))]
                 + _page_specs((None, None, IDX_DIM, PAGE_SIZE), layer, pg),
        out_specs=pl.BlockSpec((None, None, 1, pg * PAGE_SIZE), lambda b, p, pt: (b, p, 0, 0)),
    )
    out = pl.pallas_call(
        _decscore_kernel,
        grid_spec=grid_spec,
        out_shape=jax.ShapeDtypeStruct((nb, n_pages // pg, 1, pg * PAGE_SIZE), F32),
        compiler_params=_cparams(("parallel", "arbitrary")),
        name="decode_scores",
    )(page_table, qip, wrows, *([pool_kidx] * pg))
    return out.reshape(nb, n_pages * PAGE_SIZE)


def _decselect_kernel(sc_ref, qi_ref, ki_ref, w_ref, sel_ref, self_ref, keys_ref, *, k_sel, kc, nbits):
    nb, past = sc_ref.shape
    nck = past // kc
    lane_k = lax.broadcasted_iota(I32, (nb, kc), 1)
    ki = ki_ref[...].astype(F32)
    own = jnp.zeros((nb, 1), F32)
    for h in range(N_IDX_HEADS):
        s = jnp.sum(qi_ref[:, h * LANES:(h + 1) * LANES].astype(F32) * ki, axis=1, keepdims=True)
        own = own + jnp.maximum(s, 0.0) * w_ref[:, h:h + 1]
    own_key = _sortable(own)

    def key_body(c, carry):
        off = pl.multiple_of(c * kc, kc)
        keys_ref[:, pl.ds(off, kc)] = _sortable(sc_ref[:, pl.ds(off, kc)])
        return carry

    lax.fori_loop(0, nck, key_body, 0)

    def count_where(pred):
        def body(c, acc):
            off = pl.multiple_of(c * kc, kc)
            f = jnp.where(pred(keys_ref[:, pl.ds(off, kc)], off), 1.0, 0.0)
            part = f[:, 0:LANES]
            for t in range(1, kc // LANES):
                part = part + f[:, t * LANES:(t + 1) * LANES]
            return acc + part

        acc = lax.fori_loop(0, nck, body, jnp.zeros((nb, LANES), F32))
        return jnp.sum(acc, axis=1, keepdims=True)

    def bit_body(i, t):
        cand = t + lax.shift_left(jnp.int32(1), 31 - i)
        cnt = count_where(lambda kk, off: kk >= cand) + jnp.where(own_key >= cand, 1.0, 0.0)
        return jnp.where(cnt >= k_sel, cand, t)

    thr = lax.fori_loop(0, 32, bit_body, jnp.full((nb, 1), INT_MIN, I32))
    thr = jnp.maximum(thr, KEY_NEG_INF)
    finite_thr = thr > KEY_NEG_INF
    cnt_gt = count_where(lambda kk, off: kk > thr) + jnp.where(own_key > thr, 1.0, 0.0)
    need = k_sel - cnt_gt

    def xbody(i, x):
        cand = x + lax.shift_left(jnp.int32(1), nbits - 1 - i)
        cnt = count_where(lambda kk, off: jnp.logical_and(kk == thr, off + lane_k < cand))
        return jnp.where(cnt < need, cand, x)

    xcut = lax.fori_loop(0, nbits, xbody, jnp.zeros((nb, 1), I32))
    xcut = jnp.where(finite_thr, xcut, -1)
    ties_past = count_where(lambda kk, off: kk == thr)
    own_tie = jnp.logical_and(jnp.logical_and(own_key == thr, finite_thr), ties_past < need)
    self_ref[...] = jnp.broadcast_to(
        jnp.where(jnp.logical_or(own_key > thr, own_tie), 1.0, 0.0), (nb, LANES))

    def sel_body(c, carry):
        off = pl.multiple_of(c * kc, kc)
        kk = keys_ref[:, pl.ds(off, kc)]
        sel = jnp.logical_or(kk > thr, jnp.logical_and(kk == thr, off + lane_k <= xcut))
        sel_ref[:, pl.ds(off, kc)] = jnp.where(sel, 1.0, 0.0)
        return carry

    lax.fori_loop(0, nck, sel_body, 0)


def _decselect(scores, qip, kib, wi):
    nb, past = scores.shape
    k_sel = min(TOPK_MAX, (past + 1) // 4)
    kc = next(c for c in (4096, 2048, 1024, 512, PAGE_SIZE) if past % c == 0)
    nbits = max(1, past.bit_length())
    full = lambda w: pl.BlockSpec((nb, w), lambda i: (0, 0))
    return pl.pallas_call(
        functools.partial(_decselect_kernel, k_sel=k_sel, kc=kc, nbits=nbits),
        grid=(1,),
        in_specs=[full(past), full(N_IDX_HEADS * LANES), full(LANES), full(LANES)],
        out_specs=[full(past), full(LANES)],
        out_shape=[jax.ShapeDtypeStruct((nb, past), F32), jax.ShapeDtypeStruct((nb, LANES), F32)],
        scratch_shapes=[pltpu.VMEM((nb, past), I32)],
        compiler_params=_cparams(("arbitrary",)),
        name="decode_select",
    )(scores, qip, kib, wi)


def _decattn_kernel(pt_ref, q_ref, sel_ref, kn_ref, vn_ref, self_ref, *refs, n_steps):
    del pt_ref
    pg = (len(refs) - 4) // 2
    kp_refs, vp_refs = refs[:pg], refs[pg:2 * pg]
    o_ref, m_ref, l_ref, acc_ref = refs[2 * pg:]
    p = pl.program_id(1)

    @pl.when(p == 0)
    def _():
        m_ref[...] = jnp.full(m_ref.shape, NEG_BIG, F32)
        l_ref[...] = jnp.zeros(l_ref.shape, F32)
        acc_ref[...] = jnp.zeros(acc_ref.shape, F32)

    q = q_ref[...]
    slab = lambda refs_: jnp.concatenate(
        [jnp.concatenate([r[g] for r in refs_], axis=1) for g in range(N_KV_HEADS)], axis=0).astype(BF16)
    s = jnp.dot(q.astype(BF16), slab(kp_refs), preferred_element_type=F32)
    s = jnp.where(sel_ref[...] > 0.0, s, -jnp.inf)
    m_old = m_ref[:, 0:1]
    m_new = jnp.maximum(m_old, jnp.max(s, axis=1, keepdims=True))
    alpha = jnp.exp(m_old - m_new)
    pr = jnp.exp(s - m_new)
    l_new = alpha * l_ref[:, 0:1] + jnp.sum(pr, axis=1, keepdims=True)
    pv = lax.dot_general(pr.astype(BF16), slab(vp_refs), NT_DIMS, preferred_element_type=F32)
    acc_ref[...] = alpha * acc_ref[...] + pv
    m_ref[...] = jnp.broadcast_to(m_new, m_ref.shape)
    l_ref[...] = jnp.broadcast_to(l_new, l_ref.shape)

    @pl.when(p == n_steps - 1)
    def _():
        s_own = jnp.sum(q * kn_ref[...], axis=1, keepdims=True)
        s_own = jnp.where(self_ref[:, 0:1] > 0.0, s_own, -jnp.inf)
        m_o = m_ref[:, 0:1]
        m_n = jnp.maximum(m_o, s_own)
        al = jnp.exp(m_o - m_n)
        p_own = jnp.exp(s_own - m_n)
        l_n = al * l_ref[:, 0:1] + p_own
        acc = al * acc_ref[...] + p_own.astype(BF16).astype(F32) * vn_ref[...]
        o_ref[...] = acc / l_n


def _decattn(page_table, q, sel, pool_kt, pool_vt, kn, vn, self_sel, layer):
    nb, n_pages = page_table.shape
    pg = math.gcd(DECODE_PAGE_GROUP, n_pages)
    n_steps = n_pages // pg
    row = lambda r: pl.BlockSpec((None, r, LANES), lambda b, p, pt: (b, 0, 0))
    pages = _page_specs((None, None, N_KV_HEADS, HEAD_DIM, PAGE_SIZE), layer, pg)
    grid_spec = pltpu.PrefetchScalarGridSpec(
        num_scalar_prefetch=1,
        grid=(nb, n_steps),
        in_specs=[row(N_HEADS),
                  pl.BlockSpec((None, None, 1, pg * PAGE_SIZE), lambda b, p, pt: (b, p, 0, 0)),
                  row(1), row(1), row(1)] + pages + pages,
        out_specs=row(N_HEADS),
        scratch_shapes=[pltpu.VMEM((N_HEADS, LANES), F32)] * 3,
    )
    return pl.pallas_call(
        functools.partial(_decattn_kernel, n_steps=n_steps),
        grid_spec=grid_spec,
        out_shape=jax.ShapeDtypeStruct((nb, N_HEADS, LANES), F32),
        compiler_params=_cparams(("parallel", "arbitrary")),
        name="decode_attend",
    )(page_table, q, sel.reshape(nb, n_steps, 1, pg * PAGE_SIZE), kn, vn, self_sel,
      *([pool_kt] * pg), *([pool_vt] * pg))


def _pack_w_in(w):
    d = w.shape[0]
    pad = lambda n: jnp.zeros((d, n), w.dtype)
    cols = [w[:, 4172:7244],
            w[:, 0:2048],
            w[:, 2628:3140],
            w[:, 3140:4164],
            w[:, 2048:2304],
            w[:, 2304:2560],
            w[:, 2560:2628], pad(60),
            w[:, 4164:4172], pad(120),
            pad(N_PROJ - 7424)]
    return jnp.concatenate(cols, axis=1).astype(BF16)


def _pad_lanes(v):
    return jnp.zeros((1, LANES), F32).at[0, :v.shape[0]].set(v)


def _rope_tables(pos):
    half = HEAD_DIM // 2
    inv = ROPE_THETA ** (-jnp.arange(half, dtype=F32) / half)
    ang = pos.astype(F32)[:, None] * inv[None, :]
    cos, sin = jnp.cos(ang), jnp.sin(ang)
    cos2 = jnp.concatenate([cos, cos], axis=1)
    sin2 = jnp.concatenate([-sin, sin], axis=1)
    return jnp.tile(cos2, (1, 2)), jnp.tile(sin2, (1, 2))


def _head_halves(o):
    hpg = N_HEADS // N_KV_HEADS
    parts = [o[:, h, (h // hpg) * HEAD_DIM:(h // hpg + 1) * HEAD_DIM] for h in range(N_HEADS)]
    return jnp.concatenate(parts, axis=-1)


def kernel(x_prompt, x_sample, cache_k, cache_v, cache_kidx, state_conv, state_ssm_conv, state_ssm, page_table,
           norm1, w_in, b_gate, conv_w, q_norm, k_norm, w_pc, w_pa, ssm_conv_w, ssm_conv_b, dt_bias, a_log,
           d_skip, ssm_norm, w_ps, w_o, norm2, w_gate_dense, w_up_dense, w_down_dense, w_router, w_gate_moe,
           w_up_moe, w_down_moe):
    nb, seq, _ = x_prompt.shape
    db = x_sample.shape[0]
    depth = w_in.shape[0]
    n_pages = page_table.shape[1]
    past = n_pages * PAGE_SIZE

    hp = x_prompt.reshape(nb * seq, D_MODEL)
    hs = x_sample.reshape(db, D_MODEL)
    cos_p, sin_p = _rope_tables(jnp.tile(jnp.arange(seq), nb))
    cos_s, sin_s = _rope_tables(jnp.full((db,), past))

    pool_kt = jnp.transpose(cache_k, (0, 1, 3, 4, 2))
    pool_vt = jnp.transpose(cache_v, (0, 1, 3, 4, 2))
    pool_kidxt = jnp.transpose(cache_kidx, (0, 1, 3, 2))

    outs_p = [[] for _ in range(6)]
    outs_s = [[] for _ in range(6)]
    for l in range(depth):
        wp = _pack_w_in(w_in[l])
        g1 = norm1[l].reshape(1, D_MODEL)
        g2 = norm2[l].reshape(1, D_MODEL)
        qg = jnp.tile(q_norm[l], 2).reshape(1, LANES)
        kg = jnp.tile(k_norm[l], 2).reshape(1, LANES)
        cw = ssm_conv_w[l]
        cbias = ssm_conv_b[l].reshape(1, SSM_CONV_DIM)
        dtb = _pad_lanes(dt_bias[l])
        alog = _pad_lanes(a_log[l])
        dsk = jnp.repeat(d_skip[l], SSM_HEAD_DIM).reshape(1, SSM_INNER)
        sn = ssm_norm[l].reshape(1, SSM_INNER)
        wpc, wpa, wps, wo = (w.astype(BF16) for w in (w_pc[l], w_pa[l], w_ps[l], w_o[l]))
        i = l // 2
        if l % 2 == 0:
            routed = False
            wr = jnp.zeros((D_MODEL, LANES), F32)
            wg, wu, wd = (w[i:i + 1].astype(BF16) for w in (w_gate_dense, w_up_dense, w_down_dense))
        else:
            routed = True
            wr = jnp.zeros((D_MODEL, LANES), F32).at[:, :N_EXPERTS].set(w_router[i])
            wg, wu, wd = (w[i].astype(BF16) for w in (w_gate_moe, w_up_moe, w_down_moe))

        proj = _inproj(hp, g1, wp)
        qt, kf, kb, vt, qit, kif, kib, wt = _prep(proj, cos_p, sin_p, qg, kg, True)
        attn = _dsa_prompt(qt, qit, wt, kb, vt, kib, nb, seq)
        pre_a, conv_st = _conva_prompt(proj, conv_w[l], nb, seq)
        y_ssd, h_last = _ssd_prompt(proj, cw, cbias, dtb, alog, dsk, nb, seq)
        hp = _merge(hp, pre_a, attn, y_ssd, proj, sn, b_gate[l], wpc, wpa, wps, wo)
        if routed and hp.shape[0] % MOE_TILE == 0:
            hp = _moe_grouped(hp, g2, wr, wg, wu, wd)
        else:
            hp = _ffn(hp, g2, wr, wg, wu, wd, routed)
        proj3 = proj.reshape(nb, seq, N_PROJ)
        outs_p[0].append(kf.reshape(nb, seq, N_KV_HEADS, HEAD_DIM))
        outs_p[1].append(proj3[:, :, V_OFF:V_OFF + LANES].reshape(nb, seq, N_KV_HEADS, HEAD_DIM))
        outs_p[2].append(kif[:, :IDX_DIM].reshape(nb, seq, IDX_DIM))
        outs_p[3].append(conv_st)
        outs_p[4].append(proj3[:, seq - 3:, XS_OFF:XS_OFF + SSM_CONV_DIM])
        outs_p[5].append(h_last.reshape(nb, SSM_HEADS, SSM_HEAD_DIM, SSM_STATE))

        proj = _inproj(hs, g1, wp)
        qp, kf, kb, vb, qip, kif, kib, wi = _prep(proj, cos_s, sin_s, qg, kg, False)
        wrows = jnp.broadcast_to(wi[:, :N_IDX_HEADS, None], (db, N_IDX_HEADS, PAGE_SIZE))
        scores = _decscore(page_table, qip.astype(F32).reshape(db, N_IDX_HEADS, LANES), wrows, pool_kidxt, l)
        sel, self_sel = _decselect(scores, qip, kib, wi)
        o = _decattn(page_table, qp.astype(F32).reshape(db, N_HEADS, LANES), sel, pool_kt, pool_vt,
                     kb.astype(F32).reshape(db, 1, LANES), vb.astype(F32).reshape(db, 1, LANES),
                     self_sel.reshape(db, 1, LANES), l)
        attn = _head_halves(o)
        pre_a, u, xs, bcv, dt, ea = _decmix(proj, state_conv[l], state_ssm_conv[l], conv_w[l], cw, cbias, dtb, alog)
        hpg = SSM_HEADS // SSM_GROUPS
        xs4 = xs.reshape(db, SSM_HEADS, SSM_HEAD_DIM, 1)
        bh = jnp.repeat(bcv[:, :SSM_GROUPS * SSM_STATE].reshape(db, SSM_GROUPS, 1, SSM_STATE), hpg, axis=1)
        ch = jnp.repeat(bcv[:, SSM_GROUPS * SSM_STATE:].reshape(db, SSM_GROUPS, 1, SSM_STATE), hpg, axis=1)
        y4, h_new = _decssd(state_ssm[l], dt[:, :SSM_HEADS, None, None], xs4, bh, ch, ea[:, :SSM_HEADS, None, None],
                            d_skip[l].reshape(SSM_HEADS, 1, 1))
        hs = _merge(hs, pre_a, attn, y4.reshape(db, SSM_INNER), proj, sn, b_gate[l], wpc, wpa, wps, wo)
        hs = _ffn(hs, g2, wr, wg, wu, wd, routed)
        outs_s[0].append(kf.reshape(db, 1, N_KV_HEADS, HEAD_DIM))
        outs_s[1].append(proj[:, V_OFF:V_OFF + LANES].reshape(db, 1, N_KV_HEADS, HEAD_DIM))
        outs_s[2].append(kif[:, :IDX_DIM].reshape(db, 1, IDX_DIM))
        outs_s[3].append(jnp.stack([state_conv[l][:, 1], u], axis=1))
        outs_s[4].append(jnp.concatenate(
            [state_ssm_conv[l][:, 1:], proj[:, None, XS_OFF:XS_OFF + SSM_CONV_DIM]], axis=1))
        outs_s[5].append(h_new)

    return (hp.reshape(nb, seq, D_MODEL), hs.reshape(db, 1, D_MODEL),
            *(jnp.stack(o) for o in outs_p), *(jnp.stack(o) for o in outs_s))
```
